```python
import jax, jax.numpy as jnp
from jax import lax
import numpy as np

D_MODEL = 2048
BATCH = 8
SEQ = 8192
DEPTH = 1

CTX_LEN = 256
GRID_W = 64
HEAD_DIM = 64
N_Q_HEADS = D_MODEL // (2 * HEAD_DIM)
N_KV_HEADS = N_Q_HEADS // 4
ATTN_WIDTH = N_Q_HEADS * HEAD_DIM
KV_WIDTH = N_KV_HEADS * HEAD_DIM
POOL_WINDOWS = (2, 4, 8, 16)
POOL_GROUPS = len(POOL_WINDOWS)
POOL_WIDTH = D_MODEL - ATTN_WIDTH
POOL_GROUP_DIM = POOL_WIDTH // POOL_GROUPS
MIX_WIDTH = ATTN_WIDTH + POOL_WIDTH
IN_WIDTH = ATTN_WIDTH + 2 * KV_WIDTH + POOL_WIDTH
D_FF = 4 * D_MODEL
WINDOW = 128
BLOCK = 128
ROPE_BASE = 10000.0
N_MOD = 6
EPS = 1e-6
NEG_INF = -1e30

kernel_name = "hymba_style_window_gqa_pool_diffusion_block"


def rms_norm(x, w):
    xf = x.astype(jnp.float32)
    y = xf * lax.rsqrt(jnp.mean(xf * xf, axis=-1, keepdims=True) + EPS)
    return (y * w.astype(jnp.float32)).astype(x.dtype)


def ada_modulation(cond, w, b):
    m = jax.nn.silu(cond) @ w + b
    return jnp.split(m[..., None, :], N_MOD, axis=-1)


def modulate(h, shift, scale):
    return h * (1.0 + scale) + shift


def split_projection(p):
    B, L, _ = p.shape
    q, k, v, u = jnp.split(p, [ATTN_WIDTH, ATTN_WIDTH + KV_WIDTH, ATTN_WIDTH + 2 * KV_WIDTH], axis=-1)
    return (q.reshape(B, L, N_Q_HEADS, HEAD_DIM), k.reshape(B, L, N_KV_HEADS, HEAD_DIM),
            v.reshape(B, L, N_KV_HEADS, HEAD_DIM), u)


def axial_positions(seq_len):
    rows = seq_len // GRID_W
    row = jnp.broadcast_to(jnp.arange(rows)[:, None], (rows, GRID_W)).reshape(-1)
    col = jnp.broadcast_to(jnp.arange(GRID_W)[None, :], (rows, GRID_W)).reshape(-1)
    return row, col


def rope_2d(x, row, col):
    half = HEAD_DIM // 2
    inv_freq = ROPE_BASE ** (-jnp.arange(0, half, 2, dtype=jnp.float32) / half)

    def rot(xa, pos):
        ang = pos.astype(jnp.float32)[:, None] * inv_freq[None, :]
        cos = jnp.cos(ang)[None, :, None, :]
        sin = jnp.sin(ang)[None, :, None, :]
        x1, x2 = jnp.split(xa, 2, axis=-1)
        return jnp.concatenate([x1 * cos - x2 * sin, x1 * sin + x2 * cos], axis=-1)

    xf = x.astype(jnp.float32)
    out = jnp.concatenate([rot(xf[..., :half], row), rot(xf[..., half:], col)], axis=-1)
    return out.astype(x.dtype)


def latent_window_attention(q, k, v, k_ctx, v_ctx, sink):
    B, L, H, D = q.shape
    G = H // N_KV_HEADS
    nb = L // BLOCK
    scale = HEAD_DIM ** -0.5
    qb = q.reshape(B, nb, BLOCK, N_KV_HEADS, G, D)
    pad = ((0, 0), (BLOCK, BLOCK), (0, 0), (0, 0))
    kp = jnp.pad(k, pad).reshape(B, nb + 2, BLOCK, N_KV_HEADS, D)
    vp = jnp.pad(v, pad).reshape(B, nb + 2, BLOCK, N_KV_HEADS, D)
    kb = jnp.concatenate([kp[:, :-2], kp[:, 1:-1], kp[:, 2:]], axis=2)
    vb = jnp.concatenate([vp[:, :-2], vp[:, 1:-1], vp[:, 2:]], axis=2)
    s_win = jnp.einsum('bnqhgd,bnkhd->bhgnqk', qb, kb).astype(jnp.float32) * scale
    blk = jnp.arange(nb)[:, None, None] * BLOCK
    qpos = blk + jnp.arange(BLOCK)[None, :, None]
    kpos = blk - BLOCK + jnp.arange(3 * BLOCK)[None, None, :]
    valid = (jnp.abs(qpos - kpos) <= WINDOW) & (kpos >= 0) & (kpos < L)
    s_win = jnp.where(valid, s_win, NEG_INF)
    s_ctx = jnp.einsum('bnqhgd,bchd->bhgnqc', qb, k_ctx).astype(jnp.float32) * scale
    s_sink = jnp.broadcast_to(sink.astype(jnp.float32).reshape(N_KV_HEADS, G)[None, :, :, None, None, None],
                              s_win.shape[:-1] + (1,))
    p = jax.nn.softmax(jnp.concatenate([s_win, s_ctx, s_sink], axis=-1), axis=-1)
    n_win = 3 * BLOCK
    n_ctx = k_ctx.shape[1]
    p_win = p[..., :n_win].astype(v.dtype)
    p_ctx = p[..., n_win:n_win + n_ctx].astype(v.dtype)
    out = (jnp.einsum('bhgnqk,bnkhd->bnqhgd', p_win, vb)
           + jnp.einsum('bhgnqc,bchd->bnqhgd', p_ctx, v_ctx))
    return out.reshape(B, L, H * D)


def context_attention(q, k, v, sink):
    B, C, H, D = q.shape
    G = H // N_KV_HEADS
    qg = q.reshape(B, C, N_KV_HEADS, G, D)
    s = jnp.einsum('bqhgd,bkhd->bhgqk', qg, k).astype(jnp.float32) * (HEAD_DIM ** -0.5)
    s_sink = jnp.broadcast_to(sink.astype(jnp.float32).reshape(N_KV_HEADS, G)[None, :, :, None, None],
                              s.shape[:-1] + (1,))
    p = jax.nn.softmax(jnp.concatenate([s, s_sink], axis=-1), axis=-1)[..., :C].astype(v.dtype)
    out = jnp.einsum('bhgqk,bkhd->bqhgd', p, v)
    return out.reshape(B, C, H * D)


def multiscale_pool(u, pool_w, pool_scale):
    B, L, _ = u.shape
    uf = u.astype(jnp.float32)
    csum = jnp.pad(jnp.cumsum(uf, axis=1), ((0, 0), (1, 0), (0, 0)))
    t = jnp.arange(L)
    outs = []
    for g, w in enumerate(POOL_WINDOWS):
        lo = jnp.clip(t - w // 2, 0, L)
        hi = jnp.clip(t - w // 2 + w, 0, L)
        cs = csum[..., g * POOL_GROUP_DIM:(g + 1) * POOL_GROUP_DIM]
        mean = (cs[:, hi] - cs[:, lo]) / (hi - lo).astype(jnp.float32)[None, :, None]
        outs.append(mean - uf[..., g * POOL_GROUP_DIM:(g + 1) * POOL_GROUP_DIM])
    pooled = jnp.stack(outs, axis=2)
    mixed = jnp.einsum('blgc,gcd->blgd', pooled, pool_w.astype(jnp.float32)).reshape(B, L, POOL_WIDTH)
    return (mixed * pool_scale.astype(jnp.float32)).astype(u.dtype)


def squared_relu_mlp(h, w_up, w_down):
    return jnp.square(jax.nn.relu(h @ w_up)) @ w_down


def _fwd_setup_inputs(seed: int = 0) -> dict:
    key = jax.random.key(seed)
    ks = jax.random.split(key, 18)
    f32 = jnp.float32
    nrm = lambda k, shape, s: jax.random.normal(k, shape, f32) * s
    return {
        "x": nrm(ks[0], (BATCH, SEQ, D_MODEL), 1.0),
        "c": nrm(ks[1], (BATCH, D_MODEL), 1.0),
        "ctx": nrm(ks[2], (BATCH, CTX_LEN, D_MODEL), 1.0),
        "c_ctx": nrm(ks[3], (D_MODEL,), 1.0),
        "norm_attn_w": 1.0 + nrm(ks[4], (DEPTH, D_MODEL), 0.02),
        "norm_mlp_w": 1.0 + nrm(ks[5], (DEPTH, D_MODEL), 0.02),
        "w_ada": nrm(ks[6], (DEPTH, D_MODEL, N_MOD * D_MODEL), 0.5 * D_MODEL ** -0.5),
        "b_ada": nrm(ks[7], (DEPTH, N_MOD * D_MODEL), 0.02),
        "w_in": nrm(ks[8], (DEPTH, D_MODEL, IN_WIDTH), D_MODEL ** -0.5),
        "attn_sink": nrm(ks[9], (DEPTH, N_Q_HEADS), 1.0),
        "pool_w": nrm(ks[10], (DEPTH, POOL_GROUPS, POOL_GROUP_DIM, POOL_GROUP_DIM), POOL_GROUP_DIM ** -0.5),
        "pool_scale": 1.0 + nrm(ks[11], (DEPTH, POOL_WIDTH), 0.1),
        "w_out": nrm(ks[12], (DEPTH, MIX_WIDTH, D_MODEL), MIX_WIDTH ** -0.5),
        "w_mlp_up": nrm(ks[13], (DEPTH, D_MODEL, D_FF), D_MODEL ** -0.5),
        "w_mlp_down": nrm(ks[14], (DEPTH, D_FF, D_MODEL), D_FF ** -0.5),
        "final_norm_w": 1.0 + nrm(ks[15], (D_MODEL,), 0.02),
    }


def _fwd_reference(x, c, ctx, c_ctx, norm_attn_w, norm_mlp_w, w_ada, b_ada, w_in, attn_sink,
              pool_w, pool_scale, w_out, w_mlp_up, w_mlp_down, final_norm_w):
    seq_len = x.shape[1]
    row, col = axial_positions(seq_len)
    for layer in range(DEPTH):
        sh_a, sc_a, g_a, sh_m, sc_m, g_m = ada_modulation(c, w_ada[layer], b_ada[layer])
        csh_a, csc_a, cg_a, csh_m, csc_m, cg_m = ada_modulation(c_ctx, w_ada[layer], b_ada[layer])

        h = modulate(rms_norm(x, norm_attn_w[layer]), sh_a, sc_a)
        hc = modulate(rms_norm(ctx, norm_attn_w[layer]), csh_a, csc_a)
        q, k, v, u = split_projection(h @ w_in[layer])
        qc, kc, vc, uc = split_projection(hc @ w_in[layer])
        q = rope_2d(q, row, col)
        k = rope_2d(k, row, col)
        attn = latent_window_attention(q, k, v, kc, vc, attn_sink[layer])
        pool = multiscale_pool(u, pool_w[layer], pool_scale[layer])
        x = x + g_a * (jnp.concatenate([attn, pool], axis=-1) @ w_out[layer])

        hm = modulate(rms_norm(x, norm_mlp_w[layer]), sh_m, sc_m)
        x = x + g_m * squared_relu_mlp(hm, w_mlp_up[layer], w_mlp_down[layer])

        if layer < DEPTH - 1:
            attn_c = context_attention(qc, kc, vc, attn_sink[layer])
            pool_c = multiscale_pool(uc, pool_w[layer], pool_scale[layer])
            ctx = ctx + cg_a * (jnp.concatenate([attn_c, pool_c], axis=-1) @ w_out[layer])
            hcm = modulate(rms_norm(ctx, norm_mlp_w[layer]), csh_m, csc_m)
            ctx = ctx + cg_m * squared_relu_mlp(hcm, w_mlp_up[layer], w_mlp_down[layer])
    return rms_norm(x, final_norm_w)


import jax as _jax
import jax.numpy as _jnp

TWIN_FORMAT = 'train_step'
FWD_PARAMS = ['x', 'c', 'ctx', 'c_ctx', 'norm_attn_w', 'norm_mlp_w', 'w_ada', 'b_ada', 'w_in', 'attn_sink', 'pool_w', 'pool_scale', 'w_out', 'w_mlp_up', 'w_mlp_down', 'final_norm_w']
TWIN_WEIGHTS = ['c_ctx', 'norm_attn_w', 'norm_mlp_w', 'w_ada', 'b_ada', 'w_in', 'attn_sink', 'pool_w', 'pool_scale', 'w_out', 'w_mlp_up', 'w_mlp_down', 'final_norm_w']
TWIN_DIFF_INPUT = 'x'
TWIN_INPUTS = ['x', 'c', 'ctx', 'c_ctx', 'norm_attn_w', 'norm_mlp_w', 'w_ada', 'b_ada', 'w_in', 'attn_sink', 'pool_w', 'pool_scale', 'w_out', 'w_mlp_up', 'w_mlp_down', 'final_norm_w', 'loss_target', 'm_c_ctx', 'm_norm_attn_w', 'm_norm_mlp_w', 'm_w_ada', 'm_b_ada', 'm_w_in', 'm_attn_sink', 'm_pool_w', 'm_pool_scale', 'm_w_out', 'm_w_mlp_up', 'm_w_mlp_down', 'm_final_norm_w', 'v_c_ctx', 'v_norm_attn_w', 'v_norm_mlp_w', 'v_w_ada', 'v_b_ada', 'v_w_in', 'v_attn_sink', 'v_pool_w', 'v_pool_scale', 'v_w_out', 'v_w_mlp_up', 'v_w_mlp_down', 'v_final_norm_w']
TWIN_OUTPUTS = ['loss', 'grad_x', 'grad_c_ctx', 'grad_norm_attn_w', 'grad_norm_mlp_w', 'grad_w_ada', 'grad_b_ada', 'grad_w_in', 'grad_attn_sink', 'grad_pool_w', 'grad_pool_scale', 'grad_w_out', 'grad_w_mlp_up', 'grad_w_mlp_down', 'grad_final_norm_w', 'delta_c_ctx', 'delta_norm_attn_w', 'delta_norm_mlp_w', 'delta_w_ada', 'delta_b_ada', 'delta_w_in', 'delta_attn_sink', 'delta_pool_w', 'delta_pool_scale', 'delta_w_out', 'delta_w_mlp_up', 'delta_w_mlp_down', 'delta_final_norm_w', 'new_m_c_ctx', 'new_m_norm_attn_w', 'new_m_norm_mlp_w', 'new_m_w_ada', 'new_m_b_ada', 'new_m_w_in', 'new_m_attn_sink', 'new_m_pool_w', 'new_m_pool_scale', 'new_m_w_out', 'new_m_w_mlp_up', 'new_m_w_mlp_down', 'new_m_final_norm_w', 'new_v_c_ctx', 'new_v_norm_attn_w', 'new_v_norm_mlp_w', 'new_v_w_ada', 'new_v_b_ada', 'new_v_w_in', 'new_v_attn_sink', 'new_v_pool_w', 'new_v_pool_scale', 'new_v_w_out', 'new_v_w_mlp_up', 'new_v_w_mlp_down', 'new_v_final_norm_w']
TWIN_LEAF_KINDS = {'loss': 'loss', 'grad_x': 'grad_x', 'grad_c_ctx': 'grad_w', 'grad_norm_attn_w': 'grad_w', 'grad_norm_mlp_w': 'grad_w', 'grad_w_ada': 'grad_w', 'grad_b_ada': 'grad_w', 'grad_w_in': 'grad_w', 'grad_attn_sink': 'grad_w', 'grad_pool_w': 'grad_w', 'grad_pool_scale': 'grad_w', 'grad_w_out': 'grad_w', 'grad_w_mlp_up': 'grad_w', 'grad_w_mlp_down': 'grad_w', 'grad_final_norm_w': 'grad_w', 'delta_c_ctx': 'delta_w', 'delta_norm_attn_w': 'delta_w', 'delta_norm_mlp_w': 'delta_w', 'delta_w_ada': 'delta_w', 'delta_b_ada': 'delta_w', 'delta_w_in': 'delta_w', 'delta_attn_sink': 'delta_w', 'delta_pool_w': 'delta_w', 'delta_pool_scale': 'delta_w', 'delta_w_out': 'delta_w', 'delta_w_mlp_up': 'delta_w', 'delta_w_mlp_down': 'delta_w', 'delta_final_norm_w': 'delta_w', 'new_m_c_ctx': 'new_m', 'new_m_norm_attn_w': 'new_m', 'new_m_norm_mlp_w': 'new_m', 'new_m_w_ada': 'new_m', 'new_m_b_ada': 'new_m', 'new_m_w_in': 'new_m', 'new_m_attn_sink': 'new_m', 'new_m_pool_w': 'new_m', 'new_m_pool_scale': 'new_m', 'new_m_w_out': 'new_m', 'new_m_w_mlp_up': 'new_m', 'new_m_w_mlp_down': 'new_m', 'new_m_final_norm_w': 'new_m', 'new_v_c_ctx': 'new_v', 'new_v_norm_attn_w': 'new_v', 'new_v_norm_mlp_w': 'new_v', 'new_v_w_ada': 'new_v', 'new_v_b_ada': 'new_v', 'new_v_w_in': 'new_v', 'new_v_attn_sink': 'new_v', 'new_v_pool_w': 'new_v', 'new_v_pool_scale': 'new_v', 'new_v_w_out': 'new_v', 'new_v_w_mlp_up': 'new_v', 'new_v_w_mlp_down': 'new_v', 'new_v_final_norm_w': 'new_v'}


def _forward(args):
    return _fwd_reference(*[args[k] for k in FWD_PARAMS])


def _output_shape():
    def fwd():
        inp = _fwd_setup_inputs(0)
        return _fwd_reference(*[inp[k] for k in FWD_PARAMS])
    out = _jax.eval_shape(fwd)
    return out.shape, out.dtype

N_MICROBATCH = 1
ADAM_LR = 0.001
ADAM_B1 = 0.9
ADAM_B2 = 0.999
ADAM_EPS = 1e-08
ADAM_WD = 0.01
ADAM_STEP = 10
PER_EXAMPLE_BATCH_AXIS = {'x': 0, 'c': 0, 'ctx': 0, 'loss_target': 0}
SHARED_INPUTS = []
_WEIGHT_DTYPES = {'c_ctx': _jnp.float32, 'norm_attn_w': _jnp.float32, 'norm_mlp_w': _jnp.float32, 'w_ada': _jnp.float32, 'b_ada': _jnp.float32, 'w_in': _jnp.float32, 'attn_sink': _jnp.float32, 'pool_w': _jnp.float32, 'pool_scale': _jnp.float32, 'w_out': _jnp.float32, 'w_mlp_up': _jnp.float32, 'w_mlp_down': _jnp.float32, 'final_norm_w': _jnp.float32}
MOMENT_SCALE = {'c_ctx': 5.100706e-03, 'norm_attn_w': 2.466672e-02, 'norm_mlp_w': 5.449185e-02, 'w_ada': 5.973462e-02, 'b_ada': 1.066592e-01, 'w_in': 2.305925e-02, 'attn_sink': 2.999971e-04, 'pool_w': 3.481309e-02, 'pool_scale': 3.584966e-02, 'w_out': 2.557691e-02, 'w_mlp_up': 2.868947e-02, 'w_mlp_down': 5.327061e-02, 'final_norm_w': 3.205820e+01}


def _to_microbatches(a, axis):
    t = _jnp.moveaxis(a, axis, 0)
    t = t.reshape((N_MICROBATCH, t.shape[0] // N_MICROBATCH) + t.shape[1:])
    return _jnp.moveaxis(t, 1, axis + 1)


def setup_inputs(seed: int = 0) -> dict:
    inp = _fwd_setup_inputs(seed)
    key = _jax.random.fold_in(_jax.random.key(seed), 7919)
    shape, _ = _output_shape()
    out = dict(inp)
    out["loss_target"] = _jax.random.normal(_jax.random.fold_in(key, 0), shape, _jnp.float32)
    for i, name in enumerate(TWIN_WEIGHTS):
        w = inp[name].astype(_jnp.float32)
        if MOMENT_SCALE is None:
            s = _jnp.sqrt(_jnp.mean(_jnp.square(w)) + 1e-30)
        else:
            s = MOMENT_SCALE[name]
        km, kv = _jax.random.split(_jax.random.fold_in(key, i + 1))
        out[name] = w
        out["m_" + name] = s * _jax.random.normal(km, w.shape, _jnp.float32)
        out["v_" + name] = (s * s) * _jax.random.uniform(kv, w.shape, _jnp.float32, 0.5, 1.5)
    if N_MICROBATCH > 1:
        for name, axis in PER_EXAMPLE_BATCH_AXIS.items():
            out[name] = _to_microbatches(out[name], axis)
    return {'x': out['x'], 'c': out['c'], 'ctx': out['ctx'], 'c_ctx': out['c_ctx'], 'norm_attn_w': out['norm_attn_w'], 'norm_mlp_w': out['norm_mlp_w'], 'w_ada': out['w_ada'], 'b_ada': out['b_ada'], 'w_in': out['w_in'], 'attn_sink': out['attn_sink'], 'pool_w': out['pool_w'], 'pool_scale': out['pool_scale'], 'w_out': out['w_out'], 'w_mlp_up': out['w_mlp_up'], 'w_mlp_down': out['w_mlp_down'], 'final_norm_w': out['final_norm_w'], 'loss_target': out['loss_target'], 'm_c_ctx': out['m_c_ctx'], 'm_norm_attn_w': out['m_norm_attn_w'], 'm_norm_mlp_w': out['m_norm_mlp_w'], 'm_w_ada': out['m_w_ada'], 'm_b_ada': out['m_b_ada'], 'm_w_in': out['m_w_in'], 'm_attn_sink': out['m_attn_sink'], 'm_pool_w': out['m_pool_w'], 'm_pool_scale': out['m_pool_scale'], 'm_w_out': out['m_w_out'], 'm_w_mlp_up': out['m_w_mlp_up'], 'm_w_mlp_down': out['m_w_mlp_down'], 'm_final_norm_w': out['m_final_norm_w'], 'v_c_ctx': out['v_c_ctx'], 'v_norm_attn_w': out['v_norm_attn_w'], 'v_norm_mlp_w': out['v_norm_mlp_w'], 'v_w_ada': out['v_w_ada'], 'v_b_ada': out['v_b_ada'], 'v_w_in': out['v_w_in'], 'v_attn_sink': out['v_attn_sink'], 'v_pool_w': out['v_pool_w'], 'v_pool_scale': out['v_pool_scale'], 'v_w_out': out['v_w_out'], 'v_w_mlp_up': out['v_w_mlp_up'], 'v_w_mlp_down': out['v_w_mlp_down'], 'v_final_norm_w': out['v_final_norm_w']}


def _loss(weights, diff, rest, loss_target):
    with _jax.named_scope("forward"):
        args = {**rest, TWIN_DIFF_INPUT: diff, **{k: w.astype(_WEIGHT_DTYPES[k]) for k, w in weights.items()}}
        y = _forward(args)
    with _jax.named_scope("loss_head"):
        err = _jnp.square(y.astype(_jnp.float32) - loss_target)
        return 0.5 * _jnp.sum(_jnp.mean(err, axis=-1)) if err.ndim else 0.5 * err


def _adamw(w, g, m, v):
    m = ADAM_B1 * m + (1.0 - ADAM_B1) * g
    v = ADAM_B2 * v + (1.0 - ADAM_B2) * _jnp.square(g)
    m_hat = m / (1.0 - ADAM_B1 ** ADAM_STEP)
    v_hat = v / (1.0 - ADAM_B2 ** ADAM_STEP)
    delta = -ADAM_LR * (m_hat / (_jnp.sqrt(v_hat) + ADAM_EPS) + ADAM_WD * w)
    return delta, m, v


def reference(x, c, ctx, c_ctx, norm_attn_w, norm_mlp_w, w_ada, b_ada, w_in, attn_sink, pool_w, pool_scale, w_out, w_mlp_up, w_mlp_down, final_norm_w, loss_target, m_c_ctx, m_norm_attn_w, m_norm_mlp_w, m_w_ada, m_b_ada, m_w_in, m_attn_sink, m_pool_w, m_pool_scale, m_w_out, m_w_mlp_up, m_w_mlp_down, m_final_norm_w, v_c_ctx, v_norm_attn_w, v_norm_mlp_w, v_w_ada, v_b_ada, v_w_in, v_attn_sink, v_pool_w, v_pool_scale, v_w_out, v_w_mlp_up, v_w_mlp_down, v_final_norm_w):
    given = dict(x=x, c=c, ctx=ctx, c_ctx=c_ctx, norm_attn_w=norm_attn_w, norm_mlp_w=norm_mlp_w, w_ada=w_ada, b_ada=b_ada, w_in=w_in, attn_sink=attn_sink, pool_w=pool_w, pool_scale=pool_scale, w_out=w_out, w_mlp_up=w_mlp_up, w_mlp_down=w_mlp_down, final_norm_w=final_norm_w, loss_target=loss_target, m_c_ctx=m_c_ctx, m_norm_attn_w=m_norm_attn_w, m_norm_mlp_w=m_norm_mlp_w, m_w_ada=m_w_ada, m_b_ada=m_b_ada, m_w_in=m_w_in, m_attn_sink=m_attn_sink, m_pool_w=m_pool_w, m_pool_scale=m_pool_scale, m_w_out=m_w_out, m_w_mlp_up=m_w_mlp_up, m_w_mlp_down=m_w_mlp_down, m_final_norm_w=m_final_norm_w, v_c_ctx=v_c_ctx, v_norm_attn_w=v_norm_attn_w, v_norm_mlp_w=v_norm_mlp_w, v_w_ada=v_w_ada, v_b_ada=v_b_ada, v_w_in=v_w_in, v_attn_sink=v_attn_sink, v_pool_w=v_pool_w, v_pool_scale=v_pool_scale, v_w_out=v_w_out, v_w_mlp_up=v_w_mlp_up, v_w_mlp_down=v_w_mlp_down, v_final_norm_w=v_final_norm_w)
    weights = {n: given[n] for n in TWIN_WEIGHTS}
    shared = {n: given[n] for n in SHARED_INPUTS}
    per_example = {n: given[n] for n in ['x', 'c', 'ctx']}
    grad_fn = _jax.value_and_grad(_loss, argnums=(0, 1))

    def one_microbatch(ex, loss_target):
        ex = dict(ex)
        diff = ex.pop(TWIN_DIFF_INPUT)
        return grad_fn(weights, diff, {**shared, **ex}, loss_target)

    if N_MICROBATCH == 1:
        loss, (grad_w, grad_x) = one_microbatch(per_example, given["loss_target"])
    else:
        def body(carry, xs):
            loss_sum, grad_sum = carry
            l_k, (gw_k, gx_k) = one_microbatch(xs[0], xs[1])
            with _jax.named_scope("update"):
                return (loss_sum + l_k, _jax.tree.map(_jnp.add, grad_sum, gw_k)), gx_k

        init = (_jnp.zeros((), _jnp.float32), _jax.tree.map(_jnp.zeros_like, weights))
        (loss, grad_w), grad_x = _jax.lax.scan(body, init, (per_example, given["loss_target"]))
    with _jax.named_scope("update"):
        delta_w, new_m, new_v = {}, {}, {}
        for n in TWIN_WEIGHTS:
            delta_w[n], new_m[n], new_v[n] = _adamw(weights[n], grad_w[n], given["m_" + n], given["v_" + n])
    return (loss, grad_x, *[grad_w[n] for n in TWIN_WEIGHTS], *[delta_w[n] for n in TWIN_WEIGHTS],
            *[new_m[n] for n in TWIN_WEIGHTS], *[new_v[n] for n in TWIN_WEIGHTS])
```

```python
import functools

import jax
import jax.numpy as jnp
from jax import lax
from jax.experimental import pallas as pl
from jax.experimental.pallas import tpu as pltpu

F32 = jnp.float32
BF16 = jnp.bfloat16
EPS = 1e-6
NEG_INF = -1e30
HEAD_DIM = 64
N_Q_HEADS = 16
N_KV_HEADS = 4
GROUP = N_Q_HEADS // N_KV_HEADS
ATTN_WIDTH = N_Q_HEADS * HEAD_DIM
KV_WIDTH = N_KV_HEADS * HEAD_DIM
POOL_WINDOWS = (2, 4, 8, 16)
POOL_GROUP_DIM = 256
POOL_WIDTH = len(POOL_WINDOWS) * POOL_GROUP_DIM
BLOCK = 128
GRID_W = 64
ROPE_BASE = 10000.0
SCALE = HEAD_DIM ** -0.5
HALO = 16
ROWS = 16
ADAM_LR, ADAM_B1, ADAM_B2, ADAM_EPS, ADAM_WD, ADAM_STEP = 0.001, 0.9, 0.999, 1e-08, 0.01, 10
MESH = pl.DeviceIdType.MESH
MIB = 1024 * 1024
ANY = pl.BlockSpec(memory_space=pl.ANY)


def _cp(n_axes, vmem_mib=48):
    return pltpu.CompilerParams(dimension_semantics=("arbitrary",) * n_axes, vmem_limit_bytes=vmem_mib * MIB)


def _row_loop(rows, fn):
    def body(r, carry):
        fn(pl.ds(pl.multiple_of(r * ROWS, ROWS), ROWS))
        return carry

    lax.fori_loop(0, rows // ROWS, body, 0)


def _fold8(v):
    s = v[0:8]
    for t in range(1, v.shape[0] // 8):
        s = s + v[8 * t:8 * t + 8]
    return s


def _dot(a, b):
    return jnp.dot(a, b, preferred_element_type=F32)


def _dot_nt(a, b):
    return lax.dot_general(a, b, (((1,), (1,)), ((), ())), preferred_element_type=F32)


def _dot_tn(a, b):
    return lax.dot_general(a, b, (((0,), (0,)), ((), ())), preferred_element_type=F32)


def _pick(n, *cands):
    for t in cands:
        if n % t == 0:
            return t
    return n


def _flip(pos, mask):
    return tuple((1 - v) if (mask >> (2 - i)) & 1 else v for i, v in enumerate(pos))


def _exchange(name, ins, out_shapes, remote, local=(), aliases=None):
    n_io = len(ins) + len(out_shapes)

    def body(*refs):
        io = refs[:n_io]
        send_sems, recv_sems, local_sems = refs[n_io:]
        me = (lax.axis_index("x"), lax.axis_index("y"), lax.axis_index("c"))

        def copy(i, sender):
            mask, src_fn, dst_fn = remote[i]
            return pltpu.make_async_remote_copy(
                src_ref=src_fn(io, sender), dst_ref=dst_fn(io, sender), send_sem=send_sems.at[i],
                recv_sem=recv_sems.at[i], device_id=_flip(sender, mask), device_id_type=MESH)

        own = [pltpu.make_async_copy(s(io, me), d(io, me), local_sems.at[i]) for i, (s, d) in enumerate(local)]
        for cp in own:
            cp.start()
        sends = [copy(i, me) for i in range(len(remote))]
        for cp in sends:
            cp.start()
        for i in range(len(remote)):
            copy(i, _flip(me, remote[i][0])).wait_recv()
        for cp in sends:
            cp.wait_send()
        for cp in own:
            cp.wait()

    return pl.pallas_call(
        body, name=name, out_shape=tuple(out_shapes),
        in_specs=[ANY] * len(ins), out_specs=tuple([ANY] * len(out_shapes)),
        scratch_shapes=[pltpu.SemaphoreType.DMA((len(remote),)), pltpu.SemaphoreType.DMA((len(remote),)),
                        pltpu.SemaphoreType.DMA((max(len(local), 1),))],
        input_output_aliases=aliases or {},
    )(*ins)


def _dev_index(pos):
    return 4 * pos[0] + 2 * pos[1] + pos[2]


def _chip_index(pos):
    return 2 * pos[0] + pos[1]


def _allgather8(name, v):
    out = jax.ShapeDtypeStruct((8,) + v.shape, v.dtype)
    remote = [(mask, lambda io, pos: io[0], lambda io, pos: io[1].at[_dev_index(pos)]) for mask in range(1, 8)]
    local = [(lambda io, pos: io[0], lambda io, pos: io[1].at[_dev_index(pos)])]
    return _exchange(name, [v], [out], remote, local)[0]


class _Big:
    def __init__(self, kind, shard_shape):
        self.kind = kind
        self.shard_shape = tuple(shard_shape)
        if kind == "col":
            r, cs = shard_shape
            self.full_shape = (r, 4 * cs)
            self.piece_shape = (r // 2, cs)
            self.half_shape = (r // 2, 4 * cs)
        elif kind == "row":
            rs, c = shard_shape
            self.full_shape = (4, 2, rs // 2, c)
            self.piece_shape = (1, 1, rs // 2, c)
            self.half_shape = (4, 1, rs // 2, c)
        else:
            self.full_shape = (4, 256, 256)
            self.piece_shape = (2, 64, 256)
            self.half_shape = (2, 256, 256)

    def shard_as_pieces(self, a):
        return a.reshape((1, 2) + self.piece_shape[2:]) if self.kind == "row" else a

    def piece(self, ref, k, h):
        if self.kind == "col":
            r, cs = self.piece_shape
            return ref.at[pl.ds(h * r, r), pl.ds(k * cs, cs)]
        if self.kind == "row":
            return ref.at[pl.ds(k, 1), pl.ds(h, 1)]
        return ref.at[pl.ds(2 * h, 2), pl.ds(64 * k, 64)]

    def shard_in_full(self, ref, k):
        if self.kind == "col":
            return ref.at[:, pl.ds(k * self.shard_shape[1], self.shard_shape[1])]
        if self.kind == "row":
            return ref.at[pl.ds(k, 1)]
        return ref.at[:, pl.ds(64 * k, 64)]

    def half_of_shard(self, ref, h):
        if self.kind == "col":
            return ref.at[pl.ds(h * self.piece_shape[0], self.piece_shape[0])]
        if self.kind == "row":
            return ref.at[:, pl.ds(h, 1)]
        return ref.at[pl.ds(2 * h, 2)]

    def half_of_full(self, ref, h):
        if self.kind == "col":
            return ref.at[pl.ds(h * self.half_shape[0], self.half_shape[0])]
        if self.kind == "row":
            return ref.at[:, pl.ds(h, 1)]
        return ref.at[pl.ds(2 * h, 2)]

    def piece_of_half(self, ref, k):
        if self.kind == "col":
            return ref.at[:, pl.ds(k * self.piece_shape[1], self.piece_shape[1])]
        if self.kind == "row":
            return ref.at[pl.ds(k, 1)]
        return ref.at[:, pl.ds(64 * k, 64)]

    def take_half(self, a, h):
        if self.kind == "row":
            return lax.dynamic_slice_in_dim(a, h, 1, axis=1)
        n = self.half_shape[0]
        return lax.dynamic_slice_in_dim(a, h * n, n, axis=0)

    def take_piece_of_half(self, a, k):
        if self.kind == "col":
            return lax.dynamic_slice_in_dim(a, k * self.piece_shape[1], self.piece_shape[1], axis=1)
        if self.kind == "row":
            return lax.dynamic_slice_in_dim(a, k, 1, axis=0)
        return lax.dynamic_slice_in_dim(a, 64 * k, 64, axis=1)


CHIP_MASKS = (4, 2, 6)


def _gather_weights(bigs, shards):
    n = len(bigs)
    shards = [b.shard_as_pieces(s) if b.kind == "row" else s for b, s in zip(bigs, shards)]
    fulls = [jax.ShapeDtypeStruct(b.full_shape, BF16) for b in bigs]
    remote, local = [], []
    for a, b in enumerate(bigs):
        local.append((lambda io, pos, a=a: io[a],
                      lambda io, pos, a=a, b=b: b.shard_in_full(io[n + a], _chip_index(pos))))
        for mask in CHIP_MASKS:
            remote.append((mask,
                           lambda io, pos, a=a, b=b: b.half_of_shard(io[a], pos[2]),
                           lambda io, pos, a=a, b=b: b.piece(io[n + a], _chip_index(pos), pos[2])))
    got = _exchange("gather_weights_ici", shards, fulls, remote, local)
    remote = []
    for a, b in enumerate(bigs):
        for mask in CHIP_MASKS:
            def region(io, pos, a=a, b=b, mask=mask):
                return b.piece(io[n + a], _chip_index(_flip(pos, mask)), pos[2])
            remote.append((1, region, region))
    return _exchange("gather_weights_d2d", list(got), fulls, remote, aliases={a: a for a in range(n)})


def _ew(name, fn, ins, out_dtypes, rows_per_step=256):
    shape = ins[0].shape
    last = shape[-1]
    rows = 1
    for s in shape[:-1]:
        rows *= s
    ins2 = [a.reshape(rows, last) for a in ins]
    tr = _pick(rows, rows_per_step, 128, 64, 32, 16, 8)
    spec = pl.BlockSpec((tr, last), lambda i: (i, 0))

    def body(*refs):
        outs = fn(*[r[...] for r in refs[:len(ins)]])
        for o_ref, o in zip(refs[len(ins):], outs):
            o_ref[...] = o.astype(o_ref.dtype)

    outs = pl.pallas_call(
        body, name=name, grid=(rows // tr,), in_specs=[spec] * len(ins), out_specs=tuple([spec] * len(out_dtypes)),
        out_shape=tuple(jax.ShapeDtypeStruct((rows, last), d) for d in out_dtypes), compiler_params=_cp(1),
    )(*ins2)
    return [o.reshape(shape) for o in outs]


def _reduce_grads(bigs, grads, pos):
    n = len(bigs)
    c = pos[2]
    k_me = _chip_index(pos)
    halves = [jax.ShapeDtypeStruct(b.half_shape, BF16) for b in bigs]
    remote = [(1, lambda io, p, a=a, b=b: b.half_of_full(io[a], 1 - p[2]), lambda io, p, a=a: io[n + a])
              for a, b in enumerate(bigs)]
    from_sibling = _exchange("reduce_d2d", grads, halves, remote)
    chip_sum = [_ew(f"reduce_chip_sum_{a}", lambda u, v: (u.astype(F32) + v.astype(F32),),
                    [b.take_half(g, c), r], [BF16])[0] for a, (b, g, r) in enumerate(zip(bigs, grads, from_sibling))]
    thirds = [jax.ShapeDtypeStruct((3,) + b.piece_shape, BF16) for b in bigs]
    remote = []
    for a, b in enumerate(bigs):
        for j, mask in enumerate(CHIP_MASKS):
            remote.append((mask,
                           lambda io, p, a=a, b=b, mask=mask: b.piece_of_half(io[a], _chip_index(_flip(p, mask))),
                           lambda io, p, a=a, j=j: io[n + a].at[j]))
    from_chips = _exchange("reduce_ici", chip_sum, thirds, remote)
    pieces = [_ew(f"reduce_sum_{a}", lambda u, r0, r1, r2: (u.astype(F32) + r0.astype(F32) + r1.astype(F32) + r2.astype(F32),),
                  [b.take_piece_of_half(s, k_me), r[0], r[1], r[2]], [F32])[0]
              for a, (b, s, r) in enumerate(zip(bigs, chip_sum, from_chips))]
    shard_like = [jax.ShapeDtypeStruct(b.shard_as_pieces(jnp.zeros(b.shard_shape, F32)).shape
                                       if b.kind == "row" else b.shard_shape, F32) for b in bigs]
    local = [(lambda io, p, a=a: io[a], lambda io, p, a=a, b=b: b.half_of_shard(io[n + a], p[2]))
             for a, b in enumerate(bigs)]
    remote = [(1, lambda io, p, a=a: io[a], lambda io, p, a=a, b=b: b.half_of_shard(io[n + a], p[2]))
              for a, b in enumerate(bigs)]
    out = _exchange("reduce_share_d2d", pieces, shard_like, remote, local)
    return [o.reshape(b.shard_shape) for o, b in zip(out, bigs)]


def _mm(name, a, b, *, nt, tm, tn, tk, epi, extras=(), extra_specs=(), out_shape, out_specs, vmem_mib=48):
    m, kdim = a.shape
    n = b.shape[0] if nt else b.shape[1]
    gm, gn, gk = m // tm, n // tn, kdim // tk
    a_spec = pl.BlockSpec((tm, tk), lambda j, i, k: (i, k))
    b_spec = pl.BlockSpec((tn, tk), lambda j, i, k: (j, k)) if nt else pl.BlockSpec((tk, tn), lambda j, i, k: (k, j))
    n_ex = len(extras)

    def body(a_ref, b_ref, *rest):
        ex, outs, acc = rest[:n_ex], rest[n_ex:-1], rest[-1]
        part = (_dot_nt if nt else _dot)(a_ref[...], b_ref[...])
        if gk == 1:
            acc[...] = part
            epi(acc, ex, outs)
        else:
            k = pl.program_id(2)

            @pl.when(k == 0)
            def _():
                acc[...] = part

            @pl.when(k > 0)
            def _():
                acc[...] += part

            @pl.when(k == gk - 1)
            def _():
                epi(acc, ex, outs)

    return pl.pallas_call(
        body, name=name, grid=(gn, gm, gk), in_specs=[a_spec, b_spec, *extra_specs], out_specs=tuple(out_specs),
        out_shape=tuple(out_shape), scratch_shapes=[pltpu.VMEM((tm, tn), F32)], compiler_params=_cp(3, vmem_mib),
    )(a, b, *extras)


def _mm_tn(name, a, b, out_dtype, *, tmo, tn, tt, vmem_mib=48):
    t, m = a.shape
    n = b.shape[1]
    gt = t // tt

    def body(a_ref, b_ref, o_ref, acc):
        k = pl.program_id(2)
        part = _dot_tn(a_ref[...], b_ref[...])

        @pl.when(k == 0)
        def _():
            acc[...] = part

        @pl.when(k > 0)
        def _():
            acc[...] += part

        @pl.when(k == gt - 1)
        def _():
            o_ref[...] = acc[...].astype(o_ref.dtype)

    return pl.pallas_call(
        body, name=name, grid=(m // tmo, n // tn, gt),
        in_specs=[pl.BlockSpec((tt, tmo), lambda i, j, k: (k, i)), pl.BlockSpec((tt, tn), lambda i, j, k: (k, j))],
        out_specs=pl.BlockSpec((tmo, tn), lambda i, j, k: (i, j)), out_shape=jax.ShapeDtypeStruct((m, n), out_dtype),
        scratch_shapes=[pltpu.VMEM((tmo, tn), F32)], compiler_params=_cp(3, vmem_mib),
    )(a, b)


def _row_spec(d):
    return pl.BlockSpec((1, d), lambda *_: (0, 0))


def _stat_spec(k, d):
    return pl.BlockSpec((k, 8, d), lambda *_: (0, 0, 0))


def _rope(z, cs, sn):
    first = (lax.broadcasted_iota(jnp.int32, (z.shape[0], 128), 1) % 32) < 16
    outs = []
    for j in range(z.shape[1] // 128):
        zc = z[:, 128 * j:128 * (j + 1)]
        partner = jnp.where(first, pltpu.roll(zc, 112, 1), pltpu.roll(zc, 16, 1))
        outs.append(zc * cs + partner * sn)
    return outs[0] if len(outs) == 1 else jnp.concatenate(outs, axis=1)


def _rope_tables(length, rotate):
    if not rotate:
        return jnp.ones((length, 128), F32), jnp.zeros((length, 128), F32)
    half = HEAD_DIM // 2
    inv_freq = ROPE_BASE ** (-jnp.arange(0, half, 2, dtype=F32) / half)
    t = jnp.arange(length)
    row = (t // GRID_W).astype(F32)
    col = (t % GRID_W).astype(F32)
    e = jnp.arange(128) % HEAD_DIM
    pos = jnp.where(e[None, :] < half, row[:, None], col[:, None])
    ang = pos * inv_freq[(e % half) % (half // 2)][None, :]
    first = ((e % half) < half // 2)[None, :]
    return jnp.cos(ang), jnp.where(first, -jnp.sin(ang), jnp.sin(ang))


def _mixer_in(name, x, nw, sh, sc, w_in, cos, sin):
    t, d = x.shape
    tm = _pick(t, 256, 128)
    n_in = w_in.shape[1]

    def body(x_ref, nw_ref, sh_ref, sc_ref, w_ref, cos_ref, sin_ref, h_ref, q_ref, k_ref, v_ref, u_ref):
        xf = x_ref[...]
        r = lax.rsqrt(jnp.mean(xf * xf, axis=-1, keepdims=True) + EPS)
        hb = (((xf * r) * nw_ref[...]) * (1.0 + sc_ref[...]) + sh_ref[...]).astype(BF16)
        h_ref[...] = hb
        p = _dot(hb, w_ref[...])
        cs, sn = cos_ref[...], sin_ref[...]
        q_ref[...] = _rope(p[:, :ATTN_WIDTH], cs, sn).astype(BF16)
        k_ref[...] = _rope(p[:, ATTN_WIDTH:ATTN_WIDTH + KV_WIDTH], cs, sn).astype(BF16)
        v_ref[...] = p[:, ATTN_WIDTH + KV_WIDTH:ATTN_WIDTH + 2 * KV_WIDTH].astype(BF16)
        u_ref[...] = p[:, ATTN_WIDTH + 2 * KV_WIDTH:]

    def tile(w):
        return pl.BlockSpec((tm, w), lambda i: (i, 0))

    return pl.pallas_call(
        body, name=name, grid=(t // tm,),
        in_specs=[tile(d), _row_spec(d), _row_spec(d), _row_spec(d), pl.BlockSpec((d, n_in), lambda i: (0, 0)),
                  tile(128), tile(128)],
        out_specs=(tile(d), tile(ATTN_WIDTH), tile(KV_WIDTH), tile(KV_WIDTH), tile(POOL_WIDTH)),
        out_shape=(jax.ShapeDtypeStruct((t, d), BF16), jax.ShapeDtypeStruct((t, ATTN_WIDTH), BF16),
                   jax.ShapeDtypeStruct((t, KV_WIDTH), BF16), jax.ShapeDtypeStruct((t, KV_WIDTH), BF16),
                   jax.ShapeDtypeStruct((t, POOL_WIDTH), F32)),
        compiler_params=_cp(1),
    )(x, nw, sh, sc, w_in, cos, sin)


def _attn_specs(nb, n_ctx):
    def blk(w, f):
        return pl.BlockSpec((BLOCK, w), lambda n: (f(n), 0))

    prev = lambda n: jnp.maximum(n - 1, 0)
    cur = lambda n: n
    nxt = lambda n: jnp.minimum(n + 1, nb - 1)
    kv = [blk(KV_WIDTH, prev), blk(KV_WIDTH, cur), blk(KV_WIDTH, nxt)]
    ctx = pl.BlockSpec((n_ctx, KV_WIDTH), lambda n: (0, 0))
    return [pl.BlockSpec(memory_space=pltpu.SMEM), blk(ATTN_WIDTH, cur)] + kv + kv + [ctx, ctx]


def _attn_mask(n, length, n_keys):
    row = lax.broadcasted_iota(jnp.int32, (BLOCK, n_keys), 0)
    col = lax.broadcasted_iota(jnp.int32, (BLOCK, n_keys), 1)
    kpos = (n - 1) * BLOCK + col
    return ((jnp.abs(col - BLOCK - row) <= BLOCK) & (kpos >= 0) & (kpos < length)) | (col >= 3 * BLOCK)


def _attn_fwd(q, k, v, kc, vc, sink):
    length = q.shape[0]
    nb = length // BLOCK
    n_ctx = kc.shape[0]
    n_keys = 3 * BLOCK + n_ctx

    def body(sink_ref, q_ref, kp, k0, kn, vp, v0, vn, kc_ref, vc_ref, o_ref):
        n = pl.program_id(0)
        valid = _attn_mask(n, length, n_keys)
        qb = q_ref[...]
        kall = jnp.concatenate([kp[...], k0[...], kn[...], kc_ref[...]], axis=0)
        vall = jnp.concatenate([vp[...], v0[...], vn[...], vc_ref[...]], axis=0)
        outs = []
        for h in range(N_Q_HEADS):
            g = h // GROUP
            lanes = slice(HEAD_DIM * g, HEAD_DIM * (g + 1))
            s = _dot_nt(qb[:, HEAD_DIM * h:HEAD_DIM * (h + 1)], kall[:, lanes]) * SCALE
            s = jnp.where(valid, s, NEG_INF)
            sk = sink_ref[0, h]
            m = jnp.maximum(jnp.max(s, axis=-1, keepdims=True), sk)
            e = jnp.exp(s - m)
            den = jnp.sum(e, axis=-1, keepdims=True) + jnp.exp(sk - m)
            outs.append(_dot(e.astype(BF16), vall[:, lanes]) / den)
        o_ref[...] = jnp.concatenate(outs, axis=1).astype(BF16)

    return pl.pallas_call(
        body, name="attn_fwd", grid=(nb,), in_specs=_attn_specs(nb, n_ctx),
        out_specs=pl.BlockSpec((BLOCK, ATTN_WIDTH), lambda n: (n, 0)),
        out_shape=jax.ShapeDtypeStruct((length, ATTN_WIDTH), BF16), compiler_params=_cp(1),
    )(sink, q, k, k, k, v, v, v, kc, vc)


def _attn_bwd(q, k, v, kc, vc, sink, dmix):
    length = q.shape[0]
    nb = length // BLOCK
    n_ctx = kc.shape[0]
    n_keys = 3 * BLOCK + n_ctx

    def body(sink_ref, q_ref, kp, k0, kn, vp, v0, vn, kc_ref, vc_ref, do_ref,
             dq_ref, dkp_ref, dvp_ref, dkc_ref, dvc_ref, dsink_ref):
        n = pl.program_id(0)

        @pl.when(n == 0)
        def _():
            dkc_ref[...] = jnp.zeros_like(dkc_ref)
            dvc_ref[...] = jnp.zeros_like(dvc_ref)
            dsink_ref[...] = jnp.zeros_like(dsink_ref)

        valid = _attn_mask(n, length, n_keys)
        qb, dob = q_ref[...], do_ref[...]
        kall = jnp.concatenate([kp[...], k0[...], kn[...], kc_ref[...]], axis=0)
        vall = jnp.concatenate([vp[...], v0[...], vn[...], vc_ref[...]], axis=0)
        srow = lax.broadcasted_iota(jnp.int32, (8, 128), 0)
        slane = lax.broadcasted_iota(jnp.int32, (8, 128), 1)
        dqs, dks, dvs = [], [], []
        dsink = jnp.zeros((8, 128), F32)
        for g in range(N_KV_HEADS):
            lanes = slice(HEAD_DIM * g, HEAD_DIM * (g + 1))
            kg, vg = kall[:, lanes], vall[:, lanes]
            dk_g = jnp.zeros((n_keys, HEAD_DIM), F32)
            dv_g = jnp.zeros((n_keys, HEAD_DIM), F32)
            for h in range(GROUP * g, GROUP * (g + 1)):
                qh = qb[:, HEAD_DIM * h:HEAD_DIM * (h + 1)]
                doh = dob[:, HEAD_DIM * h:HEAD_DIM * (h + 1)]
                s = jnp.where(valid, _dot_nt(qh, kg) * SCALE, NEG_INF)
                sk = sink_ref[0, h]
                m = jnp.maximum(jnp.max(s, axis=-1, keepdims=True), sk)
                e = jnp.exp(s - m)
                inv = 1.0 / (jnp.sum(e, axis=-1, keepdims=True) + jnp.exp(sk - m))
                p = e * inv
                dp = _dot_nt(doh, vg)
                delta = jnp.sum(p * dp, axis=-1, keepdims=True)
                ds = (p * (dp - delta) * SCALE).astype(BF16)
                dqs.append(_dot(ds, kg))
                dk_g = dk_g + _dot_tn(ds, qh)
                dv_g = dv_g + _dot_tn(p.astype(BF16), doh)
                d_sink = -jnp.sum(jnp.exp(sk - m) * inv * delta, axis=0, keepdims=True)
                dsink = dsink + jnp.where((srow == 0) & (slane == h), d_sink, 0.0)
            dks.append(dk_g)
            dvs.append(dv_g)
        dq_ref[...] = jnp.concatenate(dqs, axis=1)
        dk = jnp.concatenate(dks, axis=1)
        dv = jnp.concatenate(dvs, axis=1)
        for j in range(3):
            dkp_ref[0, j] = dk[BLOCK * j:BLOCK * (j + 1)]
            dvp_ref[0, j] = dv[BLOCK * j:BLOCK * (j + 1)]
        dkc_ref[...] += dk[3 * BLOCK:]
        dvc_ref[...] += dv[3 * BLOCK:]
        dsink_ref[...] += dsink

    part = pl.BlockSpec((1, 3, BLOCK, KV_WIDTH), lambda n: (n, 0, 0, 0))
    ctx = pl.BlockSpec((n_ctx, KV_WIDTH), lambda n: (0, 0))
    return pl.pallas_call(
        body, name="attn_bwd", grid=(nb,),
        in_specs=_attn_specs(nb, n_ctx) + [pl.BlockSpec((BLOCK, ATTN_WIDTH), lambda n: (n, 0))],
        out_specs=(pl.BlockSpec((BLOCK, ATTN_WIDTH), lambda n: (n, 0)), part, part, ctx, ctx,
                   pl.BlockSpec((8, 128), lambda n: (0, 0))),
        out_shape=(jax.ShapeDtypeStruct((length, ATTN_WIDTH), F32),
                   jax.ShapeDtypeStruct((nb, 3, BLOCK, KV_WIDTH), F32), jax.ShapeDtypeStruct((nb, 3, BLOCK, KV_WIDTH), F32),
                   jax.ShapeDtypeStruct((n_ctx, KV_WIDTH), F32), jax.ShapeDtypeStruct((n_ctx, KV_WIDTH), F32),
                   jax.ShapeDtypeStruct((8, 128), F32)),
        compiler_params=_cp(1),
    )(sink, q, k, k, k, v, v, v, kc, vc, dmix)


def _assemble_dp(dq, dkp, dvp, du, cos, sin):
    length = dq.shape[0]
    nb = length // BLOCK

    def body(dq_ref, dka, dkb, dkc, dva, dvb, dvc, du_ref, cos_ref, sin_ref, o_ref):
        n = pl.program_id(0)
        has_next = (n + 1 < nb).astype(F32)
        has_prev = (n > 0).astype(F32)
        cs, sn = cos_ref[...], -sin_ref[...]
        dk = dka[0, 0] * has_next + dkb[0, 0] + dkc[0, 0] * has_prev
        dv = dva[0, 0] * has_next + dvb[0, 0] + dvc[0, 0] * has_prev
        o_ref[:, :ATTN_WIDTH] = _rope(dq_ref[...], cs, sn).astype(BF16)
        o_ref[:, ATTN_WIDTH:ATTN_WIDTH + KV_WIDTH] = _rope(dk, cs, sn).astype(BF16)
        o_ref[:, ATTN_WIDTH + KV_WIDTH:ATTN_WIDTH + 2 * KV_WIDTH] = dv.astype(BF16)
        o_ref[:, ATTN_WIDTH + 2 * KV_WIDTH:] = du_ref[...]

    def part(slot, f):
        return pl.BlockSpec((1, 1, BLOCK, KV_WIDTH), lambda n: (f(n), slot, 0, 0))

    parts = [part(0, lambda n: jnp.minimum(n + 1, nb - 1)), part(1, lambda n: n), part(2, lambda n: jnp.maximum(n - 1, 0))]

    def tile(w):
        return pl.BlockSpec((BLOCK, w), lambda n: (n, 0))

    width = ATTN_WIDTH + 2 * KV_WIDTH + POOL_WIDTH
    return pl.pallas_call(
        body, name="assemble_dp", grid=(nb,),
        in_specs=[tile(ATTN_WIDTH)] + parts + parts + [tile(POOL_WIDTH), tile(128), tile(128)],
        out_specs=tile(width), out_shape=jax.ShapeDtypeStruct((length, width), BF16), compiler_params=_cp(1),
    )(dq, dkp, dkp, dkp, dvp, dvp, dvp, du, cos, sin)


def _shift_rows(e, s):
    n = e.shape[0]
    return e if s % n == 0 else pltpu.roll(e, (-s) % n, 0)


def _window_sum(e, w, first):
    s, n = e, 1
    while n < w:
        s = s + _shift_rows(s, n)
        n *= 2
    return _shift_rows(s, first)


def _pool_geometry(i, tm, length):
    pos = i * tm - HALO + lax.broadcasted_iota(jnp.int32, (tm + 2 * HALO, 1), 0)
    inside = (pos >= 0) & (pos < length)
    inv_counts = []
    for w in POOL_WINDOWS:
        lo = jnp.clip(pos - w // 2, 0, length)
        hi = jnp.clip(pos - w // 2 + w, 0, length)
        inv_counts.append(1.0 / jnp.maximum(hi - lo, 1).astype(F32))
    return inside, inv_counts


def _halo_specs(tm, width, length, col=0):
    per = tm // HALO
    last = length // HALO - 1
    return [pl.BlockSpec((HALO, width), lambda i: (jnp.maximum(i * per - 1, 0), col)),
            pl.BlockSpec((tm, width), lambda i: (i, col)),
            pl.BlockSpec((HALO, width), lambda i: (jnp.minimum((i + 1) * per, last), col))]


def _pooled(ext, inv_counts, tm):
    outs = []
    for g, w in enumerate(POOL_WINDOWS):
        e = ext[:, POOL_GROUP_DIM * g:POOL_GROUP_DIM * (g + 1)]
        mean = _window_sum(e, w, -(w // 2)) * inv_counts[g]
        outs.append((mean - e)[HALO:HALO + tm])
    return outs


def _pool_fwd(u, pool_w, pool_scale):
    length = u.shape[0]
    tm = _pick(length, 256, 128)

    def body(up, u0, un, w_ref, sc_ref, o_ref):
        inside, inv_counts = _pool_geometry(pl.program_id(0), tm, length)
        ext = jnp.where(inside, jnp.concatenate([up[...], u0[...], un[...]], axis=0), 0.0)
        pooled = _pooled(ext, inv_counts, tm)
        mixed = [_dot(pooled[g].astype(BF16), w_ref[g]) for g in range(len(POOL_WINDOWS))]
        o_ref[...] = (jnp.concatenate(mixed, axis=1) * sc_ref[...]).astype(BF16)

    return pl.pallas_call(
        body, name="pool_fwd", grid=(length // tm,),
        in_specs=_halo_specs(tm, POOL_WIDTH, length) + [pl.BlockSpec(pool_w.shape, lambda i: (0, 0, 0)), _row_spec(POOL_WIDTH)],
        out_specs=pl.BlockSpec((tm, POOL_WIDTH), lambda i: (i, 0)),
        out_shape=jax.ShapeDtypeStruct((length, POOL_WIDTH), BF16), compiler_params=_cp(1),
    )(u, u, u, pool_w, pool_scale)


def _pool_bwd(u, dmix, pool_w, pool_scale):
    length = u.shape[0]
    tm = _pick(length, 256, 128)
    n_g = len(POOL_WINDOWS)

    def body(up, u0, un, dp_, d0, dn_, w_ref, sc_ref, du_ref, dw_ref, dsc_ref):
        i = pl.program_id(0)

        @pl.when(i == 0)
        def _():
            dw_ref[...] = jnp.zeros_like(dw_ref)
            dsc_ref[...] = jnp.zeros_like(dsc_ref)

        inside, inv_counts = _pool_geometry(i, tm, length)
        ext = jnp.where(inside, jnp.concatenate([up[...], u0[...], un[...]], axis=0), 0.0)
        dext = jnp.where(inside, jnp.concatenate([dp_[...], d0[...], dn_[...]], axis=0).astype(F32), 0.0)
        dmixed = (dext * sc_ref[...]).astype(BF16)
        pooled = _pooled(ext, inv_counts, tm)
        dus, dscs = [], []
        for g, w in enumerate(POOL_WINDOWS):
            lanes = slice(POOL_GROUP_DIM * g, POOL_GROUP_DIM * (g + 1))
            dpooled = _dot_nt(dmixed[:, lanes], w_ref[g])
            spread = _window_sum(dpooled * inv_counts[g], w, -(w // 2 - 1))
            dus.append((spread - dpooled)[HALO:HALO + tm])
            pb = pooled[g].astype(BF16)
            dw_ref[g] += _dot_tn(pb, dmixed[HALO:HALO + tm, lanes])
            prod = dext[HALO:HALO + tm, lanes] * _dot(pb, w_ref[g])
            dscs.append(_fold8(prod))
        du_ref[...] = jnp.concatenate(dus, axis=1).astype(BF16)
        dsc_ref[...] += jnp.concatenate(dscs, axis=1)

    return pl.pallas_call(
        body, name="pool_bwd", grid=(length // tm,),
        in_specs=_halo_specs(tm, POOL_WIDTH, length) + _halo_specs(tm, POOL_WIDTH, length, col=1)
        + [pl.BlockSpec(pool_w.shape, lambda i: (0, 0, 0)), _row_spec(POOL_WIDTH)],
        out_specs=(pl.BlockSpec((tm, POOL_WIDTH), lambda i: (i, 0)), pl.BlockSpec((n_g, POOL_GROUP_DIM, POOL_GROUP_DIM), lambda i: (0, 0, 0)),
                   pl.BlockSpec((8, POOL_WIDTH), lambda i: (0, 0))),
        out_shape=(jax.ShapeDtypeStruct((length, POOL_WIDTH), BF16), jax.ShapeDtypeStruct((n_g, POOL_GROUP_DIM, POOL_GROUP_DIM), F32),
                   jax.ShapeDtypeStruct((8, POOL_WIDTH), F32)),
        compiler_params=_cp(1),
    )(u, u, u, dmix, dmix, dmix, pool_w, pool_scale)


def _mixer_out(mix, w_out, x, g_a, nmw, sh_m, sc_m):
    t, d = x.shape
    tm = _pick(t, 256, 128)

    def epi(acc, ex, outs):
        x_ref, ga, nw, sh, sc = ex
        x1_ref, mo_ref, hm_ref = outs

        def rows(rs):
            mo = acc[rs, :]
            x1 = x_ref[rs, :] + ga[...] * mo
            x1_ref[rs, :] = x1
            mo_ref[rs, :] = mo.astype(BF16)
            r = lax.rsqrt(jnp.mean(x1 * x1, axis=-1, keepdims=True) + EPS)
            hm_ref[rs, :] = (((x1 * r) * nw[...]) * (1.0 + sc[...]) + sh[...]).astype(BF16)

        _row_loop(tm, rows)

    tile = pl.BlockSpec((tm, d), lambda j, i, k: (i, 0))
    return _mm("mixer_out", mix, w_out, nt=False, tm=tm, tn=d, tk=mix.shape[1], epi=epi,
               extras=(x, g_a, nmw, sh_m, sc_m), extra_specs=[tile] + [_row_spec(d)] * 4,
               out_shape=(jax.ShapeDtypeStruct((t, d), F32), jax.ShapeDtypeStruct((t, d), BF16), jax.ShapeDtypeStruct((t, d), BF16)),
               out_specs=(tile, tile, tile))


def _mlp_up(hm, w_up):
    t, d = hm.shape
    tm = _pick(t, 512, 256, 128)
    tn = 2048

    def epi(acc, ex, outs):
        outs[0][...] = jnp.square(jnp.maximum(acc[...], 0.0)).astype(BF16)

    return _mm("mlp_up", hm, w_up, nt=False, tm=tm, tn=tn, tk=d, epi=epi,
               out_shape=(jax.ShapeDtypeStruct((t, w_up.shape[1]), BF16),),
               out_specs=(pl.BlockSpec((tm, tn), lambda j, i, k: (i, j)),))[0]


def _mlp_down_loss(act, w_down, x1, target, g_m, fw):
    t, d = x1.shape
    tm = _pick(t, 512, 256, 128)

    def epi(acc, ex, outs):
        x1_ref, t_ref, gm, fw_ref = ex
        dx2_ref, ddn_ref, st_ref = outs

        @pl.when(pl.program_id(1) == 0)
        def _():
            st_ref[...] = jnp.zeros_like(st_ref)

        def rows(rs):
            dn = acc[rs, :]
            x2 = x1_ref[rs, :] + gm[...] * dn
            r = lax.rsqrt(jnp.mean(x2 * x2, axis=-1, keepdims=True) + EPS)
            xh = x2 * r
            diff = xh * fw_ref[...] - t_ref[rs, :]
            dy = diff * (1.0 / d)
            dxh = dy * fw_ref[...]
            dx2 = r * (dxh - xh * jnp.mean(dxh * xh, axis=-1, keepdims=True))
            dx2_ref[rs, :] = dx2
            ddn_ref[rs, :] = (dx2 * gm[...]).astype(BF16)
            st_ref[0] += _fold8(diff * diff)
            st_ref[1] += _fold8(dy * xh)
            st_ref[2] += _fold8(dx2 * dn)

        _row_loop(tm, rows)

    tile = pl.BlockSpec((tm, d), lambda j, i, k: (i, 0))
    return _mm("mlp_down_loss", act, w_down, nt=False, tm=tm, tn=d, tk=512, epi=epi,
               extras=(x1, target, g_m, fw), extra_specs=[tile, tile, _row_spec(d), _row_spec(d)],
               out_shape=(jax.ShapeDtypeStruct((t, d), F32), jax.ShapeDtypeStruct((t, d), BF16), jax.ShapeDtypeStruct((3, 8, d), F32)),
               out_specs=(tile, tile, _stat_spec(3, d)), vmem_mib=56)


def _mlp_dact(ddn, w_down, act):
    t, d = ddn.shape
    tm = _pick(t, 512, 256, 128)
    tn = 2048

    def epi(acc, ex, outs):
        outs[0][...] = (acc[...] * (2.0 * jnp.sqrt(ex[0][...].astype(F32)))).astype(BF16)

    tile = pl.BlockSpec((tm, tn), lambda j, i, k: (i, j))
    return _mm("mlp_dact", ddn, w_down, nt=True, tm=tm, tn=tn, tk=d, epi=epi, extras=(act,), extra_specs=[tile],
               out_shape=(jax.ShapeDtypeStruct(act.shape, BF16),), out_specs=(tile,))[0]


def _norm_bwd_rows(dh, xv, nw, sc, st_ref):
    r = lax.rsqrt(jnp.mean(xv * xv, axis=-1, keepdims=True) + EPS)
    xh = xv * r
    dy = dh * (1.0 + sc)
    st_ref[0] += _fold8(dh)
    st_ref[1] += _fold8(dh * (xh * nw))
    st_ref[2] += _fold8(dy * xh)
    dxh = dy * nw
    return r * (dxh - xh * jnp.mean(dxh * xh, axis=-1, keepdims=True))


def _mlp_dx(dup, w_up, x1, dx2, mo, nmw, sc_m, g_a):
    t, d = x1.shape
    tm = _pick(t, 512, 256, 128)

    def epi(acc, ex, outs):
        x1_ref, dx2_ref, mo_ref, nw, sc, ga = ex
        dx1_ref, dmi_ref, st_ref = outs

        @pl.when(pl.program_id(1) == 0)
        def _():
            st_ref[...] = jnp.zeros_like(st_ref)

        def rows(rs):
            dx1 = _norm_bwd_rows(acc[rs, :], x1_ref[rs, :], nw[...], sc[...], st_ref) + dx2_ref[rs, :]
            dx1_ref[rs, :] = dx1
            dmi_ref[rs, :] = (dx1 * ga[...]).astype(BF16)
            st_ref[3] += _fold8(dx1 * mo_ref[rs, :].astype(F32))

        _row_loop(tm, rows)

    tile = pl.BlockSpec((tm, d), lambda j, i, k: (i, 0))
    return _mm("mlp_dx", dup, w_up, nt=True, tm=tm, tn=d, tk=512, epi=epi,
               extras=(x1, dx2, mo, nmw, sc_m, g_a), extra_specs=[tile, tile, tile] + [_row_spec(d)] * 3,
               out_shape=(jax.ShapeDtypeStruct((t, d), F32), jax.ShapeDtypeStruct((t, d), BF16), jax.ShapeDtypeStruct((4, 8, d), F32)),
               out_specs=(tile, tile, _stat_spec(4, d)), vmem_mib=56)


def _mixer_dmix(dmi, w_out):
    t, d = dmi.shape
    tm = _pick(t, 512, 256, 128)

    def epi(acc, ex, outs):
        outs[0][...] = acc[...].astype(BF16)

    n = w_out.shape[0]
    return _mm("mixer_dmix", dmi, w_out, nt=True, tm=tm, tn=n, tk=d, epi=epi,
               out_shape=(jax.ShapeDtypeStruct((t, n), BF16),), out_specs=(pl.BlockSpec((tm, n), lambda j, i, k: (i, 0)),))[0]


def _mixer_dx(name, dp, w_in, x, dx1, naw, sc_a):
    t, d = x.shape
    tm = _pick(t, 256, 128)

    def epi(acc, ex, outs):
        x_ref, dx1_ref, nw, sc = ex
        gx_ref, st_ref = outs

        @pl.when(pl.program_id(1) == 0)
        def _():
            st_ref[...] = jnp.zeros_like(st_ref)

        def rows(rs):
            gx_ref[rs, :] = _norm_bwd_rows(acc[rs, :], x_ref[rs, :], nw[...], sc[...], st_ref) + dx1_ref[rs, :]

        _row_loop(tm, rows)

    tile = pl.BlockSpec((tm, d), lambda j, i, k: (i, 0))
    return _mm(name, dp, w_in, nt=True, tm=tm, tn=d, tk=dp.shape[1], epi=epi,
               extras=(x, dx1, naw, sc_a), extra_specs=[tile, tile, _row_spec(d), _row_spec(d)],
               out_shape=(jax.ShapeDtypeStruct((t, d), F32), jax.ShapeDtypeStruct((3, 8, d), F32)),
               out_specs=(tile, _stat_spec(3, d)))


def _silu(v):
    return v / (1.0 + jnp.exp(-v))


def _ada_fwd(cond, w_ada, b_ada):
    d, n = w_ada.shape
    tn = 512

    def body(c_ref, w_ref, b_ref, o_ref):
        o_ref[...] = _dot(_silu(c_ref[...]).astype(BF16), w_ref[...].astype(BF16)) + b_ref[...]

    return pl.pallas_call(
        body, name="ada_fwd", grid=(n // tn,),
        in_specs=[pl.BlockSpec(cond.shape, lambda j: (0, 0)), pl.BlockSpec((d, tn), lambda j: (0, j)), pl.BlockSpec((1, tn), lambda j: (0, j))],
        out_specs=pl.BlockSpec((cond.shape[0], tn), lambda j: (0, j)), out_shape=jax.ShapeDtypeStruct((cond.shape[0], n), F32),
        compiler_params=_cp(1),
    )(cond, w_ada, b_ada)


def _adamw_math(w, g, m, v):
    m = ADAM_B1 * m + (1.0 - ADAM_B1) * g
    v = ADAM_B2 * v + (1.0 - ADAM_B2) * jnp.square(g)
    m_hat = m / (1.0 - ADAM_B1 ** ADAM_STEP)
    v_hat = v / (1.0 - ADAM_B2 ** ADAM_STEP)
    return -ADAM_LR * (m_hat / (jnp.sqrt(v_hat) + ADAM_EPS) + ADAM_WD * w), m, v


def _ada_bwd(cond, dm, w_ada, m_ada, v_ada):
    d, n = w_ada.shape
    tn = 256
    rows = cond.shape[0]

    def body(c_ref, dm_ref, w_ref, m_ref, v_ref, g_ref, dl_ref, nm_ref, nv_ref, pc_ref):
        @pl.when(pl.program_id(0) == 0)
        def _():
            pc_ref[...] = jnp.zeros_like(pc_ref)

        dmb = dm_ref[...].astype(BF16)
        w = w_ref[...]
        g = _dot_tn(_silu(c_ref[...]).astype(BF16), dmb)
        g_ref[...] = g
        dl_ref[...], nm_ref[...], nv_ref[...] = _adamw_math(w, g, m_ref[...], v_ref[...])
        pc_ref[...] += _dot_nt(dm_ref[8:16, :].astype(BF16), w.astype(BF16))

    tile = pl.BlockSpec((d, tn), lambda j: (0, j))
    like = jax.ShapeDtypeStruct((d, n), F32)
    return pl.pallas_call(
        body, name="ada_bwd", grid=(n // tn,),
        in_specs=[pl.BlockSpec((rows, d), lambda j: (0, 0)), pl.BlockSpec((rows, tn), lambda j: (0, j)), tile, tile, tile],
        out_specs=(tile, tile, tile, tile, pl.BlockSpec((8, d), lambda j: (0, 0))),
        out_shape=(like, like, like, like, jax.ShapeDtypeStruct((8, d), F32)), compiler_params=_cp(1),
    )(cond, dm, w_ada, m_ada, v_ada)


def _adamw(name, w, g, m, v):
    return _ew(name, _adamw_math, [w, g, m, v], [F32, F32, F32])


def _colsum(st):
    return jnp.sum(st, axis=1)


def kernel(x, c, ctx, c_ctx, norm_attn_w, norm_mlp_w, w_ada, b_ada, w_in, attn_sink, pool_w, pool_scale, w_out, w_mlp_up, w_mlp_down, final_norm_w, loss_target, m_c_ctx, m_norm_attn_w, m_norm_mlp_w, m_w_ada, m_b_ada, m_w_in, m_attn_sink, m_pool_w, m_pool_scale, m_w_out, m_w_mlp_up, m_w_mlp_down, m_final_norm_w, v_c_ctx, v_norm_attn_w, v_norm_mlp_w, v_w_ada, v_b_ada, v_w_in, v_attn_sink, v_pool_w, v_pool_scale, v_w_out, v_w_mlp_up, v_w_mlp_down, v_final_norm_w):
    length, d = x.shape[1], x.shape[2]
    n_ctx = ctx.shape[1]
    pos = (lax.axis_index("x"), lax.axis_index("y"), lax.axis_index("c"))
    me, chip = _dev_index(pos), _chip_index(pos)
    xs, tgt, cx = x.reshape(length, d), loss_target.reshape(length, d), ctx.reshape(n_ctx, d)
    n_ada = w_ada.shape[2]

    c_all = _allgather8("gather_c", jnp.pad(c, ((0, 7), (0, 0))))
    cond = jnp.concatenate([c_all[:, 0, :], c_ctx[None, :], jnp.zeros((7, d), F32)], axis=0)
    b_shard = lax.dynamic_slice_in_dim(b_ada, chip * n_ada, n_ada, axis=1)
    mod_all = _allgather8("gather_mod", _ada_fwd(cond, w_ada[0], b_shard))
    mod = jnp.concatenate([mod_all[0], mod_all[2], mod_all[4], mod_all[6]], axis=1)
    mine = lax.dynamic_slice_in_dim(mod, me, 1, axis=0)
    sh_a, sc_a, g_a, sh_m, sc_m, g_m = [mine[:, d * i:d * (i + 1)] for i in range(6)]
    csh_a, csc_a = mod[8:9, :d], mod[8:9, d:2 * d]

    bigs = [_Big("col", w_in.shape[1:]), _Big("pool", pool_w.shape[1:]), _Big("row", w_out.shape[1:]),
            _Big("col", w_mlp_up.shape[1:]), _Big("row", w_mlp_down.shape[1:])]
    shards = [w_in[0], pool_w[0], w_out[0], w_mlp_up[0], w_mlp_down[0]]
    win_b, pw_b, wout_b, wup_b, wdn_b = _gather_weights(bigs, [s.astype(BF16) for s in shards])
    wout_b = wout_b.reshape(-1, d)
    wdn_b = wdn_b.reshape(-1, d)

    cos, sin = _rope_tables(length, True)
    one, zero = _rope_tables(n_ctx, False)
    h, q, k, v, u = _mixer_in("mixer_in", xs, norm_attn_w, sh_a, sc_a, win_b, cos, sin)
    hc, _, kc, vc, _ = _mixer_in("mixer_in_ctx", cx, norm_attn_w, csh_a, csc_a, win_b, one, zero)
    attn = _attn_fwd(q, k, v, kc, vc, attn_sink)
    pooled = _pool_fwd(u, pw_b, pool_scale)
    mix = jnp.concatenate([attn, pooled], axis=1)
    x1, mo, hm = _mixer_out(mix, wout_b, xs, g_a, norm_mlp_w, sh_m, sc_m)
    act = _mlp_up(hm, wup_b)
    dx2, ddn, st_loss = _mlp_down_loss(act, wdn_b, x1, tgt, g_m, final_norm_w[None, :])
    st_loss = _colsum(st_loss)
    loss = lax.psum(0.5 / d * jnp.sum(st_loss[0]), ("x", "y", "c"))

    g_wdn = _mm_tn("grad_w_down", act, ddn, BF16, tmo=1024, tn=d, tt=_pick(length, 512, 256, 128))
    dup = _mlp_dact(ddn, wdn_b, act)
    g_wup = _mm_tn("grad_w_up", hm, dup, BF16, tmo=d, tn=1024, tt=_pick(length, 512, 256, 128))
    dx1, dmi, st_mlp = _mlp_dx(dup, wup_b, x1, dx2, mo, norm_mlp_w, sc_m, g_a)
    st_mlp = _colsum(st_mlp)
    g_wout = _mm_tn("grad_w_out", mix, dmi, BF16, tmo=1024, tn=d, tt=_pick(length, 512, 256, 128))
    dmix = _mixer_dmix(dmi, wout_b)
    dq, dkp, dvp, dkc, dvc, dsink = _attn_bwd(q, k, v, kc, vc, attn_sink, dmix)
    du, g_pw, st_pool = _pool_bwd(u, dmix, pw_b, pool_scale)
    dp = _assemble_dp(dq, dkp, dvp, du, cos, sin)
    dpc = jnp.concatenate([jnp.zeros((n_ctx, ATTN_WIDTH), BF16), dkc.astype(BF16), dvc.astype(BF16),
                           jnp.zeros((n_ctx, POOL_WIDTH), BF16)], axis=1)
    grad_x, st_mix = _mixer_dx("mixer_dx", dp, win_b, xs, dx1, norm_attn_w, sc_a)
    _, st_ctx = _mixer_dx("mixer_dx_ctx", dpc, win_b, cx, jnp.zeros((n_ctx, d), F32), norm_attn_w, csc_a)
    st_mix, st_ctx = _colsum(st_mix), _colsum(st_ctx)
    h_all = jnp.concatenate([h, hc], axis=0)
    dp_all = jnp.concatenate([dp, dpc], axis=0)
    g_win = _mm_tn("grad_w_in", h_all, dp_all, BF16, tmo=d, tn=dp.shape[1] // 2, tt=_pick(length + n_ctx, 512, 256, 128))

    zrow = jnp.zeros((d,), F32)
    pad = lambda a: jnp.pad(a, (0, d - a.shape[0]))
    mine_rows = [st_mix[0], st_mix[1], st_mlp[3], st_mlp[0], st_mlp[1], st_loss[2],
                 st_ctx[0], st_ctx[1],
                 st_mix[2] + st_ctx[2], st_mlp[2], st_loss[1],
                 pad(jnp.sum(st_pool, axis=0)), pad(dsink[0, :N_Q_HEADS])] + [zrow] * 3
    small_all = _allgather8("gather_small", jnp.stack(mine_rows))
    small = small_all[0]
    for i in range(1, 8):
        small = small + small_all[i]
    dm_rows = small_all[:, 0:6, :].reshape(8, 6 * d)
    dm_ctx = jnp.concatenate([small[6], small[7], jnp.zeros((4 * d,), F32)])[None, :]
    dm = jnp.concatenate([dm_rows, dm_ctx, jnp.zeros((7, 6 * d), F32)], axis=0)
    g_bada = jnp.sum(dm[:9], axis=0, keepdims=True)
    dm_shard = lax.dynamic_slice_in_dim(dm, chip * n_ada, n_ada, axis=1)
    g_wada, dl_wada, nm_wada, nv_wada, part_cctx = _ada_bwd(cond, dm_shard, w_ada[0], m_w_ada[0], v_w_ada[0])
    cctx_all = _allgather8("gather_cctx", part_cctx)
    dsilu_in = cctx_all[0, 0] + cctx_all[2, 0] + cctx_all[4, 0] + cctx_all[6, 0]
    sig = 1.0 / (1.0 + jnp.exp(-c_ctx))
    g_cctx = dsilu_in * (sig * (1.0 + c_ctx * (1.0 - sig)))

    g_shards = _reduce_grads(bigs, [g_win, g_pw.astype(BF16), g_wout.reshape(bigs[2].full_shape),
                                    g_wup, g_wdn.reshape(bigs[4].full_shape)], pos)
    big_w = [w_in, pool_w, w_out, w_mlp_up, w_mlp_down]
    big_m = [m_w_in, m_pool_w, m_w_out, m_w_mlp_up, m_w_mlp_down]
    big_v = [v_w_in, v_pool_w, v_w_out, v_w_mlp_up, v_w_mlp_down]
    big_names = ["w_in", "pool_w", "w_out", "w_mlp_up", "w_mlp_down"]
    res = {}
    for nm, w_, g_, m_, v_ in zip(big_names, big_w, g_shards, big_m, big_v):
        g_ = g_.reshape(w_.shape)
        res[nm] = (g_,) + tuple(_adamw("adamw_" + nm, w_, g_, m_, v_))
    res["w_ada"] = (g_wada[None], dl_wada[None], nm_wada[None], nv_wada[None])

    def pack(cc, na, nm_, ba, sk, ps, fn):
        rows = [cc.reshape(1, d), na.reshape(1, d), nm_.reshape(1, d), ba.reshape(6, d),
                jnp.pad(sk.reshape(1, -1), ((0, 0), (0, d - N_Q_HEADS))), jnp.pad(ps.reshape(1, -1), ((0, 0), (0, d - POOL_WIDTH))),
                fn.reshape(1, d), jnp.zeros((4, d), F32)]
        return jnp.concatenate(rows, axis=0)

    w_s = pack(c_ctx, norm_attn_w, norm_mlp_w, b_ada, attn_sink, pool_scale, final_norm_w)
    m_s = pack(m_c_ctx, m_norm_attn_w, m_norm_mlp_w, m_b_ada, m_attn_sink, m_pool_scale, m_final_norm_w)
    v_s = pack(v_c_ctx, v_norm_attn_w, v_norm_mlp_w, v_b_ada, v_attn_sink, v_pool_scale, v_final_norm_w)
    g_s = pack(g_cctx, small[8], small[9], g_bada, small[12][:N_Q_HEADS], small[11][:POOL_WIDTH], small[10])
    small_out = [g_s] + _adamw("adamw_small", w_s, g_s, m_s, v_s)

    def unpack(p):
        return {"c_ctx": p[0], "norm_attn_w": p[1:2], "norm_mlp_w": p[2:3], "b_ada": p[3:9].reshape(1, 6 * d),
                "attn_sink": p[9:10, :N_Q_HEADS], "pool_scale": p[10:11, :POOL_WIDTH], "final_norm_w": p[11]}

    small_res = [unpack(p) for p in small_out]
    order = ["c_ctx", "norm_attn_w", "norm_mlp_w", "w_ada", "b_ada", "w_in", "attn_sink", "pool_w", "pool_scale",
             "w_out", "w_mlp_up", "w_mlp_down", "final_norm_w"]
    outs = [loss, grad_x.reshape(x.shape)]
    for kind in range(4):
        for nm in order:
            outs.append(res[nm][kind] if nm in res else small_res[kind][nm])
    return tuple(outs)
```

```python
import functools

import jax
import jax.numpy as jnp
from jax import lax
from jax.experimental import pallas as pl
from jax.experimental.pallas import tpu as pltpu

F32 = jnp.float32
BF16 = jnp.bfloat16
EPS = 1e-6
NEG_INF = -1e30
HEAD_DIM = 64
N_Q_HEADS = 16
N_KV_HEADS = 4
GROUP = N_Q_HEADS // N_KV_HEADS
ATTN_WIDTH = N_Q_HEADS * HEAD_DIM
KV_WIDTH = N_KV_HEADS * HEAD_DIM
POOL_WINDOWS = (2, 4, 8, 16)
POOL_GROUP_DIM = 256
POOL_WIDTH = len(POOL_WINDOWS) * POOL_GROUP_DIM
BLOCK = 128
GRID_W = 64
ROPE_BASE = 10000.0
SCALE = HEAD_DIM ** -0.5
HALO = 16
ROWS = 16
ADAM_LR, ADAM_B1, ADAM_B2, ADAM_EPS, ADAM_WD, ADAM_STEP = 0.001, 0.9, 0.999, 1e-08, 0.01, 10
MESH = pl.DeviceIdType.MESH
MIB = 1024 * 1024
ANY = pl.BlockSpec(memory_space=pl.ANY)


def _cp(n_axes, vmem_mib=48):
    return pltpu.CompilerParams(dimension_semantics=("arbitrary",) * n_axes, vmem_limit_bytes=vmem_mib * MIB)


def _row_loop(rows, fn):
    def body(r, carry):
        fn(pl.ds(pl.multiple_of(r * ROWS, ROWS), ROWS))
        return carry

    lax.fori_loop(0, rows // ROWS, body, 0)


def _fold8(v):
    s = v[0:8]
    for t in range(1, v.shape[0] // 8):
        s = s + v[8 * t:8 * t + 8]
    return s


def _dot(a, b):
    return jnp.dot(a, b, preferred_element_type=F32)


def _dot_nt(a, b):
    return lax.dot_general(a, b, (((1,), (1,)), ((), ())), preferred_element_type=F32)


def _dot_tn(a, b):
    return lax.dot_general(a, b, (((0,), (0,)), ((), ())), preferred_element_type=F32)


def _pick(n, *cands):
    for t in cands:
        if n % t == 0:
            return t
    return n


def _flip(pos, mask):
    return tuple((1 - v) if (mask >> (2 - i)) & 1 else v for i, v in enumerate(pos))


def _exchange(name, ins, out_shapes, remote, local=(), aliases=None):
    n_io = len(ins) + len(out_shapes)

    def body(*refs):
        io = refs[:n_io]
        send_sems, recv_sems, local_sems = refs[n_io:]
        me = (lax.axis_index("x"), lax.axis_index("y"), lax.axis_index("c"))

        def copy(i, sender):
            mask, src_fn, dst_fn = remote[i]
            return pltpu.make_async_remote_copy(
                src_ref=src_fn(io, sender), dst_ref=dst_fn(io, sender), send_sem=send_sems.at[i],
                recv_sem=recv_sems.at[i], device_id=_flip(sender, mask), device_id_type=MESH)

        own = [pltpu.make_async_copy(s(io, me), d(io, me), local_sems.at[i]) for i, (s, d) in enumerate(local)]
        for cp in own:
            cp.start()
        sends = [copy(i, me) for i in range(len(remote))]
        for cp in sends:
            cp.start()
        for i in range(len(remote)):
            copy(i, _flip(me, remote[i][0])).wait_recv()
        for cp in sends:
            cp.wait_send()
        for cp in own:
            cp.wait()

    return pl.pallas_call(
        body, name=name, out_shape=tuple(out_shapes),
        in_specs=[ANY] * len(ins), out_specs=tuple([ANY] * len(out_shapes)),
        scratch_shapes=[pltpu.SemaphoreType.DMA((len(remote),)), pltpu.SemaphoreType.DMA((len(remote),)),
                        pltpu.SemaphoreType.DMA((max(len(local), 1),))],
        input_output_aliases=aliases or {},
    )(*ins)


def _dev_index(pos):
    return 4 * pos[0] + 2 * pos[1] + pos[2]


def _chip_index(pos):
    return 2 * pos[0] + pos[1]


def _allgather8(name, v):
    out = jax.ShapeDtypeStruct((8,) + v.shape, v.dtype)
    remote = [(mask, lambda io, pos: io[0], lambda io, pos: io[1].at[_dev_index(pos)]) for mask in range(1, 8)]
    local = [(lambda io, pos: io[0], lambda io, pos: io[1].at[_dev_index(pos)])]
    return _exchange(name, [v], [out], remote, local)[0]


class _Big:
    def __init__(self, kind, shard_shape):
        self.kind = kind
        self.shard_shape = tuple(shard_shape)
        if kind == "col":
            r, cs = shard_shape
            self.full_shape = (r, 4 * cs)
            self.piece_shape = (r // 2, cs)
            self.half_shape = (r // 2, 4 * cs)
        elif kind == "row":
            rs, c = shard_shape
            self.full_shape = (4, 2, rs // 2, c)
            self.piece_shape = (1, 1, rs // 2, c)
            self.half_shape = (4, 1, rs // 2, c)
        else:
            self.full_shape = (4, 256, 256)
            self.piece_shape = (2, 64, 256)
            self.half_shape = (2, 256, 256)

    def shard_as_pieces(self, a):
        return a.reshape((1, 2) + self.piece_shape[2:]) if self.kind == "row" else a

    def piece(self, ref, k, h):
        if self.kind == "col":
            r, cs = self.piece_shape
            return ref.at[pl.ds(h * r, r), pl.ds(k * cs, cs)]
        if self.kind == "row":
            return ref.at[pl.ds(k, 1), pl.ds(h, 1)]
        return ref.at[pl.ds(2 * h, 2), pl.ds(64 * k, 64)]

    def half_of_shard(self, ref, h):
        if self.kind == "col":
            return ref.at[pl.ds(h * self.piece_shape[0], self.piece_shape[0])]
        if self.kind == "row":
            return ref.at[:, pl.ds(h, 1)]
        return ref.at[pl.ds(2 * h, 2)]

    def half_of_full(self, ref, h):
        if self.kind == "col":
            return ref.at[pl.ds(h * self.half_shape[0], self.half_shape[0])]
        if self.kind == "row":
            return ref.at[:, pl.ds(h, 1)]
        return ref.at[pl.ds(2 * h, 2)]

    def piece_of_half(self, ref, k):
        if self.kind == "col":
            return ref.at[:, pl.ds(k * self.piece_shape[1], self.piece_shape[1])]
        if self.kind == "row":
            return ref.at[pl.ds(k, 1)]
        return ref.at[:, pl.ds(64 * k, 64)]

    def place_shard(self, full, shard, k):
        if self.kind == "col":
            return lax.dynamic_update_slice(full, shard, (0, k * self.shard_shape[1]))
        if self.kind == "row":
            return lax.dynamic_update_slice(full, self.shard_as_pieces(shard), (k, 0, 0, 0))
        return lax.dynamic_update_slice(full, shard, (0, 64 * k, 0))

    def place_half(self, shard, piece, h):
        if self.kind == "col":
            return lax.dynamic_update_slice(shard, piece, (h * self.piece_shape[0], 0))
        if self.kind == "row":
            return lax.dynamic_update_slice(shard, piece, (0, h, 0, 0))
        return lax.dynamic_update_slice(shard, piece, (2 * h, 0, 0))

    def take_half(self, a, h):
        if self.kind == "row":
            return lax.dynamic_slice_in_dim(a, h, 1, axis=1)
        n = self.half_shape[0]
        return lax.dynamic_slice_in_dim(a, h * n, n, axis=0)

    def take_piece_of_half(self, a, k):
        if self.kind == "col":
            return lax.dynamic_slice_in_dim(a, k * self.piece_shape[1], self.piece_shape[1], axis=1)
        if self.kind == "row":
            return lax.dynamic_slice_in_dim(a, k, 1, axis=0)
        return lax.dynamic_slice_in_dim(a, 64 * k, 64, axis=1)


CHIP_MASKS = (4, 2, 6)


def _gather_weights(bigs, shards, pos):
    n = len(bigs)
    fulls = [jax.ShapeDtypeStruct(b.full_shape, BF16) for b in bigs]
    placed = [b.place_shard(jnp.zeros(b.full_shape, BF16), s, _chip_index(pos)) for b, s in zip(bigs, shards)]
    remote = []
    for a, b in enumerate(bigs):
        for mask in CHIP_MASKS:
            def mine(io, p, a=a, b=b):
                return b.piece(io[n + a], _chip_index(p), p[2])
            remote.append((mask, mine, mine))
    got = _exchange("gather_weights_ici", placed, fulls, remote, aliases={a: a for a in range(n)})
    remote = []
    for a, b in enumerate(bigs):
        for mask in CHIP_MASKS:
            def region(io, p, a=a, b=b, mask=mask):
                return b.piece(io[n + a], _chip_index(_flip(p, mask)), p[2])
            remote.append((1, region, region))
    return _exchange("gather_weights_d2d", list(got), fulls, remote, aliases={a: a for a in range(n)})


def _ew(name, fn, ins, out_dtypes, rows_per_step=256):
    shape = ins[0].shape
    last = shape[-1]
    rows = 1
    for s in shape[:-1]:
        rows *= s
    ins2 = [a.reshape(rows, last) for a in ins]
    tr = _pick(rows, rows_per_step, 128, 64, 32, 16, 8)
    spec = pl.BlockSpec((tr, last), lambda i: (i, 0))

    def body(*refs):
        outs = fn(*[r[...] for r in refs[:len(ins)]])
        for o_ref, o in zip(refs[len(ins):], outs):
            o_ref[...] = o.astype(o_ref.dtype)

    outs = pl.pallas_call(
        body, name=name, grid=(rows // tr,), in_specs=[spec] * len(ins), out_specs=tuple([spec] * len(out_dtypes)),
        out_shape=tuple(jax.ShapeDtypeStruct((rows, last), d) for d in out_dtypes), compiler_params=_cp(1),
    )(*ins2)
    return [o.reshape(shape) for o in outs]


def _reduce_grads(bigs, grads, pos):
    n = len(bigs)
    c = pos[2]
    k_me = _chip_index(pos)
    halves = [jax.ShapeDtypeStruct(b.half_shape, BF16) for b in bigs]
    remote = [(1, lambda io, p, a=a, b=b: b.half_of_full(io[a], 1 - p[2]), lambda io, p, a=a: io[n + a])
              for a, b in enumerate(bigs)]
    from_sibling = _exchange("reduce_d2d", grads, halves, remote)
    chip_sum = [_ew(f"reduce_chip_sum_{a}", lambda u, v: (u.astype(F32) + v.astype(F32),),
                    [b.take_half(g, c), r], [BF16])[0] for a, (b, g, r) in enumerate(zip(bigs, grads, from_sibling))]
    thirds = [jax.ShapeDtypeStruct((3,) + b.piece_shape, BF16) for b in bigs]
    remote = []
    for a, b in enumerate(bigs):
        for j, mask in enumerate(CHIP_MASKS):
            remote.append((mask,
                           lambda io, p, a=a, b=b, mask=mask: b.piece_of_half(io[a], _chip_index(_flip(p, mask))),
                           lambda io, p, a=a, j=j: io[n + a].at[j]))
    from_chips = _exchange("reduce_ici", chip_sum, thirds, remote)
    pieces = [_ew(f"reduce_sum_{a}", lambda u, r0, r1, r2: (u.astype(F32) + r0.astype(F32) + r1.astype(F32) + r2.astype(F32),),
                  [b.take_piece_of_half(s, k_me), r[0], r[1], r[2]], [F32])[0]
              for a, (b, s, r) in enumerate(zip(bigs, chip_sum, from_chips))]
    shard_like = [jax.ShapeDtypeStruct(b.shard_as_pieces(jnp.zeros(b.shard_shape, F32)).shape, F32) for b in bigs]
    placed = [b.place_half(jnp.zeros(sl.shape, F32), p, c) for b, sl, p in zip(bigs, shard_like, pieces)]
    remote = []
    for a, b in enumerate(bigs):
        def mine(io, p, a=a, b=b):
            return b.half_of_shard(io[n + a], p[2])
        remote.append((1, mine, mine))
    out = _exchange("reduce_share_d2d", placed, shard_like, remote, aliases={a: a for a in range(n)})
    return [o.reshape(b.shard_shape) for o, b in zip(out, bigs)]


def _mm(name, a, b, *, nt, tm, tn, tk, epi, extras=(), extra_specs=(), out_shape, out_specs, vmem_mib=48):
    m, kdim = a.shape
    n = b.shape[0] if nt else b.shape[1]
    gm, gn, gk = m // tm, n // tn, kdim // tk
    a_spec = pl.BlockSpec((tm, tk), lambda j, i, k: (i, k))
    b_spec = pl.BlockSpec((tn, tk), lambda j, i, k: (j, k)) if nt else pl.BlockSpec((tk, tn), lambda j, i, k: (k, j))
    n_ex = len(extras)

    def body(a_ref, b_ref, *rest):
        ex, outs, acc = rest[:n_ex], rest[n_ex:-1], rest[-1]
        dot = _dot_nt if nt else _dot
        if gk == 1:
            acc[...] = dot(a_ref[...], b_ref[...])
            epi(acc, ex, outs)
        else:
            k = pl.program_id(2)

            @pl.when(k == 0)
            def _():
                acc[...] = jnp.zeros_like(acc)

            acc[...] += dot(a_ref[...], b_ref[...])

            @pl.when(k == gk - 1)
            def _():
                epi(acc, ex, outs)

    return pl.pallas_call(
        body, name=name, grid=(gn, gm, gk), in_specs=[a_spec, b_spec, *extra_specs], out_specs=tuple(out_specs),
        out_shape=tuple(out_shape), scratch_shapes=[pltpu.VMEM((tm, tn), F32)], compiler_params=_cp(3, vmem_mib),
    )(a, b, *extras)


def _mm_tn(name, a, b, out_dtype, *, tmo, tn, tt, more=(), vmem_mib=56):
    t, m = a.shape
    n = b.shape[1]
    gt = t // tt

    def body(a_ref, b_ref, *rest):
        o_ref, acc = rest[-2:]
        k = pl.program_id(2)

        @pl.when(k == 0)
        def _():
            acc[...] = _dot_tn(rest[0][...], rest[1][...]) if more else jnp.zeros_like(acc)

        acc[...] += _dot_tn(a_ref[...], b_ref[...])

        @pl.when(k == gt - 1)
        def _():
            o_ref[...] = acc[...].astype(o_ref.dtype)

    more_specs = [pl.BlockSpec((more[0].shape[0], tmo), lambda i, j, k: (0, i)),
                  pl.BlockSpec((more[1].shape[0], tn), lambda i, j, k: (0, j))] if more else []
    return pl.pallas_call(
        body, name=name, grid=(m // tmo, n // tn, gt),
        in_specs=[pl.BlockSpec((tt, tmo), lambda i, j, k: (k, i)), pl.BlockSpec((tt, tn), lambda i, j, k: (k, j))] + more_specs,
        out_specs=pl.BlockSpec((tmo, tn), lambda i, j, k: (i, j)), out_shape=jax.ShapeDtypeStruct((m, n), out_dtype),
        scratch_shapes=[pltpu.VMEM((tmo, tn), F32)], compiler_params=_cp(3, vmem_mib),
    )(a, b, *more)


def _row_spec(d):
    return pl.BlockSpec((1, d), lambda *_: (0, 0))


def _stat_spec(k, d):
    return pl.BlockSpec((k, 8, d), lambda *_: (0, 0, 0))


def _rope(z, cs, sn):
    first = (lax.broadcasted_iota(jnp.int32, (z.shape[0], 128), 1) % 32) < 16
    outs = []
    for j in range(z.shape[1] // 128):
        zc = z[:, 128 * j:128 * (j + 1)]
        partner = jnp.where(first, pltpu.roll(zc, 112, 1), pltpu.roll(zc, 16, 1))
        outs.append(zc * cs + partner * sn)
    return outs[0] if len(outs) == 1 else jnp.concatenate(outs, axis=1)


def _rope_tables(length, rotate):
    if not rotate:
        return jnp.ones((length, 128), F32), jnp.zeros((length, 128), F32)
    half = HEAD_DIM // 2
    inv_freq = ROPE_BASE ** (-jnp.arange(0, half, 2, dtype=F32) / half)
    t = jnp.arange(length)
    row = (t // GRID_W).astype(F32)
    col = (t % GRID_W).astype(F32)
    e = jnp.arange(128) % HEAD_DIM
    pos = jnp.where(e[None, :] < half, row[:, None], col[:, None])
    ang = pos * inv_freq[(e % half) % (half // 2)][None, :]
    first = ((e % half) < half // 2)[None, :]
    return jnp.cos(ang), jnp.where(first, -jnp.sin(ang), jnp.sin(ang))


def _mixer_in(name, x, nw, sh, sc, w_in, cos, sin):
    t, d = x.shape
    tm = _pick(t, 256, 128)
    n_in = w_in.shape[1]

    def body(x_ref, nw_ref, sh_ref, sc_ref, w_ref, cos_ref, sin_ref, h_ref, q_ref, k_ref, v_ref, u_ref):
        xf = x_ref[...]
        r = lax.rsqrt(jnp.mean(xf * xf, axis=-1, keepdims=True) + EPS)
        hb = (((xf * r) * nw_ref[...]) * (1.0 + sc_ref[...]) + sh_ref[...]).astype(BF16)
        h_ref[...] = hb
        p = _dot(hb, w_ref[...])
        cs, sn = cos_ref[...], sin_ref[...]
        q_ref[...] = _rope(p[:, :ATTN_WIDTH], cs, sn).astype(BF16)
        k_ref[...] = _rope(p[:, ATTN_WIDTH:ATTN_WIDTH + KV_WIDTH], cs, sn).astype(BF16)
        v_ref[...] = p[:, ATTN_WIDTH + KV_WIDTH:ATTN_WIDTH + 2 * KV_WIDTH].astype(BF16)
        u_ref[...] = p[:, ATTN_WIDTH + 2 * KV_WIDTH:]

    def tile(w):
        return pl.BlockSpec((tm, w), lambda i: (i, 0))

    return pl.pallas_call(
        body, name=name, grid=(t // tm,),
        in_specs=[tile(d), _row_spec(d), _row_spec(d), _row_spec(d), pl.BlockSpec((d, n_in), lambda i: (0, 0)),
                  tile(128), tile(128)],
        out_specs=(tile(d), tile(ATTN_WIDTH), tile(KV_WIDTH), tile(KV_WIDTH), tile(POOL_WIDTH)),
        out_shape=(jax.ShapeDtypeStruct((t, d), BF16), jax.ShapeDtypeStruct((t, ATTN_WIDTH), BF16),
                   jax.ShapeDtypeStruct((t, KV_WIDTH), BF16), jax.ShapeDtypeStruct((t, KV_WIDTH), BF16),
                   jax.ShapeDtypeStruct((t, POOL_WIDTH), F32)),
        compiler_params=_cp(1),
    )(x, nw, sh, sc, w_in, cos, sin)


def _attn_specs(nb, n_ctx):
    def blk(w, f):
        return pl.BlockSpec((BLOCK, w), lambda n: (f(n), 0))

    prev = lambda n: jnp.maximum(n - 1, 0)
    cur = lambda n: n
    nxt = lambda n: jnp.minimum(n + 1, nb - 1)
    kv = [blk(KV_WIDTH, prev), blk(KV_WIDTH, cur), blk(KV_WIDTH, nxt)]
    ctx = pl.BlockSpec((n_ctx, KV_WIDTH), lambda n: (0, 0))
    return [pl.BlockSpec(memory_space=pltpu.SMEM), blk(ATTN_WIDTH, cur)] + kv + kv + [ctx, ctx]


def _attn_mask(n, length, n_keys):
    row = lax.broadcasted_iota(jnp.int32, (GROUP * BLOCK, n_keys), 0) % BLOCK
    col = lax.broadcasted_iota(jnp.int32, (GROUP * BLOCK, n_keys), 1)
    kpos = (n - 1) * BLOCK + col
    return ((jnp.abs(col - BLOCK - row) <= BLOCK) & (kpos >= 0) & (kpos < length)) | (col >= 3 * BLOCK)


def _group_rows(block, g):
    return jnp.concatenate([block[:, HEAD_DIM * h:HEAD_DIM * (h + 1)] for h in range(GROUP * g, GROUP * (g + 1))], axis=0)


def _group_sink(sink_ref, g):
    head = lax.broadcasted_iota(jnp.int32, (GROUP * BLOCK, 1), 0) // BLOCK
    out = jnp.full((GROUP * BLOCK, 1), sink_ref[0, GROUP * g], F32)
    for j in range(1, GROUP):
        out = jnp.where(head == j, sink_ref[0, GROUP * g + j], out)
    return out


def _attn_fwd(q, k, v, kc, vc, sink):
    length = q.shape[0]
    nb = length // BLOCK
    n_ctx = kc.shape[0]
    n_keys = 3 * BLOCK + n_ctx

    def body(sink_ref, q_ref, kp, k0, kn, vp, v0, vn, kc_ref, vc_ref, o_ref):
        n = pl.program_id(0)
        valid = _attn_mask(n, length, n_keys)
        qb = q_ref[...]
        kall = jnp.concatenate([kp[...], k0[...], kn[...], kc_ref[...]], axis=0)
        vall = jnp.concatenate([vp[...], v0[...], vn[...], vc_ref[...]], axis=0)
        outs = []
        for g in range(N_KV_HEADS):
            lanes = slice(HEAD_DIM * g, HEAD_DIM * (g + 1))
            s = jnp.where(valid, _dot_nt(_group_rows(qb, g), kall[:, lanes]) * SCALE, NEG_INF)
            sk = _group_sink(sink_ref, g)
            m = jnp.maximum(jnp.max(s, axis=-1, keepdims=True), sk)
            e = jnp.exp(s - m)
            den = jnp.sum(e, axis=-1, keepdims=True) + jnp.exp(sk - m)
            o = _dot(e.astype(BF16), vall[:, lanes]) / den
            outs += [o[BLOCK * j:BLOCK * (j + 1)] for j in range(GROUP)]
        o_ref[...] = jnp.concatenate(outs, axis=1).astype(BF16)

    return pl.pallas_call(
        body, name="attn_fwd", grid=(nb,), in_specs=_attn_specs(nb, n_ctx),
        out_specs=pl.BlockSpec((BLOCK, ATTN_WIDTH), lambda n: (n, 0)),
        out_shape=jax.ShapeDtypeStruct((length, ATTN_WIDTH), BF16), compiler_params=_cp(1),
    )(sink, q, k, k, k, v, v, v, kc, vc)


def _attn_bwd(q, k, v, kc, vc, sink, dmix):
    length = q.shape[0]
    nb = length // BLOCK
    n_ctx = kc.shape[0]
    n_keys = 3 * BLOCK + n_ctx

    def body(sink_ref, q_ref, kp, k0, kn, vp, v0, vn, kc_ref, vc_ref, do_ref,
             dq_ref, dkp_ref, dvp_ref, dkc_ref, dvc_ref, dsink_ref):
        n = pl.program_id(0)

        @pl.when(n == 0)
        def _():
            dkc_ref[...] = jnp.zeros_like(dkc_ref)
            dvc_ref[...] = jnp.zeros_like(dvc_ref)
            dsink_ref[...] = jnp.zeros_like(dsink_ref)

        valid = _attn_mask(n, length, n_keys)
        qb, dob = q_ref[...], do_ref[...]
        kall = jnp.concatenate([kp[...], k0[...], kn[...], kc_ref[...]], axis=0)
        vall = jnp.concatenate([vp[...], v0[...], vn[...], vc_ref[...]], axis=0)
        srow = lax.broadcasted_iota(jnp.int32, (8, 128), 0)
        slane = lax.broadcasted_iota(jnp.int32, (8, 128), 1)
        dqs, dks, dvs = [], [], []
        dsink = jnp.zeros((8, 128), F32)
        for g in range(N_KV_HEADS):
            lanes = slice(HEAD_DIM * g, HEAD_DIM * (g + 1))
            kg, vg = kall[:, lanes], vall[:, lanes]
            qg, dog = _group_rows(qb, g), _group_rows(dob, g)
            s = jnp.where(valid, _dot_nt(qg, kg) * SCALE, NEG_INF)
            sk = _group_sink(sink_ref, g)
            m = jnp.maximum(jnp.max(s, axis=-1, keepdims=True), sk)
            e = jnp.exp(s - m)
            inv = 1.0 / (jnp.sum(e, axis=-1, keepdims=True) + jnp.exp(sk - m))
            p = e * inv
            dp = _dot_nt(dog, vg)
            delta = jnp.sum(p * dp, axis=-1, keepdims=True)
            ds = (p * (dp - delta) * SCALE).astype(BF16)
            dq = _dot(ds, kg)
            dqs += [dq[BLOCK * j:BLOCK * (j + 1)] for j in range(GROUP)]
            dks.append(_dot_tn(ds, qg))
            dvs.append(_dot_tn(p.astype(BF16), dog))
            d_sink = jnp.exp(sk - m) * inv * delta
            for j in range(GROUP):
                total = -jnp.sum(d_sink[BLOCK * j:BLOCK * (j + 1)], axis=0, keepdims=True)
                dsink = dsink + jnp.where((srow == 0) & (slane == GROUP * g + j), total, 0.0)
        dq_ref[...] = jnp.concatenate(dqs, axis=1)
        dk = jnp.concatenate(dks, axis=1)
        dv = jnp.concatenate(dvs, axis=1)
        for j in range(3):
            dkp_ref[0, j] = dk[BLOCK * j:BLOCK * (j + 1)]
            dvp_ref[0, j] = dv[BLOCK * j:BLOCK * (j + 1)]
        dkc_ref[...] += dk[3 * BLOCK:]
        dvc_ref[...] += dv[3 * BLOCK:]
        dsink_ref[...] += dsink

    part = pl.BlockSpec((1, 3, BLOCK, KV_WIDTH), lambda n: (n, 0, 0, 0))
    ctx = pl.BlockSpec((n_ctx, KV_WIDTH), lambda n: (0, 0))
    return pl.pallas_call(
        body, name="attn_bwd", grid=(nb,),
        in_specs=_attn_specs(nb, n_ctx) + [pl.BlockSpec((BLOCK, ATTN_WIDTH), lambda n: (n, 0))],
        out_specs=(pl.BlockSpec((BLOCK, ATTN_WIDTH), lambda n: (n, 0)), part, part, ctx, ctx,
                   pl.BlockSpec((8, 128), lambda n: (0, 0))),
        out_shape=(jax.ShapeDtypeStruct((length, ATTN_WIDTH), F32),
                   jax.ShapeDtypeStruct((nb, 3, BLOCK, KV_WIDTH), F32), jax.ShapeDtypeStruct((nb, 3, BLOCK, KV_WIDTH), F32),
                   jax.ShapeDtypeStruct((n_ctx, KV_WIDTH), F32), jax.ShapeDtypeStruct((n_ctx, KV_WIDTH), F32),
                   jax.ShapeDtypeStruct((8, 128), F32)),
        compiler_params=_cp(1),
    )(sink, q, k, k, k, v, v, v, kc, vc, dmix)


def _assemble_dp(dq, dkp, dvp, du, cos, sin):
    length = dq.shape[0]
    nb = length // BLOCK

    def body(dq_ref, dka, dkb, dkc, dva, dvb, dvc, du_ref, cos_ref, sin_ref, o_ref):
        n = pl.program_id(0)
        has_next = (n + 1 < nb).astype(F32)
        has_prev = (n > 0).astype(F32)
        cs, sn = cos_ref[...], -sin_ref[...]
        dk = dka[0, 0] * has_next + dkb[0, 0] + dkc[0, 0] * has_prev
        dv = dva[0, 0] * has_next + dvb[0, 0] + dvc[0, 0] * has_prev
        o_ref[:, :ATTN_WIDTH] = _rope(dq_ref[...], cs, sn).astype(BF16)
        o_ref[:, ATTN_WIDTH:ATTN_WIDTH + KV_WIDTH] = _rope(dk, cs, sn).astype(BF16)
        o_ref[:, ATTN_WIDTH + KV_WIDTH:ATTN_WIDTH + 2 * KV_WIDTH] = dv.astype(BF16)
        o_ref[:, ATTN_WIDTH + 2 * KV_WIDTH:] = du_ref[...]

    def part(slot, f):
        return pl.BlockSpec((1, 1, BLOCK, KV_WIDTH), lambda n: (f(n), slot, 0, 0))

    parts = [part(0, lambda n: jnp.minimum(n + 1, nb - 1)), part(1, lambda n: n), part(2, lambda n: jnp.maximum(n - 1, 0))]

    def tile(w):
        return pl.BlockSpec((BLOCK, w), lambda n: (n, 0))

    width = ATTN_WIDTH + 2 * KV_WIDTH + POOL_WIDTH
    return pl.pallas_call(
        body, name="assemble_dp", grid=(nb,),
        in_specs=[tile(ATTN_WIDTH)] + parts + parts + [tile(POOL_WIDTH), tile(128), tile(128)],
        out_specs=tile(width), out_shape=jax.ShapeDtypeStruct((length, width), BF16), compiler_params=_cp(1),
    )(dq, dkp, dkp, dkp, dvp, dvp, dvp, du, cos, sin)


def _shift_rows(e, s):
    n = e.shape[0]
    return e if s % n == 0 else pltpu.roll(e, (-s) % n, 0)


def _window_sum(e, w, first):
    s, n = e, 1
    while n < w:
        s = s + _shift_rows(s, n)
        n *= 2
    return _shift_rows(s, first)


def _pool_geometry(i, tm, length):
    pos = i * tm - HALO + lax.broadcasted_iota(jnp.int32, (tm + 2 * HALO, 1), 0)
    inside = (pos >= 0) & (pos < length)
    inv_counts = []
    for w in POOL_WINDOWS:
        lo = jnp.clip(pos - w // 2, 0, length)
        hi = jnp.clip(pos - w // 2 + w, 0, length)
        inv_counts.append(1.0 / jnp.maximum(hi - lo, 1).astype(F32))
    return inside, inv_counts


def _halo_specs(tm, width, length, col=0):
    per = tm // HALO
    last = length // HALO - 1
    return [pl.BlockSpec((HALO, width), lambda i: (jnp.maximum(i * per - 1, 0), col)),
            pl.BlockSpec((tm, width), lambda i: (i, col)),
            pl.BlockSpec((HALO, width), lambda i: (jnp.minimum((i + 1) * per, last), col))]


def _pooled(ext, inv_counts, tm):
    outs = []
    for g, w in enumerate(POOL_WINDOWS):
        e = ext[:, POOL_GROUP_DIM * g:POOL_GROUP_DIM * (g + 1)]
        mean = _window_sum(e, w, -(w // 2)) * inv_counts[g]
        outs.append((mean - e)[HALO:HALO + tm])
    return outs


def _pool_fwd(u, pool_w, pool_scale):
    length = u.shape[0]
    tm = _pick(length, 256, 128)

    def body(up, u0, un, w_ref, sc_ref, o_ref):
        inside, inv_counts = _pool_geometry(pl.program_id(0), tm, length)
        ext = jnp.where(inside, jnp.concatenate([up[...], u0[...], un[...]], axis=0), 0.0)
        pooled = _pooled(ext, inv_counts, tm)
        mixed = [_dot(pooled[g].astype(BF16), w_ref[g]) for g in range(len(POOL_WINDOWS))]
        o_ref[...] = (jnp.concatenate(mixed, axis=1) * sc_ref[...]).astype(BF16)

    return pl.pallas_call(
        body, name="pool_fwd", grid=(length // tm,),
        in_specs=_halo_specs(tm, POOL_WIDTH, length) + [pl.BlockSpec(pool_w.shape, lambda i: (0, 0, 0)), _row_spec(POOL_WIDTH)],
        out_specs=pl.BlockSpec((tm, POOL_WIDTH), lambda i: (i, 0)),
        out_shape=jax.ShapeDtypeStruct((length, POOL_WIDTH), BF16), compiler_params=_cp(1),
    )(u, u, u, pool_w, pool_scale)


def _pool_bwd(u, dmix, pool_w, pool_scale):
    length = u.shape[0]
    tm = _pick(length, 256, 128)
    n_g = len(POOL_WINDOWS)

    def body(up, u0, un, dp_, d0, dn_, w_ref, sc_ref, du_ref, dw_ref, dsc_ref):
        i = pl.program_id(0)

        @pl.when(i == 0)
        def _():
            dw_ref[...] = jnp.zeros_like(dw_ref)
            dsc_ref[...] = jnp.zeros_like(dsc_ref)

        inside, inv_counts = _pool_geometry(i, tm, length)
        ext = jnp.where(inside, jnp.concatenate([up[...], u0[...], un[...]], axis=0), 0.0)
        dext = jnp.where(inside, jnp.concatenate([dp_[...], d0[...], dn_[...]], axis=0).astype(F32), 0.0)
        dmixed = (dext * sc_ref[...]).astype(BF16)
        pooled = _pooled(ext, inv_counts, tm)
        dus, dscs = [], []
        for g, w in enumerate(POOL_WINDOWS):
            lanes = slice(POOL_GROUP_DIM * g, POOL_GROUP_DIM * (g + 1))
            dpooled = _dot_nt(dmixed[:, lanes], w_ref[g])
            spread = _window_sum(dpooled * inv_counts[g], w, -(w // 2 - 1))
            dus.append((spread - dpooled)[HALO:HALO + tm])
            pb = pooled[g].astype(BF16)
            dw_ref[g] += _dot_tn(pb, dmixed[HALO:HALO + tm, lanes])
            prod = dext[HALO:HALO + tm, lanes] * _dot(pb, w_ref[g])
            dscs.append(_fold8(prod))
        du_ref[...] = jnp.concatenate(dus, axis=1).astype(BF16)
        dsc_ref[...] += jnp.concatenate(dscs, axis=1)

    return pl.pallas_call(
        body, name="pool_bwd", grid=(length // tm,),
        in_specs=_halo_specs(tm, POOL_WIDTH, length) + _halo_specs(tm, POOL_WIDTH, length, col=1)
        + [pl.BlockSpec(pool_w.shape, lambda i: (0, 0, 0)), _row_spec(POOL_WIDTH)],
        out_specs=(pl.BlockSpec((tm, POOL_WIDTH), lambda i: (i, 0)), pl.BlockSpec((n_g, POOL_GROUP_DIM, POOL_GROUP_DIM), lambda i: (0, 0, 0)),
                   pl.BlockSpec((8, POOL_WIDTH), lambda i: (0, 0))),
        out_shape=(jax.ShapeDtypeStruct((length, POOL_WIDTH), BF16), jax.ShapeDtypeStruct((n_g, POOL_GROUP_DIM, POOL_GROUP_DIM), F32),
                   jax.ShapeDtypeStruct((8, POOL_WIDTH), F32)),
        compiler_params=_cp(1),
    )(u, u, u, dmix, dmix, dmix, pool_w, pool_scale)


def _mixer_out(mix, w_out, x, g_a, nmw, sh_m, sc_m):
    t, d = x.shape
    tm = _pick(t, 256, 128)

    def epi(acc, ex, outs):
        x_ref, ga, nw, sh, sc = ex
        x1_ref, mo_ref, hm_ref = outs

        def rows(rs):
            mo = acc[rs, :]
            x1 = x_ref[rs, :] + ga[...] * mo
            x1_ref[rs, :] = x1
            mo_ref[rs, :] = mo.astype(BF16)
            r = lax.rsqrt(jnp.mean(x1 * x1, axis=-1, keepdims=True) + EPS)
            hm_ref[rs, :] = (((x1 * r) * nw[...]) * (1.0 + sc[...]) + sh[...]).astype(BF16)

        _row_loop(tm, rows)

    tile = pl.BlockSpec((tm, d), lambda j, i, k: (i, 0))
    return _mm("mixer_out", mix, w_out, nt=False, tm=tm, tn=d, tk=mix.shape[1], epi=epi,
               extras=(x, g_a, nmw, sh_m, sc_m), extra_specs=[tile] + [_row_spec(d)] * 4,
               out_shape=(jax.ShapeDtypeStruct((t, d), F32), jax.ShapeDtypeStruct((t, d), BF16), jax.ShapeDtypeStruct((t, d), BF16)),
               out_specs=(tile, tile, tile))


def _mlp_up(hm, w_up):
    t, d = hm.shape
    tm = _pick(t, 512, 256, 128)
    tn = 2048

    def epi(acc, ex, outs):
        outs[0][...] = jnp.square(jnp.maximum(acc[...], 0.0)).astype(BF16)

    return _mm("mlp_up", hm, w_up, nt=False, tm=tm, tn=tn, tk=d, epi=epi,
               out_shape=(jax.ShapeDtypeStruct((t, w_up.shape[1]), BF16),),
               out_specs=(pl.BlockSpec((tm, tn), lambda j, i, k: (i, j)),))[0]


def _mm_f32(name, a, b, *, nt):
    m, kdim = a.shape
    n = b.shape[0] if nt else b.shape[1]
    tm, tn = _pick(m, 1024, 512, 256, 128), _pick(n, 1024)

    def epi(acc, ex, outs):
        outs[0][...] = acc[...]

    return _mm(name, a, b, nt=nt, tm=tm, tn=tn, tk=_pick(kdim, 2048), epi=epi,
               out_shape=(jax.ShapeDtypeStruct((m, n), F32),), out_specs=(pl.BlockSpec((tm, tn), lambda j, i, k: (i, j)),))[0]


def _rows_call(name, rows_fn, tiles, vecs, out_shape, n_stats):
    t, d = tiles[0].shape
    tm = _pick(t, 256, 128)
    n_t, n_v = len(tiles), len(vecs)

    def body(*refs):
        st_ref = refs[-1]

        @pl.when(pl.program_id(0) == 0)
        def _():
            st_ref[...] = jnp.zeros_like(st_ref)

        _row_loop(tm, lambda rs: rows_fn(rs, refs[:n_t], refs[n_t:n_t + n_v], refs[n_t + n_v:-1], st_ref))

    tile = pl.BlockSpec((tm, d), lambda i: (i, 0))
    return pl.pallas_call(
        body, name=name, grid=(t // tm,), in_specs=[tile] * n_t + [_row_spec(d)] * n_v,
        out_specs=tuple([tile] * len(out_shape)) + (_stat_spec(n_stats, d),),
        out_shape=tuple(out_shape) + (jax.ShapeDtypeStruct((n_stats, 8, d), F32),), compiler_params=_cp(1),
    )(*tiles, *vecs)


def _loss_rows(dn, x1, target, g_m, fw):
    t, d = x1.shape

    def rows_fn(rs, tiles, vecs, outs, st_ref):
        dn_ref, x1_ref, t_ref = tiles
        gm, fw_ref = vecs
        dx2_ref, ddn_ref = outs
        dnv = dn_ref[rs, :]
        x2 = x1_ref[rs, :] + gm[...] * dnv
        r = lax.rsqrt(jnp.mean(x2 * x2, axis=-1, keepdims=True) + EPS)
        xh = x2 * r
        diff = xh * fw_ref[...] - t_ref[rs, :]
        dy = diff * (1.0 / d)
        dxh = dy * fw_ref[...]
        dx2 = r * (dxh - xh * jnp.mean(dxh * xh, axis=-1, keepdims=True))
        dx2_ref[rs, :] = dx2
        ddn_ref[rs, :] = (dx2 * gm[...]).astype(BF16)
        st_ref[0] += _fold8(diff * diff)
        st_ref[1] += _fold8(dy * xh)
        st_ref[2] += _fold8(dx2 * dnv)

    return _rows_call("loss_rows", rows_fn, (dn, x1, target), (g_m, fw),
                      (jax.ShapeDtypeStruct((t, d), F32), jax.ShapeDtypeStruct((t, d), BF16)), 3)


def _mlp_dx_rows(dhm, x1, dx2, mo, nmw, sc_m, g_a):
    t, d = x1.shape

    def rows_fn(rs, tiles, vecs, outs, st_ref):
        dh_ref, x1_ref, dx2_ref, mo_ref = tiles
        nw, sc, ga = vecs
        dx1_ref, dmi_ref = outs
        dx1 = _norm_bwd_rows(dh_ref[rs, :], x1_ref[rs, :], nw[...], sc[...], st_ref) + dx2_ref[rs, :]
        dx1_ref[rs, :] = dx1
        dmi_ref[rs, :] = (dx1 * ga[...]).astype(BF16)
        st_ref[3] += _fold8(dx1 * mo_ref[rs, :].astype(F32))

    return _rows_call("mlp_dx_rows", rows_fn, (dhm, x1, dx2, mo), (nmw, sc_m, g_a),
                      (jax.ShapeDtypeStruct((t, d), F32), jax.ShapeDtypeStruct((t, d), BF16)), 4)


def _mlp_dact(ddn, w_down, act):
    t, d = ddn.shape
    tm = _pick(t, 512, 256, 128)
    tn = 2048

    def epi(acc, ex, outs):
        outs[0][...] = (acc[...] * (2.0 * jnp.sqrt(ex[0][...].astype(F32)))).astype(BF16)

    tile = pl.BlockSpec((tm, tn), lambda j, i, k: (i, j))
    return _mm("mlp_dact", ddn, w_down, nt=True, tm=tm, tn=tn, tk=d, epi=epi, extras=(act,), extra_specs=[tile],
               out_shape=(jax.ShapeDtypeStruct(act.shape, BF16),), out_specs=(tile,))[0]


def _norm_bwd_rows(dh, xv, nw, sc, st_ref):
    r = lax.rsqrt(jnp.mean(xv * xv, axis=-1, keepdims=True) + EPS)
    xh = xv * r
    dy = dh * (1.0 + sc)
    st_ref[0] += _fold8(dh)
    st_ref[1] += _fold8(dh * (xh * nw))
    st_ref[2] += _fold8(dy * xh)
    dxh = dy * nw
    return r * (dxh - xh * jnp.mean(dxh * xh, axis=-1, keepdims=True))


def _mixer_dmix(dmi, w_out):
    t, d = dmi.shape
    tm = _pick(t, 512, 256, 128)

    def epi(acc, ex, outs):
        outs[0][...] = acc[...].astype(BF16)

    n = w_out.shape[0]
    return _mm("mixer_dmix", dmi, w_out, nt=True, tm=tm, tn=n, tk=d, epi=epi,
               out_shape=(jax.ShapeDtypeStruct((t, n), BF16),), out_specs=(pl.BlockSpec((tm, n), lambda j, i, k: (i, 0)),))[0]


def _mixer_dx(name, dp, w_in, x, dx1, naw, sc_a):
    t, d = x.shape
    tm = _pick(t, 256, 128)

    def epi(acc, ex, outs):
        x_ref, dx1_ref, nw, sc = ex
        gx_ref, st_ref = outs

        @pl.when(pl.program_id(1) == 0)
        def _():
            st_ref[...] = jnp.zeros_like(st_ref)

        def rows(rs):
            gx_ref[rs, :] = _norm_bwd_rows(acc[rs, :], x_ref[rs, :], nw[...], sc[...], st_ref) + dx1_ref[rs, :]

        _row_loop(tm, rows)

    tile = pl.BlockSpec((tm, d), lambda j, i, k: (i, 0))
    return _mm(name, dp, w_in, nt=True, tm=tm, tn=d, tk=dp.shape[1], epi=epi,
               extras=(x, dx1, naw, sc_a), extra_specs=[tile, tile, _row_spec(d), _row_spec(d)],
               out_shape=(jax.ShapeDtypeStruct((t, d), F32), jax.ShapeDtypeStruct((3, 8, d), F32)),
               out_specs=(tile, _stat_spec(3, d)))


def _silu(v):
    return v / (1.0 + jnp.exp(-v))


def _ada_fwd(cond, w_ada, b_ada):
    d, n = w_ada.shape
    tn = 512

    def body(c_ref, w_ref, b_ref, o_ref):
        o_ref[...] = _dot(_silu(c_ref[...]).astype(BF16), w_ref[...].astype(BF16)) + b_ref[...]

    return pl.pallas_call(
        body, name="ada_fwd", grid=(n // tn,),
        in_specs=[pl.BlockSpec(cond.shape, lambda j: (0, 0)), pl.BlockSpec((d, tn), lambda j: (0, j)), pl.BlockSpec((1, tn), lambda j: (0, j))],
        out_specs=pl.BlockSpec((cond.shape[0], tn), lambda j: (0, j)), out_shape=jax.ShapeDtypeStruct((cond.shape[0], n), F32),
        compiler_params=_cp(1),
    )(cond, w_ada, b_ada)


def _adamw_math(w, g, m, v):
    m = ADAM_B1 * m + (1.0 - ADAM_B1) * g
    v = ADAM_B2 * v + (1.0 - ADAM_B2) * jnp.square(g)
    m_hat = m / (1.0 - ADAM_B1 ** ADAM_STEP)
    v_hat = v / (1.0 - ADAM_B2 ** ADAM_STEP)
    return -ADAM_LR * (m_hat / (jnp.sqrt(v_hat) + ADAM_EPS) + ADAM_WD * w), m, v


def _ada_bwd(cond, dm, w_ada, m_ada, v_ada):
    d, n = w_ada.shape
    tn = 256
    rows = cond.shape[0]

    def body(c_ref, dm_ref, w_ref, m_ref, v_ref, g_ref, dl_ref, nm_ref, nv_ref, pc_ref):
        @pl.when(pl.program_id(0) == 0)
        def _():
            pc_ref[...] = jnp.zeros_like(pc_ref)

        dmb = dm_ref[...].astype(BF16)
        w = w_ref[...]
        g = _dot_tn(_silu(c_ref[...]).astype(BF16), dmb)
        g_ref[...] = g
        dl_ref[...], nm_ref[...], nv_ref[...] = _adamw_math(w, g, m_ref[...], v_ref[...])
        pc_ref[...] += _dot_nt(dm_ref[8:16, :].astype(BF16), w.astype(BF16))

    tile = pl.BlockSpec((d, tn), lambda j: (0, j))
    like = jax.ShapeDtypeStruct((d, n), F32)
    return pl.pallas_call(
        body, name="ada_bwd", grid=(n // tn,),
        in_specs=[pl.BlockSpec((rows, d), lambda j: (0, 0)), pl.BlockSpec((rows, tn), lambda j: (0, j)), tile, tile, tile],
        out_specs=(tile, tile, tile, tile, pl.BlockSpec((8, d), lambda j: (0, 0))),
        out_shape=(like, like, like, like, jax.ShapeDtypeStruct((8, d), F32)), compiler_params=_cp(1),
    )(cond, dm, w_ada, m_ada, v_ada)


def _adamw(name, w, g, m, v):
    return _ew(name, _adamw_math, [w, g, m, v], [F32, F32, F32])


def _colsum(st):
    return jnp.sum(st, axis=1)


def kernel(x, c, ctx, c_ctx, norm_attn_w, norm_mlp_w, w_ada, b_ada, w_in, attn_sink, pool_w, pool_scale, w_out, w_mlp_up, w_mlp_down, final_norm_w, loss_target, m_c_ctx, m_norm_attn_w, m_norm_mlp_w, m_w_ada, m_b_ada, m_w_in, m_attn_sink, m_pool_w, m_pool_scale, m_w_out, m_w_mlp_up, m_w_mlp_down, m_final_norm_w, v_c_ctx, v_norm_attn_w, v_norm_mlp_w, v_w_ada, v_b_ada, v_w_in, v_attn_sink, v_pool_w, v_pool_scale, v_w_out, v_w_mlp_up, v_w_mlp_down, v_final_norm_w):
    length, d = x.shape[1], x.shape[2]
    n_ctx = ctx.shape[1]
    pos = (lax.axis_index("x"), lax.axis_index("y"), lax.axis_index("c"))
    me, chip = _dev_index(pos), _chip_index(pos)
    xs, tgt, cx = x.reshape(length, d), loss_target.reshape(length, d), ctx.reshape(n_ctx, d)
    n_ada = w_ada.shape[2]

    c_all = _allgather8("gather_c", jnp.pad(c, ((0, 7), (0, 0))))
    cond = jnp.concatenate([c_all[:, 0, :], jnp.pad(c_ctx[None, :], ((0, 7), (0, 0)))], axis=0)
    b_shard = lax.dynamic_slice_in_dim(b_ada, chip * n_ada, n_ada, axis=1)
    mod_all = _allgather8("gather_mod", _ada_fwd(cond, w_ada[0], b_shard))
    mod = jnp.concatenate([mod_all[0], mod_all[2], mod_all[4], mod_all[6]], axis=1)
    mine = lax.dynamic_slice_in_dim(mod, me, 1, axis=0)
    sh_a, sc_a, g_a, sh_m, sc_m, g_m = [mine[:, d * i:d * (i + 1)] for i in range(6)]
    csh_a, csc_a = mod[8:9, :d], mod[8:9, d:2 * d]

    bigs = [_Big("col", w_in.shape[1:]), _Big("pool", pool_w.shape[1:]), _Big("row", w_out.shape[1:]),
            _Big("col", w_mlp_up.shape[1:]), _Big("row", w_mlp_down.shape[1:])]
    shards = [w_in[0], pool_w[0], w_out[0], w_mlp_up[0], w_mlp_down[0]]
    win_b, pw_b, wout_b, wup_b, wdn_b = _gather_weights(bigs, [s.astype(BF16) for s in shards], pos)
    wout_b = wout_b.reshape(-1, d)
    wdn_b = wdn_b.reshape(-1, d)

    cos, sin = _rope_tables(length, True)
    one, zero = _rope_tables(n_ctx, False)
    h, q, k, v, u = _mixer_in("mixer_in", xs, norm_attn_w, sh_a, sc_a, win_b, cos, sin)
    hc, _, kc, vc, _ = _mixer_in("mixer_in_ctx", cx, norm_attn_w, csh_a, csc_a, win_b, one, zero)
    attn = _attn_fwd(q, k, v, kc, vc, attn_sink)
    pooled = _pool_fwd(u, pw_b, pool_scale)
    mix = jnp.concatenate([attn, pooled], axis=1)
    x1, mo, hm = _mixer_out(mix, wout_b, xs, g_a, norm_mlp_w, sh_m, sc_m)
    act = _mlp_up(hm, wup_b)
    dn = _mm_f32("mlp_down", act, wdn_b, nt=False)
    dx2, ddn, st_loss = _loss_rows(dn, x1, tgt, g_m, final_norm_w[None, :])
    st_loss = _colsum(st_loss)
    loss = lax.psum(0.5 / d * jnp.sum(st_loss[0]), ("x", "y", "c"))

    tt = _pick(length, 2048, 1024, 512, 256, 128)
    g_wdn = _mm_tn("grad_w_down", act, ddn, BF16, tmo=1024, tn=d, tt=tt)
    dup = _mlp_dact(ddn, wdn_b, act)
    g_wup = _mm_tn("grad_w_up", hm, dup, BF16, tmo=d, tn=1024, tt=tt)
    dhm = _mm_f32("mlp_dhm", dup, wup_b, nt=True)
    dx1, dmi, st_mlp = _mlp_dx_rows(dhm, x1, dx2, mo, norm_mlp_w, sc_m, g_a)
    st_mlp = _colsum(st_mlp)
    g_wout = _mm_tn("grad_w_out", mix, dmi, BF16, tmo=1024, tn=d, tt=tt)
    dmix = _mixer_dmix(dmi, wout_b)
    dq, dkp, dvp, dkc, dvc, dsink = _attn_bwd(q, k, v, kc, vc, attn_sink, dmix)
    du, g_pw, st_pool = _pool_bwd(u, dmix, pw_b, pool_scale)
    dp = _assemble_dp(dq, dkp, dvp, du, cos, sin)
    dpc = jnp.concatenate([jnp.zeros((n_ctx, ATTN_WIDTH), BF16), dkc.astype(BF16), dvc.astype(BF16),
                           jnp.zeros((n_ctx, POOL_WIDTH), BF16)], axis=1)
    grad_x, st_mix = _mixer_dx("mixer_dx", dp, win_b, xs, dx1, norm_attn_w, sc_a)
    _, st_ctx = _mixer_dx("mixer_dx_ctx", dpc, win_b, cx, jnp.zeros((n_ctx, d), F32), norm_attn_w, csc_a)
    st_mix, st_ctx = _colsum(st_mix), _colsum(st_ctx)
    g_win = _mm_tn("grad_w_in", h, dp, BF16, tmo=d, tn=dp.shape[1] // 2, tt=_pick(length, 1024, 512, 256, 128), more=(hc, dpc))

    zrow = jnp.zeros((d,), F32)
    pad = lambda a: jnp.pad(a, (0, d - a.shape[0]))
    mine_rows = [st_mix[0], st_mix[1], st_mlp[3], st_mlp[0], st_mlp[1], st_loss[2],
                 st_ctx[0], st_ctx[1],
                 st_mix[2] + st_ctx[2], st_mlp[2], st_loss[1],
                 pad(jnp.sum(st_pool, axis=0)), pad(dsink[0, :N_Q_HEADS])] + [zrow] * 3
    small_all = _allgather8("gather_small", jnp.concatenate(mine_rows).reshape(len(mine_rows), d))
    small = small_all[0]
    for i in range(1, 8):
        small = small + small_all[i]
    dm_rows = small_all[:, 0:6, :].reshape(8, 6 * d)
    dm_ctx = jnp.concatenate([small[6], small[7], jnp.zeros((4 * d,), F32)])[None, :]
    dm = jnp.concatenate([dm_rows, jnp.pad(dm_ctx, ((0, 7), (0, 0)))], axis=0)
    g_bada = jnp.sum(dm[:9], axis=0, keepdims=True)
    dm_shard = lax.dynamic_slice_in_dim(dm, chip * n_ada, n_ada, axis=1)
    g_wada, dl_wada, nm_wada, nv_wada, part_cctx = _ada_bwd(cond, dm_shard, w_ada[0], m_w_ada[0], v_w_ada[0])
    cctx_all = _allgather8("gather_cctx", part_cctx)
    dsilu_in = cctx_all[0, 0] + cctx_all[2, 0] + cctx_all[4, 0] + cctx_all[6, 0]
    sig = 1.0 / (1.0 + jnp.exp(-c_ctx))
    g_cctx = dsilu_in * (sig * (1.0 + c_ctx * (1.0 - sig)))

    g_shards = _reduce_grads(bigs, [g_win, g_pw.astype(BF16), g_wout.reshape(bigs[2].full_shape),
                                    g_wup, g_wdn.reshape(bigs[4].full_shape)], pos)
    big_w = [w_in, pool_w, w_out, w_mlp_up, w_mlp_down]
    big_m = [m_w_in, m_pool_w, m_w_out, m_w_mlp_up, m_w_mlp_down]
    big_v = [v_w_in, v_pool_w, v_w_out, v_w_mlp_up, v_w_mlp_down]
    big_names = ["w_in", "pool_w", "w_out", "w_mlp_up", "w_mlp_down"]
    res = {}
    for nm, w_, g_, m_, v_ in zip(big_names, big_w, g_shards, big_m, big_v):
        g_ = g_.reshape(w_.shape)
        res[nm] = (g_,) + tuple(_adamw("adamw_" + nm, w_, g_, m_, v_))
    res["w_ada"] = (g_wada[None], dl_wada[None], nm_wada[None], nv_wada[None])

    def pack(cc, na, nm_, ba, sk, ps, fn):
        flat = [cc.reshape(-1), na.reshape(-1), nm_.reshape(-1), ba.reshape(-1), pad(sk.reshape(-1)), pad(ps.reshape(-1)),
                fn.reshape(-1), jnp.zeros((4 * d,), F32)]
        return jnp.concatenate(flat).reshape(16, d)

    w_s = pack(c_ctx, norm_attn_w, norm_mlp_w, b_ada, attn_sink, pool_scale, final_norm_w)
    m_s = pack(m_c_ctx, m_norm_attn_w, m_norm_mlp_w, m_b_ada, m_attn_sink, m_pool_scale, m_final_norm_w)
    v_s = pack(v_c_ctx, v_norm_attn_w, v_norm_mlp_w, v_b_ada, v_attn_sink, v_pool_scale, v_final_norm_w)
    g_s = pack(g_cctx, small[8], small[9], g_bada, small[12][:N_Q_HEADS], small[11][:POOL_WIDTH], small[10])
    small_out = [g_s] + _adamw("adamw_small", w_s, g_s, m_s, v_s)

    def unpack(p):
        return {"c_ctx": p[0], "norm_attn_w": p[1:2], "norm_mlp_w": p[2:3], "b_ada": p[3:9].reshape(1, 6 * d),
                "attn_sink": p[9:10, :N_Q_HEADS], "pool_scale": p[10:11, :POOL_WIDTH], "final_norm_w": p[11]}

    small_res = [unpack(p) for p in small_out]
    order = ["c_ctx", "norm_attn_w", "norm_mlp_w", "w_ada", "b_ada", "w_in", "attn_sink", "pool_w", "pool_scale",
             "w_out", "w_mlp_up", "w_mlp_down", "final_norm_w"]
    outs = [loss, grad_x.reshape(x.shape)]
    for kind in range(4):
        for nm in order:
            outs.append(res[nm][kind] if nm in res else small_res[kind][nm])
    return tuple(outs)
```

```python
import functools

import jax
import jax.numpy as jnp
from jax import lax
from jax.experimental import pallas as pl
from jax.experimental.pallas import tpu as pltpu

F32 = jnp.float32
BF16 = jnp.bfloat16
EPS = 1e-6
NEG_INF = -1e30
HEAD_DIM = 64
N_Q_HEADS = 16
N_KV_HEADS = 4
GROUP = N_Q_HEADS // N_KV_HEADS
ATTN_WIDTH = N_Q_HEADS * HEAD_DIM
KV_WIDTH = N_KV_HEADS * HEAD_DIM
POOL_WINDOWS = (2, 4, 8, 16)
POOL_GROUP_DIM = 256
POOL_WIDTH = len(POOL_WINDOWS) * POOL_GROUP_DIM
BLOCK = 128
GRID_W = 64
ROPE_BASE = 10000.0
SCALE = HEAD_DIM ** -0.5
HALO = 16
ROWS = 64
ADAM_LR, ADAM_B1, ADAM_B2, ADAM_EPS, ADAM_WD, ADAM_STEP = 0.001, 0.9, 0.999, 1e-08, 0.01, 10
MESH = pl.DeviceIdType.MESH
MIB = 1024 * 1024
ANY = pl.BlockSpec(memory_space=pl.ANY)


def _cp(n_axes, vmem_mib=48):
    return pltpu.CompilerParams(dimension_semantics=("arbitrary",) * n_axes, vmem_limit_bytes=vmem_mib * MIB)


def _row_loop(rows, fn):
    def body(r, carry):
        fn(pl.ds(pl.multiple_of(r * ROWS, ROWS), ROWS))
        return carry

    lax.fori_loop(0, rows // ROWS, body, 0)


def _fold8(v):
    s = v[0:8]
    for t in range(1, v.shape[0] // 8):
        s = s + v[8 * t:8 * t + 8]
    return s


def _dot(a, b):
    return jnp.dot(a, b, preferred_element_type=F32)


def _dot_nt(a, b):
    return lax.dot_general(a, b, (((1,), (1,)), ((), ())), preferred_element_type=F32)


def _dot_tn(a, b):
    return lax.dot_general(a, b, (((0,), (0,)), ((), ())), preferred_element_type=F32)


def _pick(n, *cands):
    for t in cands:
        if n % t == 0:
            return t
    return n


def _flip(pos, mask):
    return tuple((1 - v) if (mask >> (2 - i)) & 1 else v for i, v in enumerate(pos))


def _exchange(name, ins, out_shapes, remote, local=(), aliases=None):
    n_io = len(ins) + len(out_shapes)

    def body(*refs):
        io = refs[:n_io]
        send_sems, recv_sems, local_sems = refs[n_io:]
        me = (lax.axis_index("x"), lax.axis_index("y"), lax.axis_index("c"))

        def copy(i, sender):
            mask, src_fn, dst_fn = remote[i]
            return pltpu.make_async_remote_copy(
                src_ref=src_fn(io, sender), dst_ref=dst_fn(io, sender), send_sem=send_sems.at[i],
                recv_sem=recv_sems.at[i], device_id=_flip(sender, mask), device_id_type=MESH)

        own = [pltpu.make_async_copy(s(io, me), d(io, me), local_sems.at[i]) for i, (s, d) in enumerate(local)]
        for cp in own:
            cp.start()
        sends = [copy(i, me) for i in range(len(remote))]
        for cp in sends:
            cp.start()
        for i in range(len(remote)):
            copy(i, _flip(me, remote[i][0])).wait_recv()
        for cp in sends:
            cp.wait_send()
        for cp in own:
            cp.wait()

    return pl.pallas_call(
        body, name=name, out_shape=tuple(out_shapes),
        in_specs=[ANY] * len(ins), out_specs=tuple([ANY] * len(out_shapes)),
        scratch_shapes=[pltpu.SemaphoreType.DMA((len(remote),)), pltpu.SemaphoreType.DMA((len(remote),)),
                        pltpu.SemaphoreType.DMA((max(len(local), 1),))],
        input_output_aliases=aliases or {},
    )(*ins)


def _dev_index(pos):
    return 4 * pos[0] + 2 * pos[1] + pos[2]


def _chip_index(pos):
    return 2 * pos[0] + pos[1]


def _allgather8(name, v):
    out = jax.ShapeDtypeStruct((8,) + v.shape, v.dtype)
    remote = [(mask, lambda io, pos: io[0], lambda io, pos: io[1].at[_dev_index(pos)]) for mask in range(1, 8)]
    local = [(lambda io, pos: io[0], lambda io, pos: io[1].at[_dev_index(pos)])]
    return _exchange(name, [v], [out], remote, local)[0]


class _Big:
    def __init__(self, kind, shard_shape):
        self.kind = kind
        self.shard_shape = tuple(shard_shape)
        if kind == "col":
            r, cs = shard_shape
            self.full_shape = (r, 4 * cs)
            self.piece_shape = (r // 2, cs)
            self.half_shape = (r // 2, 4 * cs)
        elif kind == "row":
            rs, c = shard_shape
            self.full_shape = (4, 2, rs // 2, c)
            self.piece_shape = (1, 1, rs // 2, c)
            self.half_shape = (4, 1, rs // 2, c)
        else:
            self.full_shape = (4, 256, 256)
            self.piece_shape = (2, 64, 256)
            self.half_shape = (2, 256, 256)

    def shard_as_pieces(self, a):
        return a.reshape((1, 2) + self.piece_shape[2:]) if self.kind == "row" else a

    def piece(self, ref, k, h):
        if self.kind == "col":
            r, cs = self.piece_shape
            return ref.at[pl.ds(h * r, r), pl.ds(k * cs, cs)]
        if self.kind == "row":
            return ref.at[pl.ds(k, 1), pl.ds(h, 1)]
        return ref.at[pl.ds(2 * h, 2), pl.ds(64 * k, 64)]

    def half_of_shard(self, ref, h):
        if self.kind == "col":
            return ref.at[pl.ds(h * self.piece_shape[0], self.piece_shape[0])]
        if self.kind == "row":
            return ref.at[:, pl.ds(h, 1)]
        return ref.at[pl.ds(2 * h, 2)]

    def half_of_full(self, ref, h):
        if self.kind == "col":
            return ref.at[pl.ds(h * self.half_shape[0], self.half_shape[0])]
        if self.kind == "row":
            return ref.at[:, pl.ds(h, 1)]
        return ref.at[pl.ds(2 * h, 2)]

    def piece_of_half(self, ref, k):
        if self.kind == "col":
            return ref.at[:, pl.ds(k * self.piece_shape[1], self.piece_shape[1])]
        if self.kind == "row":
            return ref.at[pl.ds(k, 1)]
        return ref.at[:, pl.ds(64 * k, 64)]

    def place_shard(self, full, shard, k):
        if self.kind == "col":
            return lax.dynamic_update_slice(full, shard, (0, k * self.shard_shape[1]))
        if self.kind == "row":
            return lax.dynamic_update_slice(full, self.shard_as_pieces(shard), (k, 0, 0, 0))
        return lax.dynamic_update_slice(full, shard, (0, 64 * k, 0))

    def place_half(self, shard, piece, h):
        if self.kind == "col":
            return lax.dynamic_update_slice(shard, piece, (h * self.piece_shape[0], 0))
        if self.kind == "row":
            return lax.dynamic_update_slice(shard, piece, (0, h, 0, 0))
        return lax.dynamic_update_slice(shard, piece, (2 * h, 0, 0))

    def take_half(self, a, h):
        if self.kind == "row":
            return lax.dynamic_slice_in_dim(a, h, 1, axis=1)
        n = self.half_shape[0]
        return lax.dynamic_slice_in_dim(a, h * n, n, axis=0)

    def take_piece_of_half(self, a, k):
        if self.kind == "col":
            return lax.dynamic_slice_in_dim(a, k * self.piece_shape[1], self.piece_shape[1], axis=1)
        if self.kind == "row":
            return lax.dynamic_slice_in_dim(a, k, 1, axis=0)
        return lax.dynamic_slice_in_dim(a, 64 * k, 64, axis=1)


CHIP_MASKS = (4, 2, 6)


def _gather_weights(bigs, shards, pos):
    n = len(bigs)
    fulls = [jax.ShapeDtypeStruct(b.full_shape, BF16) for b in bigs]
    placed = [b.place_shard(lax.empty(b.full_shape, BF16), s, _chip_index(pos)) for b, s in zip(bigs, shards)]
    remote = []
    for a, b in enumerate(bigs):
        for mask in CHIP_MASKS:
            def mine(io, p, a=a, b=b):
                return b.piece(io[n + a], _chip_index(p), p[2])
            remote.append((mask, mine, mine))
    got = _exchange("gather_weights_ici", placed, fulls, remote, aliases={a: a for a in range(n)})
    remote = []
    for a, b in enumerate(bigs):
        for mask in CHIP_MASKS:
            def region(io, p, a=a, b=b, mask=mask):
                return b.piece(io[n + a], _chip_index(_flip(p, mask)), p[2])
            remote.append((1, region, region))
    return _exchange("gather_weights_d2d", list(got), fulls, remote, aliases={a: a for a in range(n)})


def _ew(name, fn, ins, out_dtypes, rows_per_step=256):
    shape = ins[0].shape
    last = shape[-1]
    rows = 1
    for s in shape[:-1]:
        rows *= s
    ins2 = [a.reshape(rows, last) for a in ins]
    tr = _pick(rows, rows_per_step, 128, 64, 32, 16, 8)
    spec = pl.BlockSpec((tr, last), lambda i: (i, 0))

    def body(*refs):
        outs = fn(*[r[...] for r in refs[:len(ins)]])
        for o_ref, o in zip(refs[len(ins):], outs):
            o_ref[...] = o.astype(o_ref.dtype)

    outs = pl.pallas_call(
        body, name=name, grid=(rows // tr,), in_specs=[spec] * len(ins), out_specs=tuple([spec] * len(out_dtypes)),
        out_shape=tuple(jax.ShapeDtypeStruct((rows, last), d) for d in out_dtypes), compiler_params=_cp(1),
    )(*ins2)
    return [o.reshape(shape) for o in outs]


def _reduce_grads(bigs, grads, pos):
    n = len(bigs)
    c = pos[2]
    k_me = _chip_index(pos)
    halves = [jax.ShapeDtypeStruct(b.half_shape, BF16) for b in bigs]
    remote = [(1, lambda io, p, a=a, b=b: b.half_of_full(io[a], 1 - p[2]), lambda io, p, a=a: io[n + a])
              for a, b in enumerate(bigs)]
    from_sibling = _exchange("reduce_d2d", grads, halves, remote)
    chip_sum = [_ew(f"reduce_chip_sum_{a}", lambda u, v: (u.astype(F32) + v.astype(F32),),
                    [b.take_half(g, c), r], [BF16])[0] for a, (b, g, r) in enumerate(zip(bigs, grads, from_sibling))]
    thirds = [jax.ShapeDtypeStruct((3,) + b.piece_shape, BF16) for b in bigs]
    remote = []
    for a, b in enumerate(bigs):
        for j, mask in enumerate(CHIP_MASKS):
            remote.append((mask,
                           lambda io, p, a=a, b=b, mask=mask: b.piece_of_half(io[a], _chip_index(_flip(p, mask))),
                           lambda io, p, a=a, j=j: io[n + a].at[j]))
    from_chips = _exchange("reduce_ici", chip_sum, thirds, remote)
    pieces = [_ew(f"reduce_sum_{a}", lambda u, r0, r1, r2: (u.astype(F32) + r0.astype(F32) + r1.astype(F32) + r2.astype(F32),),
                  [b.take_piece_of_half(s, k_me), r[0], r[1], r[2]], [F32])[0]
              for a, (b, s, r) in enumerate(zip(bigs, chip_sum, from_chips))]
    shard_like = [jax.ShapeDtypeStruct(b.shard_as_pieces(jnp.zeros(b.shard_shape, F32)).shape, F32) for b in bigs]
    placed = [b.place_half(lax.empty(sl.shape, F32), p, c) for b, sl, p in zip(bigs, shard_like, pieces)]
    remote = []
    for a, b in enumerate(bigs):
        def mine(io, p, a=a, b=b):
            return b.half_of_shard(io[n + a], p[2])
        remote.append((1, mine, mine))
    out = _exchange("reduce_share_d2d", placed, shard_like, remote, aliases={a: a for a in range(n)})
    return [o.reshape(b.shard_shape) for o, b in zip(out, bigs)]


def _mm(name, a, b, *, nt, tm, tn, tk, epi, extras=(), extra_specs=(), out_shape, out_specs, vmem_mib=48):
    m, kdim = a.shape
    n = b.shape[0] if nt else b.shape[1]
    gm, gn, gk = m // tm, n // tn, kdim // tk
    a_spec = pl.BlockSpec((tm, tk), lambda j, i, k: (i, k))
    b_spec = pl.BlockSpec((tn, tk), lambda j, i, k: (j, k)) if nt else pl.BlockSpec((tk, tn), lambda j, i, k: (k, j))
    n_ex = len(extras)

    def body(a_ref, b_ref, *rest):
        ex, outs, acc = rest[:n_ex], rest[n_ex:-1], rest[-1]
        dot = _dot_nt if nt else _dot
        if gk == 1:
            acc[...] = dot(a_ref[...], b_ref[...])
            epi(acc, ex, outs)
        else:
            k = pl.program_id(2)

            @pl.when(k == 0)
            def _():
                acc[...] = jnp.zeros_like(acc)

            acc[...] += dot(a_ref[...], b_ref[...])

            @pl.when(k == gk - 1)
            def _():
                epi(acc, ex, outs)

    return pl.pallas_call(
        body, name=name, grid=(gn, gm, gk), in_specs=[a_spec, b_spec, *extra_specs], out_specs=tuple(out_specs),
        out_shape=tuple(out_shape), scratch_shapes=[pltpu.VMEM((tm, tn), F32)], compiler_params=_cp(3, vmem_mib),
    )(a, b, *extras)


def _mm_tn(name, a, b, out_dtype, *, tmo, tn, tt, more=(), vmem_mib=56):
    t, m = a.shape
    n = b.shape[1]
    gt = t // tt

    def body(a_ref, b_ref, *rest):
        o_ref, acc = rest[-2:]
        k = pl.program_id(2)

        @pl.when(k == 0)
        def _():
            acc[...] = _dot_tn(rest[0][...], rest[1][...]) if more else jnp.zeros_like(acc)

        acc[...] += _dot_tn(a_ref[...], b_ref[...])

        @pl.when(k == gt - 1)
        def _():
            o_ref[...] = acc[...].astype(o_ref.dtype)

    more_specs = [pl.BlockSpec((more[0].shape[0], tmo), lambda i, j, k: (0, i)),
                  pl.BlockSpec((more[1].shape[0], tn), lambda i, j, k: (0, j))] if more else []
    return pl.pallas_call(
        body, name=name, grid=(m // tmo, n // tn, gt),
        in_specs=[pl.BlockSpec((tt, tmo), lambda i, j, k: (k, i)), pl.BlockSpec((tt, tn), lambda i, j, k: (k, j))] + more_specs,
        out_specs=pl.BlockSpec((tmo, tn), lambda i, j, k: (i, j)), out_shape=jax.ShapeDtypeStruct((m, n), out_dtype),
        scratch_shapes=[pltpu.VMEM((tmo, tn), F32)], compiler_params=_cp(3, vmem_mib),
    )(a, b, *more)


def _row_spec(d):
    return pl.BlockSpec((1, d), lambda *_: (0, 0))


def _stat_spec(k, d):
    return pl.BlockSpec((k, 8, d), lambda *_: (0, 0, 0))


def _rope(z, cs, sn):
    first = (lax.broadcasted_iota(jnp.int32, (z.shape[0], 128), 1) % 32) < 16
    outs = []
    for j in range(z.shape[1] // 128):
        zc = z[:, 128 * j:128 * (j + 1)]
        partner = jnp.where(first, pltpu.roll(zc, 112, 1), pltpu.roll(zc, 16, 1))
        outs.append(zc * cs + partner * sn)
    return outs[0] if len(outs) == 1 else jnp.concatenate(outs, axis=1)


def _rope_tables(length, rotate):
    if not rotate:
        return jnp.ones((length, 128), F32), jnp.zeros((length, 128), F32)
    half = HEAD_DIM // 2
    inv_freq = ROPE_BASE ** (-jnp.arange(0, half, 2, dtype=F32) / half)
    t = jnp.arange(length)
    row = (t // GRID_W).astype(F32)
    col = (t % GRID_W).astype(F32)
    e = jnp.arange(128) % HEAD_DIM
    pos = jnp.where(e[None, :] < half, row[:, None], col[:, None])
    ang = pos * inv_freq[(e % half) % (half // 2)][None, :]
    first = ((e % half) < half // 2)[None, :]
    return jnp.cos(ang), jnp.where(first, -jnp.sin(ang), jnp.sin(ang))


def _mixer_in(name, x, nw, sh, sc, w_in, cos, sin):
    t, d = x.shape
    tm = _pick(t, 256, 128)
    n_in = w_in.shape[1]

    def body(x_ref, nw_ref, sh_ref, sc_ref, w_ref, cos_ref, sin_ref, h_ref, q_ref, k_ref, v_ref, u_ref):
        xf = x_ref[...]
        r = lax.rsqrt(jnp.mean(xf * xf, axis=-1, keepdims=True) + EPS)
        hb = (((xf * r) * nw_ref[...]) * (1.0 + sc_ref[...]) + sh_ref[...]).astype(BF16)
        h_ref[...] = hb
        p = _dot(hb, w_ref[...])
        cs, sn = cos_ref[...], sin_ref[...]
        q_ref[...] = _rope(p[:, :ATTN_WIDTH], cs, sn).astype(BF16)
        k_ref[...] = _rope(p[:, ATTN_WIDTH:ATTN_WIDTH + KV_WIDTH], cs, sn).astype(BF16)
        v_ref[...] = p[:, ATTN_WIDTH + KV_WIDTH:ATTN_WIDTH + 2 * KV_WIDTH].astype(BF16)
        u_ref[...] = p[:, ATTN_WIDTH + 2 * KV_WIDTH:]

    def tile(w):
        return pl.BlockSpec((tm, w), lambda i: (i, 0))

    return pl.pallas_call(
        body, name=name, grid=(t // tm,),
        in_specs=[tile(d), _row_spec(d), _row_spec(d), _row_spec(d), pl.BlockSpec((d, n_in), lambda i: (0, 0)),
                  tile(128), tile(128)],
        out_specs=(tile(d), tile(ATTN_WIDTH), tile(KV_WIDTH), tile(KV_WIDTH), tile(POOL_WIDTH)),
        out_shape=(jax.ShapeDtypeStruct((t, d), BF16), jax.ShapeDtypeStruct((t, ATTN_WIDTH), BF16),
                   jax.ShapeDtypeStruct((t, KV_WIDTH), BF16), jax.ShapeDtypeStruct((t, KV_WIDTH), BF16),
                   jax.ShapeDtypeStruct((t, POOL_WIDTH), F32)),
        compiler_params=_cp(1),
    )(x, nw, sh, sc, w_in, cos, sin)


def _attn_specs(nb, n_ctx):
    def blk(w, f):
        return pl.BlockSpec((BLOCK, w), lambda n: (f(n), 0))

    prev = lambda n: jnp.maximum(n - 1, 0)
    cur = lambda n: n
    nxt = lambda n: jnp.minimum(n + 1, nb - 1)
    kv = [blk(KV_WIDTH, prev), blk(KV_WIDTH, cur), blk(KV_WIDTH, nxt)]
    ctx = pl.BlockSpec((n_ctx, KV_WIDTH), lambda n: (0, 0))
    return [pl.BlockSpec(memory_space=pltpu.SMEM), blk(ATTN_WIDTH, cur)] + kv + kv + [ctx, ctx]


def _attn_mask(n, length, n_keys):
    row = lax.broadcasted_iota(jnp.int32, (GROUP * BLOCK, n_keys), 0) % BLOCK
    col = lax.broadcasted_iota(jnp.int32, (GROUP * BLOCK, n_keys), 1)
    kpos = (n - 1) * BLOCK + col
    return ((jnp.abs(col - BLOCK - row) <= BLOCK) & (kpos >= 0) & (kpos < length)) | (col >= 3 * BLOCK)


def _group_rows(block, g):
    return jnp.concatenate([block[:, HEAD_DIM * h:HEAD_DIM * (h + 1)] for h in range(GROUP * g, GROUP * (g + 1))], axis=0)


def _group_sink(sink_ref, g):
    head = lax.broadcasted_iota(jnp.int32, (GROUP * BLOCK, 1), 0) // BLOCK
    out = jnp.full((GROUP * BLOCK, 1), sink_ref[0, GROUP * g], F32)
    for j in range(1, GROUP):
        out = jnp.where(head == j, sink_ref[0, GROUP * g + j], out)
    return out


def _attn_fwd(q, k, v, kc, vc, sink):
    length = q.shape[0]
    nb = length // BLOCK
    n_ctx = kc.shape[0]
    n_keys = 3 * BLOCK + n_ctx

    def body(sink_ref, q_ref, kp, k0, kn, vp, v0, vn, kc_ref, vc_ref, o_ref):
        n = pl.program_id(0)
        valid = _attn_mask(n, length, n_keys)
        qb = q_ref[...]
        kall = jnp.concatenate([kp[...], k0[...], kn[...], kc_ref[...]], axis=0)
        vall = jnp.concatenate([vp[...], v0[...], vn[...], vc_ref[...]], axis=0)
        outs = []
        for g in range(N_KV_HEADS):
            lanes = slice(HEAD_DIM * g, HEAD_DIM * (g + 1))
            s = jnp.where(valid, _dot_nt(_group_rows(qb, g), kall[:, lanes]) * SCALE, NEG_INF)
            sk = _group_sink(sink_ref, g)
            m = jnp.maximum(jnp.max(s, axis=-1, keepdims=True), sk)
            e = jnp.exp(s - m)
            den = jnp.sum(e, axis=-1, keepdims=True) + jnp.exp(sk - m)
            o = _dot(e.astype(BF16), vall[:, lanes]) / den
            outs += [o[BLOCK * j:BLOCK * (j + 1)] for j in range(GROUP)]
        o_ref[...] = jnp.concatenate(outs, axis=1).astype(BF16)

    return pl.pallas_call(
        body, name="attn_fwd", grid=(nb,), in_specs=_attn_specs(nb, n_ctx),
        out_specs=pl.BlockSpec((BLOCK, ATTN_WIDTH), lambda n: (n, 0)),
        out_shape=jax.ShapeDtypeStruct((length, ATTN_WIDTH + POOL_WIDTH), BF16), compiler_params=_cp(1),
    )(sink, q, k, k, k, v, v, v, kc, vc)


def _attn_bwd(q, k, v, kc, vc, sink, dmix):
    length = q.shape[0]
    nb = length // BLOCK
    n_ctx = kc.shape[0]
    n_keys = 3 * BLOCK + n_ctx

    def body(sink_ref, q_ref, kp, k0, kn, vp, v0, vn, kc_ref, vc_ref, do_ref,
             dq_ref, dkp_ref, dvp_ref, dkc_ref, dvc_ref, dsink_ref):
        n = pl.program_id(0)

        @pl.when(n == 0)
        def _():
            dkc_ref[...] = jnp.zeros_like(dkc_ref)
            dvc_ref[...] = jnp.zeros_like(dvc_ref)
            dsink_ref[...] = jnp.zeros_like(dsink_ref)

        valid = _attn_mask(n, length, n_keys)
        qb, dob = q_ref[...], do_ref[...]
        kall = jnp.concatenate([kp[...], k0[...], kn[...], kc_ref[...]], axis=0)
        vall = jnp.concatenate([vp[...], v0[...], vn[...], vc_ref[...]], axis=0)
        srow = lax.broadcasted_iota(jnp.int32, (8, 128), 0)
        slane = lax.broadcasted_iota(jnp.int32, (8, 128), 1)
        dqs, dks, dvs = [], [], []
        dsink = jnp.zeros((8, 128), F32)
        for g in range(N_KV_HEADS):
            lanes = slice(HEAD_DIM * g, HEAD_DIM * (g + 1))
            kg, vg = kall[:, lanes], vall[:, lanes]
            qg, dog = _group_rows(qb, g), _group_rows(dob, g)
            s = jnp.where(valid, _dot_nt(qg, kg) * SCALE, NEG_INF)
            sk = _group_sink(sink_ref, g)
            m = jnp.maximum(jnp.max(s, axis=-1, keepdims=True), sk)
            e = jnp.exp(s - m)
            inv = 1.0 / (jnp.sum(e, axis=-1, keepdims=True) + jnp.exp(sk - m))
            p = e * inv
            dp = _dot_nt(dog, vg)
            delta = jnp.sum(p * dp, axis=-1, keepdims=True)
            ds = (p * (dp - delta) * SCALE).astype(BF16)
            dq = _dot(ds, kg)
            dqs += [dq[BLOCK * j:BLOCK * (j + 1)] for j in range(GROUP)]
            dks.append(_dot_tn(ds, qg))
            dvs.append(_dot_tn(p.astype(BF16), dog))
            d_sink = jnp.exp(sk - m) * inv * delta
            for j in range(GROUP):
                total = -jnp.sum(d_sink[BLOCK * j:BLOCK * (j + 1)], axis=0, keepdims=True)
                dsink = dsink + jnp.where((srow == 0) & (slane == GROUP * g + j), total, 0.0)
        dq_ref[...] = jnp.concatenate(dqs, axis=1)
        dk = jnp.concatenate(dks, axis=1)
        dv = jnp.concatenate(dvs, axis=1)
        for j in range(3):
            dkp_ref[0, j] = dk[BLOCK * j:BLOCK * (j + 1)]
            dvp_ref[0, j] = dv[BLOCK * j:BLOCK * (j + 1)]
        dkc_ref[...] += dk[3 * BLOCK:]
        dvc_ref[...] += dv[3 * BLOCK:]
        dsink_ref[...] += dsink

    part = pl.BlockSpec((1, 3, BLOCK, KV_WIDTH), lambda n: (n, 0, 0, 0))
    ctx = pl.BlockSpec((n_ctx, KV_WIDTH), lambda n: (0, 0))
    return pl.pallas_call(
        body, name="attn_bwd", grid=(nb,),
        in_specs=_attn_specs(nb, n_ctx) + [pl.BlockSpec((BLOCK, ATTN_WIDTH), lambda n: (n, 0))],
        out_specs=(pl.BlockSpec((BLOCK, ATTN_WIDTH), lambda n: (n, 0)), part, part, ctx, ctx,
                   pl.BlockSpec((8, 128), lambda n: (0, 0))),
        out_shape=(jax.ShapeDtypeStruct((length, ATTN_WIDTH), F32),
                   jax.ShapeDtypeStruct((nb, 3, BLOCK, KV_WIDTH), F32), jax.ShapeDtypeStruct((nb, 3, BLOCK, KV_WIDTH), F32),
                   jax.ShapeDtypeStruct((n_ctx, KV_WIDTH), F32), jax.ShapeDtypeStruct((n_ctx, KV_WIDTH), F32),
                   jax.ShapeDtypeStruct((8, 128), F32)),
        compiler_params=_cp(1),
    )(sink, q, k, k, k, v, v, v, kc, vc, dmix)


def _assemble_dp(dq, dkp, dvp, du, cos, sin):
    length = dq.shape[0]
    nb = length // BLOCK

    def body(dq_ref, dka, dkb, dkc, dva, dvb, dvc, du_ref, cos_ref, sin_ref, o_ref):
        n = pl.program_id(0)
        has_next = (n + 1 < nb).astype(F32)
        has_prev = (n > 0).astype(F32)
        cs, sn = cos_ref[...], -sin_ref[...]
        dk = dka[0, 0] * has_next + dkb[0, 0] + dkc[0, 0] * has_prev
        dv = dva[0, 0] * has_next + dvb[0, 0] + dvc[0, 0] * has_prev
        o_ref[:, :ATTN_WIDTH] = _rope(dq_ref[...], cs, sn).astype(BF16)
        o_ref[:, ATTN_WIDTH:ATTN_WIDTH + KV_WIDTH] = _rope(dk, cs, sn).astype(BF16)
        o_ref[:, ATTN_WIDTH + KV_WIDTH:ATTN_WIDTH + 2 * KV_WIDTH] = dv.astype(BF16)
        o_ref[:, ATTN_WIDTH + 2 * KV_WIDTH:] = du_ref[...]

    def part(slot, f):
        return pl.BlockSpec((1, 1, BLOCK, KV_WIDTH), lambda n: (f(n), slot, 0, 0))

    parts = [part(0, lambda n: jnp.minimum(n + 1, nb - 1)), part(1, lambda n: n), part(2, lambda n: jnp.maximum(n - 1, 0))]

    def tile(w):
        return pl.BlockSpec((BLOCK, w), lambda n: (n, 0))

    width = ATTN_WIDTH + 2 * KV_WIDTH + POOL_WIDTH
    return pl.pallas_call(
        body, name="assemble_dp", grid=(nb,),
        in_specs=[tile(ATTN_WIDTH)] + parts + parts + [tile(POOL_WIDTH), tile(128), tile(128)],
        out_specs=tile(width), out_shape=jax.ShapeDtypeStruct((length, width), BF16), compiler_params=_cp(1),
    )(dq, dkp, dkp, dkp, dvp, dvp, dvp, du, cos, sin)


def _shift_rows(e, s):
    n = e.shape[0]
    return e if s % n == 0 else pltpu.roll(e, (-s) % n, 0)


def _window_sum(e, w, first):
    s, n = e, 1
    while n < w:
        s = s + _shift_rows(s, n)
        n *= 2
    return _shift_rows(s, first)


def _pool_geometry(i, tm, length):
    pos = i * tm - HALO + lax.broadcasted_iota(jnp.int32, (tm + 2 * HALO, 1), 0)
    inside = (pos >= 0) & (pos < length)
    inv_counts = []
    for w in POOL_WINDOWS:
        lo = jnp.clip(pos - w // 2, 0, length)
        hi = jnp.clip(pos - w // 2 + w, 0, length)
        inv_counts.append(1.0 / jnp.maximum(hi - lo, 1).astype(F32))
    return inside, inv_counts


def _halo_specs(tm, width, length, col=0):
    per = tm // HALO
    last = length // HALO - 1
    return [pl.BlockSpec((HALO, width), lambda i: (jnp.maximum(i * per - 1, 0), col)),
            pl.BlockSpec((tm, width), lambda i: (i, col)),
            pl.BlockSpec((HALO, width), lambda i: (jnp.minimum((i + 1) * per, last), col))]


def _pooled(ext, inv_counts, tm):
    outs = []
    for g, w in enumerate(POOL_WINDOWS):
        e = ext[:, POOL_GROUP_DIM * g:POOL_GROUP_DIM * (g + 1)]
        mean = _window_sum(e, w, -(w // 2)) * inv_counts[g]
        outs.append((mean - e)[HALO:HALO + tm])
    return outs


def _pool_fwd(u, pool_w, pool_scale, mix):
    length = u.shape[0]
    tm = _pick(length, 256, 128)

    def body(up, u0, un, w_ref, sc_ref, mix_ref, o_ref):
        inside, inv_counts = _pool_geometry(pl.program_id(0), tm, length)
        ext = jnp.where(inside, jnp.concatenate([up[...], u0[...], un[...]], axis=0), 0.0)
        pooled = _pooled(ext, inv_counts, tm)
        mixed = [_dot(pooled[g].astype(BF16), w_ref[g]) for g in range(len(POOL_WINDOWS))]
        o_ref[...] = (jnp.concatenate(mixed, axis=1) * sc_ref[...]).astype(BF16)

    return pl.pallas_call(
        body, name="pool_fwd", grid=(length // tm,),
        in_specs=_halo_specs(tm, POOL_WIDTH, length) + [pl.BlockSpec(pool_w.shape, lambda i: (0, 0, 0)), _row_spec(POOL_WIDTH), ANY],
        out_specs=pl.BlockSpec((tm, POOL_WIDTH), lambda i: (i, 1)),
        out_shape=jax.ShapeDtypeStruct(mix.shape, BF16), input_output_aliases={5: 0}, compiler_params=_cp(1),
    )(u, u, u, pool_w, pool_scale, mix)


def _pool_bwd(u, dmix, pool_w, pool_scale):
    length = u.shape[0]
    tm = _pick(length, 256, 128)
    n_g = len(POOL_WINDOWS)

    def body(up, u0, un, dp_, d0, dn_, w_ref, sc_ref, du_ref, dw_ref, dsc_ref):
        i = pl.program_id(0)

        @pl.when(i == 0)
        def _():
            dw_ref[...] = jnp.zeros_like(dw_ref)
            dsc_ref[...] = jnp.zeros_like(dsc_ref)

        inside, inv_counts = _pool_geometry(i, tm, length)
        ext = jnp.where(inside, jnp.concatenate([up[...], u0[...], un[...]], axis=0), 0.0)
        dext = jnp.where(inside, jnp.concatenate([dp_[...], d0[...], dn_[...]], axis=0).astype(F32), 0.0)
        dmixed = (dext * sc_ref[...]).astype(BF16)
        pooled = _pooled(ext, inv_counts, tm)
        dus, dscs = [], []
        for g, w in enumerate(POOL_WINDOWS):
            lanes = slice(POOL_GROUP_DIM * g, POOL_GROUP_DIM * (g + 1))
            dpooled = _dot_nt(dmixed[:, lanes], w_ref[g])
            spread = _window_sum(dpooled * inv_counts[g], w, -(w // 2 - 1))
            dus.append((spread - dpooled)[HALO:HALO + tm])
            pb = pooled[g].astype(BF16)
            dw_ref[g] += _dot_tn(pb, dmixed[HALO:HALO + tm, lanes])
            prod = dext[HALO:HALO + tm, lanes] * _dot(pb, w_ref[g])
            dscs.append(_fold8(prod))
        du_ref[...] = jnp.concatenate(dus, axis=1).astype(BF16)
        dsc_ref[...] += jnp.concatenate(dscs, axis=1)

    return pl.pallas_call(
        body, name="pool_bwd", grid=(length // tm,),
        in_specs=_halo_specs(tm, POOL_WIDTH, length) + _halo_specs(tm, POOL_WIDTH, length, col=1)
        + [pl.BlockSpec(pool_w.shape, lambda i: (0, 0, 0)), _row_spec(POOL_WIDTH)],
        out_specs=(pl.BlockSpec((tm, POOL_WIDTH), lambda i: (i, 0)), pl.BlockSpec((n_g, POOL_GROUP_DIM, POOL_GROUP_DIM), lambda i: (0, 0, 0)),
                   pl.BlockSpec((8, POOL_WIDTH), lambda i: (0, 0))),
        out_shape=(jax.ShapeDtypeStruct((length, POOL_WIDTH), BF16), jax.ShapeDtypeStruct((n_g, POOL_GROUP_DIM, POOL_GROUP_DIM), F32),
                   jax.ShapeDtypeStruct((8, POOL_WIDTH), F32)),
        compiler_params=_cp(1),
    )(u, u, u, dmix, dmix, dmix, pool_w, pool_scale)


def _mixer_out(mix, w_out, x, g_a, nmw, sh_m, sc_m):
    t, d = x.shape
    tm = _pick(t, 256, 128)

    def epi(acc, ex, outs):
        x_ref, ga, nw, sh, sc = ex
        x1_ref, mo_ref, hm_ref = outs

        def rows(rs):
            mo = acc[rs, :]
            x1 = x_ref[rs, :] + ga[...] * mo
            x1_ref[rs, :] = x1
            mo_ref[rs, :] = mo.astype(BF16)
            r = lax.rsqrt(jnp.mean(x1 * x1, axis=-1, keepdims=True) + EPS)
            hm_ref[rs, :] = (((x1 * r) * nw[...]) * (1.0 + sc[...]) + sh[...]).astype(BF16)

        _row_loop(tm, rows)

    tile = pl.BlockSpec((tm, d), lambda j, i, k: (i, 0))
    return _mm("mixer_out", mix, w_out, nt=False, tm=tm, tn=d, tk=mix.shape[1], epi=epi,
               extras=(x, g_a, nmw, sh_m, sc_m), extra_specs=[tile] + [_row_spec(d)] * 4,
               out_shape=(jax.ShapeDtypeStruct((t, d), F32), jax.ShapeDtypeStruct((t, d), BF16), jax.ShapeDtypeStruct((t, d), BF16)),
               out_specs=(tile, tile, tile))


def _mlp_up(hm, w_up):
    t, d = hm.shape
    tm = _pick(t, 512, 256, 128)
    tn = 2048

    def epi(acc, ex, outs):
        outs[0][...] = jnp.square(jnp.maximum(acc[...], 0.0)).astype(BF16)

    return _mm("mlp_up", hm, w_up, nt=False, tm=tm, tn=tn, tk=d, epi=epi,
               out_shape=(jax.ShapeDtypeStruct((t, w_up.shape[1]), BF16),),
               out_specs=(pl.BlockSpec((tm, tn), lambda j, i, k: (i, j)),))[0]


def _mm_f32(name, a, b, *, nt):
    m, kdim = a.shape
    n = b.shape[0] if nt else b.shape[1]
    tm, tn = _pick(m, 1024, 512, 256, 128), _pick(n, 1024)

    def epi(acc, ex, outs):
        outs[0][...] = acc[...]

    return _mm(name, a, b, nt=nt, tm=tm, tn=tn, tk=_pick(kdim, 2048), epi=epi,
               out_shape=(jax.ShapeDtypeStruct((m, n), F32),), out_specs=(pl.BlockSpec((tm, tn), lambda j, i, k: (i, j)),))[0]


def _rows_call(name, rows_fn, tiles, vecs, out_shape, n_stats):
    t, d = tiles[0].shape
    tm = _pick(t, 256, 128)
    n_t, n_v = len(tiles), len(vecs)

    def body(*refs):
        st_ref = refs[-1]

        @pl.when(pl.program_id(0) == 0)
        def _():
            st_ref[...] = jnp.zeros_like(st_ref)

        _row_loop(tm, lambda rs: rows_fn(rs, refs[:n_t], refs[n_t:n_t + n_v], refs[n_t + n_v:-1], st_ref))

    tile = pl.BlockSpec((tm, d), lambda i: (i, 0))
    return pl.pallas_call(
        body, name=name, grid=(t // tm,), in_specs=[tile] * n_t + [_row_spec(d)] * n_v,
        out_specs=tuple([tile] * len(out_shape)) + (_stat_spec(n_stats, d),),
        out_shape=tuple(out_shape) + (jax.ShapeDtypeStruct((n_stats, 8, d), F32),), compiler_params=_cp(1),
    )(*tiles, *vecs)


def _loss_rows(dn, x1, target, g_m, fw):
    t, d = x1.shape

    def rows_fn(rs, tiles, vecs, outs, st_ref):
        dn_ref, x1_ref, t_ref = tiles
        gm, fw_ref = vecs
        dx2_ref, ddn_ref = outs
        dnv = dn_ref[rs, :]
        x2 = x1_ref[rs, :] + gm[...] * dnv
        r = lax.rsqrt(jnp.mean(x2 * x2, axis=-1, keepdims=True) + EPS)
        xh = x2 * r
        diff = xh * fw_ref[...] - t_ref[rs, :]
        dy = diff * (1.0 / d)
        dxh = dy * fw_ref[...]
        dx2 = r * (dxh - xh * jnp.mean(dxh * xh, axis=-1, keepdims=True))
        dx2_ref[rs, :] = dx2
        ddn_ref[rs, :] = (dx2 * gm[...]).astype(BF16)
        st_ref[0] += _fold8(diff * diff)
        st_ref[1] += _fold8(dy * xh)
        st_ref[2] += _fold8(dx2 * dnv)

    return _rows_call("loss_rows", rows_fn, (dn, x1, target), (g_m, fw),
                      (jax.ShapeDtypeStruct((t, d), F32), jax.ShapeDtypeStruct((t, d), BF16)), 3)


def _mlp_dx_rows(dhm, x1, dx2, mo, nmw, sc_m, g_a):
    t, d = x1.shape

    def rows_fn(rs, tiles, vecs, outs, st_ref):
        dh_ref, x1_ref, dx2_ref, mo_ref = tiles
        nw, sc, ga = vecs
        dx1_ref, dmi_ref = outs
        dx1 = _norm_bwd_rows(dh_ref[rs, :], x1_ref[rs, :], nw[...], sc[...], st_ref) + dx2_ref[rs, :]
        dx1_ref[rs, :] = dx1
        dmi_ref[rs, :] = (dx1 * ga[...]).astype(BF16)
        st_ref[3] += _fold8(dx1 * mo_ref[rs, :].astype(F32))

    return _rows_call("mlp_dx_rows", rows_fn, (dhm, x1, dx2, mo), (nmw, sc_m, g_a),
                      (jax.ShapeDtypeStruct((t, d), F32), jax.ShapeDtypeStruct((t, d), BF16)), 4)


def _mlp_dact(ddn, w_down, act):
    t, d = ddn.shape
    tm = _pick(t, 512, 256, 128)
    tn = 2048

    def epi(acc, ex, outs):
        outs[0][...] = (acc[...] * (2.0 * jnp.sqrt(ex[0][...].astype(F32)))).astype(BF16)

    tile = pl.BlockSpec((tm, tn), lambda j, i, k: (i, j))
    return _mm("mlp_dact", ddn, w_down, nt=True, tm=tm, tn=tn, tk=d, epi=epi, extras=(act,), extra_specs=[tile],
               out_shape=(jax.ShapeDtypeStruct(act.shape, BF16),), out_specs=(tile,))[0]


def _norm_bwd_rows(dh, xv, nw, sc, st_ref):
    r = lax.rsqrt(jnp.mean(xv * xv, axis=-1, keepdims=True) + EPS)
    xh = xv * r
    dy = dh * (1.0 + sc)
    st_ref[0] += _fold8(dh)
    st_ref[1] += _fold8(dh * (xh * nw))
    st_ref[2] += _fold8(dy * xh)
    dxh = dy * nw
    return r * (dxh - xh * jnp.mean(dxh * xh, axis=-1, keepdims=True))


def _mixer_dmix(dmi, w_out):
    t, d = dmi.shape
    tm = _pick(t, 512, 256, 128)

    def epi(acc, ex, outs):
        outs[0][...] = acc[...].astype(BF16)

    n = w_out.shape[0]
    return _mm("mixer_dmix", dmi, w_out, nt=True, tm=tm, tn=n, tk=d, epi=epi,
               out_shape=(jax.ShapeDtypeStruct((t, n), BF16),), out_specs=(pl.BlockSpec((tm, n), lambda j, i, k: (i, 0)),))[0]


def _mixer_dx(name, dp, w_in, x, dx1, naw, sc_a):
    t, d = x.shape
    tm = _pick(t, 256, 128)

    def epi(acc, ex, outs):
        x_ref, dx1_ref, nw, sc = ex
        gx_ref, st_ref = outs

        @pl.when(pl.program_id(1) == 0)
        def _():
            st_ref[...] = jnp.zeros_like(st_ref)

        def rows(rs):
            gx_ref[rs, :] = _norm_bwd_rows(acc[rs, :], x_ref[rs, :], nw[...], sc[...], st_ref) + dx1_ref[rs, :]

        _row_loop(tm, rows)

    tile = pl.BlockSpec((tm, d), lambda j, i, k: (i, 0))
    return _mm(name, dp, w_in, nt=True, tm=tm, tn=d, tk=dp.shape[1], epi=epi,
               extras=(x, dx1, naw, sc_a), extra_specs=[tile, tile, _row_spec(d), _row_spec(d)],
               out_shape=(jax.ShapeDtypeStruct((t, d), F32), jax.ShapeDtypeStruct((3, 8, d), F32)),
               out_specs=(tile, _stat_spec(3, d)))


def _silu(v):
    return v / (1.0 + jnp.exp(-v))


def _ada_fwd(cond, w_ada, b_ada):
    d, n = w_ada.shape
    tn = 512

    def body(c_ref, w_ref, b_ref, o_ref):
        o_ref[...] = _dot(_silu(c_ref[...]).astype(BF16), w_ref[...].astype(BF16)) + b_ref[...]

    return pl.pallas_call(
        body, name="ada_fwd", grid=(n // tn,),
        in_specs=[pl.BlockSpec(cond.shape, lambda j: (0, 0)), pl.BlockSpec((d, tn), lambda j: (0, j)), pl.BlockSpec((1, tn), lambda j: (0, j))],
        out_specs=pl.BlockSpec((cond.shape[0], tn), lambda j: (0, j)), out_shape=jax.ShapeDtypeStruct((cond.shape[0], n), F32),
        compiler_params=_cp(1),
    )(cond, w_ada, b_ada)


def _adamw_math(w, g, m, v):
    m = ADAM_B1 * m + (1.0 - ADAM_B1) * g
    v = ADAM_B2 * v + (1.0 - ADAM_B2) * jnp.square(g)
    m_hat = m / (1.0 - ADAM_B1 ** ADAM_STEP)
    v_hat = v / (1.0 - ADAM_B2 ** ADAM_STEP)
    return -ADAM_LR * (m_hat / (jnp.sqrt(v_hat) + ADAM_EPS) + ADAM_WD * w), m, v


def _ada_bwd(cond, dm, w_ada, m_ada, v_ada):
    d, n = w_ada.shape
    tn = 256
    rows = cond.shape[0]

    def body(c_ref, dm_ref, w_ref, m_ref, v_ref, g_ref, dl_ref, nm_ref, nv_ref, pc_ref):
        @pl.when(pl.program_id(0) == 0)
        def _():
            pc_ref[...] = jnp.zeros_like(pc_ref)

        dmb = dm_ref[...].astype(BF16)
        w = w_ref[...]
        g = _dot_tn(_silu(c_ref[...]).astype(BF16), dmb)
        g_ref[...] = g
        dl_ref[...], nm_ref[...], nv_ref[...] = _adamw_math(w, g, m_ref[...], v_ref[...])
        pc_ref[...] += _dot_nt(dm_ref[8:16, :].astype(BF16), w.astype(BF16))

    tile = pl.BlockSpec((d, tn), lambda j: (0, j))
    like = jax.ShapeDtypeStruct((d, n), F32)
    return pl.pallas_call(
        body, name="ada_bwd", grid=(n // tn,),
        in_specs=[pl.BlockSpec((rows, d), lambda j: (0, 0)), pl.BlockSpec((rows, tn), lambda j: (0, j)), tile, tile, tile],
        out_specs=(tile, tile, tile, tile, pl.BlockSpec((8, d), lambda j: (0, 0))),
        out_shape=(like, like, like, like, jax.ShapeDtypeStruct((8, d), F32)), compiler_params=_cp(1),
    )(cond, dm, w_ada, m_ada, v_ada)


def _adamw(name, w, g, m, v):
    return _ew(name, _adamw_math, [w, g, m, v], [F32, F32, F32])


def _colsum(st):
    return jnp.sum(st, axis=1)


def kernel(x, c, ctx, c_ctx, norm_attn_w, norm_mlp_w, w_ada, b_ada, w_in, attn_sink, pool_w, pool_scale, w_out, w_mlp_up, w_mlp_down, final_norm_w, loss_target, m_c_ctx, m_norm_attn_w, m_norm_mlp_w, m_w_ada, m_b_ada, m_w_in, m_attn_sink, m_pool_w, m_pool_scale, m_w_out, m_w_mlp_up, m_w_mlp_down, m_final_norm_w, v_c_ctx, v_norm_attn_w, v_norm_mlp_w, v_w_ada, v_b_ada, v_w_in, v_attn_sink, v_pool_w, v_pool_scale, v_w_out, v_w_mlp_up, v_w_mlp_down, v_final_norm_w):
    length, d = x.shape[1], x.shape[2]
    n_ctx = ctx.shape[1]
    pos = (lax.axis_index("x"), lax.axis_index("y"), lax.axis_index("c"))
    me, chip = _dev_index(pos), _chip_index(pos)
    xs, tgt, cx = x.reshape(length, d), loss_target.reshape(length, d), ctx.reshape(n_ctx, d)
    n_ada = w_ada.shape[2]

    c_all = _allgather8("gather_c", jnp.pad(c, ((0, 7), (0, 0))))
    cond = jnp.concatenate([c_all[:, 0, :], jnp.pad(c_ctx[None, :], ((0, 7), (0, 0)))], axis=0)
    b_shard = lax.dynamic_slice_in_dim(b_ada, chip * n_ada, n_ada, axis=1)
    mod_all = _allgather8("gather_mod", _ada_fwd(cond, w_ada[0], b_shard))
    mod = jnp.concatenate([mod_all[0], mod_all[2], mod_all[4], mod_all[6]], axis=1)
    mine = lax.dynamic_slice_in_dim(mod, me, 1, axis=0)
    sh_a, sc_a, g_a, sh_m, sc_m, g_m = [mine[:, d * i:d * (i + 1)] for i in range(6)]
    csh_a, csc_a = mod[8:9, :d], mod[8:9, d:2 * d]

    bigs = [_Big("col", w_in.shape[1:]), _Big("pool", pool_w.shape[1:]), _Big("row", w_out.shape[1:]),
            _Big("col", w_mlp_up.shape[1:]), _Big("row", w_mlp_down.shape[1:])]
    shards = [w_in[0], pool_w[0], w_out[0], w_mlp_up[0], w_mlp_down[0]]
    win_b, pw_b, wout_b, wup_b, wdn_b = _gather_weights(bigs, [s.astype(BF16) for s in shards], pos)
    wout_b = wout_b.reshape(-1, d)
    wdn_b = wdn_b.reshape(-1, d)

    cos, sin = _rope_tables(length, True)
    one, zero = _rope_tables(n_ctx, False)
    h, q, k, v, u = _mixer_in("mixer_in", xs, norm_attn_w, sh_a, sc_a, win_b, cos, sin)
    hc, _, kc, vc, _ = _mixer_in("mixer_in_ctx", cx, norm_attn_w, csh_a, csc_a, win_b, one, zero)
    mix = _pool_fwd(u, pw_b, pool_scale, _attn_fwd(q, k, v, kc, vc, attn_sink))
    x1, mo, hm = _mixer_out(mix, wout_b, xs, g_a, norm_mlp_w, sh_m, sc_m)
    act = _mlp_up(hm, wup_b)
    dn = _mm_f32("mlp_down", act, wdn_b, nt=False)
    dx2, ddn, st_loss = _loss_rows(dn, x1, tgt, g_m, final_norm_w[None, :])
    st_loss = _colsum(st_loss)
    loss = lax.psum(0.5 / d * jnp.sum(st_loss[0]), ("x", "y", "c"))

    tt = _pick(length, 2048, 1024, 512, 256, 128)
    g_wdn = _mm_tn("grad_w_down", act, ddn, BF16, tmo=1024, tn=d, tt=tt)
    dup = _mlp_dact(ddn, wdn_b, act)
    g_wup = _mm_tn("grad_w_up", hm, dup, BF16, tmo=d, tn=1024, tt=tt)
    dhm = _mm_f32("mlp_dhm", dup, wup_b, nt=True)
    dx1, dmi, st_mlp = _mlp_dx_rows(dhm, x1, dx2, mo, norm_mlp_w, sc_m, g_a)
    st_mlp = _colsum(st_mlp)
    g_wout = _mm_tn("grad_w_out", mix, dmi, BF16, tmo=1024, tn=d, tt=tt)
    dmix = _mixer_dmix(dmi, wout_b)
    dq, dkp, dvp, dkc, dvc, dsink = _attn_bwd(q, k, v, kc, vc, attn_sink, dmix)
    du, g_pw, st_pool = _pool_bwd(u, dmix, pw_b, pool_scale)
    dp = _assemble_dp(dq, dkp, dvp, du, cos, sin)
    dpc = jnp.concatenate([jnp.zeros((n_ctx, ATTN_WIDTH), BF16), dkc.astype(BF16), dvc.astype(BF16),
                           jnp.zeros((n_ctx, POOL_WIDTH), BF16)], axis=1)
    grad_x, st_mix = _mixer_dx("mixer_dx", dp, win_b, xs, dx1, norm_attn_w, sc_a)
    _, st_ctx = _mixer_dx("mixer_dx_ctx", dpc, win_b, cx, jnp.zeros((n_ctx, d), F32), norm_attn_w, csc_a)
    st_mix, st_ctx = _colsum(st_mix), _colsum(st_ctx)
    g_win = _mm_tn("grad_w_in", h, dp, BF16, tmo=d, tn=dp.shape[1] // 2, tt=_pick(length, 1024, 512, 256, 128), more=(hc, dpc))

    zrow = jnp.zeros((d,), F32)
    pad = lambda a: jnp.pad(a, (0, d - a.shape[0]))
    mine_rows = [st_mix[0], st_mix[1], st_mlp[3], st_mlp[0], st_mlp[1], st_loss[2],
                 st_ctx[0], st_ctx[1],
                 st_mix[2] + st_ctx[2], st_mlp[2], st_loss[1],
                 pad(jnp.sum(st_pool, axis=0)), pad(dsink[0, :N_Q_HEADS])] + [zrow] * 3
    small_all = _allgather8("gather_small", jnp.concatenate(mine_rows).reshape(len(mine_rows), d))
    small = small_all[0]
    for i in range(1, 8):
        small = small + small_all[i]
    dm_rows = small_all[:, 0:6, :].reshape(8, 6 * d)
    dm_ctx = jnp.concatenate([small[6], small[7], jnp.zeros((4 * d,), F32)])[None, :]
    dm = jnp.concatenate([dm_rows, jnp.pad(dm_ctx, ((0, 7), (0, 0)))], axis=0)
    g_bada = jnp.sum(dm[:9], axis=0, keepdims=True)
    dm_shard = lax.dynamic_slice_in_dim(dm, chip * n_ada, n_ada, axis=1)
    g_wada, dl_wada, nm_wada, nv_wada, part_cctx = _ada_bwd(cond, dm_shard, w_ada[0], m_w_ada[0], v_w_ada[0])
    cctx_all = _allgather8("gather_cctx", part_cctx)
    dsilu_in = cctx_all[0, 0] + cctx_all[2, 0] + cctx_all[4, 0] + cctx_all[6, 0]
    sig = 1.0 / (1.0 + jnp.exp(-c_ctx))
    g_cctx = dsilu_in * (sig * (1.0 + c_ctx * (1.0 - sig)))

    g_shards = _reduce_grads(bigs, [g_win, g_pw.astype(BF16), g_wout.reshape(bigs[2].full_shape),
                                    g_wup, g_wdn.reshape(bigs[4].full_shape)], pos)
    big_w = [w_in, pool_w, w_out, w_mlp_up, w_mlp_down]
    big_m = [m_w_in, m_pool_w, m_w_out, m_w_mlp_up, m_w_mlp_down]
    big_v = [v_w_in, v_pool_w, v_w_out, v_w_mlp_up, v_w_mlp_down]
    big_names = ["w_in", "pool_w", "w_out", "w_mlp_up", "w_mlp_down"]
    res = {}
    for nm, w_, g_, m_, v_ in zip(big_names, big_w, g_shards, big_m, big_v):
        g_ = g_.reshape(w_.shape)
        res[nm] = (g_,) + tuple(_adamw("adamw_" + nm, w_, g_, m_, v_))
    res["w_ada"] = (g_wada[None], dl_wada[None], nm_wada[None], nv_wada[None])

    def pack(cc, na, nm_, ba, sk, ps, fn):
        flat = [cc.reshape(-1), na.reshape(-1), nm_.reshape(-1), ba.reshape(-1), pad(sk.reshape(-1)), pad(ps.reshape(-1)),
                fn.reshape(-1), jnp.zeros((4 * d,), F32)]
        return jnp.concatenate(flat).reshape(16, d)

    w_s = pack(c_ctx, norm_attn_w, norm_mlp_w, b_ada, attn_sink, pool_scale, final_norm_w)
    m_s = pack(m_c_ctx, m_norm_attn_w, m_norm_mlp_w, m_b_ada, m_attn_sink, m_pool_scale, m_final_norm_w)
    v_s = pack(v_c_ctx, v_norm_attn_w, v_norm_mlp_w, v_b_ada, v_attn_sink, v_pool_scale, v_final_norm_w)
    g_s = pack(g_cctx, small[8], small[9], g_bada, small[12][:N_Q_HEADS], small[11][:POOL_WIDTH], small[10])
    small_out = [g_s] + _adamw("adamw_small", w_s, g_s, m_s, v_s)

    def unpack(p):
        return {"c_ctx": p[0], "norm_attn_w": p[1:2], "norm_mlp_w": p[2:3], "b_ada": p[3:9].reshape(1, 6 * d),
                "attn_sink": p[9:10, :N_Q_HEADS], "pool_scale": p[10:11, :POOL_WIDTH], "final_norm_w": p[11]}

    small_res = [unpack(p) for p in small_out]
    order = ["c_ctx", "norm_attn_w", "norm_mlp_w", "w_ada", "b_ada", "w_in", "attn_sink", "pool_w", "pool_scale",
             "w_out", "w_mlp_up", "w_mlp_down", "final_norm_w"]
    outs = [loss, grad_x.reshape(x.shape)]
    for kind in range(4):
        for nm in order:
            outs.append(res[nm][kind] if nm in res else small_res[kind][nm])
    return tuple(outs)
```

```python
import functools

import jax
import jax.numpy as jnp
from jax import lax
from jax.experimental import pallas as pl
from jax.experimental.pallas import tpu as pltpu

F32 = jnp.float32
BF16 = jnp.bfloat16
EPS = 1e-6
NEG_INF = -1e30
HEAD_DIM = 64
N_Q_HEADS = 16
N_KV_HEADS = 4
GROUP = N_Q_HEADS // N_KV_HEADS
ATTN_WIDTH = N_Q_HEADS * HEAD_DIM
KV_WIDTH = N_KV_HEADS * HEAD_DIM
POOL_WINDOWS = (2, 4, 8, 16)
POOL_GROUP_DIM = 256
POOL_WIDTH = len(POOL_WINDOWS) * POOL_GROUP_DIM
BLOCK = 128
GRID_W = 64
ROPE_BASE = 10000.0
SCALE = HEAD_DIM ** -0.5
HALO = 16
ROWS = 64
ADAM_LR, ADAM_B1, ADAM_B2, ADAM_EPS, ADAM_WD, ADAM_STEP = 0.001, 0.9, 0.999, 1e-08, 0.01, 10
MESH = pl.DeviceIdType.MESH
MIB = 1024 * 1024
ANY = pl.BlockSpec(memory_space=pl.ANY)


def _cp(n_axes, vmem_mib=48):
    return pltpu.CompilerParams(dimension_semantics=("arbitrary",) * n_axes, vmem_limit_bytes=vmem_mib * MIB)


def _row_loop(rows, fn):
    def body(r, carry):
        fn(pl.ds(pl.multiple_of(r * ROWS, ROWS), ROWS))
        return carry

    lax.fori_loop(0, rows // ROWS, body, 0)


def _fold8(v):
    s = v[0:8]
    for t in range(1, v.shape[0] // 8):
        s = s + v[8 * t:8 * t + 8]
    return s


def _dot(a, b):
    return jnp.dot(a, b, preferred_element_type=F32)


def _dot_nt(a, b):
    return lax.dot_general(a, b, (((1,), (1,)), ((), ())), preferred_element_type=F32)


def _dot_tn(a, b):
    return lax.dot_general(a, b, (((0,), (0,)), ((), ())), preferred_element_type=F32)


def _pick(n, *cands):
    for t in cands:
        if n % t == 0:
            return t
    return n


def _flip(pos, mask):
    return tuple((1 - v) if (mask >> (2 - i)) & 1 else v for i, v in enumerate(pos))


def _exchange(name, ins, out_shapes, remote, local=(), aliases=None):
    n_io = len(ins) + len(out_shapes)

    def body(*refs):
        io = refs[:n_io]
        send_sems, recv_sems, local_sems = refs[n_io:]
        me = (lax.axis_index("x"), lax.axis_index("y"), lax.axis_index("c"))

        def copy(i, sender):
            mask, src_fn, dst_fn = remote[i]
            return pltpu.make_async_remote_copy(
                src_ref=src_fn(io, sender), dst_ref=dst_fn(io, sender), send_sem=send_sems.at[i],
                recv_sem=recv_sems.at[i], device_id=_flip(sender, mask), device_id_type=MESH)

        own = [pltpu.make_async_copy(s(io, me), d(io, me), local_sems.at[i]) for i, (s, d) in enumerate(local)]
        for cp in own:
            cp.start()
        sends = [copy(i, me) for i in range(len(remote))]
        for cp in sends:
            cp.start()
        for i in range(len(remote)):
            copy(i, _flip(me, remote[i][0])).wait_recv()
        for cp in sends:
            cp.wait_send()
        for cp in own:
            cp.wait()

    return pl.pallas_call(
        body, name=name, out_shape=tuple(out_shapes),
        in_specs=[ANY] * len(ins), out_specs=tuple([ANY] * len(out_shapes)),
        scratch_shapes=[pltpu.SemaphoreType.DMA((len(remote),)), pltpu.SemaphoreType.DMA((len(remote),)),
                        pltpu.SemaphoreType.DMA((max(len(local), 1),))],
        input_output_aliases=aliases or {},
    )(*ins)


HBM = pl.BlockSpec(memory_space=pltpu.HBM)
SEM = pl.BlockSpec(memory_space=pltpu.SEMAPHORE)
EFFECT = pltpu.SideEffectType.DATAFLOW_SIDE_EFFECTING


def _split_copy(remote, i, io, send_sems, recv_sems, sender):
    mask, src_fn, dst_fn = remote[i]
    return pltpu.make_async_remote_copy(
        src_ref=src_fn(io, sender), dst_ref=dst_fn(io, sender), send_sem=send_sems.at[i],
        recv_sem=recv_sems.at[i], device_id=_flip(sender, mask), device_id_type=MESH)


def _exchange_start(name, bufs, remote):
    n, r = len(bufs), len(remote)

    def body(*refs):
        io, send_sems, recv_sems, token = refs[:n], refs[2 * n], refs[2 * n + 1], refs[2 * n + 2]
        me = (lax.axis_index("x"), lax.axis_index("y"), lax.axis_index("c"))
        for i in range(r):
            _split_copy(remote, i, io, send_sems, recv_sems, me).start()
        token[...] = jnp.zeros_like(token)

    res = pl.pallas_call(
        body, name=name,
        out_shape=tuple(pltpu.HBM(b.shape, b.dtype) for b in bufs)
        + (pltpu.SemaphoreType.DMA((r,)), pltpu.SemaphoreType.DMA((r,)), jax.ShapeDtypeStruct((8, 128), F32)),
        in_specs=[HBM] * n, out_specs=tuple([HBM] * n) + (SEM, SEM, pl.BlockSpec(memory_space=pltpu.VMEM)),
        input_output_aliases={i: i for i in range(n)}, compiler_params=pltpu.CompilerParams(has_side_effects=EFFECT),
    )(*[pltpu.with_memory_space_constraint(b, pltpu.HBM) for b in bufs])
    return list(res[:n]), res[n], res[n + 1], res[n + 2]


def _exchange_wait(name, bufs, send_sems, recv_sems, remote, after):
    n, r = len(bufs), len(remote)

    def body(*refs):
        io, ss, rs = refs[:n], refs[n], refs[n + 1]
        me = (lax.axis_index("x"), lax.axis_index("y"), lax.axis_index("c"))
        for i in range(r):
            _split_copy(remote, i, io, ss, rs, _flip(me, remote[i][0])).wait_recv()
        for i in range(r):
            _split_copy(remote, i, io, ss, rs, me).wait_send()

    return list(pl.pallas_call(
        body, name=name, out_shape=tuple(pltpu.HBM(b.shape, b.dtype) for b in bufs),
        in_specs=[HBM] * n + [SEM, SEM, ANY], out_specs=tuple([HBM] * n),
        input_output_aliases={i: i for i in range(n)}, compiler_params=pltpu.CompilerParams(has_side_effects=EFFECT),
    )(*bufs, send_sems, recv_sems, after))


def _my_c():
    return lax.axis_index("c")


def _my_chip():
    return 2 * lax.axis_index("x") + lax.axis_index("y")


def _dev_index(pos):
    return 4 * pos[0] + 2 * pos[1] + pos[2]


def _chip_index(pos):
    return 2 * pos[0] + pos[1]


def _allgather8(name, v):
    out = jax.ShapeDtypeStruct((8,) + v.shape, v.dtype)
    remote = [(mask, lambda io, pos: io[0], lambda io, pos: io[1].at[_dev_index(pos)]) for mask in range(1, 8)]
    local = [(lambda io, pos: io[0], lambda io, pos: io[1].at[_dev_index(pos)])]
    return _exchange(name, [v], [out], remote, local)[0]


class _Big:
    def __init__(self, kind, shard_shape):
        self.kind = kind
        self.shard_shape = tuple(shard_shape)
        if kind == "col":
            r, cs = shard_shape
            self.full_shape = (r, 4 * cs)
            self.piece_shape = (r // 2, cs)
            self.half_shape = (r // 2, 4 * cs)
        elif kind == "row":
            rs, c = shard_shape
            self.full_shape = (4, 2, rs // 2, c)
            self.piece_shape = (1, 1, rs // 2, c)
            self.half_shape = (4, 1, rs // 2, c)
        else:
            self.full_shape = (4, 256, 256)
            self.piece_shape = (2, 64, 256)
            self.half_shape = (2, 256, 256)

    def shard_as_pieces(self, a):
        return a.reshape((1, 2) + self.piece_shape[2:]) if self.kind == "row" else a

    def piece(self, ref, k, h):
        if self.kind == "col":
            r, cs = self.piece_shape
            return ref.at[pl.ds(h * r, r), pl.ds(k * cs, cs)]
        if self.kind == "row":
            return ref.at[pl.ds(k, 1), pl.ds(h, 1)]
        return ref.at[pl.ds(2 * h, 2), pl.ds(64 * k, 64)]

    def half_of_shard(self, ref, h):
        if self.kind == "col":
            return ref.at[pl.ds(h * self.piece_shape[0], self.piece_shape[0])]
        if self.kind == "row":
            return ref.at[:, pl.ds(h, 1)]
        return ref.at[pl.ds(2 * h, 2)]

    def half_of_full(self, ref, h):
        if self.kind == "col":
            return ref.at[pl.ds(h * self.half_shape[0], self.half_shape[0])]
        if self.kind == "row":
            return ref.at[:, pl.ds(h, 1)]
        return ref.at[pl.ds(2 * h, 2)]

    def piece_of_half(self, ref, k):
        if self.kind == "col":
            return ref.at[:, pl.ds(k * self.piece_shape[1], self.piece_shape[1])]
        if self.kind == "row":
            return ref.at[pl.ds(k, 1)]
        return ref.at[:, pl.ds(64 * k, 64)]


CHIP_MASKS = (4, 2, 6)


def _cast_place(name, big, shard, after):
    if big.kind == "col":
        r, cs = big.shard_shape
        tr = _pick(r, 512, 256, 128)
        src, grid, blk = shard, (r // tr,), (tr, cs)
        imap, omap = (lambda i: (i, 0)), (lambda i: (i, _my_chip()))
    elif big.kind == "row":
        rs, c = big.shard_shape
        tr = _pick(rs // 2, 256, 128)
        src, grid, blk = big.shard_as_pieces(shard), (2, rs // 2 // tr), (1, 1, tr, c)
        imap, omap = (lambda h, i: (0, h, i, 0)), (lambda h, i: (_my_chip(), h, i, 0))
    else:
        src, grid, blk = shard, (1,), big.shard_shape
        imap, omap = (lambda i: (0, 0, 0)), (lambda i: (0, _my_chip(), 0))

    def body(s_ref, after_ref, o_ref):
        o_ref[...] = s_ref[...].astype(BF16)

    return pl.pallas_call(
        body, name=name, grid=grid, in_specs=[pl.BlockSpec(blk, imap), ANY], out_specs=pl.BlockSpec(blk, omap),
        out_shape=jax.ShapeDtypeStruct(big.full_shape, BF16), compiler_params=_cp(len(grid)),
    )(src, after)


def _gather_ici_remote(bigs, off):
    remote = []
    for a, b in enumerate(bigs):
        for mask in CHIP_MASKS:
            def mine(io, p, a=a, b=b):
                return b.piece(io[off + a], _chip_index(p), p[2])
            remote.append((mask, mine, mine))
    return remote


def _gather_d2d_remote(bigs, off):
    remote = []
    for a, b in enumerate(bigs):
        for mask in CHIP_MASKS:
            def region(io, p, a=a, b=b, mask=mask):
                return b.piece(io[off + a], _chip_index(_flip(p, mask)), p[2])
            remote.append((1, region, region))
    return remote


def _gather_weights(tag, bigs, placed):
    n = len(bigs)
    fulls = [jax.ShapeDtypeStruct(b.full_shape, BF16) for b in bigs]
    alias = {a: a for a in range(n)}
    got = _exchange(f"gather_{tag}_ici", placed, fulls, _gather_ici_remote(bigs, n), aliases=alias)
    return _exchange(f"gather_{tag}_d2d", list(got), fulls, _gather_d2d_remote(bigs, n), aliases=alias)


def _ew(name, fn, ins, out_dtypes, rows_per_step=256):
    shape = ins[0].shape
    last = shape[-1]
    rows = 1
    for s in shape[:-1]:
        rows *= s
    ins2 = [a.reshape(rows, last) for a in ins]
    tr = _pick(rows, rows_per_step, 128, 64, 32, 16, 8)
    spec = pl.BlockSpec((tr, last), lambda i: (i, 0))

    def body(*refs):
        outs = fn(*[r[...] for r in refs[:len(ins)]])
        for o_ref, o in zip(refs[len(ins):], outs):
            o_ref[...] = o.astype(o_ref.dtype)

    outs = pl.pallas_call(
        body, name=name, grid=(rows // tr,), in_specs=[spec] * len(ins), out_specs=tuple([spec] * len(out_dtypes)),
        out_shape=tuple(jax.ShapeDtypeStruct((rows, last), d) for d in out_dtypes), compiler_params=_cp(1),
    )(*ins2)
    return [o.reshape(shape) for o in outs]


def _chip_sum(name, big, grad, from_sibling):
    if big.kind == "col":
        rh, w = big.half_shape
        tr = _pick(rh, 256, 128)
        nb = rh // tr
        grid, blk = (nb,), (tr, w)
        gmap, hmap = (lambda i: (_my_c() * nb + i, 0)), (lambda i: (i, 0))
    elif big.kind == "row":
        rh, w = big.half_shape[2:]
        tr = _pick(rh, 256, 128)
        grid, blk = (4, rh // tr), (1, 1, tr, w)
        gmap, hmap = (lambda k, i: (k, _my_c(), i, 0)), (lambda k, i: (k, 0, i, 0))
    else:
        grid, blk = (1,), big.half_shape
        gmap, hmap = (lambda i: (_my_c(), 0, 0)), (lambda i: (0, 0, 0))

    def body(g_ref, s_ref, o_ref):
        o_ref[...] = (g_ref[...].astype(F32) + s_ref[...].astype(F32)).astype(BF16)

    return pl.pallas_call(
        body, name=name, grid=grid, in_specs=[pl.BlockSpec(blk, gmap), pl.BlockSpec(blk, hmap)],
        out_specs=pl.BlockSpec(blk, hmap), out_shape=jax.ShapeDtypeStruct(big.half_shape, BF16), compiler_params=_cp(len(grid)),
    )(grad, from_sibling)


def _piece_sum(name, big, chip_sum, thirds):
    if big.kind == "col":
        rp, cs = big.piece_shape
        tr = _pick(rp, 256, 128)
        nb = rp // tr
        grid, blk, tblk = (nb,), (tr, cs), (1, tr, cs)
        smap, omap = (lambda i: (i, _my_chip())), (lambda i: (_my_c() * nb + i, 0))
        tmap = lambda j: (lambda i: (j, i, 0))
        out_shape = big.shard_shape
    elif big.kind == "row":
        rp, w = big.piece_shape[2:]
        tr = _pick(rp, 256, 128)
        grid, blk, tblk = (rp // tr,), (1, 1, tr, w), (1, 1, 1, tr, w)
        smap, omap = (lambda i: (_my_chip(), 0, i, 0)), (lambda i: (0, _my_c(), i, 0))
        tmap = lambda j: (lambda i: (j, 0, 0, i, 0))
        out_shape = (1, 2, rp, w)
    else:
        grid, blk, tblk = (1,), big.piece_shape, (1,) + big.piece_shape
        smap, omap = (lambda i: (0, _my_chip(), 0)), (lambda i: (_my_c(), 0, 0))
        tmap = lambda j: (lambda i: (j, 0, 0, 0))
        out_shape = big.shard_shape

    def body(s_ref, t0, t1, t2, o_ref):
        o_ref[...] = s_ref[...].astype(F32) + t0[0].astype(F32) + t1[0].astype(F32) + t2[0].astype(F32)

    return pl.pallas_call(
        body, name=name, grid=grid,
        in_specs=[pl.BlockSpec(blk, smap)] + [pl.BlockSpec(tblk, tmap(j)) for j in range(3)],
        out_specs=pl.BlockSpec(blk, omap), out_shape=jax.ShapeDtypeStruct(out_shape, F32), compiler_params=_cp(len(grid)),
    )(chip_sum, thirds, thirds, thirds)


def _reduce_to_chip(tag, bigs, grads):
    n = len(bigs)
    halves = [jax.ShapeDtypeStruct(b.half_shape, BF16) for b in bigs]
    remote = [(1, lambda io, p, a=a, b=b: b.half_of_full(io[a], 1 - p[2]), lambda io, p, a=a: io[n + a])
              for a, b in enumerate(bigs)]
    from_sibling = _exchange(f"reduce_{tag}_d2d", grads, halves, remote)
    return [_chip_sum(f"reduce_{tag}_chip_sum_{a}", b, g, r) for a, (b, g, r) in enumerate(zip(bigs, grads, from_sibling))]


def _reduce_ici_remote(bigs):
    n = len(bigs)
    remote = []
    for a, b in enumerate(bigs):
        for j, mask in enumerate(CHIP_MASKS):
            remote.append((mask,
                           lambda io, p, a=a, b=b, mask=mask: b.piece_of_half(io[a], _chip_index(_flip(p, mask))),
                           lambda io, p, a=a, j=j: io[n + a].at[j]))
    return remote


def _thirds(bigs):
    return [jax.ShapeDtypeStruct((3,) + b.piece_shape, BF16) for b in bigs]


def _reduce_finish(tag, bigs, chip_sum, from_chips):
    n = len(bigs)
    placed = [_piece_sum(f"reduce_{tag}_sum_{a}", b, s, r) for a, (b, s, r) in enumerate(zip(bigs, chip_sum, from_chips))]
    remote = []
    for a, b in enumerate(bigs):
        def mine(io, p, a=a, b=b):
            return b.half_of_shard(io[n + a], p[2])
        remote.append((1, mine, mine))
    out = _exchange(f"reduce_{tag}_share_d2d", placed, [jax.ShapeDtypeStruct(p.shape, F32) for p in placed], remote,
                    aliases={a: a for a in range(n)})
    return [o.reshape(b.shard_shape) for o, b in zip(out, bigs)]


def _mm(name, a, b, *, nt, tm, tn, tk, epi, extras=(), extra_specs=(), out_shape, out_specs, vmem_mib=48):
    m, kdim = a.shape
    n = b.shape[0] if nt else b.shape[1]
    gm, gn, gk = m // tm, n // tn, kdim // tk
    a_spec = pl.BlockSpec((tm, tk), lambda j, i, k: (i, k))
    b_spec = pl.BlockSpec((tn, tk), lambda j, i, k: (j, k)) if nt else pl.BlockSpec((tk, tn), lambda j, i, k: (k, j))
    n_ex = len(extras)

    def body(a_ref, b_ref, *rest):
        ex, outs, acc = rest[:n_ex], rest[n_ex:-1], rest[-1]
        dot = _dot_nt if nt else _dot
        if gk == 1:
            acc[...] = dot(a_ref[...], b_ref[...])
            epi(acc, ex, outs)
        else:
            k = pl.program_id(2)

            @pl.when(k == 0)
            def _():
                acc[...] = jnp.zeros_like(acc)

            acc[...] += dot(a_ref[...], b_ref[...])

            @pl.when(k == gk - 1)
            def _():
                epi(acc, ex, outs)

    return pl.pallas_call(
        body, name=name, grid=(gn, gm, gk), in_specs=[a_spec, b_spec, *extra_specs], out_specs=tuple(out_specs),
        out_shape=tuple(out_shape), scratch_shapes=[pltpu.VMEM((tm, tn), F32)], compiler_params=_cp(3, vmem_mib),
    )(a, b, *extras)


def _mm_tn(name, a, b, out_dtype, *, tmo, tn, tt, more=(), vmem_mib=56):
    t, m = a.shape
    n = b.shape[1]
    gt = t // tt

    def body(a_ref, b_ref, *rest):
        o_ref, acc = rest[-2:]
        k = pl.program_id(2)

        @pl.when(k == 0)
        def _():
            acc[...] = _dot_tn(rest[0][...], rest[1][...]) if more else jnp.zeros_like(acc)

        acc[...] += _dot_tn(a_ref[...], b_ref[...])

        @pl.when(k == gt - 1)
        def _():
            o_ref[...] = acc[...].astype(o_ref.dtype)

    more_specs = [pl.BlockSpec((more[0].shape[0], tmo), lambda i, j, k: (0, i)),
                  pl.BlockSpec((more[1].shape[0], tn), lambda i, j, k: (0, j))] if more else []
    return pl.pallas_call(
        body, name=name, grid=(m // tmo, n // tn, gt),
        in_specs=[pl.BlockSpec((tt, tmo), lambda i, j, k: (k, i)), pl.BlockSpec((tt, tn), lambda i, j, k: (k, j))] + more_specs,
        out_specs=pl.BlockSpec((tmo, tn), lambda i, j, k: (i, j)), out_shape=jax.ShapeDtypeStruct((m, n), out_dtype),
        scratch_shapes=[pltpu.VMEM((tmo, tn), F32)], compiler_params=_cp(3, vmem_mib),
    )(a, b, *more)


def _row_spec(d):
    return pl.BlockSpec((1, d), lambda *_: (0, 0))


def _stat_spec(k, d):
    return pl.BlockSpec((k, 8, d), lambda *_: (0, 0, 0))


def _rope(z, cs, sn):
    first = (lax.broadcasted_iota(jnp.int32, (z.shape[0], 128), 1) % 32) < 16
    outs = []
    for j in range(z.shape[1] // 128):
        zc = z[:, 128 * j:128 * (j + 1)]
        partner = jnp.where(first, pltpu.roll(zc, 112, 1), pltpu.roll(zc, 16, 1))
        outs.append(zc * cs + partner * sn)
    return outs[0] if len(outs) == 1 else jnp.concatenate(outs, axis=1)


def _rope_tables(length, rotate):
    if not rotate:
        return jnp.ones((length, 128), F32), jnp.zeros((length, 128), F32)
    half = HEAD_DIM // 2
    inv_freq = ROPE_BASE ** (-jnp.arange(0, half, 2, dtype=F32) / half)
    t = jnp.arange(length)
    row = (t // GRID_W).astype(F32)
    col = (t % GRID_W).astype(F32)
    e = jnp.arange(128) % HEAD_DIM
    pos = jnp.where(e[None, :] < half, row[:, None], col[:, None])
    ang = pos * inv_freq[(e % half) % (half // 2)][None, :]
    first = ((e % half) < half // 2)[None, :]
    return jnp.cos(ang), jnp.where(first, -jnp.sin(ang), jnp.sin(ang))


def _mixer_in(name, x, nw, sh, sc, w_in, cos, sin, after):
    t, d = x.shape
    tm = _pick(t, 256, 128)
    n_in = w_in.shape[1]

    def body(x_ref, nw_ref, sh_ref, sc_ref, w_ref, cos_ref, sin_ref, after_ref, h_ref, q_ref, k_ref, v_ref, u_ref):
        xf = x_ref[...]
        r = lax.rsqrt(jnp.mean(xf * xf, axis=-1, keepdims=True) + EPS)
        hb = (((xf * r) * nw_ref[...]) * (1.0 + sc_ref[...]) + sh_ref[...]).astype(BF16)
        h_ref[...] = hb
        p = _dot(hb, w_ref[...])
        cs, sn = cos_ref[...], sin_ref[...]
        q_ref[...] = _rope(p[:, :ATTN_WIDTH], cs, sn).astype(BF16)
        k_ref[...] = _rope(p[:, ATTN_WIDTH:ATTN_WIDTH + KV_WIDTH], cs, sn).astype(BF16)
        v_ref[...] = p[:, ATTN_WIDTH + KV_WIDTH:ATTN_WIDTH + 2 * KV_WIDTH].astype(BF16)
        u_ref[...] = p[:, ATTN_WIDTH + 2 * KV_WIDTH:]

    def tile(w):
        return pl.BlockSpec((tm, w), lambda i: (i, 0))

    return pl.pallas_call(
        body, name=name, grid=(t // tm,),
        in_specs=[tile(d), _row_spec(d), _row_spec(d), _row_spec(d), pl.BlockSpec((d, n_in), lambda i: (0, 0)),
                  tile(128), tile(128), ANY],
        out_specs=(tile(d), tile(ATTN_WIDTH), tile(KV_WIDTH), tile(KV_WIDTH), tile(POOL_WIDTH)),
        out_shape=(jax.ShapeDtypeStruct((t, d), BF16), jax.ShapeDtypeStruct((t, ATTN_WIDTH), BF16),
                   jax.ShapeDtypeStruct((t, KV_WIDTH), BF16), jax.ShapeDtypeStruct((t, KV_WIDTH), BF16),
                   jax.ShapeDtypeStruct((t, POOL_WIDTH), F32)),
        compiler_params=_cp(1),
    )(x, nw, sh, sc, w_in, cos, sin, after)


def _attn_specs(nb, n_ctx):
    def blk(w, f):
        return pl.BlockSpec((BLOCK, w), lambda n: (f(n), 0))

    prev = lambda n: jnp.maximum(n - 1, 0)
    cur = lambda n: n
    nxt = lambda n: jnp.minimum(n + 1, nb - 1)
    kv = [blk(KV_WIDTH, prev), blk(KV_WIDTH, cur), blk(KV_WIDTH, nxt)]
    ctx = pl.BlockSpec((n_ctx, KV_WIDTH), lambda n: (0, 0))
    return [pl.BlockSpec(memory_space=pltpu.SMEM), blk(ATTN_WIDTH, cur)] + kv + kv + [ctx, ctx]


def _attn_mask(n, length, n_keys):
    row = lax.broadcasted_iota(jnp.int32, (GROUP * BLOCK, n_keys), 0) % BLOCK
    col = lax.broadcasted_iota(jnp.int32, (GROUP * BLOCK, n_keys), 1)
    kpos = (n - 1) * BLOCK + col
    return ((jnp.abs(col - BLOCK - row) <= BLOCK) & (kpos >= 0) & (kpos < length)) | (col >= 3 * BLOCK)


def _group_rows(block, g):
    return jnp.concatenate([block[:, HEAD_DIM * h:HEAD_DIM * (h + 1)] for h in range(GROUP * g, GROUP * (g + 1))], axis=0)


def _group_sink(sink_ref, g):
    head = lax.broadcasted_iota(jnp.int32, (GROUP * BLOCK, 1), 0) // BLOCK
    out = jnp.full((GROUP * BLOCK, 1), sink_ref[0, GROUP * g], F32)
    for j in range(1, GROUP):
        out = jnp.where(head == j, sink_ref[0, GROUP * g + j], out)
    return out


def _attn_fwd(q, k, v, kc, vc, sink):
    length = q.shape[0]
    nb = length // BLOCK
    n_ctx = kc.shape[0]
    n_keys = 3 * BLOCK + n_ctx

    def body(sink_ref, q_ref, kp, k0, kn, vp, v0, vn, kc_ref, vc_ref, o_ref):
        n = pl.program_id(0)
        valid = _attn_mask(n, length, n_keys)
        qb = q_ref[...]
        kall = jnp.concatenate([kp[...], k0[...], kn[...], kc_ref[...]], axis=0)
        vall = jnp.concatenate([vp[...], v0[...], vn[...], vc_ref[...]], axis=0)
        outs = []
        for g in range(N_KV_HEADS):
            lanes = slice(HEAD_DIM * g, HEAD_DIM * (g + 1))
            s = jnp.where(valid, _dot_nt(_group_rows(qb, g), kall[:, lanes]) * SCALE, NEG_INF)
            sk = _group_sink(sink_ref, g)
            m = jnp.maximum(jnp.max(s, axis=-1, keepdims=True), sk)
            e = jnp.exp(s - m)
            den = jnp.sum(e, axis=-1, keepdims=True) + jnp.exp(sk - m)
            o = _dot(e.astype(BF16), vall[:, lanes]) / den
            outs += [o[BLOCK * j:BLOCK * (j + 1)] for j in range(GROUP)]
        o_ref[...] = jnp.concatenate(outs, axis=1).astype(BF16)

    return pl.pallas_call(
        body, name="attn_fwd", grid=(nb,), in_specs=_attn_specs(nb, n_ctx),
        out_specs=pl.BlockSpec((BLOCK, ATTN_WIDTH), lambda n: (n, 0)),
        out_shape=jax.ShapeDtypeStruct((length, ATTN_WIDTH + POOL_WIDTH), BF16), compiler_params=_cp(1),
    )(sink, q, k, k, k, v, v, v, kc, vc)


def _attn_bwd(q, k, v, kc, vc, sink, dmix):
    length = q.shape[0]
    nb = length // BLOCK
    n_ctx = kc.shape[0]
    n_keys = 3 * BLOCK + n_ctx

    def body(sink_ref, q_ref, kp, k0, kn, vp, v0, vn, kc_ref, vc_ref, do_ref,
             dq_ref, dkp_ref, dvp_ref, dkc_ref, dvc_ref, dsink_ref):
        n = pl.program_id(0)

        @pl.when(n == 0)
        def _():
            dkc_ref[...] = jnp.zeros_like(dkc_ref)
            dvc_ref[...] = jnp.zeros_like(dvc_ref)
            dsink_ref[...] = jnp.zeros_like(dsink_ref)

        valid = _attn_mask(n, length, n_keys)
        qb, dob = q_ref[...], do_ref[...]
        kall = jnp.concatenate([kp[...], k0[...], kn[...], kc_ref[...]], axis=0)
        vall = jnp.concatenate([vp[...], v0[...], vn[...], vc_ref[...]], axis=0)
        srow = lax.broadcasted_iota(jnp.int32, (8, 128), 0)
        slane = lax.broadcasted_iota(jnp.int32, (8, 128), 1)
        dqs, dks, dvs = [], [], []
        dsink = jnp.zeros((8, 128), F32)
        for g in range(N_KV_HEADS):
            lanes = slice(HEAD_DIM * g, HEAD_DIM * (g + 1))
            kg, vg = kall[:, lanes], vall[:, lanes]
            qg, dog = _group_rows(qb, g), _group_rows(dob, g)
            s = jnp.where(valid, _dot_nt(qg, kg) * SCALE, NEG_INF)
            sk = _group_sink(sink_ref, g)
            m = jnp.maximum(jnp.max(s, axis=-1, keepdims=True), sk)
            e = jnp.exp(s - m)
            inv = 1.0 / (jnp.sum(e, axis=-1, keepdims=True) + jnp.exp(sk - m))
            p = e * inv
            dp = _dot_nt(dog, vg)
            delta = jnp.sum(p * dp, axis=-1, keepdims=True)
            ds = (p * (dp - delta) * SCALE).astype(BF16)
            dq = _dot(ds, kg)
            dqs += [dq[BLOCK * j:BLOCK * (j + 1)] for j in range(GROUP)]
            dks.append(_dot_tn(ds, qg))
            dvs.append(_dot_tn(p.astype(BF16), dog))
            d_sink = jnp.exp(sk - m) * inv * delta
            for j in range(GROUP):
                total = -jnp.sum(d_sink[BLOCK * j:BLOCK * (j + 1)], axis=0, keepdims=True)
                dsink = dsink + jnp.where((srow == 0) & (slane == GROUP * g + j), total, 0.0)
        dq_ref[...] = jnp.concatenate(dqs, axis=1)
        dk = jnp.concatenate(dks, axis=1)
        dv = jnp.concatenate(dvs, axis=1)
        for j in range(3):
            dkp_ref[0, j] = dk[BLOCK * j:BLOCK * (j + 1)]
            dvp_ref[0, j] = dv[BLOCK * j:BLOCK * (j + 1)]
        dkc_ref[...] += dk[3 * BLOCK:]
        dvc_ref[...] += dv[3 * BLOCK:]
        dsink_ref[...] += dsink

    part = pl.BlockSpec((1, 3, BLOCK, KV_WIDTH), lambda n: (n, 0, 0, 0))
    ctx = pl.BlockSpec((n_ctx, KV_WIDTH), lambda n: (0, 0))
    return pl.pallas_call(
        body, name="attn_bwd", grid=(nb,),
        in_specs=_attn_specs(nb, n_ctx) + [pl.BlockSpec((BLOCK, ATTN_WIDTH), lambda n: (n, 0))],
        out_specs=(pl.BlockSpec((BLOCK, ATTN_WIDTH), lambda n: (n, 0)), part, part, ctx, ctx,
                   pl.BlockSpec((8, 128), lambda n: (0, 0))),
        out_shape=(jax.ShapeDtypeStruct((length, ATTN_WIDTH), F32),
                   jax.ShapeDtypeStruct((nb, 3, BLOCK, KV_WIDTH), F32), jax.ShapeDtypeStruct((nb, 3, BLOCK, KV_WIDTH), F32),
                   jax.ShapeDtypeStruct((n_ctx, KV_WIDTH), F32), jax.ShapeDtypeStruct((n_ctx, KV_WIDTH), F32),
                   jax.ShapeDtypeStruct((8, 128), F32)),
        compiler_params=_cp(1),
    )(sink, q, k, k, k, v, v, v, kc, vc, dmix)


def _assemble_dp(dq, dkp, dvp, du, cos, sin):
    length = dq.shape[0]
    nb = length // BLOCK

    def body(dq_ref, dka, dkb, dkc, dva, dvb, dvc, du_ref, cos_ref, sin_ref, o_ref):
        n = pl.program_id(0)
        has_next = (n + 1 < nb).astype(F32)
        has_prev = (n > 0).astype(F32)
        cs, sn = cos_ref[...], -sin_ref[...]
        dk = dka[0, 0] * has_next + dkb[0, 0] + dkc[0, 0] * has_prev
        dv = dva[0, 0] * has_next + dvb[0, 0] + dvc[0, 0] * has_prev
        o_ref[:, :ATTN_WIDTH] = _rope(dq_ref[...], cs, sn).astype(BF16)
        o_ref[:, ATTN_WIDTH:ATTN_WIDTH + KV_WIDTH] = _rope(dk, cs, sn).astype(BF16)
        o_ref[:, ATTN_WIDTH + KV_WIDTH:ATTN_WIDTH + 2 * KV_WIDTH] = dv.astype(BF16)
        o_ref[:, ATTN_WIDTH + 2 * KV_WIDTH:] = du_ref[...]

    def part(slot, f):
        return pl.BlockSpec((1, 1, BLOCK, KV_WIDTH), lambda n: (f(n), slot, 0, 0))

    parts = [part(0, lambda n: jnp.minimum(n + 1, nb - 1)), part(1, lambda n: n), part(2, lambda n: jnp.maximum(n - 1, 0))]

    def tile(w):
        return pl.BlockSpec((BLOCK, w), lambda n: (n, 0))

    width = ATTN_WIDTH + 2 * KV_WIDTH + POOL_WIDTH
    return pl.pallas_call(
        body, name="assemble_dp", grid=(nb,),
        in_specs=[tile(ATTN_WIDTH)] + parts + parts + [tile(POOL_WIDTH), tile(128), tile(128)],
        out_specs=tile(width), out_shape=jax.ShapeDtypeStruct((length, width), BF16), compiler_params=_cp(1),
    )(dq, dkp, dkp, dkp, dvp, dvp, dvp, du, cos, sin)


def _shift_rows(e, s):
    n = e.shape[0]
    return e if s % n == 0 else pltpu.roll(e, (-s) % n, 0)


def _window_sum(e, w, first):
    s, n = e, 1
    while n < w:
        s = s + _shift_rows(s, n)
        n *= 2
    return _shift_rows(s, first)


def _pool_geometry(i, tm, length):
    pos = i * tm - HALO + lax.broadcasted_iota(jnp.int32, (tm + 2 * HALO, 1), 0)
    inside = (pos >= 0) & (pos < length)
    inv_counts = []
    for w in POOL_WINDOWS:
        lo = jnp.clip(pos - w // 2, 0, length)
        hi = jnp.clip(pos - w // 2 + w, 0, length)
        inv_counts.append(1.0 / jnp.maximum(hi - lo, 1).astype(F32))
    return inside, inv_counts


def _halo_specs(tm, width, length, col=0):
    per = tm // HALO
    last = length // HALO - 1
    return [pl.BlockSpec((HALO, width), lambda i: (jnp.maximum(i * per - 1, 0), col)),
            pl.BlockSpec((tm, width), lambda i: (i, col)),
            pl.BlockSpec((HALO, width), lambda i: (jnp.minimum((i + 1) * per, last), col))]


def _pooled(ext, inv_counts, tm):
    outs = []
    for g, w in enumerate(POOL_WINDOWS):
        e = ext[:, POOL_GROUP_DIM * g:POOL_GROUP_DIM * (g + 1)]
        mean = _window_sum(e, w, -(w // 2)) * inv_counts[g]
        outs.append((mean - e)[HALO:HALO + tm])
    return outs


def _pool_fwd(u, pool_w, pool_scale, mix):
    length = u.shape[0]
    tm = _pick(length, 256, 128)

    def body(up, u0, un, w_ref, sc_ref, mix_ref, o_ref):
        inside, inv_counts = _pool_geometry(pl.program_id(0), tm, length)
        ext = jnp.where(inside, jnp.concatenate([up[...], u0[...], un[...]], axis=0), 0.0)
        pooled = _pooled(ext, inv_counts, tm)
        mixed = [_dot(pooled[g].astype(BF16), w_ref[g]) for g in range(len(POOL_WINDOWS))]
        o_ref[...] = (jnp.concatenate(mixed, axis=1) * sc_ref[...]).astype(BF16)

    return pl.pallas_call(
        body, name="pool_fwd", grid=(length // tm,),
        in_specs=_halo_specs(tm, POOL_WIDTH, length) + [pl.BlockSpec(pool_w.shape, lambda i: (0, 0, 0)), _row_spec(POOL_WIDTH), ANY],
        out_specs=pl.BlockSpec((tm, POOL_WIDTH), lambda i: (i, 1)),
        out_shape=jax.ShapeDtypeStruct(mix.shape, BF16), input_output_aliases={5: 0}, compiler_params=_cp(1),
    )(u, u, u, pool_w, pool_scale, mix)


def _pool_bwd(u, dmix, pool_w, pool_scale):
    length = u.shape[0]
    tm = _pick(length, 256, 128)
    n_g = len(POOL_WINDOWS)

    def body(up, u0, un, dp_, d0, dn_, w_ref, sc_ref, du_ref, dw_ref, dsc_ref):
        i = pl.program_id(0)

        @pl.when(i == 0)
        def _():
            dw_ref[...] = jnp.zeros_like(dw_ref)
            dsc_ref[...] = jnp.zeros_like(dsc_ref)

        inside, inv_counts = _pool_geometry(i, tm, length)
        ext = jnp.where(inside, jnp.concatenate([up[...], u0[...], un[...]], axis=0), 0.0)
        dext = jnp.where(inside, jnp.concatenate([dp_[...], d0[...], dn_[...]], axis=0).astype(F32), 0.0)
        dmixed = (dext * sc_ref[...]).astype(BF16)
        pooled = _pooled(ext, inv_counts, tm)
        dus, dscs = [], []
        for g, w in enumerate(POOL_WINDOWS):
            lanes = slice(POOL_GROUP_DIM * g, POOL_GROUP_DIM * (g + 1))
            dpooled = _dot_nt(dmixed[:, lanes], w_ref[g])
            spread = _window_sum(dpooled * inv_counts[g], w, -(w // 2 - 1))
            dus.append((spread - dpooled)[HALO:HALO + tm])
            pb = pooled[g].astype(BF16)
            dw_ref[g] += _dot_tn(pb, dmixed[HALO:HALO + tm, lanes])
            prod = dext[HALO:HALO + tm, lanes] * _dot(pb, w_ref[g])
            dscs.append(_fold8(prod))
        du_ref[...] = jnp.concatenate(dus, axis=1).astype(BF16)
        dsc_ref[...] += jnp.concatenate(dscs, axis=1)

    return pl.pallas_call(
        body, name="pool_bwd", grid=(length // tm,),
        in_specs=_halo_specs(tm, POOL_WIDTH, length) + _halo_specs(tm, POOL_WIDTH, length, col=1)
        + [pl.BlockSpec(pool_w.shape, lambda i: (0, 0, 0)), _row_spec(POOL_WIDTH)],
        out_specs=(pl.BlockSpec((tm, POOL_WIDTH), lambda i: (i, 0)), pl.BlockSpec((n_g, POOL_GROUP_DIM, POOL_GROUP_DIM), lambda i: (0, 0, 0)),
                   pl.BlockSpec((8, POOL_WIDTH), lambda i: (0, 0))),
        out_shape=(jax.ShapeDtypeStruct((length, POOL_WIDTH), BF16), jax.ShapeDtypeStruct((n_g, POOL_GROUP_DIM, POOL_GROUP_DIM), F32),
                   jax.ShapeDtypeStruct((8, POOL_WIDTH), F32)),
        compiler_params=_cp(1),
    )(u, u, u, dmix, dmix, dmix, pool_w, pool_scale)


def _mixer_out(mix, w_out, x, g_a, nmw, sh_m, sc_m):
    t, d = x.shape
    tm = _pick(t, 256, 128)

    def epi(acc, ex, outs):
        x_ref, ga, nw, sh, sc = ex
        x1_ref, mo_ref, hm_ref = outs

        def rows(rs):
            mo = acc[rs, :]
            x1 = x_ref[rs, :] + ga[...] * mo
            x1_ref[rs, :] = x1
            mo_ref[rs, :] = mo.astype(BF16)
            r = lax.rsqrt(jnp.mean(x1 * x1, axis=-1, keepdims=True) + EPS)
            hm_ref[rs, :] = (((x1 * r) * nw[...]) * (1.0 + sc[...]) + sh[...]).astype(BF16)

        _row_loop(tm, rows)

    tile = pl.BlockSpec((tm, d), lambda j, i, k: (i, 0))
    return _mm("mixer_out", mix, w_out, nt=False, tm=tm, tn=d, tk=mix.shape[1], epi=epi,
               extras=(x, g_a, nmw, sh_m, sc_m), extra_specs=[tile] + [_row_spec(d)] * 4,
               out_shape=(jax.ShapeDtypeStruct((t, d), F32), jax.ShapeDtypeStruct((t, d), BF16), jax.ShapeDtypeStruct((t, d), BF16)),
               out_specs=(tile, tile, tile))


def _mlp_up(hm, w_up):
    t, d = hm.shape
    tm = _pick(t, 512, 256, 128)
    tn = 2048

    def epi(acc, ex, outs):
        outs[0][...] = jnp.square(jnp.maximum(acc[...], 0.0)).astype(BF16)

    return _mm("mlp_up", hm, w_up, nt=False, tm=tm, tn=tn, tk=d, epi=epi,
               out_shape=(jax.ShapeDtypeStruct((t, w_up.shape[1]), BF16),),
               out_specs=(pl.BlockSpec((tm, tn), lambda j, i, k: (i, j)),))[0]


def _mm_f32(name, a, b, *, nt, after):
    m, kdim = a.shape
    n = b.shape[0] if nt else b.shape[1]
    tm, tn = _pick(m, 1024, 512, 256, 128), _pick(n, 1024)

    def epi(acc, ex, outs):
        outs[0][...] = acc[...]

    return _mm(name, a, b, nt=nt, tm=tm, tn=tn, tk=_pick(kdim, 2048), epi=epi, extras=(after,), extra_specs=[ANY],
               out_shape=(jax.ShapeDtypeStruct((m, n), F32),), out_specs=(pl.BlockSpec((tm, tn), lambda j, i, k: (i, j)),))[0]


def _rows_call(name, rows_fn, tiles, vecs, out_shape, n_stats):
    t, d = tiles[0].shape
    tm = _pick(t, 256, 128)
    n_t, n_v = len(tiles), len(vecs)

    def body(*refs):
        st_ref = refs[-1]

        @pl.when(pl.program_id(0) == 0)
        def _():
            st_ref[...] = jnp.zeros_like(st_ref)

        _row_loop(tm, lambda rs: rows_fn(rs, refs[:n_t], refs[n_t:n_t + n_v], refs[n_t + n_v:-1], st_ref))

    tile = pl.BlockSpec((tm, d), lambda i: (i, 0))
    return pl.pallas_call(
        body, name=name, grid=(t // tm,), in_specs=[tile] * n_t + [_row_spec(d)] * n_v,
        out_specs=tuple([tile] * len(out_shape)) + (_stat_spec(n_stats, d),),
        out_shape=tuple(out_shape) + (jax.ShapeDtypeStruct((n_stats, 8, d), F32),), compiler_params=_cp(1),
    )(*tiles, *vecs)


def _loss_rows(dn, x1, target, g_m, fw):
    t, d = x1.shape

    def rows_fn(rs, tiles, vecs, outs, st_ref):
        dn_ref, x1_ref, t_ref = tiles
        gm, fw_ref = vecs
        dx2_ref, ddn_ref = outs
        dnv = dn_ref[rs, :]
        x2 = x1_ref[rs, :] + gm[...] * dnv
        r = lax.rsqrt(jnp.mean(x2 * x2, axis=-1, keepdims=True) + EPS)
        xh = x2 * r
        diff = xh * fw_ref[...] - t_ref[rs, :]
        dy = diff * (1.0 / d)
        dxh = dy * fw_ref[...]
        dx2 = r * (dxh - xh * jnp.mean(dxh * xh, axis=-1, keepdims=True))
        dx2_ref[rs, :] = dx2
        ddn_ref[rs, :] = (dx2 * gm[...]).astype(BF16)
        st_ref[0] += _fold8(diff * diff)
        st_ref[1] += _fold8(dy * xh)
        st_ref[2] += _fold8(dx2 * dnv)

    return _rows_call("loss_rows", rows_fn, (dn, x1, target), (g_m, fw),
                      (jax.ShapeDtypeStruct((t, d), F32), jax.ShapeDtypeStruct((t, d), BF16)), 3)


def _mlp_dx_rows(dhm, x1, dx2, mo, nmw, sc_m, g_a):
    t, d = x1.shape

    def rows_fn(rs, tiles, vecs, outs, st_ref):
        dh_ref, x1_ref, dx2_ref, mo_ref = tiles
        nw, sc, ga = vecs
        dx1_ref, dmi_ref = outs
        dx1 = _norm_bwd_rows(dh_ref[rs, :], x1_ref[rs, :], nw[...], sc[...], st_ref) + dx2_ref[rs, :]
        dx1_ref[rs, :] = dx1
        dmi_ref[rs, :] = (dx1 * ga[...]).astype(BF16)
        st_ref[3] += _fold8(dx1 * mo_ref[rs, :].astype(F32))

    return _rows_call("mlp_dx_rows", rows_fn, (dhm, x1, dx2, mo), (nmw, sc_m, g_a),
                      (jax.ShapeDtypeStruct((t, d), F32), jax.ShapeDtypeStruct((t, d), BF16)), 4)


def _mlp_dact(ddn, w_down, act):
    t, d = ddn.shape
    tm = _pick(t, 512, 256, 128)
    tn = 2048

    def epi(acc, ex, outs):
        outs[0][...] = (acc[...] * (2.0 * jnp.sqrt(ex[0][...].astype(F32)))).astype(BF16)

    tile = pl.BlockSpec((tm, tn), lambda j, i, k: (i, j))
    return _mm("mlp_dact", ddn, w_down, nt=True, tm=tm, tn=tn, tk=d, epi=epi, extras=(act,), extra_specs=[tile],
               out_shape=(jax.ShapeDtypeStruct(act.shape, BF16),), out_specs=(tile,))[0]


def _norm_bwd_rows(dh, xv, nw, sc, st_ref):
    r = lax.rsqrt(jnp.mean(xv * xv, axis=-1, keepdims=True) + EPS)
    xh = xv * r
    dy = dh * (1.0 + sc)
    st_ref[0] += _fold8(dh)
    st_ref[1] += _fold8(dh * (xh * nw))
    st_ref[2] += _fold8(dy * xh)
    dxh = dy * nw
    return r * (dxh - xh * jnp.mean(dxh * xh, axis=-1, keepdims=True))


def _mixer_dmix(dmi, w_out):
    t, d = dmi.shape
    tm = _pick(t, 512, 256, 128)

    def epi(acc, ex, outs):
        outs[0][...] = acc[...].astype(BF16)

    n = w_out.shape[0]
    return _mm("mixer_dmix", dmi, w_out, nt=True, tm=tm, tn=n, tk=d, epi=epi,
               out_shape=(jax.ShapeDtypeStruct((t, n), BF16),), out_specs=(pl.BlockSpec((tm, n), lambda j, i, k: (i, 0)),))[0]


def _mixer_dx(name, dp, w_in, x, dx1, naw, sc_a):
    t, d = x.shape
    tm = _pick(t, 256, 128)

    def epi(acc, ex, outs):
        x_ref, dx1_ref, nw, sc = ex
        gx_ref, st_ref = outs

        @pl.when(pl.program_id(1) == 0)
        def _():
            st_ref[...] = jnp.zeros_like(st_ref)

        def rows(rs):
            gx_ref[rs, :] = _norm_bwd_rows(acc[rs, :], x_ref[rs, :], nw[...], sc[...], st_ref) + dx1_ref[rs, :]

        _row_loop(tm, rows)

    tile = pl.BlockSpec((tm, d), lambda j, i, k: (i, 0))
    return _mm(name, dp, w_in, nt=True, tm=tm, tn=d, tk=dp.shape[1], epi=epi,
               extras=(x, dx1, naw, sc_a), extra_specs=[tile, tile, _row_spec(d), _row_spec(d)],
               out_shape=(jax.ShapeDtypeStruct((t, d), F32), jax.ShapeDtypeStruct((3, 8, d), F32)),
               out_specs=(tile, _stat_spec(3, d)))


def _silu(v):
    return v / (1.0 + jnp.exp(-v))


def _ada_fwd(cond, w_ada, b_ada):
    d, n = w_ada.shape
    tn = 512

    def body(c_ref, w_ref, b_ref, o_ref):
        o_ref[...] = _dot(_silu(c_ref[...]).astype(BF16), w_ref[...].astype(BF16)) + b_ref[...]

    return pl.pallas_call(
        body, name="ada_fwd", grid=(n // tn,),
        in_specs=[pl.BlockSpec(cond.shape, lambda j: (0, 0)), pl.BlockSpec((d, tn), lambda j: (0, j)), pl.BlockSpec((1, tn), lambda j: (0, j))],
        out_specs=pl.BlockSpec((cond.shape[0], tn), lambda j: (0, j)), out_shape=jax.ShapeDtypeStruct((cond.shape[0], n), F32),
        compiler_params=_cp(1),
    )(cond, w_ada, b_ada)


def _adamw_math(w, g, m, v):
    m = ADAM_B1 * m + (1.0 - ADAM_B1) * g
    v = ADAM_B2 * v + (1.0 - ADAM_B2) * jnp.square(g)
    m_hat = m / (1.0 - ADAM_B1 ** ADAM_STEP)
    v_hat = v / (1.0 - ADAM_B2 ** ADAM_STEP)
    return -ADAM_LR * (m_hat / (jnp.sqrt(v_hat) + ADAM_EPS) + ADAM_WD * w), m, v


def _ada_bwd(cond, dm, w_ada, m_ada, v_ada):
    d, n = w_ada.shape
    tn = 256
    rows = cond.shape[0]

    def body(c_ref, dm_ref, w_ref, m_ref, v_ref, g_ref, dl_ref, nm_ref, nv_ref, pc_ref):
        @pl.when(pl.program_id(0) == 0)
        def _():
            pc_ref[...] = jnp.zeros_like(pc_ref)

        dmb = dm_ref[...].astype(BF16)
        w = w_ref[...]
        g = _dot_tn(_silu(c_ref[...]).astype(BF16), dmb)
        g_ref[...] = g
        dl_ref[...], nm_ref[...], nv_ref[...] = _adamw_math(w, g, m_ref[...], v_ref[...])
        pc_ref[...] += _dot_nt(dm_ref[8:16, :].astype(BF16), w.astype(BF16))

    tile = pl.BlockSpec((d, tn), lambda j: (0, j))
    like = jax.ShapeDtypeStruct((d, n), F32)
    return pl.pallas_call(
        body, name="ada_bwd", grid=(n // tn,),
        in_specs=[pl.BlockSpec((rows, d), lambda j: (0, 0)), pl.BlockSpec((rows, tn), lambda j: (0, j)), tile, tile, tile],
        out_specs=(tile, tile, tile, tile, pl.BlockSpec((8, d), lambda j: (0, 0))),
        out_shape=(like, like, like, like, jax.ShapeDtypeStruct((8, d), F32)), compiler_params=_cp(1),
    )(cond, dm, w_ada, m_ada, v_ada)


def _adamw(name, w, g, m, v):
    return _ew(name, _adamw_math, [w, g, m, v], [F32, F32, F32])


def _colsum(st):
    return jnp.sum(st, axis=1)


def kernel(x, c, ctx, c_ctx, norm_attn_w, norm_mlp_w, w_ada, b_ada, w_in, attn_sink, pool_w, pool_scale, w_out, w_mlp_up, w_mlp_down, final_norm_w, loss_target, m_c_ctx, m_norm_attn_w, m_norm_mlp_w, m_w_ada, m_b_ada, m_w_in, m_attn_sink, m_pool_w, m_pool_scale, m_w_out, m_w_mlp_up, m_w_mlp_down, m_final_norm_w, v_c_ctx, v_norm_attn_w, v_norm_mlp_w, v_w_ada, v_b_ada, v_w_in, v_attn_sink, v_pool_w, v_pool_scale, v_w_out, v_w_mlp_up, v_w_mlp_down, v_final_norm_w):
    length, d = x.shape[1], x.shape[2]
    n_ctx = ctx.shape[1]
    pos = (lax.axis_index("x"), lax.axis_index("y"), lax.axis_index("c"))
    me, chip = _dev_index(pos), _chip_index(pos)
    xs, tgt, cx = x.reshape(length, d), loss_target.reshape(length, d), ctx.reshape(n_ctx, d)
    n_ada = w_ada.shape[2]

    c_all = _allgather8("gather_c", jnp.pad(c, ((0, 7), (0, 0))))
    cond = jnp.concatenate([c_all[:, 0, :], jnp.pad(c_ctx[None, :], ((0, 7), (0, 0)))], axis=0)
    b_shard = lax.dynamic_slice_in_dim(b_ada, chip * n_ada, n_ada, axis=1)
    mod_all = _allgather8("gather_mod", _ada_fwd(cond, w_ada[0], b_shard))
    mod = jnp.concatenate([mod_all[0], mod_all[2], mod_all[4], mod_all[6]], axis=1)
    mine = lax.dynamic_slice_in_dim(mod, me, 1, axis=0)
    sh_a, sc_a, g_a, sh_m, sc_m, g_m = [mine[:, d * i:d * (i + 1)] for i in range(6)]
    csh_a, csc_a = mod[8:9, :d], mod[8:9, d:2 * d]

    mixer_bigs = [_Big("col", w_in.shape[1:]), _Big("pool", pool_w.shape[1:]), _Big("row", w_out.shape[1:])]
    mlp_bigs = [_Big("col", w_mlp_up.shape[1:]), _Big("row", w_mlp_down.shape[1:])]
    placed = [_cast_place(f"place_{i}", b, s, c) for i, (b, s) in enumerate(zip(mixer_bigs, [w_in[0], pool_w[0], w_out[0]]))]
    win_b, pw_b, wout_b = _gather_weights("mixer", mixer_bigs, placed)
    wout_b = wout_b.reshape(-1, d)
    placed = [_cast_place(f"place_mlp_{i}", b, s, pw_b) for i, (b, s) in enumerate(zip(mlp_bigs, [w_mlp_up[0], w_mlp_down[0]]))]
    ici = _gather_ici_remote(mlp_bigs, 0)
    in_flight, send_sems, recv_sems, token = _exchange_start("gather_mlp_ici_start", placed, ici)

    cos, sin = _rope_tables(length, True)
    one, zero = _rope_tables(n_ctx, False)
    h, q, k, v, u = _mixer_in("mixer_in", xs, norm_attn_w, sh_a, sc_a, win_b, cos, sin, token)
    hc, _, kc, vc, _ = _mixer_in("mixer_in_ctx", cx, norm_attn_w, csh_a, csc_a, win_b, one, zero, token)
    mix = _pool_fwd(u, pw_b, pool_scale, _attn_fwd(q, k, v, kc, vc, attn_sink))
    landed = _exchange_wait("gather_mlp_ici_wait", in_flight, send_sems, recv_sems, ici, mix)
    wup_b, wdn_b = _exchange("gather_mlp_d2d", landed, [jax.ShapeDtypeStruct(b.full_shape, BF16) for b in mlp_bigs],
                             _gather_d2d_remote(mlp_bigs, 2), aliases={0: 0, 1: 1})
    wdn_b = wdn_b.reshape(-1, d)
    x1, mo, hm = _mixer_out(mix, wout_b, xs, g_a, norm_mlp_w, sh_m, sc_m)
    act = _mlp_up(hm, wup_b)
    dn = _mm_f32("mlp_down", act, wdn_b, nt=False, after=c)
    dx2, ddn, st_loss = _loss_rows(dn, x1, tgt, g_m, final_norm_w[None, :])
    st_loss = _colsum(st_loss)
    loss = lax.psum(0.5 / d * jnp.sum(st_loss[0]), ("x", "y", "c"))

    tt = _pick(length, 2048, 1024, 512, 256, 128)
    g_wdn = _mm_tn("grad_w_down", act, ddn, BF16, tmo=1024, tn=d, tt=tt)
    dup = _mlp_dact(ddn, wdn_b, act)
    g_wup = _mm_tn("grad_w_up", hm, dup, BF16, tmo=d, tn=1024, tt=tt)
    mlp_chip = _reduce_to_chip("mlp", mlp_bigs, [g_wup, g_wdn.reshape(mlp_bigs[1].full_shape)])
    ici_r = _reduce_ici_remote(mlp_bigs)
    in_flight, send_sems, recv_sems, token = _exchange_start(
        "reduce_mlp_ici_start", mlp_chip + [lax.empty(t.shape, t.dtype) for t in _thirds(mlp_bigs)], ici_r)
    dhm = _mm_f32("mlp_dhm", dup, wup_b, nt=True, after=token)
    dx1, dmi, st_mlp = _mlp_dx_rows(dhm, x1, dx2, mo, norm_mlp_w, sc_m, g_a)
    st_mlp = _colsum(st_mlp)
    g_wout = _mm_tn("grad_w_out", mix, dmi, BF16, tmo=1024, tn=d, tt=tt)
    dmix = _mixer_dmix(dmi, wout_b)
    dq, dkp, dvp, dkc, dvc, dsink = _attn_bwd(q, k, v, kc, vc, attn_sink, dmix)
    landed = _exchange_wait("reduce_mlp_ici_wait", in_flight, send_sems, recv_sems, ici_r, dq)
    g_mlp = _reduce_finish("mlp", mlp_bigs, landed[:2], landed[2:])
    du, g_pw, st_pool = _pool_bwd(u, dmix, pw_b, pool_scale)
    dp = _assemble_dp(dq, dkp, dvp, du, cos, sin)
    dpc = jnp.concatenate([jnp.zeros((n_ctx, ATTN_WIDTH), BF16), dkc.astype(BF16), dvc.astype(BF16),
                           jnp.zeros((n_ctx, POOL_WIDTH), BF16)], axis=1)
    grad_x, st_mix = _mixer_dx("mixer_dx", dp, win_b, xs, dx1, norm_attn_w, sc_a)
    _, st_ctx = _mixer_dx("mixer_dx_ctx", dpc, win_b, cx, jnp.zeros((n_ctx, d), F32), norm_attn_w, csc_a)
    st_mix, st_ctx = _colsum(st_mix), _colsum(st_ctx)
    g_win = _mm_tn("grad_w_in", h, dp, BF16, tmo=d, tn=dp.shape[1] // 2, tt=_pick(length, 1024, 512, 256, 128), more=(hc, dpc))

    zrow = jnp.zeros((d,), F32)
    pad = lambda a: jnp.pad(a, (0, d - a.shape[0]))
    mine_rows = [st_mix[0], st_mix[1], st_mlp[3], st_mlp[0], st_mlp[1], st_loss[2],
                 st_ctx[0], st_ctx[1],
                 st_mix[2] + st_ctx[2], st_mlp[2], st_loss[1],
                 pad(jnp.sum(st_pool, axis=0)), pad(dsink[0, :N_Q_HEADS])] + [zrow] * 3
    small_all = _allgather8("gather_small", jnp.concatenate(mine_rows).reshape(len(mine_rows), d))
    small = small_all[0]
    for i in range(1, 8):
        small = small + small_all[i]
    dm_rows = small_all[:, 0:6, :].reshape(8, 6 * d)
    dm_ctx = jnp.concatenate([small[6], small[7], jnp.zeros((4 * d,), F32)])[None, :]
    dm = jnp.concatenate([dm_rows, jnp.pad(dm_ctx, ((0, 7), (0, 0)))], axis=0)
    g_bada = jnp.sum(dm[:9], axis=0, keepdims=True)
    dm_shard = lax.dynamic_slice_in_dim(dm, chip * n_ada, n_ada, axis=1)
    g_wada, dl_wada, nm_wada, nv_wada, part_cctx = _ada_bwd(cond, dm_shard, w_ada[0], m_w_ada[0], v_w_ada[0])
    cctx_all = _allgather8("gather_cctx", part_cctx)
    dsilu_in = cctx_all[0, 0] + cctx_all[2, 0] + cctx_all[4, 0] + cctx_all[6, 0]
    sig = 1.0 / (1.0 + jnp.exp(-c_ctx))
    g_cctx = dsilu_in * (sig * (1.0 + c_ctx * (1.0 - sig)))

    mixer_chip = _reduce_to_chip("mixer", mixer_bigs, [g_win, g_pw.astype(BF16), g_wout.reshape(mixer_bigs[2].full_shape)])
    from_chips = _exchange("reduce_mixer_ici", mixer_chip, _thirds(mixer_bigs), _reduce_ici_remote(mixer_bigs))
    g_shards = _reduce_finish("mixer", mixer_bigs, mixer_chip, from_chips) + g_mlp
    big_w = [w_in, pool_w, w_out, w_mlp_up, w_mlp_down]
    big_m = [m_w_in, m_pool_w, m_w_out, m_w_mlp_up, m_w_mlp_down]
    big_v = [v_w_in, v_pool_w, v_w_out, v_w_mlp_up, v_w_mlp_down]
    big_names = ["w_in", "pool_w", "w_out", "w_mlp_up", "w_mlp_down"]
    res = {}
    for nm, w_, g_, m_, v_ in zip(big_names, big_w, g_shards, big_m, big_v):
        g_ = g_.reshape(w_.shape)
        res[nm] = (g_,) + tuple(_adamw("adamw_" + nm, w_, g_, m_, v_))
    res["w_ada"] = (g_wada[None], dl_wada[None], nm_wada[None], nv_wada[None])

    def pack(cc, na, nm_, ba, sk, ps, fn):
        flat = [cc.reshape(-1), na.reshape(-1), nm_.reshape(-1), ba.reshape(-1), pad(sk.reshape(-1)), pad(ps.reshape(-1)),
                fn.reshape(-1), jnp.zeros((4 * d,), F32)]
        return jnp.concatenate(flat).reshape(16, d)

    w_s = pack(c_ctx, norm_attn_w, norm_mlp_w, b_ada, attn_sink, pool_scale, final_norm_w)
    m_s = pack(m_c_ctx, m_norm_attn_w, m_norm_mlp_w, m_b_ada, m_attn_sink, m_pool_scale, m_final_norm_w)
    v_s = pack(v_c_ctx, v_norm_attn_w, v_norm_mlp_w, v_b_ada, v_attn_sink, v_pool_scale, v_final_norm_w)
    g_s = pack(g_cctx, small[8], small[9], g_bada, small[12][:N_Q_HEADS], small[11][:POOL_WIDTH], small[10])
    small_out = [g_s] + _adamw("adamw_small", w_s, g_s, m_s, v_s)

    def unpack(p):
        return {"c_ctx": p[0], "norm_attn_w": p[1:2], "norm_mlp_w": p[2:3], "b_ada": p[3:9].reshape(1, 6 * d),
                "attn_sink": p[9:10, :N_Q_HEADS], "pool_scale": p[10:11, :POOL_WIDTH], "final_norm_w": p[11]}

    small_res = [unpack(p) for p in small_out]
    order = ["c_ctx", "norm_attn_w", "norm_mlp_w", "w_ada", "b_ada", "w_in", "attn_sink", "pool_w", "pool_scale",
             "w_out", "w_mlp_up", "w_mlp_down", "final_norm_w"]
    outs = [loss, grad_x.reshape(x.shape)]
    for kind in range(4):
        for nm in order:
            outs.append(res[nm][kind] if nm in res else small_res[kind][nm])
    return tuple(outs)
```

```python
import functools

import jax
import jax.numpy as jnp
from jax import lax
from jax.experimental import pallas as pl
from jax.experimental.pallas import tpu as pltpu

F32 = jnp.float32
BF16 = jnp.bfloat16
EPS = 1e-6
NEG_INF = -1e30
HEAD_DIM = 64
N_Q_HEADS = 16
N_KV_HEADS = 4
GROUP = N_Q_HEADS // N_KV_HEADS
ATTN_WIDTH = N_Q_HEADS * HEAD_DIM
KV_WIDTH = N_KV_HEADS * HEAD_DIM
POOL_WINDOWS = (2, 4, 8, 16)
POOL_GROUP_DIM = 256
POOL_WIDTH = len(POOL_WINDOWS) * POOL_GROUP_DIM
BLOCK = 128
GRID_W = 64
ROPE_BASE = 10000.0
SCALE = HEAD_DIM ** -0.5
HALO = 16
ROWS = 64
ADAM_LR, ADAM_B1, ADAM_B2, ADAM_EPS, ADAM_WD, ADAM_STEP = 0.001, 0.9, 0.999, 1e-08, 0.01, 10
MESH = pl.DeviceIdType.MESH
MIB = 1024 * 1024
ANY = pl.BlockSpec(memory_space=pl.ANY)


def _cp(n_axes, vmem_mib=48):
    return pltpu.CompilerParams(dimension_semantics=("arbitrary",) * n_axes, vmem_limit_bytes=vmem_mib * MIB)


def _row_loop(rows, fn):
    def body(r, carry):
        fn(pl.ds(pl.multiple_of(r * ROWS, ROWS), ROWS))
        return carry

    lax.fori_loop(0, rows // ROWS, body, 0)


def _fold8(v):
    s = v[0:8]
    for t in range(1, v.shape[0] // 8):
        s = s + v[8 * t:8 * t + 8]
    return s


def _dot(a, b):
    return jnp.dot(a, b, preferred_element_type=F32)


def _dot_nt(a, b):
    return lax.dot_general(a, b, (((1,), (1,)), ((), ())), preferred_element_type=F32)


def _dot_tn(a, b):
    return lax.dot_general(a, b, (((0,), (0,)), ((), ())), preferred_element_type=F32)


def _pick(n, *cands):
    for t in cands:
        if n % t == 0:
            return t
    return n


def _flip(pos, mask):
    return tuple((1 - v) if (mask >> (2 - i)) & 1 else v for i, v in enumerate(pos))


def _exchange(name, ins, out_shapes, remote, local=(), aliases=None):
    n_io = len(ins) + len(out_shapes)

    def body(*refs):
        io = refs[:n_io]
        send_sems, recv_sems, local_sems = refs[n_io:]
        me = (lax.axis_index("x"), lax.axis_index("y"), lax.axis_index("c"))

        def copy(i, sender):
            mask, src_fn, dst_fn = remote[i]
            return pltpu.make_async_remote_copy(
                src_ref=src_fn(io, sender), dst_ref=dst_fn(io, sender), send_sem=send_sems.at[i],
                recv_sem=recv_sems.at[i], device_id=_flip(sender, mask), device_id_type=MESH)

        own = [pltpu.make_async_copy(s(io, me), d(io, me), local_sems.at[i]) for i, (s, d) in enumerate(local)]
        for cp in own:
            cp.start()
        sends = [copy(i, me) for i in range(len(remote))]
        for cp in sends:
            cp.start()
        for i in range(len(remote)):
            copy(i, _flip(me, remote[i][0])).wait_recv()
        for cp in sends:
            cp.wait_send()
        for cp in own:
            cp.wait()

    return pl.pallas_call(
        body, name=name, out_shape=tuple(out_shapes),
        in_specs=[ANY] * len(ins), out_specs=tuple([ANY] * len(out_shapes)),
        scratch_shapes=[pltpu.SemaphoreType.DMA((len(remote),)), pltpu.SemaphoreType.DMA((len(remote),)),
                        pltpu.SemaphoreType.DMA((max(len(local), 1),))],
        input_output_aliases=aliases or {},
    )(*ins)


HBM = pl.BlockSpec(memory_space=pltpu.HBM)
SEM = pl.BlockSpec(memory_space=pltpu.SEMAPHORE)
EFFECT = pltpu.SideEffectType.DATAFLOW_SIDE_EFFECTING


def _split_copy(remote, i, io, send_sems, recv_sems, sender):
    mask, src_fn, dst_fn = remote[i]
    return pltpu.make_async_remote_copy(
        src_ref=src_fn(io, sender), dst_ref=dst_fn(io, sender), send_sem=send_sems.at[i],
        recv_sem=recv_sems.at[i], device_id=_flip(sender, mask), device_id_type=MESH)


def _exchange_start(name, bufs, remote):
    n, r = len(bufs), len(remote)

    def body(*refs):
        io, send_sems, recv_sems, token = refs[:n], refs[2 * n], refs[2 * n + 1], refs[2 * n + 2]
        me = (lax.axis_index("x"), lax.axis_index("y"), lax.axis_index("c"))
        for i in range(r):
            _split_copy(remote, i, io, send_sems, recv_sems, me).start()
        token[...] = jnp.zeros_like(token)

    res = pl.pallas_call(
        body, name=name,
        out_shape=tuple(pltpu.HBM(b.shape, b.dtype) for b in bufs)
        + (pltpu.SemaphoreType.DMA((r,)), pltpu.SemaphoreType.DMA((r,)), jax.ShapeDtypeStruct((8, 128), F32)),
        in_specs=[HBM] * n, out_specs=tuple([HBM] * n) + (SEM, SEM, pl.BlockSpec(memory_space=pltpu.VMEM)),
        input_output_aliases={i: i for i in range(n)}, compiler_params=pltpu.CompilerParams(has_side_effects=EFFECT),
    )(*[pltpu.with_memory_space_constraint(b, pltpu.HBM) for b in bufs])
    return list(res[:n]), res[n], res[n + 1], res[n + 2]


def _exchange_wait(name, bufs, send_sems, recv_sems, remote, after):
    n, r = len(bufs), len(remote)

    def body(*refs):
        io, ss, rs = refs[:n], refs[n], refs[n + 1]
        me = (lax.axis_index("x"), lax.axis_index("y"), lax.axis_index("c"))
        for i in range(r):
            _split_copy(remote, i, io, ss, rs, _flip(me, remote[i][0])).wait_recv()
        for i in range(r):
            _split_copy(remote, i, io, ss, rs, me).wait_send()

    return list(pl.pallas_call(
        body, name=name, out_shape=tuple(pltpu.HBM(b.shape, b.dtype) for b in bufs),
        in_specs=[HBM] * n + [SEM, SEM, ANY], out_specs=tuple([HBM] * n),
        input_output_aliases={i: i for i in range(n)}, compiler_params=pltpu.CompilerParams(has_side_effects=EFFECT),
    )(*bufs, send_sems, recv_sems, after))


def _my_c():
    return lax.axis_index("c")


def _my_chip():
    return 2 * lax.axis_index("x") + lax.axis_index("y")


def _dev_index(pos):
    return 4 * pos[0] + 2 * pos[1] + pos[2]


def _chip_index(pos):
    return 2 * pos[0] + pos[1]


def _allgather8(name, v):
    out = jax.ShapeDtypeStruct((8,) + v.shape, v.dtype)
    remote = [(mask, lambda io, pos: io[0], lambda io, pos: io[1].at[_dev_index(pos)]) for mask in range(1, 8)]
    local = [(lambda io, pos: io[0], lambda io, pos: io[1].at[_dev_index(pos)])]
    return _exchange(name, [v], [out], remote, local)[0]


class _Big:
    def __init__(self, kind, shard_shape):
        self.kind = kind
        self.shard_shape = tuple(shard_shape)
        if kind == "col":
            r, cs = shard_shape
            self.full_shape = (r, 4 * cs)
            self.piece_shape = (r // 2, cs)
            self.half_shape = (r // 2, 4 * cs)
        elif kind == "row":
            rs, c = shard_shape
            self.full_shape = (4, 2, rs // 2, c)
            self.piece_shape = (1, 1, rs // 2, c)
            self.half_shape = (4, 1, rs // 2, c)
        else:
            self.full_shape = (4, 256, 256)
            self.piece_shape = (2, 64, 256)
            self.half_shape = (2, 256, 256)

    def shard_as_pieces(self, a):
        return a.reshape((1, 2) + self.piece_shape[2:]) if self.kind == "row" else a

    def piece(self, ref, k, h):
        if self.kind == "col":
            r, cs = self.piece_shape
            return ref.at[pl.ds(h * r, r), pl.ds(k * cs, cs)]
        if self.kind == "row":
            return ref.at[pl.ds(k, 1), pl.ds(h, 1)]
        return ref.at[pl.ds(2 * h, 2), pl.ds(64 * k, 64)]

    def half_of_shard(self, ref, h):
        if self.kind == "col":
            return ref.at[pl.ds(h * self.piece_shape[0], self.piece_shape[0])]
        if self.kind == "row":
            return ref.at[:, pl.ds(h, 1)]
        return ref.at[pl.ds(2 * h, 2)]

    def half_of_full(self, ref, h):
        if self.kind == "col":
            return ref.at[pl.ds(h * self.half_shape[0], self.half_shape[0])]
        if self.kind == "row":
            return ref.at[:, pl.ds(h, 1)]
        return ref.at[pl.ds(2 * h, 2)]

    def piece_of_half(self, ref, k):
        if self.kind == "col":
            return ref.at[:, pl.ds(k * self.piece_shape[1], self.piece_shape[1])]
        if self.kind == "row":
            return ref.at[pl.ds(k, 1)]
        return ref.at[:, pl.ds(64 * k, 64)]


CHIP_MASKS = (4, 2, 6)


def _cast_place(name, big, shard, after):
    if big.kind == "col":
        r, cs = big.shard_shape
        tr = _pick(r, 512, 256, 128)
        src, grid, blk = shard, (r // tr,), (tr, cs)
        imap, omap = (lambda i: (i, 0)), (lambda i: (i, _my_chip()))
    elif big.kind == "row":
        rs, c = big.shard_shape
        tr = _pick(rs // 2, 256, 128)
        src, grid, blk = big.shard_as_pieces(shard), (2, rs // 2 // tr), (1, 1, tr, c)
        imap, omap = (lambda h, i: (0, h, i, 0)), (lambda h, i: (_my_chip(), h, i, 0))
    else:
        src, grid, blk = shard, (1,), big.shard_shape
        imap, omap = (lambda i: (0, 0, 0)), (lambda i: (0, _my_chip(), 0))

    def body(s_ref, after_ref, o_ref):
        o_ref[...] = s_ref[...].astype(BF16)

    return pl.pallas_call(
        body, name=name, grid=grid, in_specs=[pl.BlockSpec(blk, imap), ANY], out_specs=pl.BlockSpec(blk, omap),
        out_shape=jax.ShapeDtypeStruct(big.full_shape, BF16), compiler_params=_cp(len(grid)),
    )(src, after)


def _gather_ici_remote(bigs, off):
    remote = []
    for a, b in enumerate(bigs):
        for mask in CHIP_MASKS:
            def mine(io, p, a=a, b=b):
                return b.piece(io[off + a], _chip_index(p), p[2])
            remote.append((mask, mine, mine))
    return remote


def _gather_d2d_remote(bigs, off):
    remote = []
    for a, b in enumerate(bigs):
        for mask in CHIP_MASKS:
            def region(io, p, a=a, b=b, mask=mask):
                return b.piece(io[off + a], _chip_index(_flip(p, mask)), p[2])
            remote.append((1, region, region))
    return remote


def _gather_weights(tag, bigs, placed):
    n = len(bigs)
    fulls = [jax.ShapeDtypeStruct(b.full_shape, BF16) for b in bigs]
    alias = {a: a for a in range(n)}
    got = _exchange(f"gather_{tag}_ici", placed, fulls, _gather_ici_remote(bigs, n), aliases=alias)
    return _exchange(f"gather_{tag}_d2d", list(got), fulls, _gather_d2d_remote(bigs, n), aliases=alias)


def _ew(name, fn, ins, out_dtypes, rows_per_step=256):
    shape = ins[0].shape
    last = shape[-1]
    rows = 1
    for s in shape[:-1]:
        rows *= s
    ins2 = [a.reshape(rows, last) for a in ins]
    tr = _pick(rows, rows_per_step, 128, 64, 32, 16, 8)
    spec = pl.BlockSpec((tr, last), lambda i: (i, 0))

    def body(*refs):
        outs = fn(*[r[...] for r in refs[:len(ins)]])
        for o_ref, o in zip(refs[len(ins):], outs):
            o_ref[...] = o.astype(o_ref.dtype)

    outs = pl.pallas_call(
        body, name=name, grid=(rows // tr,), in_specs=[spec] * len(ins), out_specs=tuple([spec] * len(out_dtypes)),
        out_shape=tuple(jax.ShapeDtypeStruct((rows, last), d) for d in out_dtypes), compiler_params=_cp(1),
    )(*ins2)
    return [o.reshape(shape) for o in outs]


def _chip_sum(name, big, grad, from_sibling):
    if big.kind == "col":
        rh, w = big.half_shape
        tr = _pick(rh, 256, 128)
        nb = rh // tr
        grid, blk = (nb,), (tr, w)
        gmap, hmap = (lambda i: (_my_c() * nb + i, 0)), (lambda i: (i, 0))
    elif big.kind == "row":
        rh, w = big.half_shape[2:]
        tr = _pick(rh, 256, 128)
        grid, blk = (4, rh // tr), (1, 1, tr, w)
        gmap, hmap = (lambda k, i: (k, _my_c(), i, 0)), (lambda k, i: (k, 0, i, 0))
    else:
        grid, blk = (1,), big.half_shape
        gmap, hmap = (lambda i: (_my_c(), 0, 0)), (lambda i: (0, 0, 0))

    def body(g_ref, s_ref, o_ref):
        o_ref[...] = (g_ref[...].astype(F32) + s_ref[...].astype(F32)).astype(BF16)

    return pl.pallas_call(
        body, name=name, grid=grid, in_specs=[pl.BlockSpec(blk, gmap), pl.BlockSpec(blk, hmap)],
        out_specs=pl.BlockSpec(blk, hmap), out_shape=jax.ShapeDtypeStruct(big.half_shape, BF16), compiler_params=_cp(len(grid)),
    )(grad, from_sibling)


def _piece_sum(name, big, chip_sum, thirds):
    if big.kind == "col":
        rp, cs = big.piece_shape
        tr = _pick(rp, 256, 128)
        nb = rp // tr
        grid, blk, tblk = (nb,), (tr, cs), (1, tr, cs)
        smap, omap = (lambda i: (i, _my_chip())), (lambda i: (_my_c() * nb + i, 0))
        tmap = lambda j: (lambda i: (j, i, 0))
        out_shape = big.shard_shape
    elif big.kind == "row":
        rp, w = big.piece_shape[2:]
        tr = _pick(rp, 256, 128)
        grid, blk, tblk = (rp // tr,), (1, 1, tr, w), (1, 1, 1, tr, w)
        smap, omap = (lambda i: (_my_chip(), 0, i, 0)), (lambda i: (0, _my_c(), i, 0))
        tmap = lambda j: (lambda i: (j, 0, 0, i, 0))
        out_shape = (1, 2, rp, w)
    else:
        grid, blk, tblk = (1,), big.piece_shape, (1,) + big.piece_shape
        smap, omap = (lambda i: (0, _my_chip(), 0)), (lambda i: (_my_c(), 0, 0))
        tmap = lambda j: (lambda i: (j, 0, 0, 0))
        out_shape = big.shard_shape

    def body(s_ref, t0, t1, t2, o_ref):
        o_ref[...] = s_ref[...].astype(F32) + t0[0].astype(F32) + t1[0].astype(F32) + t2[0].astype(F32)

    return pl.pallas_call(
        body, name=name, grid=grid,
        in_specs=[pl.BlockSpec(blk, smap)] + [pl.BlockSpec(tblk, tmap(j)) for j in range(3)],
        out_specs=pl.BlockSpec(blk, omap), out_shape=jax.ShapeDtypeStruct(out_shape, F32), compiler_params=_cp(len(grid)),
    )(chip_sum, thirds, thirds, thirds)


def _split(name, bufs, remote):
    return _exchange_start(name + "_start", bufs, remote) + (remote, name)


def _join(handle, after):
    bufs, send_sems, recv_sems, _, remote, name = handle
    return _exchange_wait(name + "_wait", bufs, send_sems, recv_sems, remote, after)


def _reduce_d2d_remote(bigs):
    n = len(bigs)
    return [(1, lambda io, p, a=a, b=b: b.half_of_full(io[a], 1 - p[2]), lambda io, p, a=a: io[n + a])
            for a, b in enumerate(bigs)]


def _halves(bigs):
    return [jax.ShapeDtypeStruct(b.half_shape, BF16) for b in bigs]


def _chip_sums(tag, bigs, grads, from_sibling):
    return [_chip_sum(f"reduce_{tag}_chip_sum_{a}", b, g, r) for a, (b, g, r) in enumerate(zip(bigs, grads, from_sibling))]


def _reduce_to_chip(tag, bigs, grads):
    from_sibling = _exchange(f"reduce_{tag}_d2d", grads, _halves(bigs), _reduce_d2d_remote(bigs))
    return _chip_sums(tag, bigs, grads, from_sibling)


def _reduce_ici_remote(bigs):
    n = len(bigs)
    remote = []
    for a, b in enumerate(bigs):
        for j, mask in enumerate(CHIP_MASKS):
            remote.append((mask,
                           lambda io, p, a=a, b=b, mask=mask: b.piece_of_half(io[a], _chip_index(_flip(p, mask))),
                           lambda io, p, a=a, j=j: io[n + a].at[j]))
    return remote


def _thirds(bigs):
    return [jax.ShapeDtypeStruct((3,) + b.piece_shape, BF16) for b in bigs]


def _piece_sums(tag, bigs, chip_sum, from_chips):
    return [_piece_sum(f"reduce_{tag}_sum_{a}", b, s, r) for a, (b, s, r) in enumerate(zip(bigs, chip_sum, from_chips))]


def _share_remote(bigs, off):
    remote = []
    for a, b in enumerate(bigs):
        def mine(io, p, a=a, b=b):
            return b.half_of_shard(io[off + a], p[2])
        remote.append((1, mine, mine))
    return remote


def _reduce_finish(tag, bigs, chip_sum, from_chips):
    n = len(bigs)
    placed = _piece_sums(tag, bigs, chip_sum, from_chips)
    out = _exchange(f"reduce_{tag}_share_d2d", placed, [jax.ShapeDtypeStruct(p.shape, F32) for p in placed],
                    _share_remote(bigs, n), aliases={a: a for a in range(n)})
    return [o.reshape(b.shard_shape) for o, b in zip(out, bigs)]


def _mm(name, a, b, *, nt, tm, tn, tk, epi, extras=(), extra_specs=(), out_shape, out_specs, after=None, vmem_mib=48):
    m, kdim = a.shape
    n = b.shape[0] if nt else b.shape[1]
    gm, gn, gk = m // tm, n // tn, kdim // tk
    a_spec = pl.BlockSpec((tm, tk), lambda j, i, k: (i, k))
    b_spec = pl.BlockSpec((tn, tk), lambda j, i, k: (j, k)) if nt else pl.BlockSpec((tk, tn), lambda j, i, k: (k, j))
    n_ex = len(extras)
    if after is not None:
        extras, extra_specs = tuple(extras) + (after,), list(extra_specs) + [ANY]

    def body(a_ref, b_ref, *rest):
        ex, outs, acc = rest[:n_ex], rest[len(extras):-1], rest[-1]
        dot = _dot_nt if nt else _dot
        if gk == 1:
            acc[...] = dot(a_ref[...], b_ref[...])
            epi(acc, ex, outs)
        else:
            k = pl.program_id(2)

            @pl.when(k == 0)
            def _():
                acc[...] = jnp.zeros_like(acc)

            acc[...] += dot(a_ref[...], b_ref[...])

            @pl.when(k == gk - 1)
            def _():
                epi(acc, ex, outs)

    return pl.pallas_call(
        body, name=name, grid=(gn, gm, gk), in_specs=[a_spec, b_spec, *extra_specs], out_specs=tuple(out_specs),
        out_shape=tuple(out_shape), scratch_shapes=[pltpu.VMEM((tm, tn), F32)], compiler_params=_cp(3, vmem_mib),
    )(a, b, *extras)


def _mm_tn(name, a, b, out_dtype, *, tmo, tn, tt, more=(), vmem_mib=56):
    t, m = a.shape
    n = b.shape[1]
    gt = t // tt

    def body(a_ref, b_ref, *rest):
        o_ref, acc = rest[-2:]
        k = pl.program_id(2)

        @pl.when(k == 0)
        def _():
            acc[...] = _dot_tn(rest[0][...], rest[1][...]) if more else jnp.zeros_like(acc)

        acc[...] += _dot_tn(a_ref[...], b_ref[...])

        @pl.when(k == gt - 1)
        def _():
            o_ref[...] = acc[...].astype(o_ref.dtype)

    more_specs = [pl.BlockSpec((more[0].shape[0], tmo), lambda i, j, k: (0, i)),
                  pl.BlockSpec((more[1].shape[0], tn), lambda i, j, k: (0, j))] if more else []
    return pl.pallas_call(
        body, name=name, grid=(m // tmo, n // tn, gt),
        in_specs=[pl.BlockSpec((tt, tmo), lambda i, j, k: (k, i)), pl.BlockSpec((tt, tn), lambda i, j, k: (k, j))] + more_specs,
        out_specs=pl.BlockSpec((tmo, tn), lambda i, j, k: (i, j)), out_shape=jax.ShapeDtypeStruct((m, n), out_dtype),
        scratch_shapes=[pltpu.VMEM((tmo, tn), F32)], compiler_params=_cp(3, vmem_mib),
    )(a, b, *more)


def _row_spec(d):
    return pl.BlockSpec((1, d), lambda *_: (0, 0))


def _stat_spec(k, d):
    return pl.BlockSpec((k, 8, d), lambda *_: (0, 0, 0))


def _rope(z, cs, sn):
    first = (lax.broadcasted_iota(jnp.int32, (z.shape[0], 128), 1) % 32) < 16
    outs = []
    for j in range(z.shape[1] // 128):
        zc = z[:, 128 * j:128 * (j + 1)]
        partner = jnp.where(first, pltpu.roll(zc, 112, 1), pltpu.roll(zc, 16, 1))
        outs.append(zc * cs + partner * sn)
    return outs[0] if len(outs) == 1 else jnp.concatenate(outs, axis=1)


def _rope_tables(length, rotate):
    if not rotate:
        return jnp.ones((length, 128), F32), jnp.zeros((length, 128), F32)
    half = HEAD_DIM // 2
    inv_freq = ROPE_BASE ** (-jnp.arange(0, half, 2, dtype=F32) / half)
    t = jnp.arange(length)
    row = (t // GRID_W).astype(F32)
    col = (t % GRID_W).astype(F32)
    e = jnp.arange(128) % HEAD_DIM
    pos = jnp.where(e[None, :] < half, row[:, None], col[:, None])
    ang = pos * inv_freq[(e % half) % (half // 2)][None, :]
    first = ((e % half) < half // 2)[None, :]
    return jnp.cos(ang), jnp.where(first, -jnp.sin(ang), jnp.sin(ang))


def _mixer_in(name, x, nw, sh, sc, w_in, cos, sin, after):
    t, d = x.shape
    tm = _pick(t, 256, 128)
    n_in = w_in.shape[1]

    def body(x_ref, nw_ref, sh_ref, sc_ref, w_ref, cos_ref, sin_ref, after_ref, h_ref, q_ref, k_ref, v_ref, u_ref):
        xf = x_ref[...]
        r = lax.rsqrt(jnp.mean(xf * xf, axis=-1, keepdims=True) + EPS)
        hb = (((xf * r) * nw_ref[...]) * (1.0 + sc_ref[...]) + sh_ref[...]).astype(BF16)
        h_ref[...] = hb
        p = _dot(hb, w_ref[...])
        cs, sn = cos_ref[...], sin_ref[...]
        q_ref[...] = _rope(p[:, :ATTN_WIDTH], cs, sn).astype(BF16)
        k_ref[...] = _rope(p[:, ATTN_WIDTH:ATTN_WIDTH + KV_WIDTH], cs, sn).astype(BF16)
        v_ref[...] = p[:, ATTN_WIDTH + KV_WIDTH:ATTN_WIDTH + 2 * KV_WIDTH].astype(BF16)
        u_ref[...] = p[:, ATTN_WIDTH + 2 * KV_WIDTH:]

    def tile(w):
        return pl.BlockSpec((tm, w), lambda i: (i, 0))

    return pl.pallas_call(
        body, name=name, grid=(t // tm,),
        in_specs=[tile(d), _row_spec(d), _row_spec(d), _row_spec(d), pl.BlockSpec((d, n_in), lambda i: (0, 0)),
                  tile(128), tile(128), ANY],
        out_specs=(tile(d), tile(ATTN_WIDTH), tile(KV_WIDTH), tile(KV_WIDTH), tile(POOL_WIDTH)),
        out_shape=(jax.ShapeDtypeStruct((t, d), BF16), jax.ShapeDtypeStruct((t, ATTN_WIDTH), BF16),
                   jax.ShapeDtypeStruct((t, KV_WIDTH), BF16), jax.ShapeDtypeStruct((t, KV_WIDTH), BF16),
                   jax.ShapeDtypeStruct((t, POOL_WIDTH), F32)),
        compiler_params=_cp(1),
    )(x, nw, sh, sc, w_in, cos, sin, after)


def _attn_specs(nb, n_ctx):
    def blk(w, f):
        return pl.BlockSpec((BLOCK, w), lambda n: (f(n), 0))

    prev = lambda n: jnp.maximum(n - 1, 0)
    cur = lambda n: n
    nxt = lambda n: jnp.minimum(n + 1, nb - 1)
    kv = [blk(KV_WIDTH, prev), blk(KV_WIDTH, cur), blk(KV_WIDTH, nxt)]
    ctx = pl.BlockSpec((n_ctx, KV_WIDTH), lambda n: (0, 0))
    return [pl.BlockSpec(memory_space=pltpu.SMEM), blk(ATTN_WIDTH, cur)] + kv + kv + [ctx, ctx]


def _attn_mask(n, length, n_keys):
    row = lax.broadcasted_iota(jnp.int32, (GROUP * BLOCK, n_keys), 0) % BLOCK
    col = lax.broadcasted_iota(jnp.int32, (GROUP * BLOCK, n_keys), 1)
    kpos = (n - 1) * BLOCK + col
    return ((jnp.abs(col - BLOCK - row) <= BLOCK) & (kpos >= 0) & (kpos < length)) | (col >= 3 * BLOCK)


def _group_rows(block, g):
    return jnp.concatenate([block[:, HEAD_DIM * h:HEAD_DIM * (h + 1)] for h in range(GROUP * g, GROUP * (g + 1))], axis=0)


def _group_sink(sink_ref, g):
    head = lax.broadcasted_iota(jnp.int32, (GROUP * BLOCK, 1), 0) // BLOCK
    out = jnp.full((GROUP * BLOCK, 1), sink_ref[0, GROUP * g], F32)
    for j in range(1, GROUP):
        out = jnp.where(head == j, sink_ref[0, GROUP * g + j], out)
    return out


def _attn_fwd(q, k, v, kc, vc, sink):
    length = q.shape[0]
    nb = length // BLOCK
    n_ctx = kc.shape[0]
    n_keys = 3 * BLOCK + n_ctx

    def body(sink_ref, q_ref, kp, k0, kn, vp, v0, vn, kc_ref, vc_ref, o_ref):
        n = pl.program_id(0)
        valid = _attn_mask(n, length, n_keys)
        qb = q_ref[...]
        kall = jnp.concatenate([kp[...], k0[...], kn[...], kc_ref[...]], axis=0)
        vall = jnp.concatenate([vp[...], v0[...], vn[...], vc_ref[...]], axis=0)
        outs = []
        for g in range(N_KV_HEADS):
            lanes = slice(HEAD_DIM * g, HEAD_DIM * (g + 1))
            s = jnp.where(valid, _dot_nt(_group_rows(qb, g), kall[:, lanes]) * SCALE, NEG_INF)
            sk = _group_sink(sink_ref, g)
            m = jnp.maximum(jnp.max(s, axis=-1, keepdims=True), sk)
            e = jnp.exp(s - m)
            den = jnp.sum(e, axis=-1, keepdims=True) + jnp.exp(sk - m)
            o = _dot(e.astype(BF16), vall[:, lanes]) / den
            outs += [o[BLOCK * j:BLOCK * (j + 1)] for j in range(GROUP)]
        o_ref[...] = jnp.concatenate(outs, axis=1).astype(BF16)

    return pl.pallas_call(
        body, name="attn_fwd", grid=(nb,), in_specs=_attn_specs(nb, n_ctx),
        out_specs=pl.BlockSpec((BLOCK, ATTN_WIDTH), lambda n: (n, 0)),
        out_shape=jax.ShapeDtypeStruct((length, ATTN_WIDTH + POOL_WIDTH), BF16), compiler_params=_cp(1),
    )(sink, q, k, k, k, v, v, v, kc, vc)


def _attn_bwd(q, k, v, kc, vc, sink, dmix):
    length = q.shape[0]
    nb = length // BLOCK
    n_ctx = kc.shape[0]
    n_keys = 3 * BLOCK + n_ctx

    def body(sink_ref, q_ref, kp, k0, kn, vp, v0, vn, kc_ref, vc_ref, do_ref,
             dq_ref, dkp_ref, dvp_ref, dkc_ref, dvc_ref, dsink_ref):
        n = pl.program_id(0)

        @pl.when(n == 0)
        def _():
            dkc_ref[...] = jnp.zeros_like(dkc_ref)
            dvc_ref[...] = jnp.zeros_like(dvc_ref)
            dsink_ref[...] = jnp.zeros_like(dsink_ref)

        valid = _attn_mask(n, length, n_keys)
        qb, dob = q_ref[...], do_ref[...]
        kall = jnp.concatenate([kp[...], k0[...], kn[...], kc_ref[...]], axis=0)
        vall = jnp.concatenate([vp[...], v0[...], vn[...], vc_ref[...]], axis=0)
        srow = lax.broadcasted_iota(jnp.int32, (8, 128), 0)
        slane = lax.broadcasted_iota(jnp.int32, (8, 128), 1)
        dqs, dks, dvs = [], [], []
        dsink = jnp.zeros((8, 128), F32)
        for g in range(N_KV_HEADS):
            lanes = slice(HEAD_DIM * g, HEAD_DIM * (g + 1))
            kg, vg = kall[:, lanes], vall[:, lanes]
            qg, dog = _group_rows(qb, g), _group_rows(dob, g)
            s = jnp.where(valid, _dot_nt(qg, kg) * SCALE, NEG_INF)
            sk = _group_sink(sink_ref, g)
            m = jnp.maximum(jnp.max(s, axis=-1, keepdims=True), sk)
            e = jnp.exp(s - m)
            inv = 1.0 / (jnp.sum(e, axis=-1, keepdims=True) + jnp.exp(sk - m))
            p = e * inv
            dp = _dot_nt(dog, vg)
            delta = jnp.sum(p * dp, axis=-1, keepdims=True)
            ds = (p * (dp - delta) * SCALE).astype(BF16)
            dq = _dot(ds, kg)
            dqs += [dq[BLOCK * j:BLOCK * (j + 1)] for j in range(GROUP)]
            dks.append(_dot_tn(ds, qg))
            dvs.append(_dot_tn(p.astype(BF16), dog))
            d_sink = jnp.exp(sk - m) * inv * delta
            for j in range(GROUP):
                total = -jnp.sum(d_sink[BLOCK * j:BLOCK * (j + 1)], axis=0, keepdims=True)
                dsink = dsink + jnp.where((srow == 0) & (slane == GROUP * g + j), total, 0.0)
        dq_ref[...] = jnp.concatenate(dqs, axis=1)
        dk = jnp.concatenate(dks, axis=1)
        dv = jnp.concatenate(dvs, axis=1)
        for j in range(3):
            dkp_ref[0, j] = dk[BLOCK * j:BLOCK * (j + 1)]
            dvp_ref[0, j] = dv[BLOCK * j:BLOCK * (j + 1)]
        dkc_ref[...] += dk[3 * BLOCK:]
        dvc_ref[...] += dv[3 * BLOCK:]
        dsink_ref[...] += dsink

    part = pl.BlockSpec((1, 3, BLOCK, KV_WIDTH), lambda n: (n, 0, 0, 0))
    ctx = pl.BlockSpec((n_ctx, KV_WIDTH), lambda n: (0, 0))
    return pl.pallas_call(
        body, name="attn_bwd", grid=(nb,),
        in_specs=_attn_specs(nb, n_ctx) + [pl.BlockSpec((BLOCK, ATTN_WIDTH), lambda n: (n, 0))],
        out_specs=(pl.BlockSpec((BLOCK, ATTN_WIDTH), lambda n: (n, 0)), part, part, ctx, ctx,
                   pl.BlockSpec((8, 128), lambda n: (0, 0))),
        out_shape=(jax.ShapeDtypeStruct((length, ATTN_WIDTH), F32),
                   jax.ShapeDtypeStruct((nb, 3, BLOCK, KV_WIDTH), F32), jax.ShapeDtypeStruct((nb, 3, BLOCK, KV_WIDTH), F32),
                   jax.ShapeDtypeStruct((n_ctx, KV_WIDTH), F32), jax.ShapeDtypeStruct((n_ctx, KV_WIDTH), F32),
                   jax.ShapeDtypeStruct((8, 128), F32)),
        compiler_params=_cp(1),
    )(sink, q, k, k, k, v, v, v, kc, vc, dmix)


def _assemble_dp(dq, dkp, dvp, du, cos, sin, after):
    length = dq.shape[0]
    nb = length // BLOCK

    def body(dq_ref, dka, dkb, dkc, dva, dvb, dvc, du_ref, cos_ref, sin_ref, after_ref, o_ref):
        n = pl.program_id(0)
        has_next = (n + 1 < nb).astype(F32)
        has_prev = (n > 0).astype(F32)
        cs, sn = cos_ref[...], -sin_ref[...]
        dk = dka[0, 0] * has_next + dkb[0, 0] + dkc[0, 0] * has_prev
        dv = dva[0, 0] * has_next + dvb[0, 0] + dvc[0, 0] * has_prev
        o_ref[:, :ATTN_WIDTH] = _rope(dq_ref[...], cs, sn).astype(BF16)
        o_ref[:, ATTN_WIDTH:ATTN_WIDTH + KV_WIDTH] = _rope(dk, cs, sn).astype(BF16)
        o_ref[:, ATTN_WIDTH + KV_WIDTH:ATTN_WIDTH + 2 * KV_WIDTH] = dv.astype(BF16)
        o_ref[:, ATTN_WIDTH + 2 * KV_WIDTH:] = du_ref[...]

    def part(slot, f):
        return pl.BlockSpec((1, 1, BLOCK, KV_WIDTH), lambda n: (f(n), slot, 0, 0))

    parts = [part(0, lambda n: jnp.minimum(n + 1, nb - 1)), part(1, lambda n: n), part(2, lambda n: jnp.maximum(n - 1, 0))]

    def tile(w):
        return pl.BlockSpec((BLOCK, w), lambda n: (n, 0))

    width = ATTN_WIDTH + 2 * KV_WIDTH + POOL_WIDTH
    return pl.pallas_call(
        body, name="assemble_dp", grid=(nb,),
        in_specs=[tile(ATTN_WIDTH)] + parts + parts + [tile(POOL_WIDTH), tile(128), tile(128), ANY],
        out_specs=tile(width), out_shape=jax.ShapeDtypeStruct((length, width), BF16), compiler_params=_cp(1),
    )(dq, dkp, dkp, dkp, dvp, dvp, dvp, du, cos, sin, after)


def _shift_rows(e, s):
    n = e.shape[0]
    return e if s % n == 0 else pltpu.roll(e, (-s) % n, 0)


def _window_sum(e, w, first):
    s, n = e, 1
    while n < w:
        s = s + _shift_rows(s, n)
        n *= 2
    return _shift_rows(s, first)


def _pool_geometry(i, tm, length):
    pos = i * tm - HALO + lax.broadcasted_iota(jnp.int32, (tm + 2 * HALO, 1), 0)
    inside = (pos >= 0) & (pos < length)
    inv_counts = []
    for w in POOL_WINDOWS:
        lo = jnp.clip(pos - w // 2, 0, length)
        hi = jnp.clip(pos - w // 2 + w, 0, length)
        inv_counts.append(1.0 / jnp.maximum(hi - lo, 1).astype(F32))
    return inside, inv_counts


def _halo_specs(tm, width, length, col=0):
    per = tm // HALO
    last = length // HALO - 1
    return [pl.BlockSpec((HALO, width), lambda i: (jnp.maximum(i * per - 1, 0), col)),
            pl.BlockSpec((tm, width), lambda i: (i, col)),
            pl.BlockSpec((HALO, width), lambda i: (jnp.minimum((i + 1) * per, last), col))]


def _pooled(ext, inv_counts, tm):
    outs = []
    for g, w in enumerate(POOL_WINDOWS):
        e = ext[:, POOL_GROUP_DIM * g:POOL_GROUP_DIM * (g + 1)]
        mean = _window_sum(e, w, -(w // 2)) * inv_counts[g]
        outs.append((mean - e)[HALO:HALO + tm])
    return outs


def _pool_fwd(u, pool_w, pool_scale, mix):
    length = u.shape[0]
    tm = _pick(length, 256, 128)

    def body(up, u0, un, w_ref, sc_ref, mix_ref, o_ref):
        inside, inv_counts = _pool_geometry(pl.program_id(0), tm, length)
        ext = jnp.where(inside, jnp.concatenate([up[...], u0[...], un[...]], axis=0), 0.0)
        pooled = _pooled(ext, inv_counts, tm)
        mixed = [_dot(pooled[g].astype(BF16), w_ref[g]) for g in range(len(POOL_WINDOWS))]
        o_ref[...] = (jnp.concatenate(mixed, axis=1) * sc_ref[...]).astype(BF16)

    return pl.pallas_call(
        body, name="pool_fwd", grid=(length // tm,),
        in_specs=_halo_specs(tm, POOL_WIDTH, length) + [pl.BlockSpec(pool_w.shape, lambda i: (0, 0, 0)), _row_spec(POOL_WIDTH), ANY],
        out_specs=pl.BlockSpec((tm, POOL_WIDTH), lambda i: (i, 1)),
        out_shape=jax.ShapeDtypeStruct(mix.shape, BF16), input_output_aliases={5: 0}, compiler_params=_cp(1),
    )(u, u, u, pool_w, pool_scale, mix)


def _pool_bwd(u, dmix, pool_w, pool_scale, after):
    length = u.shape[0]
    tm = _pick(length, 256, 128)
    n_g = len(POOL_WINDOWS)

    def body(up, u0, un, dp_, d0, dn_, w_ref, sc_ref, after_ref, du_ref, dw_ref, dsc_ref):
        i = pl.program_id(0)

        @pl.when(i == 0)
        def _():
            dw_ref[...] = jnp.zeros_like(dw_ref)
            dsc_ref[...] = jnp.zeros_like(dsc_ref)

        inside, inv_counts = _pool_geometry(i, tm, length)
        ext = jnp.where(inside, jnp.concatenate([up[...], u0[...], un[...]], axis=0), 0.0)
        dext = jnp.where(inside, jnp.concatenate([dp_[...], d0[...], dn_[...]], axis=0).astype(F32), 0.0)
        dmixed = (dext * sc_ref[...]).astype(BF16)
        pooled = _pooled(ext, inv_counts, tm)
        dus, dscs = [], []
        for g, w in enumerate(POOL_WINDOWS):
            lanes = slice(POOL_GROUP_DIM * g, POOL_GROUP_DIM * (g + 1))
            dpooled = _dot_nt(dmixed[:, lanes], w_ref[g])
            spread = _window_sum(dpooled * inv_counts[g], w, -(w // 2 - 1))
            dus.append((spread - dpooled)[HALO:HALO + tm])
            pb = pooled[g].astype(BF16)
            dw_ref[g] += _dot_tn(pb, dmixed[HALO:HALO + tm, lanes])
            prod = dext[HALO:HALO + tm, lanes] * _dot(pb, w_ref[g])
            dscs.append(_fold8(prod))
        du_ref[...] = jnp.concatenate(dus, axis=1).astype(BF16)
        dsc_ref[...] += jnp.concatenate(dscs, axis=1)

    return pl.pallas_call(
        body, name="pool_bwd", grid=(length // tm,),
        in_specs=_halo_specs(tm, POOL_WIDTH, length) + _halo_specs(tm, POOL_WIDTH, length, col=1)
        + [pl.BlockSpec(pool_w.shape, lambda i: (0, 0, 0)), _row_spec(POOL_WIDTH), ANY],
        out_specs=(pl.BlockSpec((tm, POOL_WIDTH), lambda i: (i, 0)), pl.BlockSpec((n_g, POOL_GROUP_DIM, POOL_GROUP_DIM), lambda i: (0, 0, 0)),
                   pl.BlockSpec((8, POOL_WIDTH), lambda i: (0, 0))),
        out_shape=(jax.ShapeDtypeStruct((length, POOL_WIDTH), BF16), jax.ShapeDtypeStruct((n_g, POOL_GROUP_DIM, POOL_GROUP_DIM), F32),
                   jax.ShapeDtypeStruct((8, POOL_WIDTH), F32)),
        compiler_params=_cp(1),
    )(u, u, u, dmix, dmix, dmix, pool_w, pool_scale, after)


def _mixer_out(mix, w_out, x, g_a, nmw, sh_m, sc_m, after):
    t, d = x.shape
    tm = _pick(t, 256, 128)

    def epi(acc, ex, outs):
        x_ref, ga, nw, sh, sc = ex
        x1_ref, mo_ref, hm_ref = outs

        def rows(rs):
            mo = acc[rs, :]
            x1 = x_ref[rs, :] + ga[...] * mo
            x1_ref[rs, :] = x1
            mo_ref[rs, :] = mo.astype(BF16)
            r = lax.rsqrt(jnp.mean(x1 * x1, axis=-1, keepdims=True) + EPS)
            hm_ref[rs, :] = (((x1 * r) * nw[...]) * (1.0 + sc[...]) + sh[...]).astype(BF16)

        _row_loop(tm, rows)

    tile = pl.BlockSpec((tm, d), lambda j, i, k: (i, 0))
    return _mm("mixer_out", mix, w_out, nt=False, tm=tm, tn=d, tk=mix.shape[1], epi=epi, after=after,
               extras=(x, g_a, nmw, sh_m, sc_m), extra_specs=[tile] + [_row_spec(d)] * 4,
               out_shape=(jax.ShapeDtypeStruct((t, d), F32), jax.ShapeDtypeStruct((t, d), BF16), jax.ShapeDtypeStruct((t, d), BF16)),
               out_specs=(tile, tile, tile))


def _mlp_up(hm, w_up):
    t, d = hm.shape
    tm = _pick(t, 512, 256, 128)
    tn = 2048

    def epi(acc, ex, outs):
        outs[0][...] = jnp.square(jnp.maximum(acc[...], 0.0)).astype(BF16)

    return _mm("mlp_up", hm, w_up, nt=False, tm=tm, tn=tn, tk=d, epi=epi,
               out_shape=(jax.ShapeDtypeStruct((t, w_up.shape[1]), BF16),),
               out_specs=(pl.BlockSpec((tm, tn), lambda j, i, k: (i, j)),))[0]


def _mm_f32(name, a, b, *, nt, after):
    m, kdim = a.shape
    n = b.shape[0] if nt else b.shape[1]
    tm, tn = _pick(m, 1024, 512, 256, 128), _pick(n, 1024)

    def epi(acc, ex, outs):
        outs[0][...] = acc[...]

    return _mm(name, a, b, nt=nt, tm=tm, tn=tn, tk=_pick(kdim, 2048), epi=epi, after=after,
               out_shape=(jax.ShapeDtypeStruct((m, n), F32),), out_specs=(pl.BlockSpec((tm, tn), lambda j, i, k: (i, j)),))[0]


def _rows_call(name, rows_fn, tiles, vecs, out_shape, n_stats, after):
    t, d = tiles[0].shape
    tm = _pick(t, 256, 128)
    n_t, n_v = len(tiles), len(vecs)

    def body(*refs):
        st_ref = refs[-1]

        @pl.when(pl.program_id(0) == 0)
        def _():
            st_ref[...] = jnp.zeros_like(st_ref)

        _row_loop(tm, lambda rs: rows_fn(rs, refs[:n_t], refs[n_t:n_t + n_v], refs[n_t + n_v + 1:-1], st_ref))

    tile = pl.BlockSpec((tm, d), lambda i: (i, 0))
    return pl.pallas_call(
        body, name=name, grid=(t // tm,), in_specs=[tile] * n_t + [_row_spec(d)] * n_v + [ANY],
        out_specs=tuple([tile] * len(out_shape)) + (_stat_spec(n_stats, d),),
        out_shape=tuple(out_shape) + (jax.ShapeDtypeStruct((n_stats, 8, d), F32),), compiler_params=_cp(1),
    )(*tiles, *vecs, after)


def _loss_rows(dn, x1, target, g_m, fw):
    t, d = x1.shape

    def rows_fn(rs, tiles, vecs, outs, st_ref):
        dn_ref, x1_ref, t_ref = tiles
        gm, fw_ref = vecs
        dx2_ref, ddn_ref = outs
        dnv = dn_ref[rs, :]
        x2 = x1_ref[rs, :] + gm[...] * dnv
        r = lax.rsqrt(jnp.mean(x2 * x2, axis=-1, keepdims=True) + EPS)
        xh = x2 * r
        diff = xh * fw_ref[...] - t_ref[rs, :]
        dy = diff * (1.0 / d)
        dxh = dy * fw_ref[...]
        dx2 = r * (dxh - xh * jnp.mean(dxh * xh, axis=-1, keepdims=True))
        dx2_ref[rs, :] = dx2
        ddn_ref[rs, :] = (dx2 * gm[...]).astype(BF16)
        st_ref[0] += _fold8(diff * diff)
        st_ref[1] += _fold8(dy * xh)
        st_ref[2] += _fold8(dx2 * dnv)

    return _rows_call("loss_rows", rows_fn, (dn, x1, target), (g_m, fw),
                      (jax.ShapeDtypeStruct((t, d), F32), jax.ShapeDtypeStruct((t, d), BF16)), 3, g_m)


def _mlp_dx_rows(dhm, x1, dx2, mo, nmw, sc_m, g_a, after):
    t, d = x1.shape

    def rows_fn(rs, tiles, vecs, outs, st_ref):
        dh_ref, x1_ref, dx2_ref, mo_ref = tiles
        nw, sc, ga = vecs
        dx1_ref, dmi_ref = outs
        dx1 = _norm_bwd_rows(dh_ref[rs, :], x1_ref[rs, :], nw[...], sc[...], st_ref) + dx2_ref[rs, :]
        dx1_ref[rs, :] = dx1
        dmi_ref[rs, :] = (dx1 * ga[...]).astype(BF16)
        st_ref[3] += _fold8(dx1 * mo_ref[rs, :].astype(F32))

    return _rows_call("mlp_dx_rows", rows_fn, (dhm, x1, dx2, mo), (nmw, sc_m, g_a),
                      (jax.ShapeDtypeStruct((t, d), F32), jax.ShapeDtypeStruct((t, d), BF16)), 4, after)


def _mlp_dact(ddn, w_down, act):
    t, d = ddn.shape
    tm = _pick(t, 512, 256, 128)
    tn = 2048

    def epi(acc, ex, outs):
        outs[0][...] = (acc[...] * (2.0 * jnp.sqrt(ex[0][...].astype(F32)))).astype(BF16)

    tile = pl.BlockSpec((tm, tn), lambda j, i, k: (i, j))
    return _mm("mlp_dact", ddn, w_down, nt=True, tm=tm, tn=tn, tk=d, epi=epi, extras=(act,), extra_specs=[tile],
               out_shape=(jax.ShapeDtypeStruct(act.shape, BF16),), out_specs=(tile,))[0]


def _norm_bwd_rows(dh, xv, nw, sc, st_ref):
    r = lax.rsqrt(jnp.mean(xv * xv, axis=-1, keepdims=True) + EPS)
    xh = xv * r
    dy = dh * (1.0 + sc)
    st_ref[0] += _fold8(dh)
    st_ref[1] += _fold8(dh * (xh * nw))
    st_ref[2] += _fold8(dy * xh)
    dxh = dy * nw
    return r * (dxh - xh * jnp.mean(dxh * xh, axis=-1, keepdims=True))


def _mixer_dmix(dmi, w_out):
    t, d = dmi.shape
    tm = _pick(t, 512, 256, 128)

    def epi(acc, ex, outs):
        outs[0][...] = acc[...].astype(BF16)

    n = w_out.shape[0]
    return _mm("mixer_dmix", dmi, w_out, nt=True, tm=tm, tn=n, tk=d, epi=epi,
               out_shape=(jax.ShapeDtypeStruct((t, n), BF16),), out_specs=(pl.BlockSpec((tm, n), lambda j, i, k: (i, 0)),))[0]


def _mixer_dx(name, dp, w_in, x, dx1, naw, sc_a, after):
    t, d = x.shape
    tm = _pick(t, 256, 128)

    def epi(acc, ex, outs):
        x_ref, dx1_ref, nw, sc = ex
        gx_ref, st_ref = outs

        @pl.when(pl.program_id(1) == 0)
        def _():
            st_ref[...] = jnp.zeros_like(st_ref)

        def rows(rs):
            gx_ref[rs, :] = _norm_bwd_rows(acc[rs, :], x_ref[rs, :], nw[...], sc[...], st_ref) + dx1_ref[rs, :]

        _row_loop(tm, rows)

    tile = pl.BlockSpec((tm, d), lambda j, i, k: (i, 0))
    return _mm(name, dp, w_in, nt=True, tm=tm, tn=d, tk=dp.shape[1], epi=epi, after=after,
               extras=(x, dx1, naw, sc_a), extra_specs=[tile, tile, _row_spec(d), _row_spec(d)],
               out_shape=(jax.ShapeDtypeStruct((t, d), F32), jax.ShapeDtypeStruct((3, 8, d), F32)),
               out_specs=(tile, _stat_spec(3, d)))


def _silu(v):
    return v / (1.0 + jnp.exp(-v))


def _ada_fwd(cond, w_ada, b_ada):
    d, n = w_ada.shape
    tn = 512

    def body(c_ref, w_ref, b_ref, o_ref):
        o_ref[...] = _dot(_silu(c_ref[...]).astype(BF16), w_ref[...].astype(BF16)) + b_ref[...]

    return pl.pallas_call(
        body, name="ada_fwd", grid=(n // tn,),
        in_specs=[pl.BlockSpec(cond.shape, lambda j: (0, 0)), pl.BlockSpec((d, tn), lambda j: (0, j)), pl.BlockSpec((1, tn), lambda j: (0, j))],
        out_specs=pl.BlockSpec((cond.shape[0], tn), lambda j: (0, j)), out_shape=jax.ShapeDtypeStruct((cond.shape[0], n), F32),
        compiler_params=_cp(1),
    )(cond, w_ada, b_ada)


def _adamw_math(w, g, m, v):
    m = ADAM_B1 * m + (1.0 - ADAM_B1) * g
    v = ADAM_B2 * v + (1.0 - ADAM_B2) * jnp.square(g)
    m_hat = m / (1.0 - ADAM_B1 ** ADAM_STEP)
    v_hat = v / (1.0 - ADAM_B2 ** ADAM_STEP)
    return -ADAM_LR * (m_hat / (jnp.sqrt(v_hat) + ADAM_EPS) + ADAM_WD * w), m, v


def _ada_bwd(cond, dm, w_ada, m_ada, v_ada):
    d, n = w_ada.shape
    tn = 256
    rows = cond.shape[0]

    def body(c_ref, dm_ref, w_ref, m_ref, v_ref, g_ref, dl_ref, nm_ref, nv_ref, pc_ref):
        @pl.when(pl.program_id(0) == 0)
        def _():
            pc_ref[...] = jnp.zeros_like(pc_ref)

        dmb = dm_ref[...].astype(BF16)
        w = w_ref[...]
        g = _dot_tn(_silu(c_ref[...]).astype(BF16), dmb)
        g_ref[...] = g
        dl_ref[...], nm_ref[...], nv_ref[...] = _adamw_math(w, g, m_ref[...], v_ref[...])
        pc_ref[...] += _dot_nt(dm_ref[8:16, :].astype(BF16), w.astype(BF16))

    tile = pl.BlockSpec((d, tn), lambda j: (0, j))
    like = jax.ShapeDtypeStruct((d, n), F32)
    return pl.pallas_call(
        body, name="ada_bwd", grid=(n // tn,),
        in_specs=[pl.BlockSpec((rows, d), lambda j: (0, 0)), pl.BlockSpec((rows, tn), lambda j: (0, j)), tile, tile, tile],
        out_specs=(tile, tile, tile, tile, pl.BlockSpec((8, d), lambda j: (0, 0))),
        out_shape=(like, like, like, like, jax.ShapeDtypeStruct((8, d), F32)), compiler_params=_cp(1),
    )(cond, dm, w_ada, m_ada, v_ada)


def _adamw(name, w, g, m, v):
    return _ew(name, _adamw_math, [w, g, m, v], [F32, F32, F32])


def _colsum(st):
    return jnp.sum(st, axis=1)


def kernel(x, c, ctx, c_ctx, norm_attn_w, norm_mlp_w, w_ada, b_ada, w_in, attn_sink, pool_w, pool_scale, w_out, w_mlp_up, w_mlp_down, final_norm_w, loss_target, m_c_ctx, m_norm_attn_w, m_norm_mlp_w, m_w_ada, m_b_ada, m_w_in, m_attn_sink, m_pool_w, m_pool_scale, m_w_out, m_w_mlp_up, m_w_mlp_down, m_final_norm_w, v_c_ctx, v_norm_attn_w, v_norm_mlp_w, v_w_ada, v_b_ada, v_w_in, v_attn_sink, v_pool_w, v_pool_scale, v_w_out, v_w_mlp_up, v_w_mlp_down, v_final_norm_w):
    length, d = x.shape[1], x.shape[2]
    n_ctx = ctx.shape[1]
    pos = (lax.axis_index("x"), lax.axis_index("y"), lax.axis_index("c"))
    me, chip = _dev_index(pos), _chip_index(pos)
    xs, tgt, cx = x.reshape(length, d), loss_target.reshape(length, d), ctx.reshape(n_ctx, d)
    n_ada = w_ada.shape[2]

    c_all = _allgather8("gather_c", jnp.pad(c, ((0, 7), (0, 0))))
    cond = jnp.concatenate([c_all[:, 0, :], jnp.pad(c_ctx[None, :], ((0, 7), (0, 0)))], axis=0)
    b_shard = lax.dynamic_slice_in_dim(b_ada, chip * n_ada, n_ada, axis=1)
    mod_all = _allgather8("gather_mod", _ada_fwd(cond, w_ada[0], b_shard))
    mod = jnp.concatenate([mod_all[0], mod_all[2], mod_all[4], mod_all[6]], axis=1)
    mine = lax.dynamic_slice_in_dim(mod, me, 1, axis=0)
    sh_a, sc_a, g_a, sh_m, sc_m, g_m = [mine[:, d * i:d * (i + 1)] for i in range(6)]
    csh_a, csc_a = mod[8:9, :d], mod[8:9, d:2 * d]

    mixer_bigs = [_Big("col", w_in.shape[1:]), _Big("pool", pool_w.shape[1:]), _Big("row", w_out.shape[1:])]
    mlp_bigs = [_Big("col", w_mlp_up.shape[1:]), _Big("row", w_mlp_down.shape[1:])]
    placed = [_cast_place(f"place_{i}", b, s, c) for i, (b, s) in enumerate(zip(mixer_bigs, [w_in[0], pool_w[0], w_out[0]]))]
    win_b, pw_b, wout_b = _gather_weights("mixer", mixer_bigs, placed)
    wout_b = wout_b.reshape(-1, d)
    placed = [_cast_place(f"place_mlp_{i}", b, s, pw_b) for i, (b, s) in enumerate(zip(mlp_bigs, [w_mlp_up[0], w_mlp_down[0]]))]
    flight = _split("gather_mlp_ici", placed, _gather_ici_remote(mlp_bigs, 0))

    cos, sin = _rope_tables(length, True)
    one, zero = _rope_tables(n_ctx, False)
    h, q, k, v, u = _mixer_in("mixer_in", xs, norm_attn_w, sh_a, sc_a, win_b, cos, sin, flight[3])
    hc, _, kc, vc, _ = _mixer_in("mixer_in_ctx", cx, norm_attn_w, csh_a, csc_a, win_b, one, zero, flight[3])
    mix = _pool_fwd(u, pw_b, pool_scale, _attn_fwd(q, k, v, kc, vc, attn_sink))
    flight = _split("gather_mlp_d2d", _join(flight, mix), _gather_d2d_remote(mlp_bigs, 0))
    x1, mo, hm = _mixer_out(mix, wout_b, xs, g_a, norm_mlp_w, sh_m, sc_m, flight[3])
    wup_b, wdn_b = _join(flight, hm)
    wdn_b = wdn_b.reshape(-1, d)
    act = _mlp_up(hm, wup_b)
    dn = _mm_f32("mlp_down", act, wdn_b, nt=False, after=c)
    dx2, ddn, st_loss = _loss_rows(dn, x1, tgt, g_m, final_norm_w[None, :])
    st_loss = _colsum(st_loss)
    loss = lax.psum(0.5 / d * jnp.sum(st_loss[0]), ("x", "y", "c"))

    tt = _pick(length, 2048, 1024, 512, 256, 128)
    g_wdn = _mm_tn("grad_w_down", act, ddn, BF16, tmo=1024, tn=d, tt=tt)
    dup = _mlp_dact(ddn, wdn_b, act)
    g_wup = _mm_tn("grad_w_up", hm, dup, BF16, tmo=d, tn=1024, tt=tt)
    empty = lambda shapes: [lax.empty(s.shape, s.dtype) for s in shapes]
    grads = [g_wup, g_wdn.reshape(mlp_bigs[1].full_shape)]
    flight = _split("reduce_mlp_d2d", grads + empty(_halves(mlp_bigs)), _reduce_d2d_remote(mlp_bigs))
    dhm = _mm_f32("mlp_dhm", dup, wup_b, nt=True, after=flight[3])
    landed = _join(flight, dhm)
    mlp_chip = _chip_sums("mlp", mlp_bigs, landed[:2], landed[2:])
    flight = _split("reduce_mlp_ici", mlp_chip + empty(_thirds(mlp_bigs)), _reduce_ici_remote(mlp_bigs))
    dx1, dmi, st_mlp = _mlp_dx_rows(dhm, x1, dx2, mo, norm_mlp_w, sc_m, g_a, flight[3])
    st_mlp = _colsum(st_mlp)
    g_wout = _mm_tn("grad_w_out", mix, dmi, BF16, tmo=1024, tn=d, tt=tt)
    dmix = _mixer_dmix(dmi, wout_b)
    dq, dkp, dvp, dkc, dvc, dsink = _attn_bwd(q, k, v, kc, vc, attn_sink, dmix)
    landed = _join(flight, dq)
    flight = _split("reduce_mlp_share", _piece_sums("mlp", mlp_bigs, landed[:2], landed[2:]), _share_remote(mlp_bigs, 0))
    du, g_pw, st_pool = _pool_bwd(u, dmix, pw_b, pool_scale, flight[3])
    g_mlp = _join(flight, du)

    wo_bigs, win_bigs = mixer_bigs[1:], mixer_bigs[:1]
    wo_chip = _reduce_to_chip("wo", wo_bigs, [g_pw.astype(BF16), g_wout.reshape(wo_bigs[1].full_shape)])
    flight = _split("reduce_wo_ici", wo_chip + empty(_thirds(wo_bigs)), _reduce_ici_remote(wo_bigs))
    dp = _assemble_dp(dq, dkp, dvp, du, cos, sin, flight[3])
    dpc = jnp.concatenate([jnp.zeros((n_ctx, ATTN_WIDTH), BF16), dkc.astype(BF16), dvc.astype(BF16),
                           jnp.zeros((n_ctx, POOL_WIDTH), BF16)], axis=1)
    g_win = _mm_tn("grad_w_in", h, dp, BF16, tmo=d, tn=dp.shape[1] // 2, tt=_pick(length, 1024, 512, 256, 128), more=(hc, dpc))
    wo_landed = _join(flight, g_win)
    win_chip = _reduce_to_chip("win", win_bigs, [g_win])
    flight = _split("reduce_win_ici", win_chip + empty(_thirds(win_bigs)), _reduce_ici_remote(win_bigs))
    grad_x, st_mix = _mixer_dx("mixer_dx", dp, win_b, xs, dx1, norm_attn_w, sc_a, flight[3])
    _, st_ctx = _mixer_dx("mixer_dx_ctx", dpc, win_b, cx, jnp.zeros((n_ctx, d), F32), norm_attn_w, csc_a, flight[3])
    st_mix, st_ctx = _colsum(st_mix), _colsum(st_ctx)
    win_landed = _join(flight, grad_x)
    g_mixer = (_reduce_finish("win", win_bigs, win_landed[:1], win_landed[1:])
               + _reduce_finish("wo", wo_bigs, wo_landed[:2], wo_landed[2:]))

    zrow = jnp.zeros((d,), F32)
    pad = lambda a: jnp.pad(a, (0, d - a.shape[0]))
    mine_rows = [st_mix[0], st_mix[1], st_mlp[3], st_mlp[0], st_mlp[1], st_loss[2],
                 st_ctx[0], st_ctx[1],
                 st_mix[2] + st_ctx[2], st_mlp[2], st_loss[1],
                 pad(jnp.sum(st_pool, axis=0)), pad(dsink[0, :N_Q_HEADS])] + [zrow] * 3
    small_all = _allgather8("gather_small", jnp.concatenate(mine_rows).reshape(len(mine_rows), d))
    small = small_all[0]
    for i in range(1, 8):
        small = small + small_all[i]
    dm_rows = small_all[:, 0:6, :].reshape(8, 6 * d)
    dm_ctx = jnp.concatenate([small[6], small[7], jnp.zeros((4 * d,), F32)])[None, :]
    dm = jnp.concatenate([dm_rows, jnp.pad(dm_ctx, ((0, 7), (0, 0)))], axis=0)
    g_bada = jnp.sum(dm[:9], axis=0, keepdims=True)
    dm_shard = lax.dynamic_slice_in_dim(dm, chip * n_ada, n_ada, axis=1)
    g_wada, dl_wada, nm_wada, nv_wada, part_cctx = _ada_bwd(cond, dm_shard, w_ada[0], m_w_ada[0], v_w_ada[0])
    cctx_all = _allgather8("gather_cctx", part_cctx)
    dsilu_in = cctx_all[0, 0] + cctx_all[2, 0] + cctx_all[4, 0] + cctx_all[6, 0]
    sig = 1.0 / (1.0 + jnp.exp(-c_ctx))
    g_cctx = dsilu_in * (sig * (1.0 + c_ctx * (1.0 - sig)))

    g_shards = g_mixer + g_mlp
    big_w = [w_in, pool_w, w_out, w_mlp_up, w_mlp_down]
    big_m = [m_w_in, m_pool_w, m_w_out, m_w_mlp_up, m_w_mlp_down]
    big_v = [v_w_in, v_pool_w, v_w_out, v_w_mlp_up, v_w_mlp_down]
    big_names = ["w_in", "pool_w", "w_out", "w_mlp_up", "w_mlp_down"]
    res = {}
    for nm, w_, g_, m_, v_ in zip(big_names, big_w, g_shards, big_m, big_v):
        g_ = g_.reshape(w_.shape)
        res[nm] = (g_,) + tuple(_adamw("adamw_" + nm, w_, g_, m_, v_))
    res["w_ada"] = (g_wada[None], dl_wada[None], nm_wada[None], nv_wada[None])

    def pack(cc, na, nm_, ba, sk, ps, fn):
        flat = [cc.reshape(-1), na.reshape(-1), nm_.reshape(-1), ba.reshape(-1), pad(sk.reshape(-1)), pad(ps.reshape(-1)),
                fn.reshape(-1), jnp.zeros((4 * d,), F32)]
        return jnp.concatenate(flat).reshape(16, d)

    w_s = pack(c_ctx, norm_attn_w, norm_mlp_w, b_ada, attn_sink, pool_scale, final_norm_w)
    m_s = pack(m_c_ctx, m_norm_attn_w, m_norm_mlp_w, m_b_ada, m_attn_sink, m_pool_scale, m_final_norm_w)
    v_s = pack(v_c_ctx, v_norm_attn_w, v_norm_mlp_w, v_b_ada, v_attn_sink, v_pool_scale, v_final_norm_w)
    g_s = pack(g_cctx, small[8], small[9], g_bada, small[12][:N_Q_HEADS], small[11][:POOL_WIDTH], small[10])
    small_out = [g_s] + _adamw("adamw_small", w_s, g_s, m_s, v_s)

    def unpack(p):
        return {"c_ctx": p[0], "norm_attn_w": p[1:2], "norm_mlp_w": p[2:3], "b_ada": p[3:9].reshape(1, 6 * d),
                "attn_sink": p[9:10, :N_Q_HEADS], "pool_scale": p[10:11, :POOL_WIDTH], "final_norm_w": p[11]}

    small_res = [unpack(p) for p in small_out]
    order = ["c_ctx", "norm_attn_w", "norm_mlp_w", "w_ada", "b_ada", "w_in", "attn_sink", "pool_w", "pool_scale",
             "w_out", "w_mlp_up", "w_mlp_down", "final_norm_w"]
    outs = [loss, grad_x.reshape(x.shape)]
    for kind in range(4):
        for nm in order:
            outs.append(res[nm][kind] if nm in res else small_res[kind][nm])
    return tuple(outs)
```

```python
import functools

import jax
import jax.numpy as jnp
from jax import lax
from jax.experimental import pallas as pl
from jax.experimental.pallas import tpu as pltpu

F32 = jnp.float32
BF16 = jnp.bfloat16
EPS = 1e-6
NEG_INF = -1e30
HEAD_DIM = 64
N_Q_HEADS = 16
N_KV_HEADS = 4
GROUP = N_Q_HEADS // N_KV_HEADS
ATTN_WIDTH = N_Q_HEADS * HEAD_DIM
KV_WIDTH = N_KV_HEADS * HEAD_DIM
POOL_WINDOWS = (2, 4, 8, 16)
POOL_GROUP_DIM = 256
POOL_WIDTH = len(POOL_WINDOWS) * POOL_GROUP_DIM
BLOCK = 128
GRID_W = 64
ROPE_BASE = 10000.0
SCALE = HEAD_DIM ** -0.5
HALO = 16
ROWS = 64
ADAM_LR, ADAM_B1, ADAM_B2, ADAM_EPS, ADAM_WD, ADAM_STEP = 0.001, 0.9, 0.999, 1e-08, 0.01, 10
MESH = pl.DeviceIdType.MESH
MIB = 1024 * 1024
ANY = pl.BlockSpec(memory_space=pl.ANY)


def _cp(n_axes, vmem_mib=48):
    return pltpu.CompilerParams(dimension_semantics=("arbitrary",) * n_axes, vmem_limit_bytes=vmem_mib * MIB)


def _row_loop(rows, fn):
    def body(r, carry):
        fn(pl.ds(pl.multiple_of(r * ROWS, ROWS), ROWS))
        return carry

    lax.fori_loop(0, rows // ROWS, body, 0)


def _fold8(v):
    s = v[0:8]
    for t in range(1, v.shape[0] // 8):
        s = s + v[8 * t:8 * t + 8]
    return s


def _dot(a, b):
    return jnp.dot(a, b, preferred_element_type=F32)


def _dot_nt(a, b):
    return lax.dot_general(a, b, (((1,), (1,)), ((), ())), preferred_element_type=F32)


def _dot_tn(a, b):
    return lax.dot_general(a, b, (((0,), (0,)), ((), ())), preferred_element_type=F32)


def _pick(n, *cands):
    for t in cands:
        if n % t == 0:
            return t
    return n


def _flip(pos, mask):
    return tuple((1 - v) if (mask >> (2 - i)) & 1 else v for i, v in enumerate(pos))


def _exchange(name, ins, out_shapes, remote, local=(), aliases=None):
    n_io = len(ins) + len(out_shapes)

    def body(*refs):
        io = refs[:n_io]
        send_sems, recv_sems, local_sems = refs[n_io:]
        me = (lax.axis_index("x"), lax.axis_index("y"), lax.axis_index("c"))

        def copy(i, sender):
            mask, src_fn, dst_fn = remote[i]
            return pltpu.make_async_remote_copy(
                src_ref=src_fn(io, sender), dst_ref=dst_fn(io, sender), send_sem=send_sems.at[i],
                recv_sem=recv_sems.at[i], device_id=_flip(sender, mask), device_id_type=MESH)

        own = [pltpu.make_async_copy(s(io, me), d(io, me), local_sems.at[i]) for i, (s, d) in enumerate(local)]
        for cp in own:
            cp.start()
        sends = [copy(i, me) for i in range(len(remote))]
        for cp in sends:
            cp.start()
        for i in range(len(remote)):
            copy(i, _flip(me, remote[i][0])).wait_recv()
        for cp in sends:
            cp.wait_send()
        for cp in own:
            cp.wait()

    return pl.pallas_call(
        body, name=name, out_shape=tuple(out_shapes),
        in_specs=[ANY] * len(ins), out_specs=tuple([ANY] * len(out_shapes)),
        scratch_shapes=[pltpu.SemaphoreType.DMA((len(remote),)), pltpu.SemaphoreType.DMA((len(remote),)),
                        pltpu.SemaphoreType.DMA((max(len(local), 1),))],
        input_output_aliases=aliases or {},
    )(*ins)


HBM = pl.BlockSpec(memory_space=pltpu.HBM)
SEM = pl.BlockSpec(memory_space=pltpu.SEMAPHORE)
EFFECT = pltpu.SideEffectType.DATAFLOW_SIDE_EFFECTING


def _split_copy(remote, i, io, send_sems, recv_sems, sender):
    mask, src_fn, dst_fn = remote[i]
    return pltpu.make_async_remote_copy(
        src_ref=src_fn(io, sender), dst_ref=dst_fn(io, sender), send_sem=send_sems.at[i],
        recv_sem=recv_sems.at[i], device_id=_flip(sender, mask), device_id_type=MESH)


def _exchange_start(name, bufs, remote):
    n, r = len(bufs), len(remote)

    def body(*refs):
        io, send_sems, recv_sems, token = refs[:n], refs[2 * n], refs[2 * n + 1], refs[2 * n + 2]
        me = (lax.axis_index("x"), lax.axis_index("y"), lax.axis_index("c"))
        for i in range(r):
            _split_copy(remote, i, io, send_sems, recv_sems, me).start()
        token[...] = jnp.zeros_like(token)

    res = pl.pallas_call(
        body, name=name,
        out_shape=tuple(pltpu.HBM(b.shape, b.dtype) for b in bufs)
        + (pltpu.SemaphoreType.DMA((r,)), pltpu.SemaphoreType.DMA((r,)), jax.ShapeDtypeStruct((8, 128), F32)),
        in_specs=[HBM] * n, out_specs=tuple([HBM] * n) + (SEM, SEM, pl.BlockSpec(memory_space=pltpu.VMEM)),
        input_output_aliases={i: i for i in range(n)}, compiler_params=pltpu.CompilerParams(has_side_effects=EFFECT),
    )(*[pltpu.with_memory_space_constraint(b, pltpu.HBM) for b in bufs])
    return list(res[:n]), res[n], res[n + 1], res[n + 2]


def _exchange_wait(name, bufs, send_sems, recv_sems, remote, after):
    n, r = len(bufs), len(remote)

    def body(*refs):
        io, ss, rs = refs[:n], refs[n], refs[n + 1]
        me = (lax.axis_index("x"), lax.axis_index("y"), lax.axis_index("c"))
        for i in range(r):
            _split_copy(remote, i, io, ss, rs, _flip(me, remote[i][0])).wait_recv()
        for i in range(r):
            _split_copy(remote, i, io, ss, rs, me).wait_send()

    return list(pl.pallas_call(
        body, name=name, out_shape=tuple(pltpu.HBM(b.shape, b.dtype) for b in bufs),
        in_specs=[HBM] * n + [SEM, SEM, ANY], out_specs=tuple([HBM] * n),
        input_output_aliases={i: i for i in range(n)}, compiler_params=pltpu.CompilerParams(has_side_effects=EFFECT),
    )(*bufs, send_sems, recv_sems, after))


def _my_c():
    return lax.axis_index("c")


def _my_chip():
    return 2 * lax.axis_index("x") + lax.axis_index("y")


def _dev_index(pos):
    return 4 * pos[0] + 2 * pos[1] + pos[2]


def _chip_index(pos):
    return 2 * pos[0] + pos[1]


def _allgather8(name, v):
    out = jax.ShapeDtypeStruct((8,) + v.shape, v.dtype)
    remote = [(mask, lambda io, pos: io[0], lambda io, pos: io[1].at[_dev_index(pos)]) for mask in range(1, 8)]
    local = [(lambda io, pos: io[0], lambda io, pos: io[1].at[_dev_index(pos)])]
    return _exchange(name, [v], [out], remote, local)[0]


class _Big:
    def __init__(self, kind, shard_shape):
        self.kind = kind
        self.shard_shape = tuple(shard_shape)
        if kind == "col":
            r, cs = shard_shape
            self.full_shape = (r, 4 * cs)
            self.piece_shape = (r // 2, cs)
            self.half_shape = (r // 2, 4 * cs)
        elif kind == "row":
            rs, c = shard_shape
            self.full_shape = (4, 2, rs // 2, c)
            self.piece_shape = (1, 1, rs // 2, c)
            self.half_shape = (4, 1, rs // 2, c)
        else:
            self.full_shape = (4, 256, 256)
            self.piece_shape = (2, 64, 256)
            self.half_shape = (2, 256, 256)

    def shard_as_pieces(self, a):
        return a.reshape((1, 2) + self.piece_shape[2:]) if self.kind == "row" else a

    def piece(self, ref, k, h):
        if self.kind == "col":
            r, cs = self.piece_shape
            return ref.at[pl.ds(h * r, r), pl.ds(k * cs, cs)]
        if self.kind == "row":
            return ref.at[pl.ds(k, 1), pl.ds(h, 1)]
        return ref.at[pl.ds(2 * h, 2), pl.ds(64 * k, 64)]

    def half_of_shard(self, ref, h):
        if self.kind == "col":
            return ref.at[pl.ds(h * self.piece_shape[0], self.piece_shape[0])]
        if self.kind == "row":
            return ref.at[:, pl.ds(h, 1)]
        return ref.at[pl.ds(2 * h, 2)]

    def half_of_full(self, ref, h):
        if self.kind == "col":
            return ref.at[pl.ds(h * self.half_shape[0], self.half_shape[0])]
        if self.kind == "row":
            return ref.at[:, pl.ds(h, 1)]
        return ref.at[pl.ds(2 * h, 2)]

    def piece_of_half(self, ref, k):
        if self.kind == "col":
            return ref.at[:, pl.ds(k * self.piece_shape[1], self.piece_shape[1])]
        if self.kind == "row":
            return ref.at[pl.ds(k, 1)]
        return ref.at[:, pl.ds(64 * k, 64)]


CHIP_MASKS = (4, 2, 6)


def _cast_place(name, big, shard, after):
    if big.kind == "col":
        r, cs = big.shard_shape
        tr = _pick(r, 512, 256, 128)
        src, grid, blk = shard, (r // tr,), (tr, cs)
        imap, omap = (lambda i: (i, 0)), (lambda i: (i, _my_chip()))
    elif big.kind == "row":
        rs, c = big.shard_shape
        tr = _pick(rs // 2, 256, 128)
        src, grid, blk = big.shard_as_pieces(shard), (2, rs // 2 // tr), (1, 1, tr, c)
        imap, omap = (lambda h, i: (0, h, i, 0)), (lambda h, i: (_my_chip(), h, i, 0))
    else:
        src, grid, blk = shard, (1,), big.shard_shape
        imap, omap = (lambda i: (0, 0, 0)), (lambda i: (0, _my_chip(), 0))

    def body(s_ref, after_ref, o_ref):
        o_ref[...] = s_ref[...].astype(BF16)

    return pl.pallas_call(
        body, name=name, grid=grid, in_specs=[pl.BlockSpec(blk, imap), ANY], out_specs=pl.BlockSpec(blk, omap),
        out_shape=jax.ShapeDtypeStruct(big.full_shape, BF16), compiler_params=_cp(len(grid)),
    )(src, after)


def _gather_ici_remote(bigs, off):
    remote = []
    for a, b in enumerate(bigs):
        for mask in CHIP_MASKS:
            def mine(io, p, a=a, b=b):
                return b.piece(io[off + a], _chip_index(p), p[2])
            remote.append((mask, mine, mine))
    return remote


def _gather_d2d_remote(bigs, off):
    remote = []
    for a, b in enumerate(bigs):
        for mask in CHIP_MASKS:
            def region(io, p, a=a, b=b, mask=mask):
                return b.piece(io[off + a], _chip_index(_flip(p, mask)), p[2])
            remote.append((1, region, region))
    return remote


def _ew(name, fn, ins, out_dtypes, rows_per_step=256):
    shape = ins[0].shape
    last = shape[-1]
    rows = 1
    for s in shape[:-1]:
        rows *= s
    ins2 = [a.reshape(rows, last) for a in ins]
    tr = _pick(rows, rows_per_step, 128, 64, 32, 16, 8)
    spec = pl.BlockSpec((tr, last), lambda i: (i, 0))

    def body(*refs):
        outs = fn(*[r[...] for r in refs[:len(ins)]])
        for o_ref, o in zip(refs[len(ins):], outs):
            o_ref[...] = o.astype(o_ref.dtype)

    outs = pl.pallas_call(
        body, name=name, grid=(rows // tr,), in_specs=[spec] * len(ins), out_specs=tuple([spec] * len(out_dtypes)),
        out_shape=tuple(jax.ShapeDtypeStruct((rows, last), d) for d in out_dtypes), compiler_params=_cp(1),
    )(*ins2)
    return [o.reshape(shape) for o in outs]


def _chip_sum(name, big, grad, from_sibling):
    if big.kind == "col":
        rh, w = big.half_shape
        tr = _pick(rh, 256, 128)
        nb = rh // tr
        grid, blk = (nb,), (tr, w)
        gmap, hmap = (lambda i: (_my_c() * nb + i, 0)), (lambda i: (i, 0))
    elif big.kind == "row":
        rh, w = big.half_shape[2:]
        tr = _pick(rh, 256, 128)
        grid, blk = (4, rh // tr), (1, 1, tr, w)
        gmap, hmap = (lambda k, i: (k, _my_c(), i, 0)), (lambda k, i: (k, 0, i, 0))
    else:
        grid, blk = (1,), big.half_shape
        gmap, hmap = (lambda i: (_my_c(), 0, 0)), (lambda i: (0, 0, 0))

    def body(g_ref, s_ref, o_ref):
        o_ref[...] = (g_ref[...].astype(F32) + s_ref[...].astype(F32)).astype(BF16)

    return pl.pallas_call(
        body, name=name, grid=grid, in_specs=[pl.BlockSpec(blk, gmap), pl.BlockSpec(blk, hmap)],
        out_specs=pl.BlockSpec(blk, hmap), out_shape=jax.ShapeDtypeStruct(big.half_shape, BF16), compiler_params=_cp(len(grid)),
    )(grad, from_sibling)


def _piece_sum(name, big, chip_sum, thirds):
    if big.kind == "col":
        rp, cs = big.piece_shape
        tr = _pick(rp, 256, 128)
        nb = rp // tr
        grid, blk, tblk = (nb,), (tr, cs), (1, tr, cs)
        smap, omap = (lambda i: (i, _my_chip())), (lambda i: (_my_c() * nb + i, 0))
        tmap = lambda j: (lambda i: (j, i, 0))
        out_shape = big.shard_shape
    elif big.kind == "row":
        rp, w = big.piece_shape[2:]
        tr = _pick(rp, 256, 128)
        grid, blk, tblk = (rp // tr,), (1, 1, tr, w), (1, 1, 1, tr, w)
        smap, omap = (lambda i: (_my_chip(), 0, i, 0)), (lambda i: (0, _my_c(), i, 0))
        tmap = lambda j: (lambda i: (j, 0, 0, i, 0))
        out_shape = (1, 2, rp, w)
    else:
        grid, blk, tblk = (1,), big.piece_shape, (1,) + big.piece_shape
        smap, omap = (lambda i: (0, _my_chip(), 0)), (lambda i: (_my_c(), 0, 0))
        tmap = lambda j: (lambda i: (j, 0, 0, 0))
        out_shape = big.shard_shape

    def body(s_ref, t0, t1, t2, o_ref):
        o_ref[...] = s_ref[...].astype(F32) + t0[0].astype(F32) + t1[0].astype(F32) + t2[0].astype(F32)

    return pl.pallas_call(
        body, name=name, grid=grid,
        in_specs=[pl.BlockSpec(blk, smap)] + [pl.BlockSpec(tblk, tmap(j)) for j in range(3)],
        out_specs=pl.BlockSpec(blk, omap), out_shape=jax.ShapeDtypeStruct(out_shape, F32), compiler_params=_cp(len(grid)),
    )(chip_sum, thirds, thirds, thirds)


def _split(name, bufs, remote):
    return _exchange_start(name + "_start", bufs, remote) + (remote, name)


def _join(handle, after):
    bufs, send_sems, recv_sems, _, remote, name = handle
    return _exchange_wait(name + "_wait", bufs, send_sems, recv_sems, remote, after)


def _reduce_d2d_remote(bigs):
    n = len(bigs)
    return [(1, lambda io, p, a=a, b=b: b.half_of_full(io[a], 1 - p[2]), lambda io, p, a=a: io[n + a])
            for a, b in enumerate(bigs)]


def _halves(bigs):
    return [jax.ShapeDtypeStruct(b.half_shape, BF16) for b in bigs]


def _chip_sums(tag, bigs, grads, from_sibling):
    return [_chip_sum(f"reduce_{tag}_chip_sum_{a}", b, g, r) for a, (b, g, r) in enumerate(zip(bigs, grads, from_sibling))]


def _reduce_to_chip(tag, bigs, grads):
    from_sibling = _exchange(f"reduce_{tag}_d2d", grads, _halves(bigs), _reduce_d2d_remote(bigs))
    return _chip_sums(tag, bigs, grads, from_sibling)


def _reduce_ici_remote(bigs):
    n = len(bigs)
    remote = []
    for a, b in enumerate(bigs):
        for j, mask in enumerate(CHIP_MASKS):
            remote.append((mask,
                           lambda io, p, a=a, b=b, mask=mask: b.piece_of_half(io[a], _chip_index(_flip(p, mask))),
                           lambda io, p, a=a, j=j: io[n + a].at[j]))
    return remote


def _thirds(bigs):
    return [jax.ShapeDtypeStruct((3,) + b.piece_shape, BF16) for b in bigs]


def _piece_sums(tag, bigs, chip_sum, from_chips):
    return [_piece_sum(f"reduce_{tag}_sum_{a}", b, s, r) for a, (b, s, r) in enumerate(zip(bigs, chip_sum, from_chips))]


def _share_remote(bigs, off):
    remote = []
    for a, b in enumerate(bigs):
        def mine(io, p, a=a, b=b):
            return b.half_of_shard(io[off + a], p[2])
        remote.append((1, mine, mine))
    return remote


def _reduce_finish(tag, bigs, chip_sum, from_chips):
    n = len(bigs)
    placed = _piece_sums(tag, bigs, chip_sum, from_chips)
    out = _exchange(f"reduce_{tag}_share_d2d", placed, [jax.ShapeDtypeStruct(p.shape, F32) for p in placed],
                    _share_remote(bigs, n), aliases={a: a for a in range(n)})
    return [o.reshape(b.shard_shape) for o, b in zip(out, bigs)]


def _mm(name, a, b, *, nt, tm, tn, tk, epi, extras=(), extra_specs=(), out_shape, out_specs, after=None, vmem_mib=48):
    m, kdim = a.shape
    n = b.shape[0] if nt else b.shape[1]
    gm, gn, gk = m // tm, n // tn, kdim // tk
    a_spec = pl.BlockSpec((tm, tk), lambda j, i, k: (i, k))
    b_spec = pl.BlockSpec((tn, tk), lambda j, i, k: (j, k)) if nt else pl.BlockSpec((tk, tn), lambda j, i, k: (k, j))
    n_ex = len(extras)
    if after is not None:
        extras, extra_specs = tuple(extras) + (after,), list(extra_specs) + [ANY]

    def body(a_ref, b_ref, *rest):
        ex, outs, acc = rest[:n_ex], rest[len(extras):-1], rest[-1]
        dot = _dot_nt if nt else _dot
        if gk == 1:
            acc[...] = dot(a_ref[...], b_ref[...])
            epi(acc, ex, outs)
        else:
            k = pl.program_id(2)

            @pl.when(k == 0)
            def _():
                acc[...] = jnp.zeros_like(acc)

            acc[...] += dot(a_ref[...], b_ref[...])

            @pl.when(k == gk - 1)
            def _():
                epi(acc, ex, outs)

    return pl.pallas_call(
        body, name=name, grid=(gn, gm, gk), in_specs=[a_spec, b_spec, *extra_specs], out_specs=tuple(out_specs),
        out_shape=tuple(out_shape), scratch_shapes=[pltpu.VMEM((tm, tn), F32)], compiler_params=_cp(3, vmem_mib),
    )(a, b, *extras)


def _mm_tn(name, a, b, out_dtype, *, tmo, tn, tt, more=(), vmem_mib=56):
    t, m = a.shape
    n = b.shape[1]
    gt = t // tt

    def body(a_ref, b_ref, *rest):
        o_ref, acc = rest[-2:]
        k = pl.program_id(2)

        @pl.when(k == 0)
        def _():
            acc[...] = _dot_tn(rest[0][...], rest[1][...]) if more else jnp.zeros_like(acc)

        acc[...] += _dot_tn(a_ref[...], b_ref[...])

        @pl.when(k == gt - 1)
        def _():
            o_ref[...] = acc[...].astype(o_ref.dtype)

    more_specs = [pl.BlockSpec((more[0].shape[0], tmo), lambda i, j, k: (0, i)),
                  pl.BlockSpec((more[1].shape[0], tn), lambda i, j, k: (0, j))] if more else []
    return pl.pallas_call(
        body, name=name, grid=(m // tmo, n // tn, gt),
        in_specs=[pl.BlockSpec((tt, tmo), lambda i, j, k: (k, i)), pl.BlockSpec((tt, tn), lambda i, j, k: (k, j))] + more_specs,
        out_specs=pl.BlockSpec((tmo, tn), lambda i, j, k: (i, j)), out_shape=jax.ShapeDtypeStruct((m, n), out_dtype),
        scratch_shapes=[pltpu.VMEM((tmo, tn), F32)], compiler_params=_cp(3, vmem_mib),
    )(a, b, *more)


def _row_spec(d):
    return pl.BlockSpec((1, d), lambda *_: (0, 0))


def _stat_spec(k, d):
    return pl.BlockSpec((k, 8, d), lambda *_: (0, 0, 0))


def _rope(z, cs, sn):
    first = (lax.broadcasted_iota(jnp.int32, (z.shape[0], 128), 1) % 32) < 16
    outs = []
    for j in range(z.shape[1] // 128):
        zc = z[:, 128 * j:128 * (j + 1)]
        partner = jnp.where(first, pltpu.roll(zc, 112, 1), pltpu.roll(zc, 16, 1))
        outs.append(zc * cs + partner * sn)
    return outs[0] if len(outs) == 1 else jnp.concatenate(outs, axis=1)


def _rope_tables(length, rotate):
    if not rotate:
        return jnp.ones((length, 128), F32), jnp.zeros((length, 128), F32)
    half = HEAD_DIM // 2
    inv_freq = ROPE_BASE ** (-jnp.arange(0, half, 2, dtype=F32) / half)
    t = jnp.arange(length)
    row = (t // GRID_W).astype(F32)
    col = (t % GRID_W).astype(F32)
    e = jnp.arange(128) % HEAD_DIM
    pos = jnp.where(e[None, :] < half, row[:, None], col[:, None])
    ang = pos * inv_freq[(e % half) % (half // 2)][None, :]
    first = ((e % half) < half // 2)[None, :]
    return jnp.cos(ang), jnp.where(first, -jnp.sin(ang), jnp.sin(ang))


def _mixer_in(name, x, nw, sh, sc, w_in, cos, sin, after):
    t, d = x.shape
    tm = _pick(t, 256, 128)
    n_in = w_in.shape[1]

    def body(x_ref, nw_ref, sh_ref, sc_ref, w_ref, cos_ref, sin_ref, after_ref, h_ref, q_ref, k_ref, v_ref, u_ref):
        xf = x_ref[...]
        r = lax.rsqrt(jnp.mean(xf * xf, axis=-1, keepdims=True) + EPS)
        hb = (((xf * r) * nw_ref[...]) * (1.0 + sc_ref[...]) + sh_ref[...]).astype(BF16)
        h_ref[...] = hb
        p = _dot(hb, w_ref[...])
        cs, sn = cos_ref[...], sin_ref[...]
        q_ref[...] = (_rope(p[:, :ATTN_WIDTH], cs, sn) * SCALE).astype(BF16)
        k_ref[...] = _rope(p[:, ATTN_WIDTH:ATTN_WIDTH + KV_WIDTH], cs, sn).astype(BF16)
        v_ref[...] = p[:, ATTN_WIDTH + KV_WIDTH:ATTN_WIDTH + 2 * KV_WIDTH].astype(BF16)
        u_ref[...] = p[:, ATTN_WIDTH + 2 * KV_WIDTH:]

    def tile(w):
        return pl.BlockSpec((tm, w), lambda i: (i, 0))

    return pl.pallas_call(
        body, name=name, grid=(t // tm,),
        in_specs=[tile(d), _row_spec(d), _row_spec(d), _row_spec(d), pl.BlockSpec((d, n_in), lambda i: (0, 0)),
                  tile(128), tile(128), ANY],
        out_specs=(tile(d), tile(ATTN_WIDTH), tile(KV_WIDTH), tile(KV_WIDTH), tile(POOL_WIDTH)),
        out_shape=(jax.ShapeDtypeStruct((t, d), BF16), jax.ShapeDtypeStruct((t, ATTN_WIDTH), BF16),
                   jax.ShapeDtypeStruct((t, KV_WIDTH), BF16), jax.ShapeDtypeStruct((t, KV_WIDTH), BF16),
                   jax.ShapeDtypeStruct((t, POOL_WIDTH), F32)),
        compiler_params=_cp(1),
    )(x, nw, sh, sc, w_in, cos, sin, after)


def _attn_specs(nb, n_ctx):
    def blk(w, f):
        return pl.BlockSpec((BLOCK, w), lambda n: (f(n), 0))

    prev = lambda n: jnp.maximum(n - 1, 0)
    cur = lambda n: n
    nxt = lambda n: jnp.minimum(n + 1, nb - 1)
    kv = [blk(KV_WIDTH, prev), blk(KV_WIDTH, cur), blk(KV_WIDTH, nxt)]
    ctx = pl.BlockSpec((n_ctx, KV_WIDTH), lambda n: (0, 0))
    return [pl.BlockSpec(memory_space=pltpu.SMEM), blk(ATTN_WIDTH, cur)] + kv + kv + [ctx, ctx]


def _attn_mask(n, length, n_keys):
    row = lax.broadcasted_iota(jnp.int32, (GROUP * BLOCK, n_keys), 0) % BLOCK
    col = lax.broadcasted_iota(jnp.int32, (GROUP * BLOCK, n_keys), 1)
    kpos = (n - 1) * BLOCK + col
    return ((jnp.abs(col - BLOCK - row) <= BLOCK) & (kpos >= 0) & (kpos < length)) | (col >= 3 * BLOCK)


def _group_rows(block, g):
    return jnp.concatenate([block[:, HEAD_DIM * h:HEAD_DIM * (h + 1)] for h in range(GROUP * g, GROUP * (g + 1))], axis=0)


def _group_sink(sink_ref, g):
    head = lax.broadcasted_iota(jnp.int32, (GROUP * BLOCK, 1), 0) // BLOCK
    out = jnp.full((GROUP * BLOCK, 1), sink_ref[0, GROUP * g], F32)
    for j in range(1, GROUP):
        out = jnp.where(head == j, sink_ref[0, GROUP * g + j], out)
    return out


def _attn_fwd(q, k, v, kc, vc, sink):
    length = q.shape[0]
    nb = length // BLOCK
    n_ctx = kc.shape[0]
    n_keys = 3 * BLOCK + n_ctx

    def body(sink_ref, q_ref, kp, k0, kn, vp, v0, vn, kc_ref, vc_ref, o_ref):
        n = pl.program_id(0)
        valid = _attn_mask(n, length, n_keys)
        qb = q_ref[...]
        kall = jnp.concatenate([kp[...], k0[...], kn[...], kc_ref[...]], axis=0)
        vall = jnp.concatenate([vp[...], v0[...], vn[...], vc_ref[...]], axis=0)
        outs = []
        for g in range(N_KV_HEADS):
            lanes = slice(HEAD_DIM * g, HEAD_DIM * (g + 1))
            s = jnp.where(valid, _dot_nt(_group_rows(qb, g), kall[:, lanes]), NEG_INF)
            sk = _group_sink(sink_ref, g)
            m = jnp.maximum(jnp.max(s, axis=-1, keepdims=True), sk)
            e = jnp.exp(s - m)
            den = jnp.sum(e, axis=-1, keepdims=True) + jnp.exp(sk - m)
            o = _dot(e.astype(BF16), vall[:, lanes]) / den
            outs += [o[BLOCK * j:BLOCK * (j + 1)] for j in range(GROUP)]
        o_ref[...] = jnp.concatenate(outs, axis=1).astype(BF16)

    return pl.pallas_call(
        body, name="attn_fwd", grid=(nb,), in_specs=_attn_specs(nb, n_ctx),
        out_specs=pl.BlockSpec((BLOCK, ATTN_WIDTH), lambda n: (n, 0)),
        out_shape=jax.ShapeDtypeStruct((length, ATTN_WIDTH + POOL_WIDTH), BF16), compiler_params=_cp(1),
    )(sink, q, k, k, k, v, v, v, kc, vc)


def _attn_bwd(q, k, v, kc, vc, sink, dmix):
    length = q.shape[0]
    nb = length // BLOCK
    n_ctx = kc.shape[0]
    n_keys = 3 * BLOCK + n_ctx

    def body(sink_ref, q_ref, kp, k0, kn, vp, v0, vn, kc_ref, vc_ref, do_ref,
             dq_ref, dkp_ref, dvp_ref, dkc_ref, dvc_ref, dsink_ref):
        n = pl.program_id(0)

        @pl.when(n == 0)
        def _():
            dkc_ref[...] = jnp.zeros_like(dkc_ref)
            dvc_ref[...] = jnp.zeros_like(dvc_ref)
            dsink_ref[...] = jnp.zeros_like(dsink_ref)

        valid = _attn_mask(n, length, n_keys)
        qb, dob = q_ref[...], do_ref[...]
        kall = jnp.concatenate([kp[...], k0[...], kn[...], kc_ref[...]], axis=0)
        vall = jnp.concatenate([vp[...], v0[...], vn[...], vc_ref[...]], axis=0)
        srow = lax.broadcasted_iota(jnp.int32, (8, 128), 0)
        slane = lax.broadcasted_iota(jnp.int32, (8, 128), 1)
        dqs, dks, dvs = [], [], []
        dsink = jnp.zeros((8, 128), F32)
        for g in range(N_KV_HEADS):
            lanes = slice(HEAD_DIM * g, HEAD_DIM * (g + 1))
            kg, vg = kall[:, lanes], vall[:, lanes]
            qg, dog = _group_rows(qb, g), _group_rows(dob, g)
            s = jnp.where(valid, _dot_nt(qg, kg), NEG_INF)
            sk = _group_sink(sink_ref, g)
            m = jnp.maximum(jnp.max(s, axis=-1, keepdims=True), sk)
            e = jnp.exp(s - m)
            inv = 1.0 / (jnp.sum(e, axis=-1, keepdims=True) + jnp.exp(sk - m))
            p = e * inv
            dp = _dot_nt(dog, vg)
            delta = jnp.sum(p * dp, axis=-1, keepdims=True)
            ds = (p * (dp - delta)).astype(BF16)
            dq = _dot(ds, kg) * SCALE
            dqs += [dq[BLOCK * j:BLOCK * (j + 1)] for j in range(GROUP)]
            dks.append(_dot_tn(ds, qg))
            dvs.append(_dot_tn(p.astype(BF16), dog))
            d_sink = jnp.exp(sk - m) * inv * delta
            for j in range(GROUP):
                total = -jnp.sum(d_sink[BLOCK * j:BLOCK * (j + 1)], axis=0, keepdims=True)
                dsink = dsink + jnp.where((srow == 0) & (slane == GROUP * g + j), total, 0.0)
        dq_ref[...] = jnp.concatenate(dqs, axis=1)
        dk = jnp.concatenate(dks, axis=1)
        dv = jnp.concatenate(dvs, axis=1)
        for j in range(3):
            dkp_ref[0, j] = dk[BLOCK * j:BLOCK * (j + 1)]
            dvp_ref[0, j] = dv[BLOCK * j:BLOCK * (j + 1)]
        dkc_ref[...] += dk[3 * BLOCK:]
        dvc_ref[...] += dv[3 * BLOCK:]
        dsink_ref[...] += dsink

    part = pl.BlockSpec((1, 3, BLOCK, KV_WIDTH), lambda n: (n, 0, 0, 0))
    ctx = pl.BlockSpec((n_ctx, KV_WIDTH), lambda n: (0, 0))
    return pl.pallas_call(
        body, name="attn_bwd", grid=(nb,),
        in_specs=_attn_specs(nb, n_ctx) + [pl.BlockSpec((BLOCK, ATTN_WIDTH), lambda n: (n, 0))],
        out_specs=(pl.BlockSpec((BLOCK, ATTN_WIDTH), lambda n: (n, 0)), part, part, ctx, ctx,
                   pl.BlockSpec((8, 128), lambda n: (0, 0))),
        out_shape=(jax.ShapeDtypeStruct((length, ATTN_WIDTH), F32),
                   jax.ShapeDtypeStruct((nb, 3, BLOCK, KV_WIDTH), F32), jax.ShapeDtypeStruct((nb, 3, BLOCK, KV_WIDTH), F32),
                   jax.ShapeDtypeStruct((n_ctx, KV_WIDTH), F32), jax.ShapeDtypeStruct((n_ctx, KV_WIDTH), F32),
                   jax.ShapeDtypeStruct((8, 128), F32)),
        compiler_params=_cp(1),
    )(sink, q, k, k, k, v, v, v, kc, vc, dmix)


def _assemble_dp(dq, dkp, dvp, du, cos, sin, after):
    length = dq.shape[0]
    nb = length // BLOCK

    def body(dq_ref, dka, dkb, dkc, dva, dvb, dvc, du_ref, cos_ref, sin_ref, after_ref, o_ref):
        n = pl.program_id(0)
        has_next = (n + 1 < nb).astype(F32)
        has_prev = (n > 0).astype(F32)
        cs, sn = cos_ref[...], -sin_ref[...]
        dk = dka[0, 0] * has_next + dkb[0, 0] + dkc[0, 0] * has_prev
        dv = dva[0, 0] * has_next + dvb[0, 0] + dvc[0, 0] * has_prev
        o_ref[:, :ATTN_WIDTH] = _rope(dq_ref[...], cs, sn).astype(BF16)
        o_ref[:, ATTN_WIDTH:ATTN_WIDTH + KV_WIDTH] = _rope(dk, cs, sn).astype(BF16)
        o_ref[:, ATTN_WIDTH + KV_WIDTH:ATTN_WIDTH + 2 * KV_WIDTH] = dv.astype(BF16)
        o_ref[:, ATTN_WIDTH + 2 * KV_WIDTH:] = du_ref[...]

    def part(slot, f):
        return pl.BlockSpec((1, 1, BLOCK, KV_WIDTH), lambda n: (f(n), slot, 0, 0))

    parts = [part(0, lambda n: jnp.minimum(n + 1, nb - 1)), part(1, lambda n: n), part(2, lambda n: jnp.maximum(n - 1, 0))]

    def tile(w):
        return pl.BlockSpec((BLOCK, w), lambda n: (n, 0))

    width = ATTN_WIDTH + 2 * KV_WIDTH + POOL_WIDTH
    return pl.pallas_call(
        body, name="assemble_dp", grid=(nb,),
        in_specs=[tile(ATTN_WIDTH)] + parts + parts + [tile(POOL_WIDTH), tile(128), tile(128), ANY],
        out_specs=tile(width), out_shape=jax.ShapeDtypeStruct((length, width), BF16), compiler_params=_cp(1),
    )(dq, dkp, dkp, dkp, dvp, dvp, dvp, du, cos, sin, after)


def _shift_rows(e, s):
    n = e.shape[0]
    return e if s % n == 0 else pltpu.roll(e, (-s) % n, 0)


def _window_sum(e, w, first):
    s, n = e, 1
    while n < w:
        s = s + _shift_rows(s, n)
        n *= 2
    return _shift_rows(s, first)


def _pool_geometry(i, tm, length):
    pos = i * tm - HALO + lax.broadcasted_iota(jnp.int32, (tm + 2 * HALO, 1), 0)
    inside = (pos >= 0) & (pos < length)
    inv_counts = []
    for w in POOL_WINDOWS:
        lo = jnp.clip(pos - w // 2, 0, length)
        hi = jnp.clip(pos - w // 2 + w, 0, length)
        inv_counts.append(1.0 / jnp.maximum(hi - lo, 1).astype(F32))
    return inside, inv_counts


def _halo_specs(tm, width, length, col=0):
    per = tm // HALO
    last = length // HALO - 1
    return [pl.BlockSpec((HALO, width), lambda i: (jnp.maximum(i * per - 1, 0), col)),
            pl.BlockSpec((tm, width), lambda i: (i, col)),
            pl.BlockSpec((HALO, width), lambda i: (jnp.minimum((i + 1) * per, last), col))]


def _pooled(ext, inv_counts, tm):
    outs = []
    for g, w in enumerate(POOL_WINDOWS):
        e = ext[:, POOL_GROUP_DIM * g:POOL_GROUP_DIM * (g + 1)]
        mean = _window_sum(e, w, -(w // 2)) * inv_counts[g]
        outs.append((mean - e)[HALO:HALO + tm])
    return outs


def _pool_fwd(u, pool_w, pool_scale, mix):
    length = u.shape[0]
    tm = _pick(length, 256, 128)

    def body(up, u0, un, w_ref, sc_ref, mix_ref, o_ref):
        inside, inv_counts = _pool_geometry(pl.program_id(0), tm, length)
        ext = jnp.where(inside, jnp.concatenate([up[...], u0[...], un[...]], axis=0), 0.0)
        pooled = _pooled(ext, inv_counts, tm)
        mixed = [_dot(pooled[g].astype(BF16), w_ref[g]) for g in range(len(POOL_WINDOWS))]
        o_ref[...] = (jnp.concatenate(mixed, axis=1) * sc_ref[...]).astype(BF16)

    return pl.pallas_call(
        body, name="pool_fwd", grid=(length // tm,),
        in_specs=_halo_specs(tm, POOL_WIDTH, length) + [pl.BlockSpec(pool_w.shape, lambda i: (0, 0, 0)), _row_spec(POOL_WIDTH), ANY],
        out_specs=pl.BlockSpec((tm, POOL_WIDTH), lambda i: (i, 1)),
        out_shape=jax.ShapeDtypeStruct(mix.shape, BF16), input_output_aliases={5: 0}, compiler_params=_cp(1),
    )(u, u, u, pool_w, pool_scale, mix)


def _pool_bwd(u, dmix, pool_w, pool_scale, after):
    length = u.shape[0]
    tm = _pick(length, 256, 128)
    n_g = len(POOL_WINDOWS)

    def body(up, u0, un, dp_, d0, dn_, w_ref, sc_ref, after_ref, du_ref, dw_ref, dsc_ref):
        i = pl.program_id(0)

        @pl.when(i == 0)
        def _():
            dw_ref[...] = jnp.zeros_like(dw_ref)
            dsc_ref[...] = jnp.zeros_like(dsc_ref)

        inside, inv_counts = _pool_geometry(i, tm, length)
        ext = jnp.where(inside, jnp.concatenate([up[...], u0[...], un[...]], axis=0), 0.0)
        dext = jnp.where(inside, jnp.concatenate([dp_[...], d0[...], dn_[...]], axis=0).astype(F32), 0.0)
        dmixed = (dext * sc_ref[...]).astype(BF16)
        pooled = _pooled(ext, inv_counts, tm)
        dus, dscs = [], []
        for g, w in enumerate(POOL_WINDOWS):
            lanes = slice(POOL_GROUP_DIM * g, POOL_GROUP_DIM * (g + 1))
            dpooled = _dot_nt(dmixed[:, lanes], w_ref[g])
            spread = _window_sum(dpooled * inv_counts[g], w, -(w // 2 - 1))
            dus.append((spread - dpooled)[HALO:HALO + tm])
            pb = pooled[g].astype(BF16)
            dw_ref[g] += _dot_tn(pb, dmixed[HALO:HALO + tm, lanes])
            prod = dext[HALO:HALO + tm, lanes] * _dot(pb, w_ref[g])
            dscs.append(_fold8(prod))
        du_ref[...] = jnp.concatenate(dus, axis=1).astype(BF16)
        dsc_ref[...] += jnp.concatenate(dscs, axis=1)

    return pl.pallas_call(
        body, name="pool_bwd", grid=(length // tm,),
        in_specs=_halo_specs(tm, POOL_WIDTH, length) + _halo_specs(tm, POOL_WIDTH, length, col=1)
        + [pl.BlockSpec(pool_w.shape, lambda i: (0, 0, 0)), _row_spec(POOL_WIDTH), ANY],
        out_specs=(pl.BlockSpec((tm, POOL_WIDTH), lambda i: (i, 0)), pl.BlockSpec((n_g, POOL_GROUP_DIM, POOL_GROUP_DIM), lambda i: (0, 0, 0)),
                   pl.BlockSpec((8, POOL_WIDTH), lambda i: (0, 0))),
        out_shape=(jax.ShapeDtypeStruct((length, POOL_WIDTH), BF16), jax.ShapeDtypeStruct((n_g, POOL_GROUP_DIM, POOL_GROUP_DIM), F32),
                   jax.ShapeDtypeStruct((8, POOL_WIDTH), F32)),
        compiler_params=_cp(1),
    )(u, u, u, dmix, dmix, dmix, pool_w, pool_scale, after)


def _mixer_out(mix, w_out, x, g_a, nmw, sh_m, sc_m, after):
    t, d = x.shape
    tm = _pick(t, 256, 128)

    def epi(acc, ex, outs):
        x_ref, ga, nw, sh, sc = ex
        x1_ref, mo_ref, hm_ref = outs

        def rows(rs):
            mo = acc[rs, :]
            x1 = x_ref[rs, :] + ga[...] * mo
            x1_ref[rs, :] = x1
            mo_ref[rs, :] = mo.astype(BF16)
            r = lax.rsqrt(jnp.mean(x1 * x1, axis=-1, keepdims=True) + EPS)
            hm_ref[rs, :] = (((x1 * r) * nw[...]) * (1.0 + sc[...]) + sh[...]).astype(BF16)

        _row_loop(tm, rows)

    tile = pl.BlockSpec((tm, d), lambda j, i, k: (i, 0))
    return _mm("mixer_out", mix, w_out, nt=False, tm=tm, tn=d, tk=mix.shape[1], epi=epi, after=after,
               extras=(x, g_a, nmw, sh_m, sc_m), extra_specs=[tile] + [_row_spec(d)] * 4,
               out_shape=(jax.ShapeDtypeStruct((t, d), F32), jax.ShapeDtypeStruct((t, d), BF16), jax.ShapeDtypeStruct((t, d), BF16)),
               out_specs=(tile, tile, tile))


def _mlp_up(hm, w_up):
    t, d = hm.shape
    tm = _pick(t, 512, 256, 128)
    tn = 2048

    def epi(acc, ex, outs):
        outs[0][...] = jnp.square(jnp.maximum(acc[...], 0.0)).astype(BF16)

    return _mm("mlp_up", hm, w_up, nt=False, tm=tm, tn=tn, tk=d, epi=epi,
               out_shape=(jax.ShapeDtypeStruct((t, w_up.shape[1]), BF16),),
               out_specs=(pl.BlockSpec((tm, tn), lambda j, i, k: (i, j)),))[0]


def _mm_f32(name, a, b, *, nt, after):
    m, kdim = a.shape
    n = b.shape[0] if nt else b.shape[1]
    tm, tn = _pick(m, 1024, 512, 256, 128), _pick(n, 1024)

    def epi(acc, ex, outs):
        outs[0][...] = acc[...]

    return _mm(name, a, b, nt=nt, tm=tm, tn=tn, tk=_pick(kdim, 2048), epi=epi, after=after,
               out_shape=(jax.ShapeDtypeStruct((m, n), F32),), out_specs=(pl.BlockSpec((tm, tn), lambda j, i, k: (i, j)),))[0]


def _rows_call(name, rows_fn, tiles, vecs, out_shape, n_stats, after):
    t, d = tiles[0].shape
    tm = _pick(t, 256, 128)
    n_t, n_v = len(tiles), len(vecs)

    def body(*refs):
        st_ref = refs[-1]

        @pl.when(pl.program_id(0) == 0)
        def _():
            st_ref[...] = jnp.zeros_like(st_ref)

        _row_loop(tm, lambda rs: rows_fn(rs, refs[:n_t], refs[n_t:n_t + n_v], refs[n_t + n_v + 1:-1], st_ref))

    tile = pl.BlockSpec((tm, d), lambda i: (i, 0))
    return pl.pallas_call(
        body, name=name, grid=(t // tm,), in_specs=[tile] * n_t + [_row_spec(d)] * n_v + [ANY],
        out_specs=tuple([tile] * len(out_shape)) + (_stat_spec(n_stats, d),),
        out_shape=tuple(out_shape) + (jax.ShapeDtypeStruct((n_stats, 8, d), F32),), compiler_params=_cp(1),
    )(*tiles, *vecs, after)


def _loss_rows(dn, x1, target, g_m, fw):
    t, d = x1.shape

    def rows_fn(rs, tiles, vecs, outs, st_ref):
        dn_ref, x1_ref, t_ref = tiles
        gm, fw_ref = vecs
        dx2_ref, ddn_ref = outs
        dnv = dn_ref[rs, :]
        x2 = x1_ref[rs, :] + gm[...] * dnv
        r = lax.rsqrt(jnp.mean(x2 * x2, axis=-1, keepdims=True) + EPS)
        xh = x2 * r
        diff = xh * fw_ref[...] - t_ref[rs, :]
        dy = diff * (1.0 / d)
        dxh = dy * fw_ref[...]
        dx2 = r * (dxh - xh * jnp.mean(dxh * xh, axis=-1, keepdims=True))
        dx2_ref[rs, :] = dx2
        ddn_ref[rs, :] = (dx2 * gm[...]).astype(BF16)
        st_ref[0] += _fold8(diff * diff)
        st_ref[1] += _fold8(dy * xh)
        st_ref[2] += _fold8(dx2 * dnv)

    return _rows_call("loss_rows", rows_fn, (dn, x1, target), (g_m, fw),
                      (jax.ShapeDtypeStruct((t, d), F32), jax.ShapeDtypeStruct((t, d), BF16)), 3, g_m)


def _mlp_dx_rows(dhm, x1, dx2, mo, nmw, sc_m, g_a, after):
    t, d = x1.shape

    def rows_fn(rs, tiles, vecs, outs, st_ref):
        dh_ref, x1_ref, dx2_ref, mo_ref = tiles
        nw, sc, ga = vecs
        dx1_ref, dmi_ref = outs
        dx1 = _norm_bwd_rows(dh_ref[rs, :], x1_ref[rs, :], nw[...], sc[...], st_ref) + dx2_ref[rs, :]
        dx1_ref[rs, :] = dx1
        dmi_ref[rs, :] = (dx1 * ga[...]).astype(BF16)
        st_ref[3] += _fold8(dx1 * mo_ref[rs, :].astype(F32))

    return _rows_call("mlp_dx_rows", rows_fn, (dhm, x1, dx2, mo), (nmw, sc_m, g_a),
                      (jax.ShapeDtypeStruct((t, d), F32), jax.ShapeDtypeStruct((t, d), BF16)), 4, after)


def _mlp_dact(ddn, w_down, act):
    t, d = ddn.shape
    tm = _pick(t, 512, 256, 128)
    tn = 2048

    def epi(acc, ex, outs):
        outs[0][...] = (acc[...] * (2.0 * jnp.sqrt(ex[0][...]).astype(F32))).astype(BF16)

    tile = pl.BlockSpec((tm, tn), lambda j, i, k: (i, j))
    return _mm("mlp_dact", ddn, w_down, nt=True, tm=tm, tn=tn, tk=d, epi=epi, extras=(act,), extra_specs=[tile],
               out_shape=(jax.ShapeDtypeStruct(act.shape, BF16),), out_specs=(tile,))[0]


def _norm_bwd_rows(dh, xv, nw, sc, st_ref):
    r = lax.rsqrt(jnp.mean(xv * xv, axis=-1, keepdims=True) + EPS)
    xh = xv * r
    dy = dh * (1.0 + sc)
    st_ref[0] += _fold8(dh)
    st_ref[1] += _fold8(dh * (xh * nw))
    st_ref[2] += _fold8(dy * xh)
    dxh = dy * nw
    return r * (dxh - xh * jnp.mean(dxh * xh, axis=-1, keepdims=True))


def _mixer_dmix(dmi, w_out):
    t, d = dmi.shape
    tm = _pick(t, 512, 256, 128)

    def epi(acc, ex, outs):
        outs[0][...] = acc[...].astype(BF16)

    n = w_out.shape[0]
    return _mm("mixer_dmix", dmi, w_out, nt=True, tm=tm, tn=n, tk=d, epi=epi,
               out_shape=(jax.ShapeDtypeStruct((t, n), BF16),), out_specs=(pl.BlockSpec((tm, n), lambda j, i, k: (i, 0)),))[0]


def _mixer_dx(name, dp, w_in, x, dx1, naw, sc_a, after):
    t, d = x.shape
    tm = _pick(t, 256, 128)

    def epi(acc, ex, outs):
        x_ref, dx1_ref, nw, sc = ex
        gx_ref, st_ref = outs

        @pl.when(pl.program_id(1) == 0)
        def _():
            st_ref[...] = jnp.zeros_like(st_ref)

        def rows(rs):
            gx_ref[rs, :] = _norm_bwd_rows(acc[rs, :], x_ref[rs, :], nw[...], sc[...], st_ref) + dx1_ref[rs, :]

        _row_loop(tm, rows)

    tile = pl.BlockSpec((tm, d), lambda j, i, k: (i, 0))
    return _mm(name, dp, w_in, nt=True, tm=tm, tn=d, tk=dp.shape[1], epi=epi, after=after,
               extras=(x, dx1, naw, sc_a), extra_specs=[tile, tile, _row_spec(d), _row_spec(d)],
               out_shape=(jax.ShapeDtypeStruct((t, d), F32), jax.ShapeDtypeStruct((3, 8, d), F32)),
               out_specs=(tile, _stat_spec(3, d)))


def _silu(v):
    return v / (1.0 + jnp.exp(-v))


def _ada_fwd(cond, w_ada, b_ada):
    d, n = w_ada.shape
    tn = 512

    def body(c_ref, w_ref, b_ref, o_ref):
        o_ref[...] = _dot(_silu(c_ref[...]).astype(BF16), w_ref[...].astype(BF16)) + b_ref[...]

    return pl.pallas_call(
        body, name="ada_fwd", grid=(n // tn,),
        in_specs=[pl.BlockSpec(cond.shape, lambda j: (0, 0)), pl.BlockSpec((d, tn), lambda j: (0, j)), pl.BlockSpec((1, tn), lambda j: (0, j))],
        out_specs=pl.BlockSpec((cond.shape[0], tn), lambda j: (0, j)), out_shape=jax.ShapeDtypeStruct((cond.shape[0], n), F32),
        compiler_params=_cp(1),
    )(cond, w_ada, b_ada)


def _adamw_math(w, g, m, v):
    m = ADAM_B1 * m + (1.0 - ADAM_B1) * g
    v = ADAM_B2 * v + (1.0 - ADAM_B2) * jnp.square(g)
    m_hat = m / (1.0 - ADAM_B1 ** ADAM_STEP)
    v_hat = v / (1.0 - ADAM_B2 ** ADAM_STEP)
    return -ADAM_LR * (m_hat / (jnp.sqrt(v_hat) + ADAM_EPS) + ADAM_WD * w), m, v


def _ada_bwd(cond, dm, w_ada, m_ada, v_ada):
    d, n = w_ada.shape
    tn = 256
    rows = cond.shape[0]

    def body(c_ref, dm_ref, w_ref, m_ref, v_ref, g_ref, dl_ref, nm_ref, nv_ref, pc_ref):
        @pl.when(pl.program_id(0) == 0)
        def _():
            pc_ref[...] = jnp.zeros_like(pc_ref)

        dmb = dm_ref[...].astype(BF16)
        w = w_ref[...]
        g = _dot_tn(_silu(c_ref[...]).astype(BF16), dmb)
        g_ref[...] = g
        dl_ref[...], nm_ref[...], nv_ref[...] = _adamw_math(w, g, m_ref[...], v_ref[...])
        pc_ref[...] += _dot_nt(dm_ref[8:16, :].astype(BF16), w.astype(BF16))

    tile = pl.BlockSpec((d, tn), lambda j: (0, j))
    like = jax.ShapeDtypeStruct((d, n), F32)
    return pl.pallas_call(
        body, name="ada_bwd", grid=(n // tn,),
        in_specs=[pl.BlockSpec((rows, d), lambda j: (0, 0)), pl.BlockSpec((rows, tn), lambda j: (0, j)), tile, tile, tile],
        out_specs=(tile, tile, tile, tile, pl.BlockSpec((8, d), lambda j: (0, 0))),
        out_shape=(like, like, like, like, jax.ShapeDtypeStruct((8, d), F32)), compiler_params=_cp(1),
    )(cond, dm, w_ada, m_ada, v_ada)


def _adamw(name, w, g, m, v):
    return _ew(name, _adamw_math, [w, g, m, v], [F32, F32, F32])


def _colsum(st):
    return jnp.sum(st, axis=1)


def kernel(x, c, ctx, c_ctx, norm_attn_w, norm_mlp_w, w_ada, b_ada, w_in, attn_sink, pool_w, pool_scale, w_out, w_mlp_up, w_mlp_down, final_norm_w, loss_target, m_c_ctx, m_norm_attn_w, m_norm_mlp_w, m_w_ada, m_b_ada, m_w_in, m_attn_sink, m_pool_w, m_pool_scale, m_w_out, m_w_mlp_up, m_w_mlp_down, m_final_norm_w, v_c_ctx, v_norm_attn_w, v_norm_mlp_w, v_w_ada, v_b_ada, v_w_in, v_attn_sink, v_pool_w, v_pool_scale, v_w_out, v_w_mlp_up, v_w_mlp_down, v_final_norm_w):
    length, d = x.shape[1], x.shape[2]
    n_ctx = ctx.shape[1]
    pos = (lax.axis_index("x"), lax.axis_index("y"), lax.axis_index("c"))
    me, chip = _dev_index(pos), _chip_index(pos)
    xs, tgt, cx = x.reshape(length, d), loss_target.reshape(length, d), ctx.reshape(n_ctx, d)
    n_ada = w_ada.shape[2]

    mixer_bigs = [_Big("col", w_in.shape[1:]), _Big("pool", pool_w.shape[1:]), _Big("row", w_out.shape[1:])]
    mlp_bigs = [_Big("col", w_mlp_up.shape[1:]), _Big("row", w_mlp_down.shape[1:])]
    placed = [_cast_place(f"place_{i}", b, s, c) for i, (b, s) in enumerate(zip(mixer_bigs, [w_in[0], pool_w[0], w_out[0]]))]
    flight = _split("gather_mixer_ici", placed, _gather_ici_remote(mixer_bigs, 0))

    c_all = _allgather8("gather_c", jnp.pad(c, ((0, 7), (0, 0))) + flight[3][0, 0])
    cond = jnp.concatenate([c_all[:, 0, :], jnp.pad(c_ctx[None, :], ((0, 7), (0, 0)))], axis=0)
    b_shard = lax.dynamic_slice_in_dim(b_ada, chip * n_ada, n_ada, axis=1)
    mod_all = _allgather8("gather_mod", _ada_fwd(cond, w_ada[0], b_shard))
    mod = jnp.concatenate([mod_all[0], mod_all[2], mod_all[4], mod_all[6]], axis=1)
    mine = lax.dynamic_slice_in_dim(mod, me, 1, axis=0)
    sh_a, sc_a, g_a, sh_m, sc_m, g_m = [mine[:, d * i:d * (i + 1)] for i in range(6)]
    csh_a, csc_a = mod[8:9, :d], mod[8:9, d:2 * d]

    win_b, pw_b, wout_b = _exchange("gather_mixer_d2d", _join(flight, mod), [jax.ShapeDtypeStruct(b.full_shape, BF16) for b in mixer_bigs],
                                    _gather_d2d_remote(mixer_bigs, 3), aliases={0: 0, 1: 1, 2: 2})
    wout_b = wout_b.reshape(-1, d)
    placed = [_cast_place(f"place_mlp_{i}", b, s, pw_b) for i, (b, s) in enumerate(zip(mlp_bigs, [w_mlp_up[0], w_mlp_down[0]]))]
    flight = _split("gather_mlp_ici", placed, _gather_ici_remote(mlp_bigs, 0))

    cos, sin = _rope_tables(length, True)
    one, zero = _rope_tables(n_ctx, False)
    h, q, k, v, u = _mixer_in("mixer_in", xs, norm_attn_w, sh_a, sc_a, win_b, cos, sin, flight[3])
    hc, _, kc, vc, _ = _mixer_in("mixer_in_ctx", cx, norm_attn_w, csh_a, csc_a, win_b, one, zero, flight[3])
    mix = _pool_fwd(u, pw_b, pool_scale, _attn_fwd(q, k, v, kc, vc, attn_sink))
    flight = _split("gather_mlp_d2d", _join(flight, mix), _gather_d2d_remote(mlp_bigs, 0))
    x1, mo, hm = _mixer_out(mix, wout_b, xs, g_a, norm_mlp_w, sh_m, sc_m, flight[3])
    wup_b, wdn_b = _join(flight, hm)
    wdn_b = wdn_b.reshape(-1, d)
    act = _mlp_up(hm, wup_b)
    dn = _mm_f32("mlp_down", act, wdn_b, nt=False, after=c)
    dx2, ddn, st_loss = _loss_rows(dn, x1, tgt, g_m, final_norm_w[None, :])
    st_loss = _colsum(st_loss)
    loss = lax.psum(0.5 / d * jnp.sum(st_loss[0]), ("x", "y", "c"))

    tt = _pick(length, 2048, 1024, 512, 256, 128)
    g_wdn = _mm_tn("grad_w_down", act, ddn, BF16, tmo=1024, tn=d, tt=tt)
    dup = _mlp_dact(ddn, wdn_b, act)
    g_wup = _mm_tn("grad_w_up", hm, dup, BF16, tmo=d, tn=1024, tt=tt)
    empty = lambda shapes: [lax.empty(s.shape, s.dtype) for s in shapes]
    grads = [g_wup, g_wdn.reshape(mlp_bigs[1].full_shape)]
    flight = _split("reduce_mlp_d2d", grads + empty(_halves(mlp_bigs)), _reduce_d2d_remote(mlp_bigs))
    dhm = _mm_f32("mlp_dhm", dup, wup_b, nt=True, after=flight[3])
    landed = _join(flight, dhm)
    mlp_chip = _chip_sums("mlp", mlp_bigs, landed[:2], landed[2:])
    flight = _split("reduce_mlp_ici", mlp_chip + empty(_thirds(mlp_bigs)), _reduce_ici_remote(mlp_bigs))
    dx1, dmi, st_mlp = _mlp_dx_rows(dhm, x1, dx2, mo, norm_mlp_w, sc_m, g_a, flight[3])
    st_mlp = _colsum(st_mlp)
    g_wout = _mm_tn("grad_w_out", mix, dmi, BF16, tmo=1024, tn=d, tt=tt)
    dmix = _mixer_dmix(dmi, wout_b)
    dq, dkp, dvp, dkc, dvc, dsink = _attn_bwd(q, k, v, kc, vc, attn_sink, dmix)
    landed = _join(flight, dq)
    flight = _split("reduce_mlp_share", _piece_sums("mlp", mlp_bigs, landed[:2], landed[2:]), _share_remote(mlp_bigs, 0))
    du, g_pw, st_pool = _pool_bwd(u, dmix, pw_b, pool_scale, flight[3])
    g_mlp = _join(flight, du)

    wo_bigs, win_bigs = mixer_bigs[1:], mixer_bigs[:1]
    wo_chip = _reduce_to_chip("wo", wo_bigs, [g_pw.astype(BF16), g_wout.reshape(wo_bigs[1].full_shape)])
    flight = _split("reduce_wo_ici", wo_chip + empty(_thirds(wo_bigs)), _reduce_ici_remote(wo_bigs))
    dp = _assemble_dp(dq, dkp, dvp, du, cos, sin, flight[3])
    dpc = jnp.concatenate([jnp.zeros((n_ctx, ATTN_WIDTH), BF16), dkc.astype(BF16), dvc.astype(BF16),
                           jnp.zeros((n_ctx, POOL_WIDTH), BF16)], axis=1)
    g_win = _mm_tn("grad_w_in", h, dp, BF16, tmo=d, tn=dp.shape[1] // 2, tt=_pick(length, 1024, 512, 256, 128), more=(hc, dpc))
    wo_landed = _join(flight, g_win)
    win_chip = _reduce_to_chip("win", win_bigs, [g_win])
    flight = _split("reduce_win_ici", win_chip + empty(_thirds(win_bigs)), _reduce_ici_remote(win_bigs))
    grad_x, st_mix = _mixer_dx("mixer_dx", dp, win_b, xs, dx1, norm_attn_w, sc_a, flight[3])
    _, st_ctx = _mixer_dx("mixer_dx_ctx", dpc, win_b, cx, jnp.zeros((n_ctx, d), F32), norm_attn_w, csc_a, flight[3])
    st_mix, st_ctx = _colsum(st_mix), _colsum(st_ctx)
    win_landed = _join(flight, grad_x)
    g_mixer = (_reduce_finish("win", win_bigs, win_landed[:1], win_landed[1:])
               + _reduce_finish("wo", wo_bigs, wo_landed[:2], wo_landed[2:]))

    zrow = jnp.zeros((d,), F32)
    pad = lambda a: jnp.pad(a, (0, d - a.shape[0]))
    mine_rows = [st_mix[0], st_mix[1], st_mlp[3], st_mlp[0], st_mlp[1], st_loss[2],
                 st_ctx[0], st_ctx[1],
                 st_mix[2] + st_ctx[2], st_mlp[2], st_loss[1],
                 pad(jnp.sum(st_pool, axis=0)), pad(dsink[0, :N_Q_HEADS])] + [zrow] * 3
    small_all = _allgather8("gather_small", jnp.concatenate(mine_rows).reshape(len(mine_rows), d))
    small = small_all[0]
    for i in range(1, 8):
        small = small + small_all[i]
    dm_rows = small_all[:, 0:6, :].reshape(8, 6 * d)
    dm_ctx = jnp.concatenate([small[6], small[7], jnp.zeros((4 * d,), F32)])[None, :]
    dm = jnp.concatenate([dm_rows, jnp.pad(dm_ctx, ((0, 7), (0, 0)))], axis=0)
    g_bada = jnp.sum(dm[:9], axis=0, keepdims=True)
    dm_shard = lax.dynamic_slice_in_dim(dm, chip * n_ada, n_ada, axis=1)
    g_wada, dl_wada, nm_wada, nv_wada, part_cctx = _ada_bwd(cond, dm_shard, w_ada[0], m_w_ada[0], v_w_ada[0])
    cctx_all = _allgather8("gather_cctx", part_cctx)
    dsilu_in = cctx_all[0, 0] + cctx_all[2, 0] + cctx_all[4, 0] + cctx_all[6, 0]
    sig = 1.0 / (1.0 + jnp.exp(-c_ctx))
    g_cctx = dsilu_in * (sig * (1.0 + c_ctx * (1.0 - sig)))

    g_shards = g_mixer + g_mlp
    big_w = [w_in, pool_w, w_out, w_mlp_up, w_mlp_down]
    big_m = [m_w_in, m_pool_w, m_w_out, m_w_mlp_up, m_w_mlp_down]
    big_v = [v_w_in, v_pool_w, v_w_out, v_w_mlp_up, v_w_mlp_down]
    big_names = ["w_in", "pool_w", "w_out", "w_mlp_up", "w_mlp_down"]
    res = {}
    for nm, w_, g_, m_, v_ in zip(big_names, big_w, g_shards, big_m, big_v):
        g_ = g_.reshape(w_.shape)
        res[nm] = (g_,) + tuple(_adamw("adamw_" + nm, w_, g_, m_, v_))
    res["w_ada"] = (g_wada[None], dl_wada[None], nm_wada[None], nv_wada[None])

    def pack(cc, na, nm_, ba, sk, ps, fn):
        flat = [cc.reshape(-1), na.reshape(-1), nm_.reshape(-1), ba.reshape(-1), pad(sk.reshape(-1)), pad(ps.reshape(-1)),
                fn.reshape(-1), jnp.zeros((4 * d,), F32)]
        return jnp.concatenate(flat).reshape(16, d)

    w_s = pack(c_ctx, norm_attn_w, norm_mlp_w, b_ada, attn_sink, pool_scale, final_norm_w)
    m_s = pack(m_c_ctx, m_norm_attn_w, m_norm_mlp_w, m_b_ada, m_attn_sink, m_pool_scale, m_final_norm_w)
    v_s = pack(v_c_ctx, v_norm_attn_w, v_norm_mlp_w, v_b_ada, v_attn_sink, v_pool_scale, v_final_norm_w)
    g_s = pack(g_cctx, small[8], small[9], g_bada, small[12][:N_Q_HEADS], small[11][:POOL_WIDTH], small[10])
    small_out = [g_s] + _adamw("adamw_small", w_s, g_s, m_s, v_s)

    def unpack(p):
        return {"c_ctx": p[0], "norm_attn_w": p[1:2], "norm_mlp_w": p[2:3], "b_ada": p[3:9].reshape(1, 6 * d),
                "attn_sink": p[9:10, :N_Q_HEADS], "pool_scale": p[10:11, :POOL_WIDTH], "final_norm_w": p[11]}

    small_res = [unpack(p) for p in small_out]
    order = ["c_ctx", "norm_attn_w", "norm_mlp_w", "w_ada", "b_ada", "w_in", "attn_sink", "pool_w", "pool_scale",
             "w_out", "w_mlp_up", "w_mlp_down", "final_norm_w"]
    outs = [loss, grad_x.reshape(x.shape)]
    for kind in range(4):
        for nm in order:
            outs.append(res[nm][kind] if nm in res else small_res[kind][nm])
    return tuple(outs)
```

```python
import functools

import jax
import jax.numpy as jnp
from jax import lax
from jax.experimental import pallas as pl
from jax.experimental.pallas import tpu as pltpu

F32 = jnp.float32
BF16 = jnp.bfloat16
EPS = 1e-6
NEG_INF = -1e30
HEAD_DIM = 64
N_Q_HEADS = 16
N_KV_HEADS = 4
GROUP = N_Q_HEADS // N_KV_HEADS
ATTN_WIDTH = N_Q_HEADS * HEAD_DIM
KV_WIDTH = N_KV_HEADS * HEAD_DIM
POOL_WINDOWS = (2, 4, 8, 16)
POOL_GROUP_DIM = 256
POOL_WIDTH = len(POOL_WINDOWS) * POOL_GROUP_DIM
BLOCK = 128
GRID_W = 64
ROPE_BASE = 10000.0
SCALE = HEAD_DIM ** -0.5
HALO = 16
ROWS = 64
ADAM_LR, ADAM_B1, ADAM_B2, ADAM_EPS, ADAM_WD, ADAM_STEP = 0.001, 0.9, 0.999, 1e-08, 0.01, 10
MESH = pl.DeviceIdType.MESH
MIB = 1024 * 1024
ANY = pl.BlockSpec(memory_space=pl.ANY)


def _cp(n_axes, vmem_mib=48):
    return pltpu.CompilerParams(dimension_semantics=("arbitrary",) * n_axes, vmem_limit_bytes=vmem_mib * MIB)


def _row_loop(rows, fn):
    def body(r, carry):
        fn(pl.ds(pl.multiple_of(r * ROWS, ROWS), ROWS))
        return carry

    lax.fori_loop(0, rows // ROWS, body, 0)


def _fold8(v):
    s = v[0:8]
    for t in range(1, v.shape[0] // 8):
        s = s + v[8 * t:8 * t + 8]
    return s


def _dot(a, b):
    return jnp.dot(a, b, preferred_element_type=F32)


def _dot_nt(a, b):
    return lax.dot_general(a, b, (((1,), (1,)), ((), ())), preferred_element_type=F32)


def _dot_tn(a, b):
    return lax.dot_general(a, b, (((0,), (0,)), ((), ())), preferred_element_type=F32)


def _pick(n, *cands):
    for t in cands:
        if n % t == 0:
            return t
    return n


def _flip(pos, mask):
    return tuple((1 - v) if (mask >> (2 - i)) & 1 else v for i, v in enumerate(pos))


def _exchange(name, ins, out_shapes, remote, local=(), aliases=None):
    n_io = len(ins) + len(out_shapes)

    def body(*refs):
        io = refs[:n_io]
        send_sems, recv_sems, local_sems = refs[n_io:]
        me = (lax.axis_index("x"), lax.axis_index("y"), lax.axis_index("c"))

        def copy(i, sender):
            mask, src_fn, dst_fn = remote[i]
            return pltpu.make_async_remote_copy(
                src_ref=src_fn(io, sender), dst_ref=dst_fn(io, sender), send_sem=send_sems.at[i],
                recv_sem=recv_sems.at[i], device_id=_flip(sender, mask), device_id_type=MESH)

        own = [pltpu.make_async_copy(s(io, me), d(io, me), local_sems.at[i]) for i, (s, d) in enumerate(local)]
        for cp in own:
            cp.start()
        sends = [copy(i, me) for i in range(len(remote))]
        for cp in sends:
            cp.start()
        for i in range(len(remote)):
            copy(i, _flip(me, remote[i][0])).wait_recv()
        for cp in sends:
            cp.wait_send()
        for cp in own:
            cp.wait()

    return pl.pallas_call(
        body, name=name, out_shape=tuple(out_shapes),
        in_specs=[ANY] * len(ins), out_specs=tuple([ANY] * len(out_shapes)),
        scratch_shapes=[pltpu.SemaphoreType.DMA((len(remote),)), pltpu.SemaphoreType.DMA((len(remote),)),
                        pltpu.SemaphoreType.DMA((max(len(local), 1),))],
        input_output_aliases=aliases or {},
    )(*ins)


HBM = pl.BlockSpec(memory_space=pltpu.HBM)
SEM = pl.BlockSpec(memory_space=pltpu.SEMAPHORE)
EFFECT = pltpu.SideEffectType.DATAFLOW_SIDE_EFFECTING


def _split_copy(remote, i, io, send_sems, recv_sems, sender):
    mask, src_fn, dst_fn = remote[i]
    return pltpu.make_async_remote_copy(
        src_ref=src_fn(io, sender), dst_ref=dst_fn(io, sender), send_sem=send_sems.at[i],
        recv_sem=recv_sems.at[i], device_id=_flip(sender, mask), device_id_type=MESH)


def _exchange_start(name, bufs, remote):
    n, r = len(bufs), len(remote)

    def body(*refs):
        io, send_sems, recv_sems, token = refs[:n], refs[2 * n], refs[2 * n + 1], refs[2 * n + 2]
        me = (lax.axis_index("x"), lax.axis_index("y"), lax.axis_index("c"))
        for i in range(r):
            _split_copy(remote, i, io, send_sems, recv_sems, me).start()
        token[...] = jnp.zeros_like(token)

    res = pl.pallas_call(
        body, name=name,
        out_shape=tuple(pltpu.HBM(b.shape, b.dtype) for b in bufs)
        + (pltpu.SemaphoreType.DMA((r,)), pltpu.SemaphoreType.DMA((r,)), jax.ShapeDtypeStruct((8, 128), F32)),
        in_specs=[HBM] * n, out_specs=tuple([HBM] * n) + (SEM, SEM, pl.BlockSpec(memory_space=pltpu.VMEM)),
        input_output_aliases={i: i for i in range(n)}, compiler_params=pltpu.CompilerParams(has_side_effects=EFFECT),
    )(*[pltpu.with_memory_space_constraint(b, pltpu.HBM) for b in bufs])
    return list(res[:n]), res[n], res[n + 1], res[n + 2]


def _exchange_wait(name, bufs, send_sems, recv_sems, remote, after):
    n, r = len(bufs), len(remote)

    def body(*refs):
        io, ss, rs = refs[:n], refs[n], refs[n + 1]
        me = (lax.axis_index("x"), lax.axis_index("y"), lax.axis_index("c"))
        for i in range(r):
            _split_copy(remote, i, io, ss, rs, _flip(me, remote[i][0])).wait_recv()
        for i in range(r):
            _split_copy(remote, i, io, ss, rs, me).wait_send()

    return list(pl.pallas_call(
        body, name=name, out_shape=tuple(pltpu.HBM(b.shape, b.dtype) for b in bufs),
        in_specs=[HBM] * n + [SEM, SEM, ANY], out_specs=tuple([HBM] * n),
        input_output_aliases={i: i for i in range(n)}, compiler_params=pltpu.CompilerParams(has_side_effects=EFFECT),
    )(*bufs, send_sems, recv_sems, after))


def _my_c():
    return lax.axis_index("c")


def _my_chip():
    return 2 * lax.axis_index("x") + lax.axis_index("y")


def _dev_index(pos):
    return 4 * pos[0] + 2 * pos[1] + pos[2]


def _chip_index(pos):
    return 2 * pos[0] + pos[1]


def _allgather8(name, v):
    out = jax.ShapeDtypeStruct((8,) + v.shape, v.dtype)
    remote = [(mask, lambda io, pos: io[0], lambda io, pos: io[1].at[_dev_index(pos)]) for mask in range(1, 8)]
    local = [(lambda io, pos: io[0], lambda io, pos: io[1].at[_dev_index(pos)])]
    return _exchange(name, [v], [out], remote, local)[0]


class _Big:
    def __init__(self, kind, shard_shape):
        self.kind = kind
        self.shard_shape = tuple(shard_shape)
        if kind == "col":
            r, cs = shard_shape
            self.full_shape = (r, 4 * cs)
            self.piece_shape = (r // 2, cs)
            self.half_shape = (r // 2, 4 * cs)
        elif kind == "row":
            rs, c = shard_shape
            self.full_shape = (4, 2, rs // 2, c)
            self.piece_shape = (1, 1, rs // 2, c)
            self.half_shape = (4, 1, rs // 2, c)
        else:
            self.full_shape = (4, 256, 256)
            self.piece_shape = (2, 64, 256)
            self.half_shape = (2, 256, 256)

    def shard_as_pieces(self, a):
        return a.reshape((1, 2) + self.piece_shape[2:]) if self.kind == "row" else a

    def piece(self, ref, k, h):
        if self.kind == "col":
            r, cs = self.piece_shape
            return ref.at[pl.ds(h * r, r), pl.ds(k * cs, cs)]
        if self.kind == "row":
            return ref.at[pl.ds(k, 1), pl.ds(h, 1)]
        return ref.at[pl.ds(2 * h, 2), pl.ds(64 * k, 64)]

    def half_of_shard(self, ref, h):
        if self.kind == "col":
            return ref.at[pl.ds(h * self.piece_shape[0], self.piece_shape[0])]
        if self.kind == "row":
            return ref.at[:, pl.ds(h, 1)]
        return ref.at[pl.ds(2 * h, 2)]

    def half_of_full(self, ref, h):
        if self.kind == "col":
            return ref.at[pl.ds(h * self.half_shape[0], self.half_shape[0])]
        if self.kind == "row":
            return ref.at[:, pl.ds(h, 1)]
        return ref.at[pl.ds(2 * h, 2)]

    def piece_of_half(self, ref, k):
        if self.kind == "col":
            return ref.at[:, pl.ds(k * self.piece_shape[1], self.piece_shape[1])]
        if self.kind == "row":
            return ref.at[pl.ds(k, 1)]
        return ref.at[:, pl.ds(64 * k, 64)]


CHIP_MASKS = (4, 2, 6)


def _cast_place(name, big, shard, after):
    if big.kind == "col":
        r, cs = big.shard_shape
        tr = _pick(r, 512, 256, 128)
        src, grid, blk = shard, (r // tr,), (tr, cs)
        imap, omap = (lambda i: (i, 0)), (lambda i: (i, _my_chip()))
    elif big.kind == "row":
        rs, c = big.shard_shape
        tr = _pick(rs // 2, 256, 128)
        src, grid, blk = big.shard_as_pieces(shard), (2, rs // 2 // tr), (1, 1, tr, c)
        imap, omap = (lambda h, i: (0, h, i, 0)), (lambda h, i: (_my_chip(), h, i, 0))
    else:
        src, grid, blk = shard, (1,), big.shard_shape
        imap, omap = (lambda i: (0, 0, 0)), (lambda i: (0, _my_chip(), 0))

    def body(s_ref, after_ref, o_ref):
        o_ref[...] = s_ref[...].astype(BF16)

    return pl.pallas_call(
        body, name=name, grid=grid, in_specs=[pl.BlockSpec(blk, imap), ANY], out_specs=pl.BlockSpec(blk, omap),
        out_shape=jax.ShapeDtypeStruct(big.full_shape, BF16), compiler_params=_cp(len(grid)),
    )(src, after)


def _gather_ici_remote(bigs, off):
    remote = []
    for a, b in enumerate(bigs):
        for mask in CHIP_MASKS:
            def mine(io, p, a=a, b=b):
                return b.piece(io[off + a], _chip_index(p), p[2])
            remote.append((mask, mine, mine))
    return remote


def _gather_d2d_remote(bigs, off):
    remote = []
    for a, b in enumerate(bigs):
        for mask in CHIP_MASKS:
            def region(io, p, a=a, b=b, mask=mask):
                return b.piece(io[off + a], _chip_index(_flip(p, mask)), p[2])
            remote.append((1, region, region))
    return remote


def _ew(name, fn, ins, out_dtypes, rows_per_step=256):
    shape = ins[0].shape
    last = shape[-1]
    rows = 1
    for s in shape[:-1]:
        rows *= s
    ins2 = [a.reshape(rows, last) for a in ins]
    tr = _pick(rows, rows_per_step, 128, 64, 32, 16, 8)
    spec = pl.BlockSpec((tr, last), lambda i: (i, 0))

    def body(*refs):
        outs = fn(*[r[...] for r in refs[:len(ins)]])
        for o_ref, o in zip(refs[len(ins):], outs):
            o_ref[...] = o.astype(o_ref.dtype)

    outs = pl.pallas_call(
        body, name=name, grid=(rows // tr,), in_specs=[spec] * len(ins), out_specs=tuple([spec] * len(out_dtypes)),
        out_shape=tuple(jax.ShapeDtypeStruct((rows, last), d) for d in out_dtypes), compiler_params=_cp(1),
    )(*ins2)
    return [o.reshape(shape) for o in outs]


def _chip_sum(name, big, grad, from_sibling):
    if big.kind == "col":
        rh, w = big.half_shape
        tr = _pick(rh, 256, 128)
        nb = rh // tr
        grid, blk = (nb,), (tr, w)
        gmap, hmap = (lambda i: (_my_c() * nb + i, 0)), (lambda i: (i, 0))
    elif big.kind == "row":
        rh, w = big.half_shape[2:]
        tr = _pick(rh, 256, 128)
        grid, blk = (4, rh // tr), (1, 1, tr, w)
        gmap, hmap = (lambda k, i: (k, _my_c(), i, 0)), (lambda k, i: (k, 0, i, 0))
    else:
        grid, blk = (1,), big.half_shape
        gmap, hmap = (lambda i: (_my_c(), 0, 0)), (lambda i: (0, 0, 0))

    def body(g_ref, s_ref, o_ref):
        o_ref[...] = (g_ref[...].astype(F32) + s_ref[...].astype(F32)).astype(BF16)

    return pl.pallas_call(
        body, name=name, grid=grid, in_specs=[pl.BlockSpec(blk, gmap), pl.BlockSpec(blk, hmap)],
        out_specs=pl.BlockSpec(blk, hmap), out_shape=jax.ShapeDtypeStruct(big.half_shape, BF16), compiler_params=_cp(len(grid)),
    )(grad, from_sibling)


def _piece_sum(name, big, chip_sum, thirds):
    if big.kind == "col":
        rp, cs = big.piece_shape
        tr = _pick(rp, 256, 128)
        nb = rp // tr
        grid, blk, tblk = (nb,), (tr, cs), (1, tr, cs)
        smap, omap = (lambda i: (i, _my_chip())), (lambda i: (_my_c() * nb + i, 0))
        tmap = lambda j: (lambda i: (j, i, 0))
        out_shape = big.shard_shape
    elif big.kind == "row":
        rp, w = big.piece_shape[2:]
        tr = _pick(rp, 256, 128)
        grid, blk, tblk = (rp // tr,), (1, 1, tr, w), (1, 1, 1, tr, w)
        smap, omap = (lambda i: (_my_chip(), 0, i, 0)), (lambda i: (0, _my_c(), i, 0))
        tmap = lambda j: (lambda i: (j, 0, 0, i, 0))
        out_shape = (1, 2, rp, w)
    else:
        grid, blk, tblk = (1,), big.piece_shape, (1,) + big.piece_shape
        smap, omap = (lambda i: (0, _my_chip(), 0)), (lambda i: (_my_c(), 0, 0))
        tmap = lambda j: (lambda i: (j, 0, 0, 0))
        out_shape = big.shard_shape

    def body(s_ref, t0, t1, t2, o_ref):
        o_ref[...] = s_ref[...].astype(F32) + t0[0].astype(F32) + t1[0].astype(F32) + t2[0].astype(F32)

    return pl.pallas_call(
        body, name=name, grid=grid,
        in_specs=[pl.BlockSpec(blk, smap)] + [pl.BlockSpec(tblk, tmap(j)) for j in range(3)],
        out_specs=pl.BlockSpec(blk, omap), out_shape=jax.ShapeDtypeStruct(out_shape, F32), compiler_params=_cp(len(grid)),
    )(chip_sum, thirds, thirds, thirds)


def _split(name, bufs, remote):
    return _exchange_start(name + "_start", bufs, remote) + (remote, name)


def _join(handle, after):
    bufs, send_sems, recv_sems, _, remote, name = handle
    return _exchange_wait(name + "_wait", bufs, send_sems, recv_sems, remote, after)


def _reduce_d2d_remote(bigs):
    n = len(bigs)
    return [(1, lambda io, p, a=a, b=b: b.half_of_full(io[a], 1 - p[2]), lambda io, p, a=a: io[n + a])
            for a, b in enumerate(bigs)]


def _halves(bigs):
    return [jax.ShapeDtypeStruct(b.half_shape, BF16) for b in bigs]


def _chip_sums(tag, bigs, grads, from_sibling):
    return [_chip_sum(f"reduce_{tag}_chip_sum_{a}", b, g, r) for a, (b, g, r) in enumerate(zip(bigs, grads, from_sibling))]


def _reduce_to_chip(tag, bigs, grads):
    from_sibling = _exchange(f"reduce_{tag}_d2d", grads, _halves(bigs), _reduce_d2d_remote(bigs))
    return _chip_sums(tag, bigs, grads, from_sibling)


def _reduce_ici_remote(bigs):
    n = len(bigs)
    remote = []
    for a, b in enumerate(bigs):
        for j, mask in enumerate(CHIP_MASKS):
            remote.append((mask,
                           lambda io, p, a=a, b=b, mask=mask: b.piece_of_half(io[a], _chip_index(_flip(p, mask))),
                           lambda io, p, a=a, j=j: io[n + a].at[j]))
    return remote


def _thirds(bigs):
    return [jax.ShapeDtypeStruct((3,) + b.piece_shape, BF16) for b in bigs]


def _piece_sums(tag, bigs, chip_sum, from_chips):
    return [_piece_sum(f"reduce_{tag}_sum_{a}", b, s, r) for a, (b, s, r) in enumerate(zip(bigs, chip_sum, from_chips))]


def _share_remote(bigs, off):
    remote = []
    for a, b in enumerate(bigs):
        def mine(io, p, a=a, b=b):
            return b.half_of_shard(io[off + a], p[2])
        remote.append((1, mine, mine))
    return remote


def _reduce_finish(tag, bigs, chip_sum, from_chips):
    n = len(bigs)
    placed = _piece_sums(tag, bigs, chip_sum, from_chips)
    out = _exchange(f"reduce_{tag}_share_d2d", placed, [jax.ShapeDtypeStruct(p.shape, F32) for p in placed],
                    _share_remote(bigs, n), aliases={a: a for a in range(n)})
    return [o.reshape(b.shard_shape) for o, b in zip(out, bigs)]


def _mm(name, a, b, *, nt, tm, tn, tk, epi, extras=(), extra_specs=(), out_shape, out_specs, after=None, vmem_mib=48):
    m, kdim = a.shape
    n = b.shape[0] if nt else b.shape[1]
    gm, gn, gk = m // tm, n // tn, kdim // tk
    a_spec = pl.BlockSpec((tm, tk), lambda j, i, k: (i, k))
    b_spec = pl.BlockSpec((tn, tk), lambda j, i, k: (j, k)) if nt else pl.BlockSpec((tk, tn), lambda j, i, k: (k, j))
    n_ex = len(extras)
    if after is not None:
        extras, extra_specs = tuple(extras) + (after,), list(extra_specs) + [ANY]

    def body(a_ref, b_ref, *rest):
        ex, outs, acc = rest[:n_ex], rest[len(extras):-1], rest[-1]
        dot = _dot_nt if nt else _dot
        if gk == 1:
            acc[...] = dot(a_ref[...], b_ref[...])
            epi(acc, ex, outs)
        else:
            k = pl.program_id(2)

            @pl.when(k == 0)
            def _():
                acc[...] = jnp.zeros_like(acc)

            acc[...] += dot(a_ref[...], b_ref[...])

            @pl.when(k == gk - 1)
            def _():
                epi(acc, ex, outs)

    return pl.pallas_call(
        body, name=name, grid=(gn, gm, gk), in_specs=[a_spec, b_spec, *extra_specs], out_specs=tuple(out_specs),
        out_shape=tuple(out_shape), scratch_shapes=[pltpu.VMEM((tm, tn), F32)], compiler_params=_cp(3, vmem_mib),
    )(a, b, *extras)


def _mm_tn(name, a, b, out_dtype, *, tmo, tn, tt, more=(), vmem_mib=56):
    t, m = a.shape
    n = b.shape[1]
    gt = t // tt

    def body(a_ref, b_ref, *rest):
        o_ref, acc = rest[-2:]
        k = pl.program_id(2)

        @pl.when(k == 0)
        def _():
            acc[...] = _dot_tn(rest[0][...], rest[1][...]) if more else jnp.zeros_like(acc)

        acc[...] += _dot_tn(a_ref[...], b_ref[...])

        @pl.when(k == gt - 1)
        def _():
            o_ref[...] = acc[...].astype(o_ref.dtype)

    more_specs = [pl.BlockSpec((more[0].shape[0], tmo), lambda i, j, k: (0, i)),
                  pl.BlockSpec((more[1].shape[0], tn), lambda i, j, k: (0, j))] if more else []
    return pl.pallas_call(
        body, name=name, grid=(m // tmo, n // tn, gt),
        in_specs=[pl.BlockSpec((tt, tmo), lambda i, j, k: (k, i)), pl.BlockSpec((tt, tn), lambda i, j, k: (k, j))] + more_specs,
        out_specs=pl.BlockSpec((tmo, tn), lambda i, j, k: (i, j)), out_shape=jax.ShapeDtypeStruct((m, n), out_dtype),
        scratch_shapes=[pltpu.VMEM((tmo, tn), F32)], compiler_params=_cp(3, vmem_mib),
    )(a, b, *more)


def _row_spec(d):
    return pl.BlockSpec((1, d), lambda *_: (0, 0))


def _stat_spec(k, d):
    return pl.BlockSpec((k, 8, d), lambda *_: (0, 0, 0))


def _rope(z, cs, sn):
    first = (lax.broadcasted_iota(jnp.int32, (z.shape[0], 128), 1) % 32) < 16
    outs = []
    for j in range(z.shape[1] // 128):
        zc = z[:, 128 * j:128 * (j + 1)]
        partner = jnp.where(first, pltpu.roll(zc, 112, 1), pltpu.roll(zc, 16, 1))
        outs.append(zc * cs + partner * sn)
    return outs[0] if len(outs) == 1 else jnp.concatenate(outs, axis=1)


def _rope_tables(length, rotate):
    if not rotate:
        return jnp.ones((length, 128), F32), jnp.zeros((length, 128), F32)
    half = HEAD_DIM // 2
    inv_freq = ROPE_BASE ** (-jnp.arange(0, half, 2, dtype=F32) / half)
    rows = length // GRID_W
    ang_row = jnp.arange(rows, dtype=F32)[:, None] * inv_freq[None, :]
    ang_col = jnp.arange(GRID_W, dtype=F32)[:, None] * inv_freq[None, :]

    def spread(of_row, of_col, sign):
        r = jnp.broadcast_to(of_row[:, None, :], (rows, GRID_W, half // 2))
        c = jnp.broadcast_to(of_col[None, :, :], (rows, GRID_W, half // 2))
        head = jnp.concatenate([sign * r, r, sign * c, c], axis=-1)
        return jnp.concatenate([head, head], axis=-1).reshape(length, 128)

    return spread(jnp.cos(ang_row), jnp.cos(ang_col), 1.0), spread(jnp.sin(ang_row), jnp.sin(ang_col), -1.0)


def _mixer_in(name, x, nw, sh, sc, w_in, cos, sin, after):
    t, d = x.shape
    tm = _pick(t, 256, 128)
    n_in = w_in.shape[1]

    def body(x_ref, nw_ref, sh_ref, sc_ref, w_ref, cos_ref, sin_ref, after_ref, h_ref, q_ref, k_ref, v_ref, u_ref):
        xf = x_ref[...]
        r = lax.rsqrt(jnp.mean(xf * xf, axis=-1, keepdims=True) + EPS)
        hb = (((xf * r) * nw_ref[...]) * (1.0 + sc_ref[...]) + sh_ref[...]).astype(BF16)
        h_ref[...] = hb
        p = _dot(hb, w_ref[...])
        cs, sn = cos_ref[...], sin_ref[...]
        q_ref[...] = (_rope(p[:, :ATTN_WIDTH], cs, sn) * SCALE).astype(BF16)
        k_ref[...] = _rope(p[:, ATTN_WIDTH:ATTN_WIDTH + KV_WIDTH], cs, sn).astype(BF16)
        v_ref[...] = p[:, ATTN_WIDTH + KV_WIDTH:ATTN_WIDTH + 2 * KV_WIDTH].astype(BF16)
        u_ref[...] = p[:, ATTN_WIDTH + 2 * KV_WIDTH:]

    def tile(w):
        return pl.BlockSpec((tm, w), lambda i: (i, 0))

    return pl.pallas_call(
        body, name=name, grid=(t // tm,),
        in_specs=[tile(d), _row_spec(d), _row_spec(d), _row_spec(d), pl.BlockSpec((d, n_in), lambda i: (0, 0)),
                  tile(128), tile(128), ANY],
        out_specs=(tile(d), tile(ATTN_WIDTH), tile(KV_WIDTH), tile(KV_WIDTH), tile(POOL_WIDTH)),
        out_shape=(jax.ShapeDtypeStruct((t, d), BF16), jax.ShapeDtypeStruct((t, ATTN_WIDTH), BF16),
                   jax.ShapeDtypeStruct((t, KV_WIDTH), BF16), jax.ShapeDtypeStruct((t, KV_WIDTH), BF16),
                   jax.ShapeDtypeStruct((t, POOL_WIDTH), F32)),
        compiler_params=_cp(1),
    )(x, nw, sh, sc, w_in, cos, sin, after)


def _attn_specs(nb, n_ctx):
    def blk(w, f):
        return pl.BlockSpec((BLOCK, w), lambda n: (f(n), 0))

    prev = lambda n: jnp.maximum(n - 1, 0)
    cur = lambda n: n
    nxt = lambda n: jnp.minimum(n + 1, nb - 1)
    kv = [blk(KV_WIDTH, prev), blk(KV_WIDTH, cur), blk(KV_WIDTH, nxt)]
    ctx = pl.BlockSpec((n_ctx, KV_WIDTH), lambda n: (0, 0))
    return [pl.BlockSpec(memory_space=pltpu.SMEM), blk(ATTN_WIDTH, cur)] + kv + kv + [ctx, ctx]


def _attn_mask(n, length, n_keys):
    row = lax.broadcasted_iota(jnp.int32, (GROUP * BLOCK, n_keys), 0) % BLOCK
    col = lax.broadcasted_iota(jnp.int32, (GROUP * BLOCK, n_keys), 1)
    kpos = (n - 1) * BLOCK + col
    return ((jnp.abs(col - BLOCK - row) <= BLOCK) & (kpos >= 0) & (kpos < length)) | (col >= 3 * BLOCK)


def _group_rows(block, g):
    return jnp.concatenate([block[:, HEAD_DIM * h:HEAD_DIM * (h + 1)] for h in range(GROUP * g, GROUP * (g + 1))], axis=0)


def _group_sink(sink_ref, g):
    head = lax.broadcasted_iota(jnp.int32, (GROUP * BLOCK, 1), 0) // BLOCK
    out = jnp.full((GROUP * BLOCK, 1), sink_ref[0, GROUP * g], F32)
    for j in range(1, GROUP):
        out = jnp.where(head == j, sink_ref[0, GROUP * g + j], out)
    return out


def _attn_fwd(q, k, v, kc, vc, sink):
    length = q.shape[0]
    nb = length // BLOCK
    n_ctx = kc.shape[0]
    n_keys = 3 * BLOCK + n_ctx

    def body(sink_ref, q_ref, kp, k0, kn, vp, v0, vn, kc_ref, vc_ref, o_ref, p_ref):
        n = pl.program_id(0)
        valid = _attn_mask(n, length, n_keys)
        qb = q_ref[...]
        kall = jnp.concatenate([kp[...], k0[...], kn[...], kc_ref[...]], axis=0)
        vall = jnp.concatenate([vp[...], v0[...], vn[...], vc_ref[...]], axis=0)
        outs = []
        for g in range(N_KV_HEADS):
            lanes = slice(HEAD_DIM * g, HEAD_DIM * (g + 1))
            s = jnp.where(valid, _dot_nt(_group_rows(qb, g), kall[:, lanes]), NEG_INF)
            sk = _group_sink(sink_ref, g)
            m = jnp.maximum(jnp.max(s, axis=-1, keepdims=True), sk)
            e = jnp.exp(s - m)
            e_sink = jnp.exp(sk - m)
            inv = 1.0 / (jnp.sum(e, axis=-1, keepdims=True) + e_sink)
            pb = (e * inv).astype(BF16)
            p_ref[0, g, :, :n_keys] = pb
            p_ref[0, g, :, n_keys:] = jnp.broadcast_to(e_sink * inv, (GROUP * BLOCK, 128)).astype(BF16)
            o = _dot(pb, vall[:, lanes])
            outs += [o[BLOCK * j:BLOCK * (j + 1)] for j in range(GROUP)]
        o_ref[...] = jnp.concatenate(outs, axis=1).astype(BF16)

    return pl.pallas_call(
        body, name="attn_fwd", grid=(nb,), in_specs=_attn_specs(nb, n_ctx),
        out_specs=(pl.BlockSpec((BLOCK, ATTN_WIDTH), lambda n: (n, 0)),
                   pl.BlockSpec((1, N_KV_HEADS, GROUP * BLOCK, n_keys + 128), lambda n: (n, 0, 0, 0))),
        out_shape=(jax.ShapeDtypeStruct((length, ATTN_WIDTH + POOL_WIDTH), BF16),
                   jax.ShapeDtypeStruct((nb, N_KV_HEADS, GROUP * BLOCK, n_keys + 128), BF16)), compiler_params=_cp(1),
    )(sink, q, k, k, k, v, v, v, kc, vc)


def _attn_bwd(q, k, v, kc, vc, sink, dmix, probs):
    length = q.shape[0]
    nb = length // BLOCK
    n_ctx = kc.shape[0]
    n_keys = 3 * BLOCK + n_ctx

    def body(sink_ref, q_ref, kp, k0, kn, vp, v0, vn, kc_ref, vc_ref, do_ref, p_ref,
             dq_ref, dkp_ref, dvp_ref, dkc_ref, dvc_ref, dsink_ref):
        n = pl.program_id(0)

        @pl.when(n == 0)
        def _():
            dkc_ref[...] = jnp.zeros_like(dkc_ref)
            dvc_ref[...] = jnp.zeros_like(dvc_ref)
            dsink_ref[...] = jnp.zeros_like(dsink_ref)

        qb, dob = q_ref[...], do_ref[...]
        kall = jnp.concatenate([kp[...], k0[...], kn[...], kc_ref[...]], axis=0)
        vall = jnp.concatenate([vp[...], v0[...], vn[...], vc_ref[...]], axis=0)
        srow = lax.broadcasted_iota(jnp.int32, (8, 128), 0)
        slane = lax.broadcasted_iota(jnp.int32, (8, 128), 1)
        dqs, dks, dvs = [], [], []
        dsink = jnp.zeros((8, 128), F32)
        for g in range(N_KV_HEADS):
            lanes = slice(HEAD_DIM * g, HEAD_DIM * (g + 1))
            kg, vg = kall[:, lanes], vall[:, lanes]
            qg, dog = _group_rows(qb, g), _group_rows(dob, g)
            pb = p_ref[0, g, :, :n_keys]
            p = pb.astype(F32)
            dp = _dot_nt(dog, vg)
            delta = jnp.sum(p * dp, axis=-1, keepdims=True)
            ds = (p * (dp - delta)).astype(BF16)
            dq = _dot(ds, kg) * SCALE
            dqs += [dq[BLOCK * j:BLOCK * (j + 1)] for j in range(GROUP)]
            dks.append(_dot_tn(ds, qg))
            dvs.append(_dot_tn(pb, dog))
            d_sink = p_ref[0, g, :, n_keys:].astype(F32)[:, :1] * delta
            for j in range(GROUP):
                total = -jnp.sum(d_sink[BLOCK * j:BLOCK * (j + 1)], axis=0, keepdims=True)
                dsink = dsink + jnp.where((srow == 0) & (slane == GROUP * g + j), total, 0.0)
        dq_ref[...] = jnp.concatenate(dqs, axis=1)
        dk = jnp.concatenate(dks, axis=1)
        dv = jnp.concatenate(dvs, axis=1)
        for j in range(3):
            dkp_ref[0, j] = dk[BLOCK * j:BLOCK * (j + 1)]
            dvp_ref[0, j] = dv[BLOCK * j:BLOCK * (j + 1)]
        dkc_ref[...] += dk[3 * BLOCK:]
        dvc_ref[...] += dv[3 * BLOCK:]
        dsink_ref[...] += dsink

    part = pl.BlockSpec((1, 3, BLOCK, KV_WIDTH), lambda n: (n, 0, 0, 0))
    ctx = pl.BlockSpec((n_ctx, KV_WIDTH), lambda n: (0, 0))
    return pl.pallas_call(
        body, name="attn_bwd", grid=(nb,),
        in_specs=_attn_specs(nb, n_ctx) + [pl.BlockSpec((BLOCK, ATTN_WIDTH), lambda n: (n, 0)),
                                           pl.BlockSpec((1,) + probs.shape[1:], lambda n: (n, 0, 0, 0))],
        out_specs=(pl.BlockSpec((BLOCK, ATTN_WIDTH), lambda n: (n, 0)), part, part, ctx, ctx,
                   pl.BlockSpec((8, 128), lambda n: (0, 0))),
        out_shape=(jax.ShapeDtypeStruct((length, ATTN_WIDTH), F32),
                   jax.ShapeDtypeStruct((nb, 3, BLOCK, KV_WIDTH), F32), jax.ShapeDtypeStruct((nb, 3, BLOCK, KV_WIDTH), F32),
                   jax.ShapeDtypeStruct((n_ctx, KV_WIDTH), F32), jax.ShapeDtypeStruct((n_ctx, KV_WIDTH), F32),
                   jax.ShapeDtypeStruct((8, 128), F32)),
        compiler_params=_cp(1),
    )(sink, q, k, k, k, v, v, v, kc, vc, dmix, probs)


def _assemble_dp(dq, dkp, dvp, du, cos, sin, after):
    length = dq.shape[0]
    nb = length // BLOCK

    def body(dq_ref, dka, dkb, dkc, dva, dvb, dvc, du_ref, cos_ref, sin_ref, after_ref, o_ref):
        n = pl.program_id(0)
        has_next = (n + 1 < nb).astype(F32)
        has_prev = (n > 0).astype(F32)
        cs, sn = cos_ref[...], -sin_ref[...]
        dk = dka[0, 0] * has_next + dkb[0, 0] + dkc[0, 0] * has_prev
        dv = dva[0, 0] * has_next + dvb[0, 0] + dvc[0, 0] * has_prev
        o_ref[:, :ATTN_WIDTH] = _rope(dq_ref[...], cs, sn).astype(BF16)
        o_ref[:, ATTN_WIDTH:ATTN_WIDTH + KV_WIDTH] = _rope(dk, cs, sn).astype(BF16)
        o_ref[:, ATTN_WIDTH + KV_WIDTH:ATTN_WIDTH + 2 * KV_WIDTH] = dv.astype(BF16)
        o_ref[:, ATTN_WIDTH + 2 * KV_WIDTH:] = du_ref[...]

    def part(slot, f):
        return pl.BlockSpec((1, 1, BLOCK, KV_WIDTH), lambda n: (f(n), slot, 0, 0))

    parts = [part(0, lambda n: jnp.minimum(n + 1, nb - 1)), part(1, lambda n: n), part(2, lambda n: jnp.maximum(n - 1, 0))]

    def tile(w):
        return pl.BlockSpec((BLOCK, w), lambda n: (n, 0))

    width = ATTN_WIDTH + 2 * KV_WIDTH + POOL_WIDTH
    return pl.pallas_call(
        body, name="assemble_dp", grid=(nb,),
        in_specs=[tile(ATTN_WIDTH)] + parts + parts + [tile(POOL_WIDTH), tile(128), tile(128), ANY],
        out_specs=tile(width), out_shape=jax.ShapeDtypeStruct((length, width), BF16), compiler_params=_cp(1),
    )(dq, dkp, dkp, dkp, dvp, dvp, dvp, du, cos, sin, after)


def _shift_rows(e, s):
    n = e.shape[0]
    return e if s % n == 0 else pltpu.roll(e, (-s) % n, 0)


def _window_sum(e, w, first):
    s, n = e, 1
    while n < w:
        s = s + _shift_rows(s, n)
        n *= 2
    return _shift_rows(s, first)


def _pool_geometry(i, tm, length):
    pos = i * tm - HALO + lax.broadcasted_iota(jnp.int32, (tm + 2 * HALO, 1), 0)
    inside = (pos >= 0) & (pos < length)
    inv_counts = []
    for w in POOL_WINDOWS:
        lo = jnp.clip(pos - w // 2, 0, length)
        hi = jnp.clip(pos - w // 2 + w, 0, length)
        inv_counts.append(1.0 / jnp.maximum(hi - lo, 1).astype(F32))
    return inside, inv_counts


def _halo_specs(tm, width, length, col=0):
    per = tm // HALO
    last = length // HALO - 1
    return [pl.BlockSpec((HALO, width), lambda i: (jnp.maximum(i * per - 1, 0), col)),
            pl.BlockSpec((tm, width), lambda i: (i, col)),
            pl.BlockSpec((HALO, width), lambda i: (jnp.minimum((i + 1) * per, last), col))]


def _pooled(ext, inv_counts, tm):
    outs = []
    for g, w in enumerate(POOL_WINDOWS):
        e = ext[:, POOL_GROUP_DIM * g:POOL_GROUP_DIM * (g + 1)]
        mean = _window_sum(e, w, -(w // 2)) * inv_counts[g]
        outs.append((mean - e)[HALO:HALO + tm])
    return outs


def _pool_fwd(u, pool_w, pool_scale, mix):
    length = u.shape[0]
    tm = _pick(length, 256, 128)

    def body(up, u0, un, w_ref, sc_ref, mix_ref, o_ref):
        inside, inv_counts = _pool_geometry(pl.program_id(0), tm, length)
        ext = jnp.where(inside, jnp.concatenate([up[...], u0[...], un[...]], axis=0), 0.0)
        pooled = _pooled(ext, inv_counts, tm)
        mixed = [_dot(pooled[g].astype(BF16), w_ref[g]) for g in range(len(POOL_WINDOWS))]
        o_ref[...] = (jnp.concatenate(mixed, axis=1) * sc_ref[...]).astype(BF16)

    return pl.pallas_call(
        body, name="pool_fwd", grid=(length // tm,),
        in_specs=_halo_specs(tm, POOL_WIDTH, length) + [pl.BlockSpec(pool_w.shape, lambda i: (0, 0, 0)), _row_spec(POOL_WIDTH), ANY],
        out_specs=pl.BlockSpec((tm, POOL_WIDTH), lambda i: (i, 1)),
        out_shape=jax.ShapeDtypeStruct(mix.shape, BF16), input_output_aliases={5: 0}, compiler_params=_cp(1),
    )(u, u, u, pool_w, pool_scale, mix)


def _pool_bwd(u, dmix, pool_w, pool_scale, after):
    length = u.shape[0]
    tm = _pick(length, 256, 128)
    n_g = len(POOL_WINDOWS)

    def body(up, u0, un, dp_, d0, dn_, w_ref, sc_ref, after_ref, du_ref, dw_ref, dsc_ref):
        i = pl.program_id(0)

        @pl.when(i == 0)
        def _():
            dw_ref[...] = jnp.zeros_like(dw_ref)
            dsc_ref[...] = jnp.zeros_like(dsc_ref)

        inside, inv_counts = _pool_geometry(i, tm, length)
        ext = jnp.where(inside, jnp.concatenate([up[...], u0[...], un[...]], axis=0), 0.0)
        dext = jnp.where(inside, jnp.concatenate([dp_[...], d0[...], dn_[...]], axis=0).astype(F32), 0.0)
        dmixed = (dext * sc_ref[...]).astype(BF16)
        pooled = _pooled(ext, inv_counts, tm)
        dus, dscs = [], []
        for g, w in enumerate(POOL_WINDOWS):
            lanes = slice(POOL_GROUP_DIM * g, POOL_GROUP_DIM * (g + 1))
            dpooled = _dot_nt(dmixed[:, lanes], w_ref[g])
            spread = _window_sum(dpooled * inv_counts[g], w, -(w // 2 - 1))
            dus.append((spread - dpooled)[HALO:HALO + tm])
            pb = pooled[g].astype(BF16)
            dw_ref[g] += _dot_tn(pb, dmixed[HALO:HALO + tm, lanes])
            prod = dext[HALO:HALO + tm, lanes] * _dot(pb, w_ref[g])
            dscs.append(_fold8(prod))
        du_ref[...] = jnp.concatenate(dus, axis=1).astype(BF16)
        dsc_ref[...] += jnp.concatenate(dscs, axis=1)

    return pl.pallas_call(
        body, name="pool_bwd", grid=(length // tm,),
        in_specs=_halo_specs(tm, POOL_WIDTH, length) + _halo_specs(tm, POOL_WIDTH, length, col=1)
        + [pl.BlockSpec(pool_w.shape, lambda i: (0, 0, 0)), _row_spec(POOL_WIDTH), ANY],
        out_specs=(pl.BlockSpec((tm, POOL_WIDTH), lambda i: (i, 0)), pl.BlockSpec((n_g, POOL_GROUP_DIM, POOL_GROUP_DIM), lambda i: (0, 0, 0)),
                   pl.BlockSpec((8, POOL_WIDTH), lambda i: (0, 0))),
        out_shape=(jax.ShapeDtypeStruct((length, POOL_WIDTH), BF16), jax.ShapeDtypeStruct((n_g, POOL_GROUP_DIM, POOL_GROUP_DIM), F32),
                   jax.ShapeDtypeStruct((8, POOL_WIDTH), F32)),
        compiler_params=_cp(1),
    )(u, u, u, dmix, dmix, dmix, pool_w, pool_scale, after)


def _mixer_out(mix, w_out, x, g_a, nmw, sh_m, sc_m, after):
    t, d = x.shape
    tm = _pick(t, 256, 128)

    def epi(acc, ex, outs):
        x_ref, ga, nw, sh, sc = ex
        x1_ref, mo_ref, hm_ref = outs

        def rows(rs):
            mo = acc[rs, :]
            x1 = x_ref[rs, :] + ga[...] * mo
            x1_ref[rs, :] = x1
            mo_ref[rs, :] = mo.astype(BF16)
            r = lax.rsqrt(jnp.mean(x1 * x1, axis=-1, keepdims=True) + EPS)
            hm_ref[rs, :] = (((x1 * r) * nw[...]) * (1.0 + sc[...]) + sh[...]).astype(BF16)

        _row_loop(tm, rows)

    tile = pl.BlockSpec((tm, d), lambda j, i, k: (i, 0))
    return _mm("mixer_out", mix, w_out, nt=False, tm=tm, tn=d, tk=mix.shape[1], epi=epi, after=after,
               extras=(x, g_a, nmw, sh_m, sc_m), extra_specs=[tile] + [_row_spec(d)] * 4,
               out_shape=(jax.ShapeDtypeStruct((t, d), F32), jax.ShapeDtypeStruct((t, d), BF16), jax.ShapeDtypeStruct((t, d), BF16)),
               out_specs=(tile, tile, tile))


def _mlp_up(hm, w_up):
    t, d = hm.shape
    tm = _pick(t, 1024, 512, 256, 128)
    tn = 2048

    def epi(acc, ex, outs):
        outs[0][...] = jnp.square(jnp.maximum(acc[...], 0.0)).astype(BF16)

    return _mm("mlp_up", hm, w_up, nt=False, tm=tm, tn=tn, tk=d, epi=epi,
               out_shape=(jax.ShapeDtypeStruct((t, w_up.shape[1]), BF16),),
               out_specs=(pl.BlockSpec((tm, tn), lambda j, i, k: (i, j)),))[0]


def _mm_f32(name, a, b, *, nt, after):
    m, kdim = a.shape
    n = b.shape[0] if nt else b.shape[1]
    tm, tn = _pick(m, 1024, 512, 256, 128), _pick(n, 1024)

    def epi(acc, ex, outs):
        outs[0][...] = acc[...]

    return _mm(name, a, b, nt=nt, tm=tm, tn=tn, tk=_pick(kdim, 2048), epi=epi, after=after,
               out_shape=(jax.ShapeDtypeStruct((m, n), F32),), out_specs=(pl.BlockSpec((tm, tn), lambda j, i, k: (i, j)),))[0]


def _rows_call(name, rows_fn, tiles, vecs, out_shape, n_stats, after):
    t, d = tiles[0].shape
    tm = _pick(t, 256, 128)
    n_t, n_v = len(tiles), len(vecs)

    def body(*refs):
        st_ref = refs[-1]

        @pl.when(pl.program_id(0) == 0)
        def _():
            st_ref[...] = jnp.zeros_like(st_ref)

        _row_loop(tm, lambda rs: rows_fn(rs, refs[:n_t], refs[n_t:n_t + n_v], refs[n_t + n_v + 1:-1], st_ref))

    tile = pl.BlockSpec((tm, d), lambda i: (i, 0))
    return pl.pallas_call(
        body, name=name, grid=(t // tm,), in_specs=[tile] * n_t + [_row_spec(d)] * n_v + [ANY],
        out_specs=tuple([tile] * len(out_shape)) + (_stat_spec(n_stats, d),),
        out_shape=tuple(out_shape) + (jax.ShapeDtypeStruct((n_stats, 8, d), F32),), compiler_params=_cp(1),
    )(*tiles, *vecs, after)


def _loss_rows(dn, x1, target, g_m, fw):
    t, d = x1.shape

    def rows_fn(rs, tiles, vecs, outs, st_ref):
        dn_ref, x1_ref, t_ref = tiles
        gm, fw_ref = vecs
        dx2_ref, ddn_ref = outs
        dnv = dn_ref[rs, :]
        x2 = x1_ref[rs, :] + gm[...] * dnv
        r = lax.rsqrt(jnp.mean(x2 * x2, axis=-1, keepdims=True) + EPS)
        xh = x2 * r
        diff = xh * fw_ref[...] - t_ref[rs, :]
        dy = diff * (1.0 / d)
        dxh = dy * fw_ref[...]
        dx2 = r * (dxh - xh * jnp.mean(dxh * xh, axis=-1, keepdims=True))
        dx2_ref[rs, :] = dx2
        ddn_ref[rs, :] = (dx2 * gm[...]).astype(BF16)
        st_ref[0] += _fold8(diff * diff)
        st_ref[1] += _fold8(dy * xh)
        st_ref[2] += _fold8(dx2 * dnv)

    return _rows_call("loss_rows", rows_fn, (dn, x1, target), (g_m, fw),
                      (jax.ShapeDtypeStruct((t, d), F32), jax.ShapeDtypeStruct((t, d), BF16)), 3, g_m)


def _mlp_dx_rows(dhm, x1, dx2, mo, nmw, sc_m, g_a, after):
    t, d = x1.shape

    def rows_fn(rs, tiles, vecs, outs, st_ref):
        dh_ref, x1_ref, dx2_ref, mo_ref = tiles
        nw, sc, ga = vecs
        dx1_ref, dmi_ref = outs
        dx1 = _norm_bwd_rows(dh_ref[rs, :], x1_ref[rs, :], nw[...], sc[...], st_ref) + dx2_ref[rs, :]
        dx1_ref[rs, :] = dx1
        dmi_ref[rs, :] = (dx1 * ga[...]).astype(BF16)
        st_ref[3] += _fold8(dx1 * mo_ref[rs, :].astype(F32))

    return _rows_call("mlp_dx_rows", rows_fn, (dhm, x1, dx2, mo), (nmw, sc_m, g_a),
                      (jax.ShapeDtypeStruct((t, d), F32), jax.ShapeDtypeStruct((t, d), BF16)), 4, after)


def _mlp_dact(ddn, w_down, act):
    t, d = ddn.shape
    tm = _pick(t, 512, 256, 128)
    tn = 2048

    def epi(acc, ex, outs):
        outs[0][...] = (acc[...] * (2.0 * jnp.sqrt(ex[0][...]).astype(F32))).astype(BF16)

    tile = pl.BlockSpec((tm, tn), lambda j, i, k: (i, j))
    return _mm("mlp_dact", ddn, w_down, nt=True, tm=tm, tn=tn, tk=d, epi=epi, extras=(act,), extra_specs=[tile],
               out_shape=(jax.ShapeDtypeStruct(act.shape, BF16),), out_specs=(tile,))[0]


def _norm_bwd_rows(dh, xv, nw, sc, st_ref):
    r = lax.rsqrt(jnp.mean(xv * xv, axis=-1, keepdims=True) + EPS)
    xh = xv * r
    dy = dh * (1.0 + sc)
    st_ref[0] += _fold8(dh)
    st_ref[1] += _fold8(dh * (xh * nw))
    st_ref[2] += _fold8(dy * xh)
    dxh = dy * nw
    return r * (dxh - xh * jnp.mean(dxh * xh, axis=-1, keepdims=True))


def _mixer_dmix(dmi, w_out):
    t, d = dmi.shape
    tm = _pick(t, 512, 256, 128)

    def epi(acc, ex, outs):
        outs[0][...] = acc[...].astype(BF16)

    n = w_out.shape[0]
    return _mm("mixer_dmix", dmi, w_out, nt=True, tm=tm, tn=n, tk=d, epi=epi,
               out_shape=(jax.ShapeDtypeStruct((t, n), BF16),), out_specs=(pl.BlockSpec((tm, n), lambda j, i, k: (i, 0)),))[0]


def _mixer_dx(name, dp, w_in, x, dx1, naw, sc_a, after):
    t, d = x.shape
    tm = _pick(t, 256, 128)

    def epi(acc, ex, outs):
        x_ref, dx1_ref, nw, sc = ex
        gx_ref, st_ref = outs

        @pl.when(pl.program_id(1) == 0)
        def _():
            st_ref[...] = jnp.zeros_like(st_ref)

        def rows(rs):
            gx_ref[rs, :] = _norm_bwd_rows(acc[rs, :], x_ref[rs, :], nw[...], sc[...], st_ref) + dx1_ref[rs, :]

        _row_loop(tm, rows)

    tile = pl.BlockSpec((tm, d), lambda j, i, k: (i, 0))
    return _mm(name, dp, w_in, nt=True, tm=tm, tn=d, tk=dp.shape[1], epi=epi, after=after,
               extras=(x, dx1, naw, sc_a), extra_specs=[tile, tile, _row_spec(d), _row_spec(d)],
               out_shape=(jax.ShapeDtypeStruct((t, d), F32), jax.ShapeDtypeStruct((3, 8, d), F32)),
               out_specs=(tile, _stat_spec(3, d)))


def _silu(v):
    return v / (1.0 + jnp.exp(-v))


def _ada_fwd(cond, w_ada, b_ada):
    d, n = w_ada.shape
    tn = 512

    def body(c_ref, w_ref, b_ref, o_ref):
        o_ref[...] = _dot(_silu(c_ref[...]).astype(BF16), w_ref[...].astype(BF16)) + b_ref[...]

    return pl.pallas_call(
        body, name="ada_fwd", grid=(n // tn,),
        in_specs=[pl.BlockSpec(cond.shape, lambda j: (0, 0)), pl.BlockSpec((d, tn), lambda j: (0, j)), pl.BlockSpec((1, tn), lambda j: (0, j))],
        out_specs=pl.BlockSpec((cond.shape[0], tn), lambda j: (0, j)), out_shape=jax.ShapeDtypeStruct((cond.shape[0], n), F32),
        compiler_params=_cp(1),
    )(cond, w_ada, b_ada)


def _adamw_math(w, g, m, v):
    m = ADAM_B1 * m + (1.0 - ADAM_B1) * g
    v = ADAM_B2 * v + (1.0 - ADAM_B2) * jnp.square(g)
    m_hat = m / (1.0 - ADAM_B1 ** ADAM_STEP)
    v_hat = v / (1.0 - ADAM_B2 ** ADAM_STEP)
    return -ADAM_LR * (m_hat / (jnp.sqrt(v_hat) + ADAM_EPS) + ADAM_WD * w), m, v


def _ada_bwd(cond, dm, w_ada, m_ada, v_ada):
    d, n = w_ada.shape
    tn = 256
    rows = cond.shape[0]

    def body(c_ref, dm_ref, w_ref, m_ref, v_ref, g_ref, dl_ref, nm_ref, nv_ref, pc_ref):
        @pl.when(pl.program_id(0) == 0)
        def _():
            pc_ref[...] = jnp.zeros_like(pc_ref)

        dmb = dm_ref[...].astype(BF16)
        w = w_ref[...]
        g = _dot_tn(_silu(c_ref[...]).astype(BF16), dmb)
        g_ref[...] = g
        dl_ref[...], nm_ref[...], nv_ref[...] = _adamw_math(w, g, m_ref[...], v_ref[...])
        pc_ref[...] += _dot_nt(dm_ref[8:16, :].astype(BF16), w.astype(BF16))

    tile = pl.BlockSpec((d, tn), lambda j: (0, j))
    like = jax.ShapeDtypeStruct((d, n), F32)
    return pl.pallas_call(
        body, name="ada_bwd", grid=(n // tn,),
        in_specs=[pl.BlockSpec((rows, d), lambda j: (0, 0)), pl.BlockSpec((rows, tn), lambda j: (0, j)), tile, tile, tile],
        out_specs=(tile, tile, tile, tile, pl.BlockSpec((8, d), lambda j: (0, 0))),
        out_shape=(like, like, like, like, jax.ShapeDtypeStruct((8, d), F32)), compiler_params=_cp(1),
    )(cond, dm, w_ada, m_ada, v_ada)


def _adamw(name, w, g, m, v):
    return _ew(name, lambda w_, g_, m_, v_: (g_,) + _adamw_math(w_, g_, m_, v_), [w, g, m, v], [F32, F32, F32, F32])


def _colsum(st):
    return jnp.sum(st, axis=1)


def kernel(x, c, ctx, c_ctx, norm_attn_w, norm_mlp_w, w_ada, b_ada, w_in, attn_sink, pool_w, pool_scale, w_out, w_mlp_up, w_mlp_down, final_norm_w, loss_target, m_c_ctx, m_norm_attn_w, m_norm_mlp_w, m_w_ada, m_b_ada, m_w_in, m_attn_sink, m_pool_w, m_pool_scale, m_w_out, m_w_mlp_up, m_w_mlp_down, m_final_norm_w, v_c_ctx, v_norm_attn_w, v_norm_mlp_w, v_w_ada, v_b_ada, v_w_in, v_attn_sink, v_pool_w, v_pool_scale, v_w_out, v_w_mlp_up, v_w_mlp_down, v_final_norm_w):
    length, d = x.shape[1], x.shape[2]
    n_ctx = ctx.shape[1]
    pos = (lax.axis_index("x"), lax.axis_index("y"), lax.axis_index("c"))
    me, chip = _dev_index(pos), _chip_index(pos)
    xs, tgt, cx = x.reshape(length, d), loss_target.reshape(length, d), ctx.reshape(n_ctx, d)
    n_ada = w_ada.shape[2]

    mixer_bigs = [_Big("col", w_in.shape[1:]), _Big("pool", pool_w.shape[1:]), _Big("row", w_out.shape[1:])]
    mlp_bigs = [_Big("col", w_mlp_up.shape[1:]), _Big("row", w_mlp_down.shape[1:])]
    placed = [_cast_place(f"place_{i}", b, s, c) for i, (b, s) in enumerate(zip(mixer_bigs, [w_in[0], pool_w[0], w_out[0]]))]
    flight = _split("gather_mixer_ici", placed, _gather_ici_remote(mixer_bigs, 0))

    c_all = _allgather8("gather_c", jnp.pad(c, ((0, 7), (0, 0))) + flight[3][0, 0])
    cond = jnp.concatenate([c_all[:, 0, :], jnp.pad(c_ctx[None, :], ((0, 7), (0, 0)))], axis=0)
    b_shard = lax.dynamic_slice_in_dim(b_ada, chip * n_ada, n_ada, axis=1)
    mod_all = _allgather8("gather_mod", _ada_fwd(cond, w_ada[0], b_shard))
    mod = jnp.concatenate([mod_all[0], mod_all[2], mod_all[4], mod_all[6]], axis=1)
    mine = lax.dynamic_slice_in_dim(mod, me, 1, axis=0)
    sh_a, sc_a, g_a, sh_m, sc_m, g_m = [mine[:, d * i:d * (i + 1)] for i in range(6)]
    csh_a, csc_a = mod[8:9, :d], mod[8:9, d:2 * d]

    win_b, pw_b, wout_b = _exchange("gather_mixer_d2d", _join(flight, mod), [jax.ShapeDtypeStruct(b.full_shape, BF16) for b in mixer_bigs],
                                    _gather_d2d_remote(mixer_bigs, 3), aliases={0: 0, 1: 1, 2: 2})
    wout_b = wout_b.reshape(-1, d)
    placed = [_cast_place(f"place_mlp_{i}", b, s, pw_b) for i, (b, s) in enumerate(zip(mlp_bigs, [w_mlp_up[0], w_mlp_down[0]]))]
    flight = _split("gather_mlp_ici", placed, _gather_ici_remote(mlp_bigs, 0))

    cos, sin = _rope_tables(length, True)
    one, zero = _rope_tables(n_ctx, False)
    h, q, k, v, u = _mixer_in("mixer_in", xs, norm_attn_w, sh_a, sc_a, win_b, cos, sin, flight[3])
    hc, _, kc, vc, _ = _mixer_in("mixer_in_ctx", cx, norm_attn_w, csh_a, csc_a, win_b, one, zero, flight[3])
    attn, probs = _attn_fwd(q, k, v, kc, vc, attn_sink)
    mix = _pool_fwd(u, pw_b, pool_scale, attn)
    flight = _split("gather_mlp_d2d", _join(flight, mix), _gather_d2d_remote(mlp_bigs, 0))
    x1, mo, hm = _mixer_out(mix, wout_b, xs, g_a, norm_mlp_w, sh_m, sc_m, flight[3])
    wup_b, wdn_b = _join(flight, hm)
    wdn_b = wdn_b.reshape(-1, d)
    act = _mlp_up(hm, wup_b)
    dn = _mm_f32("mlp_down", act, wdn_b, nt=False, after=c)
    dx2, ddn, st_loss = _loss_rows(dn, x1, tgt, g_m, final_norm_w[None, :])
    st_loss = _colsum(st_loss)
    loss = lax.psum(0.5 / d * jnp.sum(st_loss[0]), ("x", "y", "c"))

    tt = _pick(length, 2048, 1024, 512, 256, 128)
    g_wdn = _mm_tn("grad_w_down", act, ddn, BF16, tmo=1024, tn=d, tt=tt)
    dup = _mlp_dact(ddn, wdn_b, act)
    g_wup = _mm_tn("grad_w_up", hm, dup, BF16, tmo=d, tn=1024, tt=tt)
    empty = lambda shapes: [lax.empty(s.shape, s.dtype) for s in shapes]
    grads = [g_wup, g_wdn.reshape(mlp_bigs[1].full_shape)]
    flight = _split("reduce_mlp_d2d", grads + empty(_halves(mlp_bigs)), _reduce_d2d_remote(mlp_bigs))
    dhm = _mm_f32("mlp_dhm", dup, wup_b, nt=True, after=flight[3])
    landed = _join(flight, dhm)
    mlp_chip = _chip_sums("mlp", mlp_bigs, landed[:2], landed[2:])
    flight = _split("reduce_mlp_ici", mlp_chip + empty(_thirds(mlp_bigs)), _reduce_ici_remote(mlp_bigs))
    dx1, dmi, st_mlp = _mlp_dx_rows(dhm, x1, dx2, mo, norm_mlp_w, sc_m, g_a, flight[3])
    st_mlp = _colsum(st_mlp)
    g_wout = _mm_tn("grad_w_out", mix, dmi, BF16, tmo=1024, tn=d, tt=tt)
    dmix = _mixer_dmix(dmi, wout_b)
    dq, dkp, dvp, dkc, dvc, dsink = _attn_bwd(q, k, v, kc, vc, attn_sink, dmix, probs)
    landed = _join(flight, dq)
    flight = _split("reduce_mlp_share", _piece_sums("mlp", mlp_bigs, landed[:2], landed[2:]), _share_remote(mlp_bigs, 0))
    du, g_pw, st_pool = _pool_bwd(u, dmix, pw_b, pool_scale, flight[3])
    g_mlp = _join(flight, du)

    wo_bigs, win_bigs = mixer_bigs[1:], mixer_bigs[:1]
    wo_chip = _reduce_to_chip("wo", wo_bigs, [g_pw.astype(BF16), g_wout.reshape(wo_bigs[1].full_shape)])
    flight = _split("reduce_wo_ici", wo_chip + empty(_thirds(wo_bigs)), _reduce_ici_remote(wo_bigs))
    dp = _assemble_dp(dq, dkp, dvp, du, cos, sin, flight[3])
    dpc = jnp.concatenate([jnp.zeros((n_ctx, ATTN_WIDTH), BF16), dkc.astype(BF16), dvc.astype(BF16),
                           jnp.zeros((n_ctx, POOL_WIDTH), BF16)], axis=1)
    g_win = _mm_tn("grad_w_in", h, dp, BF16, tmo=d, tn=dp.shape[1] // 2, tt=_pick(length, 1024, 512, 256, 128), more=(hc, dpc))
    wo_landed = _join(flight, g_win)
    win_chip = _reduce_to_chip("win", win_bigs, [g_win])
    flight = _split("reduce_win_ici", win_chip + empty(_thirds(win_bigs)), _reduce_ici_remote(win_bigs))
    grad_x, st_mix = _mixer_dx("mixer_dx", dp, win_b, xs, dx1, norm_attn_w, sc_a, flight[3])
    _, st_ctx = _mixer_dx("mixer_dx_ctx", dpc, win_b, cx, jnp.zeros((n_ctx, d), F32), norm_attn_w, csc_a, flight[3])
    st_mix, st_ctx = _colsum(st_mix), _colsum(st_ctx)
    win_landed = _join(flight, grad_x)
    g_mixer = (_reduce_finish("win", win_bigs, win_landed[:1], win_landed[1:])
               + _reduce_finish("wo", wo_bigs, wo_landed[:2], wo_landed[2:]))

    zrow = jnp.zeros((d,), F32)
    pad = lambda a: jnp.pad(a, (0, d - a.shape[0]))
    mine_rows = [st_mix[0], st_mix[1], st_mlp[3], st_mlp[0], st_mlp[1], st_loss[2],
                 st_ctx[0], st_ctx[1],
                 st_mix[2] + st_ctx[2], st_mlp[2], st_loss[1],
                 pad(jnp.sum(st_pool, axis=0)), pad(dsink[0, :N_Q_HEADS])] + [zrow] * 3
    small_all = _allgather8("gather_small", jnp.concatenate(mine_rows).reshape(len(mine_rows), d))
    small = small_all[0]
    for i in range(1, 8):
        small = small + small_all[i]
    dm_rows = small_all[:, 0:6, :].reshape(8, 6 * d)
    dm_ctx = jnp.concatenate([small[6], small[7], jnp.zeros((4 * d,), F32)])[None, :]
    dm = jnp.concatenate([dm_rows, jnp.pad(dm_ctx, ((0, 7), (0, 0)))], axis=0)
    g_bada = jnp.sum(dm[:9], axis=0, keepdims=True)
    dm_shard = lax.dynamic_slice_in_dim(dm, chip * n_ada, n_ada, axis=1)
    g_wada, dl_wada, nm_wada, nv_wada, part_cctx = _ada_bwd(cond, dm_shard, w_ada[0], m_w_ada[0], v_w_ada[0])
    cctx_all = _allgather8("gather_cctx", part_cctx)
    dsilu_in = cctx_all[0, 0] + cctx_all[2, 0] + cctx_all[4, 0] + cctx_all[6, 0]
    sig = 1.0 / (1.0 + jnp.exp(-c_ctx))
    g_cctx = dsilu_in * (sig * (1.0 + c_ctx * (1.0 - sig)))

    g_shards = g_mixer + g_mlp
    big_w = [w_in, pool_w, w_out, w_mlp_up, w_mlp_down]
    big_m = [m_w_in, m_pool_w, m_w_out, m_w_mlp_up, m_w_mlp_down]
    big_v = [v_w_in, v_pool_w, v_w_out, v_w_mlp_up, v_w_mlp_down]
    big_names = ["w_in", "pool_w", "w_out", "w_mlp_up", "w_mlp_down"]
    res = {}
    for nm, w_, g_, m_, v_ in zip(big_names, big_w, g_shards, big_m, big_v):
        res[nm] = tuple(_adamw("adamw_" + nm, w_, g_.reshape(w_.shape), m_, v_))
    res["w_ada"] = (g_wada[None], dl_wada[None], nm_wada[None], nv_wada[None])

    def pack(cc, na, nm_, ba, sk, ps, fn):
        flat = [cc.reshape(-1), na.reshape(-1), nm_.reshape(-1), ba.reshape(-1), pad(sk.reshape(-1)), pad(ps.reshape(-1)),
                fn.reshape(-1), jnp.zeros((4 * d,), F32)]
        return jnp.concatenate(flat).reshape(16, d)

    w_s = pack(c_ctx, norm_attn_w, norm_mlp_w, b_ada, attn_sink, pool_scale, final_norm_w)
    m_s = pack(m_c_ctx, m_norm_attn_w, m_norm_mlp_w, m_b_ada, m_attn_sink, m_pool_scale, m_final_norm_w)
    v_s = pack(v_c_ctx, v_norm_attn_w, v_norm_mlp_w, v_b_ada, v_attn_sink, v_pool_scale, v_final_norm_w)
    g_s = pack(g_cctx, small[8], small[9], g_bada, small[12][:N_Q_HEADS], small[11][:POOL_WIDTH], small[10])
    small_out = _adamw("adamw_small", w_s, g_s, m_s, v_s)

    def unpack(p):
        return {"c_ctx": p[0], "norm_attn_w": p[1:2], "norm_mlp_w": p[2:3], "b_ada": p[3:9].reshape(1, 6 * d),
                "attn_sink": p[9:10, :N_Q_HEADS], "pool_scale": p[10:11, :POOL_WIDTH], "final_norm_w": p[11]}

    small_res = [unpack(p) for p in small_out]
    order = ["c_ctx", "norm_attn_w", "norm_mlp_w", "w_ada", "b_ada", "w_in", "attn_sink", "pool_w", "pool_scale",
             "w_out", "w_mlp_up", "w_mlp_down", "final_norm_w"]
    outs = [loss, grad_x.reshape(x.shape)]
    for kind in range(4):
        for nm in order:
            outs.append(res[nm][kind] if nm in res else small_res[kind][nm])
    return tuple(outs)
```

```python
import functools

import jax
import jax.numpy as jnp
from jax import lax
from jax.experimental import pallas as pl
from jax.experimental.pallas import tpu as pltpu

F32 = jnp.float32
BF16 = jnp.bfloat16
EPS = 1e-6
NEG_INF = -1e30
HEAD_DIM = 64
N_Q_HEADS = 16
N_KV_HEADS = 4
GROUP = N_Q_HEADS // N_KV_HEADS
ATTN_WIDTH = N_Q_HEADS * HEAD_DIM
KV_WIDTH = N_KV_HEADS * HEAD_DIM
POOL_WINDOWS = (2, 4, 8, 16)
POOL_GROUP_DIM = 256
POOL_WIDTH = len(POOL_WINDOWS) * POOL_GROUP_DIM
BLOCK = 128
GRID_W = 64
ROPE_BASE = 10000.0
SCALE = HEAD_DIM ** -0.5
HALO = 16
ROWS = 64
ADAM_LR, ADAM_B1, ADAM_B2, ADAM_EPS, ADAM_WD, ADAM_STEP = 0.001, 0.9, 0.999, 1e-08, 0.01, 10
MESH = pl.DeviceIdType.MESH
MIB = 1024 * 1024
ANY = pl.BlockSpec(memory_space=pl.ANY)


def _cp(n_axes, vmem_mib=48):
    return pltpu.CompilerParams(dimension_semantics=("arbitrary",) * n_axes, vmem_limit_bytes=vmem_mib * MIB)


def _row_loop(rows, fn):
    def body(r, carry):
        fn(pl.ds(pl.multiple_of(r * ROWS, ROWS), ROWS))
        return carry

    lax.fori_loop(0, rows // ROWS, body, 0)


def _fold8(v):
    s = v[0:8]
    for t in range(1, v.shape[0] // 8):
        s = s + v[8 * t:8 * t + 8]
    return s


def _dot(a, b):
    return jnp.dot(a, b, preferred_element_type=F32)


def _dot_nt(a, b):
    return lax.dot_general(a, b, (((1,), (1,)), ((), ())), preferred_element_type=F32)


def _dot_tn(a, b):
    return lax.dot_general(a, b, (((0,), (0,)), ((), ())), preferred_element_type=F32)


def _pick(n, *cands):
    for t in cands:
        if n % t == 0:
            return t
    return n


def _flip(pos, mask):
    return tuple((1 - v) if (mask >> (2 - i)) & 1 else v for i, v in enumerate(pos))


def _exchange(name, ins, out_shapes, remote, local=(), aliases=None):
    n_io = len(ins) + len(out_shapes)

    def body(*refs):
        io = refs[:n_io]
        send_sems, recv_sems, local_sems = refs[n_io:]
        me = (lax.axis_index("x"), lax.axis_index("y"), lax.axis_index("c"))

        def copy(i, sender):
            mask, src_fn, dst_fn = remote[i]
            return pltpu.make_async_remote_copy(
                src_ref=src_fn(io, sender), dst_ref=dst_fn(io, sender), send_sem=send_sems.at[i],
                recv_sem=recv_sems.at[i], device_id=_flip(sender, mask), device_id_type=MESH)

        own = [pltpu.make_async_copy(s(io, me), d(io, me), local_sems.at[i]) for i, (s, d) in enumerate(local)]
        for cp in own:
            cp.start()
        sends = [copy(i, me) for i in range(len(remote))]
        for cp in sends:
            cp.start()
        for i in range(len(remote)):
            copy(i, _flip(me, remote[i][0])).wait_recv()
        for cp in sends:
            cp.wait_send()
        for cp in own:
            cp.wait()

    return pl.pallas_call(
        body, name=name, out_shape=tuple(out_shapes),
        in_specs=[ANY] * len(ins), out_specs=tuple([ANY] * len(out_shapes)),
        scratch_shapes=[pltpu.SemaphoreType.DMA((len(remote),)), pltpu.SemaphoreType.DMA((len(remote),)),
                        pltpu.SemaphoreType.DMA((max(len(local), 1),))],
        input_output_aliases=aliases or {},
    )(*ins)


HBM = pl.BlockSpec(memory_space=pltpu.HBM)
SEM = pl.BlockSpec(memory_space=pltpu.SEMAPHORE)
EFFECT = pltpu.SideEffectType.DATAFLOW_SIDE_EFFECTING


def _split_copy(remote, i, io, send_sems, recv_sems, sender):
    mask, src_fn, dst_fn = remote[i]
    return pltpu.make_async_remote_copy(
        src_ref=src_fn(io, sender), dst_ref=dst_fn(io, sender), send_sem=send_sems.at[i],
        recv_sem=recv_sems.at[i], device_id=_flip(sender, mask), device_id_type=MESH)


def _exchange_start(name, bufs, remote):
    n, r = len(bufs), len(remote)

    def body(*refs):
        io, send_sems, recv_sems, token = refs[:n], refs[2 * n], refs[2 * n + 1], refs[2 * n + 2]
        me = (lax.axis_index("x"), lax.axis_index("y"), lax.axis_index("c"))
        for i in range(r):
            _split_copy(remote, i, io, send_sems, recv_sems, me).start()
        token[...] = jnp.zeros_like(token)

    res = pl.pallas_call(
        body, name=name,
        out_shape=tuple(pltpu.HBM(b.shape, b.dtype) for b in bufs)
        + (pltpu.SemaphoreType.DMA((r,)), pltpu.SemaphoreType.DMA((r,)), jax.ShapeDtypeStruct((8, 128), F32)),
        in_specs=[HBM] * n, out_specs=tuple([HBM] * n) + (SEM, SEM, pl.BlockSpec(memory_space=pltpu.VMEM)),
        input_output_aliases={i: i for i in range(n)}, compiler_params=pltpu.CompilerParams(has_side_effects=EFFECT),
    )(*[pltpu.with_memory_space_constraint(b, pltpu.HBM) for b in bufs])
    return list(res[:n]), res[n], res[n + 1], res[n + 2]


def _exchange_wait(name, bufs, send_sems, recv_sems, remote, after):
    n, r = len(bufs), len(remote)

    def body(*refs):
        io, ss, rs = refs[:n], refs[n], refs[n + 1]
        me = (lax.axis_index("x"), lax.axis_index("y"), lax.axis_index("c"))
        for i in range(r):
            _split_copy(remote, i, io, ss, rs, _flip(me, remote[i][0])).wait_recv()
        for i in range(r):
            _split_copy(remote, i, io, ss, rs, me).wait_send()

    return list(pl.pallas_call(
        body, name=name, out_shape=tuple(pltpu.HBM(b.shape, b.dtype) for b in bufs),
        in_specs=[HBM] * n + [SEM, SEM, ANY], out_specs=tuple([HBM] * n),
        input_output_aliases={i: i for i in range(n)}, compiler_params=pltpu.CompilerParams(has_side_effects=EFFECT),
    )(*bufs, send_sems, recv_sems, after))


def _my_c():
    return lax.axis_index("c")


def _my_chip():
    return 2 * lax.axis_index("x") + lax.axis_index("y")


def _dev_index(pos):
    return 4 * pos[0] + 2 * pos[1] + pos[2]


def _chip_index(pos):
    return 2 * pos[0] + pos[1]


def _allgather8(name, v):
    out = jax.ShapeDtypeStruct((8,) + v.shape, v.dtype)
    remote = [(mask, lambda io, pos: io[0], lambda io, pos: io[1].at[_dev_index(pos)]) for mask in range(1, 8)]
    local = [(lambda io, pos: io[0], lambda io, pos: io[1].at[_dev_index(pos)])]
    return _exchange(name, [v], [out], remote, local)[0]


class _Big:
    def __init__(self, kind, shard_shape):
        self.kind = kind
        self.shard_shape = tuple(shard_shape)
        if kind == "col":
            r, cs = shard_shape
            self.full_shape = (r, 4 * cs)
            self.piece_shape = (r // 2, cs)
            self.half_shape = (r // 2, 4 * cs)
        elif kind == "row":
            rs, c = shard_shape
            self.full_shape = (4, 2, rs // 2, c)
            self.piece_shape = (1, 1, rs // 2, c)
            self.half_shape = (4, 1, rs // 2, c)
        else:
            self.full_shape = (4, 256, 256)
            self.piece_shape = (2, 64, 256)
            self.half_shape = (2, 256, 256)

    def shard_as_pieces(self, a):
        return a.reshape((1, 2) + self.piece_shape[2:]) if self.kind == "row" else a

    def piece(self, ref, k, h):
        if self.kind == "col":
            r, cs = self.piece_shape
            return ref.at[pl.ds(h * r, r), pl.ds(k * cs, cs)]
        if self.kind == "row":
            return ref.at[pl.ds(k, 1), pl.ds(h, 1)]
        return ref.at[pl.ds(2 * h, 2), pl.ds(64 * k, 64)]

    def half_of_shard(self, ref, h):
        if self.kind == "col":
            return ref.at[pl.ds(h * self.piece_shape[0], self.piece_shape[0])]
        if self.kind == "row":
            return ref.at[:, pl.ds(h, 1)]
        return ref.at[pl.ds(2 * h, 2)]

    def half_of_full(self, ref, h):
        if self.kind == "col":
            return ref.at[pl.ds(h * self.half_shape[0], self.half_shape[0])]
        if self.kind == "row":
            return ref.at[:, pl.ds(h, 1)]
        return ref.at[pl.ds(2 * h, 2)]

    def piece_of_half(self, ref, k):
        if self.kind == "col":
            return ref.at[:, pl.ds(k * self.piece_shape[1], self.piece_shape[1])]
        if self.kind == "row":
            return ref.at[pl.ds(k, 1)]
        return ref.at[:, pl.ds(64 * k, 64)]


CHIP_MASKS = (4, 2, 6)


def _cast_place(name, big, shard, after):
    if big.kind == "col":
        r, cs = big.shard_shape
        tr = _pick(r, 512, 256, 128)
        src, grid, blk = shard, (r // tr,), (tr, cs)
        imap, omap = (lambda i: (i, 0)), (lambda i: (i, _my_chip()))
    elif big.kind == "row":
        rs, c = big.shard_shape
        tr = _pick(rs // 2, 256, 128)
        src, grid, blk = big.shard_as_pieces(shard), (2, rs // 2 // tr), (1, 1, tr, c)
        imap, omap = (lambda h, i: (0, h, i, 0)), (lambda h, i: (_my_chip(), h, i, 0))
    else:
        src, grid, blk = shard, (1,), big.shard_shape
        imap, omap = (lambda i: (0, 0, 0)), (lambda i: (0, _my_chip(), 0))

    def body(s_ref, after_ref, o_ref):
        o_ref[...] = s_ref[...].astype(BF16)

    return pl.pallas_call(
        body, name=name, grid=grid, in_specs=[pl.BlockSpec(blk, imap), ANY], out_specs=pl.BlockSpec(blk, omap),
        out_shape=jax.ShapeDtypeStruct(big.full_shape, BF16), compiler_params=_cp(len(grid)),
    )(src, after)


def _gather_ici_remote(bigs, off):
    remote = []
    for a, b in enumerate(bigs):
        for mask in CHIP_MASKS:
            def mine(io, p, a=a, b=b):
                return b.piece(io[off + a], _chip_index(p), p[2])
            remote.append((mask, mine, mine))
    return remote


def _gather_d2d_remote(bigs, off):
    remote = []
    for a, b in enumerate(bigs):
        for mask in CHIP_MASKS:
            def region(io, p, a=a, b=b, mask=mask):
                return b.piece(io[off + a], _chip_index(_flip(p, mask)), p[2])
            remote.append((1, region, region))
    return remote


def _ew(name, fn, ins, out_dtypes, rows_per_step=256):
    shape = ins[0].shape
    last = shape[-1]
    rows = 1
    for s in shape[:-1]:
        rows *= s
    ins2 = [a.reshape(rows, last) for a in ins]
    tr = _pick(rows, rows_per_step, 128, 64, 32, 16, 8)
    spec = pl.BlockSpec((tr, last), lambda i: (i, 0))

    def body(*refs):
        outs = fn(*[r[...] for r in refs[:len(ins)]])
        for o_ref, o in zip(refs[len(ins):], outs):
            o_ref[...] = o.astype(o_ref.dtype)

    outs = pl.pallas_call(
        body, name=name, grid=(rows // tr,), in_specs=[spec] * len(ins), out_specs=tuple([spec] * len(out_dtypes)),
        out_shape=tuple(jax.ShapeDtypeStruct((rows, last), d) for d in out_dtypes), compiler_params=_cp(1),
    )(*ins2)
    return [o.reshape(shape) for o in outs]


def _chip_sum(name, big, grad, from_sibling):
    if big.kind == "col":
        rh, w = big.half_shape
        tr = _pick(rh, 256, 128)
        nb = rh // tr
        grid, blk = (nb,), (tr, w)
        gmap, hmap = (lambda i: (_my_c() * nb + i, 0)), (lambda i: (i, 0))
    elif big.kind == "row":
        rh, w = big.half_shape[2:]
        tr = _pick(rh, 256, 128)
        grid, blk = (4, rh // tr), (1, 1, tr, w)
        gmap, hmap = (lambda k, i: (k, _my_c(), i, 0)), (lambda k, i: (k, 0, i, 0))
    else:
        grid, blk = (1,), big.half_shape
        gmap, hmap = (lambda i: (_my_c(), 0, 0)), (lambda i: (0, 0, 0))

    def body(g_ref, s_ref, o_ref):
        o_ref[...] = (g_ref[...].astype(F32) + s_ref[...].astype(F32)).astype(BF16)

    return pl.pallas_call(
        body, name=name, grid=grid, in_specs=[pl.BlockSpec(blk, gmap), pl.BlockSpec(blk, hmap)],
        out_specs=pl.BlockSpec(blk, hmap), out_shape=jax.ShapeDtypeStruct(big.half_shape, BF16), compiler_params=_cp(len(grid)),
    )(grad, from_sibling)


def _piece_sum(name, big, chip_sum, thirds):
    if big.kind == "col":
        rp, cs = big.piece_shape
        tr = _pick(rp, 256, 128)
        nb = rp // tr
        grid, blk, tblk = (nb,), (tr, cs), (1, tr, cs)
        smap, omap = (lambda i: (i, _my_chip())), (lambda i: (_my_c() * nb + i, 0))
        tmap = lambda j: (lambda i: (j, i, 0))
        out_shape = big.shard_shape
    elif big.kind == "row":
        rp, w = big.piece_shape[2:]
        tr = _pick(rp, 256, 128)
        grid, blk, tblk = (rp // tr,), (1, 1, tr, w), (1, 1, 1, tr, w)
        smap, omap = (lambda i: (_my_chip(), 0, i, 0)), (lambda i: (0, _my_c(), i, 0))
        tmap = lambda j: (lambda i: (j, 0, 0, i, 0))
        out_shape = (1, 2, rp, w)
    else:
        grid, blk, tblk = (1,), big.piece_shape, (1,) + big.piece_shape
        smap, omap = (lambda i: (0, _my_chip(), 0)), (lambda i: (_my_c(), 0, 0))
        tmap = lambda j: (lambda i: (j, 0, 0, 0))
        out_shape = big.shard_shape

    def body(s_ref, t0, t1, t2, o_ref):
        o_ref[...] = s_ref[...].astype(F32) + t0[0].astype(F32) + t1[0].astype(F32) + t2[0].astype(F32)

    return pl.pallas_call(
        body, name=name, grid=grid,
        in_specs=[pl.BlockSpec(blk, smap)] + [pl.BlockSpec(tblk, tmap(j)) for j in range(3)],
        out_specs=pl.BlockSpec(blk, omap), out_shape=jax.ShapeDtypeStruct(out_shape, F32), compiler_params=_cp(len(grid)),
    )(chip_sum, thirds, thirds, thirds)


def _split(name, bufs, remote):
    return _exchange_start(name + "_start", bufs, remote) + (remote, name)


def _join(handle, after):
    bufs, send_sems, recv_sems, _, remote, name = handle
    return _exchange_wait(name + "_wait", bufs, send_sems, recv_sems, remote, after)


def _reduce_d2d_remote(bigs):
    n = len(bigs)
    return [(1, lambda io, p, a=a, b=b: b.half_of_full(io[a], 1 - p[2]), lambda io, p, a=a: io[n + a])
            for a, b in enumerate(bigs)]


def _halves(bigs):
    return [jax.ShapeDtypeStruct(b.half_shape, BF16) for b in bigs]


def _chip_sums(tag, bigs, grads, from_sibling):
    return [_chip_sum(f"reduce_{tag}_chip_sum_{a}", b, g, r) for a, (b, g, r) in enumerate(zip(bigs, grads, from_sibling))]


def _reduce_to_chip(tag, bigs, grads):
    from_sibling = _exchange(f"reduce_{tag}_d2d", grads, _halves(bigs), _reduce_d2d_remote(bigs))
    return _chip_sums(tag, bigs, grads, from_sibling)


def _reduce_ici_remote(bigs):
    n = len(bigs)
    remote = []
    for a, b in enumerate(bigs):
        for j, mask in enumerate(CHIP_MASKS):
            remote.append((mask,
                           lambda io, p, a=a, b=b, mask=mask: b.piece_of_half(io[a], _chip_index(_flip(p, mask))),
                           lambda io, p, a=a, j=j: io[n + a].at[j]))
    return remote


def _thirds(bigs):
    return [jax.ShapeDtypeStruct((3,) + b.piece_shape, BF16) for b in bigs]


def _piece_sums(tag, bigs, chip_sum, from_chips):
    return [_piece_sum(f"reduce_{tag}_sum_{a}", b, s, r) for a, (b, s, r) in enumerate(zip(bigs, chip_sum, from_chips))]


def _share_remote(bigs, off):
    remote = []
    for a, b in enumerate(bigs):
        def mine(io, p, a=a, b=b):
            return b.half_of_shard(io[off + a], p[2])
        remote.append((1, mine, mine))
    return remote


def _reduce_finish(tag, bigs, chip_sum, from_chips):
    n = len(bigs)
    placed = _piece_sums(tag, bigs, chip_sum, from_chips)
    out = _exchange(f"reduce_{tag}_share_d2d", placed, [jax.ShapeDtypeStruct(p.shape, F32) for p in placed],
                    _share_remote(bigs, n), aliases={a: a for a in range(n)})
    return [o.reshape(b.shard_shape) for o, b in zip(out, bigs)]


def _mm(name, a, b, *, nt, tm, tn, tk, epi, extras=(), extra_specs=(), out_shape, out_specs, after=None, vmem_mib=48):
    m, kdim = a.shape
    n = b.shape[0] if nt else b.shape[1]
    gm, gn, gk = m // tm, n // tn, kdim // tk
    a_spec = pl.BlockSpec((tm, tk), lambda j, i, k: (i, k))
    b_spec = pl.BlockSpec((tn, tk), lambda j, i, k: (j, k)) if nt else pl.BlockSpec((tk, tn), lambda j, i, k: (k, j))
    n_ex = len(extras)
    if after is not None:
        extras, extra_specs = tuple(extras) + (after,), list(extra_specs) + [ANY]

    def body(a_ref, b_ref, *rest):
        ex, outs, acc = rest[:n_ex], rest[len(extras):-1], rest[-1]
        dot = _dot_nt if nt else _dot
        if gk == 1:
            acc[...] = dot(a_ref[...], b_ref[...])
            epi(acc, ex, outs)
        else:
            k = pl.program_id(2)

            @pl.when(k == 0)
            def _():
                acc[...] = dot(a_ref[...], b_ref[...])

            @pl.when(k > 0)
            def _():
                acc[...] += dot(a_ref[...], b_ref[...])

            @pl.when(k == gk - 1)
            def _():
                epi(acc, ex, outs)

    return pl.pallas_call(
        body, name=name, grid=(gn, gm, gk), in_specs=[a_spec, b_spec, *extra_specs], out_specs=tuple(out_specs),
        out_shape=tuple(out_shape), scratch_shapes=[pltpu.VMEM((tm, tn), F32)], compiler_params=_cp(3, vmem_mib),
    )(a, b, *extras)


def _mm_tn(name, a, b, out_dtype, *, tmo, tn, tt, more=(), vmem_mib=56):
    t, m = a.shape
    n = b.shape[1]
    gt = t // tt

    def body(a_ref, b_ref, *rest):
        o_ref, acc = rest[-2:]
        k = pl.program_id(2)

        @pl.when(k == 0)
        def _():
            first = _dot_tn(a_ref[...], b_ref[...])
            acc[...] = first + _dot_tn(rest[0][...], rest[1][...]) if more else first

        @pl.when(k > 0)
        def _():
            acc[...] += _dot_tn(a_ref[...], b_ref[...])

        @pl.when(k == gt - 1)
        def _():
            o_ref[...] = acc[...].astype(o_ref.dtype)

    more_specs = [pl.BlockSpec((more[0].shape[0], tmo), lambda i, j, k: (0, i)),
                  pl.BlockSpec((more[1].shape[0], tn), lambda i, j, k: (0, j))] if more else []
    return pl.pallas_call(
        body, name=name, grid=(m // tmo, n // tn, gt),
        in_specs=[pl.BlockSpec((tt, tmo), lambda i, j, k: (k, i)), pl.BlockSpec((tt, tn), lambda i, j, k: (k, j))] + more_specs,
        out_specs=pl.BlockSpec((tmo, tn), lambda i, j, k: (i, j)), out_shape=jax.ShapeDtypeStruct((m, n), out_dtype),
        scratch_shapes=[pltpu.VMEM((tmo, tn), F32)], compiler_params=_cp(3, vmem_mib),
    )(a, b, *more)


def _row_spec(d):
    return pl.BlockSpec((1, d), lambda *_: (0, 0))


def _stat_spec(k, d):
    return pl.BlockSpec((k, 8, d), lambda *_: (0, 0, 0))


def _rope(z, cs, sn):
    first = (lax.broadcasted_iota(jnp.int32, (z.shape[0], 128), 1) % 32) < 16
    outs = []
    for j in range(z.shape[1] // 128):
        zc = z[:, 128 * j:128 * (j + 1)]
        partner = jnp.where(first, pltpu.roll(zc, 112, 1), pltpu.roll(zc, 16, 1))
        outs.append(zc * cs + partner * sn)
    return outs[0] if len(outs) == 1 else jnp.concatenate(outs, axis=1)


def _rope_tables(length, rotate):
    if not rotate:
        return jnp.ones((length, 128), F32), jnp.zeros((length, 128), F32)
    half = HEAD_DIM // 2
    inv_freq = ROPE_BASE ** (-jnp.arange(0, half, 2, dtype=F32) / half)
    t = jnp.arange(length)
    row = (t // GRID_W).astype(F32)
    col = (t % GRID_W).astype(F32)
    e = jnp.arange(128) % HEAD_DIM
    pos = jnp.where(e[None, :] < half, row[:, None], col[:, None])
    ang = pos * inv_freq[(e % half) % (half // 2)][None, :]
    first = ((e % half) < half // 2)[None, :]
    return jnp.cos(ang), jnp.where(first, -jnp.sin(ang), jnp.sin(ang))


def _mixer_in(name, x, nw, sh, sc, w_in, cos, sin, after):
    t, d = x.shape
    tm = _pick(t, 256, 128)
    n_in = w_in.shape[1]

    def body(x_ref, nw_ref, sh_ref, sc_ref, w_ref, cos_ref, sin_ref, after_ref, h_ref, q_ref, k_ref, v_ref, u_ref):
        xf = x_ref[...]
        r = lax.rsqrt(jnp.mean(xf * xf, axis=-1, keepdims=True) + EPS)
        hb = (((xf * r) * nw_ref[...]) * (1.0 + sc_ref[...]) + sh_ref[...]).astype(BF16)
        h_ref[...] = hb
        p = _dot(hb, w_ref[...])
        cs, sn = cos_ref[...], sin_ref[...]
        q_ref[...] = (_rope(p[:, :ATTN_WIDTH], cs, sn) * SCALE).astype(BF16)
        k_ref[...] = _rope(p[:, ATTN_WIDTH:ATTN_WIDTH + KV_WIDTH], cs, sn).astype(BF16)
        v_ref[...] = p[:, ATTN_WIDTH + KV_WIDTH:ATTN_WIDTH + 2 * KV_WIDTH].astype(BF16)
        u_ref[...] = p[:, ATTN_WIDTH + 2 * KV_WIDTH:]

    def tile(w):
        return pl.BlockSpec((tm, w), lambda i: (i, 0))

    return pl.pallas_call(
        body, name=name, grid=(t // tm,),
        in_specs=[tile(d), _row_spec(d), _row_spec(d), _row_spec(d), pl.BlockSpec((d, n_in), lambda i: (0, 0)),
                  tile(128), tile(128), ANY],
        out_specs=(tile(d), tile(ATTN_WIDTH), tile(KV_WIDTH), tile(KV_WIDTH), tile(POOL_WIDTH)),
        out_shape=(jax.ShapeDtypeStruct((t, d), BF16), jax.ShapeDtypeStruct((t, ATTN_WIDTH), BF16),
                   jax.ShapeDtypeStruct((t, KV_WIDTH), BF16), jax.ShapeDtypeStruct((t, KV_WIDTH), BF16),
                   jax.ShapeDtypeStruct((t, POOL_WIDTH), F32)),
        compiler_params=_cp(1),
    )(x, nw, sh, sc, w_in, cos, sin, after)


def _attn_specs(nb, n_ctx):
    def blk(w, f):
        return pl.BlockSpec((BLOCK, w), lambda n: (f(n), 0))

    prev = lambda n: jnp.maximum(n - 1, 0)
    cur = lambda n: n
    nxt = lambda n: jnp.minimum(n + 1, nb - 1)
    kv = [blk(KV_WIDTH, prev), blk(KV_WIDTH, cur), blk(KV_WIDTH, nxt)]
    ctx = pl.BlockSpec((n_ctx, KV_WIDTH), lambda n: (0, 0))
    return [pl.BlockSpec(memory_space=pltpu.SMEM), blk(ATTN_WIDTH, cur)] + kv + kv + [ctx, ctx]


def _attn_mask(n, length, n_keys):
    row = lax.broadcasted_iota(jnp.int32, (GROUP * BLOCK, n_keys), 0) % BLOCK
    col = lax.broadcasted_iota(jnp.int32, (GROUP * BLOCK, n_keys), 1)
    kpos = (n - 1) * BLOCK + col
    return ((jnp.abs(col - BLOCK - row) <= BLOCK) & (kpos >= 0) & (kpos < length)) | (col >= 3 * BLOCK)


def _group_rows(block, g):
    return jnp.concatenate([block[:, HEAD_DIM * h:HEAD_DIM * (h + 1)] for h in range(GROUP * g, GROUP * (g + 1))], axis=0)


def _group_sink(sink_ref, g):
    head = lax.broadcasted_iota(jnp.int32, (GROUP * BLOCK, 1), 0) // BLOCK
    out = jnp.full((GROUP * BLOCK, 1), sink_ref[0, GROUP * g], F32)
    for j in range(1, GROUP):
        out = jnp.where(head == j, sink_ref[0, GROUP * g + j], out)
    return out


def _attn_fwd(q, k, v, kc, vc, sink):
    length = q.shape[0]
    nb = length // BLOCK
    n_ctx = kc.shape[0]
    n_keys = 3 * BLOCK + n_ctx

    def body(sink_ref, q_ref, kp, k0, kn, vp, v0, vn, kc_ref, vc_ref, o_ref, p_ref):
        n = pl.program_id(0)
        valid = _attn_mask(n, length, n_keys)
        qb = q_ref[...]
        kall = jnp.concatenate([kp[...], k0[...], kn[...], kc_ref[...]], axis=0)
        vall = jnp.concatenate([vp[...], v0[...], vn[...], vc_ref[...]], axis=0)
        outs = []
        for g in range(N_KV_HEADS):
            lanes = slice(HEAD_DIM * g, HEAD_DIM * (g + 1))
            s = jnp.where(valid, _dot_nt(_group_rows(qb, g), kall[:, lanes]), NEG_INF)
            sk = _group_sink(sink_ref, g)
            m = jnp.maximum(jnp.max(s, axis=-1, keepdims=True), sk)
            e = jnp.exp(s - m)
            e_sink = jnp.exp(sk - m)
            inv = 1.0 / (jnp.sum(e, axis=-1, keepdims=True) + e_sink)
            pb = (e * inv).astype(BF16)
            p_ref[0, g, :, :n_keys] = pb
            p_ref[0, g, :, n_keys:] = jnp.broadcast_to(e_sink * inv, (GROUP * BLOCK, 128)).astype(BF16)
            o = _dot(pb, vall[:, lanes])
            outs += [o[BLOCK * j:BLOCK * (j + 1)] for j in range(GROUP)]
        o_ref[...] = jnp.concatenate(outs, axis=1).astype(BF16)

    return pl.pallas_call(
        body, name="attn_fwd", grid=(nb,), in_specs=_attn_specs(nb, n_ctx),
        out_specs=(pl.BlockSpec((BLOCK, ATTN_WIDTH), lambda n: (n, 0)),
                   pl.BlockSpec((1, N_KV_HEADS, GROUP * BLOCK, n_keys + 128), lambda n: (n, 0, 0, 0))),
        out_shape=(jax.ShapeDtypeStruct((length, ATTN_WIDTH + POOL_WIDTH), BF16),
                   jax.ShapeDtypeStruct((nb, N_KV_HEADS, GROUP * BLOCK, n_keys + 128), BF16)), compiler_params=_cp(1),
    )(sink, q, k, k, k, v, v, v, kc, vc)


def _attn_bwd(q, k, v, kc, vc, sink, dmix, probs):
    length = q.shape[0]
    nb = length // BLOCK
    n_ctx = kc.shape[0]
    n_keys = 3 * BLOCK + n_ctx

    def body(sink_ref, q_ref, kp, k0, kn, vp, v0, vn, kc_ref, vc_ref, do_ref, p_ref,
             dq_ref, dkp_ref, dvp_ref, dkc_ref, dvc_ref, dsink_ref):
        n = pl.program_id(0)

        @pl.when(n == 0)
        def _():
            dkc_ref[...] = jnp.zeros_like(dkc_ref)
            dvc_ref[...] = jnp.zeros_like(dvc_ref)
            dsink_ref[...] = jnp.zeros_like(dsink_ref)

        qb, dob = q_ref[...], do_ref[...]
        kall = jnp.concatenate([kp[...], k0[...], kn[...], kc_ref[...]], axis=0)
        vall = jnp.concatenate([vp[...], v0[...], vn[...], vc_ref[...]], axis=0)
        srow = lax.broadcasted_iota(jnp.int32, (8, 128), 0)
        slane = lax.broadcasted_iota(jnp.int32, (8, 128), 1)
        dqs, dks, dvs = [], [], []
        dsink = jnp.zeros((8, 128), F32)
        for g in range(N_KV_HEADS):
            lanes = slice(HEAD_DIM * g, HEAD_DIM * (g + 1))
            kg, vg = kall[:, lanes], vall[:, lanes]
            qg, dog = _group_rows(qb, g), _group_rows(dob, g)
            pb = p_ref[0, g, :, :n_keys]
            p = pb.astype(F32)
            dp = _dot_nt(dog, vg)
            delta = jnp.sum(p * dp, axis=-1, keepdims=True)
            ds = (p * (dp - delta)).astype(BF16)
            dq = _dot(ds, kg) * SCALE
            dqs += [dq[BLOCK * j:BLOCK * (j + 1)] for j in range(GROUP)]
            dks.append(_dot_tn(ds, qg))
            dvs.append(_dot_tn(pb, dog))
            d_sink = p_ref[0, g, :, n_keys:].astype(F32)[:, :1] * delta
            for j in range(GROUP):
                total = -jnp.sum(d_sink[BLOCK * j:BLOCK * (j + 1)], axis=0, keepdims=True)
                dsink = dsink + jnp.where((srow == 0) & (slane == GROUP * g + j), total, 0.0)
        dq_ref[...] = jnp.concatenate(dqs, axis=1)
        dk = jnp.concatenate(dks, axis=1)
        dv = jnp.concatenate(dvs, axis=1)
        for j in range(3):
            dkp_ref[0, j] = dk[BLOCK * j:BLOCK * (j + 1)]
            dvp_ref[0, j] = dv[BLOCK * j:BLOCK * (j + 1)]
        dkc_ref[...] += dk[3 * BLOCK:]
        dvc_ref[...] += dv[3 * BLOCK:]
        dsink_ref[...] += dsink

    part = pl.BlockSpec((1, 3, BLOCK, KV_WIDTH), lambda n: (n, 0, 0, 0))
    ctx = pl.BlockSpec((n_ctx, KV_WIDTH), lambda n: (0, 0))
    return pl.pallas_call(
        body, name="attn_bwd", grid=(nb,),
        in_specs=_attn_specs(nb, n_ctx) + [pl.BlockSpec((BLOCK, ATTN_WIDTH), lambda n: (n, 0)),
                                           pl.BlockSpec((1,) + probs.shape[1:], lambda n: (n, 0, 0, 0))],
        out_specs=(pl.BlockSpec((BLOCK, ATTN_WIDTH), lambda n: (n, 0)), part, part, ctx, ctx,
                   pl.BlockSpec((8, 128), lambda n: (0, 0))),
        out_shape=(jax.ShapeDtypeStruct((length, ATTN_WIDTH), F32),
                   jax.ShapeDtypeStruct((nb, 3, BLOCK, KV_WIDTH), F32), jax.ShapeDtypeStruct((nb, 3, BLOCK, KV_WIDTH), F32),
                   jax.ShapeDtypeStruct((n_ctx, KV_WIDTH), F32), jax.ShapeDtypeStruct((n_ctx, KV_WIDTH), F32),
                   jax.ShapeDtypeStruct((8, 128), F32)),
        compiler_params=_cp(1),
    )(sink, q, k, k, k, v, v, v, kc, vc, dmix, probs)


def _assemble_dp(dq, dkp, dvp, du, cos, sin, after):
    length = dq.shape[0]
    nb = length // BLOCK

    def body(dq_ref, dka, dkb, dkc, dva, dvb, dvc, du_ref, cos_ref, sin_ref, after_ref, o_ref):
        n = pl.program_id(0)
        has_next = (n + 1 < nb).astype(F32)
        has_prev = (n > 0).astype(F32)
        cs, sn = cos_ref[...], -sin_ref[...]
        dk = dka[0, 0] * has_next + dkb[0, 0] + dkc[0, 0] * has_prev
        dv = dva[0, 0] * has_next + dvb[0, 0] + dvc[0, 0] * has_prev
        o_ref[:, :ATTN_WIDTH] = _rope(dq_ref[...], cs, sn).astype(BF16)
        o_ref[:, ATTN_WIDTH:ATTN_WIDTH + KV_WIDTH] = _rope(dk, cs, sn).astype(BF16)
        o_ref[:, ATTN_WIDTH + KV_WIDTH:ATTN_WIDTH + 2 * KV_WIDTH] = dv.astype(BF16)
        o_ref[:, ATTN_WIDTH + 2 * KV_WIDTH:] = du_ref[...]

    def part(slot, f):
        return pl.BlockSpec((1, 1, BLOCK, KV_WIDTH), lambda n: (f(n), slot, 0, 0))

    parts = [part(0, lambda n: jnp.minimum(n + 1, nb - 1)), part(1, lambda n: n), part(2, lambda n: jnp.maximum(n - 1, 0))]

    def tile(w):
        return pl.BlockSpec((BLOCK, w), lambda n: (n, 0))

    width = ATTN_WIDTH + 2 * KV_WIDTH + POOL_WIDTH
    return pl.pallas_call(
        body, name="assemble_dp", grid=(nb,),
        in_specs=[tile(ATTN_WIDTH)] + parts + parts + [tile(POOL_WIDTH), tile(128), tile(128), ANY],
        out_specs=tile(width), out_shape=jax.ShapeDtypeStruct((length, width), BF16), compiler_params=_cp(1),
    )(dq, dkp, dkp, dkp, dvp, dvp, dvp, du, cos, sin, after)


def _shift_rows(e, s):
    n = e.shape[0]
    return e if s % n == 0 else pltpu.roll(e, (-s) % n, 0)


def _window_sum(e, w, first):
    s, n = e, 1
    while n < w:
        s = s + _shift_rows(s, n)
        n *= 2
    return _shift_rows(s, first)


def _pool_geometry(i, tm, length):
    pos = i * tm - HALO + lax.broadcasted_iota(jnp.int32, (tm + 2 * HALO, 1), 0)
    inside = (pos >= 0) & (pos < length)
    inv_counts = []
    for w in POOL_WINDOWS:
        lo = jnp.clip(pos - w // 2, 0, length)
        hi = jnp.clip(pos - w // 2 + w, 0, length)
        inv_counts.append(1.0 / jnp.maximum(hi - lo, 1).astype(F32))
    return inside, inv_counts


def _halo_specs(tm, width, length, col=0):
    per = tm // HALO
    last = length // HALO - 1
    return [pl.BlockSpec((HALO, width), lambda i: (jnp.maximum(i * per - 1, 0), col)),
            pl.BlockSpec((tm, width), lambda i: (i, col)),
            pl.BlockSpec((HALO, width), lambda i: (jnp.minimum((i + 1) * per, last), col))]


def _pooled(ext, inv_counts, tm):
    outs = []
    for g, w in enumerate(POOL_WINDOWS):
        e = ext[:, POOL_GROUP_DIM * g:POOL_GROUP_DIM * (g + 1)]
        mean = _window_sum(e, w, -(w // 2)) * inv_counts[g]
        outs.append((mean - e)[HALO:HALO + tm])
    return outs


def _pool_fwd(u, pool_w, pool_scale, mix):
    length = u.shape[0]
    tm = _pick(length, 256, 128)

    def body(up, u0, un, w_ref, sc_ref, mix_ref, o_ref):
        inside, inv_counts = _pool_geometry(pl.program_id(0), tm, length)
        ext = jnp.where(inside, jnp.concatenate([up[...], u0[...], un[...]], axis=0), 0.0)
        pooled = _pooled(ext, inv_counts, tm)
        mixed = [_dot(pooled[g].astype(BF16), w_ref[g]) for g in range(len(POOL_WINDOWS))]
        o_ref[...] = (jnp.concatenate(mixed, axis=1) * sc_ref[...]).astype(BF16)

    return pl.pallas_call(
        body, name="pool_fwd", grid=(length // tm,),
        in_specs=_halo_specs(tm, POOL_WIDTH, length) + [pl.BlockSpec(pool_w.shape, lambda i: (0, 0, 0)), _row_spec(POOL_WIDTH), ANY],
        out_specs=pl.BlockSpec((tm, POOL_WIDTH), lambda i: (i, 1)),
        out_shape=jax.ShapeDtypeStruct(mix.shape, BF16), input_output_aliases={5: 0}, compiler_params=_cp(1),
    )(u, u, u, pool_w, pool_scale, mix)


def _pool_bwd(u, dmix, pool_w, pool_scale, after):
    length = u.shape[0]
    tm = _pick(length, 256, 128)
    n_g = len(POOL_WINDOWS)

    def body(up, u0, un, dp_, d0, dn_, w_ref, sc_ref, after_ref, du_ref, dw_ref, dsc_ref):
        i = pl.program_id(0)

        @pl.when(i == 0)
        def _():
            dw_ref[...] = jnp.zeros_like(dw_ref)
            dsc_ref[...] = jnp.zeros_like(dsc_ref)

        inside, inv_counts = _pool_geometry(i, tm, length)
        ext = jnp.where(inside, jnp.concatenate([up[...], u0[...], un[...]], axis=0), 0.0)
        dext = jnp.where(inside, jnp.concatenate([dp_[...], d0[...], dn_[...]], axis=0).astype(F32), 0.0)
        dmixed = (dext * sc_ref[...]).astype(BF16)
        pooled = _pooled(ext, inv_counts, tm)
        dus, dscs = [], []
        for g, w in enumerate(POOL_WINDOWS):
            lanes = slice(POOL_GROUP_DIM * g, POOL_GROUP_DIM * (g + 1))
            dpooled = _dot_nt(dmixed[:, lanes], w_ref[g])
            spread = _window_sum(dpooled * inv_counts[g], w, -(w // 2 - 1))
            dus.append((spread - dpooled)[HALO:HALO + tm])
            pb = pooled[g].astype(BF16)
            dw_ref[g] += _dot_tn(pb, dmixed[HALO:HALO + tm, lanes])
            prod = dext[HALO:HALO + tm, lanes] * _dot(pb, w_ref[g])
            dscs.append(_fold8(prod))
        du_ref[...] = jnp.concatenate(dus, axis=1).astype(BF16)
        dsc_ref[...] += jnp.concatenate(dscs, axis=1)

    return pl.pallas_call(
        body, name="pool_bwd", grid=(length // tm,),
        in_specs=_halo_specs(tm, POOL_WIDTH, length) + _halo_specs(tm, POOL_WIDTH, length, col=1)
        + [pl.BlockSpec(pool_w.shape, lambda i: (0, 0, 0)), _row_spec(POOL_WIDTH), ANY],
        out_specs=(pl.BlockSpec((tm, POOL_WIDTH), lambda i: (i, 0)), pl.BlockSpec((n_g, POOL_GROUP_DIM, POOL_GROUP_DIM), lambda i: (0, 0, 0)),
                   pl.BlockSpec((8, POOL_WIDTH), lambda i: (0, 0))),
        out_shape=(jax.ShapeDtypeStruct((length, POOL_WIDTH), BF16), jax.ShapeDtypeStruct((n_g, POOL_GROUP_DIM, POOL_GROUP_DIM), F32),
                   jax.ShapeDtypeStruct((8, POOL_WIDTH), F32)),
        compiler_params=_cp(1),
    )(u, u, u, dmix, dmix, dmix, pool_w, pool_scale, after)


def _mixer_out(mix, w_out, x, g_a, nmw, sh_m, sc_m, after):
    t, d = x.shape
    tm = _pick(t, 256, 128)

    def epi(acc, ex, outs):
        x_ref, ga, nw, sh, sc = ex
        x1_ref, mo_ref, hm_ref = outs

        def rows(rs):
            mo = acc[rs, :]
            x1 = x_ref[rs, :] + ga[...] * mo
            x1_ref[rs, :] = x1
            mo_ref[rs, :] = mo.astype(BF16)
            r = lax.rsqrt(jnp.mean(x1 * x1, axis=-1, keepdims=True) + EPS)
            hm_ref[rs, :] = (((x1 * r) * nw[...]) * (1.0 + sc[...]) + sh[...]).astype(BF16)

        _row_loop(tm, rows)

    tile = pl.BlockSpec((tm, d), lambda j, i, k: (i, 0))
    return _mm("mixer_out", mix, w_out, nt=False, tm=tm, tn=d, tk=mix.shape[1], epi=epi, after=after,
               extras=(x, g_a, nmw, sh_m, sc_m), extra_specs=[tile] + [_row_spec(d)] * 4,
               out_shape=(jax.ShapeDtypeStruct((t, d), F32), jax.ShapeDtypeStruct((t, d), BF16), jax.ShapeDtypeStruct((t, d), BF16)),
               out_specs=(tile, tile, tile))


def _mlp_up(hm, w_up):
    t, d = hm.shape
    tm = _pick(t, 1024, 512, 256, 128)
    tn = 2048

    def epi(acc, ex, outs):
        outs[0][...] = jnp.square(jnp.maximum(acc[...], 0.0)).astype(BF16)

    return _mm("mlp_up", hm, w_up, nt=False, tm=tm, tn=tn, tk=d, epi=epi,
               out_shape=(jax.ShapeDtypeStruct((t, w_up.shape[1]), BF16),),
               out_specs=(pl.BlockSpec((tm, tn), lambda j, i, k: (i, j)),))[0]


def _mm_f32(name, a, b, *, nt, after):
    m, kdim = a.shape
    n = b.shape[0] if nt else b.shape[1]
    tm, tn = _pick(m, 1024, 512, 256, 128), _pick(n, 1024)

    def epi(acc, ex, outs):
        outs[0][...] = acc[...]

    return _mm(name, a, b, nt=nt, tm=tm, tn=tn, tk=_pick(kdim, 2048), epi=epi, after=after,
               out_shape=(jax.ShapeDtypeStruct((m, n), F32),), out_specs=(pl.BlockSpec((tm, tn), lambda j, i, k: (i, j)),))[0]


def _rows_call(name, rows_fn, tiles, vecs, out_shape, n_stats, after):
    t, d = tiles[0].shape
    tm = _pick(t, 256, 128)
    n_t, n_v = len(tiles), len(vecs)

    def body(*refs):
        st_ref = refs[-1]

        @pl.when(pl.program_id(0) == 0)
        def _():
            st_ref[...] = jnp.zeros_like(st_ref)

        _row_loop(tm, lambda rs: rows_fn(rs, refs[:n_t], refs[n_t:n_t + n_v], refs[n_t + n_v + 1:-1], st_ref))

    tile = pl.BlockSpec((tm, d), lambda i: (i, 0))
    return pl.pallas_call(
        body, name=name, grid=(t // tm,), in_specs=[tile] * n_t + [_row_spec(d)] * n_v + [ANY],
        out_specs=tuple([tile] * len(out_shape)) + (_stat_spec(n_stats, d),),
        out_shape=tuple(out_shape) + (jax.ShapeDtypeStruct((n_stats, 8, d), F32),), compiler_params=_cp(1),
    )(*tiles, *vecs, after)


def _loss_rows(dn, x1, target, g_m, fw):
    t, d = x1.shape

    def rows_fn(rs, tiles, vecs, outs, st_ref):
        dn_ref, x1_ref, t_ref = tiles
        gm, fw_ref = vecs
        dx2_ref, ddn_ref = outs
        dnv = dn_ref[rs, :]
        x2 = x1_ref[rs, :] + gm[...] * dnv
        r = lax.rsqrt(jnp.mean(x2 * x2, axis=-1, keepdims=True) + EPS)
        xh = x2 * r
        diff = xh * fw_ref[...] - t_ref[rs, :]
        dy = diff * (1.0 / d)
        dxh = dy * fw_ref[...]
        dx2 = r * (dxh - xh * jnp.mean(dxh * xh, axis=-1, keepdims=True))
        dx2_ref[rs, :] = dx2
        ddn_ref[rs, :] = (dx2 * gm[...]).astype(BF16)
        st_ref[0] += _fold8(diff * diff)
        st_ref[1] += _fold8(dy * xh)
        st_ref[2] += _fold8(dx2 * dnv)

    return _rows_call("loss_rows", rows_fn, (dn, x1, target), (g_m, fw),
                      (jax.ShapeDtypeStruct((t, d), F32), jax.ShapeDtypeStruct((t, d), BF16)), 3, g_m)


def _mlp_dx_rows(dhm, x1, dx2, mo, nmw, sc_m, g_a, after):
    t, d = x1.shape

    def rows_fn(rs, tiles, vecs, outs, st_ref):
        dh_ref, x1_ref, dx2_ref, mo_ref = tiles
        nw, sc, ga = vecs
        dx1_ref, dmi_ref = outs
        dx1 = _norm_bwd_rows(dh_ref[rs, :], x1_ref[rs, :], nw[...], sc[...], st_ref) + dx2_ref[rs, :]
        dx1_ref[rs, :] = dx1
        dmi_ref[rs, :] = (dx1 * ga[...]).astype(BF16)
        st_ref[3] += _fold8(dx1 * mo_ref[rs, :].astype(F32))

    return _rows_call("mlp_dx_rows", rows_fn, (dhm, x1, dx2, mo), (nmw, sc_m, g_a),
                      (jax.ShapeDtypeStruct((t, d), F32), jax.ShapeDtypeStruct((t, d), BF16)), 4, after)


def _mlp_dact(ddn, w_down, act):
    t, d = ddn.shape
    tm = _pick(t, 512, 256, 128)
    tn = 2048

    def epi(acc, ex, outs):
        outs[0][...] = (acc[...] * (2.0 * jnp.sqrt(ex[0][...]).astype(F32))).astype(BF16)

    tile = pl.BlockSpec((tm, tn), lambda j, i, k: (i, j))
    return _mm("mlp_dact", ddn, w_down, nt=True, tm=tm, tn=tn, tk=d, epi=epi, extras=(act,), extra_specs=[tile],
               out_shape=(jax.ShapeDtypeStruct(act.shape, BF16),), out_specs=(tile,))[0]


def _norm_bwd_rows(dh, xv, nw, sc, st_ref):
    r = lax.rsqrt(jnp.mean(xv * xv, axis=-1, keepdims=True) + EPS)
    xh = xv * r
    dy = dh * (1.0 + sc)
    st_ref[0] += _fold8(dh)
    st_ref[1] += _fold8(dh * (xh * nw))
    st_ref[2] += _fold8(dy * xh)
    dxh = dy * nw
    return r * (dxh - xh * jnp.mean(dxh * xh, axis=-1, keepdims=True))


def _mixer_dmix(dmi, w_out):
    t, d = dmi.shape
    tm = _pick(t, 512, 256, 128)

    def epi(acc, ex, outs):
        outs[0][...] = acc[...].astype(BF16)

    n = w_out.shape[0]
    return _mm("mixer_dmix", dmi, w_out, nt=True, tm=tm, tn=n, tk=d, epi=epi,
               out_shape=(jax.ShapeDtypeStruct((t, n), BF16),), out_specs=(pl.BlockSpec((tm, n), lambda j, i, k: (i, 0)),))[0]


def _mixer_dx(name, dp, w_in, x, dx1, naw, sc_a, after):
    t, d = x.shape
    tm = _pick(t, 256, 128)

    def epi(acc, ex, outs):
        x_ref, dx1_ref, nw, sc = ex
        gx_ref, st_ref = outs

        @pl.when(pl.program_id(1) == 0)
        def _():
            st_ref[...] = jnp.zeros_like(st_ref)

        def rows(rs):
            gx_ref[rs, :] = _norm_bwd_rows(acc[rs, :], x_ref[rs, :], nw[...], sc[...], st_ref) + dx1_ref[rs, :]

        _row_loop(tm, rows)

    tile = pl.BlockSpec((tm, d), lambda j, i, k: (i, 0))
    return _mm(name, dp, w_in, nt=True, tm=tm, tn=d, tk=dp.shape[1], epi=epi, after=after,
               extras=(x, dx1, naw, sc_a), extra_specs=[tile, tile, _row_spec(d), _row_spec(d)],
               out_shape=(jax.ShapeDtypeStruct((t, d), F32), jax.ShapeDtypeStruct((3, 8, d), F32)),
               out_specs=(tile, _stat_spec(3, d)))


def _silu(v):
    return v / (1.0 + jnp.exp(-v))


def _ada_fwd(cond, w_ada, b_ada):
    d, n = w_ada.shape
    tn = 512

    def body(c_ref, w_ref, b_ref, o_ref):
        o_ref[...] = _dot(_silu(c_ref[...]).astype(BF16), w_ref[...].astype(BF16)) + b_ref[...]

    return pl.pallas_call(
        body, name="ada_fwd", grid=(n // tn,),
        in_specs=[pl.BlockSpec(cond.shape, lambda j: (0, 0)), pl.BlockSpec((d, tn), lambda j: (0, j)), pl.BlockSpec((1, tn), lambda j: (0, j))],
        out_specs=pl.BlockSpec((cond.shape[0], tn), lambda j: (0, j)), out_shape=jax.ShapeDtypeStruct((cond.shape[0], n), F32),
        compiler_params=_cp(1),
    )(cond, w_ada, b_ada)


def _adamw_math(w, g, m, v):
    m = ADAM_B1 * m + (1.0 - ADAM_B1) * g
    v = ADAM_B2 * v + (1.0 - ADAM_B2) * jnp.square(g)
    m_hat = m / (1.0 - ADAM_B1 ** ADAM_STEP)
    v_hat = v / (1.0 - ADAM_B2 ** ADAM_STEP)
    return -ADAM_LR * (m_hat / (jnp.sqrt(v_hat) + ADAM_EPS) + ADAM_WD * w), m, v


def _ada_bwd(cond, dm, w_ada, m_ada, v_ada):
    d, n = w_ada.shape
    tn = 256
    rows = cond.shape[0]

    def body(c_ref, dm_ref, w_ref, m_ref, v_ref, g_ref, dl_ref, nm_ref, nv_ref, pc_ref):
        @pl.when(pl.program_id(0) == 0)
        def _():
            pc_ref[...] = jnp.zeros_like(pc_ref)

        dmb = dm_ref[...].astype(BF16)
        w = w_ref[...]
        g = _dot_tn(_silu(c_ref[...]).astype(BF16), dmb)
        g_ref[...] = g
        dl_ref[...], nm_ref[...], nv_ref[...] = _adamw_math(w, g, m_ref[...], v_ref[...])
        pc_ref[...] += _dot_nt(dm_ref[8:16, :].astype(BF16), w.astype(BF16))

    tile = pl.BlockSpec((d, tn), lambda j: (0, j))
    like = jax.ShapeDtypeStruct((d, n), F32)
    return pl.pallas_call(
        body, name="ada_bwd", grid=(n // tn,),
        in_specs=[pl.BlockSpec((rows, d), lambda j: (0, 0)), pl.BlockSpec((rows, tn), lambda j: (0, j)), tile, tile, tile],
        out_specs=(tile, tile, tile, tile, pl.BlockSpec((8, d), lambda j: (0, 0))),
        out_shape=(like, like, like, like, jax.ShapeDtypeStruct((8, d), F32)), compiler_params=_cp(1),
    )(cond, dm, w_ada, m_ada, v_ada)


def _adamw(name, w, g, m, v):
    return _ew(name, lambda w_, g_, m_, v_: (g_,) + _adamw_math(w_, g_, m_, v_), [w, g, m, v], [F32, F32, F32, F32])


def _colsum(st):
    return jnp.sum(st, axis=1)


def kernel(x, c, ctx, c_ctx, norm_attn_w, norm_mlp_w, w_ada, b_ada, w_in, attn_sink, pool_w, pool_scale, w_out, w_mlp_up, w_mlp_down, final_norm_w, loss_target, m_c_ctx, m_norm_attn_w, m_norm_mlp_w, m_w_ada, m_b_ada, m_w_in, m_attn_sink, m_pool_w, m_pool_scale, m_w_out, m_w_mlp_up, m_w_mlp_down, m_final_norm_w, v_c_ctx, v_norm_attn_w, v_norm_mlp_w, v_w_ada, v_b_ada, v_w_in, v_attn_sink, v_pool_w, v_pool_scale, v_w_out, v_w_mlp_up, v_w_mlp_down, v_final_norm_w):
    length, d = x.shape[1], x.shape[2]
    n_ctx = ctx.shape[1]
    pos = (lax.axis_index("x"), lax.axis_index("y"), lax.axis_index("c"))
    me, chip = _dev_index(pos), _chip_index(pos)
    xs, tgt, cx = x.reshape(length, d), loss_target.reshape(length, d), ctx.reshape(n_ctx, d)
    n_ada = w_ada.shape[2]

    mixer_bigs = [_Big("col", w_in.shape[1:]), _Big("pool", pool_w.shape[1:]), _Big("row", w_out.shape[1:])]
    mlp_bigs = [_Big("col", w_mlp_up.shape[1:]), _Big("row", w_mlp_down.shape[1:])]
    placed = [_cast_place(f"place_{i}", b, s, c) for i, (b, s) in enumerate(zip(mixer_bigs, [w_in[0], pool_w[0], w_out[0]]))]
    flight = _split("gather_mixer_ici", placed, _gather_ici_remote(mixer_bigs, 0))

    c_all = _allgather8("gather_c", jnp.pad(c, ((0, 7), (0, 0))) + flight[3][0, 0])
    cond = jnp.concatenate([c_all[:, 0, :], jnp.pad(c_ctx[None, :], ((0, 7), (0, 0)))], axis=0)
    b_shard = lax.dynamic_slice_in_dim(b_ada, chip * n_ada, n_ada, axis=1)
    mod_all = _allgather8("gather_mod", _ada_fwd(cond, w_ada[0], b_shard))
    mod = jnp.concatenate([mod_all[0], mod_all[2], mod_all[4], mod_all[6]], axis=1)
    mine = lax.dynamic_slice_in_dim(mod, me, 1, axis=0)
    sh_a, sc_a, g_a, sh_m, sc_m, g_m = [mine[:, d * i:d * (i + 1)] for i in range(6)]
    csh_a, csc_a = mod[8:9, :d], mod[8:9, d:2 * d]

    win_b, pw_b, wout_b = _exchange("gather_mixer_d2d", _join(flight, mod), [jax.ShapeDtypeStruct(b.full_shape, BF16) for b in mixer_bigs],
                                    _gather_d2d_remote(mixer_bigs, 3), aliases={0: 0, 1: 1, 2: 2})
    wout_b = wout_b.reshape(-1, d)
    placed = [_cast_place(f"place_mlp_{i}", b, s, pw_b) for i, (b, s) in enumerate(zip(mlp_bigs, [w_mlp_up[0], w_mlp_down[0]]))]
    flight = _split("gather_mlp_ici", placed, _gather_ici_remote(mlp_bigs, 0))

    cos, sin = _rope_tables(length, True)
    one, zero = _rope_tables(n_ctx, False)
    h, q, k, v, u = _mixer_in("mixer_in", xs, norm_attn_w, sh_a, sc_a, win_b, cos, sin, flight[3])
    hc, _, kc, vc, _ = _mixer_in("mixer_in_ctx", cx, norm_attn_w, csh_a, csc_a, win_b, one, zero, flight[3])
    attn, probs = _attn_fwd(q, k, v, kc, vc, attn_sink)
    mix = _pool_fwd(u, pw_b, pool_scale, attn)
    flight = _split("gather_mlp_d2d", _join(flight, mix), _gather_d2d_remote(mlp_bigs, 0))
    x1, mo, hm = _mixer_out(mix, wout_b, xs, g_a, norm_mlp_w, sh_m, sc_m, flight[3])
    wup_b, wdn_b = _join(flight, hm)
    wdn_b = wdn_b.reshape(-1, d)
    act = _mlp_up(hm, wup_b)
    dn = _mm_f32("mlp_down", act, wdn_b, nt=False, after=c)
    dx2, ddn, st_loss = _loss_rows(dn, x1, tgt, g_m, final_norm_w[None, :])
    st_loss = _colsum(st_loss)
    loss = lax.psum(0.5 / d * jnp.sum(st_loss[0]), ("x", "y", "c"))

    tt = _pick(length, 2048, 1024, 512, 256, 128)
    g_wdn = _mm_tn("grad_w_down", act, ddn, BF16, tmo=1024, tn=d, tt=tt)
    dup = _mlp_dact(ddn, wdn_b, act)
    g_wup = _mm_tn("grad_w_up", hm, dup, BF16, tmo=d, tn=1024, tt=tt)
    empty = lambda shapes: [lax.empty(s.shape, s.dtype) for s in shapes]
    grads = [g_wup, g_wdn.reshape(mlp_bigs[1].full_shape)]
    flight = _split("reduce_mlp_d2d", grads + empty(_halves(mlp_bigs)), _reduce_d2d_remote(mlp_bigs))
    dhm = _mm_f32("mlp_dhm", dup, wup_b, nt=True, after=flight[3])
    landed = _join(flight, dhm)
    mlp_chip = _chip_sums("mlp", mlp_bigs, landed[:2], landed[2:])
    flight = _split("reduce_mlp_ici", mlp_chip + empty(_thirds(mlp_bigs)), _reduce_ici_remote(mlp_bigs))
    dx1, dmi, st_mlp = _mlp_dx_rows(dhm, x1, dx2, mo, norm_mlp_w, sc_m, g_a, flight[3])
    st_mlp = _colsum(st_mlp)
    g_wout = _mm_tn("grad_w_out", mix, dmi, BF16, tmo=1024, tn=d, tt=tt)
    dmix = _mixer_dmix(dmi, wout_b)
    dq, dkp, dvp, dkc, dvc, dsink = _attn_bwd(q, k, v, kc, vc, attn_sink, dmix, probs)
    landed = _join(flight, dq)
    flight = _split("reduce_mlp_share", _piece_sums("mlp", mlp_bigs, landed[:2], landed[2:]), _share_remote(mlp_bigs, 0))
    du, g_pw, st_pool = _pool_bwd(u, dmix, pw_b, pool_scale, flight[3])
    g_mlp = _join(flight, du)

    wo_bigs, win_bigs = mixer_bigs[1:], mixer_bigs[:1]
    wo_chip = _reduce_to_chip("wo", wo_bigs, [g_pw.astype(BF16), g_wout.reshape(wo_bigs[1].full_shape)])
    flight = _split("reduce_wo_ici", wo_chip + empty(_thirds(wo_bigs)), _reduce_ici_remote(wo_bigs))
    dp = _assemble_dp(dq, dkp, dvp, du, cos, sin, flight[3])
    dpc = jnp.concatenate([jnp.zeros((n_ctx, ATTN_WIDTH), BF16), dkc.astype(BF16), dvc.astype(BF16),
                           jnp.zeros((n_ctx, POOL_WIDTH), BF16)], axis=1)
    g_win = _mm_tn("grad_w_in", h, dp, BF16, tmo=d, tn=dp.shape[1] // 2, tt=_pick(length, 1024, 512, 256, 128), more=(hc, dpc))
    wo_landed = _join(flight, g_win)
    win_chip = _reduce_to_chip("win", win_bigs, [g_win])
    flight = _split("reduce_win_ici", win_chip + empty(_thirds(win_bigs)), _reduce_ici_remote(win_bigs))
    grad_x, st_mix = _mixer_dx("mixer_dx", dp, win_b, xs, dx1, norm_attn_w, sc_a, flight[3])
    _, st_ctx = _mixer_dx("mixer_dx_ctx", dpc, win_b, cx, jnp.zeros((n_ctx, d), F32), norm_attn_w, csc_a, flight[3])
    st_mix, st_ctx = _colsum(st_mix), _colsum(st_ctx)
    win_landed = _join(flight, grad_x)
    g_mixer = (_reduce_finish("win", win_bigs, win_landed[:1], win_landed[1:])
               + _reduce_finish("wo", wo_bigs, wo_landed[:2], wo_landed[2:]))

    zrow = jnp.zeros((d,), F32)
    pad = lambda a: jnp.pad(a, (0, d - a.shape[0]))
    mine_rows = [st_mix[0], st_mix[1], st_mlp[3], st_mlp[0], st_mlp[1], st_loss[2],
                 st_ctx[0], st_ctx[1],
                 st_mix[2] + st_ctx[2], st_mlp[2], st_loss[1],
                 pad(jnp.sum(st_pool, axis=0)), pad(dsink[0, :N_Q_HEADS])] + [zrow] * 3
    small_all = _allgather8("gather_small", jnp.concatenate(mine_rows).reshape(len(mine_rows), d))
    small = small_all[0]
    for i in range(1, 8):
        small = small + small_all[i]
    dm_rows = small_all[:, 0:6, :].reshape(8, 6 * d)
    dm_ctx = jnp.concatenate([small[6], small[7], jnp.zeros((4 * d,), F32)])[None, :]
    dm = jnp.concatenate([dm_rows, jnp.pad(dm_ctx, ((0, 7), (0, 0)))], axis=0)
    g_bada = jnp.sum(dm[:9], axis=0, keepdims=True)
    dm_shard = lax.dynamic_slice_in_dim(dm, chip * n_ada, n_ada, axis=1)
    g_wada, dl_wada, nm_wada, nv_wada, part_cctx = _ada_bwd(cond, dm_shard, w_ada[0], m_w_ada[0], v_w_ada[0])
    cctx_all = _allgather8("gather_cctx", part_cctx)
    dsilu_in = cctx_all[0, 0] + cctx_all[2, 0] + cctx_all[4, 0] + cctx_all[6, 0]
    sig = 1.0 / (1.0 + jnp.exp(-c_ctx))
    g_cctx = dsilu_in * (sig * (1.0 + c_ctx * (1.0 - sig)))

    g_shards = g_mixer + g_mlp
    big_w = [w_in, pool_w, w_out, w_mlp_up, w_mlp_down]
    big_m = [m_w_in, m_pool_w, m_w_out, m_w_mlp_up, m_w_mlp_down]
    big_v = [v_w_in, v_pool_w, v_w_out, v_w_mlp_up, v_w_mlp_down]
    big_names = ["w_in", "pool_w", "w_out", "w_mlp_up", "w_mlp_down"]
    res = {}
    for nm, w_, g_, m_, v_ in zip(big_names, big_w, g_shards, big_m, big_v):
        res[nm] = tuple(_adamw("adamw_" + nm, w_, g_.reshape(w_.shape), m_, v_))
    res["w_ada"] = (g_wada[None], dl_wada[None], nm_wada[None], nv_wada[None])

    def pack(cc, na, nm_, ba, sk, ps, fn):
        flat = [cc.reshape(-1), na.reshape(-1), nm_.reshape(-1), ba.reshape(-1), pad(sk.reshape(-1)), pad(ps.reshape(-1)),
                fn.reshape(-1), jnp.zeros((4 * d,), F32)]
        return jnp.concatenate(flat).reshape(16, d)

    w_s = pack(c_ctx, norm_attn_w, norm_mlp_w, b_ada, attn_sink, pool_scale, final_norm_w)
    m_s = pack(m_c_ctx, m_norm_attn_w, m_norm_mlp_w, m_b_ada, m_attn_sink, m_pool_scale, m_final_norm_w)
    v_s = pack(v_c_ctx, v_norm_attn_w, v_norm_mlp_w, v_b_ada, v_attn_sink, v_pool_scale, v_final_norm_w)
    g_s = pack(g_cctx, small[8], small[9], g_bada, small[12][:N_Q_HEADS], small[11][:POOL_WIDTH], small[10])
    small_out = _adamw("adamw_small", w_s, g_s, m_s, v_s)

    def unpack(p):
        return {"c_ctx": p[0], "norm_attn_w": p[1:2], "norm_mlp_w": p[2:3], "b_ada": p[3:9].reshape(1, 6 * d),
                "attn_sink": p[9:10, :N_Q_HEADS], "pool_scale": p[10:11, :POOL_WIDTH], "final_norm_w": p[11]}

    small_res = [unpack(p) for p in small_out]
    order = ["c_ctx", "norm_attn_w", "norm_mlp_w", "w_ada", "b_ada", "w_in", "attn_sink", "pool_w", "pool_scale",
             "w_out", "w_mlp_up", "w_mlp_down", "final_norm_w"]
    outs = [loss, grad_x.reshape(x.shape)]
    for kind in range(4):
        for nm in order:
            outs.append(res[nm][kind] if nm in res else small_res[kind][nm])
    return tuple(outs)
```

```python
import functools

import jax
import jax.numpy as jnp
from jax import lax
from jax.experimental import pallas as pl
from jax.experimental.pallas import tpu as pltpu

F32 = jnp.float32
BF16 = jnp.bfloat16
EPS = 1e-6
NEG_INF = -1e30
HEAD_DIM = 64
N_Q_HEADS = 16
N_KV_HEADS = 4
GROUP = N_Q_HEADS // N_KV_HEADS
ATTN_WIDTH = N_Q_HEADS * HEAD_DIM
KV_WIDTH = N_KV_HEADS * HEAD_DIM
POOL_WINDOWS = (2, 4, 8, 16)
POOL_GROUP_DIM = 256
POOL_WIDTH = len(POOL_WINDOWS) * POOL_GROUP_DIM
BLOCK = 128
GRID_W = 64
ROPE_BASE = 10000.0
SCALE = HEAD_DIM ** -0.5
HALO = 16
ROWS = 64
ADAM_LR, ADAM_B1, ADAM_B2, ADAM_EPS, ADAM_WD, ADAM_STEP = 0.001, 0.9, 0.999, 1e-08, 0.01, 10
MESH = pl.DeviceIdType.MESH
MIB = 1024 * 1024
ANY = pl.BlockSpec(memory_space=pl.ANY)


def _cp(n_axes, vmem_mib=48):
    return pltpu.CompilerParams(dimension_semantics=("arbitrary",) * n_axes, vmem_limit_bytes=vmem_mib * MIB)


def _row_loop(rows, fn):
    def body(r, carry):
        fn(pl.ds(pl.multiple_of(r * ROWS, ROWS), ROWS))
        return carry

    lax.fori_loop(0, rows // ROWS, body, 0)


def _fold8(v):
    s = v[0:8]
    for t in range(1, v.shape[0] // 8):
        s = s + v[8 * t:8 * t + 8]
    return s


def _dot(a, b):
    return jnp.dot(a, b, preferred_element_type=F32)


def _dot_nt(a, b):
    return lax.dot_general(a, b, (((1,), (1,)), ((), ())), preferred_element_type=F32)


def _dot_tn(a, b):
    return lax.dot_general(a, b, (((0,), (0,)), ((), ())), preferred_element_type=F32)


def _pick(n, *cands):
    for t in cands:
        if n % t == 0:
            return t
    return n


def _flip(pos, mask):
    return tuple((1 - v) if (mask >> (2 - i)) & 1 else v for i, v in enumerate(pos))


def _exchange(name, ins, out_shapes, remote, local=(), aliases=None):
    n_io = len(ins) + len(out_shapes)

    def body(*refs):
        io = refs[:n_io]
        send_sems, recv_sems, local_sems = refs[n_io:]
        me = (lax.axis_index("x"), lax.axis_index("y"), lax.axis_index("c"))

        def copy(i, sender):
            mask, src_fn, dst_fn = remote[i]
            return pltpu.make_async_remote_copy(
                src_ref=src_fn(io, sender), dst_ref=dst_fn(io, sender), send_sem=send_sems.at[i],
                recv_sem=recv_sems.at[i], device_id=_flip(sender, mask), device_id_type=MESH)

        own = [pltpu.make_async_copy(s(io, me), d(io, me), local_sems.at[i]) for i, (s, d) in enumerate(local)]
        for cp in own:
            cp.start()
        sends = [copy(i, me) for i in range(len(remote))]
        for cp in sends:
            cp.start()
        for i in range(len(remote)):
            copy(i, _flip(me, remote[i][0])).wait_recv()
        for cp in sends:
            cp.wait_send()
        for cp in own:
            cp.wait()

    return pl.pallas_call(
        body, name=name, out_shape=tuple(out_shapes),
        in_specs=[ANY] * len(ins), out_specs=tuple([ANY] * len(out_shapes)),
        scratch_shapes=[pltpu.SemaphoreType.DMA((len(remote),)), pltpu.SemaphoreType.DMA((len(remote),)),
                        pltpu.SemaphoreType.DMA((max(len(local), 1),))],
        input_output_aliases=aliases or {},
    )(*ins)


HBM = pl.BlockSpec(memory_space=pltpu.HBM)
SEM = pl.BlockSpec(memory_space=pltpu.SEMAPHORE)
EFFECT = pltpu.SideEffectType.DATAFLOW_SIDE_EFFECTING


def _split_copy(remote, i, io, send_sems, recv_sems, sender):
    mask, src_fn, dst_fn = remote[i]
    return pltpu.make_async_remote_copy(
        src_ref=src_fn(io, sender), dst_ref=dst_fn(io, sender), send_sem=send_sems.at[i],
        recv_sem=recv_sems.at[i], device_id=_flip(sender, mask), device_id_type=MESH)


def _exchange_start(name, bufs, remote):
    n, r = len(bufs), len(remote)

    def body(*refs):
        io, send_sems, recv_sems, token = refs[:n], refs[2 * n], refs[2 * n + 1], refs[2 * n + 2]
        me = (lax.axis_index("x"), lax.axis_index("y"), lax.axis_index("c"))
        for i in range(r):
            _split_copy(remote, i, io, send_sems, recv_sems, me).start()
        token[...] = jnp.zeros_like(token)

    res = pl.pallas_call(
        body, name=name,
        out_shape=tuple(pltpu.HBM(b.shape, b.dtype) for b in bufs)
        + (pltpu.SemaphoreType.DMA((r,)), pltpu.SemaphoreType.DMA((r,)), jax.ShapeDtypeStruct((8, 128), F32)),
        in_specs=[HBM] * n, out_specs=tuple([HBM] * n) + (SEM, SEM, pl.BlockSpec(memory_space=pltpu.VMEM)),
        input_output_aliases={i: i for i in range(n)}, compiler_params=pltpu.CompilerParams(has_side_effects=EFFECT),
    )(*[pltpu.with_memory_space_constraint(b, pltpu.HBM) for b in bufs])
    return list(res[:n]), res[n], res[n + 1], res[n + 2]


def _exchange_wait(name, bufs, send_sems, recv_sems, remote, after):
    n, r = len(bufs), len(remote)

    def body(*refs):
        io, ss, rs = refs[:n], refs[n], refs[n + 1]
        me = (lax.axis_index("x"), lax.axis_index("y"), lax.axis_index("c"))
        for i in range(r):
            _split_copy(remote, i, io, ss, rs, _flip(me, remote[i][0])).wait_recv()
        for i in range(r):
            _split_copy(remote, i, io, ss, rs, me).wait_send()

    return list(pl.pallas_call(
        body, name=name, out_shape=tuple(pltpu.HBM(b.shape, b.dtype) for b in bufs),
        in_specs=[HBM] * n + [SEM, SEM, ANY], out_specs=tuple([HBM] * n),
        input_output_aliases={i: i for i in range(n)}, compiler_params=pltpu.CompilerParams(has_side_effects=EFFECT),
    )(*bufs, send_sems, recv_sems, after))


def _my_c():
    return lax.axis_index("c")


def _my_chip():
    return 2 * lax.axis_index("x") + lax.axis_index("y")


def _dev_index(pos):
    return 4 * pos[0] + 2 * pos[1] + pos[2]


def _chip_index(pos):
    return 2 * pos[0] + pos[1]


def _allgather8(name, v):
    out = jax.ShapeDtypeStruct((8,) + v.shape, v.dtype)
    remote = [(mask, lambda io, pos: io[0], lambda io, pos: io[1].at[_dev_index(pos)]) for mask in range(1, 8)]
    local = [(lambda io, pos: io[0], lambda io, pos: io[1].at[_dev_index(pos)])]
    return _exchange(name, [v], [out], remote, local)[0]


class _Big:
    def __init__(self, kind, shard_shape):
        self.kind = kind
        self.shard_shape = tuple(shard_shape)
        if kind == "col":
            r, cs = shard_shape
            self.full_shape = (r, 4 * cs)
            self.piece_shape = (r // 2, cs)
            self.half_shape = (r // 2, 4 * cs)
        elif kind == "row":
            rs, c = shard_shape
            self.full_shape = (4, 2, rs // 2, c)
            self.piece_shape = (1, 1, rs // 2, c)
            self.half_shape = (4, 1, rs // 2, c)
        else:
            self.full_shape = (4, 256, 256)
            self.piece_shape = (2, 64, 256)
            self.half_shape = (2, 256, 256)

    def shard_as_pieces(self, a):
        return a.reshape((1, 2) + self.piece_shape[2:]) if self.kind == "row" else a

    def piece(self, ref, k, h):
        if self.kind == "col":
            r, cs = self.piece_shape
            return ref.at[pl.ds(h * r, r), pl.ds(k * cs, cs)]
        if self.kind == "row":
            return ref.at[pl.ds(k, 1), pl.ds(h, 1)]
        return ref.at[pl.ds(2 * h, 2), pl.ds(64 * k, 64)]

    def half_of_shard(self, ref, h):
        if self.kind == "col":
            return ref.at[pl.ds(h * self.piece_shape[0], self.piece_shape[0])]
        if self.kind == "row":
            return ref.at[:, pl.ds(h, 1)]
        return ref.at[pl.ds(2 * h, 2)]

    def half_of_full(self, ref, h):
        if self.kind == "col":
            return ref.at[pl.ds(h * self.half_shape[0], self.half_shape[0])]
        if self.kind == "row":
            return ref.at[:, pl.ds(h, 1)]
        return ref.at[pl.ds(2 * h, 2)]

    def piece_of_half(self, ref, k):
        if self.kind == "col":
            return ref.at[:, pl.ds(k * self.piece_shape[1], self.piece_shape[1])]
        if self.kind == "row":
            return ref.at[pl.ds(k, 1)]
        return ref.at[:, pl.ds(64 * k, 64)]


CHIP_MASKS = (4, 2, 6)


def _cast_place(name, big, shard, after):
    if big.kind == "col":
        r, cs = big.shard_shape
        tr = _pick(r, 512, 256, 128)
        src, grid, blk = shard, (r // tr,), (tr, cs)
        imap, omap = (lambda i: (i, 0)), (lambda i: (i, _my_chip()))
    elif big.kind == "row":
        rs, c = big.shard_shape
        tr = _pick(rs // 2, 256, 128)
        src, grid, blk = big.shard_as_pieces(shard), (2, rs // 2 // tr), (1, 1, tr, c)
        imap, omap = (lambda h, i: (0, h, i, 0)), (lambda h, i: (_my_chip(), h, i, 0))
    else:
        src, grid, blk = shard, (1,), big.shard_shape
        imap, omap = (lambda i: (0, 0, 0)), (lambda i: (0, _my_chip(), 0))

    def body(s_ref, after_ref, o_ref):
        o_ref[...] = s_ref[...].astype(BF16)

    return pl.pallas_call(
        body, name=name, grid=grid, in_specs=[pl.BlockSpec(blk, imap), ANY], out_specs=pl.BlockSpec(blk, omap),
        out_shape=jax.ShapeDtypeStruct(big.full_shape, BF16), compiler_params=_cp(len(grid)),
    )(src, after)


def _gather_ici_remote(bigs, off):
    remote = []
    for a, b in enumerate(bigs):
        for mask in CHIP_MASKS:
            def mine(io, p, a=a, b=b):
                return b.piece(io[off + a], _chip_index(p), p[2])
            remote.append((mask, mine, mine))
    return remote


def _gather_d2d_remote(bigs, off):
    remote = []
    for a, b in enumerate(bigs):
        for mask in CHIP_MASKS:
            def region(io, p, a=a, b=b, mask=mask):
                return b.piece(io[off + a], _chip_index(_flip(p, mask)), p[2])
            remote.append((1, region, region))
    return remote


def _ew(name, fn, ins, out_dtypes, rows_per_step=256):
    shape = ins[0].shape
    last = shape[-1]
    rows = 1
    for s in shape[:-1]:
        rows *= s
    ins2 = [a.reshape(rows, last) for a in ins]
    tr = _pick(rows, rows_per_step, 128, 64, 32, 16, 8)
    spec = pl.BlockSpec((tr, last), lambda i: (i, 0))

    def body(*refs):
        outs = fn(*[r[...] for r in refs[:len(ins)]])
        for o_ref, o in zip(refs[len(ins):], outs):
            o_ref[...] = o.astype(o_ref.dtype)

    outs = pl.pallas_call(
        body, name=name, grid=(rows // tr,), in_specs=[spec] * len(ins), out_specs=tuple([spec] * len(out_dtypes)),
        out_shape=tuple(jax.ShapeDtypeStruct((rows, last), d) for d in out_dtypes), compiler_params=_cp(1),
    )(*ins2)
    return [o.reshape(shape) for o in outs]


def _chip_sum(name, big, grad, from_sibling):
    if big.kind == "col":
        rh, w = big.half_shape
        tr = _pick(rh, 256, 128)
        nb = rh // tr
        grid, blk = (nb,), (tr, w)
        gmap, hmap = (lambda i: (_my_c() * nb + i, 0)), (lambda i: (i, 0))
    elif big.kind == "row":
        rh, w = big.half_shape[2:]
        tr = _pick(rh, 256, 128)
        grid, blk = (4, rh // tr), (1, 1, tr, w)
        gmap, hmap = (lambda k, i: (k, _my_c(), i, 0)), (lambda k, i: (k, 0, i, 0))
    else:
        grid, blk = (1,), big.half_shape
        gmap, hmap = (lambda i: (_my_c(), 0, 0)), (lambda i: (0, 0, 0))

    def body(g_ref, s_ref, o_ref):
        o_ref[...] = (g_ref[...].astype(F32) + s_ref[...].astype(F32)).astype(BF16)

    return pl.pallas_call(
        body, name=name, grid=grid, in_specs=[pl.BlockSpec(blk, gmap), pl.BlockSpec(blk, hmap)],
        out_specs=pl.BlockSpec(blk, hmap), out_shape=jax.ShapeDtypeStruct(big.half_shape, BF16), compiler_params=_cp(len(grid)),
    )(grad, from_sibling)


def _piece_sum(name, big, chip_sum, thirds):
    if big.kind == "col":
        rp, cs = big.piece_shape
        tr = _pick(rp, 256, 128)
        nb = rp // tr
        grid, blk, tblk = (nb,), (tr, cs), (1, tr, cs)
        smap, omap = (lambda i: (i, _my_chip())), (lambda i: (_my_c() * nb + i, 0))
        tmap = lambda j: (lambda i: (j, i, 0))
        out_shape = big.shard_shape
    elif big.kind == "row":
        rp, w = big.piece_shape[2:]
        tr = _pick(rp, 256, 128)
        grid, blk, tblk = (rp // tr,), (1, 1, tr, w), (1, 1, 1, tr, w)
        smap, omap = (lambda i: (_my_chip(), 0, i, 0)), (lambda i: (0, _my_c(), i, 0))
        tmap = lambda j: (lambda i: (j, 0, 0, i, 0))
        out_shape = (1, 2, rp, w)
    else:
        grid, blk, tblk = (1,), big.piece_shape, (1,) + big.piece_shape
        smap, omap = (lambda i: (0, _my_chip(), 0)), (lambda i: (_my_c(), 0, 0))
        tmap = lambda j: (lambda i: (j, 0, 0, 0))
        out_shape = big.shard_shape

    def body(s_ref, t0, t1, t2, o_ref):
        o_ref[...] = s_ref[...].astype(F32) + t0[0].astype(F32) + t1[0].astype(F32) + t2[0].astype(F32)

    return pl.pallas_call(
        body, name=name, grid=grid,
        in_specs=[pl.BlockSpec(blk, smap)] + [pl.BlockSpec(tblk, tmap(j)) for j in range(3)],
        out_specs=pl.BlockSpec(blk, omap), out_shape=jax.ShapeDtypeStruct(out_shape, F32), compiler_params=_cp(len(grid)),
    )(chip_sum, thirds, thirds, thirds)


def _split(name, bufs, remote):
    return _exchange_start(name + "_start", bufs, remote) + (remote, name)


def _join(handle, after):
    bufs, send_sems, recv_sems, _, remote, name = handle
    return _exchange_wait(name + "_wait", bufs, send_sems, recv_sems, remote, after)


def _reduce_d2d_remote(bigs):
    n = len(bigs)
    return [(1, lambda io, p, a=a, b=b: b.half_of_full(io[a], 1 - p[2]), lambda io, p, a=a: io[n + a])
            for a, b in enumerate(bigs)]


def _halves(bigs):
    return [jax.ShapeDtypeStruct(b.half_shape, BF16) for b in bigs]


def _chip_sums(tag, bigs, grads, from_sibling):
    return [_chip_sum(f"reduce_{tag}_chip_sum_{a}", b, g, r) for a, (b, g, r) in enumerate(zip(bigs, grads, from_sibling))]


def _reduce_to_chip(tag, bigs, grads):
    from_sibling = _exchange(f"reduce_{tag}_d2d", grads, _halves(bigs), _reduce_d2d_remote(bigs))
    return _chip_sums(tag, bigs, grads, from_sibling)


def _reduce_ici_remote(bigs):
    n = len(bigs)
    remote = []
    for a, b in enumerate(bigs):
        for j, mask in enumerate(CHIP_MASKS):
            remote.append((mask,
                           lambda io, p, a=a, b=b, mask=mask: b.piece_of_half(io[a], _chip_index(_flip(p, mask))),
                           lambda io, p, a=a, j=j: io[n + a].at[j]))
    return remote


def _thirds(bigs):
    return [jax.ShapeDtypeStruct((3,) + b.piece_shape, BF16) for b in bigs]


def _piece_sums(tag, bigs, chip_sum, from_chips):
    return [_piece_sum(f"reduce_{tag}_sum_{a}", b, s, r) for a, (b, s, r) in enumerate(zip(bigs, chip_sum, from_chips))]


def _share_remote(bigs, off):
    remote = []
    for a, b in enumerate(bigs):
        def mine(io, p, a=a, b=b):
            return b.half_of_shard(io[off + a], p[2])
        remote.append((1, mine, mine))
    return remote


def _reduce_finish(tag, bigs, chip_sum, from_chips):
    n = len(bigs)
    placed = _piece_sums(tag, bigs, chip_sum, from_chips)
    out = _exchange(f"reduce_{tag}_share_d2d", placed, [jax.ShapeDtypeStruct(p.shape, F32) for p in placed],
                    _share_remote(bigs, n), aliases={a: a for a in range(n)})
    return [o.reshape(b.shard_shape) for o, b in zip(out, bigs)]


def _mm(name, a, b, *, nt, tm, tn, tk, epi, extras=(), extra_specs=(), out_shape, out_specs, after=None, vmem_mib=48):
    m, kdim = a.shape
    n = b.shape[0] if nt else b.shape[1]
    gm, gn, gk = m // tm, n // tn, kdim // tk
    a_spec = pl.BlockSpec((tm, tk), lambda j, i, k: (i, k))
    b_spec = pl.BlockSpec((tn, tk), lambda j, i, k: (j, k)) if nt else pl.BlockSpec((tk, tn), lambda j, i, k: (k, j))
    n_ex = len(extras)
    if after is not None:
        extras, extra_specs = tuple(extras) + (after,), list(extra_specs) + [ANY]

    def body(a_ref, b_ref, *rest):
        ex, outs, acc = rest[:n_ex], rest[len(extras):-1], rest[-1]
        dot = _dot_nt if nt else _dot
        if gk == 1:
            acc[...] = dot(a_ref[...], b_ref[...])
            epi(acc, ex, outs)
        else:
            k = pl.program_id(2)

            @pl.when(k == 0)
            def _():
                acc[...] = dot(a_ref[...], b_ref[...])

            @pl.when(k > 0)
            def _():
                acc[...] += dot(a_ref[...], b_ref[...])

            @pl.when(k == gk - 1)
            def _():
                epi(acc, ex, outs)

    return pl.pallas_call(
        body, name=name, grid=(gn, gm, gk), in_specs=[a_spec, b_spec, *extra_specs], out_specs=tuple(out_specs),
        out_shape=tuple(out_shape), scratch_shapes=[pltpu.VMEM((tm, tn), F32)], compiler_params=_cp(3, vmem_mib),
    )(a, b, *extras)


def _mm_deferred(name, a, b, *, nt, tm, epi, tiles, vecs, out_tiles, n_stats, after, vmem_mib=48):
    m, kdim = a.shape
    n = b.shape[0] if nt else b.shape[1]
    gm = m // tm
    n_t, n_v, n_o = len(tiles), len(vecs), len(out_tiles)
    dot = _dot_nt if nt else _dot

    def body(a_ref, b_ref, *rest):
        t_refs, v_refs = rest[:n_t], rest[n_t:n_t + n_v]
        o_refs, st_ref, acc0, acc1 = rest[n_t + n_v + 1:n_t + n_v + 1 + n_o], rest[-3], rest[-2], rest[-1]
        i = pl.program_id(0)

        @pl.when(i == 0)
        def _():
            acc1[...] = jnp.zeros_like(acc1)
            st_ref[...] = jnp.zeros_like(st_ref)

        @pl.when(i % 2 == 0)
        def _():
            acc0[...] = dot(a_ref[...], b_ref[...])
            epi(acc1[...], t_refs, v_refs, o_refs, st_ref)

        @pl.when(i % 2 == 1)
        def _():
            acc1[...] = dot(a_ref[...], b_ref[...])
            epi(acc0[...], t_refs, v_refs, o_refs, st_ref)

    prev = lambda i: (jnp.maximum(i - 1, 0), 0)
    tile = pl.BlockSpec((tm, n), prev)
    return pl.pallas_call(
        body, name=name, grid=(gm + 1,),
        in_specs=[pl.BlockSpec((tm, kdim), lambda i: (jnp.minimum(i, gm - 1), 0)), pl.BlockSpec(b.shape, lambda i: (0, 0))]
        + [tile] * n_t + [_row_spec(n)] * n_v + [ANY],
        out_specs=tuple([tile] * n_o) + (_stat_spec(n_stats, n),),
        out_shape=tuple(out_tiles) + (jax.ShapeDtypeStruct((n_stats, 8, n), F32),),
        scratch_shapes=[pltpu.VMEM((tm, n), F32), pltpu.VMEM((tm, n), F32)], compiler_params=_cp(1, vmem_mib),
    )(a, b, *tiles, *vecs, after)


def _mm_tn(name, a, b, out_dtype, *, tmo, tn, tt, more=(), vmem_mib=56):
    t, m = a.shape
    n = b.shape[1]
    gt = t // tt

    def body(a_ref, b_ref, *rest):
        o_ref, acc = rest[-2:]
        k = pl.program_id(2)

        @pl.when(k == 0)
        def _():
            first = _dot_tn(a_ref[...], b_ref[...])
            acc[...] = first + _dot_tn(rest[0][...], rest[1][...]) if more else first

        @pl.when(k > 0)
        def _():
            acc[...] += _dot_tn(a_ref[...], b_ref[...])

        @pl.when(k == gt - 1)
        def _():
            o_ref[...] = acc[...].astype(o_ref.dtype)

    more_specs = [pl.BlockSpec((more[0].shape[0], tmo), lambda i, j, k: (0, i)),
                  pl.BlockSpec((more[1].shape[0], tn), lambda i, j, k: (0, j))] if more else []
    return pl.pallas_call(
        body, name=name, grid=(m // tmo, n // tn, gt),
        in_specs=[pl.BlockSpec((tt, tmo), lambda i, j, k: (k, i)), pl.BlockSpec((tt, tn), lambda i, j, k: (k, j))] + more_specs,
        out_specs=pl.BlockSpec((tmo, tn), lambda i, j, k: (i, j)), out_shape=jax.ShapeDtypeStruct((m, n), out_dtype),
        scratch_shapes=[pltpu.VMEM((tmo, tn), F32)], compiler_params=_cp(3, vmem_mib),
    )(a, b, *more)


def _row_spec(d):
    return pl.BlockSpec((1, d), lambda *_: (0, 0))


def _stat_spec(k, d):
    return pl.BlockSpec((k, 8, d), lambda *_: (0, 0, 0))


def _rope(z, cs, sn):
    first = (lax.broadcasted_iota(jnp.int32, (z.shape[0], 128), 1) % 32) < 16
    outs = []
    for j in range(z.shape[1] // 128):
        zc = z[:, 128 * j:128 * (j + 1)]
        partner = jnp.where(first, pltpu.roll(zc, 112, 1), pltpu.roll(zc, 16, 1))
        outs.append(zc * cs + partner * sn)
    return outs[0] if len(outs) == 1 else jnp.concatenate(outs, axis=1)


def _rope_tables(length, rotate):
    if not rotate:
        return jnp.ones((length, 128), F32), jnp.zeros((length, 128), F32)
    half = HEAD_DIM // 2
    t = jnp.arange(length)
    row = (t // GRID_W).astype(F32)
    col = (t % GRID_W).astype(F32)
    e = jnp.arange(128) % HEAD_DIM
    inv_freq = ROPE_BASE ** (-(2 * ((e % half) % (half // 2))).astype(F32) / half)
    pos = jnp.where(e[None, :] < half, row[:, None], col[:, None])
    ang = pos * inv_freq[None, :]
    first = ((e % half) < half // 2)[None, :]
    return jnp.cos(ang), jnp.where(first, -jnp.sin(ang), jnp.sin(ang))


def _mixer_in(name, x, nw, sh, sc, w_in, cos, sin, after):
    t, d = x.shape
    tm = _pick(t, 256, 128)
    n_in = w_in.shape[1]

    def body(x_ref, nw_ref, sh_ref, sc_ref, w_ref, cos_ref, sin_ref, after_ref, h_ref, q_ref, k_ref, v_ref, u_ref):
        xf = x_ref[...]
        r = lax.rsqrt(jnp.mean(xf * xf, axis=-1, keepdims=True) + EPS)
        hb = (((xf * r) * nw_ref[...]) * (1.0 + sc_ref[...]) + sh_ref[...]).astype(BF16)
        h_ref[...] = hb
        p = _dot(hb, w_ref[...])
        cs, sn = cos_ref[...], sin_ref[...]
        q_ref[...] = (_rope(p[:, :ATTN_WIDTH], cs, sn) * SCALE).astype(BF16)
        k_ref[...] = _rope(p[:, ATTN_WIDTH:ATTN_WIDTH + KV_WIDTH], cs, sn).astype(BF16)
        v_ref[...] = p[:, ATTN_WIDTH + KV_WIDTH:ATTN_WIDTH + 2 * KV_WIDTH].astype(BF16)
        u_ref[...] = p[:, ATTN_WIDTH + 2 * KV_WIDTH:]

    def tile(w):
        return pl.BlockSpec((tm, w), lambda i: (i, 0))

    return pl.pallas_call(
        body, name=name, grid=(t // tm,),
        in_specs=[tile(d), _row_spec(d), _row_spec(d), _row_spec(d), pl.BlockSpec((d, n_in), lambda i: (0, 0)),
                  tile(128), tile(128), ANY],
        out_specs=(tile(d), tile(ATTN_WIDTH), tile(KV_WIDTH), tile(KV_WIDTH), tile(POOL_WIDTH)),
        out_shape=(jax.ShapeDtypeStruct((t, d), BF16), jax.ShapeDtypeStruct((t, ATTN_WIDTH), BF16),
                   jax.ShapeDtypeStruct((t, KV_WIDTH), BF16), jax.ShapeDtypeStruct((t, KV_WIDTH), BF16),
                   jax.ShapeDtypeStruct((t, POOL_WIDTH), F32)),
        compiler_params=_cp(1),
    )(x, nw, sh, sc, w_in, cos, sin, after)


def _attn_specs(nb, n_ctx):
    def blk(w, f):
        return pl.BlockSpec((BLOCK, w), lambda n: (f(n), 0))

    prev = lambda n: jnp.maximum(n - 1, 0)
    cur = lambda n: n
    nxt = lambda n: jnp.minimum(n + 1, nb - 1)
    kv = [blk(KV_WIDTH, prev), blk(KV_WIDTH, cur), blk(KV_WIDTH, nxt)]
    ctx = pl.BlockSpec((n_ctx, KV_WIDTH), lambda n: (0, 0))
    return [pl.BlockSpec(memory_space=pltpu.SMEM), blk(ATTN_WIDTH, cur)] + kv + kv + [ctx, ctx]


def _attn_mask(n, length, n_keys):
    row = lax.broadcasted_iota(jnp.int32, (GROUP * BLOCK, n_keys), 0) % BLOCK
    col = lax.broadcasted_iota(jnp.int32, (GROUP * BLOCK, n_keys), 1)
    kpos = (n - 1) * BLOCK + col
    return ((jnp.abs(col - BLOCK - row) <= BLOCK) & (kpos >= 0) & (kpos < length)) | (col >= 3 * BLOCK)


def _group_rows(block, g):
    return jnp.concatenate([block[:, HEAD_DIM * h:HEAD_DIM * (h + 1)] for h in range(GROUP * g, GROUP * (g + 1))], axis=0)


def _group_sink(sink_ref, g):
    head = lax.broadcasted_iota(jnp.int32, (GROUP * BLOCK, 1), 0) // BLOCK
    out = jnp.full((GROUP * BLOCK, 1), sink_ref[0, GROUP * g], F32)
    for j in range(1, GROUP):
        out = jnp.where(head == j, sink_ref[0, GROUP * g + j], out)
    return out


def _attn_fwd(q, k, v, kc, vc, sink):
    length = q.shape[0]
    nb = length // BLOCK
    n_ctx = kc.shape[0]
    n_keys = 3 * BLOCK + n_ctx

    def body(sink_ref, q_ref, kp, k0, kn, vp, v0, vn, kc_ref, vc_ref, o_ref, p_ref):
        n = pl.program_id(0)
        valid = _attn_mask(n, length, n_keys)
        qb = q_ref[...]
        kall = jnp.concatenate([kp[...], k0[...], kn[...], kc_ref[...]], axis=0)
        vall = jnp.concatenate([vp[...], v0[...], vn[...], vc_ref[...]], axis=0)
        outs = []
        for g in range(N_KV_HEADS):
            lanes = slice(HEAD_DIM * g, HEAD_DIM * (g + 1))
            s = jnp.where(valid, _dot_nt(_group_rows(qb, g), kall[:, lanes]), NEG_INF)
            sk = _group_sink(sink_ref, g)
            m = jnp.maximum(jnp.max(s, axis=-1, keepdims=True), sk)
            e = jnp.exp(s - m)
            e_sink = jnp.exp(sk - m)
            inv = 1.0 / (jnp.sum(e, axis=-1, keepdims=True) + e_sink)
            pb = (e * inv).astype(BF16)
            p_ref[0, g, :, :n_keys] = pb
            p_ref[0, g, :, n_keys:] = jnp.broadcast_to(e_sink * inv, (GROUP * BLOCK, 128)).astype(BF16)
            o = _dot(pb, vall[:, lanes])
            outs += [o[BLOCK * j:BLOCK * (j + 1)] for j in range(GROUP)]
        o_ref[...] = jnp.concatenate(outs, axis=1).astype(BF16)

    return pl.pallas_call(
        body, name="attn_fwd", grid=(nb,), in_specs=_attn_specs(nb, n_ctx),
        out_specs=(pl.BlockSpec((BLOCK, ATTN_WIDTH), lambda n: (n, 0)),
                   pl.BlockSpec((1, N_KV_HEADS, GROUP * BLOCK, n_keys + 128), lambda n: (n, 0, 0, 0))),
        out_shape=(jax.ShapeDtypeStruct((length, ATTN_WIDTH + POOL_WIDTH), BF16),
                   jax.ShapeDtypeStruct((nb, N_KV_HEADS, GROUP * BLOCK, n_keys + 128), BF16)), compiler_params=_cp(1),
    )(sink, q, k, k, k, v, v, v, kc, vc)


def _attn_bwd(q, k, v, kc, vc, sink, dmix, probs):
    length = q.shape[0]
    nb = length // BLOCK
    n_ctx = kc.shape[0]
    n_keys = 3 * BLOCK + n_ctx

    def body(sink_ref, q_ref, kp, k0, kn, vp, v0, vn, kc_ref, vc_ref, do_ref, p_ref,
             dq_ref, dkp_ref, dvp_ref, dkc_ref, dvc_ref, dsink_ref):
        n = pl.program_id(0)

        @pl.when(n == 0)
        def _():
            dkc_ref[...] = jnp.zeros_like(dkc_ref)
            dvc_ref[...] = jnp.zeros_like(dvc_ref)
            dsink_ref[...] = jnp.zeros_like(dsink_ref)

        qb, dob = q_ref[...], do_ref[...]
        kall = jnp.concatenate([kp[...], k0[...], kn[...], kc_ref[...]], axis=0)
        vall = jnp.concatenate([vp[...], v0[...], vn[...], vc_ref[...]], axis=0)
        srow = lax.broadcasted_iota(jnp.int32, (8, 128), 0)
        slane = lax.broadcasted_iota(jnp.int32, (8, 128), 1)
        dqs, dks, dvs = [], [], []
        dsink = jnp.zeros((8, 128), F32)
        for g in range(N_KV_HEADS):
            lanes = slice(HEAD_DIM * g, HEAD_DIM * (g + 1))
            kg, vg = kall[:, lanes], vall[:, lanes]
            qg, dog = _group_rows(qb, g), _group_rows(dob, g)
            pb = p_ref[0, g, :, :n_keys]
            p = pb.astype(F32)
            dp = _dot_nt(dog, vg)
            delta = jnp.sum(p * dp, axis=-1, keepdims=True)
            ds = (p * (dp - delta)).astype(BF16)
            dq = _dot(ds, kg) * SCALE
            dqs += [dq[BLOCK * j:BLOCK * (j + 1)] for j in range(GROUP)]
            dks.append(_dot_tn(ds, qg))
            dvs.append(_dot_tn(pb, dog))
            d_sink = p_ref[0, g, :, n_keys:].astype(F32)[:, :1] * delta
            for j in range(GROUP):
                total = -jnp.sum(d_sink[BLOCK * j:BLOCK * (j + 1)], axis=0, keepdims=True)
                dsink = dsink + jnp.where((srow == 0) & (slane == GROUP * g + j), total, 0.0)
        dq_ref[...] = jnp.concatenate(dqs, axis=1)
        dk = jnp.concatenate(dks, axis=1)
        dv = jnp.concatenate(dvs, axis=1)
        for j in range(3):
            dkp_ref[0, j] = dk[BLOCK * j:BLOCK * (j + 1)]
            dvp_ref[0, j] = dv[BLOCK * j:BLOCK * (j + 1)]
        dkc_ref[...] += dk[3 * BLOCK:]
        dvc_ref[...] += dv[3 * BLOCK:]
        dsink_ref[...] += dsink

    part = pl.BlockSpec((1, 3, BLOCK, KV_WIDTH), lambda n: (n, 0, 0, 0))
    ctx = pl.BlockSpec((n_ctx, KV_WIDTH), lambda n: (0, 0))
    return pl.pallas_call(
        body, name="attn_bwd", grid=(nb,),
        in_specs=_attn_specs(nb, n_ctx) + [pl.BlockSpec((BLOCK, ATTN_WIDTH), lambda n: (n, 0)),
                                           pl.BlockSpec((1,) + probs.shape[1:], lambda n: (n, 0, 0, 0))],
        out_specs=(pl.BlockSpec((BLOCK, ATTN_WIDTH), lambda n: (n, 0)), part, part, ctx, ctx,
                   pl.BlockSpec((8, 128), lambda n: (0, 0))),
        out_shape=(jax.ShapeDtypeStruct((length, ATTN_WIDTH), F32),
                   jax.ShapeDtypeStruct((nb, 3, BLOCK, KV_WIDTH), F32), jax.ShapeDtypeStruct((nb, 3, BLOCK, KV_WIDTH), F32),
                   jax.ShapeDtypeStruct((n_ctx, KV_WIDTH), F32), jax.ShapeDtypeStruct((n_ctx, KV_WIDTH), F32),
                   jax.ShapeDtypeStruct((8, 128), F32)),
        compiler_params=_cp(1),
    )(sink, q, k, k, k, v, v, v, kc, vc, dmix, probs)


def _assemble_dp(dq, dkp, dvp, du, cos, sin, after):
    length = dq.shape[0]
    nb = length // BLOCK

    def body(dq_ref, dka, dkb, dkc, dva, dvb, dvc, du_ref, cos_ref, sin_ref, after_ref, o_ref):
        n = pl.program_id(0)
        has_next = (n + 1 < nb).astype(F32)
        has_prev = (n > 0).astype(F32)
        cs, sn = cos_ref[...], -sin_ref[...]
        dk = dka[0, 0] * has_next + dkb[0, 0] + dkc[0, 0] * has_prev
        dv = dva[0, 0] * has_next + dvb[0, 0] + dvc[0, 0] * has_prev
        o_ref[:, :ATTN_WIDTH] = _rope(dq_ref[...], cs, sn).astype(BF16)
        o_ref[:, ATTN_WIDTH:ATTN_WIDTH + KV_WIDTH] = _rope(dk, cs, sn).astype(BF16)
        o_ref[:, ATTN_WIDTH + KV_WIDTH:ATTN_WIDTH + 2 * KV_WIDTH] = dv.astype(BF16)
        o_ref[:, ATTN_WIDTH + 2 * KV_WIDTH:] = du_ref[...]

    def part(slot, f):
        return pl.BlockSpec((1, 1, BLOCK, KV_WIDTH), lambda n: (f(n), slot, 0, 0))

    parts = [part(0, lambda n: jnp.minimum(n + 1, nb - 1)), part(1, lambda n: n), part(2, lambda n: jnp.maximum(n - 1, 0))]

    def tile(w):
        return pl.BlockSpec((BLOCK, w), lambda n: (n, 0))

    width = ATTN_WIDTH + 2 * KV_WIDTH + POOL_WIDTH
    return pl.pallas_call(
        body, name="assemble_dp", grid=(nb,),
        in_specs=[tile(ATTN_WIDTH)] + parts + parts + [tile(POOL_WIDTH), tile(128), tile(128), ANY],
        out_specs=tile(width), out_shape=jax.ShapeDtypeStruct((length, width), BF16), compiler_params=_cp(1),
    )(dq, dkp, dkp, dkp, dvp, dvp, dvp, du, cos, sin, after)


def _shift_rows(e, s):
    n = e.shape[0]
    return e if s % n == 0 else pltpu.roll(e, (-s) % n, 0)


def _window_sum(e, w, first):
    s, n = e, 1
    while n < w:
        s = s + _shift_rows(s, n)
        n *= 2
    return _shift_rows(s, first)


def _pool_geometry(i, tm, length):
    pos = i * tm - HALO + lax.broadcasted_iota(jnp.int32, (tm + 2 * HALO, 1), 0)
    inside = (pos >= 0) & (pos < length)
    inv_counts = []
    for w in POOL_WINDOWS:
        lo = jnp.clip(pos - w // 2, 0, length)
        hi = jnp.clip(pos - w // 2 + w, 0, length)
        inv_counts.append(1.0 / jnp.maximum(hi - lo, 1).astype(F32))
    return inside, inv_counts


def _halo_specs(tm, width, length, col=0):
    per = tm // HALO
    last = length // HALO - 1
    return [pl.BlockSpec((HALO, width), lambda i: (jnp.maximum(i * per - 1, 0), col)),
            pl.BlockSpec((tm, width), lambda i: (i, col)),
            pl.BlockSpec((HALO, width), lambda i: (jnp.minimum((i + 1) * per, last), col))]


def _pooled(ext, inv_counts, tm):
    outs = []
    for g, w in enumerate(POOL_WINDOWS):
        e = ext[:, POOL_GROUP_DIM * g:POOL_GROUP_DIM * (g + 1)]
        mean = _window_sum(e, w, -(w // 2)) * inv_counts[g]
        outs.append((mean - e)[HALO:HALO + tm])
    return outs


def _pool_fwd(u, pool_w, pool_scale, mix):
    length = u.shape[0]
    tm = _pick(length, 256, 128)

    def body(up, u0, un, w_ref, sc_ref, mix_ref, o_ref):
        inside, inv_counts = _pool_geometry(pl.program_id(0), tm, length)
        ext = jnp.where(inside, jnp.concatenate([up[...], u0[...], un[...]], axis=0), 0.0)
        pooled = _pooled(ext, inv_counts, tm)
        mixed = [_dot(pooled[g].astype(BF16), w_ref[g]) for g in range(len(POOL_WINDOWS))]
        o_ref[...] = (jnp.concatenate(mixed, axis=1) * sc_ref[...]).astype(BF16)

    return pl.pallas_call(
        body, name="pool_fwd", grid=(length // tm,),
        in_specs=_halo_specs(tm, POOL_WIDTH, length) + [pl.BlockSpec(pool_w.shape, lambda i: (0, 0, 0)), _row_spec(POOL_WIDTH), ANY],
        out_specs=pl.BlockSpec((tm, POOL_WIDTH), lambda i: (i, 1)),
        out_shape=jax.ShapeDtypeStruct(mix.shape, BF16), input_output_aliases={5: 0}, compiler_params=_cp(1),
    )(u, u, u, pool_w, pool_scale, mix)


def _pool_bwd(u, dmix, pool_w, pool_scale, after):
    length = u.shape[0]
    tm = _pick(length, 256, 128)
    n_g = len(POOL_WINDOWS)

    def body(up, u0, un, dp_, d0, dn_, w_ref, sc_ref, after_ref, du_ref, dw_ref, dsc_ref):
        i = pl.program_id(0)

        @pl.when(i == 0)
        def _():
            dw_ref[...] = jnp.zeros_like(dw_ref)
            dsc_ref[...] = jnp.zeros_like(dsc_ref)

        inside, inv_counts = _pool_geometry(i, tm, length)
        ext = jnp.where(inside, jnp.concatenate([up[...], u0[...], un[...]], axis=0), 0.0)
        dext = jnp.where(inside, jnp.concatenate([dp_[...], d0[...], dn_[...]], axis=0).astype(F32), 0.0)
        dmixed = (dext * sc_ref[...]).astype(BF16)
        pooled = _pooled(ext, inv_counts, tm)
        dus, dscs = [], []
        for g, w in enumerate(POOL_WINDOWS):
            lanes = slice(POOL_GROUP_DIM * g, POOL_GROUP_DIM * (g + 1))
            dpooled = _dot_nt(dmixed[:, lanes], w_ref[g])
            spread = _window_sum(dpooled * inv_counts[g], w, -(w // 2 - 1))
            dus.append((spread - dpooled)[HALO:HALO + tm])
            pb = pooled[g].astype(BF16)
            dw_ref[g] += _dot_tn(pb, dmixed[HALO:HALO + tm, lanes])
            prod = dext[HALO:HALO + tm, lanes] * _dot(pb, w_ref[g])
            dscs.append(_fold8(prod))
        du_ref[...] = jnp.concatenate(dus, axis=1).astype(BF16)
        dsc_ref[...] += jnp.concatenate(dscs, axis=1)

    return pl.pallas_call(
        body, name="pool_bwd", grid=(length // tm,),
        in_specs=_halo_specs(tm, POOL_WIDTH, length) + _halo_specs(tm, POOL_WIDTH, length, col=1)
        + [pl.BlockSpec(pool_w.shape, lambda i: (0, 0, 0)), _row_spec(POOL_WIDTH), ANY],
        out_specs=(pl.BlockSpec((tm, POOL_WIDTH), lambda i: (i, 0)), pl.BlockSpec((n_g, POOL_GROUP_DIM, POOL_GROUP_DIM), lambda i: (0, 0, 0)),
                   pl.BlockSpec((8, POOL_WIDTH), lambda i: (0, 0))),
        out_shape=(jax.ShapeDtypeStruct((length, POOL_WIDTH), BF16), jax.ShapeDtypeStruct((n_g, POOL_GROUP_DIM, POOL_GROUP_DIM), F32),
                   jax.ShapeDtypeStruct((8, POOL_WIDTH), F32)),
        compiler_params=_cp(1),
    )(u, u, u, dmix, dmix, dmix, pool_w, pool_scale, after)


def _mixer_out(mix, w_out, x, g_a, nmw, sh_m, sc_m, after):
    t, d = x.shape

    def epi(mo, tiles, vecs, outs, st_ref):
        ga, nw, sh, sc = vecs
        x1_ref, mo_ref, hm_ref = outs
        x1 = tiles[0][...] + ga[...] * mo
        x1_ref[...] = x1
        mo_ref[...] = mo.astype(BF16)
        r = lax.rsqrt(jnp.mean(x1 * x1, axis=-1, keepdims=True) + EPS)
        hm_ref[...] = (((x1 * r) * nw[...]) * (1.0 + sc[...]) + sh[...]).astype(BF16)

    return _mm_deferred("mixer_out", mix, w_out, nt=False, tm=_pick(t, 256, 128), epi=epi, tiles=(x,), vecs=(g_a, nmw, sh_m, sc_m),
                        out_tiles=(jax.ShapeDtypeStruct((t, d), F32), jax.ShapeDtypeStruct((t, d), BF16), jax.ShapeDtypeStruct((t, d), BF16)),
                        n_stats=1, after=after)[:3]


def _mlp_up(hm, w_up):
    t, d = hm.shape
    tm = _pick(t, 1024, 512, 256, 128)
    tn = 2048

    def epi(acc, ex, outs):
        outs[0][...] = jnp.square(jnp.maximum(acc[...], 0.0)).astype(BF16)

    return _mm("mlp_up", hm, w_up, nt=False, tm=tm, tn=tn, tk=d, epi=epi,
               out_shape=(jax.ShapeDtypeStruct((t, w_up.shape[1]), BF16),),
               out_specs=(pl.BlockSpec((tm, tn), lambda j, i, k: (i, j)),))[0]


def _mm_f32(name, a, b, *, nt, after):
    m, kdim = a.shape
    n = b.shape[0] if nt else b.shape[1]
    tm, tn = _pick(m, 1024, 512, 256, 128), _pick(n, 1024)

    def epi(acc, ex, outs):
        outs[0][...] = acc[...]

    return _mm(name, a, b, nt=nt, tm=tm, tn=tn, tk=_pick(kdim, 2048), epi=epi, after=after,
               out_shape=(jax.ShapeDtypeStruct((m, n), F32),), out_specs=(pl.BlockSpec((tm, tn), lambda j, i, k: (i, j)),))[0]


def _rows_call(name, rows_fn, tiles, vecs, out_shape, n_stats, after):
    t, d = tiles[0].shape
    tm = _pick(t, 256, 128)
    n_t, n_v = len(tiles), len(vecs)

    def body(*refs):
        st_ref = refs[-1]

        @pl.when(pl.program_id(0) == 0)
        def _():
            st_ref[...] = jnp.zeros_like(st_ref)

        _row_loop(tm, lambda rs: rows_fn(rs, refs[:n_t], refs[n_t:n_t + n_v], refs[n_t + n_v + 1:-1], st_ref))

    tile = pl.BlockSpec((tm, d), lambda i: (i, 0))
    return pl.pallas_call(
        body, name=name, grid=(t // tm,), in_specs=[tile] * n_t + [_row_spec(d)] * n_v + [ANY],
        out_specs=tuple([tile] * len(out_shape)) + (_stat_spec(n_stats, d),),
        out_shape=tuple(out_shape) + (jax.ShapeDtypeStruct((n_stats, 8, d), F32),), compiler_params=_cp(1),
    )(*tiles, *vecs, after)


def _loss_rows(dn, x1, target, g_m, fw):
    t, d = x1.shape

    def rows_fn(rs, tiles, vecs, outs, st_ref):
        dn_ref, x1_ref, t_ref = tiles
        gm, fw_ref = vecs
        dx2_ref, ddn_ref = outs
        dnv = dn_ref[rs, :]
        x2 = x1_ref[rs, :] + gm[...] * dnv
        r = lax.rsqrt(jnp.mean(x2 * x2, axis=-1, keepdims=True) + EPS)
        xh = x2 * r
        diff = xh * fw_ref[...] - t_ref[rs, :]
        dy = diff * (1.0 / d)
        dxh = dy * fw_ref[...]
        dx2 = r * (dxh - xh * jnp.mean(dxh * xh, axis=-1, keepdims=True))
        dx2_ref[rs, :] = dx2
        ddn_ref[rs, :] = (dx2 * gm[...]).astype(BF16)
        st_ref[0] += _fold8(diff * diff)
        st_ref[1] += _fold8(dy * xh)
        st_ref[2] += _fold8(dx2 * dnv)

    return _rows_call("loss_rows", rows_fn, (dn, x1, target), (g_m, fw),
                      (jax.ShapeDtypeStruct((t, d), F32), jax.ShapeDtypeStruct((t, d), BF16)), 3, g_m)


def _mlp_dx_rows(dhm, x1, dx2, mo, nmw, sc_m, g_a, after):
    t, d = x1.shape

    def rows_fn(rs, tiles, vecs, outs, st_ref):
        dh_ref, x1_ref, dx2_ref, mo_ref = tiles
        nw, sc, ga = vecs
        dx1_ref, dmi_ref = outs
        dx1 = _norm_bwd_rows(dh_ref[rs, :], x1_ref[rs, :], nw[...], sc[...], st_ref) + dx2_ref[rs, :]
        dx1_ref[rs, :] = dx1
        dmi_ref[rs, :] = (dx1 * ga[...]).astype(BF16)
        st_ref[3] += _fold8(dx1 * mo_ref[rs, :].astype(F32))

    return _rows_call("mlp_dx_rows", rows_fn, (dhm, x1, dx2, mo), (nmw, sc_m, g_a),
                      (jax.ShapeDtypeStruct((t, d), F32), jax.ShapeDtypeStruct((t, d), BF16)), 4, after)


def _mlp_dact(ddn, w_down, act):
    t, d = ddn.shape
    tm = _pick(t, 512, 256, 128)
    tn = 2048

    def epi(acc, ex, outs):
        outs[0][...] = (acc[...] * (2.0 * jnp.sqrt(ex[0][...]).astype(F32))).astype(BF16)

    tile = pl.BlockSpec((tm, tn), lambda j, i, k: (i, j))
    return _mm("mlp_dact", ddn, w_down, nt=True, tm=tm, tn=tn, tk=d, epi=epi, extras=(act,), extra_specs=[tile],
               out_shape=(jax.ShapeDtypeStruct(act.shape, BF16),), out_specs=(tile,))[0]


def _norm_bwd_rows(dh, xv, nw, sc, st_ref):
    r = lax.rsqrt(jnp.mean(xv * xv, axis=-1, keepdims=True) + EPS)
    xh = xv * r
    dy = dh * (1.0 + sc)
    st_ref[0] += _fold8(dh)
    st_ref[1] += _fold8(dh * (xh * nw))
    st_ref[2] += _fold8(dy * xh)
    dxh = dy * nw
    return r * (dxh - xh * jnp.mean(dxh * xh, axis=-1, keepdims=True))


def _mixer_dmix(dmi, w_out):
    t, d = dmi.shape
    tm = _pick(t, 512, 256, 128)

    def epi(acc, ex, outs):
        outs[0][...] = acc[...].astype(BF16)

    n = w_out.shape[0]
    return _mm("mixer_dmix", dmi, w_out, nt=True, tm=tm, tn=n, tk=d, epi=epi,
               out_shape=(jax.ShapeDtypeStruct((t, n), BF16),), out_specs=(pl.BlockSpec((tm, n), lambda j, i, k: (i, 0)),))[0]


def _mixer_dx(name, dp, w_in, x, dx1, naw, sc_a, after):
    t, d = x.shape

    def epi(dh, tiles, vecs, outs, st_ref):
        x_ref, dx1_ref = tiles
        nw, sc = vecs
        outs[0][...] = _norm_bwd_rows(dh, x_ref[...], nw[...], sc[...], st_ref) + dx1_ref[...]

    return _mm_deferred(name, dp, w_in, nt=True, tm=_pick(t, 256, 128), epi=epi, tiles=(x, dx1), vecs=(naw, sc_a),
                        out_tiles=(jax.ShapeDtypeStruct((t, d), F32),), n_stats=3, after=after, vmem_mib=56)


def _silu(v):
    return v / (1.0 + jnp.exp(-v))


def _ada_fwd(cond, w_ada, b_ada):
    d, n = w_ada.shape
    tn = 512

    def body(c_ref, w_ref, b_ref, o_ref):
        o_ref[...] = _dot(_silu(c_ref[...]).astype(BF16), w_ref[...].astype(BF16)) + b_ref[...]

    return pl.pallas_call(
        body, name="ada_fwd", grid=(n // tn,),
        in_specs=[pl.BlockSpec(cond.shape, lambda j: (0, 0)), pl.BlockSpec((d, tn), lambda j: (0, j)), pl.BlockSpec((1, tn), lambda j: (0, j))],
        out_specs=pl.BlockSpec((cond.shape[0], tn), lambda j: (0, j)), out_shape=jax.ShapeDtypeStruct((cond.shape[0], n), F32),
        compiler_params=_cp(1),
    )(cond, w_ada, b_ada)


def _adamw_math(w, g, m, v):
    m = ADAM_B1 * m + (1.0 - ADAM_B1) * g
    v = ADAM_B2 * v + (1.0 - ADAM_B2) * jnp.square(g)
    m_hat = m / (1.0 - ADAM_B1 ** ADAM_STEP)
    v_hat = v / (1.0 - ADAM_B2 ** ADAM_STEP)
    return -ADAM_LR * (m_hat / (jnp.sqrt(v_hat) + ADAM_EPS) + ADAM_WD * w), m, v


def _ada_bwd(cond, dm, w_ada, m_ada, v_ada):
    d, n = w_ada.shape
    tn = 256
    rows = cond.shape[0]

    def body(c_ref, dm_ref, w_ref, m_ref, v_ref, g_ref, dl_ref, nm_ref, nv_ref, pc_ref):
        @pl.when(pl.program_id(0) == 0)
        def _():
            pc_ref[...] = jnp.zeros_like(pc_ref)

        dmb = dm_ref[...].astype(BF16)
        w = w_ref[...]
        g = _dot_tn(_silu(c_ref[...]).astype(BF16), dmb)
        g_ref[...] = g
        dl_ref[...], nm_ref[...], nv_ref[...] = _adamw_math(w, g, m_ref[...], v_ref[...])
        pc_ref[...] += _dot_nt(dm_ref[8:16, :].astype(BF16), w.astype(BF16))

    tile = pl.BlockSpec((d, tn), lambda j: (0, j))
    like = jax.ShapeDtypeStruct((d, n), F32)
    return pl.pallas_call(
        body, name="ada_bwd", grid=(n // tn,),
        in_specs=[pl.BlockSpec((rows, d), lambda j: (0, 0)), pl.BlockSpec((rows, tn), lambda j: (0, j)), tile, tile, tile],
        out_specs=(tile, tile, tile, tile, pl.BlockSpec((8, d), lambda j: (0, 0))),
        out_shape=(like, like, like, like, jax.ShapeDtypeStruct((8, d), F32)), compiler_params=_cp(1),
    )(cond, dm, w_ada, m_ada, v_ada)


def _adamw(name, w, g, m, v):
    return _ew(name, lambda w_, g_, m_, v_: (g_,) + _adamw_math(w_, g_, m_, v_), [w, g, m, v], [F32, F32, F32, F32])


def _colsum(st):
    return jnp.sum(st, axis=1)


def kernel(x, c, ctx, c_ctx, norm_attn_w, norm_mlp_w, w_ada, b_ada, w_in, attn_sink, pool_w, pool_scale, w_out, w_mlp_up, w_mlp_down, final_norm_w, loss_target, m_c_ctx, m_norm_attn_w, m_norm_mlp_w, m_w_ada, m_b_ada, m_w_in, m_attn_sink, m_pool_w, m_pool_scale, m_w_out, m_w_mlp_up, m_w_mlp_down, m_final_norm_w, v_c_ctx, v_norm_attn_w, v_norm_mlp_w, v_w_ada, v_b_ada, v_w_in, v_attn_sink, v_pool_w, v_pool_scale, v_w_out, v_w_mlp_up, v_w_mlp_down, v_final_norm_w):
    length, d = x.shape[1], x.shape[2]
    n_ctx = ctx.shape[1]
    pos = (lax.axis_index("x"), lax.axis_index("y"), lax.axis_index("c"))
    me, chip = _dev_index(pos), _chip_index(pos)
    xs, tgt, cx = x.reshape(length, d), loss_target.reshape(length, d), ctx.reshape(n_ctx, d)
    n_ada = w_ada.shape[2]

    c_all = _allgather8("gather_c", jnp.pad(c, ((0, 7), (0, 0))))
    mixer_bigs = [_Big("col", w_in.shape[1:]), _Big("pool", pool_w.shape[1:]), _Big("row", w_out.shape[1:])]
    mlp_bigs = [_Big("col", w_mlp_up.shape[1:]), _Big("row", w_mlp_down.shape[1:])]
    placed = [_cast_place(f"place_{i}", b, s, c_all) for i, (b, s) in enumerate(zip(mixer_bigs, [w_in[0], pool_w[0], w_out[0]]))]
    flight = _split("gather_mixer_ici", placed, _gather_ici_remote(mixer_bigs, 0))
    cond = jnp.concatenate([c_all[:, 0, :], jnp.pad(c_ctx[None, :], ((0, 7), (0, 0)))], axis=0) + flight[3][0, 0]
    b_shard = lax.dynamic_slice_in_dim(b_ada, chip * n_ada, n_ada, axis=1)
    mod_all = _allgather8("gather_mod", _ada_fwd(cond, w_ada[0], b_shard))
    mod = jnp.concatenate([mod_all[0], mod_all[2], mod_all[4], mod_all[6]], axis=1)
    mine = lax.dynamic_slice_in_dim(mod, me, 1, axis=0)
    sh_a, sc_a, g_a, sh_m, sc_m, g_m = [mine[:, d * i:d * (i + 1)] for i in range(6)]
    csh_a, csc_a = mod[8:9, :d], mod[8:9, d:2 * d]

    win_b, pw_b, wout_b = _exchange("gather_mixer_d2d", _join(flight, mod), [jax.ShapeDtypeStruct(b.full_shape, BF16) for b in mixer_bigs],
                                    _gather_d2d_remote(mixer_bigs, 3), aliases={0: 0, 1: 1, 2: 2})
    wout_b = wout_b.reshape(-1, d)
    placed = [_cast_place(f"place_mlp_{i}", b, s, pw_b) for i, (b, s) in enumerate(zip(mlp_bigs, [w_mlp_up[0], w_mlp_down[0]]))]
    flight = _split("gather_mlp_ici", placed, _gather_ici_remote(mlp_bigs, 0))

    cos, sin = _rope_tables(length, True)
    one, zero = _rope_tables(n_ctx, False)
    h, q, k, v, u = _mixer_in("mixer_in", xs, norm_attn_w, sh_a, sc_a, win_b, cos, sin, flight[3])
    hc, _, kc, vc, _ = _mixer_in("mixer_in_ctx", cx, norm_attn_w, csh_a, csc_a, win_b, one, zero, flight[3])
    attn, probs = _attn_fwd(q, k, v, kc, vc, attn_sink)
    mix = _pool_fwd(u, pw_b, pool_scale, attn)
    flight = _split("gather_mlp_d2d", _join(flight, mix), _gather_d2d_remote(mlp_bigs, 0))
    x1, mo, hm = _mixer_out(mix, wout_b, xs, g_a, norm_mlp_w, sh_m, sc_m, flight[3])
    wup_b, wdn_b = _join(flight, hm)
    wdn_b = wdn_b.reshape(-1, d)
    act = _mlp_up(hm, wup_b)
    dn = _mm_f32("mlp_down", act, wdn_b, nt=False, after=c)
    dx2, ddn, st_loss = _loss_rows(dn, x1, tgt, g_m, final_norm_w[None, :])
    st_loss = _colsum(st_loss)
    loss = lax.psum(0.5 / d * jnp.sum(st_loss[0]), ("x", "y", "c"))

    tt = _pick(length, 2048, 1024, 512, 256, 128)
    g_wdn = _mm_tn("grad_w_down", act, ddn, BF16, tmo=1024, tn=d, tt=tt)
    dup = _mlp_dact(ddn, wdn_b, act)
    g_wup = _mm_tn("grad_w_up", hm, dup, BF16, tmo=d, tn=1024, tt=tt)
    empty = lambda shapes: [lax.empty(s.shape, s.dtype) for s in shapes]
    grads = [g_wup, g_wdn.reshape(mlp_bigs[1].full_shape)]
    flight = _split("reduce_mlp_d2d", grads + empty(_halves(mlp_bigs)), _reduce_d2d_remote(mlp_bigs))
    dhm = _mm_f32("mlp_dhm", dup, wup_b, nt=True, after=flight[3])
    landed = _join(flight, dhm)
    mlp_chip = _chip_sums("mlp", mlp_bigs, landed[:2], landed[2:])
    flight = _split("reduce_mlp_ici", mlp_chip + empty(_thirds(mlp_bigs)), _reduce_ici_remote(mlp_bigs))
    dx1, dmi, st_mlp = _mlp_dx_rows(dhm, x1, dx2, mo, norm_mlp_w, sc_m, g_a, flight[3])
    st_mlp = _colsum(st_mlp)
    g_wout = _mm_tn("grad_w_out", mix, dmi, BF16, tmo=1024, tn=d, tt=tt)
    dmix = _mixer_dmix(dmi, wout_b)
    dq, dkp, dvp, dkc, dvc, dsink = _attn_bwd(q, k, v, kc, vc, attn_sink, dmix, probs)
    landed = _join(flight, dq)
    flight = _split("reduce_mlp_share", _piece_sums("mlp", mlp_bigs, landed[:2], landed[2:]), _share_remote(mlp_bigs, 0))
    du, g_pw, st_pool = _pool_bwd(u, dmix, pw_b, pool_scale, flight[3])
    g_mlp = _join(flight, du)

    wo_bigs, win_bigs = mixer_bigs[1:], mixer_bigs[:1]
    wo_chip = _reduce_to_chip("wo", wo_bigs, [g_pw.astype(BF16), g_wout.reshape(wo_bigs[1].full_shape)])
    flight = _split("reduce_wo_ici", wo_chip + empty(_thirds(wo_bigs)), _reduce_ici_remote(wo_bigs))
    dp = _assemble_dp(dq, dkp, dvp, du, cos, sin, flight[3])
    dpc = jnp.concatenate([jnp.zeros((n_ctx, ATTN_WIDTH), BF16), dkc.astype(BF16), dvc.astype(BF16),
                           jnp.zeros((n_ctx, POOL_WIDTH), BF16)], axis=1)
    g_win = _mm_tn("grad_w_in", h, dp, BF16, tmo=d, tn=dp.shape[1] // 2, tt=_pick(length, 1024, 512, 256, 128), more=(hc, dpc))
    wo_landed = _join(flight, g_win)
    win_chip = _reduce_to_chip("win", win_bigs, [g_win])
    flight = _split("reduce_win_ici", win_chip + empty(_thirds(win_bigs)), _reduce_ici_remote(win_bigs))
    grad_x, st_mix = _mixer_dx("mixer_dx", dp, win_b, xs, dx1, norm_attn_w, sc_a, flight[3])
    _, st_ctx = _mixer_dx("mixer_dx_ctx", dpc, win_b, cx, jnp.zeros((n_ctx, d), F32), norm_attn_w, csc_a, flight[3])
    st_mix, st_ctx = _colsum(st_mix), _colsum(st_ctx)
    win_landed = _join(flight, grad_x)
    g_mixer = (_reduce_finish("win", win_bigs, win_landed[:1], win_landed[1:])
               + _reduce_finish("wo", wo_bigs, wo_landed[:2], wo_landed[2:]))

    zrow = jnp.zeros((d,), F32)
    pad = lambda a: jnp.pad(a, (0, d - a.shape[0]))
    mine_rows = [st_mix[0], st_mix[1], st_mlp[3], st_mlp[0], st_mlp[1], st_loss[2],
                 st_ctx[0], st_ctx[1],
                 st_mix[2] + st_ctx[2], st_mlp[2], st_loss[1],
                 pad(jnp.sum(st_pool, axis=0)), pad(dsink[0, :N_Q_HEADS])] + [zrow] * 3
    small_all = _allgather8("gather_small", jnp.concatenate(mine_rows).reshape(len(mine_rows), d))
    small = small_all[0]
    for i in range(1, 8):
        small = small + small_all[i]
    dm_rows = small_all[:, 0:6, :].reshape(8, 6 * d)
    dm_ctx = jnp.concatenate([small[6], small[7], jnp.zeros((4 * d,), F32)])[None, :]
    dm = jnp.concatenate([dm_rows, jnp.pad(dm_ctx, ((0, 7), (0, 0)))], axis=0)
    g_bada = jnp.sum(dm[:9], axis=0, keepdims=True)
    dm_shard = lax.dynamic_slice_in_dim(dm, chip * n_ada, n_ada, axis=1)
    g_wada, dl_wada, nm_wada, nv_wada, part_cctx = _ada_bwd(cond, dm_shard, w_ada[0], m_w_ada[0], v_w_ada[0])
    cctx_all = _allgather8("gather_cctx", part_cctx)
    dsilu_in = cctx_all[0, 0] + cctx_all[2, 0] + cctx_all[4, 0] + cctx_all[6, 0]
    sig = 1.0 / (1.0 + jnp.exp(-c_ctx))
    g_cctx = dsilu_in * (sig * (1.0 + c_ctx * (1.0 - sig)))

    g_shards = g_mixer + g_mlp
    big_w = [w_in, pool_w, w_out, w_mlp_up, w_mlp_down]
    big_m = [m_w_in, m_pool_w, m_w_out, m_w_mlp_up, m_w_mlp_down]
    big_v = [v_w_in, v_pool_w, v_w_out, v_w_mlp_up, v_w_mlp_down]
    big_names = ["w_in", "pool_w", "w_out", "w_mlp_up", "w_mlp_down"]
    res = {}
    for nm, w_, g_, m_, v_ in zip(big_names, big_w, g_shards, big_m, big_v):
        res[nm] = tuple(_adamw("adamw_" + nm, w_, g_.reshape(w_.shape), m_, v_))
    res["w_ada"] = (g_wada[None], dl_wada[None], nm_wada[None], nv_wada[None])

    def pack(cc, na, nm_, ba, sk, ps, fn):
        flat = [cc.reshape(-1), na.reshape(-1), nm_.reshape(-1), ba.reshape(-1), pad(sk.reshape(-1)), pad(ps.reshape(-1)),
                fn.reshape(-1), jnp.zeros((4 * d,), F32)]
        return jnp.concatenate(flat).reshape(16, d)

    w_s = pack(c_ctx, norm_attn_w, norm_mlp_w, b_ada, attn_sink, pool_scale, final_norm_w)
    m_s = pack(m_c_ctx, m_norm_attn_w, m_norm_mlp_w, m_b_ada, m_attn_sink, m_pool_scale, m_final_norm_w)
    v_s = pack(v_c_ctx, v_norm_attn_w, v_norm_mlp_w, v_b_ada, v_attn_sink, v_pool_scale, v_final_norm_w)
    g_s = pack(g_cctx, small[8], small[9], g_bada, small[12][:N_Q_HEADS], small[11][:POOL_WIDTH], small[10])
    small_out = _adamw("adamw_small", w_s, g_s, m_s, v_s)

    def unpack(p):
        return {"c_ctx": p[0], "norm_attn_w": p[1:2], "norm_mlp_w": p[2:3], "b_ada": p[3:9].reshape(1, 6 * d),
                "attn_sink": p[9:10, :N_Q_HEADS], "pool_scale": p[10:11, :POOL_WIDTH], "final_norm_w": p[11]}

    small_res = [unpack(p) for p in small_out]
    order = ["c_ctx", "norm_attn_w", "norm_mlp_w", "w_ada", "b_ada", "w_in", "attn_sink", "pool_w", "pool_scale",
             "w_out", "w_mlp_up", "w_mlp_down", "final_norm_w"]
    outs = [loss, grad_x.reshape(x.shape)]
    for kind in range(4):
        for nm in order:
            outs.append(res[nm][kind] if nm in res else small_res[kind][nm])
    return tuple(outs)
```

```python
import functools

import jax
import jax.numpy as jnp
from jax import lax
from jax.experimental import pallas as pl
from jax.experimental.pallas import tpu as pltpu

F32 = jnp.float32
BF16 = jnp.bfloat16
EPS = 1e-6
NEG_INF = -1e30
HEAD_DIM = 64
N_Q_HEADS = 16
N_KV_HEADS = 4
GROUP = N_Q_HEADS // N_KV_HEADS
ATTN_WIDTH = N_Q_HEADS * HEAD_DIM
KV_WIDTH = N_KV_HEADS * HEAD_DIM
POOL_WINDOWS = (2, 4, 8, 16)
POOL_GROUP_DIM = 256
POOL_WIDTH = len(POOL_WINDOWS) * POOL_GROUP_DIM
BLOCK = 128
GRID_W = 64
ROPE_BASE = 10000.0
SCALE = HEAD_DIM ** -0.5
HALO = 16
STRIP = 16
ADAM_LR, ADAM_B1, ADAM_B2, ADAM_EPS, ADAM_WD, ADAM_STEP = 0.001, 0.9, 0.999, 1e-08, 0.01, 10
MESH = pl.DeviceIdType.MESH
MIB = 1024 * 1024
ANY = pl.BlockSpec(memory_space=pl.ANY)


def _cp(n_axes, vmem_mib=48):
    return pltpu.CompilerParams(dimension_semantics=("arbitrary",) * n_axes, vmem_limit_bytes=vmem_mib * MIB)


def _fold8(v):
    s = v[0:8]
    for t in range(1, v.shape[0] // 8):
        s = s + v[8 * t:8 * t + 8]
    return s


def _dot(a, b):
    return jnp.dot(a, b, preferred_element_type=F32)


def _dot_nt(a, b):
    return lax.dot_general(a, b, (((1,), (1,)), ((), ())), preferred_element_type=F32)


def _dot_tn(a, b):
    return lax.dot_general(a, b, (((0,), (0,)), ((), ())), preferred_element_type=F32)


def _pick(n, *cands):
    for t in cands:
        if n % t == 0:
            return t
    return n


def _flip(pos, mask):
    return tuple((1 - v) if (mask >> (2 - i)) & 1 else v for i, v in enumerate(pos))


def _exchange(name, ins, out_shapes, remote, local=(), aliases=None):
    n_io = len(ins) + len(out_shapes)

    def body(*refs):
        io = refs[:n_io]
        send_sems, recv_sems, local_sems = refs[n_io:]
        me = (lax.axis_index("x"), lax.axis_index("y"), lax.axis_index("c"))

        def copy(i, sender):
            mask, src_fn, dst_fn = remote[i]
            return pltpu.make_async_remote_copy(
                src_ref=src_fn(io, sender), dst_ref=dst_fn(io, sender), send_sem=send_sems.at[i],
                recv_sem=recv_sems.at[i], device_id=_flip(sender, mask), device_id_type=MESH)

        own = [pltpu.make_async_copy(s(io, me), d(io, me), local_sems.at[i]) for i, (s, d) in enumerate(local)]
        for cp in own:
            cp.start()
        sends = [copy(i, me) for i in range(len(remote))]
        for cp in sends:
            cp.start()
        for i in range(len(remote)):
            copy(i, _flip(me, remote[i][0])).wait_recv()
        for cp in sends:
            cp.wait_send()
        for cp in own:
            cp.wait()

    return pl.pallas_call(
        body, name=name, out_shape=tuple(out_shapes),
        in_specs=[ANY] * len(ins), out_specs=tuple([ANY] * len(out_shapes)),
        scratch_shapes=[pltpu.SemaphoreType.DMA((len(remote),)), pltpu.SemaphoreType.DMA((len(remote),)),
                        pltpu.SemaphoreType.DMA((max(len(local), 1),))],
        input_output_aliases=aliases or {},
    )(*ins)


HBM = pl.BlockSpec(memory_space=pltpu.HBM)
SEM = pl.BlockSpec(memory_space=pltpu.SEMAPHORE)
EFFECT = pltpu.SideEffectType.DATAFLOW_SIDE_EFFECTING


def _split_copy(remote, i, io, send_sems, recv_sems, sender):
    mask, src_fn, dst_fn = remote[i]
    return pltpu.make_async_remote_copy(
        src_ref=src_fn(io, sender), dst_ref=dst_fn(io, sender), send_sem=send_sems.at[i],
        recv_sem=recv_sems.at[i], device_id=_flip(sender, mask), device_id_type=MESH)


def _exchange_start(name, bufs, remote):
    n, r = len(bufs), len(remote)

    def body(*refs):
        io, send_sems, recv_sems, token = refs[:n], refs[2 * n], refs[2 * n + 1], refs[2 * n + 2]
        me = (lax.axis_index("x"), lax.axis_index("y"), lax.axis_index("c"))
        for i in range(r):
            _split_copy(remote, i, io, send_sems, recv_sems, me).start()
        token[...] = jnp.zeros_like(token)

    res = pl.pallas_call(
        body, name=name,
        out_shape=tuple(pltpu.HBM(b.shape, b.dtype) for b in bufs)
        + (pltpu.SemaphoreType.DMA((r,)), pltpu.SemaphoreType.DMA((r,)), jax.ShapeDtypeStruct((8, 128), F32)),
        in_specs=[HBM] * n, out_specs=tuple([HBM] * n) + (SEM, SEM, pl.BlockSpec(memory_space=pltpu.VMEM)),
        input_output_aliases={i: i for i in range(n)}, compiler_params=pltpu.CompilerParams(has_side_effects=EFFECT),
    )(*[pltpu.with_memory_space_constraint(b, pltpu.HBM) for b in bufs])
    return list(res[:n]), res[n], res[n + 1], res[n + 2]


def _exchange_wait(name, bufs, send_sems, recv_sems, remote, after):
    n, r = len(bufs), len(remote)

    def body(*refs):
        io, ss, rs = refs[:n], refs[n], refs[n + 1]
        me = (lax.axis_index("x"), lax.axis_index("y"), lax.axis_index("c"))
        for i in range(r):
            _split_copy(remote, i, io, ss, rs, _flip(me, remote[i][0])).wait_recv()
        for i in range(r):
            _split_copy(remote, i, io, ss, rs, me).wait_send()

    return list(pl.pallas_call(
        body, name=name, out_shape=tuple(pltpu.HBM(b.shape, b.dtype) for b in bufs),
        in_specs=[HBM] * n + [SEM, SEM, ANY], out_specs=tuple([HBM] * n),
        input_output_aliases={i: i for i in range(n)}, compiler_params=pltpu.CompilerParams(has_side_effects=EFFECT),
    )(*bufs, send_sems, recv_sems, after))


def _my_c():
    return lax.axis_index("c")


def _my_chip():
    return 2 * lax.axis_index("x") + lax.axis_index("y")


def _dev_index(pos):
    return 4 * pos[0] + 2 * pos[1] + pos[2]


def _chip_index(pos):
    return 2 * pos[0] + pos[1]


def _allgather8(name, v):
    out = jax.ShapeDtypeStruct((8,) + v.shape, v.dtype)
    remote = [(mask, lambda io, pos: io[0], lambda io, pos: io[1].at[_dev_index(pos)]) for mask in range(1, 8)]
    local = [(lambda io, pos: io[0], lambda io, pos: io[1].at[_dev_index(pos)])]
    return _exchange(name, [v], [out], remote, local)[0]


class _Big:
    def __init__(self, kind, shard_shape):
        self.kind = kind
        self.shard_shape = tuple(shard_shape)
        if kind == "col":
            r, cs = shard_shape
            self.full_shape = (r, 4 * cs)
            self.piece_shape = (r // 2, cs)
            self.half_shape = (r // 2, 4 * cs)
        elif kind == "row":
            rs, c = shard_shape
            self.full_shape = (4, 2, rs // 2, c)
            self.piece_shape = (1, 1, rs // 2, c)
            self.half_shape = (4, 1, rs // 2, c)
        else:
            self.full_shape = (4, 256, 256)
            self.piece_shape = (2, 64, 256)
            self.half_shape = (2, 256, 256)

    def shard_as_pieces(self, a):
        return a.reshape((1, 2) + self.piece_shape[2:]) if self.kind == "row" else a

    def piece(self, ref, k, h):
        if self.kind == "col":
            r, cs = self.piece_shape
            return ref.at[pl.ds(h * r, r), pl.ds(k * cs, cs)]
        if self.kind == "row":
            return ref.at[pl.ds(k, 1), pl.ds(h, 1)]
        return ref.at[pl.ds(2 * h, 2), pl.ds(64 * k, 64)]

    def half_of_shard(self, ref, h):
        if self.kind == "col":
            return ref.at[pl.ds(h * self.piece_shape[0], self.piece_shape[0])]
        if self.kind == "row":
            return ref.at[:, pl.ds(h, 1)]
        return ref.at[pl.ds(2 * h, 2)]

    def half_of_full(self, ref, h):
        if self.kind == "col":
            return ref.at[pl.ds(h * self.half_shape[0], self.half_shape[0])]
        if self.kind == "row":
            return ref.at[:, pl.ds(h, 1)]
        return ref.at[pl.ds(2 * h, 2)]

    def piece_of_half(self, ref, k):
        if self.kind == "col":
            return ref.at[:, pl.ds(k * self.piece_shape[1], self.piece_shape[1])]
        if self.kind == "row":
            return ref.at[pl.ds(k, 1)]
        return ref.at[:, pl.ds(64 * k, 64)]


CHIP_MASKS = (4, 2, 6)


def _cast_place(name, big, shard, after):
    if big.kind == "col":
        r, cs = big.shard_shape
        tr = _pick(r, 512, 256, 128)
        src, grid, blk = shard, (r // tr,), (tr, cs)
        imap, omap = (lambda i: (i, 0)), (lambda i: (i, _my_chip()))
    elif big.kind == "row":
        rs, c = big.shard_shape
        tr = _pick(rs // 2, 256, 128)
        src, grid, blk = big.shard_as_pieces(shard), (2, rs // 2 // tr), (1, 1, tr, c)
        imap, omap = (lambda h, i: (0, h, i, 0)), (lambda h, i: (_my_chip(), h, i, 0))
    else:
        src, grid, blk = shard, (1,), big.shard_shape
        imap, omap = (lambda i: (0, 0, 0)), (lambda i: (0, _my_chip(), 0))

    def body(s_ref, after_ref, o_ref):
        o_ref[...] = s_ref[...].astype(BF16)

    return pl.pallas_call(
        body, name=name, grid=grid, in_specs=[pl.BlockSpec(blk, imap), ANY], out_specs=pl.BlockSpec(blk, omap),
        out_shape=jax.ShapeDtypeStruct(big.full_shape, BF16), compiler_params=_cp(len(grid)),
    )(src, after)


def _gather_ici_remote(bigs, off):
    remote = []
    for a, b in enumerate(bigs):
        for mask in CHIP_MASKS:
            def mine(io, p, a=a, b=b):
                return b.piece(io[off + a], _chip_index(p), p[2])
            remote.append((mask, mine, mine))
    return remote


def _gather_d2d_remote(bigs, off):
    remote = []
    for a, b in enumerate(bigs):
        for mask in CHIP_MASKS:
            def region(io, p, a=a, b=b, mask=mask):
                return b.piece(io[off + a], _chip_index(_flip(p, mask)), p[2])
            remote.append((1, region, region))
    return remote


def _ew(name, fn, ins, out_dtypes, rows_per_step=256):
    shape = ins[0].shape
    last = shape[-1]
    rows = 1
    for s in shape[:-1]:
        rows *= s
    ins2 = [a.reshape(rows, last) for a in ins]
    tr = _pick(rows, rows_per_step, 128, 64, 32, 16, 8)
    spec = pl.BlockSpec((tr, last), lambda i: (i, 0))

    def body(*refs):
        outs = fn(*[r[...] for r in refs[:len(ins)]])
        for o_ref, o in zip(refs[len(ins):], outs):
            o_ref[...] = o.astype(o_ref.dtype)

    outs = pl.pallas_call(
        body, name=name, grid=(rows // tr,), in_specs=[spec] * len(ins), out_specs=tuple([spec] * len(out_dtypes)),
        out_shape=tuple(jax.ShapeDtypeStruct((rows, last), d) for d in out_dtypes), compiler_params=_cp(1),
    )(*ins2)
    return [o.reshape(shape) for o in outs]


def _chip_sum(name, big, grad, from_sibling):
    if big.kind == "col":
        rh, w = big.half_shape
        tr = _pick(rh, 256, 128)
        nb = rh // tr
        grid, blk = (nb,), (tr, w)
        gmap, hmap = (lambda i: (_my_c() * nb + i, 0)), (lambda i: (i, 0))
    elif big.kind == "row":
        rh, w = big.half_shape[2:]
        tr = _pick(rh, 256, 128)
        grid, blk = (4, rh // tr), (1, 1, tr, w)
        gmap, hmap = (lambda k, i: (k, _my_c(), i, 0)), (lambda k, i: (k, 0, i, 0))
    else:
        grid, blk = (1,), big.half_shape
        gmap, hmap = (lambda i: (_my_c(), 0, 0)), (lambda i: (0, 0, 0))

    def body(g_ref, s_ref, o_ref):
        o_ref[...] = (g_ref[...].astype(F32) + s_ref[...].astype(F32)).astype(BF16)

    return pl.pallas_call(
        body, name=name, grid=grid, in_specs=[pl.BlockSpec(blk, gmap), pl.BlockSpec(blk, hmap)],
        out_specs=pl.BlockSpec(blk, hmap), out_shape=jax.ShapeDtypeStruct(big.half_shape, BF16), compiler_params=_cp(len(grid)),
    )(grad, from_sibling)


def _piece_sum(name, big, chip_sum, thirds):
    if big.kind == "col":
        rp, cs = big.piece_shape
        tr = _pick(rp, 256, 128)
        nb = rp // tr
        grid, blk, tblk = (nb,), (tr, cs), (1, tr, cs)
        smap, omap = (lambda i: (i, _my_chip())), (lambda i: (_my_c() * nb + i, 0))
        tmap = lambda j: (lambda i: (j, i, 0))
        out_shape = big.shard_shape
    elif big.kind == "row":
        rp, w = big.piece_shape[2:]
        tr = _pick(rp, 256, 128)
        grid, blk, tblk = (rp // tr,), (1, 1, tr, w), (1, 1, 1, tr, w)
        smap, omap = (lambda i: (_my_chip(), 0, i, 0)), (lambda i: (0, _my_c(), i, 0))
        tmap = lambda j: (lambda i: (j, 0, 0, i, 0))
        out_shape = (1, 2, rp, w)
    else:
        grid, blk, tblk = (1,), big.piece_shape, (1,) + big.piece_shape
        smap, omap = (lambda i: (0, _my_chip(), 0)), (lambda i: (_my_c(), 0, 0))
        tmap = lambda j: (lambda i: (j, 0, 0, 0))
        out_shape = big.shard_shape

    def body(s_ref, t0, t1, t2, o_ref):
        o_ref[...] = s_ref[...].astype(F32) + t0[0].astype(F32) + t1[0].astype(F32) + t2[0].astype(F32)

    return pl.pallas_call(
        body, name=name, grid=grid,
        in_specs=[pl.BlockSpec(blk, smap)] + [pl.BlockSpec(tblk, tmap(j)) for j in range(3)],
        out_specs=pl.BlockSpec(blk, omap), out_shape=jax.ShapeDtypeStruct(out_shape, F32), compiler_params=_cp(len(grid)),
    )(chip_sum, thirds, thirds, thirds)


def _split(name, bufs, remote):
    return _exchange_start(name + "_start", bufs, remote) + (remote, name)


def _join(handle, after):
    bufs, send_sems, recv_sems, _, remote, name = handle
    return _exchange_wait(name + "_wait", bufs, send_sems, recv_sems, remote, after)


def _reduce_d2d_remote(bigs):
    n = len(bigs)
    return [(1, lambda io, p, a=a, b=b: b.half_of_full(io[a], 1 - p[2]), lambda io, p, a=a: io[n + a])
            for a, b in enumerate(bigs)]


def _halves(bigs):
    return [jax.ShapeDtypeStruct(b.half_shape, BF16) for b in bigs]


def _chip_sums(tag, bigs, grads, from_sibling):
    return [_chip_sum(f"reduce_{tag}_chip_sum_{a}", b, g, r) for a, (b, g, r) in enumerate(zip(bigs, grads, from_sibling))]


def _reduce_to_chip(tag, bigs, grads):
    from_sibling = _exchange(f"reduce_{tag}_d2d", grads, _halves(bigs), _reduce_d2d_remote(bigs))
    return _chip_sums(tag, bigs, grads, from_sibling)


def _reduce_ici_remote(bigs):
    n = len(bigs)
    remote = []
    for a, b in enumerate(bigs):
        for j, mask in enumerate(CHIP_MASKS):
            remote.append((mask,
                           lambda io, p, a=a, b=b, mask=mask: b.piece_of_half(io[a], _chip_index(_flip(p, mask))),
                           lambda io, p, a=a, j=j: io[n + a].at[j]))
    return remote


def _thirds(bigs):
    return [jax.ShapeDtypeStruct((3,) + b.piece_shape, BF16) for b in bigs]


def _piece_sums(tag, bigs, chip_sum, from_chips):
    return [_piece_sum(f"reduce_{tag}_sum_{a}", b, s, r) for a, (b, s, r) in enumerate(zip(bigs, chip_sum, from_chips))]


def _share_remote(bigs, off):
    remote = []
    for a, b in enumerate(bigs):
        def mine(io, p, a=a, b=b):
            return b.half_of_shard(io[off + a], p[2])
        remote.append((1, mine, mine))
    return remote


def _reduce_finish(tag, bigs, chip_sum, from_chips):
    n = len(bigs)
    placed = _piece_sums(tag, bigs, chip_sum, from_chips)
    out = _exchange(f"reduce_{tag}_share_d2d", placed, [jax.ShapeDtypeStruct(p.shape, F32) for p in placed],
                    _share_remote(bigs, n), aliases={a: a for a in range(n)})
    return [o.reshape(b.shard_shape) for o, b in zip(out, bigs)]


def _mm(name, a, b, *, nt, tm, tn, tk, epi, extras=(), extra_specs=(), out_shape, out_specs, after=None, vmem_mib=48):
    m, kdim = a.shape
    n = b.shape[0] if nt else b.shape[1]
    gm, gn, gk = m // tm, n // tn, kdim // tk
    a_spec = pl.BlockSpec((tm, tk), lambda j, i, k: (i, k))
    b_spec = pl.BlockSpec((tn, tk), lambda j, i, k: (j, k)) if nt else pl.BlockSpec((tk, tn), lambda j, i, k: (k, j))
    n_ex = len(extras)
    if after is not None:
        extras, extra_specs = tuple(extras) + (after,), list(extra_specs) + [ANY]

    def body(a_ref, b_ref, *rest):
        ex, outs, acc = rest[:n_ex], rest[len(extras):-1], rest[-1]
        dot = _dot_nt if nt else _dot
        if gk == 1:
            acc[...] = dot(a_ref[...], b_ref[...])
            epi(acc, ex, outs)
        else:
            k = pl.program_id(2)

            @pl.when(k == 0)
            def _():
                acc[...] = dot(a_ref[...], b_ref[...])

            @pl.when(k > 0)
            def _():
                acc[...] += dot(a_ref[...], b_ref[...])

            @pl.when(k == gk - 1)
            def _():
                epi(acc, ex, outs)

    return pl.pallas_call(
        body, name=name, grid=(gn, gm, gk), in_specs=[a_spec, b_spec, *extra_specs], out_specs=tuple(out_specs),
        out_shape=tuple(out_shape), scratch_shapes=[pltpu.VMEM((tm, tn), F32)], compiler_params=_cp(3, vmem_mib),
    )(a, b, *extras)


def _mm_deferred(name, a, b, *, nt, tm, epi, tiles, vecs, out_tiles, n_stats, after, vmem_mib=48):
    m, kdim = a.shape
    n = b.shape[0] if nt else b.shape[1]
    gm = m // tm
    n_t, n_v, n_o = len(tiles), len(vecs), len(out_tiles)
    dot = _dot_nt if nt else _dot

    def body(a_ref, b_ref, *rest):
        t_refs, v_refs = rest[:n_t], rest[n_t:n_t + n_v]
        o_refs, st_ref, acc0, acc1 = rest[n_t + n_v + 1:n_t + n_v + 1 + n_o], rest[-3], rest[-2], rest[-1]
        i = pl.program_id(0)

        @pl.when(i == 0)
        def _():
            acc1[...] = jnp.zeros_like(acc1)
            st_ref[...] = jnp.zeros_like(st_ref)

        def finish(prev):
            for r0 in range(0, tm, STRIP):
                rs = slice(r0, r0 + STRIP)
                epi(prev[rs, :], rs, t_refs, v_refs, o_refs, st_ref, i > 0)

        @pl.when((i % 2 == 0) & (i < gm))
        def _():
            acc0[...] = dot(a_ref[...], b_ref[...])
            finish(acc1)

        @pl.when((i % 2 == 1) & (i < gm))
        def _():
            acc1[...] = dot(a_ref[...], b_ref[...])
            finish(acc0)

        @pl.when(i == gm)
        def _():
            finish(acc1 if gm % 2 == 0 else acc0)

    prev = lambda i: (jnp.maximum(i - 1, 0), 0)
    tile = pl.BlockSpec((tm, n), prev)
    return pl.pallas_call(
        body, name=name, grid=(gm + 1,),
        in_specs=[pl.BlockSpec((tm, kdim), lambda i: (jnp.minimum(i, gm - 1), 0)), pl.BlockSpec(b.shape, lambda i: (0, 0))]
        + [tile] * n_t + [_row_spec(n)] * n_v + [ANY],
        out_specs=tuple([tile] * n_o) + (_stat_spec(n_stats, n),),
        out_shape=tuple(out_tiles) + (jax.ShapeDtypeStruct((n_stats, 8, n), F32),),
        scratch_shapes=[pltpu.VMEM((tm, n), F32), pltpu.VMEM((tm, n), F32)], compiler_params=_cp(1, vmem_mib),
    )(a, b, *tiles, *vecs, after)


def _mm_k_deferred(name, a, b, *, nt, tm, tk, epi, tiles, vecs, out_tiles, n_stats, after, vmem_mib=56):
    m, kdim = a.shape
    n = b.shape[0] if nt else b.shape[1]
    gm, gk = m // tm, kdim // tk
    rows = tm // gk
    n_t, n_v, n_o = len(tiles), len(vecs), len(out_tiles)
    dot = _dot_nt if nt else _dot

    def body(a_ref, b_ref, *rest):
        t_refs, v_refs = rest[:n_t], rest[n_t:n_t + n_v]
        o_refs, st_ref, acc0, acc1 = rest[n_t + n_v + 1:n_t + n_v + 1 + n_o], rest[-3], rest[-2], rest[-1]
        i, k = pl.program_id(0), pl.program_id(1)

        @pl.when((i == 0) & (k == 0))
        def _():
            acc1[...] = jnp.zeros_like(acc1)
            st_ref[...] = jnp.zeros_like(st_ref)

        def finish(prev):
            for r0 in range(0, rows, STRIP):
                acc_rows = prev[pl.ds(pl.multiple_of(k * rows + r0, STRIP), STRIP), :]
                epi(acc_rows, slice(r0, r0 + STRIP), t_refs, v_refs, o_refs, st_ref, i > 0)

        def step(cur, prev):
            cur[...] = jnp.where(k > 0, cur[...], 0.0) + dot(a_ref[...], b_ref[...])
            finish(prev)

        @pl.when((i % 2 == 0) & (i < gm))
        def _():
            step(acc0, acc1)

        @pl.when((i % 2 == 1) & (i < gm))
        def _():
            step(acc1, acc0)

        @pl.when(i == gm)
        def _():
            finish(acc1 if gm % 2 == 0 else acc0)

    prev = lambda i, k: (jnp.where(i == 0, 0, (i - 1) * gk + k), 0)
    part = pl.BlockSpec((rows, n), prev)
    b_spec = pl.BlockSpec((n, tk), lambda i, k: (0, k)) if nt else pl.BlockSpec((tk, n), lambda i, k: (k, 0))
    return pl.pallas_call(
        body, name=name, grid=(gm + 1, gk),
        in_specs=[pl.BlockSpec((tm, tk), lambda i, k: (jnp.minimum(i, gm - 1), k)), b_spec]
        + [part] * n_t + [_row_spec(n)] * n_v + [ANY],
        out_specs=tuple([part] * n_o) + (_stat_spec(n_stats, n),),
        out_shape=tuple(out_tiles) + (jax.ShapeDtypeStruct((n_stats, 8, n), F32),),
        scratch_shapes=[pltpu.VMEM((tm, n), F32), pltpu.VMEM((tm, n), F32)], compiler_params=_cp(2, vmem_mib),
    )(a, b, *tiles, *vecs, after)


def _mm_tn(name, a, b, out_dtype, *, tmo, tn, tt, more=(), vmem_mib=56):
    t, m = a.shape
    n = b.shape[1]
    gt = t // tt

    def body(a_ref, b_ref, *rest):
        o_ref, acc = rest[-2:]
        k = pl.program_id(2)

        @pl.when(k == 0)
        def _():
            first = _dot_tn(a_ref[...], b_ref[...])
            acc[...] = first + _dot_tn(rest[0][...], rest[1][...]) if more else first

        @pl.when(k > 0)
        def _():
            acc[...] += _dot_tn(a_ref[...], b_ref[...])

        @pl.when(k == gt - 1)
        def _():
            o_ref[...] = acc[...].astype(o_ref.dtype)

    more_specs = [pl.BlockSpec((more[0].shape[0], tmo), lambda i, j, k: (0, i)),
                  pl.BlockSpec((more[1].shape[0], tn), lambda i, j, k: (0, j))] if more else []
    return pl.pallas_call(
        body, name=name, grid=(m // tmo, n // tn, gt),
        in_specs=[pl.BlockSpec((tt, tmo), lambda i, j, k: (k, i)), pl.BlockSpec((tt, tn), lambda i, j, k: (k, j))] + more_specs,
        out_specs=pl.BlockSpec((tmo, tn), lambda i, j, k: (i, j)), out_shape=jax.ShapeDtypeStruct((m, n), out_dtype),
        scratch_shapes=[pltpu.VMEM((tmo, tn), F32)], compiler_params=_cp(3, vmem_mib),
    )(a, b, *more)


def _row_spec(d):
    return pl.BlockSpec((1, d), lambda *_: (0, 0))


def _stat_spec(k, d):
    return pl.BlockSpec((k, 8, d), lambda *_: (0, 0, 0))


def _rope(z, cs, sn):
    first = (lax.broadcasted_iota(jnp.int32, (z.shape[0], 128), 1) % 32) < 16
    outs = []
    for j in range(z.shape[1] // 128):
        zc = z[:, 128 * j:128 * (j + 1)]
        partner = jnp.where(first, pltpu.roll(zc, 112, 1), pltpu.roll(zc, 16, 1))
        outs.append(zc * cs + partner * sn)
    return outs[0] if len(outs) == 1 else jnp.concatenate(outs, axis=1)


def _rope_tables(length, rotate):
    if not rotate:
        return jnp.ones((length, 128), F32), jnp.zeros((length, 128), F32)
    half = HEAD_DIM // 2
    t = jnp.arange(length)
    row = (t // GRID_W).astype(F32)
    col = (t % GRID_W).astype(F32)
    e = jnp.arange(128) % HEAD_DIM
    inv_freq = ROPE_BASE ** (-(2 * ((e % half) % (half // 2))).astype(F32) / half)
    pos = jnp.where(e[None, :] < half, row[:, None], col[:, None])
    ang = pos * inv_freq[None, :]
    first = ((e % half) < half // 2)[None, :]
    return jnp.cos(ang), jnp.where(first, -jnp.sin(ang), jnp.sin(ang))


def _mixer_in(name, x, nw, sh, sc, w_in, cos, sin, after):
    t, d = x.shape
    tm = _pick(t, 256, 128)
    n_in = w_in.shape[1]

    def body(x_ref, nw_ref, sh_ref, sc_ref, w_ref, cos_ref, sin_ref, after_ref, h_ref, q_ref, k_ref, v_ref, u_ref):
        xf = x_ref[...]
        r = lax.rsqrt(jnp.mean(xf * xf, axis=-1, keepdims=True) + EPS)
        hb = (((xf * r) * nw_ref[...]) * (1.0 + sc_ref[...]) + sh_ref[...]).astype(BF16)
        h_ref[...] = hb
        p = _dot(hb, w_ref[...])
        cs, sn = cos_ref[...], sin_ref[...]
        q_ref[...] = (_rope(p[:, :ATTN_WIDTH], cs, sn) * SCALE).astype(BF16)
        k_ref[...] = _rope(p[:, ATTN_WIDTH:ATTN_WIDTH + KV_WIDTH], cs, sn).astype(BF16)
        v_ref[...] = p[:, ATTN_WIDTH + KV_WIDTH:ATTN_WIDTH + 2 * KV_WIDTH].astype(BF16)
        u_ref[...] = p[:, ATTN_WIDTH + 2 * KV_WIDTH:]

    def tile(w):
        return pl.BlockSpec((tm, w), lambda i: (i, 0))

    return pl.pallas_call(
        body, name=name, grid=(t // tm,),
        in_specs=[tile(d), _row_spec(d), _row_spec(d), _row_spec(d), pl.BlockSpec((d, n_in), lambda i: (0, 0)),
                  tile(128), tile(128), ANY],
        out_specs=(tile(d), tile(ATTN_WIDTH), tile(KV_WIDTH), tile(KV_WIDTH), tile(POOL_WIDTH)),
        out_shape=(jax.ShapeDtypeStruct((t, d), BF16), jax.ShapeDtypeStruct((t, ATTN_WIDTH), BF16),
                   jax.ShapeDtypeStruct((t, KV_WIDTH), BF16), jax.ShapeDtypeStruct((t, KV_WIDTH), BF16),
                   jax.ShapeDtypeStruct((t, POOL_WIDTH), F32)),
        compiler_params=_cp(1),
    )(x, nw, sh, sc, w_in, cos, sin, after)


def _attn_specs(nb, n_ctx):
    def blk(w, f):
        return pl.BlockSpec((BLOCK, w), lambda n: (f(n), 0))

    prev = lambda n: jnp.maximum(n - 1, 0)
    cur = lambda n: n
    nxt = lambda n: jnp.minimum(n + 1, nb - 1)
    kv = [blk(KV_WIDTH, prev), blk(KV_WIDTH, cur), blk(KV_WIDTH, nxt)]
    ctx = pl.BlockSpec((n_ctx, KV_WIDTH), lambda n: (0, 0))
    return [pl.BlockSpec(memory_space=pltpu.SMEM), blk(ATTN_WIDTH, cur)] + kv + kv + [ctx, ctx]


def _attn_mask(n, length, n_keys):
    row = lax.broadcasted_iota(jnp.int32, (GROUP * BLOCK, n_keys), 0) % BLOCK
    col = lax.broadcasted_iota(jnp.int32, (GROUP * BLOCK, n_keys), 1)
    kpos = (n - 1) * BLOCK + col
    return ((jnp.abs(col - BLOCK - row) <= BLOCK) & (kpos >= 0) & (kpos < length)) | (col >= 3 * BLOCK)


def _group_rows(block, g):
    return jnp.concatenate([block[:, HEAD_DIM * h:HEAD_DIM * (h + 1)] for h in range(GROUP * g, GROUP * (g + 1))], axis=0)


def _group_sink(sink_ref, g):
    head = lax.broadcasted_iota(jnp.int32, (GROUP * BLOCK, 1), 0) // BLOCK
    out = jnp.full((GROUP * BLOCK, 1), sink_ref[0, GROUP * g], F32)
    for j in range(1, GROUP):
        out = jnp.where(head == j, sink_ref[0, GROUP * g + j], out)
    return out


def _attn_fwd(q, k, v, kc, vc, sink):
    length = q.shape[0]
    nb = length // BLOCK
    n_ctx = kc.shape[0]
    n_keys = 3 * BLOCK + n_ctx

    def body(sink_ref, q_ref, kp, k0, kn, vp, v0, vn, kc_ref, vc_ref, o_ref, p_ref):
        n = pl.program_id(0)
        valid = _attn_mask(n, length, n_keys)
        qb = q_ref[...]
        kall = jnp.concatenate([kp[...], k0[...], kn[...], kc_ref[...]], axis=0)
        vall = jnp.concatenate([vp[...], v0[...], vn[...], vc_ref[...]], axis=0)
        outs = []
        for g in range(N_KV_HEADS):
            lanes = slice(HEAD_DIM * g, HEAD_DIM * (g + 1))
            s = jnp.where(valid, _dot_nt(_group_rows(qb, g), kall[:, lanes]), NEG_INF)
            sk = _group_sink(sink_ref, g)
            m = jnp.maximum(jnp.max(s, axis=-1, keepdims=True), sk)
            e = jnp.exp(s - m)
            e_sink = jnp.exp(sk - m)
            inv = 1.0 / (jnp.sum(e, axis=-1, keepdims=True) + e_sink)
            pb = (e * inv).astype(BF16)
            p_ref[0, g, :, :n_keys] = pb
            p_ref[0, g, :, n_keys:] = jnp.broadcast_to(e_sink * inv, (GROUP * BLOCK, 128)).astype(BF16)
            o = _dot(pb, vall[:, lanes])
            outs += [o[BLOCK * j:BLOCK * (j + 1)] for j in range(GROUP)]
        o_ref[...] = jnp.concatenate(outs, axis=1).astype(BF16)

    return pl.pallas_call(
        body, name="attn_fwd", grid=(nb,), in_specs=_attn_specs(nb, n_ctx),
        out_specs=(pl.BlockSpec((BLOCK, ATTN_WIDTH), lambda n: (n, 0)),
                   pl.BlockSpec((1, N_KV_HEADS, GROUP * BLOCK, n_keys + 128), lambda n: (n, 0, 0, 0))),
        out_shape=(jax.ShapeDtypeStruct((length, ATTN_WIDTH + POOL_WIDTH), BF16),
                   jax.ShapeDtypeStruct((nb, N_KV_HEADS, GROUP * BLOCK, n_keys + 128), BF16)), compiler_params=_cp(1),
    )(sink, q, k, k, k, v, v, v, kc, vc)


def _attn_bwd(q, k, v, kc, vc, sink, dmix, probs):
    length = q.shape[0]
    nb = length // BLOCK
    n_ctx = kc.shape[0]
    n_keys = 3 * BLOCK + n_ctx

    def body(sink_ref, q_ref, kp, k0, kn, vp, v0, vn, kc_ref, vc_ref, do_ref, p_ref,
             dq_ref, dkp_ref, dvp_ref, dkc_ref, dvc_ref, dsink_ref):
        n = pl.program_id(0)

        @pl.when(n == 0)
        def _():
            dkc_ref[...] = jnp.zeros_like(dkc_ref)
            dvc_ref[...] = jnp.zeros_like(dvc_ref)
            dsink_ref[...] = jnp.zeros_like(dsink_ref)

        qb, dob = q_ref[...], do_ref[...]
        kall = jnp.concatenate([kp[...], k0[...], kn[...], kc_ref[...]], axis=0)
        vall = jnp.concatenate([vp[...], v0[...], vn[...], vc_ref[...]], axis=0)
        srow = lax.broadcasted_iota(jnp.int32, (8, 128), 0)
        slane = lax.broadcasted_iota(jnp.int32, (8, 128), 1)
        dqs, dks, dvs = [], [], []
        dsink = jnp.zeros((8, 128), F32)
        for g in range(N_KV_HEADS):
            lanes = slice(HEAD_DIM * g, HEAD_DIM * (g + 1))
            kg, vg = kall[:, lanes], vall[:, lanes]
            qg, dog = _group_rows(qb, g), _group_rows(dob, g)
            pb = p_ref[0, g, :, :n_keys]
            p = pb.astype(F32)
            dp = _dot_nt(dog, vg)
            delta = jnp.sum(p * dp, axis=-1, keepdims=True)
            ds = (p * (dp - delta)).astype(BF16)
            dq = _dot(ds, kg) * SCALE
            dqs += [dq[BLOCK * j:BLOCK * (j + 1)] for j in range(GROUP)]
            dks.append(_dot_tn(ds, qg))
            dvs.append(_dot_tn(pb, dog))
            d_sink = p_ref[0, g, :, n_keys:].astype(F32)[:, :1] * delta
            for j in range(GROUP):
                total = -jnp.sum(d_sink[BLOCK * j:BLOCK * (j + 1)], axis=0, keepdims=True)
                dsink = dsink + jnp.where((srow == 0) & (slane == GROUP * g + j), total, 0.0)
        dq_ref[...] = jnp.concatenate(dqs, axis=1)
        dk = jnp.concatenate(dks, axis=1)
        dv = jnp.concatenate(dvs, axis=1)
        for j in range(3):
            dkp_ref[0, j] = dk[BLOCK * j:BLOCK * (j + 1)]
            dvp_ref[0, j] = dv[BLOCK * j:BLOCK * (j + 1)]
        dkc_ref[...] += dk[3 * BLOCK:]
        dvc_ref[...] += dv[3 * BLOCK:]
        dsink_ref[...] += dsink

    part = pl.BlockSpec((1, 3, BLOCK, KV_WIDTH), lambda n: (n, 0, 0, 0))
    ctx = pl.BlockSpec((n_ctx, KV_WIDTH), lambda n: (0, 0))
    return pl.pallas_call(
        body, name="attn_bwd", grid=(nb,),
        in_specs=_attn_specs(nb, n_ctx) + [pl.BlockSpec((BLOCK, ATTN_WIDTH), lambda n: (n, 0)),
                                           pl.BlockSpec((1,) + probs.shape[1:], lambda n: (n, 0, 0, 0))],
        out_specs=(pl.BlockSpec((BLOCK, ATTN_WIDTH), lambda n: (n, 0)), part, part, ctx, ctx,
                   pl.BlockSpec((8, 128), lambda n: (0, 0))),
        out_shape=(jax.ShapeDtypeStruct((length, ATTN_WIDTH), F32),
                   jax.ShapeDtypeStruct((nb, 3, BLOCK, KV_WIDTH), F32), jax.ShapeDtypeStruct((nb, 3, BLOCK, KV_WIDTH), F32),
                   jax.ShapeDtypeStruct((n_ctx, KV_WIDTH), F32), jax.ShapeDtypeStruct((n_ctx, KV_WIDTH), F32),
                   jax.ShapeDtypeStruct((8, 128), F32)),
        compiler_params=_cp(1),
    )(sink, q, k, k, k, v, v, v, kc, vc, dmix, probs)


def _assemble_dp(dq, dkp, dvp, du, cos, sin, after):
    length = dq.shape[0]
    nb = length // BLOCK

    def body(dq_ref, dka, dkb, dkc, dva, dvb, dvc, du_ref, cos_ref, sin_ref, after_ref, o_ref):
        n = pl.program_id(0)
        has_next = (n + 1 < nb).astype(F32)
        has_prev = (n > 0).astype(F32)
        cs, sn = cos_ref[...], -sin_ref[...]
        dk = dka[0, 0] * has_next + dkb[0, 0] + dkc[0, 0] * has_prev
        dv = dva[0, 0] * has_next + dvb[0, 0] + dvc[0, 0] * has_prev
        o_ref[:, :ATTN_WIDTH] = _rope(dq_ref[...], cs, sn).astype(BF16)
        o_ref[:, ATTN_WIDTH:ATTN_WIDTH + KV_WIDTH] = _rope(dk, cs, sn).astype(BF16)
        o_ref[:, ATTN_WIDTH + KV_WIDTH:ATTN_WIDTH + 2 * KV_WIDTH] = dv.astype(BF16)
        o_ref[:, ATTN_WIDTH + 2 * KV_WIDTH:] = du_ref[...]

    def part(slot, f):
        return pl.BlockSpec((1, 1, BLOCK, KV_WIDTH), lambda n: (f(n), slot, 0, 0))

    parts = [part(0, lambda n: jnp.minimum(n + 1, nb - 1)), part(1, lambda n: n), part(2, lambda n: jnp.maximum(n - 1, 0))]

    def tile(w):
        return pl.BlockSpec((BLOCK, w), lambda n: (n, 0))

    width = ATTN_WIDTH + 2 * KV_WIDTH + POOL_WIDTH
    return pl.pallas_call(
        body, name="assemble_dp", grid=(nb,),
        in_specs=[tile(ATTN_WIDTH)] + parts + parts + [tile(POOL_WIDTH), tile(128), tile(128), ANY],
        out_specs=tile(width), out_shape=jax.ShapeDtypeStruct((length, width), BF16), compiler_params=_cp(1),
    )(dq, dkp, dkp, dkp, dvp, dvp, dvp, du, cos, sin, after)


def _shift_rows(e, s):
    n = e.shape[0]
    return e if s % n == 0 else pltpu.roll(e, (-s) % n, 0)


def _window_sum(e, w, first):
    s, n = e, 1
    while n < w:
        s = s + _shift_rows(s, n)
        n *= 2
    return _shift_rows(s, first)


def _pool_geometry(i, tm, length):
    pos = i * tm - HALO + lax.broadcasted_iota(jnp.int32, (tm + 2 * HALO, 1), 0)
    inside = (pos >= 0) & (pos < length)
    inv_counts = []
    for w in POOL_WINDOWS:
        lo = jnp.clip(pos - w // 2, 0, length)
        hi = jnp.clip(pos - w // 2 + w, 0, length)
        inv_counts.append(1.0 / jnp.maximum(hi - lo, 1).astype(F32))
    return inside, inv_counts


def _halo_specs(tm, width, length, col=0):
    per = tm // HALO
    last = length // HALO - 1
    return [pl.BlockSpec((HALO, width), lambda i: (jnp.maximum(i * per - 1, 0), col)),
            pl.BlockSpec((tm, width), lambda i: (i, col)),
            pl.BlockSpec((HALO, width), lambda i: (jnp.minimum((i + 1) * per, last), col))]


def _pooled(ext, inv_counts, tm):
    outs = []
    for g, w in enumerate(POOL_WINDOWS):
        e = ext[:, POOL_GROUP_DIM * g:POOL_GROUP_DIM * (g + 1)]
        mean = _window_sum(e, w, -(w // 2)) * inv_counts[g]
        outs.append((mean - e)[HALO:HALO + tm])
    return outs


def _pool_fwd(u, pool_w, pool_scale, mix):
    length = u.shape[0]
    tm = _pick(length, 256, 128)

    def body(up, u0, un, w_ref, sc_ref, mix_ref, o_ref):
        inside, inv_counts = _pool_geometry(pl.program_id(0), tm, length)
        ext = jnp.where(inside, jnp.concatenate([up[...], u0[...], un[...]], axis=0), 0.0)
        pooled = _pooled(ext, inv_counts, tm)
        mixed = [_dot(pooled[g].astype(BF16), w_ref[g]) for g in range(len(POOL_WINDOWS))]
        o_ref[...] = (jnp.concatenate(mixed, axis=1) * sc_ref[...]).astype(BF16)

    return pl.pallas_call(
        body, name="pool_fwd", grid=(length // tm,),
        in_specs=_halo_specs(tm, POOL_WIDTH, length) + [pl.BlockSpec(pool_w.shape, lambda i: (0, 0, 0)), _row_spec(POOL_WIDTH), ANY],
        out_specs=pl.BlockSpec((tm, POOL_WIDTH), lambda i: (i, 1)),
        out_shape=jax.ShapeDtypeStruct(mix.shape, BF16), input_output_aliases={5: 0}, compiler_params=_cp(1),
    )(u, u, u, pool_w, pool_scale, mix)


def _pool_bwd(u, dmix, pool_w, pool_scale, after):
    length = u.shape[0]
    tm = _pick(length, 256, 128)
    n_g = len(POOL_WINDOWS)

    def body(up, u0, un, dp_, d0, dn_, w_ref, sc_ref, after_ref, du_ref, dw_ref, dsc_ref):
        i = pl.program_id(0)

        @pl.when(i == 0)
        def _():
            dw_ref[...] = jnp.zeros_like(dw_ref)
            dsc_ref[...] = jnp.zeros_like(dsc_ref)

        inside, inv_counts = _pool_geometry(i, tm, length)
        ext = jnp.where(inside, jnp.concatenate([up[...], u0[...], un[...]], axis=0), 0.0)
        dext = jnp.where(inside, jnp.concatenate([dp_[...], d0[...], dn_[...]], axis=0).astype(F32), 0.0)
        dmixed = (dext * sc_ref[...]).astype(BF16)
        pooled = _pooled(ext, inv_counts, tm)
        dus, dscs = [], []
        for g, w in enumerate(POOL_WINDOWS):
            lanes = slice(POOL_GROUP_DIM * g, POOL_GROUP_DIM * (g + 1))
            dpooled = _dot_nt(dmixed[:, lanes], w_ref[g])
            spread = _window_sum(dpooled * inv_counts[g], w, -(w // 2 - 1))
            dus.append((spread - dpooled)[HALO:HALO + tm])
            pb = pooled[g].astype(BF16)
            dw_ref[g] += _dot_tn(pb, dmixed[HALO:HALO + tm, lanes])
            prod = dext[HALO:HALO + tm, lanes] * _dot(pb, w_ref[g])
            dscs.append(_fold8(prod))
        du_ref[...] = jnp.concatenate(dus, axis=1).astype(BF16)
        dsc_ref[...] += jnp.concatenate(dscs, axis=1)

    return pl.pallas_call(
        body, name="pool_bwd", grid=(length // tm,),
        in_specs=_halo_specs(tm, POOL_WIDTH, length) + _halo_specs(tm, POOL_WIDTH, length, col=1)
        + [pl.BlockSpec(pool_w.shape, lambda i: (0, 0, 0)), _row_spec(POOL_WIDTH), ANY],
        out_specs=(pl.BlockSpec((tm, POOL_WIDTH), lambda i: (i, 0)), pl.BlockSpec((n_g, POOL_GROUP_DIM, POOL_GROUP_DIM), lambda i: (0, 0, 0)),
                   pl.BlockSpec((8, POOL_WIDTH), lambda i: (0, 0))),
        out_shape=(jax.ShapeDtypeStruct((length, POOL_WIDTH), BF16), jax.ShapeDtypeStruct((n_g, POOL_GROUP_DIM, POOL_GROUP_DIM), F32),
                   jax.ShapeDtypeStruct((8, POOL_WIDTH), F32)),
        compiler_params=_cp(1),
    )(u, u, u, dmix, dmix, dmix, pool_w, pool_scale, after)


def _mixer_out(mix, w_out, x, g_a, nmw, sh_m, sc_m, after):
    t, d = x.shape

    def epi(mo, rs, tiles, vecs, outs, st_ref, live):
        ga, nw, sh, sc = vecs
        x1_ref, mo_ref, hm_ref = outs
        x1 = tiles[0][rs, :] + ga[...] * mo
        x1_ref[rs, :] = x1
        mo_ref[rs, :] = mo.astype(BF16)
        r = lax.rsqrt(jnp.mean(x1 * x1, axis=-1, keepdims=True) + EPS)
        hm_ref[rs, :] = (((x1 * r) * nw[...]) * (1.0 + sc[...]) + sh[...]).astype(BF16)

    return _mm_deferred("mixer_out", mix, w_out, nt=False, tm=_pick(t, 256, 128), epi=epi, tiles=(x,), vecs=(g_a, nmw, sh_m, sc_m),
                        out_tiles=(jax.ShapeDtypeStruct((t, d), F32), jax.ShapeDtypeStruct((t, d), BF16), jax.ShapeDtypeStruct((t, d), BF16)),
                        n_stats=1, after=after)[:3]


def _mlp_up(hm, w_up):
    t, d = hm.shape
    tm = _pick(t, 1024, 512, 256, 128)
    tn = 2048

    def epi(acc, ex, outs):
        outs[0][...] = jnp.square(jnp.maximum(acc[...], 0.0)).astype(BF16)

    return _mm("mlp_up", hm, w_up, nt=False, tm=tm, tn=tn, tk=d, epi=epi,
               out_shape=(jax.ShapeDtypeStruct((t, w_up.shape[1]), BF16),),
               out_specs=(pl.BlockSpec((tm, tn), lambda j, i, k: (i, j)),))[0]


def _mlp_down_loss(act, w_down, x1, target, g_m, fw, after):
    t, d = x1.shape

    def epi(dnv, rs, tiles, vecs, outs, st_ref, live):
        x1_ref, t_ref = tiles
        gm, fw_ref = vecs
        dx2_ref, ddn_ref = outs
        x2 = x1_ref[rs, :] + gm[...] * dnv
        r = lax.rsqrt(jnp.mean(x2 * x2, axis=-1, keepdims=True) + EPS)
        xh = x2 * r
        diff = xh * fw_ref[...] - t_ref[rs, :]
        dy = diff * (1.0 / d)
        dxh = dy * fw_ref[...]
        dx2 = r * (dxh - xh * jnp.mean(dxh * xh, axis=-1, keepdims=True))
        dx2_ref[rs, :] = dx2
        ddn_ref[rs, :] = (dx2 * gm[...]).astype(BF16)
        st_ref[0] += jnp.where(live, _fold8(diff * diff), 0.0)
        st_ref[1] += jnp.where(live, _fold8(dy * xh), 0.0)
        st_ref[2] += jnp.where(live, _fold8(dx2 * dnv), 0.0)

    return _mm_k_deferred("mlp_down_loss", act, w_down, nt=False, tm=_pick(t, 512, 256), tk=_pick(act.shape[1], 2048), epi=epi,
                          tiles=(x1, target), vecs=(g_m, fw), n_stats=3, after=after,
                          out_tiles=(jax.ShapeDtypeStruct((t, d), F32), jax.ShapeDtypeStruct((t, d), BF16)))


def _mlp_dx(dup, w_up, x1, dx2, mo, nmw, sc_m, g_a, after):
    t, d = x1.shape

    def epi(dh, rs, tiles, vecs, outs, st_ref, live):
        x1_ref, dx2_ref, mo_ref = tiles
        nw, sc, ga = vecs
        dx1_ref, dmi_ref = outs
        dx1 = _norm_bwd_rows(dh, x1_ref[rs, :], nw[...], sc[...], st_ref) + dx2_ref[rs, :]
        dx1_ref[rs, :] = dx1
        dmi_ref[rs, :] = (dx1 * ga[...]).astype(BF16)
        st_ref[3] += jnp.where(live, _fold8(dx1 * mo_ref[rs, :].astype(F32)), 0.0)

    return _mm_k_deferred("mlp_dx", dup, w_up, nt=True, tm=_pick(t, 512, 256), tk=_pick(dup.shape[1], 2048), epi=epi,
                          tiles=(x1, dx2, mo), vecs=(nmw, sc_m, g_a), n_stats=4, after=after,
                          out_tiles=(jax.ShapeDtypeStruct((t, d), F32), jax.ShapeDtypeStruct((t, d), BF16)))


def _mlp_dact(ddn, w_down, act):
    t, d = ddn.shape
    tm = _pick(t, 512, 256, 128)
    tn = 2048

    def epi(acc, ex, outs):
        outs[0][...] = (acc[...] * (2.0 * jnp.sqrt(ex[0][...]).astype(F32))).astype(BF16)

    tile = pl.BlockSpec((tm, tn), lambda j, i, k: (i, j))
    return _mm("mlp_dact", ddn, w_down, nt=True, tm=tm, tn=tn, tk=d, epi=epi, extras=(act,), extra_specs=[tile],
               out_shape=(jax.ShapeDtypeStruct(act.shape, BF16),), out_specs=(tile,))[0]


def _norm_bwd_rows(dh, xv, nw, sc, st_ref):
    r = lax.rsqrt(jnp.mean(xv * xv, axis=-1, keepdims=True) + EPS)
    xh = xv * r
    dy = dh * (1.0 + sc)
    st_ref[0] += _fold8(dh)
    st_ref[1] += _fold8(dh * (xh * nw))
    st_ref[2] += _fold8(dy * xh)
    dxh = dy * nw
    return r * (dxh - xh * jnp.mean(dxh * xh, axis=-1, keepdims=True))


def _mixer_dmix(dmi, w_out, after):
    t, d = dmi.shape
    tm = _pick(t, 512, 256, 128)

    def epi(acc, ex, outs):
        outs[0][...] = acc[...].astype(BF16)

    n = w_out.shape[0]
    return _mm("mixer_dmix", dmi, w_out, nt=True, tm=tm, tn=n, tk=d, epi=epi, after=after,
               out_shape=(jax.ShapeDtypeStruct((t, n), BF16),), out_specs=(pl.BlockSpec((tm, n), lambda j, i, k: (i, 0)),))[0]


def _mixer_dx(name, dp, w_in, x, dx1, naw, sc_a, after):
    t, d = x.shape

    def epi(dh, rs, tiles, vecs, outs, st_ref, live):
        x_ref, dx1_ref = tiles
        nw, sc = vecs
        outs[0][rs, :] = _norm_bwd_rows(dh, x_ref[rs, :], nw[...], sc[...], st_ref) + dx1_ref[rs, :]

    return _mm_deferred(name, dp, w_in, nt=True, tm=_pick(t, 256, 128), epi=epi, tiles=(x, dx1), vecs=(naw, sc_a),
                        out_tiles=(jax.ShapeDtypeStruct((t, d), F32),), n_stats=3, after=after, vmem_mib=56)


def _silu(v):
    return v / (1.0 + jnp.exp(-v))


def _ada_fwd(cond, w_ada, b_ada):
    d, n = w_ada.shape
    tn = 512

    def body(c_ref, w_ref, b_ref, o_ref):
        o_ref[...] = _dot(_silu(c_ref[...]).astype(BF16), w_ref[...].astype(BF16)) + b_ref[...]

    return pl.pallas_call(
        body, name="ada_fwd", grid=(n // tn,),
        in_specs=[pl.BlockSpec(cond.shape, lambda j: (0, 0)), pl.BlockSpec((d, tn), lambda j: (0, j)), pl.BlockSpec((1, tn), lambda j: (0, j))],
        out_specs=pl.BlockSpec((cond.shape[0], tn), lambda j: (0, j)), out_shape=jax.ShapeDtypeStruct((cond.shape[0], n), F32),
        compiler_params=_cp(1),
    )(cond, w_ada, b_ada)


def _adamw_math(w, g, m, v):
    m = ADAM_B1 * m + (1.0 - ADAM_B1) * g
    v = ADAM_B2 * v + (1.0 - ADAM_B2) * jnp.square(g)
    m_hat = m / (1.0 - ADAM_B1 ** ADAM_STEP)
    v_hat = v / (1.0 - ADAM_B2 ** ADAM_STEP)
    return -ADAM_LR * (m_hat / (jnp.sqrt(v_hat) + ADAM_EPS) + ADAM_WD * w), m, v


def _ada_bwd(cond, dm, w_ada, m_ada, v_ada):
    d, n = w_ada.shape
    tn = 256
    rows = cond.shape[0]

    def body(c_ref, dm_ref, w_ref, m_ref, v_ref, g_ref, dl_ref, nm_ref, nv_ref, pc_ref):
        @pl.when(pl.program_id(0) == 0)
        def _():
            pc_ref[...] = jnp.zeros_like(pc_ref)

        dmb = dm_ref[...].astype(BF16)
        w = w_ref[...]
        g = _dot_tn(_silu(c_ref[...]).astype(BF16), dmb)
        g_ref[...] = g
        dl_ref[...], nm_ref[...], nv_ref[...] = _adamw_math(w, g, m_ref[...], v_ref[...])
        pc_ref[...] += _dot_nt(dm_ref[8:16, :].astype(BF16), w.astype(BF16))

    tile = pl.BlockSpec((d, tn), lambda j: (0, j))
    like = jax.ShapeDtypeStruct((d, n), F32)
    return pl.pallas_call(
        body, name="ada_bwd", grid=(n // tn,),
        in_specs=[pl.BlockSpec((rows, d), lambda j: (0, 0)), pl.BlockSpec((rows, tn), lambda j: (0, j)), tile, tile, tile],
        out_specs=(tile, tile, tile, tile, pl.BlockSpec((8, d), lambda j: (0, 0))),
        out_shape=(like, like, like, like, jax.ShapeDtypeStruct((8, d), F32)), compiler_params=_cp(1),
    )(cond, dm, w_ada, m_ada, v_ada)


def _adamw(name, w, g, m, v):
    return _ew(name, lambda w_, g_, m_, v_: (g_,) + _adamw_math(w_, g_, m_, v_), [w, g, m, v], [F32, F32, F32, F32])


def _colsum(st):
    return jnp.sum(st, axis=1)


def kernel(x, c, ctx, c_ctx, norm_attn_w, norm_mlp_w, w_ada, b_ada, w_in, attn_sink, pool_w, pool_scale, w_out, w_mlp_up, w_mlp_down, final_norm_w, loss_target, m_c_ctx, m_norm_attn_w, m_norm_mlp_w, m_w_ada, m_b_ada, m_w_in, m_attn_sink, m_pool_w, m_pool_scale, m_w_out, m_w_mlp_up, m_w_mlp_down, m_final_norm_w, v_c_ctx, v_norm_attn_w, v_norm_mlp_w, v_w_ada, v_b_ada, v_w_in, v_attn_sink, v_pool_w, v_pool_scale, v_w_out, v_w_mlp_up, v_w_mlp_down, v_final_norm_w):
    length, d = x.shape[1], x.shape[2]
    n_ctx = ctx.shape[1]
    pos = (lax.axis_index("x"), lax.axis_index("y"), lax.axis_index("c"))
    me, chip = _dev_index(pos), _chip_index(pos)
    xs, tgt, cx = x.reshape(length, d), loss_target.reshape(length, d), ctx.reshape(n_ctx, d)
    n_ada = w_ada.shape[2]

    c_all = _allgather8("gather_c", jnp.pad(c, ((0, 7), (0, 0))))
    mixer_bigs = [_Big("col", w_in.shape[1:]), _Big("pool", pool_w.shape[1:]), _Big("row", w_out.shape[1:])]
    mlp_bigs = [_Big("col", w_mlp_up.shape[1:]), _Big("row", w_mlp_down.shape[1:])]
    placed = [_cast_place(f"place_{i}", b, s, c_all) for i, (b, s) in enumerate(zip(mixer_bigs, [w_in[0], pool_w[0], w_out[0]]))]
    flight = _split("gather_mixer_ici", placed, _gather_ici_remote(mixer_bigs, 0))
    cond = jnp.concatenate([c_all[:, 0, :], jnp.pad(c_ctx[None, :], ((0, 7), (0, 0)))], axis=0) + flight[3][0, 0]
    b_shard = lax.dynamic_slice_in_dim(b_ada, chip * n_ada, n_ada, axis=1)
    mod_all = _allgather8("gather_mod", _ada_fwd(cond, w_ada[0], b_shard))
    mod = jnp.concatenate([mod_all[0], mod_all[2], mod_all[4], mod_all[6]], axis=1)
    mine = lax.dynamic_slice_in_dim(mod, me, 1, axis=0)
    sh_a, sc_a, g_a, sh_m, sc_m, g_m = [mine[:, d * i:d * (i + 1)] for i in range(6)]
    csh_a, csc_a = mod[8:9, :d], mod[8:9, d:2 * d]

    win_b, pw_b, wout_b = _exchange("gather_mixer_d2d", _join(flight, mod), [jax.ShapeDtypeStruct(b.full_shape, BF16) for b in mixer_bigs],
                                    _gather_d2d_remote(mixer_bigs, 3), aliases={0: 0, 1: 1, 2: 2})
    wout_b = wout_b.reshape(-1, d)
    placed = [_cast_place(f"place_mlp_{i}", b, s, pw_b) for i, (b, s) in enumerate(zip(mlp_bigs, [w_mlp_up[0], w_mlp_down[0]]))]
    flight = _split("gather_mlp_ici", placed, _gather_ici_remote(mlp_bigs, 0))

    cos, sin = _rope_tables(length, True)
    one, zero = _rope_tables(n_ctx, False)
    h, q, k, v, u = _mixer_in("mixer_in", xs, norm_attn_w, sh_a, sc_a, win_b, cos, sin, flight[3])
    hc, _, kc, vc, _ = _mixer_in("mixer_in_ctx", cx, norm_attn_w, csh_a, csc_a, win_b, one, zero, flight[3])
    attn, probs = _attn_fwd(q, k, v, kc, vc, attn_sink)
    mix = _pool_fwd(u, pw_b, pool_scale, attn)
    flight = _split("gather_mlp_d2d", _join(flight, mix), _gather_d2d_remote(mlp_bigs, 0))
    x1, mo, hm = _mixer_out(mix, wout_b, xs, g_a, norm_mlp_w, sh_m, sc_m, flight[3])
    wup_b, wdn_b = _join(flight, hm)
    wdn_b = wdn_b.reshape(-1, d)
    act = _mlp_up(hm, wup_b)
    dx2, ddn, st_loss = _mlp_down_loss(act, wdn_b, x1, tgt, g_m, final_norm_w[None, :], c)
    st_loss = _colsum(st_loss)
    loss = lax.psum(0.5 / d * jnp.sum(st_loss[0]), ("x", "y", "c"))

    tt = _pick(length, 2048, 1024, 512, 256, 128)
    g_wdn = _mm_tn("grad_w_down", act, ddn, BF16, tmo=1024, tn=d, tt=tt)
    dup = _mlp_dact(ddn, wdn_b, act)
    g_wup = _mm_tn("grad_w_up", hm, dup, BF16, tmo=d, tn=1024, tt=tt)
    empty = lambda shapes: [lax.empty(s.shape, s.dtype) for s in shapes]
    grads = [g_wup, g_wdn.reshape(mlp_bigs[1].full_shape)]
    flight = _split("reduce_mlp_d2d", grads + empty(_halves(mlp_bigs)), _reduce_d2d_remote(mlp_bigs))
    dx1, dmi, st_mlp = _mlp_dx(dup, wup_b, x1, dx2, mo, norm_mlp_w, sc_m, g_a, flight[3])
    st_mlp = _colsum(st_mlp)
    landed = _join(flight, dmi)
    mlp_chip = _chip_sums("mlp", mlp_bigs, landed[:2], landed[2:])
    flight = _split("reduce_mlp_ici", mlp_chip + empty(_thirds(mlp_bigs)), _reduce_ici_remote(mlp_bigs))
    g_wout = _mm_tn("grad_w_out", mix, dmi, BF16, tmo=1024, tn=d, tt=tt)
    dmix = _mixer_dmix(dmi, wout_b, flight[3])
    dq, dkp, dvp, dkc, dvc, dsink = _attn_bwd(q, k, v, kc, vc, attn_sink, dmix, probs)
    landed = _join(flight, dq)
    flight = _split("reduce_mlp_share", _piece_sums("mlp", mlp_bigs, landed[:2], landed[2:]), _share_remote(mlp_bigs, 0))
    du, g_pw, st_pool = _pool_bwd(u, dmix, pw_b, pool_scale, flight[3])
    g_mlp = _join(flight, du)

    wo_bigs, win_bigs = mixer_bigs[1:], mixer_bigs[:1]
    wo_chip = _reduce_to_chip("wo", wo_bigs, [g_pw.astype(BF16), g_wout.reshape(wo_bigs[1].full_shape)])
    flight = _split("reduce_wo_ici", wo_chip + empty(_thirds(wo_bigs)), _reduce_ici_remote(wo_bigs))
    dp = _assemble_dp(dq, dkp, dvp, du, cos, sin, flight[3])
    dpc = jnp.concatenate([jnp.zeros((n_ctx, ATTN_WIDTH), BF16), dkc.astype(BF16), dvc.astype(BF16),
                           jnp.zeros((n_ctx, POOL_WIDTH), BF16)], axis=1)
    g_win = _mm_tn("grad_w_in", h, dp, BF16, tmo=d, tn=dp.shape[1] // 2, tt=_pick(length, 1024, 512, 256, 128), more=(hc, dpc))
    wo_landed = _join(flight, g_win)
    win_chip = _reduce_to_chip("win", win_bigs, [g_win])
    flight = _split("reduce_win_ici", win_chip + empty(_thirds(win_bigs)), _reduce_ici_remote(win_bigs))
    grad_x, st_mix = _mixer_dx("mixer_dx", dp, win_b, xs, dx1, norm_attn_w, sc_a, flight[3])
    _, st_ctx = _mixer_dx("mixer_dx_ctx", dpc, win_b, cx, jnp.zeros((n_ctx, d), F32), norm_attn_w, csc_a, flight[3])
    st_mix, st_ctx = _colsum(st_mix), _colsum(st_ctx)
    win_landed = _join(flight, grad_x)
    g_mixer = (_reduce_finish("win", win_bigs, win_landed[:1], win_landed[1:])
               + _reduce_finish("wo", wo_bigs, wo_landed[:2], wo_landed[2:]))

    zrow = jnp.zeros((d,), F32)
    pad = lambda a: jnp.pad(a, (0, d - a.shape[0]))
    mine_rows = [st_mix[0], st_mix[1], st_mlp[3], st_mlp[0], st_mlp[1], st_loss[2],
                 st_ctx[0], st_ctx[1],
                 st_mix[2] + st_ctx[2], st_mlp[2], st_loss[1],
                 pad(jnp.sum(st_pool, axis=0)), pad(dsink[0, :N_Q_HEADS])] + [zrow] * 3
    small_all = _allgather8("gather_small", jnp.concatenate(mine_rows).reshape(len(mine_rows), d))
    small = small_all[0]
    for i in range(1, 8):
        small = small + small_all[i]
    dm_rows = small_all[:, 0:6, :].reshape(8, 6 * d)
    dm_ctx = jnp.concatenate([small[6], small[7], jnp.zeros((4 * d,), F32)])[None, :]
    dm = jnp.concatenate([dm_rows, jnp.pad(dm_ctx, ((0, 7), (0, 0)))], axis=0)
    g_bada = jnp.sum(dm[:9], axis=0, keepdims=True)
    dm_shard = lax.dynamic_slice_in_dim(dm, chip * n_ada, n_ada, axis=1)
    g_wada, dl_wada, nm_wada, nv_wada, part_cctx = _ada_bwd(cond, dm_shard, w_ada[0], m_w_ada[0], v_w_ada[0])
    cctx_all = _allgather8("gather_cctx", part_cctx)
    dsilu_in = cctx_all[0, 0] + cctx_all[2, 0] + cctx_all[4, 0] + cctx_all[6, 0]
    sig = 1.0 / (1.0 + jnp.exp(-c_ctx))
    g_cctx = dsilu_in * (sig * (1.0 + c_ctx * (1.0 - sig)))

    g_shards = g_mixer + g_mlp
    big_w = [w_in, pool_w, w_out, w_mlp_up, w_mlp_down]
    big_m = [m_w_in, m_pool_w, m_w_out, m_w_mlp_up, m_w_mlp_down]
    big_v = [v_w_in, v_pool_w, v_w_out, v_w_mlp_up, v_w_mlp_down]
    big_names = ["w_in", "pool_w", "w_out", "w_mlp_up", "w_mlp_down"]
    res = {}
    for nm, w_, g_, m_, v_ in zip(big_names, big_w, g_shards, big_m, big_v):
        res[nm] = tuple(_adamw("adamw_" + nm, w_, g_.reshape(w_.shape), m_, v_))
    res["w_ada"] = (g_wada[None], dl_wada[None], nm_wada[None], nv_wada[None])

    def pack(cc, na, nm_, ba, sk, ps, fn):
        flat = [cc.reshape(-1), na.reshape(-1), nm_.reshape(-1), ba.reshape(-1), pad(sk.reshape(-1)), pad(ps.reshape(-1)),
                fn.reshape(-1), jnp.zeros((4 * d,), F32)]
        return jnp.concatenate(flat).reshape(16, d)

    w_s = pack(c_ctx, norm_attn_w, norm_mlp_w, b_ada, attn_sink, pool_scale, final_norm_w)
    m_s = pack(m_c_ctx, m_norm_attn_w, m_norm_mlp_w, m_b_ada, m_attn_sink, m_pool_scale, m_final_norm_w)
    v_s = pack(v_c_ctx, v_norm_attn_w, v_norm_mlp_w, v_b_ada, v_attn_sink, v_pool_scale, v_final_norm_w)
    g_s = pack(g_cctx, small[8], small[9], g_bada, small[12][:N_Q_HEADS], small[11][:POOL_WIDTH], small[10])
    small_out = _adamw("adamw_small", w_s, g_s, m_s, v_s)

    def unpack(p):
        return {"c_ctx": p[0], "norm_attn_w": p[1:2], "norm_mlp_w": p[2:3], "b_ada": p[3:9].reshape(1, 6 * d),
                "attn_sink": p[9:10, :N_Q_HEADS], "pool_scale": p[10:11, :POOL_WIDTH], "final_norm_w": p[11]}

    small_res = [unpack(p) for p in small_out]
    order = ["c_ctx", "norm_attn_w", "norm_mlp_w", "w_ada", "b_ada", "w_in", "attn_sink", "pool_w", "pool_scale",
             "w_out", "w_mlp_up", "w_mlp_down", "final_norm_w"]
    outs = [loss, grad_x.reshape(x.shape)]
    for kind in range(4):
        for nm in order:
            outs.append(res[nm][kind] if nm in res else small_res[kind][nm])
    return tuple(outs)
```

```python
import functools

import jax
import jax.numpy as jnp
from jax import lax
from jax.experimental import pallas as pl
from jax.experimental.pallas import tpu as pltpu

F32 = jnp.float32
BF16 = jnp.bfloat16
EPS = 1e-6
NEG_INF = -1e30
HEAD_DIM = 64
N_Q_HEADS = 16
N_KV_HEADS = 4
GROUP = N_Q_HEADS // N_KV_HEADS
ATTN_WIDTH = N_Q_HEADS * HEAD_DIM
KV_WIDTH = N_KV_HEADS * HEAD_DIM
POOL_WINDOWS = (2, 4, 8, 16)
POOL_GROUP_DIM = 256
POOL_WIDTH = len(POOL_WINDOWS) * POOL_GROUP_DIM
BLOCK = 128
GRID_W = 64
ROPE_BASE = 10000.0
SCALE = HEAD_DIM ** -0.5
HALO = 16
STRIP = 16
ADAM_LR, ADAM_B1, ADAM_B2, ADAM_EPS, ADAM_WD, ADAM_STEP = 0.001, 0.9, 0.999, 1e-08, 0.01, 10
MESH = pl.DeviceIdType.MESH
MIB = 1024 * 1024
ANY = pl.BlockSpec(memory_space=pl.ANY)


def _cp(n_axes, vmem_mib=48):
    return pltpu.CompilerParams(dimension_semantics=("arbitrary",) * n_axes, vmem_limit_bytes=vmem_mib * MIB)


def _fold8(v):
    s = v[0:8]
    for t in range(1, v.shape[0] // 8):
        s = s + v[8 * t:8 * t + 8]
    return s


def _dot(a, b):
    return jnp.dot(a, b, preferred_element_type=F32)


def _dot_nt(a, b):
    return lax.dot_general(a, b, (((1,), (1,)), ((), ())), preferred_element_type=F32)


def _dot_tn(a, b):
    return lax.dot_general(a, b, (((0,), (0,)), ((), ())), preferred_element_type=F32)


def _pick(n, *cands):
    for t in cands:
        if n % t == 0:
            return t
    return n


def _flip(pos, mask):
    return tuple((1 - v) if (mask >> (2 - i)) & 1 else v for i, v in enumerate(pos))


def _exchange(name, ins, out_shapes, remote, local=(), aliases=None):
    n_io = len(ins) + len(out_shapes)

    def body(*refs):
        io = refs[:n_io]
        send_sems, recv_sems, local_sems = refs[n_io:]
        me = (lax.axis_index("x"), lax.axis_index("y"), lax.axis_index("c"))

        def copy(i, sender):
            mask, src_fn, dst_fn = remote[i]
            return pltpu.make_async_remote_copy(
                src_ref=src_fn(io, sender), dst_ref=dst_fn(io, sender), send_sem=send_sems.at[i],
                recv_sem=recv_sems.at[i], device_id=_flip(sender, mask), device_id_type=MESH)

        own = [pltpu.make_async_copy(s(io, me), d(io, me), local_sems.at[i]) for i, (s, d) in enumerate(local)]
        for cp in own:
            cp.start()
        sends = [copy(i, me) for i in range(len(remote))]
        for cp in sends:
            cp.start()
        for i in range(len(remote)):
            copy(i, _flip(me, remote[i][0])).wait_recv()
        for cp in sends:
            cp.wait_send()
        for cp in own:
            cp.wait()

    return pl.pallas_call(
        body, name=name, out_shape=tuple(out_shapes),
        in_specs=[ANY] * len(ins), out_specs=tuple([ANY] * len(out_shapes)),
        scratch_shapes=[pltpu.SemaphoreType.DMA((len(remote),)), pltpu.SemaphoreType.DMA((len(remote),)),
                        pltpu.SemaphoreType.DMA((max(len(local), 1),))],
        input_output_aliases=aliases or {},
    )(*ins)


HBM = pl.BlockSpec(memory_space=pltpu.HBM)
SEM = pl.BlockSpec(memory_space=pltpu.SEMAPHORE)
EFFECT = pltpu.SideEffectType.DATAFLOW_SIDE_EFFECTING


def _split_copy(remote, i, io, send_sems, recv_sems, sender):
    mask, src_fn, dst_fn = remote[i]
    return pltpu.make_async_remote_copy(
        src_ref=src_fn(io, sender), dst_ref=dst_fn(io, sender), send_sem=send_sems.at[i],
        recv_sem=recv_sems.at[i], device_id=_flip(sender, mask), device_id_type=MESH)


def _exchange_start(name, bufs, remote, after=None):
    n, r = len(bufs), len(remote)
    more = [] if after is None else [after]

    def body(*refs):
        io, (send_sems, recv_sems, token) = refs[:n], refs[-3:]
        me = (lax.axis_index("x"), lax.axis_index("y"), lax.axis_index("c"))
        for i in range(r):
            _split_copy(remote, i, io, send_sems, recv_sems, me).start()
        token[...] = jnp.zeros_like(token)

    res = pl.pallas_call(
        body, name=name,
        out_shape=tuple(pltpu.HBM(b.shape, b.dtype) for b in bufs)
        + (pltpu.SemaphoreType.DMA((r,)), pltpu.SemaphoreType.DMA((r,)), jax.ShapeDtypeStruct((8, 128), F32)),
        in_specs=[HBM] * n + [ANY] * len(more), out_specs=tuple([HBM] * n) + (SEM, SEM, pl.BlockSpec(memory_space=pltpu.VMEM)),
        input_output_aliases={i: i for i in range(n)}, compiler_params=pltpu.CompilerParams(has_side_effects=EFFECT),
    )(*[pltpu.with_memory_space_constraint(b, pltpu.HBM) for b in bufs], *more)
    return list(res[:n]), res[n], res[n + 1], res[n + 2]


def _exchange_wait(name, bufs, send_sems, recv_sems, remote, after):
    n, r = len(bufs), len(remote)

    def body(*refs):
        io, ss, rs = refs[:n], refs[n], refs[n + 1]
        me = (lax.axis_index("x"), lax.axis_index("y"), lax.axis_index("c"))
        for i in range(r):
            _split_copy(remote, i, io, ss, rs, _flip(me, remote[i][0])).wait_recv()
        for i in range(r):
            _split_copy(remote, i, io, ss, rs, me).wait_send()

    return list(pl.pallas_call(
        body, name=name, out_shape=tuple(pltpu.HBM(b.shape, b.dtype) for b in bufs),
        in_specs=[HBM] * n + [SEM, SEM, ANY], out_specs=tuple([HBM] * n),
        input_output_aliases={i: i for i in range(n)}, compiler_params=pltpu.CompilerParams(has_side_effects=EFFECT),
    )(*bufs, send_sems, recv_sems, after))


def _my_c():
    return lax.axis_index("c")


def _my_chip():
    return 2 * lax.axis_index("x") + lax.axis_index("y")


def _dev_index(pos):
    return 4 * pos[0] + 2 * pos[1] + pos[2]


def _chip_index(pos):
    return 2 * pos[0] + pos[1]


def _allgather8(name, v):
    out = jax.ShapeDtypeStruct((8,) + v.shape, v.dtype)
    remote = [(mask, lambda io, pos: io[0], lambda io, pos: io[1].at[_dev_index(pos)]) for mask in range(1, 8)]
    local = [(lambda io, pos: io[0], lambda io, pos: io[1].at[_dev_index(pos)])]
    return _exchange(name, [v], [out], remote, local)[0]


class _Big:
    def __init__(self, kind, shard_shape):
        self.kind = kind
        self.shard_shape = tuple(shard_shape)
        if kind == "col":
            r, cs = shard_shape
            self.full_shape = (r, 4 * cs)
            self.piece_shape = (r // 2, cs)
            self.half_shape = (r // 2, 4 * cs)
        elif kind == "row":
            rs, c = shard_shape
            self.full_shape = (4, 2, rs // 2, c)
            self.piece_shape = (1, 1, rs // 2, c)
            self.half_shape = (4, 1, rs // 2, c)
        else:
            self.full_shape = (4, 256, 256)
            self.piece_shape = (2, 64, 256)
            self.half_shape = (2, 256, 256)

    def shard_as_pieces(self, a):
        return a.reshape((1, 2) + self.piece_shape[2:]) if self.kind == "row" else a

    def piece(self, ref, k, h):
        if self.kind == "col":
            r, cs = self.piece_shape
            return ref.at[pl.ds(h * r, r), pl.ds(k * cs, cs)]
        if self.kind == "row":
            return ref.at[pl.ds(k, 1), pl.ds(h, 1)]
        return ref.at[pl.ds(2 * h, 2), pl.ds(64 * k, 64)]

    def half_of_shard(self, ref, h):
        if self.kind == "col":
            return ref.at[pl.ds(h * self.piece_shape[0], self.piece_shape[0])]
        if self.kind == "row":
            return ref.at[:, pl.ds(h, 1)]
        return ref.at[pl.ds(2 * h, 2)]

    def half_of_full(self, ref, h):
        if self.kind == "col":
            return ref.at[pl.ds(h * self.half_shape[0], self.half_shape[0])]
        if self.kind == "row":
            return ref.at[:, pl.ds(h, 1)]
        return ref.at[pl.ds(2 * h, 2)]

    def piece_of_half(self, ref, k):
        if self.kind == "col":
            return ref.at[:, pl.ds(k * self.piece_shape[1], self.piece_shape[1])]
        if self.kind == "row":
            return ref.at[pl.ds(k, 1)]
        return ref.at[:, pl.ds(64 * k, 64)]


CHIP_MASKS = (4, 2, 6)


def _cast_place(name, big, shard, after):
    if big.kind == "col":
        r, cs = big.shard_shape
        tr = _pick(r, 512, 256, 128)
        src, grid, blk = shard, (r // tr,), (tr, cs)
        imap, omap = (lambda i: (i, 0)), (lambda i: (i, _my_chip()))
    elif big.kind == "row":
        rs, c = big.shard_shape
        tr = _pick(rs // 2, 256, 128)
        src, grid, blk = big.shard_as_pieces(shard), (2, rs // 2 // tr), (1, 1, tr, c)
        imap, omap = (lambda h, i: (0, h, i, 0)), (lambda h, i: (_my_chip(), h, i, 0))
    else:
        src, grid, blk = shard, (1,), big.shard_shape
        imap, omap = (lambda i: (0, 0, 0)), (lambda i: (0, _my_chip(), 0))

    def body(s_ref, after_ref, o_ref, token_ref):
        o_ref[...] = s_ref[...].astype(BF16)
        token_ref[...] = jnp.zeros_like(token_ref)

    return pl.pallas_call(
        body, name=name, grid=grid, in_specs=[pl.BlockSpec(blk, imap), ANY],
        out_specs=(pl.BlockSpec(blk, omap), pl.BlockSpec((8, 128), lambda *_: (0, 0))),
        out_shape=(jax.ShapeDtypeStruct(big.full_shape, BF16), jax.ShapeDtypeStruct((8, 128), F32)), compiler_params=_cp(len(grid)),
    )(src, after)


def _gather_ici_remote(bigs, off):
    remote = []
    for a, b in enumerate(bigs):
        for mask in CHIP_MASKS:
            def mine(io, p, a=a, b=b):
                return b.piece(io[off + a], _chip_index(p), p[2])
            remote.append((mask, mine, mine))
    return remote


def _gather_d2d_remote(bigs, off):
    remote = []
    for a, b in enumerate(bigs):
        for mask in CHIP_MASKS:
            def region(io, p, a=a, b=b, mask=mask):
                return b.piece(io[off + a], _chip_index(_flip(p, mask)), p[2])
            remote.append((1, region, region))
    return remote


def _ew(name, fn, ins, out_dtypes, rows_per_step=256):
    shape = ins[0].shape
    last = shape[-1]
    rows = 1
    for s in shape[:-1]:
        rows *= s
    ins2 = [a.reshape(rows, last) for a in ins]
    tr = _pick(rows, rows_per_step, 128, 64, 32, 16, 8)
    spec = pl.BlockSpec((tr, last), lambda i: (i, 0))

    def body(*refs):
        outs = fn(*[r[...] for r in refs[:len(ins)]])
        for o_ref, o in zip(refs[len(ins):], outs):
            o_ref[...] = o.astype(o_ref.dtype)

    outs = pl.pallas_call(
        body, name=name, grid=(rows // tr,), in_specs=[spec] * len(ins), out_specs=tuple([spec] * len(out_dtypes)),
        out_shape=tuple(jax.ShapeDtypeStruct((rows, last), d) for d in out_dtypes), compiler_params=_cp(1),
    )(*ins2)
    return [o.reshape(shape) for o in outs]


def _chip_sum(name, big, grad, from_sibling):
    if big.kind == "col":
        rh, w = big.half_shape
        tr = _pick(rh, 256, 128)
        nb = rh // tr
        grid, blk = (nb,), (tr, w)
        gmap, hmap = (lambda i: (_my_c() * nb + i, 0)), (lambda i: (i, 0))
    elif big.kind == "row":
        rh, w = big.half_shape[2:]
        tr = _pick(rh, 256, 128)
        grid, blk = (4, rh // tr), (1, 1, tr, w)
        gmap, hmap = (lambda k, i: (k, _my_c(), i, 0)), (lambda k, i: (k, 0, i, 0))
    else:
        grid, blk = (1,), big.half_shape
        gmap, hmap = (lambda i: (_my_c(), 0, 0)), (lambda i: (0, 0, 0))

    def body(g_ref, s_ref, o_ref):
        o_ref[...] = (g_ref[...].astype(F32) + s_ref[...].astype(F32)).astype(BF16)

    return pl.pallas_call(
        body, name=name, grid=grid, in_specs=[pl.BlockSpec(blk, gmap), pl.BlockSpec(blk, hmap)],
        out_specs=pl.BlockSpec(blk, hmap), out_shape=jax.ShapeDtypeStruct(big.half_shape, BF16), compiler_params=_cp(len(grid)),
    )(grad, from_sibling)


def _piece_sum(name, big, chip_sum, thirds):
    if big.kind == "col":
        rp, cs = big.piece_shape
        tr = _pick(rp, 256, 128)
        nb = rp // tr
        grid, blk, tblk = (nb,), (tr, cs), (1, tr, cs)
        smap, omap = (lambda i: (i, _my_chip())), (lambda i: (_my_c() * nb + i, 0))
        tmap = lambda j: (lambda i: (j, i, 0))
        out_shape = big.shard_shape
    elif big.kind == "row":
        rp, w = big.piece_shape[2:]
        tr = _pick(rp, 256, 128)
        grid, blk, tblk = (rp // tr,), (1, 1, tr, w), (1, 1, 1, tr, w)
        smap, omap = (lambda i: (_my_chip(), 0, i, 0)), (lambda i: (0, _my_c(), i, 0))
        tmap = lambda j: (lambda i: (j, 0, 0, i, 0))
        out_shape = (1, 2, rp, w)
    else:
        grid, blk, tblk = (1,), big.piece_shape, (1,) + big.piece_shape
        smap, omap = (lambda i: (0, _my_chip(), 0)), (lambda i: (_my_c(), 0, 0))
        tmap = lambda j: (lambda i: (j, 0, 0, 0))
        out_shape = big.shard_shape

    def body(s_ref, t0, t1, t2, o_ref):
        o_ref[...] = s_ref[...].astype(F32) + t0[0].astype(F32) + t1[0].astype(F32) + t2[0].astype(F32)

    return pl.pallas_call(
        body, name=name, grid=grid,
        in_specs=[pl.BlockSpec(blk, smap)] + [pl.BlockSpec(tblk, tmap(j)) for j in range(3)],
        out_specs=pl.BlockSpec(blk, omap), out_shape=jax.ShapeDtypeStruct(out_shape, F32), compiler_params=_cp(len(grid)),
    )(chip_sum, thirds, thirds, thirds)


def _split(name, bufs, remote, after=None):
    return _exchange_start(name + "_start", bufs, remote, after) + (remote, name)


def _join(handle, after):
    bufs, send_sems, recv_sems, _, remote, name = handle
    return _exchange_wait(name + "_wait", bufs, send_sems, recv_sems, remote, after)


def _reduce_d2d_remote(bigs):
    n = len(bigs)
    return [(1, lambda io, p, a=a, b=b: b.half_of_full(io[a], 1 - p[2]), lambda io, p, a=a: io[n + a])
            for a, b in enumerate(bigs)]


def _halves(bigs):
    return [jax.ShapeDtypeStruct(b.half_shape, BF16) for b in bigs]


def _chip_sums(tag, bigs, grads, from_sibling):
    return [_chip_sum(f"reduce_{tag}_chip_sum_{a}", b, g, r) for a, (b, g, r) in enumerate(zip(bigs, grads, from_sibling))]


def _reduce_to_chip(tag, bigs, grads):
    from_sibling = _exchange(f"reduce_{tag}_d2d", grads, _halves(bigs), _reduce_d2d_remote(bigs))
    return _chip_sums(tag, bigs, grads, from_sibling)


def _reduce_ici_remote(bigs):
    n = len(bigs)
    remote = []
    for a, b in enumerate(bigs):
        for j, mask in enumerate(CHIP_MASKS):
            remote.append((mask,
                           lambda io, p, a=a, b=b, mask=mask: b.piece_of_half(io[a], _chip_index(_flip(p, mask))),
                           lambda io, p, a=a, j=j: io[n + a].at[j]))
    return remote


def _thirds(bigs):
    return [jax.ShapeDtypeStruct((3,) + b.piece_shape, BF16) for b in bigs]


def _piece_sums(tag, bigs, chip_sum, from_chips):
    return [_piece_sum(f"reduce_{tag}_sum_{a}", b, s, r) for a, (b, s, r) in enumerate(zip(bigs, chip_sum, from_chips))]


def _share_remote(bigs, off):
    remote = []
    for a, b in enumerate(bigs):
        def mine(io, p, a=a, b=b):
            return b.half_of_shard(io[off + a], p[2])
        remote.append((1, mine, mine))
    return remote


def _reduce_finish(tag, bigs, chip_sum, from_chips):
    n = len(bigs)
    placed = _piece_sums(tag, bigs, chip_sum, from_chips)
    out = _exchange(f"reduce_{tag}_share_d2d", placed, [jax.ShapeDtypeStruct(p.shape, F32) for p in placed],
                    _share_remote(bigs, n), aliases={a: a for a in range(n)})
    return [o.reshape(b.shard_shape) for o, b in zip(out, bigs)]


def _mm(name, a, b, *, nt, tm, tn, tk, epi, extras=(), extra_specs=(), out_shape, out_specs, after=None, vmem_mib=48):
    m, kdim = a.shape
    n = b.shape[0] if nt else b.shape[1]
    gm, gn, gk = m // tm, n // tn, kdim // tk
    a_spec = pl.BlockSpec((tm, tk), lambda j, i, k: (i, k))
    b_spec = pl.BlockSpec((tn, tk), lambda j, i, k: (j, k)) if nt else pl.BlockSpec((tk, tn), lambda j, i, k: (k, j))
    n_ex = len(extras)
    if after is not None:
        extras, extra_specs = tuple(extras) + (after,), list(extra_specs) + [ANY]

    def body(a_ref, b_ref, *rest):
        ex, outs, acc = rest[:n_ex], rest[len(extras):-1], rest[-1]
        dot = _dot_nt if nt else _dot
        if gk == 1:
            acc[...] = dot(a_ref[...], b_ref[...])
            epi(acc, ex, outs)
        else:
            k = pl.program_id(2)

            @pl.when(k == 0)
            def _():
                acc[...] = dot(a_ref[...], b_ref[...])

            @pl.when(k > 0)
            def _():
                acc[...] += dot(a_ref[...], b_ref[...])

            @pl.when(k == gk - 1)
            def _():
                epi(acc, ex, outs)

    return pl.pallas_call(
        body, name=name, grid=(gn, gm, gk), in_specs=[a_spec, b_spec, *extra_specs], out_specs=tuple(out_specs),
        out_shape=tuple(out_shape), scratch_shapes=[pltpu.VMEM((tm, tn), F32)], compiler_params=_cp(3, vmem_mib),
    )(a, b, *extras)


def _mm_deferred(name, a, b, *, nt, tm, epi, tiles, vecs, out_tiles, n_stats, after, vmem_mib=48):
    m, kdim = a.shape
    n = b.shape[0] if nt else b.shape[1]
    gm = m // tm
    n_t, n_v, n_o = len(tiles), len(vecs), len(out_tiles)
    dot = _dot_nt if nt else _dot

    def body(a_ref, b_ref, *rest):
        t_refs, v_refs = rest[:n_t], rest[n_t:n_t + n_v]
        o_refs, st_ref, acc0, acc1 = rest[n_t + n_v + 1:n_t + n_v + 1 + n_o], rest[-3], rest[-2], rest[-1]
        i = pl.program_id(0)

        @pl.when(i == 0)
        def _():
            acc1[...] = jnp.zeros_like(acc1)
            st_ref[...] = jnp.zeros_like(st_ref)

        def finish(prev):
            for r0 in range(0, tm, STRIP):
                rs = slice(r0, r0 + STRIP)
                epi(prev[rs, :], rs, t_refs, v_refs, o_refs, st_ref, i > 0)

        @pl.when((i % 2 == 0) & (i < gm))
        def _():
            acc0[...] = dot(a_ref[...], b_ref[...])
            finish(acc1)

        @pl.when((i % 2 == 1) & (i < gm))
        def _():
            acc1[...] = dot(a_ref[...], b_ref[...])
            finish(acc0)

        @pl.when(i == gm)
        def _():
            finish(acc1 if gm % 2 == 0 else acc0)

    prev = lambda i: (jnp.maximum(i - 1, 0), 0)
    tile = pl.BlockSpec((tm, n), prev)
    return pl.pallas_call(
        body, name=name, grid=(gm + 1,),
        in_specs=[pl.BlockSpec((tm, kdim), lambda i: (jnp.minimum(i, gm - 1), 0)), pl.BlockSpec(b.shape, lambda i: (0, 0))]
        + [tile] * n_t + [_row_spec(n)] * n_v + [ANY],
        out_specs=tuple([tile] * n_o) + (_stat_spec(n_stats, n),),
        out_shape=tuple(out_tiles) + (jax.ShapeDtypeStruct((n_stats, 8, n), F32),),
        scratch_shapes=[pltpu.VMEM((tm, n), F32), pltpu.VMEM((tm, n), F32)], compiler_params=_cp(1, vmem_mib),
    )(a, b, *tiles, *vecs, after)


def _mm_k_deferred(name, a, b, *, nt, tm, tk, epi, tiles, vecs, out_tiles, n_stats, after, vmem_mib=56):
    m, kdim = a.shape
    n = b.shape[0] if nt else b.shape[1]
    gm, gk = m // tm, kdim // tk
    rows = tm // gk
    n_t, n_v, n_o = len(tiles), len(vecs), len(out_tiles)
    dot = _dot_nt if nt else _dot

    def body(a_ref, b_ref, *rest):
        t_refs, v_refs = rest[:n_t], rest[n_t:n_t + n_v]
        o_refs, st_ref, acc0, acc1 = rest[n_t + n_v + 1:n_t + n_v + 1 + n_o], rest[-3], rest[-2], rest[-1]
        i, k = pl.program_id(0), pl.program_id(1)

        @pl.when((i == 0) & (k == 0))
        def _():
            acc1[...] = jnp.zeros_like(acc1)
            st_ref[...] = jnp.zeros_like(st_ref)

        def finish(prev):
            for r0 in range(0, rows, STRIP):
                acc_rows = prev[pl.ds(pl.multiple_of(k * rows + r0, STRIP), STRIP), :]
                epi(acc_rows, slice(r0, r0 + STRIP), t_refs, v_refs, o_refs, st_ref, i > 0)

        def step(cur, prev):
            cur[...] = jnp.where(k > 0, cur[...], 0.0) + dot(a_ref[...], b_ref[...])
            finish(prev)

        @pl.when((i % 2 == 0) & (i < gm))
        def _():
            step(acc0, acc1)

        @pl.when((i % 2 == 1) & (i < gm))
        def _():
            step(acc1, acc0)

        @pl.when(i == gm)
        def _():
            finish(acc1 if gm % 2 == 0 else acc0)

    prev = lambda i, k: (jnp.where(i == 0, 0, (i - 1) * gk + k), 0)
    part = pl.BlockSpec((rows, n), prev)
    b_spec = pl.BlockSpec((n, tk), lambda i, k: (0, k)) if nt else pl.BlockSpec((tk, n), lambda i, k: (k, 0))
    return pl.pallas_call(
        body, name=name, grid=(gm + 1, gk),
        in_specs=[pl.BlockSpec((tm, tk), lambda i, k: (jnp.minimum(i, gm - 1), k)), b_spec]
        + [part] * n_t + [_row_spec(n)] * n_v + [ANY],
        out_specs=tuple([part] * n_o) + (_stat_spec(n_stats, n),),
        out_shape=tuple(out_tiles) + (jax.ShapeDtypeStruct((n_stats, 8, n), F32),),
        scratch_shapes=[pltpu.VMEM((tm, n), F32), pltpu.VMEM((tm, n), F32)], compiler_params=_cp(2, vmem_mib),
    )(a, b, *tiles, *vecs, after)


def _mm_tn(name, a, b, out_dtype, *, tmo, tn, tt, more=(), vmem_mib=56):
    t, m = a.shape
    n = b.shape[1]
    gt = t // tt

    def body(a_ref, b_ref, *rest):
        o_ref, acc = rest[-2:]
        k = pl.program_id(2)

        @pl.when(k == 0)
        def _():
            first = _dot_tn(a_ref[...], b_ref[...])
            acc[...] = first + _dot_tn(rest[0][...], rest[1][...]) if more else first

        @pl.when(k > 0)
        def _():
            acc[...] += _dot_tn(a_ref[...], b_ref[...])

        @pl.when(k == gt - 1)
        def _():
            o_ref[...] = acc[...].astype(o_ref.dtype)

    more_specs = [pl.BlockSpec((more[0].shape[0], tmo), lambda i, j, k: (0, i)),
                  pl.BlockSpec((more[1].shape[0], tn), lambda i, j, k: (0, j))] if more else []
    return pl.pallas_call(
        body, name=name, grid=(m // tmo, n // tn, gt),
        in_specs=[pl.BlockSpec((tt, tmo), lambda i, j, k: (k, i)), pl.BlockSpec((tt, tn), lambda i, j, k: (k, j))] + more_specs,
        out_specs=pl.BlockSpec((tmo, tn), lambda i, j, k: (i, j)), out_shape=jax.ShapeDtypeStruct((m, n), out_dtype),
        scratch_shapes=[pltpu.VMEM((tmo, tn), F32)], compiler_params=_cp(3, vmem_mib),
    )(a, b, *more)


def _row_spec(d):
    return pl.BlockSpec((1, d), lambda *_: (0, 0))


def _stat_spec(k, d):
    return pl.BlockSpec((k, 8, d), lambda *_: (0, 0, 0))


def _rope(z, cs, sn):
    first = (lax.broadcasted_iota(jnp.int32, (z.shape[0], 128), 1) % 32) < 16
    outs = []
    for j in range(z.shape[1] // 128):
        zc = z[:, 128 * j:128 * (j + 1)]
        partner = jnp.where(first, pltpu.roll(zc, 112, 1), pltpu.roll(zc, 16, 1))
        outs.append(zc * cs + partner * sn)
    return outs[0] if len(outs) == 1 else jnp.concatenate(outs, axis=1)


def _rope_tables(length, rotate, zero=0.0):
    if not rotate:
        return jnp.ones((length, 128), F32), jnp.zeros((length, 128), F32)
    half = HEAD_DIM // 2
    t = jnp.arange(length)
    row = (t // GRID_W).astype(F32) + zero
    col = (t % GRID_W).astype(F32)
    e = jnp.arange(128) % HEAD_DIM
    inv_freq = ROPE_BASE ** (-(2 * ((e % half) % (half // 2))).astype(F32) / half)
    pos = jnp.where(e[None, :] < half, row[:, None], col[:, None])
    ang = pos * inv_freq[None, :]
    first = ((e % half) < half // 2)[None, :]
    return jnp.cos(ang), jnp.where(first, -jnp.sin(ang), jnp.sin(ang))


def _mixer_in(name, x, nw, sh, sc, w_in, cos, sin, after):
    t, d = x.shape
    tm = _pick(t, 256, 128)
    n_in = w_in.shape[1]

    def body(x_ref, nw_ref, sh_ref, sc_ref, w_ref, cos_ref, sin_ref, after_ref, h_ref, q_ref, k_ref, v_ref, u_ref):
        xf = x_ref[...]
        r = lax.rsqrt(jnp.mean(xf * xf, axis=-1, keepdims=True) + EPS)
        hb = (((xf * r) * nw_ref[...]) * (1.0 + sc_ref[...]) + sh_ref[...]).astype(BF16)
        h_ref[...] = hb
        p = _dot(hb, w_ref[...])
        cs, sn = cos_ref[...], sin_ref[...]
        q_ref[...] = (_rope(p[:, :ATTN_WIDTH], cs, sn) * SCALE).astype(BF16)
        k_ref[...] = _rope(p[:, ATTN_WIDTH:ATTN_WIDTH + KV_WIDTH], cs, sn).astype(BF16)
        v_ref[...] = p[:, ATTN_WIDTH + KV_WIDTH:ATTN_WIDTH + 2 * KV_WIDTH].astype(BF16)
        u_ref[...] = p[:, ATTN_WIDTH + 2 * KV_WIDTH:]

    def tile(w):
        return pl.BlockSpec((tm, w), lambda i: (i, 0))

    return pl.pallas_call(
        body, name=name, grid=(t // tm,),
        in_specs=[tile(d), _row_spec(d), _row_spec(d), _row_spec(d), pl.BlockSpec((d, n_in), lambda i: (0, 0)),
                  tile(128), tile(128), ANY],
        out_specs=(tile(d), tile(ATTN_WIDTH), tile(KV_WIDTH), tile(KV_WIDTH), tile(POOL_WIDTH)),
        out_shape=(jax.ShapeDtypeStruct((t, d), BF16), jax.ShapeDtypeStruct((t, ATTN_WIDTH), BF16),
                   jax.ShapeDtypeStruct((t, KV_WIDTH), BF16), jax.ShapeDtypeStruct((t, KV_WIDTH), BF16),
                   jax.ShapeDtypeStruct((t, POOL_WIDTH), F32)),
        compiler_params=_cp(1),
    )(x, nw, sh, sc, w_in, cos, sin, after)


def _attn_specs(nb, n_ctx):
    def blk(w, f):
        return pl.BlockSpec((BLOCK, w), lambda n: (f(n), 0))

    prev = lambda n: jnp.maximum(n - 1, 0)
    cur = lambda n: n
    nxt = lambda n: jnp.minimum(n + 1, nb - 1)
    kv = [blk(KV_WIDTH, prev), blk(KV_WIDTH, cur), blk(KV_WIDTH, nxt)]
    ctx = pl.BlockSpec((n_ctx, KV_WIDTH), lambda n: (0, 0))
    return [pl.BlockSpec(memory_space=pltpu.SMEM), blk(ATTN_WIDTH, cur)] + kv + kv + [ctx, ctx]


def _attn_mask(n, length, n_keys):
    row = lax.broadcasted_iota(jnp.int32, (GROUP * BLOCK, n_keys), 0) % BLOCK
    col = lax.broadcasted_iota(jnp.int32, (GROUP * BLOCK, n_keys), 1)
    kpos = (n - 1) * BLOCK + col
    return ((jnp.abs(col - BLOCK - row) <= BLOCK) & (kpos >= 0) & (kpos < length)) | (col >= 3 * BLOCK)


def _group_rows(block, g):
    return jnp.concatenate([block[:, HEAD_DIM * h:HEAD_DIM * (h + 1)] for h in range(GROUP * g, GROUP * (g + 1))], axis=0)


def _group_sink(sink_ref, g):
    head = lax.broadcasted_iota(jnp.int32, (GROUP * BLOCK, 1), 0) // BLOCK
    out = jnp.full((GROUP * BLOCK, 1), sink_ref[0, GROUP * g], F32)
    for j in range(1, GROUP):
        out = jnp.where(head == j, sink_ref[0, GROUP * g + j], out)
    return out


def _attn_fwd(q, k, v, kc, vc, sink):
    length = q.shape[0]
    nb = length // BLOCK
    n_ctx = kc.shape[0]
    n_keys = 3 * BLOCK + n_ctx

    def body(sink_ref, q_ref, kp, k0, kn, vp, v0, vn, kc_ref, vc_ref, o_ref, p_ref):
        n = pl.program_id(0)
        valid = _attn_mask(n, length, n_keys)
        qb = q_ref[...]
        kall = jnp.concatenate([kp[...], k0[...], kn[...], kc_ref[...]], axis=0)
        vall = jnp.concatenate([vp[...], v0[...], vn[...], vc_ref[...]], axis=0)
        outs = []
        for g in range(N_KV_HEADS):
            lanes = slice(HEAD_DIM * g, HEAD_DIM * (g + 1))
            s = jnp.where(valid, _dot_nt(_group_rows(qb, g), kall[:, lanes]), NEG_INF)
            sk = _group_sink(sink_ref, g)
            m = jnp.maximum(jnp.max(s, axis=-1, keepdims=True), sk)
            e = jnp.exp(s - m)
            e_sink = jnp.exp(sk - m)
            inv = 1.0 / (jnp.sum(e, axis=-1, keepdims=True) + e_sink)
            pb = (e * inv).astype(BF16)
            p_ref[0, g, :, :n_keys] = pb
            p_ref[0, g, :, n_keys:] = jnp.broadcast_to(e_sink * inv, (GROUP * BLOCK, 128)).astype(BF16)
            o = _dot(pb, vall[:, lanes])
            outs += [o[BLOCK * j:BLOCK * (j + 1)] for j in range(GROUP)]
        o_ref[...] = jnp.concatenate(outs, axis=1).astype(BF16)

    return pl.pallas_call(
        body, name="attn_fwd", grid=(nb,), in_specs=_attn_specs(nb, n_ctx),
        out_specs=(pl.BlockSpec((BLOCK, ATTN_WIDTH), lambda n: (n, 0)),
                   pl.BlockSpec((1, N_KV_HEADS, GROUP * BLOCK, n_keys + 128), lambda n: (n, 0, 0, 0))),
        out_shape=(jax.ShapeDtypeStruct((length, ATTN_WIDTH + POOL_WIDTH), BF16),
                   jax.ShapeDtypeStruct((nb, N_KV_HEADS, GROUP * BLOCK, n_keys + 128), BF16)), compiler_params=_cp(1),
    )(sink, q, k, k, k, v, v, v, kc, vc)


def _attn_bwd(q, k, v, kc, vc, dmix, probs):
    length = q.shape[0]
    nb = length // BLOCK
    n_ctx = kc.shape[0]
    n_keys = 3 * BLOCK + n_ctx

    def body(q_ref, kp, k0, kn, vp, v0, vn, kc_ref, vc_ref, do_ref, p_ref,
             dq_ref, dkp_ref, dvp_ref, dkc_ref, dvc_ref, dsink_ref):
        n = pl.program_id(0)

        @pl.when(n == 0)
        def _():
            dkc_ref[...] = jnp.zeros_like(dkc_ref)
            dvc_ref[...] = jnp.zeros_like(dvc_ref)
            dsink_ref[...] = jnp.zeros_like(dsink_ref)

        qb, dob = q_ref[...], do_ref[...]
        kall = jnp.concatenate([kp[...], k0[...], kn[...], kc_ref[...]], axis=0)
        vall = jnp.concatenate([vp[...], v0[...], vn[...], vc_ref[...]], axis=0)
        srow = lax.broadcasted_iota(jnp.int32, (8, 128), 0)
        slane = lax.broadcasted_iota(jnp.int32, (8, 128), 1)
        dqs, dks, dvs = [], [], []
        dsink = jnp.zeros((8, 128), F32)
        for g in range(N_KV_HEADS):
            lanes = slice(HEAD_DIM * g, HEAD_DIM * (g + 1))
            kg, vg = kall[:, lanes], vall[:, lanes]
            qg, dog = _group_rows(qb, g), _group_rows(dob, g)
            pb = p_ref[0, g, :, :n_keys]
            p = pb.astype(F32)
            dp = _dot_nt(dog, vg)
            delta = jnp.sum(p * dp, axis=-1, keepdims=True)
            ds = (p * (dp - delta)).astype(BF16)
            dq = _dot(ds, kg) * SCALE
            dqs += [dq[BLOCK * j:BLOCK * (j + 1)] for j in range(GROUP)]
            dks.append(_dot_tn(ds, qg))
            dvs.append(_dot_tn(pb, dog))
            d_sink = p_ref[0, g, :, n_keys:].astype(F32)[:, :1] * delta
            for j in range(GROUP):
                total = -jnp.sum(d_sink[BLOCK * j:BLOCK * (j + 1)], axis=0, keepdims=True)
                dsink = dsink + jnp.where((srow == 0) & (slane == GROUP * g + j), total, 0.0)
        dq_ref[...] = jnp.concatenate(dqs, axis=1)
        dk = jnp.concatenate(dks, axis=1)
        dv = jnp.concatenate(dvs, axis=1)
        for j in range(3):
            dkp_ref[0, j] = dk[BLOCK * j:BLOCK * (j + 1)]
            dvp_ref[0, j] = dv[BLOCK * j:BLOCK * (j + 1)]
        dkc_ref[...] += dk[3 * BLOCK:]
        dvc_ref[...] += dv[3 * BLOCK:]
        dsink_ref[...] += dsink

    part = pl.BlockSpec((1, 3, BLOCK, KV_WIDTH), lambda n: (n, 0, 0, 0))
    ctx = pl.BlockSpec((n_ctx, KV_WIDTH), lambda n: (0, 0))
    return pl.pallas_call(
        body, name="attn_bwd", grid=(nb,),
        in_specs=_attn_specs(nb, n_ctx)[1:] + [pl.BlockSpec((BLOCK, ATTN_WIDTH), lambda n: (n, 0)),
                                           pl.BlockSpec((1,) + probs.shape[1:], lambda n: (n, 0, 0, 0))],
        out_specs=(pl.BlockSpec((BLOCK, ATTN_WIDTH), lambda n: (n, 0)), part, part, ctx, ctx,
                   pl.BlockSpec((8, 128), lambda n: (0, 0))),
        out_shape=(jax.ShapeDtypeStruct((length, ATTN_WIDTH), F32),
                   jax.ShapeDtypeStruct((nb, 3, BLOCK, KV_WIDTH), F32), jax.ShapeDtypeStruct((nb, 3, BLOCK, KV_WIDTH), F32),
                   jax.ShapeDtypeStruct((n_ctx, KV_WIDTH), F32), jax.ShapeDtypeStruct((n_ctx, KV_WIDTH), F32),
                   jax.ShapeDtypeStruct((8, 128), F32)),
        compiler_params=_cp(1),
    )(q, k, k, k, v, v, v, kc, vc, dmix, probs)


def _assemble_dp(dq, dkp, dvp, du, cos, sin, after):
    length = dq.shape[0]
    nb = length // BLOCK

    def body(dq_ref, dka, dkb, dkc, dva, dvb, dvc, du_ref, cos_ref, sin_ref, after_ref, o_ref):
        n = pl.program_id(0)
        has_next = (n + 1 < nb).astype(F32)
        has_prev = (n > 0).astype(F32)
        cs, sn = cos_ref[...], -sin_ref[...]
        dk = dka[0, 0] * has_next + dkb[0, 0] + dkc[0, 0] * has_prev
        dv = dva[0, 0] * has_next + dvb[0, 0] + dvc[0, 0] * has_prev
        o_ref[:, :ATTN_WIDTH] = _rope(dq_ref[...], cs, sn).astype(BF16)
        o_ref[:, ATTN_WIDTH:ATTN_WIDTH + KV_WIDTH] = _rope(dk, cs, sn).astype(BF16)
        o_ref[:, ATTN_WIDTH + KV_WIDTH:ATTN_WIDTH + 2 * KV_WIDTH] = dv.astype(BF16)
        o_ref[:, ATTN_WIDTH + 2 * KV_WIDTH:] = du_ref[...]

    def part(slot, f):
        return pl.BlockSpec((1, 1, BLOCK, KV_WIDTH), lambda n: (f(n), slot, 0, 0))

    parts = [part(0, lambda n: jnp.minimum(n + 1, nb - 1)), part(1, lambda n: n), part(2, lambda n: jnp.maximum(n - 1, 0))]

    def tile(w):
        return pl.BlockSpec((BLOCK, w), lambda n: (n, 0))

    width = ATTN_WIDTH + 2 * KV_WIDTH + POOL_WIDTH
    return pl.pallas_call(
        body, name="assemble_dp", grid=(nb,),
        in_specs=[tile(ATTN_WIDTH)] + parts + parts + [tile(POOL_WIDTH), tile(128), tile(128), ANY],
        out_specs=tile(width), out_shape=jax.ShapeDtypeStruct((length, width), BF16), compiler_params=_cp(1),
    )(dq, dkp, dkp, dkp, dvp, dvp, dvp, du, cos, sin, after)


def _shift_rows(e, s):
    n = e.shape[0]
    return e if s % n == 0 else pltpu.roll(e, (-s) % n, 0)


def _window_sum(e, w, first):
    s, n = e, 1
    while n < w:
        s = s + _shift_rows(s, n)
        n *= 2
    return _shift_rows(s, first)


def _pool_geometry(i, tm, length):
    pos = i * tm - HALO + lax.broadcasted_iota(jnp.int32, (tm + 2 * HALO, 1), 0)
    inside = (pos >= 0) & (pos < length)
    inv_counts = []
    for w in POOL_WINDOWS:
        lo = jnp.clip(pos - w // 2, 0, length)
        hi = jnp.clip(pos - w // 2 + w, 0, length)
        inv_counts.append(1.0 / jnp.maximum(hi - lo, 1).astype(F32))
    return inside, inv_counts


def _halo_specs(tm, width, length, col=0):
    per = tm // HALO
    last = length // HALO - 1
    return [pl.BlockSpec((HALO, width), lambda i: (jnp.maximum(i * per - 1, 0), col)),
            pl.BlockSpec((tm, width), lambda i: (i, col)),
            pl.BlockSpec((HALO, width), lambda i: (jnp.minimum((i + 1) * per, last), col))]


def _pooled(ext, inv_counts, tm):
    outs = []
    for g, w in enumerate(POOL_WINDOWS):
        e = ext[:, POOL_GROUP_DIM * g:POOL_GROUP_DIM * (g + 1)]
        mean = _window_sum(e, w, -(w // 2)) * inv_counts[g]
        outs.append((mean - e)[HALO:HALO + tm])
    return outs


def _pool_fwd(u, pool_w, pool_scale, mix):
    length = u.shape[0]
    tm = _pick(length, 256, 128)

    def body(up, u0, un, w_ref, sc_ref, mix_ref, o_ref):
        inside, inv_counts = _pool_geometry(pl.program_id(0), tm, length)
        ext = jnp.where(inside, jnp.concatenate([up[...], u0[...], un[...]], axis=0), 0.0)
        pooled = _pooled(ext, inv_counts, tm)
        mixed = [_dot(pooled[g].astype(BF16), w_ref[g]) for g in range(len(POOL_WINDOWS))]
        o_ref[...] = (jnp.concatenate(mixed, axis=1) * sc_ref[...]).astype(BF16)

    return pl.pallas_call(
        body, name="pool_fwd", grid=(length // tm,),
        in_specs=_halo_specs(tm, POOL_WIDTH, length) + [pl.BlockSpec(pool_w.shape, lambda i: (0, 0, 0)), _row_spec(POOL_WIDTH), ANY],
        out_specs=pl.BlockSpec((tm, POOL_WIDTH), lambda i: (i, 1)),
        out_shape=jax.ShapeDtypeStruct(mix.shape, BF16), input_output_aliases={5: 0}, compiler_params=_cp(1),
    )(u, u, u, pool_w, pool_scale, mix)


def _pool_bwd(u, dmix, pool_w, pool_scale, after):
    length = u.shape[0]
    tm = _pick(length, 256, 128)
    n_g = len(POOL_WINDOWS)

    def body(up, u0, un, dp_, d0, dn_, w_ref, sc_ref, after_ref, du_ref, dw_ref, dsc_ref):
        i = pl.program_id(0)

        @pl.when(i == 0)
        def _():
            dw_ref[...] = jnp.zeros_like(dw_ref)
            dsc_ref[...] = jnp.zeros_like(dsc_ref)

        inside, inv_counts = _pool_geometry(i, tm, length)
        ext = jnp.where(inside, jnp.concatenate([up[...], u0[...], un[...]], axis=0), 0.0)
        dext = jnp.where(inside, jnp.concatenate([dp_[...], d0[...], dn_[...]], axis=0).astype(F32), 0.0)
        dmixed = (dext * sc_ref[...]).astype(BF16)
        pooled = _pooled(ext, inv_counts, tm)
        dus, dscs = [], []
        for g, w in enumerate(POOL_WINDOWS):
            lanes = slice(POOL_GROUP_DIM * g, POOL_GROUP_DIM * (g + 1))
            dpooled = _dot_nt(dmixed[:, lanes], w_ref[g])
            spread = _window_sum(dpooled * inv_counts[g], w, -(w // 2 - 1))
            dus.append((spread - dpooled)[HALO:HALO + tm])
            pb = pooled[g].astype(BF16)
            dw_ref[g] += _dot_tn(pb, dmixed[HALO:HALO + tm, lanes])
            prod = dext[HALO:HALO + tm, lanes] * _dot(pb, w_ref[g])
            dscs.append(_fold8(prod))
        du_ref[...] = jnp.concatenate(dus, axis=1).astype(BF16)
        dsc_ref[...] += jnp.concatenate(dscs, axis=1)

    return pl.pallas_call(
        body, name="pool_bwd", grid=(length // tm,),
        in_specs=_halo_specs(tm, POOL_WIDTH, length) + _halo_specs(tm, POOL_WIDTH, length, col=1)
        + [pl.BlockSpec(pool_w.shape, lambda i: (0, 0, 0)), _row_spec(POOL_WIDTH), ANY],
        out_specs=(pl.BlockSpec((tm, POOL_WIDTH), lambda i: (i, 0)), pl.BlockSpec((n_g, POOL_GROUP_DIM, POOL_GROUP_DIM), lambda i: (0, 0, 0)),
                   pl.BlockSpec((8, POOL_WIDTH), lambda i: (0, 0))),
        out_shape=(jax.ShapeDtypeStruct((length, POOL_WIDTH), BF16), jax.ShapeDtypeStruct((n_g, POOL_GROUP_DIM, POOL_GROUP_DIM), F32),
                   jax.ShapeDtypeStruct((8, POOL_WIDTH), F32)),
        compiler_params=_cp(1),
    )(u, u, u, dmix, dmix, dmix, pool_w, pool_scale, after)


def _mixer_out(mix, w_out, x, g_a, nmw, sh_m, sc_m, after):
    t, d = x.shape

    def epi(mo, rs, tiles, vecs, outs, st_ref, live):
        ga, nw, sh, sc = vecs
        x1_ref, mo_ref, hm_ref = outs
        x1 = tiles[0][rs, :] + ga[...] * mo
        x1_ref[rs, :] = x1
        mo_ref[rs, :] = mo.astype(BF16)
        r = lax.rsqrt(jnp.mean(x1 * x1, axis=-1, keepdims=True) + EPS)
        hm_ref[rs, :] = (((x1 * r) * nw[...]) * (1.0 + sc[...]) + sh[...]).astype(BF16)

    return _mm_deferred("mixer_out", mix, w_out, nt=False, tm=_pick(t, 256, 128), epi=epi, tiles=(x,), vecs=(g_a, nmw, sh_m, sc_m),
                        out_tiles=(jax.ShapeDtypeStruct((t, d), F32), jax.ShapeDtypeStruct((t, d), BF16), jax.ShapeDtypeStruct((t, d), BF16)),
                        n_stats=1, after=after)[:3]


def _mlp_up(hm, w_up):
    t, d = hm.shape
    tm = _pick(t, 1024, 512, 256, 128)
    tn = 2048

    def epi(acc, ex, outs):
        outs[0][...] = jnp.square(jnp.maximum(acc[...], 0.0)).astype(BF16)

    return _mm("mlp_up", hm, w_up, nt=False, tm=tm, tn=tn, tk=d, epi=epi,
               out_shape=(jax.ShapeDtypeStruct((t, w_up.shape[1]), BF16),),
               out_specs=(pl.BlockSpec((tm, tn), lambda j, i, k: (i, j)),))[0]


def _mlp_down_loss(act, w_down, x1, target, g_m, fw, after):
    t, d = x1.shape

    def epi(dnv, rs, tiles, vecs, outs, st_ref, live):
        x1_ref, t_ref = tiles
        gm, fw_ref = vecs
        dx2_ref, ddn_ref = outs
        x2 = x1_ref[rs, :] + gm[...] * dnv
        r = lax.rsqrt(jnp.mean(x2 * x2, axis=-1, keepdims=True) + EPS)
        xh = x2 * r
        diff = xh * fw_ref[...] - t_ref[rs, :]
        dy = diff * (1.0 / d)
        dxh = dy * fw_ref[...]
        dx2 = r * (dxh - xh * jnp.mean(dxh * xh, axis=-1, keepdims=True))
        dx2_ref[rs, :] = dx2
        ddn_ref[rs, :] = (dx2 * gm[...]).astype(BF16)
        st_ref[0] += jnp.where(live, _fold8(diff * diff), 0.0)
        st_ref[1] += jnp.where(live, _fold8(dy * xh), 0.0)
        st_ref[2] += jnp.where(live, _fold8(dx2 * dnv), 0.0)

    return _mm_k_deferred("mlp_down_loss", act, w_down, nt=False, tm=_pick(t, 512, 256), tk=_pick(act.shape[1], 2048), epi=epi,
                          tiles=(x1, target), vecs=(g_m, fw), n_stats=3, after=after,
                          out_tiles=(jax.ShapeDtypeStruct((t, d), F32), jax.ShapeDtypeStruct((t, d), BF16)))


def _mlp_dx(dup, w_up, x1, dx2, mo, nmw, sc_m, g_a, after):
    t, d = x1.shape

    def epi(dh, rs, tiles, vecs, outs, st_ref, live):
        x1_ref, dx2_ref, mo_ref = tiles
        nw, sc, ga = vecs
        dx1_ref, dmi_ref = outs
        dx1 = _norm_bwd_rows(dh, x1_ref[rs, :], nw[...], sc[...], st_ref) + dx2_ref[rs, :]
        dx1_ref[rs, :] = dx1
        dmi_ref[rs, :] = (dx1 * ga[...]).astype(BF16)
        st_ref[3] += jnp.where(live, _fold8(dx1 * mo_ref[rs, :].astype(F32)), 0.0)

    return _mm_k_deferred("mlp_dx", dup, w_up, nt=True, tm=_pick(t, 512, 256), tk=_pick(dup.shape[1], 2048), epi=epi,
                          tiles=(x1, dx2, mo), vecs=(nmw, sc_m, g_a), n_stats=4, after=after,
                          out_tiles=(jax.ShapeDtypeStruct((t, d), F32), jax.ShapeDtypeStruct((t, d), BF16)))


def _mlp_dact(ddn, w_down, act):
    t, d = ddn.shape
    tm = _pick(t, 512, 256, 128)
    tn = 2048

    def epi(acc, ex, outs):
        outs[0][...] = (acc[...] * (2.0 * jnp.sqrt(ex[0][...]).astype(F32))).astype(BF16)

    tile = pl.BlockSpec((tm, tn), lambda j, i, k: (i, j))
    return _mm("mlp_dact", ddn, w_down, nt=True, tm=tm, tn=tn, tk=d, epi=epi, extras=(act,), extra_specs=[tile],
               out_shape=(jax.ShapeDtypeStruct(act.shape, BF16),), out_specs=(tile,))[0]


def _norm_bwd_rows(dh, xv, nw, sc, st_ref):
    r = lax.rsqrt(jnp.mean(xv * xv, axis=-1, keepdims=True) + EPS)
    xh = xv * r
    dy = dh * (1.0 + sc)
    st_ref[0] += _fold8(dh)
    st_ref[1] += _fold8(dh * (xh * nw))
    st_ref[2] += _fold8(dy * xh)
    dxh = dy * nw
    return r * (dxh - xh * jnp.mean(dxh * xh, axis=-1, keepdims=True))


def _mixer_dmix(dmi, w_out, after):
    t, d = dmi.shape
    tm = _pick(t, 512, 256, 128)

    def epi(acc, ex, outs):
        outs[0][...] = acc[...].astype(BF16)

    n = w_out.shape[0]
    return _mm("mixer_dmix", dmi, w_out, nt=True, tm=tm, tn=n, tk=d, epi=epi, after=after,
               out_shape=(jax.ShapeDtypeStruct((t, n), BF16),), out_specs=(pl.BlockSpec((tm, n), lambda j, i, k: (i, 0)),))[0]


def _mixer_dx(name, dp, w_in, x, dx1, naw, sc_a, after):
    t, d = x.shape

    def epi(dh, rs, tiles, vecs, outs, st_ref, live):
        x_ref, dx1_ref = tiles
        nw, sc = vecs
        outs[0][rs, :] = _norm_bwd_rows(dh, x_ref[rs, :], nw[...], sc[...], st_ref) + dx1_ref[rs, :]

    return _mm_deferred(name, dp, w_in, nt=True, tm=_pick(t, 256, 128), epi=epi, tiles=(x, dx1), vecs=(naw, sc_a),
                        out_tiles=(jax.ShapeDtypeStruct((t, d), F32),), n_stats=3, after=after, vmem_mib=56)


def _silu(v):
    return v / (1.0 + jnp.exp(-v))


def _ada_fwd(cond, w_ada, b_ada):
    d, n = w_ada.shape
    tn = 512

    def body(c_ref, w_ref, b_ref, o_ref):
        o_ref[...] = _dot(_silu(c_ref[...]).astype(BF16), w_ref[...].astype(BF16)) + b_ref[...]

    return pl.pallas_call(
        body, name="ada_fwd", grid=(n // tn,),
        in_specs=[pl.BlockSpec(cond.shape, lambda j: (0, 0)), pl.BlockSpec((d, tn), lambda j: (0, j)), pl.BlockSpec((1, tn), lambda j: (0, j))],
        out_specs=pl.BlockSpec((cond.shape[0], tn), lambda j: (0, j)), out_shape=jax.ShapeDtypeStruct((cond.shape[0], n), F32),
        compiler_params=_cp(1),
    )(cond, w_ada, b_ada)


def _adamw_math(w, g, m, v):
    m = ADAM_B1 * m + (1.0 - ADAM_B1) * g
    v = ADAM_B2 * v + (1.0 - ADAM_B2) * jnp.square(g)
    m_hat = m / (1.0 - ADAM_B1 ** ADAM_STEP)
    v_hat = v / (1.0 - ADAM_B2 ** ADAM_STEP)
    return -ADAM_LR * (m_hat / (jnp.sqrt(v_hat) + ADAM_EPS) + ADAM_WD * w), m, v


def _ada_bwd(cond, dm, w_ada, m_ada, v_ada):
    d, n = w_ada.shape
    tn = 256
    rows = cond.shape[0]

    def body(c_ref, dm_ref, w_ref, m_ref, v_ref, g_ref, dl_ref, nm_ref, nv_ref, pc_ref):
        @pl.when(pl.program_id(0) == 0)
        def _():
            pc_ref[...] = jnp.zeros_like(pc_ref)

        dmb = dm_ref[...].astype(BF16)
        w = w_ref[...]
        g = _dot_tn(_silu(c_ref[...]).astype(BF16), dmb)
        g_ref[...] = g
        dl_ref[...], nm_ref[...], nv_ref[...] = _adamw_math(w, g, m_ref[...], v_ref[...])
        pc_ref[...] += _dot_nt(dm_ref[8:16, :].astype(BF16), w.astype(BF16))

    tile = pl.BlockSpec((d, tn), lambda j: (0, j))
    like = jax.ShapeDtypeStruct((d, n), F32)
    return pl.pallas_call(
        body, name="ada_bwd", grid=(n // tn,),
        in_specs=[pl.BlockSpec((rows, d), lambda j: (0, 0)), pl.BlockSpec((rows, tn), lambda j: (0, j)), tile, tile, tile],
        out_specs=(tile, tile, tile, tile, pl.BlockSpec((8, d), lambda j: (0, 0))),
        out_shape=(like, like, like, like, jax.ShapeDtypeStruct((8, d), F32)), compiler_params=_cp(1),
    )(cond, dm, w_ada, m_ada, v_ada)


def _adamw(name, w, g, m, v):
    return _ew(name, lambda w_, g_, m_, v_: (g_,) + _adamw_math(w_, g_, m_, v_), [w, g, m, v], [F32, F32, F32, F32])


def _colsum(st):
    return jnp.sum(st, axis=1)


def kernel(x, c, ctx, c_ctx, norm_attn_w, norm_mlp_w, w_ada, b_ada, w_in, attn_sink, pool_w, pool_scale, w_out, w_mlp_up, w_mlp_down, final_norm_w, loss_target, m_c_ctx, m_norm_attn_w, m_norm_mlp_w, m_w_ada, m_b_ada, m_w_in, m_attn_sink, m_pool_w, m_pool_scale, m_w_out, m_w_mlp_up, m_w_mlp_down, m_final_norm_w, v_c_ctx, v_norm_attn_w, v_norm_mlp_w, v_w_ada, v_b_ada, v_w_in, v_attn_sink, v_pool_w, v_pool_scale, v_w_out, v_w_mlp_up, v_w_mlp_down, v_final_norm_w):
    length, d = x.shape[1], x.shape[2]
    n_ctx = ctx.shape[1]
    pos = (lax.axis_index("x"), lax.axis_index("y"), lax.axis_index("c"))
    me, chip = _dev_index(pos), _chip_index(pos)
    xs, tgt, cx = x.reshape(length, d), loss_target.reshape(length, d), ctx.reshape(n_ctx, d)
    n_ada = w_ada.shape[2]

    c_all = _allgather8("gather_c", jnp.pad(c, ((0, 7), (0, 0))))
    mixer_bigs = [_Big("col", w_in.shape[1:]), _Big("pool", pool_w.shape[1:]), _Big("row", w_out.shape[1:])]
    mlp_bigs = [_Big("col", w_mlp_up.shape[1:]), _Big("row", w_mlp_down.shape[1:])]
    placed = [_cast_place(f"place_{i}", b, s, c_all)[0] for i, (b, s) in enumerate(zip(mixer_bigs, [w_in[0], pool_w[0], w_out[0]]))]
    flight = _split("gather_mixer_ici", placed, _gather_ici_remote(mixer_bigs, 0))
    token, placed_mlp = flight[3], []
    for i, (b, s) in enumerate(zip(mlp_bigs, [w_mlp_up[0], w_mlp_down[0]])):
        p, token = _cast_place(f"place_mlp_{i}", b, s, token)
        placed_mlp.append(p)
    cos, sin = _rope_tables(length, True, token[0, 0])
    cond = jnp.concatenate([c_all[:, 0, :], jnp.pad(c_ctx[None, :], ((0, 7), (0, 0)))], axis=0) + 0.0 * cos[0, 0]
    b_shard = lax.dynamic_slice_in_dim(b_ada, chip * n_ada, n_ada, axis=1)
    mod_all = _allgather8("gather_mod", _ada_fwd(cond, w_ada[0], b_shard))
    mod = jnp.concatenate([mod_all[0], mod_all[2], mod_all[4], mod_all[6]], axis=1)
    mine = lax.dynamic_slice_in_dim(mod, me, 1, axis=0)
    sh_a, sc_a, g_a, sh_m, sc_m, g_m = [mine[:, d * i:d * (i + 1)] for i in range(6)]
    csh_a, csc_a = mod[8:9, :d], mod[8:9, d:2 * d]

    win_b, pw_b, wout_b = _exchange("gather_mixer_d2d", _join(flight, mod), [jax.ShapeDtypeStruct(b.full_shape, BF16) for b in mixer_bigs],
                                    _gather_d2d_remote(mixer_bigs, 3), aliases={0: 0, 1: 1, 2: 2})
    wout_b = wout_b.reshape(-1, d)
    flight = _split("gather_mlp_ici", placed_mlp, _gather_ici_remote(mlp_bigs, 0), after=pw_b)

    one, zero = _rope_tables(n_ctx, False)
    h, q, k, v, u = _mixer_in("mixer_in", xs, norm_attn_w, sh_a, sc_a, win_b, cos, sin, flight[3])
    hc, _, kc, vc, _ = _mixer_in("mixer_in_ctx", cx, norm_attn_w, csh_a, csc_a, win_b, one, zero, flight[3])
    attn, probs = _attn_fwd(q, k, v, kc, vc, attn_sink)
    mix = _pool_fwd(u, pw_b, pool_scale, attn)
    flight = _split("gather_mlp_d2d", _join(flight, mix), _gather_d2d_remote(mlp_bigs, 0))
    x1, mo, hm = _mixer_out(mix, wout_b, xs, g_a, norm_mlp_w, sh_m, sc_m, flight[3])
    wup_b, wdn_b = _join(flight, hm)
    wdn_b = wdn_b.reshape(-1, d)
    act = _mlp_up(hm, wup_b)
    dx2, ddn, st_loss = _mlp_down_loss(act, wdn_b, x1, tgt, g_m, final_norm_w[None, :], c)
    st_loss = _colsum(st_loss)
    loss = lax.psum(0.5 / d * jnp.sum(st_loss[0]), ("x", "y", "c"))

    tt = _pick(length, 2048, 1024, 512, 256, 128)
    g_wdn = _mm_tn("grad_w_down", act, ddn, BF16, tmo=1024, tn=d, tt=tt)
    dup = _mlp_dact(ddn, wdn_b, act)
    g_wup = _mm_tn("grad_w_up", hm, dup, BF16, tmo=d, tn=1024, tt=tt)
    empty = lambda shapes: [lax.empty(s.shape, s.dtype) for s in shapes]
    grads = [g_wup, g_wdn.reshape(mlp_bigs[1].full_shape)]
    flight = _split("reduce_mlp_d2d", grads + empty(_halves(mlp_bigs)), _reduce_d2d_remote(mlp_bigs))
    dx1, dmi, st_mlp = _mlp_dx(dup, wup_b, x1, dx2, mo, norm_mlp_w, sc_m, g_a, flight[3])
    st_mlp = _colsum(st_mlp)
    landed = _join(flight, dmi)
    mlp_chip = _chip_sums("mlp", mlp_bigs, landed[:2], landed[2:])
    flight = _split("reduce_mlp_ici", mlp_chip + empty(_thirds(mlp_bigs)), _reduce_ici_remote(mlp_bigs))
    g_wout = _mm_tn("grad_w_out", mix, dmi, BF16, tmo=1024, tn=d, tt=tt)
    dmix = _mixer_dmix(dmi, wout_b, flight[3])
    dq, dkp, dvp, dkc, dvc, dsink = _attn_bwd(q, k, v, kc, vc, dmix, probs)
    landed = _join(flight, dq)
    flight = _split("reduce_mlp_share", _piece_sums("mlp", mlp_bigs, landed[:2], landed[2:]), _share_remote(mlp_bigs, 0))
    du, g_pw, st_pool = _pool_bwd(u, dmix, pw_b, pool_scale, flight[3])
    g_mlp = _join(flight, du)

    wo_bigs, win_bigs = mixer_bigs[1:], mixer_bigs[:1]
    wo_chip = _reduce_to_chip("wo", wo_bigs, [g_pw.astype(BF16), g_wout.reshape(wo_bigs[1].full_shape)])
    flight = _split("reduce_wo_ici", wo_chip + empty(_thirds(wo_bigs)), _reduce_ici_remote(wo_bigs))
    dp = _assemble_dp(dq, dkp, dvp, du, cos, sin, flight[3])
    dpc = jnp.concatenate([jnp.zeros((n_ctx, ATTN_WIDTH), BF16), dkc.astype(BF16), dvc.astype(BF16),
                           jnp.zeros((n_ctx, POOL_WIDTH), BF16)], axis=1)
    g_win = _mm_tn("grad_w_in", h, dp, BF16, tmo=d, tn=dp.shape[1] // 2, tt=_pick(length, 1024, 512, 256, 128), more=(hc, dpc))
    wo_landed = _join(flight, g_win)
    win_chip = _reduce_to_chip("win", win_bigs, [g_win])
    flight = _split("reduce_win_ici", win_chip + empty(_thirds(win_bigs)), _reduce_ici_remote(win_bigs))
    grad_x, st_mix = _mixer_dx("mixer_dx", dp, win_b, xs, dx1, norm_attn_w, sc_a, flight[3])
    _, st_ctx = _mixer_dx("mixer_dx_ctx", dpc, win_b, cx, jnp.zeros((n_ctx, d), F32), norm_attn_w, csc_a, flight[3])
    st_mix, st_ctx = _colsum(st_mix), _colsum(st_ctx)
    win_landed = _join(flight, grad_x)
    g_mixer = (_reduce_finish("win", win_bigs, win_landed[:1], win_landed[1:])
               + _reduce_finish("wo", wo_bigs, wo_landed[:2], wo_landed[2:]))

    zrow = jnp.zeros((d,), F32)
    pad = lambda a: jnp.pad(a, (0, d - a.shape[0]))
    mine_rows = [st_mix[0], st_mix[1], st_mlp[3], st_mlp[0], st_mlp[1], st_loss[2],
                 st_ctx[0], st_ctx[1],
                 st_mix[2] + st_ctx[2], st_mlp[2], st_loss[1],
                 pad(jnp.sum(st_pool, axis=0)), pad(dsink[0, :N_Q_HEADS])] + [zrow] * 3
    small_all = _allgather8("gather_small", jnp.concatenate(mine_rows).reshape(len(mine_rows), d))
    small = small_all[0]
    for i in range(1, 8):
        small = small + small_all[i]
    dm_rows = small_all[:, 0:6, :].reshape(8, 6 * d)
    dm_ctx = jnp.concatenate([small[6], small[7], jnp.zeros((4 * d,), F32)])[None, :]
    dm = jnp.concatenate([dm_rows, jnp.pad(dm_ctx, ((0, 7), (0, 0)))], axis=0)
    g_bada = jnp.sum(dm[:9], axis=0, keepdims=True)
    dm_shard = lax.dynamic_slice_in_dim(dm, chip * n_ada, n_ada, axis=1)
    g_wada, dl_wada, nm_wada, nv_wada, part_cctx = _ada_bwd(cond, dm_shard, w_ada[0], m_w_ada[0], v_w_ada[0])
    cctx_all = _allgather8("gather_cctx", part_cctx)
    dsilu_in = cctx_all[0, 0] + cctx_all[2, 0] + cctx_all[4, 0] + cctx_all[6, 0]
    sig = 1.0 / (1.0 + jnp.exp(-c_ctx))
    g_cctx = dsilu_in * (sig * (1.0 + c_ctx * (1.0 - sig)))

    g_shards = g_mixer + g_mlp
    big_w = [w_in, pool_w, w_out, w_mlp_up, w_mlp_down]
    big_m = [m_w_in, m_pool_w, m_w_out, m_w_mlp_up, m_w_mlp_down]
    big_v = [v_w_in, v_pool_w, v_w_out, v_w_mlp_up, v_w_mlp_down]
    big_names = ["w_in", "pool_w", "w_out", "w_mlp_up", "w_mlp_down"]
    res = {}
    for nm, w_, g_, m_, v_ in zip(big_names, big_w, g_shards, big_m, big_v):
        res[nm] = tuple(_adamw("adamw_" + nm, w_, g_.reshape(w_.shape), m_, v_))
    res["w_ada"] = (g_wada[None], dl_wada[None], nm_wada[None], nv_wada[None])

    def pack(cc, na, nm_, ba, sk, ps, fn):
        flat = [cc.reshape(-1), na.reshape(-1), nm_.reshape(-1), ba.reshape(-1), pad(sk.reshape(-1)), pad(ps.reshape(-1)),
                fn.reshape(-1), jnp.zeros((4 * d,), F32)]
        return jnp.concatenate(flat).reshape(16, d)

    w_s = pack(c_ctx, norm_attn_w, norm_mlp_w, b_ada, attn_sink, pool_scale, final_norm_w)
    m_s = pack(m_c_ctx, m_norm_attn_w, m_norm_mlp_w, m_b_ada, m_attn_sink, m_pool_scale, m_final_norm_w)
    v_s = pack(v_c_ctx, v_norm_attn_w, v_norm_mlp_w, v_b_ada, v_attn_sink, v_pool_scale, v_final_norm_w)
    g_s = pack(g_cctx, small[8], small[9], g_bada, small[12][:N_Q_HEADS], small[11][:POOL_WIDTH], small[10])
    small_out = _adamw("adamw_small", w_s, g_s, m_s, v_s)

    def unpack(p):
        return {"c_ctx": p[0], "norm_attn_w": p[1:2], "norm_mlp_w": p[2:3], "b_ada": p[3:9].reshape(1, 6 * d),
                "attn_sink": p[9:10, :N_Q_HEADS], "pool_scale": p[10:11, :POOL_WIDTH], "final_norm_w": p[11]}

    small_res = [unpack(p) for p in small_out]
    order = ["c_ctx", "norm_attn_w", "norm_mlp_w", "w_ada", "b_ada", "w_in", "attn_sink", "pool_w", "pool_scale",
             "w_out", "w_mlp_up", "w_mlp_down", "final_norm_w"]
    outs = [loss, grad_x.reshape(x.shape)]
    for kind in range(4):
        for nm in order:
            outs.append(res[nm][kind] if nm in res else small_res[kind][nm])
    return tuple(outs)
```

```python
import functools

import jax
import jax.numpy as jnp
from jax import lax
from jax.experimental import pallas as pl
from jax.experimental.pallas import tpu as pltpu

F32 = jnp.float32
BF16 = jnp.bfloat16
EPS = 1e-6
NEG_INF = -1e30
HEAD_DIM = 64
N_Q_HEADS = 16
N_KV_HEADS = 4
GROUP = N_Q_HEADS // N_KV_HEADS
ATTN_WIDTH = N_Q_HEADS * HEAD_DIM
KV_WIDTH = N_KV_HEADS * HEAD_DIM
POOL_WINDOWS = (2, 4, 8, 16)
POOL_GROUP_DIM = 256
POOL_WIDTH = len(POOL_WINDOWS) * POOL_GROUP_DIM
BLOCK = 128
GRID_W = 64
ROPE_BASE = 10000.0
SCALE = HEAD_DIM ** -0.5
HALO = 16
STRIP = 16
ADAM_LR, ADAM_B1, ADAM_B2, ADAM_EPS, ADAM_WD, ADAM_STEP = 0.001, 0.9, 0.999, 1e-08, 0.01, 10
MESH = pl.DeviceIdType.MESH
MIB = 1024 * 1024
ANY = pl.BlockSpec(memory_space=pl.ANY)


def _cp(n_axes, vmem_mib=48):
    return pltpu.CompilerParams(dimension_semantics=("arbitrary",) * n_axes, vmem_limit_bytes=vmem_mib * MIB)


def _fold8(v):
    s = v[0:8]
    for t in range(1, v.shape[0] // 8):
        s = s + v[8 * t:8 * t + 8]
    return s


def _dot(a, b):
    return jnp.dot(a, b, preferred_element_type=F32)


def _dot_nt(a, b):
    return lax.dot_general(a, b, (((1,), (1,)), ((), ())), preferred_element_type=F32)


def _dot_tn(a, b):
    return lax.dot_general(a, b, (((0,), (0,)), ((), ())), preferred_element_type=F32)


def _pick(n, *cands):
    for t in cands:
        if n % t == 0:
            return t
    return n


def _flip(pos, mask):
    return tuple((1 - v) if (mask >> (2 - i)) & 1 else v for i, v in enumerate(pos))


def _exchange(name, ins, out_shapes, remote, local=(), aliases=None):
    n_io = len(ins) + len(out_shapes)

    def body(*refs):
        io = refs[:n_io]
        send_sems, recv_sems, local_sems = refs[n_io:]
        me = (lax.axis_index("x"), lax.axis_index("y"), lax.axis_index("c"))

        def copy(i, sender):
            mask, src_fn, dst_fn = remote[i]
            return pltpu.make_async_remote_copy(
                src_ref=src_fn(io, sender), dst_ref=dst_fn(io, sender), send_sem=send_sems.at[i],
                recv_sem=recv_sems.at[i], device_id=_flip(sender, mask), device_id_type=MESH)

        own = [pltpu.make_async_copy(s(io, me), d(io, me), local_sems.at[i]) for i, (s, d) in enumerate(local)]
        for cp in own:
            cp.start()
        sends = [copy(i, me) for i in range(len(remote))]
        for cp in sends:
            cp.start()
        for i in range(len(remote)):
            copy(i, _flip(me, remote[i][0])).wait_recv()
        for cp in sends:
            cp.wait_send()
        for cp in own:
            cp.wait()

    return pl.pallas_call(
        body, name=name, out_shape=tuple(out_shapes),
        in_specs=[ANY] * len(ins), out_specs=tuple([ANY] * len(out_shapes)),
        scratch_shapes=[pltpu.SemaphoreType.DMA((len(remote),)), pltpu.SemaphoreType.DMA((len(remote),)),
                        pltpu.SemaphoreType.DMA((max(len(local), 1),))],
        input_output_aliases=aliases or {},
    )(*ins)


HBM = pl.BlockSpec(memory_space=pltpu.HBM)
SEM = pl.BlockSpec(memory_space=pltpu.SEMAPHORE)
EFFECT = pltpu.SideEffectType.DATAFLOW_SIDE_EFFECTING


def _split_copy(remote, i, io, send_sems, recv_sems, sender):
    mask, src_fn, dst_fn = remote[i]
    return pltpu.make_async_remote_copy(
        src_ref=src_fn(io, sender), dst_ref=dst_fn(io, sender), send_sem=send_sems.at[i],
        recv_sem=recv_sems.at[i], device_id=_flip(sender, mask), device_id_type=MESH)


def _exchange_start(name, bufs, remote, after=None):
    n, r = len(bufs), len(remote)
    more = [] if after is None else [after]

    def body(*refs):
        io, (send_sems, recv_sems, token) = refs[:n], refs[-3:]
        me = (lax.axis_index("x"), lax.axis_index("y"), lax.axis_index("c"))
        for i in range(r):
            _split_copy(remote, i, io, send_sems, recv_sems, me).start()
        token[...] = jnp.zeros_like(token)

    res = pl.pallas_call(
        body, name=name,
        out_shape=tuple(pltpu.HBM(b.shape, b.dtype) for b in bufs)
        + (pltpu.SemaphoreType.DMA((r,)), pltpu.SemaphoreType.DMA((r,)), jax.ShapeDtypeStruct((8, 128), F32)),
        in_specs=[HBM] * n + [ANY] * len(more), out_specs=tuple([HBM] * n) + (SEM, SEM, pl.BlockSpec(memory_space=pltpu.VMEM)),
        input_output_aliases={i: i for i in range(n)}, compiler_params=pltpu.CompilerParams(has_side_effects=EFFECT),
    )(*[pltpu.with_memory_space_constraint(b, pltpu.HBM) for b in bufs], *more)
    return list(res[:n]), res[n], res[n + 1], res[n + 2]


def _exchange_wait(name, bufs, send_sems, recv_sems, remote, after):
    n, r = len(bufs), len(remote)

    def body(*refs):
        io, ss, rs = refs[:n], refs[n], refs[n + 1]
        me = (lax.axis_index("x"), lax.axis_index("y"), lax.axis_index("c"))
        for i in range(r):
            _split_copy(remote, i, io, ss, rs, _flip(me, remote[i][0])).wait_recv()
        for i in range(r):
            _split_copy(remote, i, io, ss, rs, me).wait_send()

    return list(pl.pallas_call(
        body, name=name, out_shape=tuple(pltpu.HBM(b.shape, b.dtype) for b in bufs),
        in_specs=[HBM] * n + [SEM, SEM, ANY], out_specs=tuple([HBM] * n),
        input_output_aliases={i: i for i in range(n)}, compiler_params=pltpu.CompilerParams(has_side_effects=EFFECT),
    )(*bufs, send_sems, recv_sems, after))


def _my_c():
    return lax.axis_index("c")


def _my_chip():
    return 2 * lax.axis_index("x") + lax.axis_index("y")


def _dev_index(pos):
    return 4 * pos[0] + 2 * pos[1] + pos[2]


def _chip_index(pos):
    return 2 * pos[0] + pos[1]


def _allgather8(name, v):
    out = jax.ShapeDtypeStruct((8,) + v.shape, v.dtype)
    remote = [(mask, lambda io, pos: io[0], lambda io, pos: io[1].at[_dev_index(pos)]) for mask in range(1, 8)]
    local = [(lambda io, pos: io[0], lambda io, pos: io[1].at[_dev_index(pos)])]
    return _exchange(name, [v], [out], remote, local)[0]


class _Big:
    def __init__(self, kind, shard_shape):
        self.kind = kind
        self.shard_shape = tuple(shard_shape)
        if kind == "col":
            r, cs = shard_shape
            self.full_shape = (r, 4 * cs)
            self.piece_shape = (r // 2, cs)
            self.half_shape = (r // 2, 4 * cs)
        elif kind == "row":
            rs, c = shard_shape
            self.full_shape = (4, 2, rs // 2, c)
            self.piece_shape = (1, 1, rs // 2, c)
            self.half_shape = (4, 1, rs // 2, c)
        else:
            self.full_shape = (4, 256, 256)
            self.piece_shape = (2, 64, 256)
            self.half_shape = (2, 256, 256)

    def shard_as_pieces(self, a):
        return a.reshape((1, 2) + self.piece_shape[2:]) if self.kind == "row" else a

    def piece(self, ref, k, h):
        if self.kind == "col":
            r, cs = self.piece_shape
            return ref.at[pl.ds(h * r, r), pl.ds(k * cs, cs)]
        if self.kind == "row":
            return ref.at[pl.ds(k, 1), pl.ds(h, 1)]
        return ref.at[pl.ds(2 * h, 2), pl.ds(64 * k, 64)]

    def half_of_shard(self, ref, h):
        if self.kind == "col":
            return ref.at[pl.ds(h * self.piece_shape[0], self.piece_shape[0])]
        if self.kind == "row":
            return ref.at[:, pl.ds(h, 1)]
        return ref.at[pl.ds(2 * h, 2)]

    def half_of_full(self, ref, h):
        if self.kind == "col":
            return ref.at[pl.ds(h * self.half_shape[0], self.half_shape[0])]
        if self.kind == "row":
            return ref.at[:, pl.ds(h, 1)]
        return ref.at[pl.ds(2 * h, 2)]

    def piece_of_half(self, ref, k):
        if self.kind == "col":
            return ref.at[:, pl.ds(k * self.piece_shape[1], self.piece_shape[1])]
        if self.kind == "row":
            return ref.at[pl.ds(k, 1)]
        return ref.at[:, pl.ds(64 * k, 64)]


CHIP_MASKS = (4, 2, 6)


def _cast_place(name, big, shard, after):
    if big.kind == "col":
        r, cs = big.shard_shape
        tr = _pick(r, 512, 256, 128)
        src, grid, blk = shard, (r // tr,), (tr, cs)
        imap, omap = (lambda i: (i, 0)), (lambda i: (i, _my_chip()))
    elif big.kind == "row":
        rs, c = big.shard_shape
        tr = _pick(rs // 2, 256, 128)
        src, grid, blk = big.shard_as_pieces(shard), (2, rs // 2 // tr), (1, 1, tr, c)
        imap, omap = (lambda h, i: (0, h, i, 0)), (lambda h, i: (_my_chip(), h, i, 0))
    else:
        src, grid, blk = shard, (1,), big.shard_shape
        imap, omap = (lambda i: (0, 0, 0)), (lambda i: (0, _my_chip(), 0))

    def body(s_ref, after_ref, o_ref, token_ref):
        o_ref[...] = s_ref[...].astype(BF16)
        token_ref[...] = jnp.zeros_like(token_ref)

    return pl.pallas_call(
        body, name=name, grid=grid, in_specs=[pl.BlockSpec(blk, imap), ANY],
        out_specs=(pl.BlockSpec(blk, omap), pl.BlockSpec((8, 128), lambda *_: (0, 0))),
        out_shape=(jax.ShapeDtypeStruct(big.full_shape, BF16), jax.ShapeDtypeStruct((8, 128), F32)), compiler_params=_cp(len(grid)),
    )(src, after)


def _gather_ici_remote(bigs, off):
    remote = []
    for a, b in enumerate(bigs):
        for mask in CHIP_MASKS:
            def mine(io, p, a=a, b=b):
                return b.piece(io[off + a], _chip_index(p), p[2])
            remote.append((mask, mine, mine))
    return remote


def _gather_d2d_remote(bigs, off):
    remote = []
    for a, b in enumerate(bigs):
        for mask in CHIP_MASKS:
            def region(io, p, a=a, b=b, mask=mask):
                return b.piece(io[off + a], _chip_index(_flip(p, mask)), p[2])
            remote.append((1, region, region))
    return remote


def _ew(name, fn, ins, out_dtypes, after, rows_per_step=256):
    shape = ins[0].shape
    last = shape[-1]
    rows = 1
    for s in shape[:-1]:
        rows *= s
    ins2 = [a.reshape(rows, last) for a in ins]
    tr = _pick(rows, rows_per_step, 128, 64, 32, 16, 8)
    spec = pl.BlockSpec((tr, last), lambda i: (i, 0))

    def body(*refs):
        outs = fn(*[r[...] for r in refs[:len(ins)]])
        for o_ref, o in zip(refs[len(ins) + 1:], outs):
            o_ref[...] = o.astype(o_ref.dtype)

    outs = pl.pallas_call(
        body, name=name, grid=(rows // tr,), in_specs=[spec] * len(ins) + [ANY], out_specs=tuple([spec] * len(out_dtypes)),
        out_shape=tuple(jax.ShapeDtypeStruct((rows, last), d) for d in out_dtypes), compiler_params=_cp(1),
    )(*ins2, after)
    return [o.reshape(shape) for o in outs]


def _chip_sum(name, big, grad, from_sibling):
    if big.kind == "col":
        rh, w = big.half_shape
        tr = _pick(rh, 256, 128)
        nb = rh // tr
        grid, blk = (nb,), (tr, w)
        gmap, hmap = (lambda i: (_my_c() * nb + i, 0)), (lambda i: (i, 0))
    elif big.kind == "row":
        rh, w = big.half_shape[2:]
        tr = _pick(rh, 256, 128)
        grid, blk = (4, rh // tr), (1, 1, tr, w)
        gmap, hmap = (lambda k, i: (k, _my_c(), i, 0)), (lambda k, i: (k, 0, i, 0))
    else:
        grid, blk = (1,), big.half_shape
        gmap, hmap = (lambda i: (_my_c(), 0, 0)), (lambda i: (0, 0, 0))

    def body(g_ref, s_ref, o_ref):
        o_ref[...] = (g_ref[...].astype(F32) + s_ref[...].astype(F32)).astype(BF16)

    return pl.pallas_call(
        body, name=name, grid=grid, in_specs=[pl.BlockSpec(blk, gmap), pl.BlockSpec(blk, hmap)],
        out_specs=pl.BlockSpec(blk, hmap), out_shape=jax.ShapeDtypeStruct(big.half_shape, BF16), compiler_params=_cp(len(grid)),
    )(grad, from_sibling)


def _piece_sum(name, big, chip_sum, thirds):
    if big.kind == "col":
        rp, cs = big.piece_shape
        tr = _pick(rp, 256, 128)
        nb = rp // tr
        grid, blk, tblk = (nb,), (tr, cs), (1, tr, cs)
        smap, omap = (lambda i: (i, _my_chip())), (lambda i: (_my_c() * nb + i, 0))
        tmap = lambda j: (lambda i: (j, i, 0))
        out_shape = big.shard_shape
    elif big.kind == "row":
        rp, w = big.piece_shape[2:]
        tr = _pick(rp, 256, 128)
        grid, blk, tblk = (rp // tr,), (1, 1, tr, w), (1, 1, 1, tr, w)
        smap, omap = (lambda i: (_my_chip(), 0, i, 0)), (lambda i: (0, _my_c(), i, 0))
        tmap = lambda j: (lambda i: (j, 0, 0, i, 0))
        out_shape = (1, 2, rp, w)
    else:
        grid, blk, tblk = (1,), big.piece_shape, (1,) + big.piece_shape
        smap, omap = (lambda i: (0, _my_chip(), 0)), (lambda i: (_my_c(), 0, 0))
        tmap = lambda j: (lambda i: (j, 0, 0, 0))
        out_shape = big.shard_shape

    def body(s_ref, t0, t1, t2, o_ref):
        o_ref[...] = s_ref[...].astype(F32) + t0[0].astype(F32) + t1[0].astype(F32) + t2[0].astype(F32)

    return pl.pallas_call(
        body, name=name, grid=grid,
        in_specs=[pl.BlockSpec(blk, smap)] + [pl.BlockSpec(tblk, tmap(j)) for j in range(3)],
        out_specs=pl.BlockSpec(blk, omap), out_shape=jax.ShapeDtypeStruct(out_shape, F32), compiler_params=_cp(len(grid)),
    )(chip_sum, thirds, thirds, thirds)


def _split(name, bufs, remote, after=None):
    return _exchange_start(name + "_start", bufs, remote, after) + (remote, name)


def _join(handle, after):
    bufs, send_sems, recv_sems, _, remote, name = handle
    return _exchange_wait(name + "_wait", bufs, send_sems, recv_sems, remote, after)


def _allgather8_split(name, v, me):
    own = lax.dynamic_update_slice(lax.empty((8,) + v.shape, v.dtype), v[None], (me, 0, 0))
    remote = [(mask, lambda io, pos: io[0], lambda io, pos: io[1].at[_dev_index(pos)]) for mask in range(1, 8)]
    return _split(name, [v, own], remote)


def _reduce_d2d_remote(bigs):
    n = len(bigs)
    return [(1, lambda io, p, a=a, b=b: b.half_of_full(io[a], 1 - p[2]), lambda io, p, a=a: io[n + a])
            for a, b in enumerate(bigs)]


def _halves(bigs):
    return [jax.ShapeDtypeStruct(b.half_shape, BF16) for b in bigs]


def _chip_sums(tag, bigs, grads, from_sibling):
    return [_chip_sum(f"reduce_{tag}_chip_sum_{a}", b, g, r) for a, (b, g, r) in enumerate(zip(bigs, grads, from_sibling))]


def _reduce_to_chip(tag, bigs, grads):
    from_sibling = _exchange(f"reduce_{tag}_d2d", grads, _halves(bigs), _reduce_d2d_remote(bigs))
    return _chip_sums(tag, bigs, grads, from_sibling)


def _reduce_ici_remote(bigs):
    n = len(bigs)
    remote = []
    for a, b in enumerate(bigs):
        for j, mask in enumerate(CHIP_MASKS):
            remote.append((mask,
                           lambda io, p, a=a, b=b, mask=mask: b.piece_of_half(io[a], _chip_index(_flip(p, mask))),
                           lambda io, p, a=a, j=j: io[n + a].at[j]))
    return remote


def _thirds(bigs):
    return [jax.ShapeDtypeStruct((3,) + b.piece_shape, BF16) for b in bigs]


def _piece_sums(tag, bigs, chip_sum, from_chips):
    return [_piece_sum(f"reduce_{tag}_sum_{a}", b, s, r) for a, (b, s, r) in enumerate(zip(bigs, chip_sum, from_chips))]


def _share_remote(bigs, off):
    remote = []
    for a, b in enumerate(bigs):
        def mine(io, p, a=a, b=b):
            return b.half_of_shard(io[off + a], p[2])
        remote.append((1, mine, mine))
    return remote


def _reduce_finish(tag, bigs, chip_sum, from_chips):
    n = len(bigs)
    placed = _piece_sums(tag, bigs, chip_sum, from_chips)
    out = _exchange(f"reduce_{tag}_share_d2d", placed, [jax.ShapeDtypeStruct(p.shape, F32) for p in placed],
                    _share_remote(bigs, n), aliases={a: a for a in range(n)})
    return [o.reshape(b.shard_shape) for o, b in zip(out, bigs)]


def _mm(name, a, b, *, nt, tm, tn, tk, epi, extras=(), extra_specs=(), out_shape, out_specs, after=None, vmem_mib=48):
    m, kdim = a.shape
    n = b.shape[0] if nt else b.shape[1]
    gm, gn, gk = m // tm, n // tn, kdim // tk
    a_spec = pl.BlockSpec((tm, tk), lambda j, i, k: (i, k))
    b_spec = pl.BlockSpec((tn, tk), lambda j, i, k: (j, k)) if nt else pl.BlockSpec((tk, tn), lambda j, i, k: (k, j))
    n_ex = len(extras)
    if after is not None:
        extras, extra_specs = tuple(extras) + (after,), list(extra_specs) + [ANY]

    def body(a_ref, b_ref, *rest):
        ex, outs, acc = rest[:n_ex], rest[len(extras):-1], rest[-1]
        dot = _dot_nt if nt else _dot
        if gk == 1:
            acc[...] = dot(a_ref[...], b_ref[...])
            epi(acc, ex, outs)
        else:
            k = pl.program_id(2)

            @pl.when(k == 0)
            def _():
                acc[...] = dot(a_ref[...], b_ref[...])

            @pl.when(k > 0)
            def _():
                acc[...] += dot(a_ref[...], b_ref[...])

            @pl.when(k == gk - 1)
            def _():
                epi(acc, ex, outs)

    return pl.pallas_call(
        body, name=name, grid=(gn, gm, gk), in_specs=[a_spec, b_spec, *extra_specs], out_specs=tuple(out_specs),
        out_shape=tuple(out_shape), scratch_shapes=[pltpu.VMEM((tm, tn), F32)], compiler_params=_cp(3, vmem_mib),
    )(a, b, *extras)


def _mm_deferred(name, a, b, *, nt, tm, epi, tiles, vecs, out_tiles, n_stats, after, vmem_mib=48):
    m, kdim = a.shape
    n = b.shape[0] if nt else b.shape[1]
    gm = m // tm
    n_t, n_v, n_o = len(tiles), len(vecs), len(out_tiles)
    dot = _dot_nt if nt else _dot

    def body(a_ref, b_ref, *rest):
        t_refs, v_refs = rest[:n_t], rest[n_t:n_t + n_v]
        o_refs, st_ref, acc0, acc1 = rest[n_t + n_v + 1:n_t + n_v + 1 + n_o], rest[-3], rest[-2], rest[-1]
        i = pl.program_id(0)

        @pl.when(i == 0)
        def _():
            acc1[...] = jnp.zeros_like(acc1)
            st_ref[...] = jnp.zeros_like(st_ref)

        def finish(prev):
            for r0 in range(0, tm, STRIP):
                rs = slice(r0, r0 + STRIP)
                epi(prev[rs, :], rs, t_refs, v_refs, o_refs, st_ref, i > 0)

        @pl.when((i % 2 == 0) & (i < gm))
        def _():
            acc0[...] = dot(a_ref[...], b_ref[...])
            finish(acc1)

        @pl.when((i % 2 == 1) & (i < gm))
        def _():
            acc1[...] = dot(a_ref[...], b_ref[...])
            finish(acc0)

        @pl.when(i == gm)
        def _():
            finish(acc1 if gm % 2 == 0 else acc0)

    prev = lambda i: (jnp.maximum(i - 1, 0), 0)
    tile = pl.BlockSpec((tm, n), prev)
    return pl.pallas_call(
        body, name=name, grid=(gm + 1,),
        in_specs=[pl.BlockSpec((tm, kdim), lambda i: (jnp.minimum(i, gm - 1), 0)), pl.BlockSpec(b.shape, lambda i: (0, 0))]
        + [tile] * n_t + [_row_spec(n)] * n_v + [ANY],
        out_specs=tuple([tile] * n_o) + (_stat_spec(n_stats, n),),
        out_shape=tuple(out_tiles) + (jax.ShapeDtypeStruct((n_stats, 8, n), F32),),
        scratch_shapes=[pltpu.VMEM((tm, n), F32), pltpu.VMEM((tm, n), F32)], compiler_params=_cp(1, vmem_mib),
    )(a, b, *tiles, *vecs, after)


def _mm_k_deferred(name, a, b, *, nt, tm, tk, epi, tiles, vecs, out_tiles, n_stats, after, vmem_mib=56):
    m, kdim = a.shape
    n = b.shape[0] if nt else b.shape[1]
    gm, gk = m // tm, kdim // tk
    rows = tm // gk
    n_t, n_v, n_o = len(tiles), len(vecs), len(out_tiles)
    dot = _dot_nt if nt else _dot

    def body(a_ref, b_ref, *rest):
        t_refs, v_refs = rest[:n_t], rest[n_t:n_t + n_v]
        o_refs, st_ref, acc0, acc1 = rest[n_t + n_v + 1:n_t + n_v + 1 + n_o], rest[-3], rest[-2], rest[-1]
        i, k = pl.program_id(0), pl.program_id(1)

        @pl.when((i == 0) & (k == 0))
        def _():
            acc1[...] = jnp.zeros_like(acc1)
            st_ref[...] = jnp.zeros_like(st_ref)

        def finish(prev):
            for r0 in range(0, rows, STRIP):
                acc_rows = prev[pl.ds(pl.multiple_of(k * rows + r0, STRIP), STRIP), :]
                epi(acc_rows, slice(r0, r0 + STRIP), t_refs, v_refs, o_refs, st_ref, i > 0)

        def step(cur, prev):
            cur[...] = jnp.where(k > 0, cur[...], 0.0) + dot(a_ref[...], b_ref[...])
            finish(prev)

        @pl.when((i % 2 == 0) & (i < gm))
        def _():
            step(acc0, acc1)

        @pl.when((i % 2 == 1) & (i < gm))
        def _():
            step(acc1, acc0)

        @pl.when(i == gm)
        def _():
            finish(acc1 if gm % 2 == 0 else acc0)

    prev = lambda i, k: (jnp.where(i == 0, 0, (i - 1) * gk + k), 0)
    part = pl.BlockSpec((rows, n), prev)
    b_spec = pl.BlockSpec((n, tk), lambda i, k: (0, k)) if nt else pl.BlockSpec((tk, n), lambda i, k: (k, 0))
    return pl.pallas_call(
        body, name=name, grid=(gm + 1, gk),
        in_specs=[pl.BlockSpec((tm, tk), lambda i, k: (jnp.minimum(i, gm - 1), k)), b_spec]
        + [part] * n_t + [_row_spec(n)] * n_v + [ANY],
        out_specs=tuple([part] * n_o) + (_stat_spec(n_stats, n),),
        out_shape=tuple(out_tiles) + (jax.ShapeDtypeStruct((n_stats, 8, n), F32),),
        scratch_shapes=[pltpu.VMEM((tm, n), F32), pltpu.VMEM((tm, n), F32)], compiler_params=_cp(2, vmem_mib),
    )(a, b, *tiles, *vecs, after)


def _mm_tn(name, a, b, out_dtype, *, tmo, tn, tt, more=(), vmem_mib=56):
    t, m = a.shape
    n = b.shape[1]
    gt = t // tt

    def body(a_ref, b_ref, *rest):
        o_ref, acc = rest[-2:]
        k = pl.program_id(2)

        @pl.when(k == 0)
        def _():
            first = _dot_tn(a_ref[...], b_ref[...])
            acc[...] = first + _dot_tn(rest[0][...], rest[1][...]) if more else first

        @pl.when(k > 0)
        def _():
            acc[...] += _dot_tn(a_ref[...], b_ref[...])

        @pl.when(k == gt - 1)
        def _():
            o_ref[...] = acc[...].astype(o_ref.dtype)

    more_specs = [pl.BlockSpec((more[0].shape[0], tmo), lambda i, j, k: (0, i)),
                  pl.BlockSpec((more[1].shape[0], tn), lambda i, j, k: (0, j))] if more else []
    return pl.pallas_call(
        body, name=name, grid=(m // tmo, n // tn, gt),
        in_specs=[pl.BlockSpec((tt, tmo), lambda i, j, k: (k, i)), pl.BlockSpec((tt, tn), lambda i, j, k: (k, j))] + more_specs,
        out_specs=pl.BlockSpec((tmo, tn), lambda i, j, k: (i, j)), out_shape=jax.ShapeDtypeStruct((m, n), out_dtype),
        scratch_shapes=[pltpu.VMEM((tmo, tn), F32)], compiler_params=_cp(3, vmem_mib),
    )(a, b, *more)


def _row_spec(d):
    return pl.BlockSpec((1, d), lambda *_: (0, 0))


def _stat_spec(k, d):
    return pl.BlockSpec((k, 8, d), lambda *_: (0, 0, 0))


def _rope(z, cs, sn):
    first = (lax.broadcasted_iota(jnp.int32, (z.shape[0], 128), 1) % 32) < 16
    outs = []
    for j in range(z.shape[1] // 128):
        zc = z[:, 128 * j:128 * (j + 1)]
        partner = jnp.where(first, pltpu.roll(zc, 112, 1), pltpu.roll(zc, 16, 1))
        outs.append(zc * cs + partner * sn)
    return outs[0] if len(outs) == 1 else jnp.concatenate(outs, axis=1)


def _rope_tables(length, rotate, zero=0.0):
    if not rotate:
        return jnp.ones((length, 128), F32), jnp.zeros((length, 128), F32)
    half = HEAD_DIM // 2
    t = jnp.arange(length)
    row = (t // GRID_W).astype(F32) + zero
    col = (t % GRID_W).astype(F32)
    e = jnp.arange(128) % HEAD_DIM
    inv_freq = ROPE_BASE ** (-(2 * ((e % half) % (half // 2))).astype(F32) / half)
    pos = jnp.where(e[None, :] < half, row[:, None], col[:, None])
    ang = pos * inv_freq[None, :]
    first = ((e % half) < half // 2)[None, :]
    return jnp.cos(ang), jnp.where(first, -jnp.sin(ang), jnp.sin(ang))


def _mixer_in(name, x, nw, sh, sc, w_in, cos, sin, after):
    t, d = x.shape
    tm = _pick(t, 256, 128)
    n_in = w_in.shape[1]

    def body(x_ref, nw_ref, sh_ref, sc_ref, w_ref, cos_ref, sin_ref, after_ref, h_ref, q_ref, k_ref, v_ref, u_ref):
        xf = x_ref[...]
        r = lax.rsqrt(jnp.mean(xf * xf, axis=-1, keepdims=True) + EPS)
        hb = (((xf * r) * nw_ref[...]) * (1.0 + sc_ref[...]) + sh_ref[...]).astype(BF16)
        h_ref[...] = hb
        p = _dot(hb, w_ref[...])
        cs, sn = cos_ref[...], sin_ref[...]
        q_ref[...] = (_rope(p[:, :ATTN_WIDTH], cs, sn) * SCALE).astype(BF16)
        k_ref[...] = _rope(p[:, ATTN_WIDTH:ATTN_WIDTH + KV_WIDTH], cs, sn).astype(BF16)
        v_ref[...] = p[:, ATTN_WIDTH + KV_WIDTH:ATTN_WIDTH + 2 * KV_WIDTH].astype(BF16)
        u_ref[...] = p[:, ATTN_WIDTH + 2 * KV_WIDTH:]

    def tile(w):
        return pl.BlockSpec((tm, w), lambda i: (i, 0))

    return pl.pallas_call(
        body, name=name, grid=(t // tm,),
        in_specs=[tile(d), _row_spec(d), _row_spec(d), _row_spec(d), pl.BlockSpec((d, n_in), lambda i: (0, 0)),
                  tile(128), tile(128), ANY],
        out_specs=(tile(d), tile(ATTN_WIDTH), tile(KV_WIDTH), tile(KV_WIDTH), tile(POOL_WIDTH)),
        out_shape=(jax.ShapeDtypeStruct((t, d), BF16), jax.ShapeDtypeStruct((t, ATTN_WIDTH), BF16),
                   jax.ShapeDtypeStruct((t, KV_WIDTH), BF16), jax.ShapeDtypeStruct((t, KV_WIDTH), BF16),
                   jax.ShapeDtypeStruct((t, POOL_WIDTH), F32)),
        compiler_params=_cp(1),
    )(x, nw, sh, sc, w_in, cos, sin, after)


def _attn_specs(nb, n_ctx):
    def blk(w, f):
        return pl.BlockSpec((BLOCK, w), lambda n: (f(n), 0))

    prev = lambda n: jnp.maximum(n - 1, 0)
    cur = lambda n: n
    nxt = lambda n: jnp.minimum(n + 1, nb - 1)
    kv = [blk(KV_WIDTH, prev), blk(KV_WIDTH, cur), blk(KV_WIDTH, nxt)]
    ctx = pl.BlockSpec((n_ctx, KV_WIDTH), lambda n: (0, 0))
    return [pl.BlockSpec(memory_space=pltpu.SMEM), blk(ATTN_WIDTH, cur)] + kv + kv + [ctx, ctx]


def _attn_mask(n, length, n_keys):
    row = lax.broadcasted_iota(jnp.int32, (GROUP * BLOCK, n_keys), 0) % BLOCK
    col = lax.broadcasted_iota(jnp.int32, (GROUP * BLOCK, n_keys), 1)
    kpos = (n - 1) * BLOCK + col
    return ((jnp.abs(col - BLOCK - row) <= BLOCK) & (kpos >= 0) & (kpos < length)) | (col >= 3 * BLOCK)


def _group_rows(block, g):
    return jnp.concatenate([block[:, HEAD_DIM * h:HEAD_DIM * (h + 1)] for h in range(GROUP * g, GROUP * (g + 1))], axis=0)


def _group_sink(sink_ref, g):
    head = lax.broadcasted_iota(jnp.int32, (GROUP * BLOCK, 1), 0) // BLOCK
    out = jnp.full((GROUP * BLOCK, 1), sink_ref[0, GROUP * g], F32)
    for j in range(1, GROUP):
        out = jnp.where(head == j, sink_ref[0, GROUP * g + j], out)
    return out


def _attn_fwd(q, k, v, kc, vc, sink):
    length = q.shape[0]
    nb = length // BLOCK
    n_ctx = kc.shape[0]
    n_keys = 3 * BLOCK + n_ctx

    def body(sink_ref, q_ref, kp, k0, kn, vp, v0, vn, kc_ref, vc_ref, o_ref, p_ref):
        n = pl.program_id(0)
        valid = _attn_mask(n, length, n_keys)
        qb = q_ref[...]
        kall = jnp.concatenate([kp[...], k0[...], kn[...], kc_ref[...]], axis=0)
        vall = jnp.concatenate([vp[...], v0[...], vn[...], vc_ref[...]], axis=0)
        outs = []
        for g in range(N_KV_HEADS):
            lanes = slice(HEAD_DIM * g, HEAD_DIM * (g + 1))
            s = jnp.where(valid, _dot_nt(_group_rows(qb, g), kall[:, lanes]), NEG_INF)
            sk = _group_sink(sink_ref, g)
            m = jnp.maximum(jnp.max(s, axis=-1, keepdims=True), sk)
            e = jnp.exp(s - m)
            e_sink = jnp.exp(sk - m)
            inv = 1.0 / (jnp.sum(e, axis=-1, keepdims=True) + e_sink)
            pb = (e * inv).astype(BF16)
            p_ref[0, g, :, :n_keys] = pb
            p_ref[0, g, :, n_keys:] = jnp.broadcast_to(e_sink * inv, (GROUP * BLOCK, 128)).astype(BF16)
            o = _dot(pb, vall[:, lanes])
            outs += [o[BLOCK * j:BLOCK * (j + 1)] for j in range(GROUP)]
        o_ref[...] = jnp.concatenate(outs, axis=1).astype(BF16)

    return pl.pallas_call(
        body, name="attn_fwd", grid=(nb,), in_specs=_attn_specs(nb, n_ctx),
        out_specs=(pl.BlockSpec((BLOCK, ATTN_WIDTH), lambda n: (n, 0)),
                   pl.BlockSpec((1, N_KV_HEADS, GROUP * BLOCK, n_keys + 128), lambda n: (n, 0, 0, 0))),
        out_shape=(jax.ShapeDtypeStruct((length, ATTN_WIDTH + POOL_WIDTH), BF16),
                   jax.ShapeDtypeStruct((nb, N_KV_HEADS, GROUP * BLOCK, n_keys + 128), BF16)), compiler_params=_cp(1),
    )(sink, q, k, k, k, v, v, v, kc, vc)


def _attn_bwd(q, k, v, kc, vc, dmix, probs):
    length = q.shape[0]
    nb = length // BLOCK
    n_ctx = kc.shape[0]
    n_keys = 3 * BLOCK + n_ctx

    def body(q_ref, kp, k0, kn, vp, v0, vn, kc_ref, vc_ref, do_ref, p_ref,
             dq_ref, dkp_ref, dvp_ref, dkc_ref, dvc_ref, dsink_ref):
        n = pl.program_id(0)

        @pl.when(n == 0)
        def _():
            dkc_ref[...] = jnp.zeros_like(dkc_ref)
            dvc_ref[...] = jnp.zeros_like(dvc_ref)
            dsink_ref[...] = jnp.zeros_like(dsink_ref)

        qb, dob = q_ref[...], do_ref[...]
        kall = jnp.concatenate([kp[...], k0[...], kn[...], kc_ref[...]], axis=0)
        vall = jnp.concatenate([vp[...], v0[...], vn[...], vc_ref[...]], axis=0)
        srow = lax.broadcasted_iota(jnp.int32, (8, 128), 0)
        slane = lax.broadcasted_iota(jnp.int32, (8, 128), 1)
        dqs, dks, dvs = [], [], []
        dsink = jnp.zeros((8, 128), F32)
        for g in range(N_KV_HEADS):
            lanes = slice(HEAD_DIM * g, HEAD_DIM * (g + 1))
            kg, vg = kall[:, lanes], vall[:, lanes]
            qg, dog = _group_rows(qb, g), _group_rows(dob, g)
            pb = p_ref[0, g, :, :n_keys]
            p = pb.astype(F32)
            dp = _dot_nt(dog, vg)
            delta = jnp.sum(p * dp, axis=-1, keepdims=True)
            ds = (p * (dp - delta)).astype(BF16)
            dq = _dot(ds, kg) * SCALE
            dqs += [dq[BLOCK * j:BLOCK * (j + 1)] for j in range(GROUP)]
            dks.append(_dot_tn(ds, qg))
            dvs.append(_dot_tn(pb, dog))
            d_sink = p_ref[0, g, :, n_keys:].astype(F32)[:, :1] * delta
            for j in range(GROUP):
                total = -jnp.sum(d_sink[BLOCK * j:BLOCK * (j + 1)], axis=0, keepdims=True)
                dsink = dsink + jnp.where((srow == 0) & (slane == GROUP * g + j), total, 0.0)
        dq_ref[...] = jnp.concatenate(dqs, axis=1)
        dk = jnp.concatenate(dks, axis=1)
        dv = jnp.concatenate(dvs, axis=1)
        for j in range(3):
            dkp_ref[0, j] = dk[BLOCK * j:BLOCK * (j + 1)]
            dvp_ref[0, j] = dv[BLOCK * j:BLOCK * (j + 1)]
        dkc_ref[...] += dk[3 * BLOCK:]
        dvc_ref[...] += dv[3 * BLOCK:]
        dsink_ref[...] += dsink

    part = pl.BlockSpec((1, 3, BLOCK, KV_WIDTH), lambda n: (n, 0, 0, 0))
    ctx = pl.BlockSpec((n_ctx, KV_WIDTH), lambda n: (0, 0))
    return pl.pallas_call(
        body, name="attn_bwd", grid=(nb,),
        in_specs=_attn_specs(nb, n_ctx)[1:] + [pl.BlockSpec((BLOCK, ATTN_WIDTH), lambda n: (n, 0)),
                                           pl.BlockSpec((1,) + probs.shape[1:], lambda n: (n, 0, 0, 0))],
        out_specs=(pl.BlockSpec((BLOCK, ATTN_WIDTH), lambda n: (n, 0)), part, part, ctx, ctx,
                   pl.BlockSpec((8, 128), lambda n: (0, 0))),
        out_shape=(jax.ShapeDtypeStruct((length, ATTN_WIDTH), F32),
                   jax.ShapeDtypeStruct((nb, 3, BLOCK, KV_WIDTH), F32), jax.ShapeDtypeStruct((nb, 3, BLOCK, KV_WIDTH), F32),
                   jax.ShapeDtypeStruct((n_ctx, KV_WIDTH), F32), jax.ShapeDtypeStruct((n_ctx, KV_WIDTH), F32),
                   jax.ShapeDtypeStruct((8, 128), F32)),
        compiler_params=_cp(1),
    )(q, k, k, k, v, v, v, kc, vc, dmix, probs)


def _assemble_dp(dq, dkp, dvp, du, cos, sin, after):
    length = dq.shape[0]
    nb = length // BLOCK

    def body(dq_ref, dka, dkb, dkc, dva, dvb, dvc, du_ref, cos_ref, sin_ref, after_ref, o_ref):
        n = pl.program_id(0)
        has_next = (n + 1 < nb).astype(F32)
        has_prev = (n > 0).astype(F32)
        cs, sn = cos_ref[...], -sin_ref[...]
        dk = dka[0, 0] * has_next + dkb[0, 0] + dkc[0, 0] * has_prev
        dv = dva[0, 0] * has_next + dvb[0, 0] + dvc[0, 0] * has_prev
        o_ref[:, :ATTN_WIDTH] = _rope(dq_ref[...], cs, sn).astype(BF16)
        o_ref[:, ATTN_WIDTH:ATTN_WIDTH + KV_WIDTH] = _rope(dk, cs, sn).astype(BF16)
        o_ref[:, ATTN_WIDTH + KV_WIDTH:ATTN_WIDTH + 2 * KV_WIDTH] = dv.astype(BF16)
        o_ref[:, ATTN_WIDTH + 2 * KV_WIDTH:] = du_ref[...]

    def part(slot, f):
        return pl.BlockSpec((1, 1, BLOCK, KV_WIDTH), lambda n: (f(n), slot, 0, 0))

    parts = [part(0, lambda n: jnp.minimum(n + 1, nb - 1)), part(1, lambda n: n), part(2, lambda n: jnp.maximum(n - 1, 0))]

    def tile(w):
        return pl.BlockSpec((BLOCK, w), lambda n: (n, 0))

    width = ATTN_WIDTH + 2 * KV_WIDTH + POOL_WIDTH
    return pl.pallas_call(
        body, name="assemble_dp", grid=(nb,),
        in_specs=[tile(ATTN_WIDTH)] + parts + parts + [tile(POOL_WIDTH), tile(128), tile(128), ANY],
        out_specs=tile(width), out_shape=jax.ShapeDtypeStruct((length, width), BF16), compiler_params=_cp(1),
    )(dq, dkp, dkp, dkp, dvp, dvp, dvp, du, cos, sin, after)


def _shift_rows(e, s):
    n = e.shape[0]
    return e if s % n == 0 else pltpu.roll(e, (-s) % n, 0)


def _window_sum(e, w, first):
    s, n = e, 1
    while n < w:
        s = s + _shift_rows(s, n)
        n *= 2
    return _shift_rows(s, first)


def _pool_geometry(i, tm, length):
    pos = i * tm - HALO + lax.broadcasted_iota(jnp.int32, (tm + 2 * HALO, 1), 0)
    inside = (pos >= 0) & (pos < length)
    inv_counts = []
    for w in POOL_WINDOWS:
        lo = jnp.clip(pos - w // 2, 0, length)
        hi = jnp.clip(pos - w // 2 + w, 0, length)
        inv_counts.append(1.0 / jnp.maximum(hi - lo, 1).astype(F32))
    return inside, inv_counts


def _halo_specs(tm, width, length, col=0):
    per = tm // HALO
    last = length // HALO - 1
    return [pl.BlockSpec((HALO, width), lambda i: (jnp.maximum(i * per - 1, 0), col)),
            pl.BlockSpec((tm, width), lambda i: (i, col)),
            pl.BlockSpec((HALO, width), lambda i: (jnp.minimum((i + 1) * per, last), col))]


def _pooled(ext, inv_counts, tm):
    outs = []
    for g, w in enumerate(POOL_WINDOWS):
        e = ext[:, POOL_GROUP_DIM * g:POOL_GROUP_DIM * (g + 1)]
        mean = _window_sum(e, w, -(w // 2)) * inv_counts[g]
        outs.append((mean - e)[HALO:HALO + tm])
    return outs


def _pool_fwd(u, pool_w, pool_scale, mix):
    length = u.shape[0]
    tm = _pick(length, 256, 128)

    def body(up, u0, un, w_ref, sc_ref, mix_ref, o_ref):
        inside, inv_counts = _pool_geometry(pl.program_id(0), tm, length)
        ext = jnp.where(inside, jnp.concatenate([up[...], u0[...], un[...]], axis=0), 0.0)
        pooled = _pooled(ext, inv_counts, tm)
        mixed = [_dot(pooled[g].astype(BF16), w_ref[g]) for g in range(len(POOL_WINDOWS))]
        o_ref[...] = (jnp.concatenate(mixed, axis=1) * sc_ref[...]).astype(BF16)

    return pl.pallas_call(
        body, name="pool_fwd", grid=(length // tm,),
        in_specs=_halo_specs(tm, POOL_WIDTH, length) + [pl.BlockSpec(pool_w.shape, lambda i: (0, 0, 0)), _row_spec(POOL_WIDTH), ANY],
        out_specs=pl.BlockSpec((tm, POOL_WIDTH), lambda i: (i, 1)),
        out_shape=jax.ShapeDtypeStruct(mix.shape, BF16), input_output_aliases={5: 0}, compiler_params=_cp(1),
    )(u, u, u, pool_w, pool_scale, mix)


def _pool_bwd(u, dmix, pool_w, pool_scale, after):
    length = u.shape[0]
    tm = _pick(length, 256, 128)
    n_g = len(POOL_WINDOWS)

    def body(up, u0, un, dp_, d0, dn_, w_ref, sc_ref, after_ref, du_ref, dw_ref, dsc_ref):
        i = pl.program_id(0)

        @pl.when(i == 0)
        def _():
            dw_ref[...] = jnp.zeros_like(dw_ref)
            dsc_ref[...] = jnp.zeros_like(dsc_ref)

        inside, inv_counts = _pool_geometry(i, tm, length)
        ext = jnp.where(inside, jnp.concatenate([up[...], u0[...], un[...]], axis=0), 0.0)
        dext = jnp.where(inside, jnp.concatenate([dp_[...], d0[...], dn_[...]], axis=0).astype(F32), 0.0)
        dmixed = (dext * sc_ref[...]).astype(BF16)
        pooled = _pooled(ext, inv_counts, tm)
        dus, dscs = [], []
        for g, w in enumerate(POOL_WINDOWS):
            lanes = slice(POOL_GROUP_DIM * g, POOL_GROUP_DIM * (g + 1))
            dpooled = _dot_nt(dmixed[:, lanes], w_ref[g])
            spread = _window_sum(dpooled * inv_counts[g], w, -(w // 2 - 1))
            dus.append((spread - dpooled)[HALO:HALO + tm])
            pb = pooled[g].astype(BF16)
            dw_ref[g] += _dot_tn(pb, dmixed[HALO:HALO + tm, lanes])
            prod = dext[HALO:HALO + tm, lanes] * _dot(pb, w_ref[g])
            dscs.append(_fold8(prod))
        du_ref[...] = jnp.concatenate(dus, axis=1).astype(BF16)
        dsc_ref[...] += jnp.concatenate(dscs, axis=1)

    return pl.pallas_call(
        body, name="pool_bwd", grid=(length // tm,),
        in_specs=_halo_specs(tm, POOL_WIDTH, length) + _halo_specs(tm, POOL_WIDTH, length, col=1)
        + [pl.BlockSpec(pool_w.shape, lambda i: (0, 0, 0)), _row_spec(POOL_WIDTH), ANY],
        out_specs=(pl.BlockSpec((tm, POOL_WIDTH), lambda i: (i, 0)), pl.BlockSpec((n_g, POOL_GROUP_DIM, POOL_GROUP_DIM), lambda i: (0, 0, 0)),
                   pl.BlockSpec((8, POOL_WIDTH), lambda i: (0, 0))),
        out_shape=(jax.ShapeDtypeStruct((length, POOL_WIDTH), BF16), jax.ShapeDtypeStruct((n_g, POOL_GROUP_DIM, POOL_GROUP_DIM), F32),
                   jax.ShapeDtypeStruct((8, POOL_WIDTH), F32)),
        compiler_params=_cp(1),
    )(u, u, u, dmix, dmix, dmix, pool_w, pool_scale, after)


def _mixer_out(mix, w_out, x, g_a, nmw, sh_m, sc_m, after):
    t, d = x.shape

    def epi(mo, rs, tiles, vecs, outs, st_ref, live):
        ga, nw, sh, sc = vecs
        x1_ref, mo_ref, hm_ref = outs
        x1 = tiles[0][rs, :] + ga[...] * mo
        x1_ref[rs, :] = x1
        mo_ref[rs, :] = mo.astype(BF16)
        r = lax.rsqrt(jnp.mean(x1 * x1, axis=-1, keepdims=True) + EPS)
        hm_ref[rs, :] = (((x1 * r) * nw[...]) * (1.0 + sc[...]) + sh[...]).astype(BF16)

    return _mm_deferred("mixer_out", mix, w_out, nt=False, tm=_pick(t, 256, 128), epi=epi, tiles=(x,), vecs=(g_a, nmw, sh_m, sc_m),
                        out_tiles=(jax.ShapeDtypeStruct((t, d), F32), jax.ShapeDtypeStruct((t, d), BF16), jax.ShapeDtypeStruct((t, d), BF16)),
                        n_stats=1, after=after)[:3]


def _mlp_up(hm, w_up):
    t, d = hm.shape
    tm = _pick(t, 1024, 512, 256, 128)
    tn = 2048

    def epi(acc, ex, outs):
        outs[0][...] = jnp.square(jnp.maximum(acc[...], 0.0)).astype(BF16)

    return _mm("mlp_up", hm, w_up, nt=False, tm=tm, tn=tn, tk=d, epi=epi,
               out_shape=(jax.ShapeDtypeStruct((t, w_up.shape[1]), BF16),),
               out_specs=(pl.BlockSpec((tm, tn), lambda j, i, k: (i, j)),))[0]


def _mlp_down_loss(act, w_down, x1, target, g_m, fw, after):
    t, d = x1.shape

    def epi(dnv, rs, tiles, vecs, outs, st_ref, live):
        x1_ref, t_ref = tiles
        gm, fw_ref = vecs
        dx2_ref, ddn_ref = outs
        x2 = x1_ref[rs, :] + gm[...] * dnv
        r = lax.rsqrt(jnp.mean(x2 * x2, axis=-1, keepdims=True) + EPS)
        xh = x2 * r
        diff = xh * fw_ref[...] - t_ref[rs, :]
        dy = diff * (1.0 / d)
        dxh = dy * fw_ref[...]
        dx2 = r * (dxh - xh * jnp.mean(dxh * xh, axis=-1, keepdims=True))
        dx2_ref[rs, :] = dx2
        ddn_ref[rs, :] = (dx2 * gm[...]).astype(BF16)
        st_ref[0] += jnp.where(live, _fold8(diff * diff), 0.0)
        st_ref[1] += jnp.where(live, _fold8(dy * xh), 0.0)
        st_ref[2] += jnp.where(live, _fold8(dx2 * dnv), 0.0)

    return _mm_k_deferred("mlp_down_loss", act, w_down, nt=False, tm=_pick(t, 512, 256), tk=_pick(act.shape[1], 2048), epi=epi,
                          tiles=(x1, target), vecs=(g_m, fw), n_stats=3, after=after,
                          out_tiles=(jax.ShapeDtypeStruct((t, d), F32), jax.ShapeDtypeStruct((t, d), BF16)))


def _mlp_dx(dup, w_up, x1, dx2, mo, nmw, sc_m, g_a, after):
    t, d = x1.shape

    def epi(dh, rs, tiles, vecs, outs, st_ref, live):
        x1_ref, dx2_ref, mo_ref = tiles
        nw, sc, ga = vecs
        dx1_ref, dmi_ref = outs
        dx1 = _norm_bwd_rows(dh, x1_ref[rs, :], nw[...], sc[...], st_ref) + dx2_ref[rs, :]
        dx1_ref[rs, :] = dx1
        dmi_ref[rs, :] = (dx1 * ga[...]).astype(BF16)
        st_ref[3] += jnp.where(live, _fold8(dx1 * mo_ref[rs, :].astype(F32)), 0.0)

    return _mm_k_deferred("mlp_dx", dup, w_up, nt=True, tm=_pick(t, 512, 256), tk=_pick(dup.shape[1], 2048), epi=epi,
                          tiles=(x1, dx2, mo), vecs=(nmw, sc_m, g_a), n_stats=4, after=after,
                          out_tiles=(jax.ShapeDtypeStruct((t, d), F32), jax.ShapeDtypeStruct((t, d), BF16)))


def _mlp_dact(ddn, w_down, act):
    t, d = ddn.shape
    tm = _pick(t, 512, 256, 128)
    tn = 2048

    def epi(acc, ex, outs):
        outs[0][...] = (acc[...] * (2.0 * jnp.sqrt(ex[0][...]).astype(F32))).astype(BF16)

    tile = pl.BlockSpec((tm, tn), lambda j, i, k: (i, j))
    return _mm("mlp_dact", ddn, w_down, nt=True, tm=tm, tn=tn, tk=d, epi=epi, extras=(act,), extra_specs=[tile],
               out_shape=(jax.ShapeDtypeStruct(act.shape, BF16),), out_specs=(tile,))[0]


def _norm_bwd_rows(dh, xv, nw, sc, st_ref):
    r = lax.rsqrt(jnp.mean(xv * xv, axis=-1, keepdims=True) + EPS)
    xh = xv * r
    dy = dh * (1.0 + sc)
    st_ref[0] += _fold8(dh)
    st_ref[1] += _fold8(dh * (xh * nw))
    st_ref[2] += _fold8(dy * xh)
    dxh = dy * nw
    return r * (dxh - xh * jnp.mean(dxh * xh, axis=-1, keepdims=True))


def _mixer_dmix(dmi, w_out, after):
    t, d = dmi.shape
    tm = _pick(t, 512, 256, 128)

    def epi(acc, ex, outs):
        outs[0][...] = acc[...].astype(BF16)

    n = w_out.shape[0]
    return _mm("mixer_dmix", dmi, w_out, nt=True, tm=tm, tn=n, tk=d, epi=epi, after=after,
               out_shape=(jax.ShapeDtypeStruct((t, n), BF16),), out_specs=(pl.BlockSpec((tm, n), lambda j, i, k: (i, 0)),))[0]


def _mixer_dx(name, dp, w_in, x, dx1, naw, sc_a, after):
    t, d = x.shape

    def epi(dh, rs, tiles, vecs, outs, st_ref, live):
        x_ref, dx1_ref = tiles
        nw, sc = vecs
        outs[0][rs, :] = _norm_bwd_rows(dh, x_ref[rs, :], nw[...], sc[...], st_ref) + dx1_ref[rs, :]

    return _mm_deferred(name, dp, w_in, nt=True, tm=_pick(t, 256, 128), epi=epi, tiles=(x, dx1), vecs=(naw, sc_a),
                        out_tiles=(jax.ShapeDtypeStruct((t, d), F32),), n_stats=3, after=after, vmem_mib=56)


def _silu(v):
    return v / (1.0 + jnp.exp(-v))


def _ada_fwd(cond, w_ada, b_ada):
    d, n = w_ada.shape
    tn = 512

    def body(c_ref, w_ref, b_ref, o_ref):
        o_ref[...] = _dot(_silu(c_ref[...]).astype(BF16), w_ref[...].astype(BF16)) + b_ref[...]

    return pl.pallas_call(
        body, name="ada_fwd", grid=(n // tn,),
        in_specs=[pl.BlockSpec(cond.shape, lambda j: (0, 0)), pl.BlockSpec((d, tn), lambda j: (0, j)), pl.BlockSpec((1, tn), lambda j: (0, j))],
        out_specs=pl.BlockSpec((cond.shape[0], tn), lambda j: (0, j)), out_shape=jax.ShapeDtypeStruct((cond.shape[0], n), F32),
        compiler_params=_cp(1),
    )(cond, w_ada, b_ada)


def _adamw_math(w, g, m, v):
    m = ADAM_B1 * m + (1.0 - ADAM_B1) * g
    v = ADAM_B2 * v + (1.0 - ADAM_B2) * jnp.square(g)
    m_hat = m / (1.0 - ADAM_B1 ** ADAM_STEP)
    v_hat = v / (1.0 - ADAM_B2 ** ADAM_STEP)
    return -ADAM_LR * (m_hat / (jnp.sqrt(v_hat) + ADAM_EPS) + ADAM_WD * w), m, v


def _ada_bwd(cond, dm, w_ada, m_ada, v_ada):
    d, n = w_ada.shape
    tn = 256
    rows = cond.shape[0]

    def body(c_ref, dm_ref, w_ref, m_ref, v_ref, g_ref, dl_ref, nm_ref, nv_ref, pc_ref):
        @pl.when(pl.program_id(0) == 0)
        def _():
            pc_ref[...] = jnp.zeros_like(pc_ref)

        dmb = dm_ref[...].astype(BF16)
        w = w_ref[...]
        g = _dot_tn(_silu(c_ref[...]).astype(BF16), dmb)
        g_ref[...] = g
        dl_ref[...], nm_ref[...], nv_ref[...] = _adamw_math(w, g, m_ref[...], v_ref[...])
        pc_ref[...] += _dot_nt(dm_ref[8:16, :].astype(BF16), w.astype(BF16))

    tile = pl.BlockSpec((d, tn), lambda j: (0, j))
    like = jax.ShapeDtypeStruct((d, n), F32)
    return pl.pallas_call(
        body, name="ada_bwd", grid=(n // tn,),
        in_specs=[pl.BlockSpec((rows, d), lambda j: (0, 0)), pl.BlockSpec((rows, tn), lambda j: (0, j)), tile, tile, tile],
        out_specs=(tile, tile, tile, tile, pl.BlockSpec((8, d), lambda j: (0, 0))),
        out_shape=(like, like, like, like, jax.ShapeDtypeStruct((8, d), F32)), compiler_params=_cp(1),
    )(cond, dm, w_ada, m_ada, v_ada)


def _adamw(name, w, g, m, v, after=None):
    return _ew(name, lambda w_, g_, m_, v_: (g_,) + _adamw_math(w_, g_, m_, v_), [w, g, m, v], [F32, F32, F32, F32],
               g if after is None else after)


def _colsum(st):
    return jnp.sum(st, axis=1)


def kernel(x, c, ctx, c_ctx, norm_attn_w, norm_mlp_w, w_ada, b_ada, w_in, attn_sink, pool_w, pool_scale, w_out, w_mlp_up, w_mlp_down, final_norm_w, loss_target, m_c_ctx, m_norm_attn_w, m_norm_mlp_w, m_w_ada, m_b_ada, m_w_in, m_attn_sink, m_pool_w, m_pool_scale, m_w_out, m_w_mlp_up, m_w_mlp_down, m_final_norm_w, v_c_ctx, v_norm_attn_w, v_norm_mlp_w, v_w_ada, v_b_ada, v_w_in, v_attn_sink, v_pool_w, v_pool_scale, v_w_out, v_w_mlp_up, v_w_mlp_down, v_final_norm_w):
    length, d = x.shape[1], x.shape[2]
    n_ctx = ctx.shape[1]
    pos = (lax.axis_index("x"), lax.axis_index("y"), lax.axis_index("c"))
    me, chip = _dev_index(pos), _chip_index(pos)
    xs, tgt, cx = x.reshape(length, d), loss_target.reshape(length, d), ctx.reshape(n_ctx, d)
    n_ada = w_ada.shape[2]

    c_all = _allgather8("gather_c", jnp.pad(c, ((0, 7), (0, 0))))
    mixer_bigs = [_Big("col", w_in.shape[1:]), _Big("pool", pool_w.shape[1:]), _Big("row", w_out.shape[1:])]
    mlp_bigs = [_Big("col", w_mlp_up.shape[1:]), _Big("row", w_mlp_down.shape[1:])]
    placed = [_cast_place(f"place_{i}", b, s, c_all)[0] for i, (b, s) in enumerate(zip(mixer_bigs, [w_in[0], pool_w[0], w_out[0]]))]
    flight = _split("gather_mixer_ici", placed, _gather_ici_remote(mixer_bigs, 0))
    token, placed_mlp = flight[3], []
    for i, (b, s) in enumerate(zip(mlp_bigs, [w_mlp_up[0], w_mlp_down[0]])):
        p, token = _cast_place(f"place_mlp_{i}", b, s, token)
        placed_mlp.append(p)
    cos, sin = _rope_tables(length, True, token[0, 0])
    cond = jnp.concatenate([c_all[:, 0, :], jnp.pad(c_ctx[None, :], ((0, 7), (0, 0)))], axis=0) + 0.0 * cos[0, 0]
    b_shard = lax.dynamic_slice_in_dim(b_ada, chip * n_ada, n_ada, axis=1)
    mod_all = _allgather8("gather_mod", _ada_fwd(cond, w_ada[0], b_shard))
    mod = jnp.concatenate([mod_all[0], mod_all[2], mod_all[4], mod_all[6]], axis=1)
    mine = lax.dynamic_slice_in_dim(mod, me, 1, axis=0)
    sh_a, sc_a, g_a, sh_m, sc_m, g_m = [mine[:, d * i:d * (i + 1)] for i in range(6)]
    csh_a, csc_a = mod[8:9, :d], mod[8:9, d:2 * d]

    win_b, pw_b, wout_b = _exchange("gather_mixer_d2d", _join(flight, mod), [jax.ShapeDtypeStruct(b.full_shape, BF16) for b in mixer_bigs],
                                    _gather_d2d_remote(mixer_bigs, 3), aliases={0: 0, 1: 1, 2: 2})
    wout_b = wout_b.reshape(-1, d)
    flight = _split("gather_mlp_ici", placed_mlp, _gather_ici_remote(mlp_bigs, 0), after=pw_b)

    one, zero = _rope_tables(n_ctx, False)
    h, q, k, v, u = _mixer_in("mixer_in", xs, norm_attn_w, sh_a, sc_a, win_b, cos, sin, flight[3])
    hc, _, kc, vc, _ = _mixer_in("mixer_in_ctx", cx, norm_attn_w, csh_a, csc_a, win_b, one, zero, flight[3])
    attn, probs = _attn_fwd(q, k, v, kc, vc, attn_sink)
    mix = _pool_fwd(u, pw_b, pool_scale, attn)
    flight = _split("gather_mlp_d2d", _join(flight, mix), _gather_d2d_remote(mlp_bigs, 0))
    x1, mo, hm = _mixer_out(mix, wout_b, xs, g_a, norm_mlp_w, sh_m, sc_m, flight[3])
    wup_b, wdn_b = _join(flight, hm)
    wdn_b = wdn_b.reshape(-1, d)
    act = _mlp_up(hm, wup_b)
    dx2, ddn, st_loss = _mlp_down_loss(act, wdn_b, x1, tgt, g_m, final_norm_w[None, :], c)
    st_loss = _colsum(st_loss)

    tt = _pick(length, 2048, 1024, 512, 256, 128)
    g_wdn = _mm_tn("grad_w_down", act, ddn, BF16, tmo=1024, tn=d, tt=tt)
    dup = _mlp_dact(ddn, wdn_b, act)
    g_wup = _mm_tn("grad_w_up", hm, dup, BF16, tmo=d, tn=1024, tt=tt)
    empty = lambda shapes: [lax.empty(s.shape, s.dtype) for s in shapes]
    grads = [g_wup, g_wdn.reshape(mlp_bigs[1].full_shape)]
    flight = _split("reduce_mlp_d2d", grads + empty(_halves(mlp_bigs)), _reduce_d2d_remote(mlp_bigs))
    dx1, dmi, st_mlp = _mlp_dx(dup, wup_b, x1, dx2, mo, norm_mlp_w, sc_m, g_a, flight[3])
    st_mlp = _colsum(st_mlp)
    landed = _join(flight, dmi)
    mlp_chip = _chip_sums("mlp", mlp_bigs, landed[:2], landed[2:])
    flight = _split("reduce_mlp_ici", mlp_chip + empty(_thirds(mlp_bigs)), _reduce_ici_remote(mlp_bigs))
    g_wout = _mm_tn("grad_w_out", mix, dmi, BF16, tmo=1024, tn=d, tt=tt)
    dmix = _mixer_dmix(dmi, wout_b, flight[3])
    dq, dkp, dvp, dkc, dvc, dsink = _attn_bwd(q, k, v, kc, vc, dmix, probs)
    landed = _join(flight, dq)
    flight = _split("reduce_mlp_share", _piece_sums("mlp", mlp_bigs, landed[:2], landed[2:]), _share_remote(mlp_bigs, 0))
    du, g_pw, st_pool = _pool_bwd(u, dmix, pw_b, pool_scale, flight[3])
    g_mlp = _join(flight, du)

    wo_bigs, win_bigs = mixer_bigs[1:], mixer_bigs[:1]
    wo_chip = _reduce_to_chip("wo", wo_bigs, [g_pw.astype(BF16), g_wout.reshape(wo_bigs[1].full_shape)])
    flight = _split("reduce_wo_ici", wo_chip + empty(_thirds(wo_bigs)), _reduce_ici_remote(wo_bigs))
    dp = _assemble_dp(dq, dkp, dvp, du, cos, sin, flight[3])
    dpc = jnp.concatenate([jnp.zeros((n_ctx, ATTN_WIDTH), BF16), dkc.astype(BF16), dvc.astype(BF16),
                           jnp.zeros((n_ctx, POOL_WIDTH), BF16)], axis=1)
    g_win = _mm_tn("grad_w_in", h, dp, BF16, tmo=d, tn=dp.shape[1] // 2, tt=_pick(length, 1024, 512, 256, 128), more=(hc, dpc))
    wo_landed = _join(flight, g_win)
    win_chip = _reduce_to_chip("win", win_bigs, [g_win])
    flight = _split("reduce_win_ici", win_chip + empty(_thirds(win_bigs)), _reduce_ici_remote(win_bigs))
    grad_x, st_mix = _mixer_dx("mixer_dx", dp, win_b, xs, dx1, norm_attn_w, sc_a, flight[3])
    _, st_ctx = _mixer_dx("mixer_dx_ctx", dpc, win_b, cx, jnp.zeros((n_ctx, d), F32), norm_attn_w, csc_a, flight[3])
    st_mix, st_ctx = _colsum(st_mix), _colsum(st_ctx)
    win_landed = _join(flight, grad_x)
    g_mixer = (_reduce_finish("win", win_bigs, win_landed[:1], win_landed[1:])
               + _reduce_finish("wo", wo_bigs, wo_landed[:2], wo_landed[2:]))

    zrow = jnp.zeros((d,), F32)
    pad = lambda a: jnp.pad(a, (0, d - a.shape[0]))
    mine_rows = [st_mix[0], st_mix[1], st_mlp[3], st_mlp[0], st_mlp[1], st_loss[2],
                 st_ctx[0], st_ctx[1],
                 st_mix[2] + st_ctx[2], st_mlp[2], st_loss[1],
                 pad(jnp.sum(st_pool, axis=0)), pad(dsink[0, :N_Q_HEADS]), st_loss[0]] + [zrow] * 2
    flight = _allgather8_split("gather_small", jnp.concatenate(mine_rows).reshape(len(mine_rows), d), me)
    res = {"w_mlp_up": tuple(_adamw("adamw_w_mlp_up", w_mlp_up, g_mlp[0].reshape(w_mlp_up.shape), m_w_mlp_up, v_w_mlp_up, flight[3]))}
    small_all = _join(flight, res["w_mlp_up"][1])[1]
    small = small_all[0]
    for i in range(1, 8):
        small = small + small_all[i]
    loss = 0.5 / d * jnp.sum(small[13])
    dm_rows = small_all[:, 0:6, :].reshape(8, 6 * d)
    dm_ctx = jnp.concatenate([small[6], small[7], jnp.zeros((4 * d,), F32)])[None, :]
    dm = jnp.concatenate([dm_rows, jnp.pad(dm_ctx, ((0, 7), (0, 0)))], axis=0)
    g_bada = jnp.sum(dm[:9], axis=0, keepdims=True)
    dm_shard = lax.dynamic_slice_in_dim(dm, chip * n_ada, n_ada, axis=1)
    g_wada, dl_wada, nm_wada, nv_wada, part_cctx = _ada_bwd(cond, dm_shard, w_ada[0], m_w_ada[0], v_w_ada[0])
    flight = _allgather8_split("gather_cctx", part_cctx, me)
    res["w_mlp_down"] = tuple(_adamw("adamw_w_mlp_down", w_mlp_down, g_mlp[1].reshape(w_mlp_down.shape), m_w_mlp_down,
                                     v_w_mlp_down, flight[3]))
    cctx_all = _join(flight, res["w_mlp_down"][1])[1]
    dsilu_in = cctx_all[0, 0] + cctx_all[2, 0] + cctx_all[4, 0] + cctx_all[6, 0]
    sig = 1.0 / (1.0 + jnp.exp(-c_ctx))
    g_cctx = dsilu_in * (sig * (1.0 + c_ctx * (1.0 - sig)))

    for nm, w_, g_, m_, v_ in zip(["w_in", "pool_w", "w_out"], [w_in, pool_w, w_out], g_mixer,
                                  [m_w_in, m_pool_w, m_w_out], [v_w_in, v_pool_w, v_w_out]):
        res[nm] = tuple(_adamw("adamw_" + nm, w_, g_.reshape(w_.shape), m_, v_))
    res["w_ada"] = (g_wada[None], dl_wada[None], nm_wada[None], nv_wada[None])

    def pack(cc, na, nm_, ba, sk, ps, fn):
        flat = [cc.reshape(-1), na.reshape(-1), nm_.reshape(-1), ba.reshape(-1), pad(sk.reshape(-1)), pad(ps.reshape(-1)),
                fn.reshape(-1), jnp.zeros((4 * d,), F32)]
        return jnp.concatenate(flat).reshape(16, d)

    w_s = pack(c_ctx, norm_attn_w, norm_mlp_w, b_ada, attn_sink, pool_scale, final_norm_w)
    m_s = pack(m_c_ctx, m_norm_attn_w, m_norm_mlp_w, m_b_ada, m_attn_sink, m_pool_scale, m_final_norm_w)
    v_s = pack(v_c_ctx, v_norm_attn_w, v_norm_mlp_w, v_b_ada, v_attn_sink, v_pool_scale, v_final_norm_w)
    g_s = pack(g_cctx, small[8], small[9], g_bada, small[12][:N_Q_HEADS], small[11][:POOL_WIDTH], small[10])
    small_out = _adamw("adamw_small", w_s, g_s, m_s, v_s)

    def unpack(p):
        return {"c_ctx": p[0], "norm_attn_w": p[1:2], "norm_mlp_w": p[2:3], "b_ada": p[3:9].reshape(1, 6 * d),
                "attn_sink": p[9:10, :N_Q_HEADS], "pool_scale": p[10:11, :POOL_WIDTH], "final_norm_w": p[11]}

    small_res = [unpack(p) for p in small_out]
    order = ["c_ctx", "norm_attn_w", "norm_mlp_w", "w_ada", "b_ada", "w_in", "attn_sink", "pool_w", "pool_scale",
             "w_out", "w_mlp_up", "w_mlp_down", "final_norm_w"]
    outs = [loss, grad_x.reshape(x.shape)]
    for kind in range(4):
        for nm in order:
            outs.append(res[nm][kind] if nm in res else small_res[kind][nm])
    return tuple(outs)
```

```python
import functools

import jax
import jax.numpy as jnp
from jax import lax
from jax.experimental import pallas as pl
from jax.experimental.pallas import tpu as pltpu

F32 = jnp.float32
BF16 = jnp.bfloat16
EPS = 1e-6
NEG_INF = -1e30
HEAD_DIM = 64
N_Q_HEADS = 16
N_KV_HEADS = 4
GROUP = N_Q_HEADS // N_KV_HEADS
ATTN_WIDTH = N_Q_HEADS * HEAD_DIM
KV_WIDTH = N_KV_HEADS * HEAD_DIM
POOL_WINDOWS = (2, 4, 8, 16)
POOL_GROUP_DIM = 256
POOL_WIDTH = len(POOL_WINDOWS) * POOL_GROUP_DIM
BLOCK = 128
GRID_W = 64
ROPE_BASE = 10000.0
SCALE = HEAD_DIM ** -0.5
HALO = 16
STRIP = 16
ADAM_LR, ADAM_B1, ADAM_B2, ADAM_EPS, ADAM_WD, ADAM_STEP = 0.001, 0.9, 0.999, 1e-08, 0.01, 10
MESH = pl.DeviceIdType.MESH
MIB = 1024 * 1024
ANY = pl.BlockSpec(memory_space=pl.ANY)


def _cp(n_axes, vmem_mib=48):
    return pltpu.CompilerParams(dimension_semantics=("arbitrary",) * n_axes, vmem_limit_bytes=vmem_mib * MIB)


def _fold8(v):
    s = v[0:8]
    for t in range(1, v.shape[0] // 8):
        s = s + v[8 * t:8 * t + 8]
    return s


def _dot(a, b):
    return jnp.dot(a, b, preferred_element_type=F32)


def _dot_nt(a, b):
    return lax.dot_general(a, b, (((1,), (1,)), ((), ())), preferred_element_type=F32)


def _dot_tn(a, b):
    return lax.dot_general(a, b, (((0,), (0,)), ((), ())), preferred_element_type=F32)


def _pick(n, *cands):
    for t in cands:
        if n % t == 0:
            return t
    return n


def _flip(pos, mask):
    return tuple((1 - v) if (mask >> (2 - i)) & 1 else v for i, v in enumerate(pos))


def _exchange(name, ins, out_shapes, remote, local=(), aliases=None):
    n_io = len(ins) + len(out_shapes)

    def body(*refs):
        io = refs[:n_io]
        send_sems, recv_sems, local_sems = refs[n_io:]
        me = (lax.axis_index("x"), lax.axis_index("y"), lax.axis_index("c"))

        def copy(i, sender):
            mask, src_fn, dst_fn = remote[i]
            return pltpu.make_async_remote_copy(
                src_ref=src_fn(io, sender), dst_ref=dst_fn(io, sender), send_sem=send_sems.at[i],
                recv_sem=recv_sems.at[i], device_id=_flip(sender, mask), device_id_type=MESH)

        own = [pltpu.make_async_copy(s(io, me), d(io, me), local_sems.at[i]) for i, (s, d) in enumerate(local)]
        for cp in own:
            cp.start()
        sends = [copy(i, me) for i in range(len(remote))]
        for cp in sends:
            cp.start()
        for i in range(len(remote)):
            copy(i, _flip(me, remote[i][0])).wait_recv()
        for cp in sends:
            cp.wait_send()
        for cp in own:
            cp.wait()

    return pl.pallas_call(
        body, name=name, out_shape=tuple(out_shapes),
        in_specs=[ANY] * len(ins), out_specs=tuple([ANY] * len(out_shapes)),
        scratch_shapes=[pltpu.SemaphoreType.DMA((len(remote),)), pltpu.SemaphoreType.DMA((len(remote),)),
                        pltpu.SemaphoreType.DMA((max(len(local), 1),))],
        input_output_aliases=aliases or {},
    )(*ins)


HBM = pl.BlockSpec(memory_space=pltpu.HBM)
SEM = pl.BlockSpec(memory_space=pltpu.SEMAPHORE)
EFFECT = pltpu.SideEffectType.DATAFLOW_SIDE_EFFECTING


def _split_copy(remote, i, io, send_sems, recv_sems, sender):
    mask, src_fn, dst_fn = remote[i]
    return pltpu.make_async_remote_copy(
        src_ref=src_fn(io, sender), dst_ref=dst_fn(io, sender), send_sem=send_sems.at[i],
        recv_sem=recv_sems.at[i], device_id=_flip(sender, mask), device_id_type=MESH)


def _exchange_start(name, bufs, remote, after=None):
    n, r = len(bufs), len(remote)
    more = [] if after is None else [after]

    def body(*refs):
        io, (send_sems, recv_sems, token) = refs[:n], refs[-3:]
        me = (lax.axis_index("x"), lax.axis_index("y"), lax.axis_index("c"))
        for i in range(r):
            _split_copy(remote, i, io, send_sems, recv_sems, me).start()
        token[...] = jnp.zeros_like(token)

    res = pl.pallas_call(
        body, name=name,
        out_shape=tuple(pltpu.HBM(b.shape, b.dtype) for b in bufs)
        + (pltpu.SemaphoreType.DMA((r,)), pltpu.SemaphoreType.DMA((r,)), jax.ShapeDtypeStruct((8, 128), F32)),
        in_specs=[HBM] * n + [ANY] * len(more), out_specs=tuple([HBM] * n) + (SEM, SEM, pl.BlockSpec(memory_space=pltpu.VMEM)),
        input_output_aliases={i: i for i in range(n)}, compiler_params=pltpu.CompilerParams(has_side_effects=EFFECT),
    )(*[pltpu.with_memory_space_constraint(b, pltpu.HBM) for b in bufs], *more)
    return list(res[:n]), res[n], res[n + 1], res[n + 2]


def _exchange_wait(name, bufs, send_sems, recv_sems, remote, after):
    n, r = len(bufs), len(remote)

    def body(*refs):
        io, ss, rs = refs[:n], refs[n], refs[n + 1]
        me = (lax.axis_index("x"), lax.axis_index("y"), lax.axis_index("c"))
        for i in range(r):
            _split_copy(remote, i, io, ss, rs, _flip(me, remote[i][0])).wait_recv()
        for i in range(r):
            _split_copy(remote, i, io, ss, rs, me).wait_send()

    return list(pl.pallas_call(
        body, name=name, out_shape=tuple(pltpu.HBM(b.shape, b.dtype) for b in bufs),
        in_specs=[HBM] * n + [SEM, SEM, ANY], out_specs=tuple([HBM] * n),
        input_output_aliases={i: i for i in range(n)}, compiler_params=pltpu.CompilerParams(has_side_effects=EFFECT),
    )(*bufs, send_sems, recv_sems, after))


def _my_c():
    return lax.axis_index("c")


def _my_chip():
    return 2 * lax.axis_index("x") + lax.axis_index("y")


def _dev_index(pos):
    return 4 * pos[0] + 2 * pos[1] + pos[2]


def _chip_index(pos):
    return 2 * pos[0] + pos[1]


def _allgather8(name, v):
    out = jax.ShapeDtypeStruct((8,) + v.shape, v.dtype)
    remote = [(mask, lambda io, pos: io[0], lambda io, pos: io[1].at[_dev_index(pos)]) for mask in range(1, 8)]
    local = [(lambda io, pos: io[0], lambda io, pos: io[1].at[_dev_index(pos)])]
    return _exchange(name, [v], [out], remote, local)[0]


class _Big:
    def __init__(self, kind, shard_shape):
        self.kind = kind
        self.shard_shape = tuple(shard_shape)
        if kind == "col":
            r, cs = shard_shape
            self.full_shape = (r, 4 * cs)
            self.piece_shape = (r // 2, cs)
            self.half_shape = (r // 2, 4 * cs)
        elif kind == "row":
            rs, c = shard_shape
            self.full_shape = (4, 2, rs // 2, c)
            self.piece_shape = (1, 1, rs // 2, c)
            self.half_shape = (4, 1, rs // 2, c)
        else:
            self.full_shape = (4, 256, 256)
            self.piece_shape = (2, 64, 256)
            self.half_shape = (2, 256, 256)

    def shard_as_pieces(self, a):
        return a.reshape((1, 2) + self.piece_shape[2:]) if self.kind == "row" else a

    def piece(self, ref, k, h):
        if self.kind == "col":
            r, cs = self.piece_shape
            return ref.at[pl.ds(h * r, r), pl.ds(k * cs, cs)]
        if self.kind == "row":
            return ref.at[pl.ds(k, 1), pl.ds(h, 1)]
        return ref.at[pl.ds(2 * h, 2), pl.ds(64 * k, 64)]

    def half_of_shard(self, ref, h):
        if self.kind == "col":
            return ref.at[pl.ds(h * self.piece_shape[0], self.piece_shape[0])]
        if self.kind == "row":
            return ref.at[:, pl.ds(h, 1)]
        return ref.at[pl.ds(2 * h, 2)]

    def half_of_full(self, ref, h):
        if self.kind == "col":
            return ref.at[pl.ds(h * self.half_shape[0], self.half_shape[0])]
        if self.kind == "row":
            return ref.at[:, pl.ds(h, 1)]
        return ref.at[pl.ds(2 * h, 2)]

    def piece_of_half(self, ref, k):
        if self.kind == "col":
            return ref.at[:, pl.ds(k * self.piece_shape[1], self.piece_shape[1])]
        if self.kind == "row":
            return ref.at[pl.ds(k, 1)]
        return ref.at[:, pl.ds(64 * k, 64)]


CHIP_MASKS = (4, 2, 6)


def _cast_place(name, big, shard, after):
    if big.kind == "col":
        r, cs = big.shard_shape
        tr = _pick(r, 512, 256, 128)
        src, grid, blk = shard, (r // tr,), (tr, cs)
        imap, omap = (lambda i: (i, 0)), (lambda i: (i, _my_chip()))
    elif big.kind == "row":
        rs, c = big.shard_shape
        tr = _pick(rs // 2, 256, 128)
        src, grid, blk = big.shard_as_pieces(shard), (2, rs // 2 // tr), (1, 1, tr, c)
        imap, omap = (lambda h, i: (0, h, i, 0)), (lambda h, i: (_my_chip(), h, i, 0))
    else:
        src, grid, blk = shard, (1,), big.shard_shape
        imap, omap = (lambda i: (0, 0, 0)), (lambda i: (0, _my_chip(), 0))

    def body(s_ref, after_ref, o_ref, token_ref):
        o_ref[...] = s_ref[...].astype(BF16)
        token_ref[...] = jnp.zeros_like(token_ref)

    return pl.pallas_call(
        body, name=name, grid=grid, in_specs=[pl.BlockSpec(blk, imap), ANY],
        out_specs=(pl.BlockSpec(blk, omap), pl.BlockSpec((8, 128), lambda *_: (0, 0))),
        out_shape=(jax.ShapeDtypeStruct(big.full_shape, BF16), jax.ShapeDtypeStruct((8, 128), F32)), compiler_params=_cp(len(grid)),
    )(src, after)


def _gather_ici_remote(bigs, off):
    remote = []
    for a, b in enumerate(bigs):
        for mask in CHIP_MASKS:
            def mine(io, p, a=a, b=b):
                return b.piece(io[off + a], _chip_index(p), p[2])
            remote.append((mask, mine, mine))
    return remote


def _gather_d2d_remote(bigs, off):
    remote = []
    for a, b in enumerate(bigs):
        for mask in CHIP_MASKS:
            def region(io, p, a=a, b=b, mask=mask):
                return b.piece(io[off + a], _chip_index(_flip(p, mask)), p[2])
            remote.append((1, region, region))
    return remote


def _ew(name, fn, ins, out_dtypes, after, rows_per_step=256):
    shape = ins[0].shape
    last = shape[-1]
    rows = 1
    for s in shape[:-1]:
        rows *= s
    ins2 = [a.reshape(rows, last) for a in ins]
    tr = _pick(rows, rows_per_step, 128, 64, 32, 16, 8)
    spec = pl.BlockSpec((tr, last), lambda i: (i, 0))

    def body(*refs):
        outs = fn(*[r[...] for r in refs[:len(ins)]])
        for o_ref, o in zip(refs[len(ins) + 1:], outs):
            o_ref[...] = o.astype(o_ref.dtype)

    outs = pl.pallas_call(
        body, name=name, grid=(rows // tr,), in_specs=[spec] * len(ins) + [ANY], out_specs=tuple([spec] * len(out_dtypes)),
        out_shape=tuple(jax.ShapeDtypeStruct((rows, last), d) for d in out_dtypes), compiler_params=_cp(1),
    )(*ins2, after)
    return [o.reshape(shape) for o in outs]


def _chip_sum(name, big, grad, from_sibling):
    if big.kind == "col":
        rh, w = big.half_shape
        tr = _pick(rh, 256, 128)
        nb = rh // tr
        grid, blk = (nb,), (tr, w)
        gmap, hmap = (lambda i: (_my_c() * nb + i, 0)), (lambda i: (i, 0))
    elif big.kind == "row":
        rh, w = big.half_shape[2:]
        tr = _pick(rh, 256, 128)
        grid, blk = (4, rh // tr), (1, 1, tr, w)
        gmap, hmap = (lambda k, i: (k, _my_c(), i, 0)), (lambda k, i: (k, 0, i, 0))
    else:
        grid, blk = (1,), big.half_shape
        gmap, hmap = (lambda i: (_my_c(), 0, 0)), (lambda i: (0, 0, 0))

    def body(g_ref, s_ref, o_ref):
        o_ref[...] = (g_ref[...].astype(F32) + s_ref[...].astype(F32)).astype(BF16)

    return pl.pallas_call(
        body, name=name, grid=grid, in_specs=[pl.BlockSpec(blk, gmap), pl.BlockSpec(blk, hmap)],
        out_specs=pl.BlockSpec(blk, hmap), out_shape=jax.ShapeDtypeStruct(big.half_shape, BF16), compiler_params=_cp(len(grid)),
    )(grad, from_sibling)


def _piece_sum(name, big, chip_sum, thirds):
    if big.kind == "col":
        rp, cs = big.piece_shape
        tr = _pick(rp, 256, 128)
        nb = rp // tr
        grid, blk, tblk = (nb,), (tr, cs), (1, tr, cs)
        smap, omap = (lambda i: (i, _my_chip())), (lambda i: (_my_c() * nb + i, 0))
        tmap = lambda j: (lambda i: (j, i, 0))
        out_shape = big.shard_shape
    elif big.kind == "row":
        rp, w = big.piece_shape[2:]
        tr = _pick(rp, 256, 128)
        grid, blk, tblk = (rp // tr,), (1, 1, tr, w), (1, 1, 1, tr, w)
        smap, omap = (lambda i: (_my_chip(), 0, i, 0)), (lambda i: (0, _my_c(), i, 0))
        tmap = lambda j: (lambda i: (j, 0, 0, i, 0))
        out_shape = (1, 2, rp, w)
    else:
        grid, blk, tblk = (1,), big.piece_shape, (1,) + big.piece_shape
        smap, omap = (lambda i: (0, _my_chip(), 0)), (lambda i: (_my_c(), 0, 0))
        tmap = lambda j: (lambda i: (j, 0, 0, 0))
        out_shape = big.shard_shape

    def body(s_ref, t0, t1, t2, o_ref):
        o_ref[...] = s_ref[...].astype(F32) + t0[0].astype(F32) + t1[0].astype(F32) + t2[0].astype(F32)

    return pl.pallas_call(
        body, name=name, grid=grid,
        in_specs=[pl.BlockSpec(blk, smap)] + [pl.BlockSpec(tblk, tmap(j)) for j in range(3)],
        out_specs=pl.BlockSpec(blk, omap), out_shape=jax.ShapeDtypeStruct(out_shape, F32), compiler_params=_cp(len(grid)),
    )(chip_sum, thirds, thirds, thirds)


def _split(name, bufs, remote, after=None):
    return _exchange_start(name + "_start", bufs, remote, after) + (remote, name)


def _join(handle, after):
    bufs, send_sems, recv_sems, _, remote, name = handle
    return _exchange_wait(name + "_wait", bufs, send_sems, recv_sems, remote, after)


def _allgather8_split(name, v, me):
    own = lax.dynamic_update_slice(lax.empty((8,) + v.shape, v.dtype), v[None], (me, 0, 0))
    remote = [(mask, lambda io, pos: io[0], lambda io, pos: io[1].at[_dev_index(pos)]) for mask in range(1, 8)]
    return _split(name, [v, own], remote)


def _reduce_d2d_remote(bigs):
    n = len(bigs)
    return [(1, lambda io, p, a=a, b=b: b.half_of_full(io[a], 1 - p[2]), lambda io, p, a=a: io[n + a])
            for a, b in enumerate(bigs)]


def _halves(bigs):
    return [jax.ShapeDtypeStruct(b.half_shape, BF16) for b in bigs]


def _chip_sums(tag, bigs, grads, from_sibling):
    return [_chip_sum(f"reduce_{tag}_chip_sum_{a}", b, g, r) for a, (b, g, r) in enumerate(zip(bigs, grads, from_sibling))]


def _reduce_to_chip(tag, bigs, grads):
    from_sibling = _exchange(f"reduce_{tag}_d2d", grads, _halves(bigs), _reduce_d2d_remote(bigs))
    return _chip_sums(tag, bigs, grads, from_sibling)


def _reduce_ici_remote(bigs):
    n = len(bigs)
    remote = []
    for a, b in enumerate(bigs):
        for j, mask in enumerate(CHIP_MASKS):
            remote.append((mask,
                           lambda io, p, a=a, b=b, mask=mask: b.piece_of_half(io[a], _chip_index(_flip(p, mask))),
                           lambda io, p, a=a, j=j: io[n + a].at[j]))
    return remote


def _thirds(bigs):
    return [jax.ShapeDtypeStruct((3,) + b.piece_shape, BF16) for b in bigs]


def _piece_sums(tag, bigs, chip_sum, from_chips):
    return [_piece_sum(f"reduce_{tag}_sum_{a}", b, s, r) for a, (b, s, r) in enumerate(zip(bigs, chip_sum, from_chips))]


def _share_remote(bigs, off):
    remote = []
    for a, b in enumerate(bigs):
        def mine(io, p, a=a, b=b):
            return b.half_of_shard(io[off + a], p[2])
        remote.append((1, mine, mine))
    return remote


def _reduce_finish(tag, bigs, chip_sum, from_chips):
    n = len(bigs)
    placed = _piece_sums(tag, bigs, chip_sum, from_chips)
    out = _exchange(f"reduce_{tag}_share_d2d", placed, [jax.ShapeDtypeStruct(p.shape, F32) for p in placed],
                    _share_remote(bigs, n), aliases={a: a for a in range(n)})
    return [o.reshape(b.shard_shape) for o, b in zip(out, bigs)]


def _mm(name, a, b, *, nt, tm, tn, tk, epi, extras=(), extra_specs=(), out_shape, out_specs, after=None, vmem_mib=48):
    m, kdim = a.shape
    n = b.shape[0] if nt else b.shape[1]
    gm, gn, gk = m // tm, n // tn, kdim // tk
    a_spec = pl.BlockSpec((tm, tk), lambda j, i, k: (i, k))
    b_spec = pl.BlockSpec((tn, tk), lambda j, i, k: (j, k)) if nt else pl.BlockSpec((tk, tn), lambda j, i, k: (k, j))
    n_ex = len(extras)
    if after is not None:
        extras, extra_specs = tuple(extras) + (after,), list(extra_specs) + [ANY]

    def body(a_ref, b_ref, *rest):
        ex, outs, acc = rest[:n_ex], rest[len(extras):-1], rest[-1]
        dot = _dot_nt if nt else _dot
        if gk == 1:
            acc[...] = dot(a_ref[...], b_ref[...])
            epi(acc, ex, outs)
        else:
            k = pl.program_id(2)

            @pl.when(k == 0)
            def _():
                acc[...] = dot(a_ref[...], b_ref[...])

            @pl.when(k > 0)
            def _():
                acc[...] += dot(a_ref[...], b_ref[...])

            @pl.when(k == gk - 1)
            def _():
                epi(acc, ex, outs)

    return pl.pallas_call(
        body, name=name, grid=(gn, gm, gk), in_specs=[a_spec, b_spec, *extra_specs], out_specs=tuple(out_specs),
        out_shape=tuple(out_shape), scratch_shapes=[pltpu.VMEM((tm, tn), F32)], compiler_params=_cp(3, vmem_mib),
    )(a, b, *extras)


def _mm_deferred(name, a, b, *, nt, tm, epi, tiles, vecs, out_tiles, n_stats, after, vmem_mib=48):
    pieces = a if isinstance(a, (list, tuple)) else [(a, 0)]
    m = pieces[0][0].shape[0]
    n = b.shape[0] if nt else b.shape[1]
    gm = m // tm
    n_a, n_t, n_v, n_o = len(pieces), len(tiles), len(vecs), len(out_tiles)

    def body(*refs):
        a_refs, b_ref, rest = refs[:n_a], refs[n_a], refs[n_a + 1:]
        t_refs, v_refs = rest[:n_t], rest[n_t:n_t + n_v]
        o_refs, st_ref, acc0, acc1 = rest[n_t + n_v + 1:n_t + n_v + 1 + n_o], rest[-3], rest[-2], rest[-1]
        i = pl.program_id(0)

        def dot():
            if n_a == 1 and pieces[0][0].shape[1] == b.shape[1 if nt else 0]:
                return (_dot_nt if nt else _dot)(a_refs[0][...], b_ref[...])
            parts = [_dot_nt(r[...], b_ref[:, off:off + p.shape[1]]) for r, (p, off) in zip(a_refs, pieces)]
            return functools.reduce(lambda u, v: u + v, parts)

        @pl.when(i == 0)
        def _():
            acc1[...] = jnp.zeros_like(acc1)
            st_ref[...] = jnp.zeros_like(st_ref)

        def finish(prev):
            for r0 in range(0, tm, STRIP):
                rs = slice(r0, r0 + STRIP)
                epi(prev[rs, :], rs, t_refs, v_refs, o_refs, st_ref, i > 0)

        @pl.when((i % 2 == 0) & (i < gm))
        def _():
            acc0[...] = dot()
            finish(acc1)

        @pl.when((i % 2 == 1) & (i < gm))
        def _():
            acc1[...] = dot()
            finish(acc0)

        @pl.when(i == gm)
        def _():
            finish(acc1 if gm % 2 == 0 else acc0)

    prev = lambda i: (jnp.maximum(i - 1, 0), 0)
    tile = pl.BlockSpec((tm, n), prev)
    return pl.pallas_call(
        body, name=name, grid=(gm + 1,),
        in_specs=[pl.BlockSpec((tm, p.shape[1]), lambda i: (jnp.minimum(i, gm - 1), 0)) for p, _ in pieces]
        + [pl.BlockSpec(b.shape, lambda i: (0, 0))] + [tile] * n_t + [_row_spec(n)] * n_v + [ANY],
        out_specs=tuple([tile] * n_o) + (_stat_spec(n_stats, n),),
        out_shape=tuple(out_tiles) + (jax.ShapeDtypeStruct((n_stats, 8, n), F32),),
        scratch_shapes=[pltpu.VMEM((tm, n), F32), pltpu.VMEM((tm, n), F32)], compiler_params=_cp(1, vmem_mib),
    )(*[p for p, _ in pieces], b, *tiles, *vecs, after)


def _mm_k_deferred(name, a, b, *, nt, tm, tk, epi, tiles, vecs, out_tiles, n_stats, after, vmem_mib=56):
    m, kdim = a.shape
    n = b.shape[0] if nt else b.shape[1]
    gm, gk = m // tm, kdim // tk
    rows = tm // gk
    n_t, n_v, n_o = len(tiles), len(vecs), len(out_tiles)
    dot = _dot_nt if nt else _dot

    def body(a_ref, b_ref, *rest):
        t_refs, v_refs = rest[:n_t], rest[n_t:n_t + n_v]
        o_refs, st_ref, acc0, acc1 = rest[n_t + n_v + 1:n_t + n_v + 1 + n_o], rest[-3], rest[-2], rest[-1]
        i, k = pl.program_id(0), pl.program_id(1)

        @pl.when((i == 0) & (k == 0))
        def _():
            acc1[...] = jnp.zeros_like(acc1)
            st_ref[...] = jnp.zeros_like(st_ref)

        def finish(prev):
            for r0 in range(0, rows, STRIP):
                acc_rows = prev[pl.ds(pl.multiple_of(k * rows + r0, STRIP), STRIP), :]
                epi(acc_rows, slice(r0, r0 + STRIP), t_refs, v_refs, o_refs, st_ref, i > 0)

        def step(cur, prev):
            cur[...] = jnp.where(k > 0, cur[...], 0.0) + dot(a_ref[...], b_ref[...])
            finish(prev)

        @pl.when((i % 2 == 0) & (i < gm))
        def _():
            step(acc0, acc1)

        @pl.when((i % 2 == 1) & (i < gm))
        def _():
            step(acc1, acc0)

        @pl.when(i == gm)
        def _():
            finish(acc1 if gm % 2 == 0 else acc0)

    prev = lambda i, k: (jnp.where(i == 0, 0, (i - 1) * gk + k), 0)
    part = pl.BlockSpec((rows, n), prev)
    b_spec = pl.BlockSpec((n, tk), lambda i, k: (0, k)) if nt else pl.BlockSpec((tk, n), lambda i, k: (k, 0))
    return pl.pallas_call(
        body, name=name, grid=(gm + 1, gk),
        in_specs=[pl.BlockSpec((tm, tk), lambda i, k: (jnp.minimum(i, gm - 1), k)), b_spec]
        + [part] * n_t + [_row_spec(n)] * n_v + [ANY],
        out_specs=tuple([part] * n_o) + (_stat_spec(n_stats, n),),
        out_shape=tuple(out_tiles) + (jax.ShapeDtypeStruct((n_stats, 8, n), F32),),
        scratch_shapes=[pltpu.VMEM((tm, n), F32), pltpu.VMEM((tm, n), F32)], compiler_params=_cp(2, vmem_mib),
    )(a, b, *tiles, *vecs, after)


def _mm_tn(name, a, b, out_dtype, *, tmo, tn, tt, more=(), vmem_mib=56):
    t, m = a.shape
    n = b.shape[1]
    gt = t // tt

    def body(a_ref, b_ref, *rest):
        o_ref, acc = rest[-2:]
        k = pl.program_id(2)

        @pl.when(k == 0)
        def _():
            first = _dot_tn(a_ref[...], b_ref[...])
            acc[...] = first + _dot_tn(rest[0][...], rest[1][...]) if more else first

        @pl.when(k > 0)
        def _():
            acc[...] += _dot_tn(a_ref[...], b_ref[...])

        @pl.when(k == gt - 1)
        def _():
            o_ref[...] = acc[...].astype(o_ref.dtype)

    more_specs = [pl.BlockSpec((more[0].shape[0], tmo), lambda i, j, k: (0, i)),
                  pl.BlockSpec((more[1].shape[0], tn), lambda i, j, k: (0, j))] if more else []
    return pl.pallas_call(
        body, name=name, grid=(m // tmo, n // tn, gt),
        in_specs=[pl.BlockSpec((tt, tmo), lambda i, j, k: (k, i)), pl.BlockSpec((tt, tn), lambda i, j, k: (k, j))] + more_specs,
        out_specs=pl.BlockSpec((tmo, tn), lambda i, j, k: (i, j)), out_shape=jax.ShapeDtypeStruct((m, n), out_dtype),
        scratch_shapes=[pltpu.VMEM((tmo, tn), F32)], compiler_params=_cp(3, vmem_mib),
    )(a, b, *more)


def _row_spec(d):
    return pl.BlockSpec((1, d), lambda *_: (0, 0))


def _stat_spec(k, d):
    return pl.BlockSpec((k, 8, d), lambda *_: (0, 0, 0))


def _rope(z, cs, sn):
    first = (lax.broadcasted_iota(jnp.int32, (z.shape[0], 128), 1) % 32) < 16
    outs = []
    for j in range(z.shape[1] // 128):
        zc = z[:, 128 * j:128 * (j + 1)]
        partner = jnp.where(first, pltpu.roll(zc, 112, 1), pltpu.roll(zc, 16, 1))
        outs.append(zc * cs + partner * sn)
    return outs[0] if len(outs) == 1 else jnp.concatenate(outs, axis=1)


def _rope_tables(length, rotate, zero=0.0):
    if not rotate:
        return jnp.ones((length, 128), F32), jnp.zeros((length, 128), F32)
    half = HEAD_DIM // 2
    t = jnp.arange(length)
    row = (t // GRID_W).astype(F32) + zero
    col = (t % GRID_W).astype(F32)
    e = jnp.arange(128) % HEAD_DIM
    inv_freq = ROPE_BASE ** (-(2 * ((e % half) % (half // 2))).astype(F32) / half)
    pos = jnp.where(e[None, :] < half, row[:, None], col[:, None])
    ang = pos * inv_freq[None, :]
    first = ((e % half) < half // 2)[None, :]
    return jnp.cos(ang), jnp.where(first, -jnp.sin(ang), jnp.sin(ang))


def _mixer_in(name, x, nw, sh, sc, w_in, cos, sin, after):
    t, d = x.shape
    tm = _pick(t, 256, 128)
    n_in = w_in.shape[1]

    def body(x_ref, nw_ref, sh_ref, sc_ref, w_ref, cos_ref, sin_ref, after_ref, h_ref, q_ref, k_ref, v_ref, u_ref):
        xf = x_ref[...]
        r = lax.rsqrt(jnp.mean(xf * xf, axis=-1, keepdims=True) + EPS)
        hb = (((xf * r) * nw_ref[...]) * (1.0 + sc_ref[...]) + sh_ref[...]).astype(BF16)
        h_ref[...] = hb
        p = _dot(hb, w_ref[...])
        cs, sn = cos_ref[...], sin_ref[...]
        q_ref[...] = (_rope(p[:, :ATTN_WIDTH], cs, sn) * SCALE).astype(BF16)
        k_ref[...] = _rope(p[:, ATTN_WIDTH:ATTN_WIDTH + KV_WIDTH], cs, sn).astype(BF16)
        v_ref[...] = p[:, ATTN_WIDTH + KV_WIDTH:ATTN_WIDTH + 2 * KV_WIDTH].astype(BF16)
        u_ref[...] = p[:, ATTN_WIDTH + 2 * KV_WIDTH:]

    def tile(w):
        return pl.BlockSpec((tm, w), lambda i: (i, 0))

    return pl.pallas_call(
        body, name=name, grid=(t // tm,),
        in_specs=[tile(d), _row_spec(d), _row_spec(d), _row_spec(d), pl.BlockSpec((d, n_in), lambda i: (0, 0)),
                  tile(128), tile(128), ANY],
        out_specs=(tile(d), tile(ATTN_WIDTH), tile(KV_WIDTH), tile(KV_WIDTH), tile(POOL_WIDTH)),
        out_shape=(jax.ShapeDtypeStruct((t, d), BF16), jax.ShapeDtypeStruct((t, ATTN_WIDTH), BF16),
                   jax.ShapeDtypeStruct((t, KV_WIDTH), BF16), jax.ShapeDtypeStruct((t, KV_WIDTH), BF16),
                   jax.ShapeDtypeStruct((t, POOL_WIDTH), F32)),
        compiler_params=_cp(1),
    )(x, nw, sh, sc, w_in, cos, sin, after)


def _attn_specs(nb, n_ctx):
    def blk(w, f):
        return pl.BlockSpec((BLOCK, w), lambda n: (f(n), 0))

    prev = lambda n: jnp.maximum(n - 1, 0)
    cur = lambda n: n
    nxt = lambda n: jnp.minimum(n + 1, nb - 1)
    kv = [blk(KV_WIDTH, prev), blk(KV_WIDTH, cur), blk(KV_WIDTH, nxt)]
    ctx = pl.BlockSpec((n_ctx, KV_WIDTH), lambda n: (0, 0))
    return [pl.BlockSpec(memory_space=pltpu.SMEM), blk(ATTN_WIDTH, cur)] + kv + kv + [ctx, ctx]


def _attn_mask(n, length, n_keys):
    row = lax.broadcasted_iota(jnp.int32, (GROUP * BLOCK, n_keys), 0) % BLOCK
    col = lax.broadcasted_iota(jnp.int32, (GROUP * BLOCK, n_keys), 1)
    kpos = (n - 1) * BLOCK + col
    return ((jnp.abs(col - BLOCK - row) <= BLOCK) & (kpos >= 0) & (kpos < length)) | (col >= 3 * BLOCK)


def _group_rows(block, g):
    return jnp.concatenate([block[:, HEAD_DIM * h:HEAD_DIM * (h + 1)] for h in range(GROUP * g, GROUP * (g + 1))], axis=0)


def _group_sink(sink_ref, g):
    head = lax.broadcasted_iota(jnp.int32, (GROUP * BLOCK, 1), 0) // BLOCK
    out = jnp.full((GROUP * BLOCK, 1), sink_ref[0, GROUP * g], F32)
    for j in range(1, GROUP):
        out = jnp.where(head == j, sink_ref[0, GROUP * g + j], out)
    return out


def _attn_fwd(q, k, v, kc, vc, sink):
    length = q.shape[0]
    nb = length // BLOCK
    n_ctx = kc.shape[0]
    n_keys = 3 * BLOCK + n_ctx

    def body(sink_ref, q_ref, kp, k0, kn, vp, v0, vn, kc_ref, vc_ref, o_ref, p_ref):
        n = pl.program_id(0)
        valid = _attn_mask(n, length, n_keys)
        qb = q_ref[...]
        kall = jnp.concatenate([kp[...], k0[...], kn[...], kc_ref[...]], axis=0)
        vall = jnp.concatenate([vp[...], v0[...], vn[...], vc_ref[...]], axis=0)
        outs = []
        for g in range(N_KV_HEADS):
            lanes = slice(HEAD_DIM * g, HEAD_DIM * (g + 1))
            s = jnp.where(valid, _dot_nt(_group_rows(qb, g), kall[:, lanes]), NEG_INF)
            sk = _group_sink(sink_ref, g)
            m = jnp.maximum(jnp.max(s, axis=-1, keepdims=True), sk)
            e = jnp.exp(s - m)
            e_sink = jnp.exp(sk - m)
            inv = 1.0 / (jnp.sum(e, axis=-1, keepdims=True) + e_sink)
            p_ref[0, g, :, :n_keys] = (e * inv).astype(BF16)
            p_ref[0, g, :, n_keys:] = jnp.broadcast_to(e_sink * inv, (GROUP * BLOCK, 128)).astype(BF16)
            o = _dot(p_ref[0, g, :, :n_keys], vall[:, lanes])
            outs += [o[BLOCK * j:BLOCK * (j + 1)] for j in range(GROUP)]
        o_ref[...] = jnp.concatenate(outs, axis=1).astype(BF16)

    return pl.pallas_call(
        body, name="attn_fwd", grid=(nb,), in_specs=_attn_specs(nb, n_ctx),
        out_specs=(pl.BlockSpec((BLOCK, ATTN_WIDTH), lambda n: (n, 0)),
                   pl.BlockSpec((1, N_KV_HEADS, GROUP * BLOCK, n_keys + 128), lambda n: (n, 0, 0, 0))),
        out_shape=(jax.ShapeDtypeStruct((length, ATTN_WIDTH + POOL_WIDTH), BF16),
                   jax.ShapeDtypeStruct((nb, N_KV_HEADS, GROUP * BLOCK, n_keys + 128), BF16)), compiler_params=_cp(1),
    )(sink, q, k, k, k, v, v, v, kc, vc)


def _attn_bwd(q, k, v, kc, vc, dmix, probs, cos, sin):
    length = q.shape[0]
    nb = length // BLOCK
    n_ctx = kc.shape[0]
    n_keys = 3 * BLOCK + n_ctx

    def body(q_ref, kp, k0, kn, vp, v0, vn, kc_ref, vc_ref, do_ref, p_ref, cos_ref, sin_ref,
             dq_ref, dkp_ref, dvp_ref, dkc_ref, dvc_ref, dsink_ref):
        n = pl.program_id(0)

        @pl.when(n == 0)
        def _():
            dkc_ref[...] = jnp.zeros_like(dkc_ref)
            dvc_ref[...] = jnp.zeros_like(dvc_ref)
            dsink_ref[...] = jnp.zeros_like(dsink_ref)

        qb, dob = q_ref[...], do_ref[...]
        kall = jnp.concatenate([kp[...], k0[...], kn[...], kc_ref[...]], axis=0)
        vall = jnp.concatenate([vp[...], v0[...], vn[...], vc_ref[...]], axis=0)
        srow = lax.broadcasted_iota(jnp.int32, (8, 128), 0)
        slane = lax.broadcasted_iota(jnp.int32, (8, 128), 1)
        dqs, dks, dvs = [], [], []
        dsink = jnp.zeros((8, 128), F32)
        for g in range(N_KV_HEADS):
            lanes = slice(HEAD_DIM * g, HEAD_DIM * (g + 1))
            kg, vg = kall[:, lanes], vall[:, lanes]
            qg, dog = _group_rows(qb, g), _group_rows(dob, g)
            pb = p_ref[0, g, :, :n_keys]
            p = pb.astype(F32)
            dp = _dot_nt(dog, vg)
            delta = jnp.sum(p * dp, axis=-1, keepdims=True)
            ds = (p * (dp - delta)).astype(BF16)
            dq = _dot(ds, kg) * SCALE
            dqs += [dq[BLOCK * j:BLOCK * (j + 1)] for j in range(GROUP)]
            dks.append(_dot_tn(ds, qg))
            dvs.append(_dot_tn(pb, dog))
            d_sink = p_ref[0, g, :, n_keys:].astype(F32)[:, :1] * delta
            for j in range(GROUP):
                total = -jnp.sum(d_sink[BLOCK * j:BLOCK * (j + 1)], axis=0, keepdims=True)
                dsink = dsink + jnp.where((srow == 0) & (slane == GROUP * g + j), total, 0.0)
        dq_ref[...] = _rope(jnp.concatenate(dqs, axis=1), cos_ref[...], -sin_ref[...]).astype(BF16)
        dk = jnp.concatenate(dks, axis=1)
        dv = jnp.concatenate(dvs, axis=1)
        for j in range(3):
            dkp_ref[0, j] = dk[BLOCK * j:BLOCK * (j + 1)]
            dvp_ref[0, j] = dv[BLOCK * j:BLOCK * (j + 1)]
        dkc_ref[...] += dk[3 * BLOCK:]
        dvc_ref[...] += dv[3 * BLOCK:]
        dsink_ref[...] += dsink

    part = pl.BlockSpec((1, 3, BLOCK, KV_WIDTH), lambda n: (n, 0, 0, 0))
    ctx = pl.BlockSpec((n_ctx, KV_WIDTH), lambda n: (0, 0))
    return pl.pallas_call(
        body, name="attn_bwd", grid=(nb,),
        in_specs=_attn_specs(nb, n_ctx)[1:] + [pl.BlockSpec((BLOCK, ATTN_WIDTH), lambda n: (n, 0)),
                                           pl.BlockSpec((1,) + probs.shape[1:], lambda n: (n, 0, 0, 0)),
                                           pl.BlockSpec((BLOCK, 128), lambda n: (n, 0)), pl.BlockSpec((BLOCK, 128), lambda n: (n, 0))],
        out_specs=(pl.BlockSpec((BLOCK, ATTN_WIDTH), lambda n: (n, 0)), part, part, ctx, ctx,
                   pl.BlockSpec((8, 128), lambda n: (0, 0))),
        out_shape=(jax.ShapeDtypeStruct((length, ATTN_WIDTH), BF16),
                   jax.ShapeDtypeStruct((nb, 3, BLOCK, KV_WIDTH), F32), jax.ShapeDtypeStruct((nb, 3, BLOCK, KV_WIDTH), F32),
                   jax.ShapeDtypeStruct((n_ctx, KV_WIDTH), F32), jax.ShapeDtypeStruct((n_ctx, KV_WIDTH), F32),
                   jax.ShapeDtypeStruct((8, 128), F32)),
        compiler_params=_cp(1),
    )(q, k, k, k, v, v, v, kc, vc, dmix, probs, cos, sin)


def _sum_dkv(dkp, dvp, cos, sin, after):
    nb = dkp.shape[0]

    def body(dka, dkb, dkc, dva, dvb, dvc, cos_ref, sin_ref, after_ref, o_ref):
        n = pl.program_id(0)
        has_next = (n + 1 < nb).astype(F32)
        has_prev = (n > 0).astype(F32)
        dk = dka[0, 0] * has_next + dkb[0, 0] + dkc[0, 0] * has_prev
        dv = dva[0, 0] * has_next + dvb[0, 0] + dvc[0, 0] * has_prev
        o_ref[:, :KV_WIDTH] = _rope(dk, cos_ref[...], -sin_ref[...]).astype(BF16)
        o_ref[:, KV_WIDTH:] = dv.astype(BF16)

    def part(slot, f):
        return pl.BlockSpec((1, 1, BLOCK, KV_WIDTH), lambda n: (f(n), slot, 0, 0))

    parts = [part(0, lambda n: jnp.minimum(n + 1, nb - 1)), part(1, lambda n: n), part(2, lambda n: jnp.maximum(n - 1, 0))]
    tile = pl.BlockSpec((BLOCK, 128), lambda n: (n, 0))
    return pl.pallas_call(
        body, name="sum_dkv", grid=(nb,), in_specs=parts + parts + [tile, tile, ANY],
        out_specs=pl.BlockSpec((BLOCK, 2 * KV_WIDTH), lambda n: (n, 0)),
        out_shape=jax.ShapeDtypeStruct((nb * BLOCK, 2 * KV_WIDTH), BF16), compiler_params=_cp(1),
    )(dkp, dkp, dkp, dvp, dvp, dvp, cos, sin, after)


def _shift_rows(e, s):
    n = e.shape[0]
    return e if s % n == 0 else pltpu.roll(e, (-s) % n, 0)


def _window_sum(e, w, first):
    s, n = e, 1
    while n < w:
        s = s + _shift_rows(s, n)
        n *= 2
    return _shift_rows(s, first)


def _pool_geometry(i, tm, length):
    pos = i * tm - HALO + lax.broadcasted_iota(jnp.int32, (tm + 2 * HALO, 1), 0)
    inside = (pos >= 0) & (pos < length)
    inv_counts = []
    for w in POOL_WINDOWS:
        lo = jnp.clip(pos - w // 2, 0, length)
        hi = jnp.clip(pos - w // 2 + w, 0, length)
        inv_counts.append(1.0 / jnp.maximum(hi - lo, 1).astype(F32))
    return inside, inv_counts


def _halo_specs(tm, width, length, col=0):
    per = tm // HALO
    last = length // HALO - 1
    return [pl.BlockSpec((HALO, width), lambda i: (jnp.maximum(i * per - 1, 0), col)),
            pl.BlockSpec((tm, width), lambda i: (i, col)),
            pl.BlockSpec((HALO, width), lambda i: (jnp.minimum((i + 1) * per, last), col))]


def _pooled(ext, inv_counts, tm):
    outs = []
    for g, w in enumerate(POOL_WINDOWS):
        e = ext[:, POOL_GROUP_DIM * g:POOL_GROUP_DIM * (g + 1)]
        mean = _window_sum(e, w, -(w // 2)) * inv_counts[g]
        outs.append((mean - e)[HALO:HALO + tm])
    return outs


def _pool_fwd(u, pool_w, pool_scale, mix):
    length = u.shape[0]
    tm = _pick(length, 256, 128)

    def body(up, u0, un, w_ref, sc_ref, mix_ref, o_ref):
        inside, inv_counts = _pool_geometry(pl.program_id(0), tm, length)
        ext = jnp.where(inside, jnp.concatenate([up[...], u0[...], un[...]], axis=0), 0.0)
        pooled = _pooled(ext, inv_counts, tm)
        mixed = [_dot(pooled[g].astype(BF16), w_ref[g]) for g in range(len(POOL_WINDOWS))]
        o_ref[...] = (jnp.concatenate(mixed, axis=1) * sc_ref[...]).astype(BF16)

    return pl.pallas_call(
        body, name="pool_fwd", grid=(length // tm,),
        in_specs=_halo_specs(tm, POOL_WIDTH, length) + [pl.BlockSpec(pool_w.shape, lambda i: (0, 0, 0)), _row_spec(POOL_WIDTH), ANY],
        out_specs=pl.BlockSpec((tm, POOL_WIDTH), lambda i: (i, 1)),
        out_shape=jax.ShapeDtypeStruct(mix.shape, BF16), input_output_aliases={5: 0}, compiler_params=_cp(1),
    )(u, u, u, pool_w, pool_scale, mix)


def _pool_bwd(u, dmix, pool_w, pool_scale, after):
    length = u.shape[0]
    tm = _pick(length, 256, 128)
    n_g = len(POOL_WINDOWS)

    def body(up, u0, un, dp_, d0, dn_, w_ref, sc_ref, after_ref, du_ref, dw_ref, dsc_ref):
        i = pl.program_id(0)

        @pl.when(i == 0)
        def _():
            dw_ref[...] = jnp.zeros_like(dw_ref)
            dsc_ref[...] = jnp.zeros_like(dsc_ref)

        inside, inv_counts = _pool_geometry(i, tm, length)
        ext = jnp.where(inside, jnp.concatenate([up[...], u0[...], un[...]], axis=0), 0.0)
        dext = jnp.where(inside, jnp.concatenate([dp_[...], d0[...], dn_[...]], axis=0).astype(F32), 0.0)
        dmixed = (dext * sc_ref[...]).astype(BF16)
        pooled = _pooled(ext, inv_counts, tm)
        dus, dscs = [], []
        for g, w in enumerate(POOL_WINDOWS):
            lanes = slice(POOL_GROUP_DIM * g, POOL_GROUP_DIM * (g + 1))
            dpooled = _dot_nt(dmixed[:, lanes], w_ref[g])
            spread = _window_sum(dpooled * inv_counts[g], w, -(w // 2 - 1))
            dus.append((spread - dpooled)[HALO:HALO + tm])
            pb = pooled[g].astype(BF16)
            dw_ref[g] += _dot_tn(pb, dmixed[HALO:HALO + tm, lanes])
            prod = dext[HALO:HALO + tm, lanes] * _dot(pb, w_ref[g])
            dscs.append(_fold8(prod))
        du_ref[...] = jnp.concatenate(dus, axis=1).astype(BF16)
        dsc_ref[...] += jnp.concatenate(dscs, axis=1)

    return pl.pallas_call(
        body, name="pool_bwd", grid=(length // tm,),
        in_specs=_halo_specs(tm, POOL_WIDTH, length) + _halo_specs(tm, POOL_WIDTH, length, col=1)
        + [pl.BlockSpec(pool_w.shape, lambda i: (0, 0, 0)), _row_spec(POOL_WIDTH), ANY],
        out_specs=(pl.BlockSpec((tm, POOL_WIDTH), lambda i: (i, 0)), pl.BlockSpec((n_g, POOL_GROUP_DIM, POOL_GROUP_DIM), lambda i: (0, 0, 0)),
                   pl.BlockSpec((8, POOL_WIDTH), lambda i: (0, 0))),
        out_shape=(jax.ShapeDtypeStruct((length, POOL_WIDTH), BF16), jax.ShapeDtypeStruct((n_g, POOL_GROUP_DIM, POOL_GROUP_DIM), F32),
                   jax.ShapeDtypeStruct((8, POOL_WIDTH), F32)),
        compiler_params=_cp(1),
    )(u, u, u, dmix, dmix, dmix, pool_w, pool_scale, after)


def _mixer_out(mix, w_out, x, g_a, nmw, sh_m, sc_m, after):
    t, d = x.shape

    def epi(mo, rs, tiles, vecs, outs, st_ref, live):
        ga, nw, sh, sc = vecs
        x1_ref, mo_ref, hm_ref = outs
        x1 = tiles[0][rs, :] + ga[...] * mo
        x1_ref[rs, :] = x1
        mo_ref[rs, :] = mo.astype(BF16)
        r = lax.rsqrt(jnp.mean(x1 * x1, axis=-1, keepdims=True) + EPS)
        hm_ref[rs, :] = (((x1 * r) * nw[...]) * (1.0 + sc[...]) + sh[...]).astype(BF16)

    return _mm_deferred("mixer_out", mix, w_out, nt=False, tm=_pick(t, 256, 128), epi=epi, tiles=(x,), vecs=(g_a, nmw, sh_m, sc_m),
                        out_tiles=(jax.ShapeDtypeStruct((t, d), F32), jax.ShapeDtypeStruct((t, d), BF16), jax.ShapeDtypeStruct((t, d), BF16)),
                        n_stats=1, after=after)[:3]


def _mlp_up(hm, w_up):
    t, d = hm.shape
    tm = _pick(t, 1024, 512, 256, 128)
    tn = 2048

    def epi(acc, ex, outs):
        outs[0][...] = jnp.square(jnp.maximum(acc[...], 0.0)).astype(BF16)

    return _mm("mlp_up", hm, w_up, nt=False, tm=tm, tn=tn, tk=d, epi=epi,
               out_shape=(jax.ShapeDtypeStruct((t, w_up.shape[1]), BF16),),
               out_specs=(pl.BlockSpec((tm, tn), lambda j, i, k: (i, j)),))[0]


def _mlp_down_loss(act, w_down, x1, target, g_m, fw, after):
    t, d = x1.shape

    def epi(dnv, rs, tiles, vecs, outs, st_ref, live):
        x1_ref, t_ref = tiles
        gm, fw_ref = vecs
        dx2_ref, ddn_ref = outs
        x2 = x1_ref[rs, :] + gm[...] * dnv
        r = lax.rsqrt(jnp.mean(x2 * x2, axis=-1, keepdims=True) + EPS)
        xh = x2 * r
        diff = xh * fw_ref[...] - t_ref[rs, :]
        dy = diff * (1.0 / d)
        dxh = dy * fw_ref[...]
        dx2 = r * (dxh - xh * jnp.mean(dxh * xh, axis=-1, keepdims=True))
        dx2_ref[rs, :] = dx2
        ddn_ref[rs, :] = (dx2 * gm[...]).astype(BF16)
        st_ref[0] += jnp.where(live, _fold8(diff * diff), 0.0)
        st_ref[1] += jnp.where(live, _fold8(dy * xh), 0.0)
        st_ref[2] += jnp.where(live, _fold8(dx2 * dnv), 0.0)

    return _mm_k_deferred("mlp_down_loss", act, w_down, nt=False, tm=_pick(t, 512, 256), tk=_pick(act.shape[1], 2048), epi=epi,
                          tiles=(x1, target), vecs=(g_m, fw), n_stats=3, after=after,
                          out_tiles=(jax.ShapeDtypeStruct((t, d), F32), jax.ShapeDtypeStruct((t, d), BF16)))


def _mlp_dx(dup, w_up, x1, dx2, mo, nmw, sc_m, g_a, after):
    t, d = x1.shape

    def epi(dh, rs, tiles, vecs, outs, st_ref, live):
        x1_ref, dx2_ref, mo_ref = tiles
        nw, sc, ga = vecs
        dx1_ref, dmi_ref = outs
        dx1 = _norm_bwd_rows(dh, x1_ref[rs, :], nw[...], sc[...], st_ref) + dx2_ref[rs, :]
        dx1_ref[rs, :] = dx1
        dmi_ref[rs, :] = (dx1 * ga[...]).astype(BF16)
        st_ref[3] += jnp.where(live, _fold8(dx1 * mo_ref[rs, :].astype(F32)), 0.0)

    return _mm_k_deferred("mlp_dx", dup, w_up, nt=True, tm=_pick(t, 512, 256), tk=_pick(dup.shape[1], 2048), epi=epi,
                          tiles=(x1, dx2, mo), vecs=(nmw, sc_m, g_a), n_stats=4, after=after,
                          out_tiles=(jax.ShapeDtypeStruct((t, d), F32), jax.ShapeDtypeStruct((t, d), BF16)))


def _mlp_dact(ddn, w_down, act):
    t, d = ddn.shape
    tm = _pick(t, 512, 256, 128)
    tn = 2048

    def epi(acc, ex, outs):
        outs[0][...] = (acc[...] * (2.0 * jnp.sqrt(ex[0][...]).astype(F32))).astype(BF16)

    tile = pl.BlockSpec((tm, tn), lambda j, i, k: (i, j))
    return _mm("mlp_dact", ddn, w_down, nt=True, tm=tm, tn=tn, tk=d, epi=epi, extras=(act,), extra_specs=[tile],
               out_shape=(jax.ShapeDtypeStruct(act.shape, BF16),), out_specs=(tile,))[0]


def _norm_bwd_rows(dh, xv, nw, sc, st_ref):
    r = lax.rsqrt(jnp.mean(xv * xv, axis=-1, keepdims=True) + EPS)
    xh = xv * r
    dy = dh * (1.0 + sc)
    st_ref[0] += _fold8(dh)
    st_ref[1] += _fold8(dh * (xh * nw))
    st_ref[2] += _fold8(dy * xh)
    dxh = dy * nw
    return r * (dxh - xh * jnp.mean(dxh * xh, axis=-1, keepdims=True))


def _mixer_dmix(dmi, w_out, after):
    t, d = dmi.shape
    tm = _pick(t, 512, 256, 128)

    def epi(acc, ex, outs):
        outs[0][...] = acc[...].astype(BF16)

    n = w_out.shape[0]
    return _mm("mixer_dmix", dmi, w_out, nt=True, tm=tm, tn=n, tk=d, epi=epi, after=after,
               out_shape=(jax.ShapeDtypeStruct((t, n), BF16),), out_specs=(pl.BlockSpec((tm, n), lambda j, i, k: (i, 0)),))[0]


def _mixer_dx(name, dp, w_in, x, dx1, naw, sc_a, after):
    t, d = x.shape

    def epi(dh, rs, tiles, vecs, outs, st_ref, live):
        x_ref, dx1_ref = tiles
        nw, sc = vecs
        outs[0][rs, :] = _norm_bwd_rows(dh, x_ref[rs, :], nw[...], sc[...], st_ref) + dx1_ref[rs, :]

    return _mm_deferred(name, dp, w_in, nt=True, tm=_pick(t, 256, 128), epi=epi, tiles=(x, dx1), vecs=(naw, sc_a),
                        out_tiles=(jax.ShapeDtypeStruct((t, d), F32),), n_stats=3, after=after, vmem_mib=56)


def _silu(v):
    return v / (1.0 + jnp.exp(-v))


def _ada_fwd(cond, w_ada, b_ada):
    d, n = w_ada.shape
    tn = 512

    def body(c_ref, w_ref, b_ref, o_ref):
        o_ref[...] = _dot(_silu(c_ref[...]).astype(BF16), w_ref[...].astype(BF16)) + b_ref[...]

    return pl.pallas_call(
        body, name="ada_fwd", grid=(n // tn,),
        in_specs=[pl.BlockSpec(cond.shape, lambda j: (0, 0)), pl.BlockSpec((d, tn), lambda j: (0, j)), pl.BlockSpec((1, tn), lambda j: (0, j))],
        out_specs=pl.BlockSpec((cond.shape[0], tn), lambda j: (0, j)), out_shape=jax.ShapeDtypeStruct((cond.shape[0], n), F32),
        compiler_params=_cp(1),
    )(cond, w_ada, b_ada)


def _adamw_math(w, g, m, v):
    m = ADAM_B1 * m + (1.0 - ADAM_B1) * g
    v = ADAM_B2 * v + (1.0 - ADAM_B2) * jnp.square(g)
    m_hat = m / (1.0 - ADAM_B1 ** ADAM_STEP)
    v_hat = v / (1.0 - ADAM_B2 ** ADAM_STEP)
    return -ADAM_LR * (m_hat / (jnp.sqrt(v_hat) + ADAM_EPS) + ADAM_WD * w), m, v


def _ada_bwd(cond, dm, w_ada, m_ada, v_ada):
    d, n = w_ada.shape
    tn = 256
    rows = cond.shape[0]

    def body(c_ref, dm_ref, w_ref, m_ref, v_ref, g_ref, dl_ref, nm_ref, nv_ref, pc_ref):
        @pl.when(pl.program_id(0) == 0)
        def _():
            pc_ref[...] = jnp.zeros_like(pc_ref)

        dmb = dm_ref[...].astype(BF16)
        w = w_ref[...]
        g = _dot_tn(_silu(c_ref[...]).astype(BF16), dmb)
        g_ref[...] = g
        dl_ref[...], nm_ref[...], nv_ref[...] = _adamw_math(w, g, m_ref[...], v_ref[...])
        pc_ref[...] += _dot_nt(dm_ref[8:16, :].astype(BF16), w.astype(BF16))

    tile = pl.BlockSpec((d, tn), lambda j: (0, j))
    like = jax.ShapeDtypeStruct((d, n), F32)
    return pl.pallas_call(
        body, name="ada_bwd", grid=(n // tn,),
        in_specs=[pl.BlockSpec((rows, d), lambda j: (0, 0)), pl.BlockSpec((rows, tn), lambda j: (0, j)), tile, tile, tile],
        out_specs=(tile, tile, tile, tile, pl.BlockSpec((8, d), lambda j: (0, 0))),
        out_shape=(like, like, like, like, jax.ShapeDtypeStruct((8, d), F32)), compiler_params=_cp(1),
    )(cond, dm, w_ada, m_ada, v_ada)


def _adamw(name, w, g, m, v, after=None):
    return _ew(name, lambda w_, g_, m_, v_: (g_,) + _adamw_math(w_, g_, m_, v_), [w, g, m, v], [F32, F32, F32, F32],
               g if after is None else after)


def _colsum(st):
    return jnp.sum(st, axis=1)


def kernel(x, c, ctx, c_ctx, norm_attn_w, norm_mlp_w, w_ada, b_ada, w_in, attn_sink, pool_w, pool_scale, w_out, w_mlp_up, w_mlp_down, final_norm_w, loss_target, m_c_ctx, m_norm_attn_w, m_norm_mlp_w, m_w_ada, m_b_ada, m_w_in, m_attn_sink, m_pool_w, m_pool_scale, m_w_out, m_w_mlp_up, m_w_mlp_down, m_final_norm_w, v_c_ctx, v_norm_attn_w, v_norm_mlp_w, v_w_ada, v_b_ada, v_w_in, v_attn_sink, v_pool_w, v_pool_scale, v_w_out, v_w_mlp_up, v_w_mlp_down, v_final_norm_w):
    length, d = x.shape[1], x.shape[2]
    n_ctx = ctx.shape[1]
    pos = (lax.axis_index("x"), lax.axis_index("y"), lax.axis_index("c"))
    me, chip = _dev_index(pos), _chip_index(pos)
    xs, tgt, cx = x.reshape(length, d), loss_target.reshape(length, d), ctx.reshape(n_ctx, d)
    n_ada = w_ada.shape[2]

    c_all = _allgather8("gather_c", jnp.pad(c, ((0, 7), (0, 0))))
    mixer_bigs = [_Big("col", w_in.shape[1:]), _Big("pool", pool_w.shape[1:]), _Big("row", w_out.shape[1:])]
    mlp_bigs = [_Big("col", w_mlp_up.shape[1:]), _Big("row", w_mlp_down.shape[1:])]
    placed = [_cast_place(f"place_{i}", b, s, c_all)[0] for i, (b, s) in enumerate(zip(mixer_bigs, [w_in[0], pool_w[0], w_out[0]]))]
    flight = _split("gather_mixer_ici", placed, _gather_ici_remote(mixer_bigs, 0))
    token, placed_mlp = flight[3], []
    for i, (b, s) in enumerate(zip(mlp_bigs, [w_mlp_up[0], w_mlp_down[0]])):
        p, token = _cast_place(f"place_mlp_{i}", b, s, token)
        placed_mlp.append(p)
    cos, sin = _rope_tables(length, True, token[0, 0])
    cond = jnp.concatenate([c_all[:, 0, :], jnp.pad(c_ctx[None, :], ((0, 7), (0, 0)))], axis=0) + 0.0 * cos[0, 0]
    b_shard = lax.dynamic_slice_in_dim(b_ada, chip * n_ada, n_ada, axis=1)
    mod_all = _allgather8("gather_mod", _ada_fwd(cond, w_ada[0], b_shard))
    mod = jnp.concatenate([mod_all[0], mod_all[2], mod_all[4], mod_all[6]], axis=1)
    mine = lax.dynamic_slice_in_dim(mod, me, 1, axis=0)
    sh_a, sc_a, g_a, sh_m, sc_m, g_m = [mine[:, d * i:d * (i + 1)] for i in range(6)]
    csh_a, csc_a = mod[8:9, :d], mod[8:9, d:2 * d]

    win_b, pw_b, wout_b = _exchange("gather_mixer_d2d", _join(flight, mod), [jax.ShapeDtypeStruct(b.full_shape, BF16) for b in mixer_bigs],
                                    _gather_d2d_remote(mixer_bigs, 3), aliases={0: 0, 1: 1, 2: 2})
    wout_b = wout_b.reshape(-1, d)
    flight = _split("gather_mlp_ici", placed_mlp, _gather_ici_remote(mlp_bigs, 0), after=pw_b)

    one, zero = _rope_tables(n_ctx, False)
    h, q, k, v, u = _mixer_in("mixer_in", xs, norm_attn_w, sh_a, sc_a, win_b, cos, sin, flight[3])
    hc, _, kc, vc, _ = _mixer_in("mixer_in_ctx", cx, norm_attn_w, csh_a, csc_a, win_b, one, zero, flight[3])
    attn, probs = _attn_fwd(q, k, v, kc, vc, attn_sink)
    mix = _pool_fwd(u, pw_b, pool_scale, attn)
    flight = _split("gather_mlp_d2d", _join(flight, mix), _gather_d2d_remote(mlp_bigs, 0))
    x1, mo, hm = _mixer_out(mix, wout_b, xs, g_a, norm_mlp_w, sh_m, sc_m, flight[3])
    wup_b, wdn_b = _join(flight, hm)
    wdn_b = wdn_b.reshape(-1, d)
    act = _mlp_up(hm, wup_b)
    dx2, ddn, st_loss = _mlp_down_loss(act, wdn_b, x1, tgt, g_m, final_norm_w[None, :], c)
    st_loss = _colsum(st_loss)

    tt = _pick(length, 2048, 1024, 512, 256, 128)
    g_wdn = _mm_tn("grad_w_down", act, ddn, BF16, tmo=1024, tn=d, tt=tt)
    dup = _mlp_dact(ddn, wdn_b, act)
    g_wup = _mm_tn("grad_w_up", hm, dup, BF16, tmo=d, tn=1024, tt=tt)
    empty = lambda shapes: [lax.empty(s.shape, s.dtype) for s in shapes]
    grads = [g_wup, g_wdn.reshape(mlp_bigs[1].full_shape)]
    flight = _split("reduce_mlp_d2d", grads + empty(_halves(mlp_bigs)), _reduce_d2d_remote(mlp_bigs))
    dx1, dmi, st_mlp = _mlp_dx(dup, wup_b, x1, dx2, mo, norm_mlp_w, sc_m, g_a, flight[3])
    st_mlp = _colsum(st_mlp)
    landed = _join(flight, dmi)
    mlp_chip = _chip_sums("mlp", mlp_bigs, landed[:2], landed[2:])
    flight = _split("reduce_mlp_ici", mlp_chip + empty(_thirds(mlp_bigs)), _reduce_ici_remote(mlp_bigs))
    g_wout = _mm_tn("grad_w_out", mix, dmi, BF16, tmo=1024, tn=d, tt=tt)
    dmix = _mixer_dmix(dmi, wout_b, flight[3])
    dq, dkp, dvp, dkc, dvc, dsink = _attn_bwd(q, k, v, kc, vc, dmix, probs, cos, sin)
    landed = _join(flight, dq)
    flight = _split("reduce_mlp_share", _piece_sums("mlp", mlp_bigs, landed[:2], landed[2:]), _share_remote(mlp_bigs, 0))
    du, g_pw, st_pool = _pool_bwd(u, dmix, pw_b, pool_scale, flight[3])
    g_mlp = _join(flight, du)

    wo_bigs, win_bigs = mixer_bigs[1:], mixer_bigs[:1]
    wo_chip = _reduce_to_chip("wo", wo_bigs, [g_pw.astype(BF16), g_wout.reshape(wo_bigs[1].full_shape)])
    flight = _split("reduce_wo_ici", wo_chip + empty(_thirds(wo_bigs)), _reduce_ici_remote(wo_bigs))
    dkv = _sum_dkv(dkp, dvp, cos, sin, flight[3])
    dkv_ctx = jnp.concatenate([dkc.astype(BF16), dvc.astype(BF16)], axis=1)
    at_kv, at_u = ATTN_WIDTH, ATTN_WIDTH + 2 * KV_WIDTH
    tt_in = _pick(length, 1024, 512, 256, 128)
    g_win = jnp.concatenate([_mm_tn("grad_w_in_q", h, dq, BF16, tmo=d, tn=ATTN_WIDTH, tt=tt_in),
                             _mm_tn("grad_w_in_kv", h, dkv, BF16, tmo=d, tn=2 * KV_WIDTH, tt=tt_in, more=(hc, dkv_ctx)),
                             _mm_tn("grad_w_in_u", h, du, BF16, tmo=d, tn=POOL_WIDTH, tt=tt_in)], axis=1)
    wo_landed = _join(flight, g_win)
    win_chip = _reduce_to_chip("win", win_bigs, [g_win])
    flight = _split("reduce_win_ici", win_chip + empty(_thirds(win_bigs)), _reduce_ici_remote(win_bigs))
    grad_x, st_mix = _mixer_dx("mixer_dx", [(dq, 0), (dkv, at_kv), (du, at_u)], win_b, xs, dx1, norm_attn_w, sc_a, flight[3])
    _, st_ctx = _mixer_dx("mixer_dx_ctx", [(dkv_ctx, at_kv)], win_b, cx, jnp.zeros((n_ctx, d), F32), norm_attn_w, csc_a, flight[3])
    st_mix, st_ctx = _colsum(st_mix), _colsum(st_ctx)
    win_landed = _join(flight, grad_x)
    g_mixer = (_reduce_finish("win", win_bigs, win_landed[:1], win_landed[1:])
               + _reduce_finish("wo", wo_bigs, wo_landed[:2], wo_landed[2:]))

    zrow = jnp.zeros((d,), F32)
    pad = lambda a: jnp.pad(a, (0, d - a.shape[0]))
    mine_rows = [st_mix[0], st_mix[1], st_mlp[3], st_mlp[0], st_mlp[1], st_loss[2],
                 st_ctx[0], st_ctx[1],
                 st_mix[2] + st_ctx[2], st_mlp[2], st_loss[1],
                 pad(jnp.sum(st_pool, axis=0)), pad(dsink[0, :N_Q_HEADS]), st_loss[0]] + [zrow] * 2
    flight = _allgather8_split("gather_small", jnp.concatenate(mine_rows).reshape(len(mine_rows), d), me)
    res = {"w_mlp_up": tuple(_adamw("adamw_w_mlp_up", w_mlp_up, g_mlp[0].reshape(w_mlp_up.shape), m_w_mlp_up, v_w_mlp_up, flight[3]))}
    small_all = _join(flight, res["w_mlp_up"][1])[1]
    small = small_all[0]
    for i in range(1, 8):
        small = small + small_all[i]
    loss = 0.5 / d * jnp.sum(small[13])
    dm_rows = small_all[:, 0:6, :].reshape(8, 6 * d)
    dm_ctx = jnp.concatenate([small[6], small[7], jnp.zeros((4 * d,), F32)])[None, :]
    dm = jnp.concatenate([dm_rows, jnp.pad(dm_ctx, ((0, 7), (0, 0)))], axis=0)
    g_bada = jnp.sum(dm[:9], axis=0, keepdims=True)
    dm_shard = lax.dynamic_slice_in_dim(dm, chip * n_ada, n_ada, axis=1)
    g_wada, dl_wada, nm_wada, nv_wada, part_cctx = _ada_bwd(cond, dm_shard, w_ada[0], m_w_ada[0], v_w_ada[0])
    flight = _allgather8_split("gather_cctx", part_cctx, me)
    res["w_mlp_down"] = tuple(_adamw("adamw_w_mlp_down", w_mlp_down, g_mlp[1].reshape(w_mlp_down.shape), m_w_mlp_down,
                                     v_w_mlp_down, flight[3]))
    cctx_all = _join(flight, res["w_mlp_down"][1])[1]
    dsilu_in = cctx_all[0, 0] + cctx_all[2, 0] + cctx_all[4, 0] + cctx_all[6, 0]
    sig = 1.0 / (1.0 + jnp.exp(-c_ctx))
    g_cctx = dsilu_in * (sig * (1.0 + c_ctx * (1.0 - sig)))

    for nm, w_, g_, m_, v_ in zip(["w_in", "pool_w", "w_out"], [w_in, pool_w, w_out], g_mixer,
                                  [m_w_in, m_pool_w, m_w_out], [v_w_in, v_pool_w, v_w_out]):
        res[nm] = tuple(_adamw("adamw_" + nm, w_, g_.reshape(w_.shape), m_, v_))
    res["w_ada"] = (g_wada[None], dl_wada[None], nm_wada[None], nv_wada[None])

    def pack(cc, na, nm_, ba, sk, ps, fn):
        flat = [cc.reshape(-1), na.reshape(-1), nm_.reshape(-1), ba.reshape(-1), pad(sk.reshape(-1)), pad(ps.reshape(-1)),
                fn.reshape(-1), jnp.zeros((4 * d,), F32)]
        return jnp.concatenate(flat).reshape(16, d)

    w_s = pack(c_ctx, norm_attn_w, norm_mlp_w, b_ada, attn_sink, pool_scale, final_norm_w)
    m_s = pack(m_c_ctx, m_norm_attn_w, m_norm_mlp_w, m_b_ada, m_attn_sink, m_pool_scale, m_final_norm_w)
    v_s = pack(v_c_ctx, v_norm_attn_w, v_norm_mlp_w, v_b_ada, v_attn_sink, v_pool_scale, v_final_norm_w)
    g_s = pack(g_cctx, small[8], small[9], g_bada, small[12][:N_Q_HEADS], small[11][:POOL_WIDTH], small[10])
    small_out = _adamw("adamw_small", w_s, g_s, m_s, v_s)

    def unpack(p):
        return {"c_ctx": p[0], "norm_attn_w": p[1:2], "norm_mlp_w": p[2:3], "b_ada": p[3:9].reshape(1, 6 * d),
                "attn_sink": p[9:10, :N_Q_HEADS], "pool_scale": p[10:11, :POOL_WIDTH], "final_norm_w": p[11]}

    small_res = [unpack(p) for p in small_out]
    order = ["c_ctx", "norm_attn_w", "norm_mlp_w", "w_ada", "b_ada", "w_in", "attn_sink", "pool_w", "pool_scale",
             "w_out", "w_mlp_up", "w_mlp_down", "final_norm_w"]
    outs = [loss, grad_x.reshape(x.shape)]
    for kind in range(4):
        for nm in order:
            outs.append(res[nm][kind] if nm in res else small_res[kind][nm])
    return tuple(outs)
```

```python
import functools

import jax
import jax.numpy as jnp
from jax import lax
from jax.experimental import pallas as pl
from jax.experimental.pallas import tpu as pltpu

F32 = jnp.float32
BF16 = jnp.bfloat16
EPS = 1e-6
NEG_INF = -1e30
HEAD_DIM = 64
N_Q_HEADS = 16
N_KV_HEADS = 4
GROUP = N_Q_HEADS // N_KV_HEADS
ATTN_WIDTH = N_Q_HEADS * HEAD_DIM
KV_WIDTH = N_KV_HEADS * HEAD_DIM
POOL_WINDOWS = (2, 4, 8, 16)
POOL_GROUP_DIM = 256
POOL_WIDTH = len(POOL_WINDOWS) * POOL_GROUP_DIM
BLOCK = 128
GRID_W = 64
ROPE_BASE = 10000.0
SCALE = HEAD_DIM ** -0.5
HALO = 16
STRIP = 16
ADAM_LR, ADAM_B1, ADAM_B2, ADAM_EPS, ADAM_WD, ADAM_STEP = 0.001, 0.9, 0.999, 1e-08, 0.01, 10
MESH = pl.DeviceIdType.MESH
MIB = 1024 * 1024
ANY = pl.BlockSpec(memory_space=pl.ANY)


def _cp(n_axes, vmem_mib=48):
    return pltpu.CompilerParams(dimension_semantics=("arbitrary",) * n_axes, vmem_limit_bytes=vmem_mib * MIB)


def _fold8(v):
    s = v[0:8]
    for t in range(1, v.shape[0] // 8):
        s = s + v[8 * t:8 * t + 8]
    return s


def _dot(a, b):
    return jnp.dot(a, b, preferred_element_type=F32)


def _dot_nt(a, b):
    return lax.dot_general(a, b, (((1,), (1,)), ((), ())), preferred_element_type=F32)


def _dot_tn(a, b):
    return lax.dot_general(a, b, (((0,), (0,)), ((), ())), preferred_element_type=F32)


def _pick(n, *cands):
    for t in cands:
        if n % t == 0:
            return t
    return n


def _flip(pos, mask):
    return tuple((1 - v) if (mask >> (2 - i)) & 1 else v for i, v in enumerate(pos))


def _exchange(name, ins, out_shapes, remote, local=(), aliases=None):
    n_io = len(ins) + len(out_shapes)

    def body(*refs):
        io = refs[:n_io]
        send_sems, recv_sems, local_sems = refs[n_io:]
        me = (lax.axis_index("x"), lax.axis_index("y"), lax.axis_index("c"))

        def copy(i, sender):
            mask, src_fn, dst_fn = remote[i]
            return pltpu.make_async_remote_copy(
                src_ref=src_fn(io, sender), dst_ref=dst_fn(io, sender), send_sem=send_sems.at[i],
                recv_sem=recv_sems.at[i], device_id=_flip(sender, mask), device_id_type=MESH)

        own = [pltpu.make_async_copy(s(io, me), d(io, me), local_sems.at[i]) for i, (s, d) in enumerate(local)]
        for cp in own:
            cp.start()
        sends = [copy(i, me) for i in range(len(remote))]
        for cp in sends:
            cp.start()
        for i in range(len(remote)):
            copy(i, _flip(me, remote[i][0])).wait_recv()
        for cp in sends:
            cp.wait_send()
        for cp in own:
            cp.wait()

    return pl.pallas_call(
        body, name=name, out_shape=tuple(out_shapes),
        in_specs=[ANY] * len(ins), out_specs=tuple([ANY] * len(out_shapes)),
        scratch_shapes=[pltpu.SemaphoreType.DMA((len(remote),)), pltpu.SemaphoreType.DMA((len(remote),)),
                        pltpu.SemaphoreType.DMA((max(len(local), 1),))],
        input_output_aliases=aliases or {},
    )(*ins)


HBM = pl.BlockSpec(memory_space=pltpu.HBM)
SEM = pl.BlockSpec(memory_space=pltpu.SEMAPHORE)
EFFECT = pltpu.SideEffectType.DATAFLOW_SIDE_EFFECTING


def _split_copy(remote, i, io, send_sems, recv_sems, sender):
    mask, src_fn, dst_fn = remote[i]
    return pltpu.make_async_remote_copy(
        src_ref=src_fn(io, sender), dst_ref=dst_fn(io, sender), send_sem=send_sems.at[i],
        recv_sem=recv_sems.at[i], device_id=_flip(sender, mask), device_id_type=MESH)


def _exchange_start(name, bufs, remote, after=None):
    n, r = len(bufs), len(remote)
    more = [] if after is None else [after]

    def body(*refs):
        io, (send_sems, recv_sems, token) = refs[:n], refs[-3:]
        me = (lax.axis_index("x"), lax.axis_index("y"), lax.axis_index("c"))
        for i in range(r):
            _split_copy(remote, i, io, send_sems, recv_sems, me).start()
        token[...] = jnp.zeros_like(token)

    res = pl.pallas_call(
        body, name=name,
        out_shape=tuple(pltpu.HBM(b.shape, b.dtype) for b in bufs)
        + (pltpu.SemaphoreType.DMA((r,)), pltpu.SemaphoreType.DMA((r,)), jax.ShapeDtypeStruct((8, 128), F32)),
        in_specs=[HBM] * n + [ANY] * len(more), out_specs=tuple([HBM] * n) + (SEM, SEM, pl.BlockSpec(memory_space=pltpu.VMEM)),
        input_output_aliases={i: i for i in range(n)}, compiler_params=pltpu.CompilerParams(has_side_effects=EFFECT),
    )(*[pltpu.with_memory_space_constraint(b, pltpu.HBM) for b in bufs], *more)
    return list(res[:n]), res[n], res[n + 1], res[n + 2]


def _exchange_wait(name, bufs, send_sems, recv_sems, remote, after):
    n, r = len(bufs), len(remote)

    def body(*refs):
        io, ss, rs = refs[:n], refs[n], refs[n + 1]
        me = (lax.axis_index("x"), lax.axis_index("y"), lax.axis_index("c"))
        for i in range(r):
            _split_copy(remote, i, io, ss, rs, _flip(me, remote[i][0])).wait_recv()
        for i in range(r):
            _split_copy(remote, i, io, ss, rs, me).wait_send()

    return list(pl.pallas_call(
        body, name=name, out_shape=tuple(pltpu.HBM(b.shape, b.dtype) for b in bufs),
        in_specs=[HBM] * n + [SEM, SEM, ANY], out_specs=tuple([HBM] * n),
        input_output_aliases={i: i for i in range(n)}, compiler_params=pltpu.CompilerParams(has_side_effects=EFFECT),
    )(*bufs, send_sems, recv_sems, after))


def _my_c():
    return lax.axis_index("c")


def _my_chip():
    return 2 * lax.axis_index("x") + lax.axis_index("y")


def _dev_index(pos):
    return 4 * pos[0] + 2 * pos[1] + pos[2]


def _chip_index(pos):
    return 2 * pos[0] + pos[1]


def _allgather8(name, v):
    out = jax.ShapeDtypeStruct((8,) + v.shape, v.dtype)
    remote = [(mask, lambda io, pos: io[0], lambda io, pos: io[1].at[_dev_index(pos)]) for mask in range(1, 8)]
    local = [(lambda io, pos: io[0], lambda io, pos: io[1].at[_dev_index(pos)])]
    return _exchange(name, [v], [out], remote, local)[0]


class _Big:
    def __init__(self, kind, shard_shape):
        self.kind = kind
        self.shard_shape = tuple(shard_shape)
        if kind == "col":
            r, cs = shard_shape
            self.full_shape = (r, 4 * cs)
            self.piece_shape = (r // 2, cs)
            self.half_shape = (r // 2, 4 * cs)
        elif kind == "row":
            rs, c = shard_shape
            self.full_shape = (4, 2, rs // 2, c)
            self.piece_shape = (1, 1, rs // 2, c)
            self.half_shape = (4, 1, rs // 2, c)
        else:
            self.full_shape = (4, 256, 256)
            self.piece_shape = (2, 64, 256)
            self.half_shape = (2, 256, 256)

    def shard_as_pieces(self, a):
        return a.reshape((1, 2) + self.piece_shape[2:]) if self.kind == "row" else a

    def piece(self, ref, k, h):
        if self.kind == "col":
            r, cs = self.piece_shape
            return ref.at[pl.ds(h * r, r), pl.ds(k * cs, cs)]
        if self.kind == "row":
            return ref.at[pl.ds(k, 1), pl.ds(h, 1)]
        return ref.at[pl.ds(2 * h, 2), pl.ds(64 * k, 64)]

    def half_of_shard(self, ref, h):
        if self.kind == "col":
            return ref.at[pl.ds(h * self.piece_shape[0], self.piece_shape[0])]
        if self.kind == "row":
            return ref.at[:, pl.ds(h, 1)]
        return ref.at[pl.ds(2 * h, 2)]

    def half_of_full(self, ref, h):
        if self.kind == "col":
            return ref.at[pl.ds(h * self.half_shape[0], self.half_shape[0])]
        if self.kind == "row":
            return ref.at[:, pl.ds(h, 1)]
        return ref.at[pl.ds(2 * h, 2)]

    def piece_of_half(self, ref, k):
        if self.kind == "col":
            return ref.at[:, pl.ds(k * self.piece_shape[1], self.piece_shape[1])]
        if self.kind == "row":
            return ref.at[pl.ds(k, 1)]
        return ref.at[:, pl.ds(64 * k, 64)]


CHIP_MASKS = (4, 2, 6)


def _cast_place(name, big, shard, after):
    if big.kind == "col":
        r, cs = big.shard_shape
        tr = _pick(r, 512, 256, 128)
        src, grid, blk = shard, (r // tr,), (tr, cs)
        imap, omap = (lambda i: (i, 0)), (lambda i: (i, _my_chip()))
    elif big.kind == "row":
        rs, c = big.shard_shape
        tr = _pick(rs // 2, 256, 128)
        src, grid, blk = big.shard_as_pieces(shard), (2, rs // 2 // tr), (1, 1, tr, c)
        imap, omap = (lambda h, i: (0, h, i, 0)), (lambda h, i: (_my_chip(), h, i, 0))
    else:
        src, grid, blk = shard, (1,), big.shard_shape
        imap, omap = (lambda i: (0, 0, 0)), (lambda i: (0, _my_chip(), 0))

    def body(s_ref, after_ref, o_ref, token_ref):
        o_ref[...] = s_ref[...].astype(BF16)
        token_ref[...] = jnp.zeros_like(token_ref)

    return pl.pallas_call(
        body, name=name, grid=grid, in_specs=[pl.BlockSpec(blk, imap), ANY],
        out_specs=(pl.BlockSpec(blk, omap), pl.BlockSpec((8, 128), lambda *_: (0, 0))),
        out_shape=(jax.ShapeDtypeStruct(big.full_shape, BF16), jax.ShapeDtypeStruct((8, 128), F32)), compiler_params=_cp(len(grid)),
    )(src, after)


def _gather_ici_remote(bigs, off):
    remote = []
    for a, b in enumerate(bigs):
        for mask in CHIP_MASKS:
            def mine(io, p, a=a, b=b):
                return b.piece(io[off + a], _chip_index(p), p[2])
            remote.append((mask, mine, mine))
    return remote


def _gather_d2d_remote(bigs, off):
    remote = []
    for a, b in enumerate(bigs):
        for mask in CHIP_MASKS:
            def region(io, p, a=a, b=b, mask=mask):
                return b.piece(io[off + a], _chip_index(_flip(p, mask)), p[2])
            remote.append((1, region, region))
    return remote


def _ew(name, fn, ins, out_dtypes, after, rows_per_step=256):
    shape = ins[0].shape
    last = shape[-1]
    rows = 1
    for s in shape[:-1]:
        rows *= s
    ins2 = [a.reshape(rows, last) for a in ins]
    tr = _pick(rows, rows_per_step, 128, 64, 32, 16, 8)
    spec = pl.BlockSpec((tr, last), lambda i: (i, 0))

    def body(*refs):
        outs = fn(*[r[...] for r in refs[:len(ins)]])
        for o_ref, o in zip(refs[len(ins) + 1:], outs):
            o_ref[...] = o.astype(o_ref.dtype)

    outs = pl.pallas_call(
        body, name=name, grid=(rows // tr,), in_specs=[spec] * len(ins) + [ANY], out_specs=tuple([spec] * len(out_dtypes)),
        out_shape=tuple(jax.ShapeDtypeStruct((rows, last), d) for d in out_dtypes), compiler_params=_cp(1),
    )(*ins2, after)
    return [o.reshape(shape) for o in outs]


def _chip_sum(name, big, grad, from_sibling):
    if big.kind == "col":
        rh, w = big.half_shape
        tr = _pick(rh, 256, 128)
        nb = rh // tr
        grid, blk = (nb,), (tr, w)
        gmap, hmap = (lambda i: (_my_c() * nb + i, 0)), (lambda i: (i, 0))
    elif big.kind == "row":
        rh, w = big.half_shape[2:]
        tr = _pick(rh, 256, 128)
        grid, blk = (4, rh // tr), (1, 1, tr, w)
        gmap, hmap = (lambda k, i: (k, _my_c(), i, 0)), (lambda k, i: (k, 0, i, 0))
    else:
        grid, blk = (1,), big.half_shape
        gmap, hmap = (lambda i: (_my_c(), 0, 0)), (lambda i: (0, 0, 0))

    def body(g_ref, s_ref, o_ref):
        o_ref[...] = (g_ref[...].astype(F32) + s_ref[...].astype(F32)).astype(BF16)

    return pl.pallas_call(
        body, name=name, grid=grid, in_specs=[pl.BlockSpec(blk, gmap), pl.BlockSpec(blk, hmap)],
        out_specs=pl.BlockSpec(blk, hmap), out_shape=jax.ShapeDtypeStruct(big.half_shape, BF16), compiler_params=_cp(len(grid)),
    )(grad, from_sibling)


def _piece_sum(name, big, chip_sum, thirds):
    if big.kind == "col":
        rp, cs = big.piece_shape
        tr = _pick(rp, 256, 128)
        nb = rp // tr
        grid, blk, tblk = (nb,), (tr, cs), (1, tr, cs)
        smap, omap = (lambda i: (i, _my_chip())), (lambda i: (_my_c() * nb + i, 0))
        tmap = lambda j: (lambda i: (j, i, 0))
        out_shape = big.shard_shape
    elif big.kind == "row":
        rp, w = big.piece_shape[2:]
        tr = _pick(rp, 256, 128)
        grid, blk, tblk = (rp // tr,), (1, 1, tr, w), (1, 1, 1, tr, w)
        smap, omap = (lambda i: (_my_chip(), 0, i, 0)), (lambda i: (0, _my_c(), i, 0))
        tmap = lambda j: (lambda i: (j, 0, 0, i, 0))
        out_shape = (1, 2, rp, w)
    else:
        grid, blk, tblk = (1,), big.piece_shape, (1,) + big.piece_shape
        smap, omap = (lambda i: (0, _my_chip(), 0)), (lambda i: (_my_c(), 0, 0))
        tmap = lambda j: (lambda i: (j, 0, 0, 0))
        out_shape = big.shard_shape

    def body(s_ref, t0, t1, t2, o_ref):
        o_ref[...] = s_ref[...].astype(F32) + t0[0].astype(F32) + t1[0].astype(F32) + t2[0].astype(F32)

    return pl.pallas_call(
        body, name=name, grid=grid,
        in_specs=[pl.BlockSpec(blk, smap)] + [pl.BlockSpec(tblk, tmap(j)) for j in range(3)],
        out_specs=pl.BlockSpec(blk, omap), out_shape=jax.ShapeDtypeStruct(out_shape, F32), compiler_params=_cp(len(grid)),
    )(chip_sum, thirds, thirds, thirds)


def _split(name, bufs, remote, after=None):
    return _exchange_start(name + "_start", bufs, remote, after) + (remote, name)


def _join(handle, after):
    bufs, send_sems, recv_sems, _, remote, name = handle
    return _exchange_wait(name + "_wait", bufs, send_sems, recv_sems, remote, after)


def _allgather8_split(name, v, me):
    own = lax.dynamic_update_slice(lax.empty((8,) + v.shape, v.dtype), v[None], (me, 0, 0))
    remote = [(mask, lambda io, pos: io[0], lambda io, pos: io[1].at[_dev_index(pos)]) for mask in range(1, 8)]
    return _split(name, [v, own], remote)


def _reduce_d2d_remote(bigs):
    n = len(bigs)
    return [(1, lambda io, p, a=a, b=b: b.half_of_full(io[a], 1 - p[2]), lambda io, p, a=a: io[n + a])
            for a, b in enumerate(bigs)]


def _halves(bigs):
    return [jax.ShapeDtypeStruct(b.half_shape, BF16) for b in bigs]


def _chip_sums(tag, bigs, grads, from_sibling):
    return [_chip_sum(f"reduce_{tag}_chip_sum_{a}", b, g, r) for a, (b, g, r) in enumerate(zip(bigs, grads, from_sibling))]


def _reduce_to_chip(tag, bigs, grads):
    from_sibling = _exchange(f"reduce_{tag}_d2d", grads, _halves(bigs), _reduce_d2d_remote(bigs))
    return _chip_sums(tag, bigs, grads, from_sibling)


def _reduce_ici_remote(bigs):
    n = len(bigs)
    remote = []
    for a, b in enumerate(bigs):
        for j, mask in enumerate(CHIP_MASKS):
            remote.append((mask,
                           lambda io, p, a=a, b=b, mask=mask: b.piece_of_half(io[a], _chip_index(_flip(p, mask))),
                           lambda io, p, a=a, j=j: io[n + a].at[j]))
    return remote


def _thirds(bigs):
    return [jax.ShapeDtypeStruct((3,) + b.piece_shape, BF16) for b in bigs]


def _piece_sums(tag, bigs, chip_sum, from_chips):
    return [_piece_sum(f"reduce_{tag}_sum_{a}", b, s, r) for a, (b, s, r) in enumerate(zip(bigs, chip_sum, from_chips))]


def _share_remote(bigs, off):
    remote = []
    for a, b in enumerate(bigs):
        def mine(io, p, a=a, b=b):
            return b.half_of_shard(io[off + a], p[2])
        remote.append((1, mine, mine))
    return remote


def _reduce_finish(tag, bigs, chip_sum, from_chips):
    n = len(bigs)
    placed = _piece_sums(tag, bigs, chip_sum, from_chips)
    out = _exchange(f"reduce_{tag}_share_d2d", placed, [jax.ShapeDtypeStruct(p.shape, F32) for p in placed],
                    _share_remote(bigs, n), aliases={a: a for a in range(n)})
    return [o.reshape(b.shard_shape) for o, b in zip(out, bigs)]


def _mm(name, a, b, *, nt, tm, tn, tk, epi, extras=(), extra_specs=(), out_shape, out_specs, after=None, vmem_mib=48):
    m, kdim = a.shape
    n = b.shape[0] if nt else b.shape[1]
    gm, gn, gk = m // tm, n // tn, kdim // tk
    a_spec = pl.BlockSpec((tm, tk), lambda j, i, k: (i, k))
    b_spec = pl.BlockSpec((tn, tk), lambda j, i, k: (j, k)) if nt else pl.BlockSpec((tk, tn), lambda j, i, k: (k, j))
    n_ex = len(extras)
    if after is not None:
        extras, extra_specs = tuple(extras) + (after,), list(extra_specs) + [ANY]

    def body(a_ref, b_ref, *rest):
        ex, outs, acc = rest[:n_ex], rest[len(extras):-1], rest[-1]
        dot = _dot_nt if nt else _dot
        if gk == 1:
            acc[...] = dot(a_ref[...], b_ref[...])
            epi(acc, ex, outs)
        else:
            k = pl.program_id(2)

            @pl.when(k == 0)
            def _():
                acc[...] = dot(a_ref[...], b_ref[...])

            @pl.when(k > 0)
            def _():
                acc[...] += dot(a_ref[...], b_ref[...])

            @pl.when(k == gk - 1)
            def _():
                epi(acc, ex, outs)

    return pl.pallas_call(
        body, name=name, grid=(gn, gm, gk), in_specs=[a_spec, b_spec, *extra_specs], out_specs=tuple(out_specs),
        out_shape=tuple(out_shape), scratch_shapes=[pltpu.VMEM((tm, tn), F32)], compiler_params=_cp(3, vmem_mib),
    )(a, b, *extras)


def _mm_deferred(name, a, b, *, nt, tm, epi, tiles, vecs, out_tiles, n_stats, after, vmem_mib=48):
    pieces = a if isinstance(a, (list, tuple)) else [(a, 0)]
    m = pieces[0][0].shape[0]
    n = b.shape[0] if nt else b.shape[1]
    gm = m // tm
    n_a, n_t, n_v, n_o = len(pieces), len(tiles), len(vecs), len(out_tiles)

    def body(*refs):
        a_refs, b_ref, rest = refs[:n_a], refs[n_a], refs[n_a + 1:]
        t_refs, v_refs = rest[:n_t], rest[n_t:n_t + n_v]
        o_refs, st_ref, acc0, acc1 = rest[n_t + n_v + 1:n_t + n_v + 1 + n_o], rest[-3], rest[-2], rest[-1]
        i = pl.program_id(0)

        def dot():
            if n_a == 1 and pieces[0][0].shape[1] == b.shape[1 if nt else 0]:
                return (_dot_nt if nt else _dot)(a_refs[0][...], b_ref[...])
            parts = [_dot_nt(r[...], b_ref[:, off:off + p.shape[1]]) for r, (p, off) in zip(a_refs, pieces)]
            return functools.reduce(lambda u, v: u + v, parts)

        @pl.when(i == 0)
        def _():
            acc1[...] = jnp.zeros_like(acc1)
            st_ref[...] = jnp.zeros_like(st_ref)

        def finish(prev):
            for r0 in range(0, tm, STRIP):
                rs = slice(r0, r0 + STRIP)
                epi(prev[rs, :], rs, t_refs, v_refs, o_refs, st_ref, i > 0)

        @pl.when((i % 2 == 0) & (i < gm))
        def _():
            acc0[...] = dot()
            finish(acc1)

        @pl.when((i % 2 == 1) & (i < gm))
        def _():
            acc1[...] = dot()
            finish(acc0)

        @pl.when(i == gm)
        def _():
            finish(acc1 if gm % 2 == 0 else acc0)

    prev = lambda i: (jnp.maximum(i - 1, 0), 0)
    tile = pl.BlockSpec((tm, n), prev)
    return pl.pallas_call(
        body, name=name, grid=(gm + 1,),
        in_specs=[pl.BlockSpec((tm, p.shape[1]), lambda i: (jnp.minimum(i, gm - 1), 0)) for p, _ in pieces]
        + [pl.BlockSpec(b.shape, lambda i: (0, 0))] + [tile] * n_t + [_row_spec(n)] * n_v + [ANY],
        out_specs=tuple([tile] * n_o) + (_stat_spec(n_stats, n),),
        out_shape=tuple(out_tiles) + (jax.ShapeDtypeStruct((n_stats, 8, n), F32),),
        scratch_shapes=[pltpu.VMEM((tm, n), F32), pltpu.VMEM((tm, n), F32)], compiler_params=_cp(1, vmem_mib),
    )(*[p for p, _ in pieces], b, *tiles, *vecs, after)


def _mm_k_deferred(name, a, b, *, nt, tm, tk, epi, tiles, vecs, out_tiles, n_stats, after, vmem_mib=56):
    m, kdim = a.shape
    n = b.shape[0] if nt else b.shape[1]
    gm, gk = m // tm, kdim // tk
    rows = tm // gk
    n_t, n_v, n_o = len(tiles), len(vecs), len(out_tiles)
    dot = _dot_nt if nt else _dot

    def body(a_ref, b_ref, *rest):
        t_refs, v_refs = rest[:n_t], rest[n_t:n_t + n_v]
        o_refs, st_ref, acc0, acc1 = rest[n_t + n_v + 1:n_t + n_v + 1 + n_o], rest[-3], rest[-2], rest[-1]
        i, k = pl.program_id(0), pl.program_id(1)

        @pl.when((i == 0) & (k == 0))
        def _():
            acc1[...] = jnp.zeros_like(acc1)
            st_ref[...] = jnp.zeros_like(st_ref)

        def finish(prev):
            for r0 in range(0, rows, STRIP):
                acc_rows = prev[pl.ds(pl.multiple_of(k * rows + r0, STRIP), STRIP), :]
                epi(acc_rows, slice(r0, r0 + STRIP), t_refs, v_refs, o_refs, st_ref, i > 0)

        def step(cur, prev):
            cur[...] = jnp.where(k > 0, cur[...], 0.0) + dot(a_ref[...], b_ref[...])
            finish(prev)

        @pl.when((i % 2 == 0) & (i < gm))
        def _():
            step(acc0, acc1)

        @pl.when((i % 2 == 1) & (i < gm))
        def _():
            step(acc1, acc0)

        @pl.when(i == gm)
        def _():
            finish(acc1 if gm % 2 == 0 else acc0)

    prev = lambda i, k: (jnp.where(i == 0, 0, (i - 1) * gk + k), 0)
    part = pl.BlockSpec((rows, n), prev)
    b_spec = pl.BlockSpec((n, tk), lambda i, k: (0, k)) if nt else pl.BlockSpec((tk, n), lambda i, k: (k, 0))
    return pl.pallas_call(
        body, name=name, grid=(gm + 1, gk),
        in_specs=[pl.BlockSpec((tm, tk), lambda i, k: (jnp.minimum(i, gm - 1), k)), b_spec]
        + [part] * n_t + [_row_spec(n)] * n_v + [ANY],
        out_specs=tuple([part] * n_o) + (_stat_spec(n_stats, n),),
        out_shape=tuple(out_tiles) + (jax.ShapeDtypeStruct((n_stats, 8, n), F32),),
        scratch_shapes=[pltpu.VMEM((tm, n), F32), pltpu.VMEM((tm, n), F32)], compiler_params=_cp(2, vmem_mib),
    )(a, b, *tiles, *vecs, after)


def _mm_tn(name, a, b, out_dtype, *, tmo, tn, tt, more=(), after=None, vmem_mib=56):
    t, m = a.shape
    n = b.shape[1]
    gt = t // tt
    wait_for = [] if after is None else [after]

    def body(a_ref, b_ref, *rest):
        o_ref, acc = rest[-2:]
        k = pl.program_id(2)

        @pl.when(k == 0)
        def _():
            first = _dot_tn(a_ref[...], b_ref[...])
            acc[...] = first + _dot_tn(rest[0][...], rest[1][...]) if more else first

        @pl.when(k > 0)
        def _():
            acc[...] += _dot_tn(a_ref[...], b_ref[...])

        @pl.when(k == gt - 1)
        def _():
            o_ref[...] = acc[...].astype(o_ref.dtype)

    more_specs = [pl.BlockSpec((more[0].shape[0], tmo), lambda i, j, k: (0, i)),
                  pl.BlockSpec((more[1].shape[0], tn), lambda i, j, k: (0, j))] if more else []
    return pl.pallas_call(
        body, name=name, grid=(m // tmo, n // tn, gt),
        in_specs=[pl.BlockSpec((tt, tmo), lambda i, j, k: (k, i)), pl.BlockSpec((tt, tn), lambda i, j, k: (k, j))] + more_specs
        + [ANY] * len(wait_for),
        out_specs=pl.BlockSpec((tmo, tn), lambda i, j, k: (i, j)), out_shape=jax.ShapeDtypeStruct((m, n), out_dtype),
        scratch_shapes=[pltpu.VMEM((tmo, tn), F32)], compiler_params=_cp(3, vmem_mib),
    )(a, b, *more, *wait_for)


def _row_spec(d):
    return pl.BlockSpec((1, d), lambda *_: (0, 0))


def _stat_spec(k, d):
    return pl.BlockSpec((k, 8, d), lambda *_: (0, 0, 0))


def _rope(z, cs, sn):
    first = (lax.broadcasted_iota(jnp.int32, (z.shape[0], 128), 1) % 32) < 16
    outs = []
    for j in range(z.shape[1] // 128):
        zc = z[:, 128 * j:128 * (j + 1)]
        partner = jnp.where(first, pltpu.roll(zc, 112, 1), pltpu.roll(zc, 16, 1))
        outs.append(zc * cs + partner * sn)
    return outs[0] if len(outs) == 1 else jnp.concatenate(outs, axis=1)


def _rope_tables(length, rotate, zero=0.0):
    if not rotate:
        return jnp.ones((length, 128), F32), jnp.zeros((length, 128), F32)
    half = HEAD_DIM // 2
    t = jnp.arange(length)
    row = (t // GRID_W).astype(F32) + zero
    col = (t % GRID_W).astype(F32)
    e = jnp.arange(128) % HEAD_DIM
    inv_freq = ROPE_BASE ** (-(2 * ((e % half) % (half // 2))).astype(F32) / half)
    pos = jnp.where(e[None, :] < half, row[:, None], col[:, None])
    ang = pos * inv_freq[None, :]
    first = ((e % half) < half // 2)[None, :]
    return jnp.cos(ang), jnp.where(first, -jnp.sin(ang), jnp.sin(ang))


def _mixer_in(name, x, nw, sh, sc, w_in, cos, sin, after):
    t, d = x.shape
    tm = _pick(t, 256, 128)
    n_in = w_in.shape[1]

    def body(x_ref, nw_ref, sh_ref, sc_ref, w_ref, cos_ref, sin_ref, after_ref, h_ref, q_ref, k_ref, v_ref, u_ref):
        xf = x_ref[...]
        r = lax.rsqrt(jnp.mean(xf * xf, axis=-1, keepdims=True) + EPS)
        hb = (((xf * r) * nw_ref[...]) * (1.0 + sc_ref[...]) + sh_ref[...]).astype(BF16)
        h_ref[...] = hb
        p = _dot(hb, w_ref[...])
        cs, sn = cos_ref[...], sin_ref[...]
        q_ref[...] = (_rope(p[:, :ATTN_WIDTH], cs, sn) * SCALE).astype(BF16)
        k_ref[...] = _rope(p[:, ATTN_WIDTH:ATTN_WIDTH + KV_WIDTH], cs, sn).astype(BF16)
        v_ref[...] = p[:, ATTN_WIDTH + KV_WIDTH:ATTN_WIDTH + 2 * KV_WIDTH].astype(BF16)
        u_ref[...] = p[:, ATTN_WIDTH + 2 * KV_WIDTH:]

    def tile(w):
        return pl.BlockSpec((tm, w), lambda i: (i, 0))

    return pl.pallas_call(
        body, name=name, grid=(t // tm,),
        in_specs=[tile(d), _row_spec(d), _row_spec(d), _row_spec(d), pl.BlockSpec((d, n_in), lambda i: (0, 0)),
                  tile(128), tile(128), ANY],
        out_specs=(tile(d), tile(ATTN_WIDTH), tile(KV_WIDTH), tile(KV_WIDTH), tile(POOL_WIDTH)),
        out_shape=(jax.ShapeDtypeStruct((t, d), BF16), jax.ShapeDtypeStruct((t, ATTN_WIDTH), BF16),
                   jax.ShapeDtypeStruct((t, KV_WIDTH), BF16), jax.ShapeDtypeStruct((t, KV_WIDTH), BF16),
                   jax.ShapeDtypeStruct((t, POOL_WIDTH), F32)),
        compiler_params=_cp(1),
    )(x, nw, sh, sc, w_in, cos, sin, after)


def _attn_specs(nb, n_ctx):
    def blk(w, f):
        return pl.BlockSpec((BLOCK, w), lambda n: (f(n), 0))

    prev = lambda n: jnp.maximum(jnp.minimum(n, nb - 1) - 1, 0)
    cur = lambda n: jnp.minimum(n, nb - 1)
    nxt = lambda n: jnp.minimum(n + 1, nb - 1)
    kv = [blk(KV_WIDTH, prev), blk(KV_WIDTH, cur), blk(KV_WIDTH, nxt)]
    ctx = pl.BlockSpec((n_ctx, KV_WIDTH), lambda n: (0, 0))
    return [pl.BlockSpec(memory_space=pltpu.SMEM), blk(ATTN_WIDTH, cur)] + kv + kv + [ctx, ctx]


def _attn_mask(n, length, n_keys):
    row = lax.broadcasted_iota(jnp.int32, (GROUP * BLOCK, n_keys), 0) % BLOCK
    col = lax.broadcasted_iota(jnp.int32, (GROUP * BLOCK, n_keys), 1)
    kpos = (n - 1) * BLOCK + col
    return ((jnp.abs(col - BLOCK - row) <= BLOCK) & (kpos >= 0) & (kpos < length)) | (col >= 3 * BLOCK)


def _group_rows(block, g):
    return jnp.concatenate([block[:, HEAD_DIM * h:HEAD_DIM * (h + 1)] for h in range(GROUP * g, GROUP * (g + 1))], axis=0)


def _group_sink(sink_ref, g):
    head = lax.broadcasted_iota(jnp.int32, (GROUP * BLOCK, 1), 0) // BLOCK
    out = jnp.full((GROUP * BLOCK, 1), sink_ref[0, GROUP * g], F32)
    for j in range(1, GROUP):
        out = jnp.where(head == j, sink_ref[0, GROUP * g + j], out)
    return out


def _attn_fwd(q, k, v, kc, vc, sink):
    length = q.shape[0]
    nb = length // BLOCK
    n_ctx = kc.shape[0]
    n_keys = 3 * BLOCK + n_ctx

    def body(sink_ref, q_ref, kp, k0, kn, vp, v0, vn, kc_ref, vc_ref, o_ref, p_ref):
        n = pl.program_id(0)
        valid = _attn_mask(n, length, n_keys)
        qb = q_ref[...]
        kall = jnp.concatenate([kp[...], k0[...], kn[...], kc_ref[...]], axis=0)
        vall = jnp.concatenate([vp[...], v0[...], vn[...], vc_ref[...]], axis=0)
        outs = []
        for g in range(N_KV_HEADS):
            lanes = slice(HEAD_DIM * g, HEAD_DIM * (g + 1))
            s = jnp.where(valid, _dot_nt(_group_rows(qb, g), kall[:, lanes]), NEG_INF)
            sk = _group_sink(sink_ref, g)
            m = jnp.maximum(jnp.max(s, axis=-1, keepdims=True), sk)
            e = jnp.exp(s - m)
            e_sink = jnp.exp(sk - m)
            inv = 1.0 / (jnp.sum(e, axis=-1, keepdims=True) + e_sink)
            p_ref[0, g, :, :n_keys] = (e * inv).astype(BF16)
            p_ref[0, g, :, n_keys:] = jnp.broadcast_to(e_sink * inv, (GROUP * BLOCK, 128)).astype(BF16)
            o = _dot(p_ref[0, g, :, :n_keys], vall[:, lanes])
            outs += [o[BLOCK * j:BLOCK * (j + 1)] for j in range(GROUP)]
        o_ref[...] = jnp.concatenate(outs, axis=1).astype(BF16)

    return pl.pallas_call(
        body, name="attn_fwd", grid=(nb,), in_specs=_attn_specs(nb, n_ctx),
        out_specs=(pl.BlockSpec((BLOCK, ATTN_WIDTH), lambda n: (n, 0)),
                   pl.BlockSpec((1, N_KV_HEADS, GROUP * BLOCK, n_keys + 128), lambda n: (n, 0, 0, 0))),
        out_shape=(jax.ShapeDtypeStruct((length, ATTN_WIDTH + POOL_WIDTH), BF16),
                   jax.ShapeDtypeStruct((nb, N_KV_HEADS, GROUP * BLOCK, n_keys + 128), BF16)), compiler_params=_cp(1),
    )(sink, q, k, k, k, v, v, v, kc, vc)


def _attn_bwd(q, k, v, kc, vc, dmix, probs, cos, sin):
    length = q.shape[0]
    nb = length // BLOCK
    n_ctx = kc.shape[0]
    n_keys = 3 * BLOCK + n_ctx

    def body(q_ref, kp, k0, kn, vp, v0, vn, kc_ref, vc_ref, do_ref, p_ref, cos_ref, sin_ref, cos_prev, sin_prev,
             dq_ref, dkv_ref, dkc_ref, dvc_ref, dsink_ref, done, ahead):
        n = pl.program_id(0)

        @pl.when(n == 0)
        def _():
            dkc_ref[...] = jnp.zeros_like(dkc_ref)
            dvc_ref[...] = jnp.zeros_like(dvc_ref)
            dsink_ref[...] = jnp.zeros_like(dsink_ref)
            done[...] = jnp.zeros_like(done)
            ahead[...] = jnp.zeros_like(ahead)

        def write_block(dkv):
            dkv_ref[:, :KV_WIDTH] = _rope(dkv[:, :KV_WIDTH], cos_prev[...], -sin_prev[...]).astype(BF16)
            dkv_ref[:, KV_WIDTH:] = dkv[:, KV_WIDTH:].astype(BF16)

        def query_block():
            qb, dob = q_ref[...], do_ref[...]
            kall = jnp.concatenate([kp[...], k0[...], kn[...], kc_ref[...]], axis=0)
            vall = jnp.concatenate([vp[...], v0[...], vn[...], vc_ref[...]], axis=0)
            srow = lax.broadcasted_iota(jnp.int32, (8, 128), 0)
            slane = lax.broadcasted_iota(jnp.int32, (8, 128), 1)
            dqs, dks, dvs = [], [], []
            dsink = jnp.zeros((8, 128), F32)
            for g in range(N_KV_HEADS):
                lanes = slice(HEAD_DIM * g, HEAD_DIM * (g + 1))
                kg, vg = kall[:, lanes], vall[:, lanes]
                qg, dog = _group_rows(qb, g), _group_rows(dob, g)
                pb = p_ref[0, g, :, :n_keys]
                p = pb.astype(F32)
                dp = _dot_nt(dog, vg)
                delta = jnp.sum(p * dp, axis=-1, keepdims=True)
                ds = (p * (dp - delta)).astype(BF16)
                dq = _dot(ds, kg) * SCALE
                dqs += [dq[BLOCK * j:BLOCK * (j + 1)] for j in range(GROUP)]
                dks.append(_dot_tn(ds, qg))
                dvs.append(_dot_tn(pb, dog))
                d_sink = p_ref[0, g, :, n_keys:].astype(F32)[:, :1] * delta
                for j in range(GROUP):
                    total = -jnp.sum(d_sink[BLOCK * j:BLOCK * (j + 1)], axis=0, keepdims=True)
                    dsink = dsink + jnp.where((srow == 0) & (slane == GROUP * g + j), total, 0.0)
            dq_ref[...] = _rope(jnp.concatenate(dqs, axis=1), cos_ref[...], -sin_ref[...]).astype(BF16)
            dkv = jnp.concatenate(dks + dvs, axis=1)
            write_block(done[...] + dkv[:BLOCK])
            done[...] = ahead[...] + dkv[BLOCK:2 * BLOCK]
            ahead[...] = dkv[2 * BLOCK:3 * BLOCK]
            dkc_ref[...] += dkv[3 * BLOCK:, :KV_WIDTH]
            dvc_ref[...] += dkv[3 * BLOCK:, KV_WIDTH:]
            dsink_ref[...] += dsink

        pl.when(n < nb)(query_block)

        @pl.when(n == nb)
        def _():
            write_block(done[...])

    here = lambda n: (jnp.minimum(n, nb - 1), 0)
    before = lambda n: (jnp.maximum(n - 1, 0), 0)
    ctx = pl.BlockSpec((n_ctx, KV_WIDTH), lambda n: (0, 0))
    return pl.pallas_call(
        body, name="attn_bwd", grid=(nb + 1,),
        in_specs=_attn_specs(nb, n_ctx)[1:] + [pl.BlockSpec((BLOCK, ATTN_WIDTH), here),
                                           pl.BlockSpec((1,) + probs.shape[1:], lambda n: (jnp.minimum(n, nb - 1), 0, 0, 0)),
                                           pl.BlockSpec((BLOCK, 128), here), pl.BlockSpec((BLOCK, 128), here),
                                           pl.BlockSpec((BLOCK, 128), before), pl.BlockSpec((BLOCK, 128), before)],
        out_specs=(pl.BlockSpec((BLOCK, ATTN_WIDTH), here), pl.BlockSpec((BLOCK, 2 * KV_WIDTH), before), ctx, ctx,
                   pl.BlockSpec((8, 128), lambda n: (0, 0))),
        out_shape=(jax.ShapeDtypeStruct((length, ATTN_WIDTH), BF16), jax.ShapeDtypeStruct((length, 2 * KV_WIDTH), BF16),
                   jax.ShapeDtypeStruct((n_ctx, KV_WIDTH), F32), jax.ShapeDtypeStruct((n_ctx, KV_WIDTH), F32),
                   jax.ShapeDtypeStruct((8, 128), F32)),
        scratch_shapes=[pltpu.VMEM((BLOCK, 2 * KV_WIDTH), F32), pltpu.VMEM((BLOCK, 2 * KV_WIDTH), F32)],
        compiler_params=_cp(1),
    )(q, k, k, k, v, v, v, kc, vc, dmix, probs, cos, sin, cos, sin)


def _shift_rows(e, s):
    n = e.shape[0]
    return e if s % n == 0 else pltpu.roll(e, (-s) % n, 0)


def _window_sum(e, w, first):
    s, n = e, 1
    while n < w:
        s = s + _shift_rows(s, n)
        n *= 2
    return _shift_rows(s, first)


def _pool_geometry(i, tm, length):
    pos = i * tm - HALO + lax.broadcasted_iota(jnp.int32, (tm + 2 * HALO, 1), 0)
    inside = (pos >= 0) & (pos < length)
    inv_counts = []
    for w in POOL_WINDOWS:
        lo = jnp.clip(pos - w // 2, 0, length)
        hi = jnp.clip(pos - w // 2 + w, 0, length)
        inv_counts.append(1.0 / jnp.maximum(hi - lo, 1).astype(F32))
    return inside, inv_counts


def _halo_specs(tm, width, length, col=0):
    per = tm // HALO
    last = length // HALO - 1
    return [pl.BlockSpec((HALO, width), lambda i: (jnp.maximum(i * per - 1, 0), col)),
            pl.BlockSpec((tm, width), lambda i: (i, col)),
            pl.BlockSpec((HALO, width), lambda i: (jnp.minimum((i + 1) * per, last), col))]


def _pooled(ext, inv_counts, tm):
    outs = []
    for g, w in enumerate(POOL_WINDOWS):
        e = ext[:, POOL_GROUP_DIM * g:POOL_GROUP_DIM * (g + 1)]
        mean = _window_sum(e, w, -(w // 2)) * inv_counts[g]
        outs.append((mean - e)[HALO:HALO + tm])
    return outs


def _pool_fwd(u, pool_w, pool_scale, mix):
    length = u.shape[0]
    tm = _pick(length, 256, 128)

    def body(up, u0, un, w_ref, sc_ref, mix_ref, o_ref):
        inside, inv_counts = _pool_geometry(pl.program_id(0), tm, length)
        ext = jnp.where(inside, jnp.concatenate([up[...], u0[...], un[...]], axis=0), 0.0)
        pooled = _pooled(ext, inv_counts, tm)
        mixed = [_dot(pooled[g].astype(BF16), w_ref[g]) for g in range(len(POOL_WINDOWS))]
        o_ref[...] = (jnp.concatenate(mixed, axis=1) * sc_ref[...]).astype(BF16)

    return pl.pallas_call(
        body, name="pool_fwd", grid=(length // tm,),
        in_specs=_halo_specs(tm, POOL_WIDTH, length) + [pl.BlockSpec(pool_w.shape, lambda i: (0, 0, 0)), _row_spec(POOL_WIDTH), ANY],
        out_specs=pl.BlockSpec((tm, POOL_WIDTH), lambda i: (i, 1)),
        out_shape=jax.ShapeDtypeStruct(mix.shape, BF16), input_output_aliases={5: 0}, compiler_params=_cp(1),
    )(u, u, u, pool_w, pool_scale, mix)


def _pool_bwd(u, dmix, pool_w, pool_scale, after):
    length = u.shape[0]
    tm = _pick(length, 256, 128)
    n_g = len(POOL_WINDOWS)

    def body(up, u0, un, dp_, d0, dn_, w_ref, sc_ref, after_ref, du_ref, dw_ref, dsc_ref):
        i = pl.program_id(0)

        @pl.when(i == 0)
        def _():
            dw_ref[...] = jnp.zeros_like(dw_ref)
            dsc_ref[...] = jnp.zeros_like(dsc_ref)

        inside, inv_counts = _pool_geometry(i, tm, length)
        ext = jnp.where(inside, jnp.concatenate([up[...], u0[...], un[...]], axis=0), 0.0)
        dext = jnp.where(inside, jnp.concatenate([dp_[...], d0[...], dn_[...]], axis=0).astype(F32), 0.0)
        dmixed = (dext * sc_ref[...]).astype(BF16)
        pooled = _pooled(ext, inv_counts, tm)
        dus, dscs = [], []
        for g, w in enumerate(POOL_WINDOWS):
            lanes = slice(POOL_GROUP_DIM * g, POOL_GROUP_DIM * (g + 1))
            dpooled = _dot_nt(dmixed[:, lanes], w_ref[g])
            spread = _window_sum(dpooled * inv_counts[g], w, -(w // 2 - 1))
            dus.append((spread - dpooled)[HALO:HALO + tm])
            pb = pooled[g].astype(BF16)
            dw_ref[g] += _dot_tn(pb, dmixed[HALO:HALO + tm, lanes])
            prod = dext[HALO:HALO + tm, lanes] * _dot(pb, w_ref[g])
            dscs.append(_fold8(prod))
        du_ref[...] = jnp.concatenate(dus, axis=1).astype(BF16)
        dsc_ref[...] += jnp.concatenate(dscs, axis=1)

    return pl.pallas_call(
        body, name="pool_bwd", grid=(length // tm,),
        in_specs=_halo_specs(tm, POOL_WIDTH, length) + _halo_specs(tm, POOL_WIDTH, length, col=1)
        + [pl.BlockSpec(pool_w.shape, lambda i: (0, 0, 0)), _row_spec(POOL_WIDTH), ANY],
        out_specs=(pl.BlockSpec((tm, POOL_WIDTH), lambda i: (i, 0)), pl.BlockSpec((n_g, POOL_GROUP_DIM, POOL_GROUP_DIM), lambda i: (0, 0, 0)),
                   pl.BlockSpec((8, POOL_WIDTH), lambda i: (0, 0))),
        out_shape=(jax.ShapeDtypeStruct((length, POOL_WIDTH), BF16), jax.ShapeDtypeStruct((n_g, POOL_GROUP_DIM, POOL_GROUP_DIM), F32),
                   jax.ShapeDtypeStruct((8, POOL_WIDTH), F32)),
        compiler_params=_cp(1),
    )(u, u, u, dmix, dmix, dmix, pool_w, pool_scale, after)


def _mixer_out(mix, w_out, x, g_a, nmw, sh_m, sc_m, after):
    t, d = x.shape

    def epi(mo, rs, tiles, vecs, outs, st_ref, live):
        ga, nw, sh, sc = vecs
        x1_ref, mo_ref, hm_ref = outs
        x1 = tiles[0][rs, :] + ga[...] * mo
        x1_ref[rs, :] = x1
        mo_ref[rs, :] = mo.astype(BF16)
        r = lax.rsqrt(jnp.mean(x1 * x1, axis=-1, keepdims=True) + EPS)
        hm_ref[rs, :] = (((x1 * r) * nw[...]) * (1.0 + sc[...]) + sh[...]).astype(BF16)

    return _mm_deferred("mixer_out", mix, w_out, nt=False, tm=_pick(t, 256, 128), epi=epi, tiles=(x,), vecs=(g_a, nmw, sh_m, sc_m),
                        out_tiles=(jax.ShapeDtypeStruct((t, d), F32), jax.ShapeDtypeStruct((t, d), BF16), jax.ShapeDtypeStruct((t, d), BF16)),
                        n_stats=1, after=after)[:3]


def _mlp_up(hm, w_up):
    t, d = hm.shape
    tm = _pick(t, 1024, 512, 256, 128)
    tn = 2048

    def epi(acc, ex, outs):
        outs[0][...] = jnp.square(jnp.maximum(acc[...], 0.0)).astype(BF16)

    return _mm("mlp_up", hm, w_up, nt=False, tm=tm, tn=tn, tk=d, epi=epi,
               out_shape=(jax.ShapeDtypeStruct((t, w_up.shape[1]), BF16),),
               out_specs=(pl.BlockSpec((tm, tn), lambda j, i, k: (i, j)),))[0]


def _mlp_down_loss(act, w_down, x1, target, g_m, fw, after):
    t, d = x1.shape

    def epi(dnv, rs, tiles, vecs, outs, st_ref, live):
        x1_ref, t_ref = tiles
        gm, fw_ref = vecs
        dx2_ref, ddn_ref = outs
        x2 = x1_ref[rs, :] + gm[...] * dnv
        r = lax.rsqrt(jnp.mean(x2 * x2, axis=-1, keepdims=True) + EPS)
        xh = x2 * r
        diff = xh * fw_ref[...] - t_ref[rs, :]
        dy = diff * (1.0 / d)
        dxh = dy * fw_ref[...]
        dx2 = r * (dxh - xh * jnp.mean(dxh * xh, axis=-1, keepdims=True))
        dx2_ref[rs, :] = dx2
        ddn_ref[rs, :] = (dx2 * gm[...]).astype(BF16)
        st_ref[0] += jnp.where(live, _fold8(diff * diff), 0.0)
        st_ref[1] += jnp.where(live, _fold8(dy * xh), 0.0)
        st_ref[2] += jnp.where(live, _fold8(dx2 * dnv), 0.0)

    return _mm_k_deferred("mlp_down_loss", act, w_down, nt=False, tm=_pick(t, 512, 256), tk=_pick(act.shape[1], 2048), epi=epi,
                          tiles=(x1, target), vecs=(g_m, fw), n_stats=3, after=after,
                          out_tiles=(jax.ShapeDtypeStruct((t, d), F32), jax.ShapeDtypeStruct((t, d), BF16)))


def _mlp_dx(dup, w_up, x1, dx2, mo, nmw, sc_m, g_a, after):
    t, d = x1.shape

    def epi(dh, rs, tiles, vecs, outs, st_ref, live):
        x1_ref, dx2_ref, mo_ref = tiles
        nw, sc, ga = vecs
        dx1_ref, dmi_ref = outs
        dx1 = _norm_bwd_rows(dh, x1_ref[rs, :], nw[...], sc[...], st_ref) + dx2_ref[rs, :]
        dx1_ref[rs, :] = dx1
        dmi_ref[rs, :] = (dx1 * ga[...]).astype(BF16)
        st_ref[3] += jnp.where(live, _fold8(dx1 * mo_ref[rs, :].astype(F32)), 0.0)

    return _mm_k_deferred("mlp_dx", dup, w_up, nt=True, tm=_pick(t, 512, 256), tk=_pick(dup.shape[1], 2048), epi=epi,
                          tiles=(x1, dx2, mo), vecs=(nmw, sc_m, g_a), n_stats=4, after=after,
                          out_tiles=(jax.ShapeDtypeStruct((t, d), F32), jax.ShapeDtypeStruct((t, d), BF16)))


def _mlp_dact(ddn, w_down, act):
    t, d = ddn.shape
    tm = _pick(t, 512, 256, 128)
    tn = 2048

    def epi(acc, ex, outs):
        outs[0][...] = (acc[...] * (2.0 * jnp.sqrt(ex[0][...]).astype(F32))).astype(BF16)

    tile = pl.BlockSpec((tm, tn), lambda j, i, k: (i, j))
    return _mm("mlp_dact", ddn, w_down, nt=True, tm=tm, tn=tn, tk=d, epi=epi, extras=(act,), extra_specs=[tile],
               out_shape=(jax.ShapeDtypeStruct(act.shape, BF16),), out_specs=(tile,))[0]


def _norm_bwd_rows(dh, xv, nw, sc, st_ref):
    r = lax.rsqrt(jnp.mean(xv * xv, axis=-1, keepdims=True) + EPS)
    xh = xv * r
    dy = dh * (1.0 + sc)
    st_ref[0] += _fold8(dh)
    st_ref[1] += _fold8(dh * (xh * nw))
    st_ref[2] += _fold8(dy * xh)
    dxh = dy * nw
    return r * (dxh - xh * jnp.mean(dxh * xh, axis=-1, keepdims=True))


def _mixer_dmix(dmi, w_out, after):
    t, d = dmi.shape
    tm = _pick(t, 512, 256, 128)

    def epi(acc, ex, outs):
        outs[0][...] = acc[...].astype(BF16)

    n = w_out.shape[0]
    return _mm("mixer_dmix", dmi, w_out, nt=True, tm=tm, tn=n, tk=d, epi=epi, after=after,
               out_shape=(jax.ShapeDtypeStruct((t, n), BF16),), out_specs=(pl.BlockSpec((tm, n), lambda j, i, k: (i, 0)),))[0]


def _mixer_dx(name, dp, w_in, x, dx1, naw, sc_a, after):
    t, d = x.shape

    def epi(dh, rs, tiles, vecs, outs, st_ref, live):
        x_ref, dx1_ref = tiles
        nw, sc = vecs
        outs[0][rs, :] = _norm_bwd_rows(dh, x_ref[rs, :], nw[...], sc[...], st_ref) + dx1_ref[rs, :]

    return _mm_deferred(name, dp, w_in, nt=True, tm=_pick(t, 256, 128), epi=epi, tiles=(x, dx1), vecs=(naw, sc_a),
                        out_tiles=(jax.ShapeDtypeStruct((t, d), F32),), n_stats=3, after=after, vmem_mib=56)


def _silu(v):
    return v / (1.0 + jnp.exp(-v))


def _ada_fwd(cond, w_ada, b_ada):
    d, n = w_ada.shape
    tn = 512

    def body(c_ref, w_ref, b_ref, o_ref):
        o_ref[...] = _dot(_silu(c_ref[...]).astype(BF16), w_ref[...].astype(BF16)) + b_ref[...]

    return pl.pallas_call(
        body, name="ada_fwd", grid=(n // tn,),
        in_specs=[pl.BlockSpec(cond.shape, lambda j: (0, 0)), pl.BlockSpec((d, tn), lambda j: (0, j)), pl.BlockSpec((1, tn), lambda j: (0, j))],
        out_specs=pl.BlockSpec((cond.shape[0], tn), lambda j: (0, j)), out_shape=jax.ShapeDtypeStruct((cond.shape[0], n), F32),
        compiler_params=_cp(1),
    )(cond, w_ada, b_ada)


def _adamw_math(w, g, m, v):
    m = ADAM_B1 * m + (1.0 - ADAM_B1) * g
    v = ADAM_B2 * v + (1.0 - ADAM_B2) * jnp.square(g)
    m_hat = m / (1.0 - ADAM_B1 ** ADAM_STEP)
    v_hat = v / (1.0 - ADAM_B2 ** ADAM_STEP)
    return -ADAM_LR * (m_hat / (jnp.sqrt(v_hat) + ADAM_EPS) + ADAM_WD * w), m, v


def _ada_bwd(cond, dm, w_ada, m_ada, v_ada):
    d, n = w_ada.shape
    tn = 256
    rows = cond.shape[0]

    def body(c_ref, dm_ref, w_ref, m_ref, v_ref, g_ref, dl_ref, nm_ref, nv_ref, pc_ref):
        @pl.when(pl.program_id(0) == 0)
        def _():
            pc_ref[...] = jnp.zeros_like(pc_ref)

        dmb = dm_ref[...].astype(BF16)
        w = w_ref[...]
        g = _dot_tn(_silu(c_ref[...]).astype(BF16), dmb)
        g_ref[...] = g
        dl_ref[...], nm_ref[...], nv_ref[...] = _adamw_math(w, g, m_ref[...], v_ref[...])
        pc_ref[...] += _dot_nt(dm_ref[8:16, :].astype(BF16), w.astype(BF16))

    tile = pl.BlockSpec((d, tn), lambda j: (0, j))
    like = jax.ShapeDtypeStruct((d, n), F32)
    return pl.pallas_call(
        body, name="ada_bwd", grid=(n // tn,),
        in_specs=[pl.BlockSpec((rows, d), lambda j: (0, 0)), pl.BlockSpec((rows, tn), lambda j: (0, j)), tile, tile, tile],
        out_specs=(tile, tile, tile, tile, pl.BlockSpec((8, d), lambda j: (0, 0))),
        out_shape=(like, like, like, like, jax.ShapeDtypeStruct((8, d), F32)), compiler_params=_cp(1),
    )(cond, dm, w_ada, m_ada, v_ada)


def _adamw(name, w, g, m, v, after=None):
    return _ew(name, lambda w_, g_, m_, v_: (g_,) + _adamw_math(w_, g_, m_, v_), [w, g, m, v], [F32, F32, F32, F32],
               g if after is None else after)


def _colsum(st):
    return jnp.sum(st, axis=1)


def kernel(x, c, ctx, c_ctx, norm_attn_w, norm_mlp_w, w_ada, b_ada, w_in, attn_sink, pool_w, pool_scale, w_out, w_mlp_up, w_mlp_down, final_norm_w, loss_target, m_c_ctx, m_norm_attn_w, m_norm_mlp_w, m_w_ada, m_b_ada, m_w_in, m_attn_sink, m_pool_w, m_pool_scale, m_w_out, m_w_mlp_up, m_w_mlp_down, m_final_norm_w, v_c_ctx, v_norm_attn_w, v_norm_mlp_w, v_w_ada, v_b_ada, v_w_in, v_attn_sink, v_pool_w, v_pool_scale, v_w_out, v_w_mlp_up, v_w_mlp_down, v_final_norm_w):
    length, d = x.shape[1], x.shape[2]
    n_ctx = ctx.shape[1]
    pos = (lax.axis_index("x"), lax.axis_index("y"), lax.axis_index("c"))
    me, chip = _dev_index(pos), _chip_index(pos)
    xs, tgt, cx = x.reshape(length, d), loss_target.reshape(length, d), ctx.reshape(n_ctx, d)
    n_ada = w_ada.shape[2]

    c_all = _allgather8("gather_c", jnp.pad(c, ((0, 7), (0, 0))))
    mixer_bigs = [_Big("col", w_in.shape[1:]), _Big("pool", pool_w.shape[1:]), _Big("row", w_out.shape[1:])]
    mlp_bigs = [_Big("col", w_mlp_up.shape[1:]), _Big("row", w_mlp_down.shape[1:])]
    placed = [_cast_place(f"place_{i}", b, s, c_all)[0] for i, (b, s) in enumerate(zip(mixer_bigs, [w_in[0], pool_w[0], w_out[0]]))]
    flight = _split("gather_mixer_ici", placed, _gather_ici_remote(mixer_bigs, 0))
    token, placed_mlp = flight[3], []
    for i, (b, s) in enumerate(zip(mlp_bigs, [w_mlp_up[0], w_mlp_down[0]])):
        p, token = _cast_place(f"place_mlp_{i}", b, s, token)
        placed_mlp.append(p)
    cos, sin = _rope_tables(length, True, token[0, 0])
    cond = jnp.concatenate([c_all[:, 0, :], jnp.pad(c_ctx[None, :], ((0, 7), (0, 0)))], axis=0) + 0.0 * cos[0, 0]
    b_shard = lax.dynamic_slice_in_dim(b_ada, chip * n_ada, n_ada, axis=1)
    mod_all = _allgather8("gather_mod", _ada_fwd(cond, w_ada[0], b_shard))
    mod = jnp.concatenate([mod_all[0], mod_all[2], mod_all[4], mod_all[6]], axis=1)
    mine = lax.dynamic_slice_in_dim(mod, me, 1, axis=0)
    sh_a, sc_a, g_a, sh_m, sc_m, g_m = [mine[:, d * i:d * (i + 1)] for i in range(6)]
    csh_a, csc_a = mod[8:9, :d], mod[8:9, d:2 * d]

    win_b, pw_b, wout_b = _exchange("gather_mixer_d2d", _join(flight, mod), [jax.ShapeDtypeStruct(b.full_shape, BF16) for b in mixer_bigs],
                                    _gather_d2d_remote(mixer_bigs, 3), aliases={0: 0, 1: 1, 2: 2})
    wout_b = wout_b.reshape(-1, d)
    flight = _split("gather_mlp_ici", placed_mlp, _gather_ici_remote(mlp_bigs, 0), after=pw_b)

    one, zero = _rope_tables(n_ctx, False)
    h, q, k, v, u = _mixer_in("mixer_in", xs, norm_attn_w, sh_a, sc_a, win_b, cos, sin, flight[3])
    hc, _, kc, vc, _ = _mixer_in("mixer_in_ctx", cx, norm_attn_w, csh_a, csc_a, win_b, one, zero, flight[3])
    attn, probs = _attn_fwd(q, k, v, kc, vc, attn_sink)
    mix = _pool_fwd(u, pw_b, pool_scale, attn)
    flight = _split("gather_mlp_d2d", _join(flight, mix), _gather_d2d_remote(mlp_bigs, 0))
    x1, mo, hm = _mixer_out(mix, wout_b, xs, g_a, norm_mlp_w, sh_m, sc_m, flight[3])
    wup_b, wdn_b = _join(flight, hm)
    wdn_b = wdn_b.reshape(-1, d)
    act = _mlp_up(hm, wup_b)
    dx2, ddn, st_loss = _mlp_down_loss(act, wdn_b, x1, tgt, g_m, final_norm_w[None, :], c)
    st_loss = _colsum(st_loss)

    tt = _pick(length, 2048, 1024, 512, 256, 128)
    g_wdn = _mm_tn("grad_w_down", act, ddn, BF16, tmo=1024, tn=d, tt=tt)
    dup = _mlp_dact(ddn, wdn_b, act)
    g_wup = _mm_tn("grad_w_up", hm, dup, BF16, tmo=d, tn=1024, tt=tt)
    empty = lambda shapes: [lax.empty(s.shape, s.dtype) for s in shapes]
    grads = [g_wup, g_wdn.reshape(mlp_bigs[1].full_shape)]
    flight = _split("reduce_mlp_d2d", grads + empty(_halves(mlp_bigs)), _reduce_d2d_remote(mlp_bigs))
    dx1, dmi, st_mlp = _mlp_dx(dup, wup_b, x1, dx2, mo, norm_mlp_w, sc_m, g_a, flight[3])
    st_mlp = _colsum(st_mlp)
    landed = _join(flight, dmi)
    mlp_chip = _chip_sums("mlp", mlp_bigs, landed[:2], landed[2:])
    flight = _split("reduce_mlp_ici", mlp_chip + empty(_thirds(mlp_bigs)), _reduce_ici_remote(mlp_bigs))
    g_wout = _mm_tn("grad_w_out", mix, dmi, BF16, tmo=1024, tn=d, tt=tt)
    dmix = _mixer_dmix(dmi, wout_b, flight[3])
    dq, dkv, dkc, dvc, dsink = _attn_bwd(q, k, v, kc, vc, dmix, probs, cos, sin)
    landed = _join(flight, dq)
    flight = _split("reduce_mlp_share", _piece_sums("mlp", mlp_bigs, landed[:2], landed[2:]), _share_remote(mlp_bigs, 0))
    du, g_pw, st_pool = _pool_bwd(u, dmix, pw_b, pool_scale, flight[3])
    g_mlp = _join(flight, du)

    wo_bigs, win_bigs = mixer_bigs[1:], mixer_bigs[:1]
    wo_chip = _reduce_to_chip("wo", wo_bigs, [g_pw.astype(BF16), g_wout.reshape(wo_bigs[1].full_shape)])
    flight = _split("reduce_wo_ici", wo_chip + empty(_thirds(wo_bigs)), _reduce_ici_remote(wo_bigs))
    dkv_ctx = jnp.concatenate([dkc.astype(BF16), dvc.astype(BF16)], axis=1)
    at_kv, at_u = ATTN_WIDTH, ATTN_WIDTH + 2 * KV_WIDTH
    tt_in = _pick(length, 1024, 512, 256, 128)
    g_win = jnp.concatenate([_mm_tn("grad_w_in_q", h, dq, BF16, tmo=d, tn=ATTN_WIDTH, tt=tt_in, after=flight[3]),
                             _mm_tn("grad_w_in_kv", h, dkv, BF16, tmo=d, tn=2 * KV_WIDTH, tt=tt_in, more=(hc, dkv_ctx)),
                             _mm_tn("grad_w_in_u", h, du, BF16, tmo=d, tn=POOL_WIDTH, tt=tt_in)], axis=1)
    wo_landed = _join(flight, g_win)
    win_chip = _reduce_to_chip("win", win_bigs, [g_win])
    flight = _split("reduce_win_ici", win_chip + empty(_thirds(win_bigs)), _reduce_ici_remote(win_bigs))
    grad_x, st_mix = _mixer_dx("mixer_dx", [(dq, 0), (dkv, at_kv), (du, at_u)], win_b, xs, dx1, norm_attn_w, sc_a, flight[3])
    _, st_ctx = _mixer_dx("mixer_dx_ctx", [(dkv_ctx, at_kv)], win_b, cx, jnp.zeros((n_ctx, d), F32), norm_attn_w, csc_a, flight[3])
    st_mix, st_ctx = _colsum(st_mix), _colsum(st_ctx)
    win_landed = _join(flight, grad_x)
    g_mixer = (_reduce_finish("win", win_bigs, win_landed[:1], win_landed[1:])
               + _reduce_finish("wo", wo_bigs, wo_landed[:2], wo_landed[2:]))

    zrow = jnp.zeros((d,), F32)
    pad = lambda a: jnp.pad(a, (0, d - a.shape[0]))
    mine_rows = [st_mix[0], st_mix[1], st_mlp[3], st_mlp[0], st_mlp[1], st_loss[2],
                 st_ctx[0], st_ctx[1],
                 st_mix[2] + st_ctx[2], st_mlp[2], st_loss[1],
                 pad(jnp.sum(st_pool, axis=0)), pad(dsink[0, :N_Q_HEADS]), st_loss[0]] + [zrow] * 2
    flight = _allgather8_split("gather_small", jnp.concatenate(mine_rows).reshape(len(mine_rows), d), me)
    res = {"w_mlp_up": tuple(_adamw("adamw_w_mlp_up", w_mlp_up, g_mlp[0].reshape(w_mlp_up.shape), m_w_mlp_up, v_w_mlp_up, flight[3]))}
    small_all = _join(flight, res["w_mlp_up"][1])[1]
    small = small_all[0]
    for i in range(1, 8):
        small = small + small_all[i]
    loss = 0.5 / d * jnp.sum(small[13])
    dm_rows = small_all[:, 0:6, :].reshape(8, 6 * d)
    dm_ctx = jnp.concatenate([small[6], small[7], jnp.zeros((4 * d,), F32)])[None, :]
    dm = jnp.concatenate([dm_rows, jnp.pad(dm_ctx, ((0, 7), (0, 0)))], axis=0)
    g_bada = jnp.sum(dm[:9], axis=0, keepdims=True)
    dm_shard = lax.dynamic_slice_in_dim(dm, chip * n_ada, n_ada, axis=1)
    g_wada, dl_wada, nm_wada, nv_wada, part_cctx = _ada_bwd(cond, dm_shard, w_ada[0], m_w_ada[0], v_w_ada[0])
    flight = _allgather8_split("gather_cctx", part_cctx, me)
    res["w_mlp_down"] = tuple(_adamw("adamw_w_mlp_down", w_mlp_down, g_mlp[1].reshape(w_mlp_down.shape), m_w_mlp_down,
                                     v_w_mlp_down, flight[3]))
    cctx_all = _join(flight, res["w_mlp_down"][1])[1]
    dsilu_in = cctx_all[0, 0] + cctx_all[2, 0] + cctx_all[4, 0] + cctx_all[6, 0]
    sig = 1.0 / (1.0 + jnp.exp(-c_ctx))
    g_cctx = dsilu_in * (sig * (1.0 + c_ctx * (1.0 - sig)))

    for nm, w_, g_, m_, v_ in zip(["w_in", "pool_w", "w_out"], [w_in, pool_w, w_out], g_mixer,
                                  [m_w_in, m_pool_w, m_w_out], [v_w_in, v_pool_w, v_w_out]):
        res[nm] = tuple(_adamw("adamw_" + nm, w_, g_.reshape(w_.shape), m_, v_))
    res["w_ada"] = (g_wada[None], dl_wada[None], nm_wada[None], nv_wada[None])

    def pack(cc, na, nm_, ba, sk, ps, fn):
        flat = [cc.reshape(-1), na.reshape(-1), nm_.reshape(-1), ba.reshape(-1), pad(sk.reshape(-1)), pad(ps.reshape(-1)),
                fn.reshape(-1), jnp.zeros((4 * d,), F32)]
        return jnp.concatenate(flat).reshape(16, d)

    w_s = pack(c_ctx, norm_attn_w, norm_mlp_w, b_ada, attn_sink, pool_scale, final_norm_w)
    m_s = pack(m_c_ctx, m_norm_attn_w, m_norm_mlp_w, m_b_ada, m_attn_sink, m_pool_scale, m_final_norm_w)
    v_s = pack(v_c_ctx, v_norm_attn_w, v_norm_mlp_w, v_b_ada, v_attn_sink, v_pool_scale, v_final_norm_w)
    g_s = pack(g_cctx, small[8], small[9], g_bada, small[12][:N_Q_HEADS], small[11][:POOL_WIDTH], small[10])
    small_out = _adamw("adamw_small", w_s, g_s, m_s, v_s)

    def unpack(p):
        return {"c_ctx": p[0], "norm_attn_w": p[1:2], "norm_mlp_w": p[2:3], "b_ada": p[3:9].reshape(1, 6 * d),
                "attn_sink": p[9:10, :N_Q_HEADS], "pool_scale": p[10:11, :POOL_WIDTH], "final_norm_w": p[11]}

    small_res = [unpack(p) for p in small_out]
    order = ["c_ctx", "norm_attn_w", "norm_mlp_w", "w_ada", "b_ada", "w_in", "attn_sink", "pool_w", "pool_scale",
             "w_out", "w_mlp_up", "w_mlp_down", "final_norm_w"]
    outs = [loss, grad_x.reshape(x.shape)]
    for kind in range(4):
        for nm in order:
            outs.append(res[nm][kind] if nm in res else small_res[kind][nm])
    return tuple(outs)
```

```python
import functools

import jax
import jax.numpy as jnp
from jax import lax
from jax.experimental import pallas as pl
from jax.experimental.pallas import tpu as pltpu

F32 = jnp.float32
BF16 = jnp.bfloat16
EPS = 1e-6
NEG_INF = -1e30
HEAD_DIM = 64
N_Q_HEADS = 16
N_KV_HEADS = 4
GROUP = N_Q_HEADS // N_KV_HEADS
ATTN_WIDTH = N_Q_HEADS * HEAD_DIM
KV_WIDTH = N_KV_HEADS * HEAD_DIM
POOL_WINDOWS = (2, 4, 8, 16)
POOL_GROUP_DIM = 256
POOL_WIDTH = len(POOL_WINDOWS) * POOL_GROUP_DIM
BLOCK = 128
GRID_W = 64
ROPE_BASE = 10000.0
SCALE = HEAD_DIM ** -0.5
HALO = 16
STRIP = 16
ADAM_LR, ADAM_B1, ADAM_B2, ADAM_EPS, ADAM_WD, ADAM_STEP = 0.001, 0.9, 0.999, 1e-08, 0.01, 10
MESH = pl.DeviceIdType.MESH
MIB = 1024 * 1024
ANY = pl.BlockSpec(memory_space=pl.ANY)


def _cp(n_axes, vmem_mib=48):
    return pltpu.CompilerParams(dimension_semantics=("arbitrary",) * n_axes, vmem_limit_bytes=vmem_mib * MIB)


def _fold8(v):
    s = v[0:8]
    for t in range(1, v.shape[0] // 8):
        s = s + v[8 * t:8 * t + 8]
    return s


def _dot(a, b):
    return jnp.dot(a, b, preferred_element_type=F32)


def _dot_nt(a, b):
    return lax.dot_general(a, b, (((1,), (1,)), ((), ())), preferred_element_type=F32)


def _dot_tn(a, b):
    return lax.dot_general(a, b, (((0,), (0,)), ((), ())), preferred_element_type=F32)


def _pick(n, *cands):
    for t in cands:
        if n % t == 0:
            return t
    return n


def _flip(pos, mask):
    return tuple((1 - v) if (mask >> (2 - i)) & 1 else v for i, v in enumerate(pos))


def _exchange(name, ins, out_shapes, remote, local=(), aliases=None):
    n_io = len(ins) + len(out_shapes)

    def body(*refs):
        io = refs[:n_io]
        send_sems, recv_sems, local_sems = refs[n_io:]
        me = (lax.axis_index("x"), lax.axis_index("y"), lax.axis_index("c"))

        def copy(i, sender):
            mask, src_fn, dst_fn = remote[i]
            return pltpu.make_async_remote_copy(
                src_ref=src_fn(io, sender), dst_ref=dst_fn(io, sender), send_sem=send_sems.at[i],
                recv_sem=recv_sems.at[i], device_id=_flip(sender, mask), device_id_type=MESH)

        own = [pltpu.make_async_copy(s(io, me), d(io, me), local_sems.at[i]) for i, (s, d) in enumerate(local)]
        for cp in own:
            cp.start()
        sends = [copy(i, me) for i in range(len(remote))]
        for cp in sends:
            cp.start()
        for i in range(len(remote)):
            copy(i, _flip(me, remote[i][0])).wait_recv()
        for cp in sends:
            cp.wait_send()
        for cp in own:
            cp.wait()

    return pl.pallas_call(
        body, name=name, out_shape=tuple(out_shapes),
        in_specs=[ANY] * len(ins), out_specs=tuple([ANY] * len(out_shapes)),
        scratch_shapes=[pltpu.SemaphoreType.DMA((len(remote),)), pltpu.SemaphoreType.DMA((len(remote),)),
                        pltpu.SemaphoreType.DMA((max(len(local), 1),))],
        input_output_aliases=aliases or {},
    )(*ins)


HBM = pl.BlockSpec(memory_space=pltpu.HBM)
SEM = pl.BlockSpec(memory_space=pltpu.SEMAPHORE)
EFFECT = pltpu.SideEffectType.DATAFLOW_SIDE_EFFECTING


def _split_copy(remote, i, io, send_sems, recv_sems, sender):
    mask, src_fn, dst_fn = remote[i]
    return pltpu.make_async_remote_copy(
        src_ref=src_fn(io, sender), dst_ref=dst_fn(io, sender), send_sem=send_sems.at[i],
        recv_sem=recv_sems.at[i], device_id=_flip(sender, mask), device_id_type=MESH)


def _exchange_start(name, bufs, remote, after=None):
    n, r = len(bufs), len(remote)
    more = [] if after is None else [after]

    def body(*refs):
        io, (send_sems, recv_sems, token) = refs[:n], refs[-3:]
        me = (lax.axis_index("x"), lax.axis_index("y"), lax.axis_index("c"))
        for i in range(r):
            _split_copy(remote, i, io, send_sems, recv_sems, me).start()
        token[...] = jnp.zeros_like(token)

    res = pl.pallas_call(
        body, name=name,
        out_shape=tuple(pltpu.HBM(b.shape, b.dtype) for b in bufs)
        + (pltpu.SemaphoreType.DMA((r,)), pltpu.SemaphoreType.DMA((r,)), jax.ShapeDtypeStruct((8, 128), F32)),
        in_specs=[HBM] * n + [ANY] * len(more), out_specs=tuple([HBM] * n) + (SEM, SEM, pl.BlockSpec(memory_space=pltpu.VMEM)),
        input_output_aliases={i: i for i in range(n)}, compiler_params=pltpu.CompilerParams(has_side_effects=EFFECT),
    )(*[pltpu.with_memory_space_constraint(b, pltpu.HBM) for b in bufs], *more)
    return list(res[:n]), res[n], res[n + 1], res[n + 2]


def _exchange_wait(name, bufs, send_sems, recv_sems, remote, after):
    n, r = len(bufs), len(remote)

    def body(*refs):
        io, ss, rs = refs[:n], refs[n], refs[n + 1]
        me = (lax.axis_index("x"), lax.axis_index("y"), lax.axis_index("c"))
        for i in range(r):
            _split_copy(remote, i, io, ss, rs, _flip(me, remote[i][0])).wait_recv()
        for i in range(r):
            _split_copy(remote, i, io, ss, rs, me).wait_send()

    return list(pl.pallas_call(
        body, name=name, out_shape=tuple(pltpu.HBM(b.shape, b.dtype) for b in bufs),
        in_specs=[HBM] * n + [SEM, SEM, ANY], out_specs=tuple([HBM] * n),
        input_output_aliases={i: i for i in range(n)}, compiler_params=pltpu.CompilerParams(has_side_effects=EFFECT),
    )(*bufs, send_sems, recv_sems, after))


def _my_c():
    return lax.axis_index("c")


def _my_chip():
    return 2 * lax.axis_index("x") + lax.axis_index("y")


def _dev_index(pos):
    return 4 * pos[0] + 2 * pos[1] + pos[2]


def _chip_index(pos):
    return 2 * pos[0] + pos[1]


def _allgather8(name, v):
    out = jax.ShapeDtypeStruct((8,) + v.shape, v.dtype)
    remote = [(mask, lambda io, pos: io[0], lambda io, pos: io[1].at[_dev_index(pos)]) for mask in range(1, 8)]
    local = [(lambda io, pos: io[0], lambda io, pos: io[1].at[_dev_index(pos)])]
    return _exchange(name, [v], [out], remote, local)[0]


class _Big:
    def __init__(self, kind, shard_shape):
        self.kind = kind
        self.shard_shape = tuple(shard_shape)
        if kind == "col":
            r, cs = shard_shape
            self.full_shape = (r, 4 * cs)
            self.piece_shape = (r // 2, cs)
            self.half_shape = (r // 2, 4 * cs)
        elif kind == "row":
            rs, c = shard_shape
            self.full_shape = (4, 2, rs // 2, c)
            self.piece_shape = (1, 1, rs // 2, c)
            self.half_shape = (4, 1, rs // 2, c)
        else:
            self.full_shape = (4, 256, 256)
            self.piece_shape = (2, 64, 256)
            self.half_shape = (2, 256, 256)

    def shard_as_pieces(self, a):
        return a.reshape((1, 2) + self.piece_shape[2:]) if self.kind == "row" else a

    def piece(self, ref, k, h):
        if self.kind == "col":
            r, cs = self.piece_shape
            return ref.at[pl.ds(h * r, r), pl.ds(k * cs, cs)]
        if self.kind == "row":
            return ref.at[pl.ds(k, 1), pl.ds(h, 1)]
        return ref.at[pl.ds(2 * h, 2), pl.ds(64 * k, 64)]

    def half_of_shard(self, ref, h):
        if self.kind == "col":
            return ref.at[pl.ds(h * self.piece_shape[0], self.piece_shape[0])]
        if self.kind == "row":
            return ref.at[:, pl.ds(h, 1)]
        return ref.at[pl.ds(2 * h, 2)]

    def half_of_full(self, ref, h):
        if self.kind == "col":
            return ref.at[pl.ds(h * self.half_shape[0], self.half_shape[0])]
        if self.kind == "row":
            return ref.at[:, pl.ds(h, 1)]
        return ref.at[pl.ds(2 * h, 2)]

    def piece_of_half(self, ref, k):
        if self.kind == "col":
            return ref.at[:, pl.ds(k * self.piece_shape[1], self.piece_shape[1])]
        if self.kind == "row":
            return ref.at[pl.ds(k, 1)]
        return ref.at[:, pl.ds(64 * k, 64)]


CHIP_MASKS = (4, 2, 6)


def _cast_place(name, big, shard, after):
    if big.kind == "col":
        r, cs = big.shard_shape
        tr = _pick(r, 512, 256, 128)
        src, grid, blk = shard, (r // tr,), (tr, cs)
        imap, omap = (lambda i: (i, 0)), (lambda i: (i, _my_chip()))
    elif big.kind == "row":
        rs, c = big.shard_shape
        tr = _pick(rs // 2, 256, 128)
        src, grid, blk = big.shard_as_pieces(shard), (2, rs // 2 // tr), (1, 1, tr, c)
        imap, omap = (lambda h, i: (0, h, i, 0)), (lambda h, i: (_my_chip(), h, i, 0))
    else:
        src, grid, blk = shard, (1,), big.shard_shape
        imap, omap = (lambda i: (0, 0, 0)), (lambda i: (0, _my_chip(), 0))

    def body(s_ref, after_ref, o_ref, token_ref):
        o_ref[...] = s_ref[...].astype(BF16)
        token_ref[...] = jnp.zeros_like(token_ref)

    return pl.pallas_call(
        body, name=name, grid=grid, in_specs=[pl.BlockSpec(blk, imap), ANY],
        out_specs=(pl.BlockSpec(blk, omap), pl.BlockSpec((8, 128), lambda *_: (0, 0))),
        out_shape=(jax.ShapeDtypeStruct(big.full_shape, BF16), jax.ShapeDtypeStruct((8, 128), F32)), compiler_params=_cp(len(grid)),
    )(src, after)


def _gather_ici_remote(bigs, off):
    remote = []
    for a, b in enumerate(bigs):
        for mask in CHIP_MASKS:
            def mine(io, p, a=a, b=b):
                return b.piece(io[off + a], _chip_index(p), p[2])
            remote.append((mask, mine, mine))
    return remote


def _gather_d2d_remote(bigs, off):
    remote = []
    for a, b in enumerate(bigs):
        for mask in CHIP_MASKS:
            def region(io, p, a=a, b=b, mask=mask):
                return b.piece(io[off + a], _chip_index(_flip(p, mask)), p[2])
            remote.append((1, region, region))
    return remote


def _ew(name, fn, ins, out_dtypes, after, rows_per_step=256):
    shape = ins[0].shape
    last = shape[-1]
    rows = 1
    for s in shape[:-1]:
        rows *= s
    ins2 = [a.reshape(rows, last) for a in ins]
    tr = _pick(rows, rows_per_step, 128, 64, 32, 16, 8)
    spec = pl.BlockSpec((tr, last), lambda i: (i, 0))

    def body(*refs):
        outs = fn(*[r[...] for r in refs[:len(ins)]])
        for o_ref, o in zip(refs[len(ins) + 1:], outs):
            o_ref[...] = o.astype(o_ref.dtype)

    outs = pl.pallas_call(
        body, name=name, grid=(rows // tr,), in_specs=[spec] * len(ins) + [ANY], out_specs=tuple([spec] * len(out_dtypes)),
        out_shape=tuple(jax.ShapeDtypeStruct((rows, last), d) for d in out_dtypes), compiler_params=_cp(1),
    )(*ins2, after)
    return [o.reshape(shape) for o in outs]


def _chip_sum(name, big, grad, from_sibling):
    if big.kind == "col":
        rh, w = big.half_shape
        tr = _pick(rh, 256, 128)
        nb = rh // tr
        grid, blk = (nb,), (tr, w)
        gmap, hmap = (lambda i: (_my_c() * nb + i, 0)), (lambda i: (i, 0))
    elif big.kind == "row":
        rh, w = big.half_shape[2:]
        tr = _pick(rh, 256, 128)
        grid, blk = (4, rh // tr), (1, 1, tr, w)
        gmap, hmap = (lambda k, i: (k, _my_c(), i, 0)), (lambda k, i: (k, 0, i, 0))
    else:
        grid, blk = (1,), big.half_shape
        gmap, hmap = (lambda i: (_my_c(), 0, 0)), (lambda i: (0, 0, 0))

    def body(g_ref, s_ref, o_ref):
        o_ref[...] = (g_ref[...].astype(F32) + s_ref[...].astype(F32)).astype(BF16)

    return pl.pallas_call(
        body, name=name, grid=grid, in_specs=[pl.BlockSpec(blk, gmap), pl.BlockSpec(blk, hmap)],
        out_specs=pl.BlockSpec(blk, hmap), out_shape=jax.ShapeDtypeStruct(big.half_shape, BF16), compiler_params=_cp(len(grid)),
    )(grad, from_sibling)


def _piece_sum(name, big, chip_sum, thirds):
    if big.kind == "col":
        rp, cs = big.piece_shape
        tr = _pick(rp, 256, 128)
        nb = rp // tr
        grid, blk, tblk = (nb,), (tr, cs), (1, tr, cs)
        smap, omap = (lambda i: (i, _my_chip())), (lambda i: (_my_c() * nb + i, 0))
        tmap = lambda j: (lambda i: (j, i, 0))
        out_shape = big.shard_shape
    elif big.kind == "row":
        rp, w = big.piece_shape[2:]
        tr = _pick(rp, 256, 128)
        grid, blk, tblk = (rp // tr,), (1, 1, tr, w), (1, 1, 1, tr, w)
        smap, omap = (lambda i: (_my_chip(), 0, i, 0)), (lambda i: (0, _my_c(), i, 0))
        tmap = lambda j: (lambda i: (j, 0, 0, i, 0))
        out_shape = (1, 2, rp, w)
    else:
        grid, blk, tblk = (1,), big.piece_shape, (1,) + big.piece_shape
        smap, omap = (lambda i: (0, _my_chip(), 0)), (lambda i: (_my_c(), 0, 0))
        tmap = lambda j: (lambda i: (j, 0, 0, 0))
        out_shape = big.shard_shape

    def body(s_ref, t0, t1, t2, o_ref):
        o_ref[...] = s_ref[...].astype(F32) + t0[0].astype(F32) + t1[0].astype(F32) + t2[0].astype(F32)

    return pl.pallas_call(
        body, name=name, grid=grid,
        in_specs=[pl.BlockSpec(blk, smap)] + [pl.BlockSpec(tblk, tmap(j)) for j in range(3)],
        out_specs=pl.BlockSpec(blk, omap), out_shape=jax.ShapeDtypeStruct(out_shape, F32), compiler_params=_cp(len(grid)),
    )(chip_sum, thirds, thirds, thirds)


def _split(name, bufs, remote, after=None):
    return _exchange_start(name + "_start", bufs, remote, after) + (remote, name)


def _join(handle, after):
    bufs, send_sems, recv_sems, _, remote, name = handle
    return _exchange_wait(name + "_wait", bufs, send_sems, recv_sems, remote, after)


def _allgather8_split(name, v, me):
    own = lax.dynamic_update_slice(lax.empty((8,) + v.shape, v.dtype), v[None], (me, 0, 0))
    remote = [(mask, lambda io, pos: io[0], lambda io, pos: io[1].at[_dev_index(pos)]) for mask in range(1, 8)]
    return _split(name, [v, own], remote)


def _reduce_d2d_remote(bigs):
    n = len(bigs)
    return [(1, lambda io, p, a=a, b=b: b.half_of_full(io[a], 1 - p[2]), lambda io, p, a=a: io[n + a])
            for a, b in enumerate(bigs)]


def _halves(bigs):
    return [jax.ShapeDtypeStruct(b.half_shape, BF16) for b in bigs]


def _chip_sums(tag, bigs, grads, from_sibling):
    return [_chip_sum(f"reduce_{tag}_chip_sum_{a}", b, g, r) for a, (b, g, r) in enumerate(zip(bigs, grads, from_sibling))]


def _reduce_to_chip(tag, bigs, grads):
    from_sibling = _exchange(f"reduce_{tag}_d2d", grads, _halves(bigs), _reduce_d2d_remote(bigs))
    return _chip_sums(tag, bigs, grads, from_sibling)


def _reduce_ici_remote(bigs):
    n = len(bigs)
    remote = []
    for a, b in enumerate(bigs):
        for j, mask in enumerate(CHIP_MASKS):
            remote.append((mask,
                           lambda io, p, a=a, b=b, mask=mask: b.piece_of_half(io[a], _chip_index(_flip(p, mask))),
                           lambda io, p, a=a, j=j: io[n + a].at[j]))
    return remote


def _thirds(bigs):
    return [jax.ShapeDtypeStruct((3,) + b.piece_shape, BF16) for b in bigs]


def _piece_sums(tag, bigs, chip_sum, from_chips):
    return [_piece_sum(f"reduce_{tag}_sum_{a}", b, s, r) for a, (b, s, r) in enumerate(zip(bigs, chip_sum, from_chips))]


def _share_remote(bigs, off):
    remote = []
    for a, b in enumerate(bigs):
        def mine(io, p, a=a, b=b):
            return b.half_of_shard(io[off + a], p[2])
        remote.append((1, mine, mine))
    return remote


def _reduce_finish(tag, bigs, chip_sum, from_chips):
    n = len(bigs)
    placed = _piece_sums(tag, bigs, chip_sum, from_chips)
    out = _exchange(f"reduce_{tag}_share_d2d", placed, [jax.ShapeDtypeStruct(p.shape, F32) for p in placed],
                    _share_remote(bigs, n), aliases={a: a for a in range(n)})
    return [o.reshape(b.shard_shape) for o, b in zip(out, bigs)]


def _mm(name, a, b, *, nt, tm, tn, tk, epi, extras=(), extra_specs=(), out_shape, out_specs, after=None, vmem_mib=48):
    m, kdim = a.shape
    n = b.shape[0] if nt else b.shape[1]
    gm, gn, gk = m // tm, n // tn, kdim // tk
    a_spec = pl.BlockSpec((tm, tk), lambda j, i, k: (i, k))
    b_spec = pl.BlockSpec((tn, tk), lambda j, i, k: (j, k)) if nt else pl.BlockSpec((tk, tn), lambda j, i, k: (k, j))
    n_ex = len(extras)
    if after is not None:
        extras, extra_specs = tuple(extras) + (after,), list(extra_specs) + [ANY]

    def body(a_ref, b_ref, *rest):
        ex, outs, acc = rest[:n_ex], rest[len(extras):-1], rest[-1]
        dot = _dot_nt if nt else _dot
        if gk == 1:
            acc[...] = dot(a_ref[...], b_ref[...])
            epi(acc, ex, outs)
        else:
            k = pl.program_id(2)

            @pl.when(k == 0)
            def _():
                acc[...] = dot(a_ref[...], b_ref[...])

            @pl.when(k > 0)
            def _():
                acc[...] += dot(a_ref[...], b_ref[...])

            @pl.when(k == gk - 1)
            def _():
                epi(acc, ex, outs)

    return pl.pallas_call(
        body, name=name, grid=(gn, gm, gk), in_specs=[a_spec, b_spec, *extra_specs], out_specs=tuple(out_specs),
        out_shape=tuple(out_shape), scratch_shapes=[pltpu.VMEM((tm, tn), F32)], compiler_params=_cp(3, vmem_mib),
    )(a, b, *extras)


def _mm_deferred(name, a, b, *, nt, tm, epi, tiles, vecs, out_tiles, n_stats, after, vmem_mib=48):
    pieces = a if isinstance(a, (list, tuple)) else [(a, 0)]
    m = pieces[0][0].shape[0]
    n = b.shape[0] if nt else b.shape[1]
    gm = m // tm
    n_a, n_t, n_v, n_o = len(pieces), len(tiles), len(vecs), len(out_tiles)

    def body(*refs):
        a_refs, b_ref, rest = refs[:n_a], refs[n_a], refs[n_a + 1:]
        t_refs, v_refs = rest[:n_t], rest[n_t:n_t + n_v]
        o_refs, st_ref, acc0, acc1 = rest[n_t + n_v + 1:n_t + n_v + 1 + n_o], rest[-3], rest[-2], rest[-1]
        i = pl.program_id(0)

        def dot():
            if n_a == 1 and pieces[0][0].shape[1] == b.shape[1 if nt else 0]:
                return (_dot_nt if nt else _dot)(a_refs[0][...], b_ref[...])
            parts = [_dot_nt(r[...], b_ref[:, off:off + p.shape[1]]) for r, (p, off) in zip(a_refs, pieces)]
            return functools.reduce(lambda u, v: u + v, parts)

        @pl.when(i == 0)
        def _():
            acc1[...] = jnp.zeros_like(acc1)
            st_ref[...] = jnp.zeros_like(st_ref)

        def finish(prev):
            for r0 in range(0, tm, STRIP):
                rs = slice(r0, r0 + STRIP)
                epi(prev[rs, :], rs, t_refs, v_refs, o_refs, st_ref, i > 0)

        @pl.when((i % 2 == 0) & (i < gm))
        def _():
            acc0[...] = dot()
            finish(acc1)

        @pl.when((i % 2 == 1) & (i < gm))
        def _():
            acc1[...] = dot()
            finish(acc0)

        @pl.when(i == gm)
        def _():
            finish(acc1 if gm % 2 == 0 else acc0)

    prev = lambda i: (jnp.maximum(i - 1, 0), 0)
    tile = pl.BlockSpec((tm, n), prev)
    return pl.pallas_call(
        body, name=name, grid=(gm + 1,),
        in_specs=[pl.BlockSpec((tm, p.shape[1]), lambda i: (jnp.minimum(i, gm - 1), 0)) for p, _ in pieces]
        + [pl.BlockSpec(b.shape, lambda i: (0, 0))] + [tile] * n_t + [_row_spec(n)] * n_v + [ANY],
        out_specs=tuple([tile] * n_o) + (_stat_spec(n_stats, n),),
        out_shape=tuple(out_tiles) + (jax.ShapeDtypeStruct((n_stats, 8, n), F32),),
        scratch_shapes=[pltpu.VMEM((tm, n), F32), pltpu.VMEM((tm, n), F32)], compiler_params=_cp(1, vmem_mib),
    )(*[p for p, _ in pieces], b, *tiles, *vecs, after)


def _mm_k_deferred(name, a, b, *, nt, tm, tk, epi, tiles, vecs, out_tiles, n_stats, after, vmem_mib=56):
    m, kdim = a.shape
    n = b.shape[0] if nt else b.shape[1]
    gm, gk = m // tm, kdim // tk
    rows = tm // gk
    n_t, n_v, n_o = len(tiles), len(vecs), len(out_tiles)
    dot = _dot_nt if nt else _dot

    def body(a_ref, b_ref, *rest):
        t_refs, v_refs = rest[:n_t], rest[n_t:n_t + n_v]
        o_refs, st_ref, acc0, acc1 = rest[n_t + n_v + 1:n_t + n_v + 1 + n_o], rest[-3], rest[-2], rest[-1]
        i, k = pl.program_id(0), pl.program_id(1)

        @pl.when((i == 0) & (k == 0))
        def _():
            acc1[...] = jnp.zeros_like(acc1)
            st_ref[...] = jnp.zeros_like(st_ref)

        def finish(prev):
            for r0 in range(0, rows, STRIP):
                acc_rows = prev[pl.ds(pl.multiple_of(k * rows + r0, STRIP), STRIP), :]
                epi(acc_rows, slice(r0, r0 + STRIP), t_refs, v_refs, o_refs, st_ref, i > 0)

        def step(cur, prev):
            cur[...] = jnp.where(k > 0, cur[...], 0.0) + dot(a_ref[...], b_ref[...])
            finish(prev)

        @pl.when((i % 2 == 0) & (i < gm))
        def _():
            step(acc0, acc1)

        @pl.when((i % 2 == 1) & (i < gm))
        def _():
            step(acc1, acc0)

        @pl.when(i == gm)
        def _():
            finish(acc1 if gm % 2 == 0 else acc0)

    prev = lambda i, k: (jnp.where(i == 0, 0, (i - 1) * gk + k), 0)
    part = pl.BlockSpec((rows, n), prev)
    b_spec = pl.BlockSpec((n, tk), lambda i, k: (0, k)) if nt else pl.BlockSpec((tk, n), lambda i, k: (k, 0))
    return pl.pallas_call(
        body, name=name, grid=(gm + 1, gk),
        in_specs=[pl.BlockSpec((tm, tk), lambda i, k: (jnp.minimum(i, gm - 1), k)), b_spec]
        + [part] * n_t + [_row_spec(n)] * n_v + [ANY],
        out_specs=tuple([part] * n_o) + (_stat_spec(n_stats, n),),
        out_shape=tuple(out_tiles) + (jax.ShapeDtypeStruct((n_stats, 8, n), F32),),
        scratch_shapes=[pltpu.VMEM((tm, n), F32), pltpu.VMEM((tm, n), F32)], compiler_params=_cp(2, vmem_mib),
    )(a, b, *tiles, *vecs, after)


def _mm_tn(name, a, b, out_dtype, *, tmo, tn, tt, more=(), after=None, vmem_mib=56):
    t, m = a.shape
    n = b.shape[1]
    gt = t // tt
    wait_for = [] if after is None else [after]

    def body(a_ref, b_ref, *rest):
        o_ref, acc = rest[-2:]
        k = pl.program_id(2)

        @pl.when(k == 0)
        def _():
            first = _dot_tn(a_ref[...], b_ref[...])
            acc[...] = first + _dot_tn(rest[0][...], rest[1][...]) if more else first

        @pl.when(k > 0)
        def _():
            acc[...] += _dot_tn(a_ref[...], b_ref[...])

        @pl.when(k == gt - 1)
        def _():
            o_ref[...] = acc[...].astype(o_ref.dtype)

    more_specs = [pl.BlockSpec((more[0].shape[0], tmo), lambda i, j, k: (0, i)),
                  pl.BlockSpec((more[1].shape[0], tn), lambda i, j, k: (0, j))] if more else []
    return pl.pallas_call(
        body, name=name, grid=(m // tmo, n // tn, gt),
        in_specs=[pl.BlockSpec((tt, tmo), lambda i, j, k: (k, i)), pl.BlockSpec((tt, tn), lambda i, j, k: (k, j))] + more_specs
        + [ANY] * len(wait_for),
        out_specs=pl.BlockSpec((tmo, tn), lambda i, j, k: (i, j)), out_shape=jax.ShapeDtypeStruct((m, n), out_dtype),
        scratch_shapes=[pltpu.VMEM((tmo, tn), F32)], compiler_params=_cp(3, vmem_mib),
    )(a, b, *more, *wait_for)


def _row_spec(d):
    return pl.BlockSpec((1, d), lambda *_: (0, 0))


def _stat_spec(k, d):
    return pl.BlockSpec((k, 8, d), lambda *_: (0, 0, 0))


def _rope(z, cs, sn):
    first = (lax.broadcasted_iota(jnp.int32, (z.shape[0], 128), 1) % 32) < 16
    outs = []
    for j in range(z.shape[1] // 128):
        zc = z[:, 128 * j:128 * (j + 1)]
        partner = jnp.where(first, pltpu.roll(zc, 112, 1), pltpu.roll(zc, 16, 1))
        outs.append(zc * cs + partner * sn)
    return outs[0] if len(outs) == 1 else jnp.concatenate(outs, axis=1)


def _rope_tables(length, rotate, zero=0.0):
    if not rotate:
        return jnp.ones((length, 128), F32), jnp.zeros((length, 128), F32)
    half = HEAD_DIM // 2
    t = jnp.arange(length)
    row = (t // GRID_W).astype(F32) + zero
    col = (t % GRID_W).astype(F32)
    e = jnp.arange(128) % HEAD_DIM
    inv_freq = ROPE_BASE ** (-(2 * ((e % half) % (half // 2))).astype(F32) / half)
    pos = jnp.where(e[None, :] < half, row[:, None], col[:, None])
    ang = pos * inv_freq[None, :]
    first = ((e % half) < half // 2)[None, :]
    return jnp.cos(ang), jnp.where(first, -jnp.sin(ang), jnp.sin(ang))


def _mixer_in(name, x, nw, sh, sc, w_in, cos, sin, after):
    t, d = x.shape
    tm = _pick(t, 256, 128)
    n_in = w_in.shape[1]

    def body(x_ref, nw_ref, sh_ref, sc_ref, w_ref, cos_ref, sin_ref, after_ref, h_ref, q_ref, k_ref, v_ref, u_ref):
        xf = x_ref[...]
        r = lax.rsqrt(jnp.mean(xf * xf, axis=-1, keepdims=True) + EPS)
        hb = (((xf * r) * nw_ref[...]) * (1.0 + sc_ref[...]) + sh_ref[...]).astype(BF16)
        h_ref[...] = hb
        p = _dot(hb, w_ref[...])
        cs, sn = cos_ref[...], sin_ref[...]
        q_ref[...] = (_rope(p[:, :ATTN_WIDTH], cs, sn) * SCALE).astype(BF16)
        k_ref[...] = _rope(p[:, ATTN_WIDTH:ATTN_WIDTH + KV_WIDTH], cs, sn).astype(BF16)
        v_ref[...] = p[:, ATTN_WIDTH + KV_WIDTH:ATTN_WIDTH + 2 * KV_WIDTH].astype(BF16)
        u_ref[...] = p[:, ATTN_WIDTH + 2 * KV_WIDTH:]

    def tile(w):
        return pl.BlockSpec((tm, w), lambda i: (i, 0))

    return pl.pallas_call(
        body, name=name, grid=(t // tm,),
        in_specs=[tile(d), _row_spec(d), _row_spec(d), _row_spec(d), pl.BlockSpec((d, n_in), lambda i: (0, 0)),
                  tile(128), tile(128), ANY],
        out_specs=(tile(d), tile(ATTN_WIDTH), tile(KV_WIDTH), tile(KV_WIDTH), tile(POOL_WIDTH)),
        out_shape=(jax.ShapeDtypeStruct((t, d), BF16), jax.ShapeDtypeStruct((t, ATTN_WIDTH), BF16),
                   jax.ShapeDtypeStruct((t, KV_WIDTH), BF16), jax.ShapeDtypeStruct((t, KV_WIDTH), BF16),
                   jax.ShapeDtypeStruct((t, POOL_WIDTH), F32)),
        compiler_params=_cp(1),
    )(x, nw, sh, sc, w_in, cos, sin, after)


def _attn_specs(nb, n_ctx):
    def blk(w, f):
        return pl.BlockSpec((BLOCK, w), lambda n: (f(n), 0))

    prev = lambda n: jnp.maximum(jnp.minimum(n, nb - 1) - 1, 0)
    cur = lambda n: jnp.minimum(n, nb - 1)
    nxt = lambda n: jnp.minimum(n + 1, nb - 1)
    kv = [blk(KV_WIDTH, prev), blk(KV_WIDTH, cur), blk(KV_WIDTH, nxt)]
    ctx = pl.BlockSpec((n_ctx, KV_WIDTH), lambda n: (0, 0))
    return [pl.BlockSpec(memory_space=pltpu.SMEM), blk(ATTN_WIDTH, cur)] + kv + kv + [ctx, ctx]


def _attn_mask(n, length, n_keys):
    row = lax.broadcasted_iota(jnp.int32, (GROUP * BLOCK, n_keys), 0) % BLOCK
    col = lax.broadcasted_iota(jnp.int32, (GROUP * BLOCK, n_keys), 1)
    kpos = (n - 1) * BLOCK + col
    return ((jnp.abs(col - BLOCK - row) <= BLOCK) & (kpos >= 0) & (kpos < length)) | (col >= 3 * BLOCK)


def _group_rows(block, g):
    return jnp.concatenate([block[:, HEAD_DIM * h:HEAD_DIM * (h + 1)] for h in range(GROUP * g, GROUP * (g + 1))], axis=0)


def _group_sink(sink_ref, g):
    head = lax.broadcasted_iota(jnp.int32, (GROUP * BLOCK, 1), 0) // BLOCK
    out = jnp.full((GROUP * BLOCK, 1), sink_ref[0, GROUP * g], F32)
    for j in range(1, GROUP):
        out = jnp.where(head == j, sink_ref[0, GROUP * g + j], out)
    return out


def _attn_fwd(q, k, v, kc, vc, sink):
    length = q.shape[0]
    nb = length // BLOCK
    n_ctx = kc.shape[0]
    n_keys = 3 * BLOCK + n_ctx

    def body(sink_ref, q_ref, kp, k0, kn, vp, v0, vn, kc_ref, vc_ref, o_ref, p_ref):
        n = pl.program_id(0)
        valid = _attn_mask(n, length, n_keys)
        qb = q_ref[...]
        kall = jnp.concatenate([kp[...], k0[...], kn[...], kc_ref[...]], axis=0)
        vall = jnp.concatenate([vp[...], v0[...], vn[...], vc_ref[...]], axis=0)
        outs = []
        for g in range(N_KV_HEADS):
            lanes = slice(HEAD_DIM * g, HEAD_DIM * (g + 1))
            s = jnp.where(valid, _dot_nt(_group_rows(qb, g), kall[:, lanes]), NEG_INF)
            sk = _group_sink(sink_ref, g)
            m = jnp.maximum(jnp.max(s, axis=-1, keepdims=True), sk)
            e = jnp.exp(s - m)
            e_sink = jnp.exp(sk - m)
            inv = 1.0 / (jnp.sum(e, axis=-1, keepdims=True) + e_sink)
            p_ref[0, g, :, :n_keys] = (e * inv).astype(BF16)
            p_ref[0, g, :, n_keys:] = jnp.broadcast_to(e_sink * inv, (GROUP * BLOCK, 128)).astype(BF16)
            o = _dot(p_ref[0, g, :, :n_keys], vall[:, lanes])
            outs += [o[BLOCK * j:BLOCK * (j + 1)] for j in range(GROUP)]
        o_ref[...] = jnp.concatenate(outs, axis=1).astype(BF16)

    return pl.pallas_call(
        body, name="attn_fwd", grid=(nb,), in_specs=_attn_specs(nb, n_ctx),
        out_specs=(pl.BlockSpec((BLOCK, ATTN_WIDTH), lambda n: (n, 0)),
                   pl.BlockSpec((1, N_KV_HEADS, GROUP * BLOCK, n_keys + 128), lambda n: (n, 0, 0, 0))),
        out_shape=(jax.ShapeDtypeStruct((length, ATTN_WIDTH + POOL_WIDTH), BF16),
                   jax.ShapeDtypeStruct((nb, N_KV_HEADS, GROUP * BLOCK, n_keys + 128), BF16)), compiler_params=_cp(1),
    )(sink, q, k, k, k, v, v, v, kc, vc)


def _attn_bwd(q, k, v, kc, vc, dmix, probs, cos, sin):
    length = q.shape[0]
    nb = length // BLOCK
    n_ctx = kc.shape[0]
    n_keys = 3 * BLOCK + n_ctx

    def body(q_ref, kp, k0, kn, vp, v0, vn, kc_ref, vc_ref, do_ref, p_ref, cos_ref, sin_ref, cos_prev, sin_prev,
             dq_ref, dkv_ref, dkc_ref, dvc_ref, dsink_ref, done, ahead):
        n = pl.program_id(0)

        @pl.when(n == 0)
        def _():
            dkc_ref[...] = jnp.zeros_like(dkc_ref)
            dvc_ref[...] = jnp.zeros_like(dvc_ref)
            dsink_ref[...] = jnp.zeros_like(dsink_ref)
            done[...] = jnp.zeros_like(done)
            ahead[...] = jnp.zeros_like(ahead)

        def write_block(dkv):
            dkv_ref[:, :KV_WIDTH] = _rope(dkv[:, :KV_WIDTH], cos_prev[...], -sin_prev[...]).astype(BF16)
            dkv_ref[:, KV_WIDTH:] = dkv[:, KV_WIDTH:].astype(BF16)

        def query_block():
            qb, dob = q_ref[...], do_ref[...]
            kall = jnp.concatenate([kp[...], k0[...], kn[...], kc_ref[...]], axis=0)
            vall = jnp.concatenate([vp[...], v0[...], vn[...], vc_ref[...]], axis=0)
            srow = lax.broadcasted_iota(jnp.int32, (8, 128), 0)
            slane = lax.broadcasted_iota(jnp.int32, (8, 128), 1)
            dqs, dks, dvs = [], [], []
            dsink = jnp.zeros((8, 128), F32)
            for g in range(N_KV_HEADS):
                lanes = slice(HEAD_DIM * g, HEAD_DIM * (g + 1))
                kg, vg = kall[:, lanes], vall[:, lanes]
                qg, dog = _group_rows(qb, g), _group_rows(dob, g)
                pb = p_ref[0, g, :, :n_keys]
                p = pb.astype(F32)
                dp = _dot_nt(dog, vg)
                delta = jnp.sum(p * dp, axis=-1, keepdims=True)
                ds = (p * (dp - delta)).astype(BF16)
                dq = _dot(ds, kg) * SCALE
                dqs += [dq[BLOCK * j:BLOCK * (j + 1)] for j in range(GROUP)]
                dks.append(_dot_tn(ds, qg))
                dvs.append(_dot_tn(pb, dog))
                d_sink = p_ref[0, g, :, n_keys:].astype(F32)[:, :1] * delta
                for j in range(GROUP):
                    total = -jnp.sum(d_sink[BLOCK * j:BLOCK * (j + 1)], axis=0, keepdims=True)
                    dsink = dsink + jnp.where((srow == 0) & (slane == GROUP * g + j), total, 0.0)
            dq_ref[...] = _rope(jnp.concatenate(dqs, axis=1), cos_ref[...], -sin_ref[...]).astype(BF16)
            dkv = jnp.concatenate(dks + dvs, axis=1)
            write_block(done[...] + dkv[:BLOCK])
            done[...] = ahead[...] + dkv[BLOCK:2 * BLOCK]
            ahead[...] = dkv[2 * BLOCK:3 * BLOCK]
            dkc_ref[...] += dkv[3 * BLOCK:, :KV_WIDTH]
            dvc_ref[...] += dkv[3 * BLOCK:, KV_WIDTH:]
            dsink_ref[...] += dsink

        pl.when(n < nb)(query_block)

        @pl.when(n == nb)
        def _():
            write_block(done[...])

    here = lambda n: (jnp.minimum(n, nb - 1), 0)
    before = lambda n: (jnp.maximum(n - 1, 0), 0)
    ctx = pl.BlockSpec((n_ctx, KV_WIDTH), lambda n: (0, 0))
    return pl.pallas_call(
        body, name="attn_bwd", grid=(nb + 1,),
        in_specs=_attn_specs(nb, n_ctx)[1:] + [pl.BlockSpec((BLOCK, ATTN_WIDTH), here),
                                           pl.BlockSpec((1,) + probs.shape[1:], lambda n: (jnp.minimum(n, nb - 1), 0, 0, 0)),
                                           pl.BlockSpec((BLOCK, 128), here), pl.BlockSpec((BLOCK, 128), here),
                                           pl.BlockSpec((BLOCK, 128), before), pl.BlockSpec((BLOCK, 128), before)],
        out_specs=(pl.BlockSpec((BLOCK, ATTN_WIDTH), here), pl.BlockSpec((BLOCK, 2 * KV_WIDTH), before), ctx, ctx,
                   pl.BlockSpec((8, 128), lambda n: (0, 0))),
        out_shape=(jax.ShapeDtypeStruct((length, ATTN_WIDTH), BF16), jax.ShapeDtypeStruct((length, 2 * KV_WIDTH), BF16),
                   jax.ShapeDtypeStruct((n_ctx, KV_WIDTH), F32), jax.ShapeDtypeStruct((n_ctx, KV_WIDTH), F32),
                   jax.ShapeDtypeStruct((8, 128), F32)),
        scratch_shapes=[pltpu.VMEM((BLOCK, 2 * KV_WIDTH), F32), pltpu.VMEM((BLOCK, 2 * KV_WIDTH), F32)],
        compiler_params=_cp(1),
    )(q, k, k, k, v, v, v, kc, vc, dmix, probs, cos, sin, cos, sin)


def _shift_rows(e, s):
    n = e.shape[0]
    return e if s % n == 0 else pltpu.roll(e, (-s) % n, 0)


def _window_sum(e, w, first):
    s, n = e, 1
    while n < w:
        s = s + _shift_rows(s, n)
        n *= 2
    return _shift_rows(s, first)


def _pool_geometry(i, tm, length):
    pos = i * tm - HALO + lax.broadcasted_iota(jnp.int32, (tm + 2 * HALO, 1), 0)
    inside = (pos >= 0) & (pos < length)
    inv_counts = []
    for w in POOL_WINDOWS:
        lo = jnp.clip(pos - w // 2, 0, length)
        hi = jnp.clip(pos - w // 2 + w, 0, length)
        inv_counts.append(1.0 / jnp.maximum(hi - lo, 1).astype(F32))
    return inside, inv_counts


def _halo_specs(tm, width, length, col=0):
    per = tm // HALO
    last = length // HALO - 1
    return [pl.BlockSpec((HALO, width), lambda i: (jnp.maximum(i * per - 1, 0), col)),
            pl.BlockSpec((tm, width), lambda i: (i, col)),
            pl.BlockSpec((HALO, width), lambda i: (jnp.minimum((i + 1) * per, last), col))]


def _pooled(ext, inv_counts, tm):
    outs = []
    for g, w in enumerate(POOL_WINDOWS):
        e = ext[:, POOL_GROUP_DIM * g:POOL_GROUP_DIM * (g + 1)]
        mean = _window_sum(e, w, -(w // 2)) * inv_counts[g]
        outs.append((mean - e)[HALO:HALO + tm])
    return outs


def _pool_fwd(u, pool_w, pool_scale, mix):
    length = u.shape[0]
    tm = _pick(length, 512, 256, 128)

    def body(up, u0, un, w_ref, sc_ref, mix_ref, o_ref):
        inside, inv_counts = _pool_geometry(pl.program_id(0), tm, length)
        ext = jnp.where(inside, jnp.concatenate([up[...], u0[...], un[...]], axis=0), 0.0)
        pooled = _pooled(ext, inv_counts, tm)
        mixed = [_dot(pooled[g].astype(BF16), w_ref[g]) for g in range(len(POOL_WINDOWS))]
        o_ref[...] = (jnp.concatenate(mixed, axis=1) * sc_ref[...]).astype(BF16)

    return pl.pallas_call(
        body, name="pool_fwd", grid=(length // tm,),
        in_specs=_halo_specs(tm, POOL_WIDTH, length) + [pl.BlockSpec(pool_w.shape, lambda i: (0, 0, 0)), _row_spec(POOL_WIDTH), ANY],
        out_specs=pl.BlockSpec((tm, POOL_WIDTH), lambda i: (i, 1)),
        out_shape=jax.ShapeDtypeStruct(mix.shape, BF16), input_output_aliases={5: 0}, compiler_params=_cp(1),
    )(u, u, u, pool_w, pool_scale, mix)


def _pool_bwd(u, dmix, pool_w, pool_scale, after):
    length = u.shape[0]
    tm = _pick(length, 512, 256, 128)
    n_g = len(POOL_WINDOWS)

    def body(up, u0, un, dp_, d0, dn_, w_ref, sc_ref, after_ref, du_ref, dw_ref, dsc_ref):
        i = pl.program_id(0)

        @pl.when(i == 0)
        def _():
            dw_ref[...] = jnp.zeros_like(dw_ref)
            dsc_ref[...] = jnp.zeros_like(dsc_ref)

        inside, inv_counts = _pool_geometry(i, tm, length)
        ext = jnp.where(inside, jnp.concatenate([up[...], u0[...], un[...]], axis=0), 0.0)
        dext = jnp.where(inside, jnp.concatenate([dp_[...], d0[...], dn_[...]], axis=0).astype(F32), 0.0)
        dmixed = (dext * sc_ref[...]).astype(BF16)
        pooled = _pooled(ext, inv_counts, tm)
        dus, dscs = [], []
        for g, w in enumerate(POOL_WINDOWS):
            lanes = slice(POOL_GROUP_DIM * g, POOL_GROUP_DIM * (g + 1))
            dpooled = _dot_nt(dmixed[:, lanes], w_ref[g])
            spread = _window_sum(dpooled * inv_counts[g], w, -(w // 2 - 1))
            dus.append((spread - dpooled)[HALO:HALO + tm])
            pb = pooled[g].astype(BF16)
            dw_ref[g] += _dot_tn(pb, dmixed[HALO:HALO + tm, lanes])
            prod = dext[HALO:HALO + tm, lanes] * _dot(pb, w_ref[g])
            dscs.append(_fold8(prod))
        du_ref[...] = jnp.concatenate(dus, axis=1).astype(BF16)
        dsc_ref[...] += jnp.concatenate(dscs, axis=1)

    return pl.pallas_call(
        body, name="pool_bwd", grid=(length // tm,),
        in_specs=_halo_specs(tm, POOL_WIDTH, length) + _halo_specs(tm, POOL_WIDTH, length, col=1)
        + [pl.BlockSpec(pool_w.shape, lambda i: (0, 0, 0)), _row_spec(POOL_WIDTH), ANY],
        out_specs=(pl.BlockSpec((tm, POOL_WIDTH), lambda i: (i, 0)), pl.BlockSpec((n_g, POOL_GROUP_DIM, POOL_GROUP_DIM), lambda i: (0, 0, 0)),
                   pl.BlockSpec((8, POOL_WIDTH), lambda i: (0, 0))),
        out_shape=(jax.ShapeDtypeStruct((length, POOL_WIDTH), BF16), jax.ShapeDtypeStruct((n_g, POOL_GROUP_DIM, POOL_GROUP_DIM), F32),
                   jax.ShapeDtypeStruct((8, POOL_WIDTH), F32)),
        compiler_params=_cp(1),
    )(u, u, u, dmix, dmix, dmix, pool_w, pool_scale, after)


def _mixer_out(mix, w_out, x, g_a, nmw, sh_m, sc_m, after):
    t, d = x.shape

    def epi(mo, rs, tiles, vecs, outs, st_ref, live):
        ga, nw, sh, sc = vecs
        x1_ref, mo_ref, hm_ref = outs
        x1 = tiles[0][rs, :] + ga[...] * mo
        x1_ref[rs, :] = x1
        mo_ref[rs, :] = mo.astype(BF16)
        r = lax.rsqrt(jnp.mean(x1 * x1, axis=-1, keepdims=True) + EPS)
        hm_ref[rs, :] = (((x1 * r) * nw[...]) * (1.0 + sc[...]) + sh[...]).astype(BF16)

    return _mm_deferred("mixer_out", mix, w_out, nt=False, tm=_pick(t, 256, 128), epi=epi, tiles=(x,), vecs=(g_a, nmw, sh_m, sc_m),
                        out_tiles=(jax.ShapeDtypeStruct((t, d), F32), jax.ShapeDtypeStruct((t, d), BF16), jax.ShapeDtypeStruct((t, d), BF16)),
                        n_stats=1, after=after)[:3]


def _mlp_up(hm, w_up):
    t, d = hm.shape
    tm = _pick(t, 1024, 512, 256, 128)
    tn = 2048

    def epi(acc, ex, outs):
        outs[0][...] = jnp.square(jnp.maximum(acc[...], 0.0)).astype(BF16)

    return _mm("mlp_up", hm, w_up, nt=False, tm=tm, tn=tn, tk=d, epi=epi,
               out_shape=(jax.ShapeDtypeStruct((t, w_up.shape[1]), BF16),),
               out_specs=(pl.BlockSpec((tm, tn), lambda j, i, k: (i, j)),))[0]


def _mlp_down_loss(act, w_down, x1, target, g_m, fw, after):
    t, d = x1.shape

    def epi(dnv, rs, tiles, vecs, outs, st_ref, live):
        x1_ref, t_ref = tiles
        gm, fw_ref = vecs
        dx2_ref, ddn_ref = outs
        x2 = x1_ref[rs, :] + gm[...] * dnv
        r = lax.rsqrt(jnp.mean(x2 * x2, axis=-1, keepdims=True) + EPS)
        xh = x2 * r
        diff = xh * fw_ref[...] - t_ref[rs, :]
        dy = diff * (1.0 / d)
        dxh = dy * fw_ref[...]
        dx2 = r * (dxh - xh * jnp.mean(dxh * xh, axis=-1, keepdims=True))
        dx2_ref[rs, :] = dx2
        ddn_ref[rs, :] = (dx2 * gm[...]).astype(BF16)
        st_ref[0] += jnp.where(live, _fold8(diff * diff), 0.0)
        st_ref[1] += jnp.where(live, _fold8(dy * xh), 0.0)
        st_ref[2] += jnp.where(live, _fold8(dx2 * dnv), 0.0)

    return _mm_k_deferred("mlp_down_loss", act, w_down, nt=False, tm=_pick(t, 512, 256), tk=_pick(act.shape[1], 2048), epi=epi,
                          tiles=(x1, target), vecs=(g_m, fw), n_stats=3, after=after,
                          out_tiles=(jax.ShapeDtypeStruct((t, d), F32), jax.ShapeDtypeStruct((t, d), BF16)))


def _mlp_dx(dup, w_up, x1, dx2, mo, nmw, sc_m, g_a, after):
    t, d = x1.shape

    def epi(dh, rs, tiles, vecs, outs, st_ref, live):
        x1_ref, dx2_ref, mo_ref = tiles
        nw, sc, ga = vecs
        dx1_ref, dmi_ref = outs
        dx1 = _norm_bwd_rows(dh, x1_ref[rs, :], nw[...], sc[...], st_ref) + dx2_ref[rs, :]
        dx1_ref[rs, :] = dx1
        dmi_ref[rs, :] = (dx1 * ga[...]).astype(BF16)
        st_ref[3] += jnp.where(live, _fold8(dx1 * mo_ref[rs, :].astype(F32)), 0.0)

    return _mm_k_deferred("mlp_dx", dup, w_up, nt=True, tm=_pick(t, 512, 256), tk=_pick(dup.shape[1], 2048), epi=epi,
                          tiles=(x1, dx2, mo), vecs=(nmw, sc_m, g_a), n_stats=4, after=after,
                          out_tiles=(jax.ShapeDtypeStruct((t, d), F32), jax.ShapeDtypeStruct((t, d), BF16)))


def _mlp_dact(ddn, w_down, act):
    t, d = ddn.shape
    tm = _pick(t, 512, 256, 128)
    tn = 2048

    def epi(acc, ex, outs):
        outs[0][...] = (acc[...] * (2.0 * jnp.sqrt(ex[0][...]).astype(F32))).astype(BF16)

    tile = pl.BlockSpec((tm, tn), lambda j, i, k: (i, j))
    return _mm("mlp_dact", ddn, w_down, nt=True, tm=tm, tn=tn, tk=d, epi=epi, extras=(act,), extra_specs=[tile],
               out_shape=(jax.ShapeDtypeStruct(act.shape, BF16),), out_specs=(tile,))[0]


def _norm_bwd_rows(dh, xv, nw, sc, st_ref):
    r = lax.rsqrt(jnp.mean(xv * xv, axis=-1, keepdims=True) + EPS)
    xh = xv * r
    dy = dh * (1.0 + sc)
    st_ref[0] += _fold8(dh)
    st_ref[1] += _fold8(dh * (xh * nw))
    st_ref[2] += _fold8(dy * xh)
    dxh = dy * nw
    return r * (dxh - xh * jnp.mean(dxh * xh, axis=-1, keepdims=True))


def _mixer_dmix(dmi, w_out, after):
    t, d = dmi.shape
    tm = _pick(t, 512, 256, 128)

    def epi(acc, ex, outs):
        outs[0][...] = acc[...].astype(BF16)

    n = w_out.shape[0]
    return _mm("mixer_dmix", dmi, w_out, nt=True, tm=tm, tn=n, tk=d, epi=epi, after=after,
               out_shape=(jax.ShapeDtypeStruct((t, n), BF16),), out_specs=(pl.BlockSpec((tm, n), lambda j, i, k: (i, 0)),))[0]


def _mixer_dx(name, dp, w_in, x, dx1, naw, sc_a, after):
    t, d = x.shape

    def epi(dh, rs, tiles, vecs, outs, st_ref, live):
        x_ref, dx1_ref = tiles
        nw, sc = vecs
        outs[0][rs, :] = _norm_bwd_rows(dh, x_ref[rs, :], nw[...], sc[...], st_ref) + dx1_ref[rs, :]

    return _mm_deferred(name, dp, w_in, nt=True, tm=_pick(t, 256, 128), epi=epi, tiles=(x, dx1), vecs=(naw, sc_a),
                        out_tiles=(jax.ShapeDtypeStruct((t, d), F32),), n_stats=3, after=after, vmem_mib=56)


def _silu(v):
    return v / (1.0 + jnp.exp(-v))


def _ada_fwd(cond, w_ada, b_ada):
    d, n = w_ada.shape
    tn = 512

    def body(c_ref, w_ref, b_ref, o_ref):
        o_ref[...] = _dot(_silu(c_ref[...]).astype(BF16), w_ref[...].astype(BF16)) + b_ref[...]

    return pl.pallas_call(
        body, name="ada_fwd", grid=(n // tn,),
        in_specs=[pl.BlockSpec(cond.shape, lambda j: (0, 0)), pl.BlockSpec((d, tn), lambda j: (0, j)), pl.BlockSpec((1, tn), lambda j: (0, j))],
        out_specs=pl.BlockSpec((cond.shape[0], tn), lambda j: (0, j)), out_shape=jax.ShapeDtypeStruct((cond.shape[0], n), F32),
        compiler_params=_cp(1),
    )(cond, w_ada, b_ada)


def _adamw_math(w, g, m, v):
    m = ADAM_B1 * m + (1.0 - ADAM_B1) * g
    v = ADAM_B2 * v + (1.0 - ADAM_B2) * jnp.square(g)
    m_hat = m / (1.0 - ADAM_B1 ** ADAM_STEP)
    v_hat = v / (1.0 - ADAM_B2 ** ADAM_STEP)
    return -ADAM_LR * (m_hat / (jnp.sqrt(v_hat) + ADAM_EPS) + ADAM_WD * w), m, v


def _ada_bwd(cond, dm, w_ada, m_ada, v_ada):
    d, n = w_ada.shape
    tn = 256
    rows = cond.shape[0]

    def body(c_ref, dm_ref, w_ref, m_ref, v_ref, g_ref, dl_ref, nm_ref, nv_ref, pc_ref):
        @pl.when(pl.program_id(0) == 0)
        def _():
            pc_ref[...] = jnp.zeros_like(pc_ref)

        dmb = dm_ref[...].astype(BF16)
        w = w_ref[...]
        g = _dot_tn(_silu(c_ref[...]).astype(BF16), dmb)
        g_ref[...] = g
        dl_ref[...], nm_ref[...], nv_ref[...] = _adamw_math(w, g, m_ref[...], v_ref[...])
        pc_ref[...] += _dot_nt(dm_ref[8:16, :].astype(BF16), w.astype(BF16))

    tile = pl.BlockSpec((d, tn), lambda j: (0, j))
    like = jax.ShapeDtypeStruct((d, n), F32)
    return pl.pallas_call(
        body, name="ada_bwd", grid=(n // tn,),
        in_specs=[pl.BlockSpec((rows, d), lambda j: (0, 0)), pl.BlockSpec((rows, tn), lambda j: (0, j)), tile, tile, tile],
        out_specs=(tile, tile, tile, tile, pl.BlockSpec((8, d), lambda j: (0, 0))),
        out_shape=(like, like, like, like, jax.ShapeDtypeStruct((8, d), F32)), compiler_params=_cp(1),
    )(cond, dm, w_ada, m_ada, v_ada)


def _adamw(name, w, g, m, v, after=None):
    return _ew(name, lambda w_, g_, m_, v_: (g_,) + _adamw_math(w_, g_, m_, v_), [w, g, m, v], [F32, F32, F32, F32],
               g if after is None else after)


def _colsum(st):
    return jnp.sum(st, axis=1)


def kernel(x, c, ctx, c_ctx, norm_attn_w, norm_mlp_w, w_ada, b_ada, w_in, attn_sink, pool_w, pool_scale, w_out, w_mlp_up, w_mlp_down, final_norm_w, loss_target, m_c_ctx, m_norm_attn_w, m_norm_mlp_w, m_w_ada, m_b_ada, m_w_in, m_attn_sink, m_pool_w, m_pool_scale, m_w_out, m_w_mlp_up, m_w_mlp_down, m_final_norm_w, v_c_ctx, v_norm_attn_w, v_norm_mlp_w, v_w_ada, v_b_ada, v_w_in, v_attn_sink, v_pool_w, v_pool_scale, v_w_out, v_w_mlp_up, v_w_mlp_down, v_final_norm_w):
    length, d = x.shape[1], x.shape[2]
    n_ctx = ctx.shape[1]
    pos = (lax.axis_index("x"), lax.axis_index("y"), lax.axis_index("c"))
    me, chip = _dev_index(pos), _chip_index(pos)
    xs, tgt, cx = x.reshape(length, d), loss_target.reshape(length, d), ctx.reshape(n_ctx, d)
    n_ada = w_ada.shape[2]

    c_all = _allgather8("gather_c", jnp.pad(c, ((0, 7), (0, 0))))
    mixer_bigs = [_Big("col", w_in.shape[1:]), _Big("pool", pool_w.shape[1:]), _Big("row", w_out.shape[1:])]
    mlp_bigs = [_Big("col", w_mlp_up.shape[1:]), _Big("row", w_mlp_down.shape[1:])]
    placed = [_cast_place(f"place_{i}", b, s, c_all)[0] for i, (b, s) in enumerate(zip(mixer_bigs, [w_in[0], pool_w[0], w_out[0]]))]
    flight = _split("gather_mixer_ici", placed, _gather_ici_remote(mixer_bigs, 0))
    token, placed_mlp = flight[3], []
    for i, (b, s) in enumerate(zip(mlp_bigs, [w_mlp_up[0], w_mlp_down[0]])):
        p, token = _cast_place(f"place_mlp_{i}", b, s, token)
        placed_mlp.append(p)
    cos, sin = _rope_tables(length, True, token[0, 0])
    cond = jnp.concatenate([c_all[:, 0, :], jnp.pad(c_ctx[None, :], ((0, 7), (0, 0)))], axis=0) + 0.0 * cos[0, 0]
    b_shard = lax.dynamic_slice_in_dim(b_ada, chip * n_ada, n_ada, axis=1)
    mod_all = _allgather8("gather_mod", _ada_fwd(cond, w_ada[0], b_shard))
    mod = jnp.concatenate([mod_all[0], mod_all[2], mod_all[4], mod_all[6]], axis=1)
    mine = lax.dynamic_slice_in_dim(mod, me, 1, axis=0)
    sh_a, sc_a, g_a, sh_m, sc_m, g_m = [mine[:, d * i:d * (i + 1)] for i in range(6)]
    csh_a, csc_a = mod[8:9, :d], mod[8:9, d:2 * d]

    win_b, pw_b, wout_b = _exchange("gather_mixer_d2d", _join(flight, mod), [jax.ShapeDtypeStruct(b.full_shape, BF16) for b in mixer_bigs],
                                    _gather_d2d_remote(mixer_bigs, 3), aliases={0: 0, 1: 1, 2: 2})
    wout_b = wout_b.reshape(-1, d)
    flight = _split("gather_mlp_ici", placed_mlp, _gather_ici_remote(mlp_bigs, 0), after=pw_b)

    one, zero = _rope_tables(n_ctx, False)
    h, q, k, v, u = _mixer_in("mixer_in", xs, norm_attn_w, sh_a, sc_a, win_b, cos, sin, flight[3])
    hc, _, kc, vc, _ = _mixer_in("mixer_in_ctx", cx, norm_attn_w, csh_a, csc_a, win_b, one, zero, flight[3])
    attn, probs = _attn_fwd(q, k, v, kc, vc, attn_sink)
    mix = _pool_fwd(u, pw_b, pool_scale, attn)
    flight = _split("gather_mlp_d2d", _join(flight, mix), _gather_d2d_remote(mlp_bigs, 0))
    x1, mo, hm = _mixer_out(mix, wout_b, xs, g_a, norm_mlp_w, sh_m, sc_m, flight[3])
    wup_b, wdn_b = _join(flight, hm)
    wdn_b = wdn_b.reshape(-1, d)
    act = _mlp_up(hm, wup_b)
    dx2, ddn, st_loss = _mlp_down_loss(act, wdn_b, x1, tgt, g_m, final_norm_w[None, :], c)
    st_loss = _colsum(st_loss)

    tt = _pick(length, 2048, 1024, 512, 256, 128)
    g_wdn = _mm_tn("grad_w_down", act, ddn, BF16, tmo=1024, tn=d, tt=tt)
    dup = _mlp_dact(ddn, wdn_b, act)
    g_wup = _mm_tn("grad_w_up", hm, dup, BF16, tmo=d, tn=1024, tt=tt)
    empty = lambda shapes: [lax.empty(s.shape, s.dtype) for s in shapes]
    grads = [g_wup, g_wdn.reshape(mlp_bigs[1].full_shape)]
    flight = _split("reduce_mlp_d2d", grads + empty(_halves(mlp_bigs)), _reduce_d2d_remote(mlp_bigs))
    dx1, dmi, st_mlp = _mlp_dx(dup, wup_b, x1, dx2, mo, norm_mlp_w, sc_m, g_a, flight[3])
    st_mlp = _colsum(st_mlp)
    landed = _join(flight, dmi)
    mlp_chip = _chip_sums("mlp", mlp_bigs, landed[:2], landed[2:])
    flight = _split("reduce_mlp_ici", mlp_chip + empty(_thirds(mlp_bigs)), _reduce_ici_remote(mlp_bigs))
    g_wout = _mm_tn("grad_w_out", mix, dmi, BF16, tmo=1024, tn=d, tt=tt)
    dmix = _mixer_dmix(dmi, wout_b, flight[3])
    dq, dkv, dkc, dvc, dsink = _attn_bwd(q, k, v, kc, vc, dmix, probs, cos, sin)
    landed = _join(flight, dq)
    flight = _split("reduce_mlp_share", _piece_sums("mlp", mlp_bigs, landed[:2], landed[2:]), _share_remote(mlp_bigs, 0))
    du, g_pw, st_pool = _pool_bwd(u, dmix, pw_b, pool_scale, flight[3])
    g_mlp = _join(flight, du)

    wo_bigs, win_bigs = mixer_bigs[1:], mixer_bigs[:1]
    wo_chip = _reduce_to_chip("wo", wo_bigs, [g_pw.astype(BF16), g_wout.reshape(wo_bigs[1].full_shape)])
    flight = _split("reduce_wo_ici", wo_chip + empty(_thirds(wo_bigs)), _reduce_ici_remote(wo_bigs))
    dkv_ctx = jnp.concatenate([dkc.astype(BF16), dvc.astype(BF16)], axis=1)
    at_kv, at_u = ATTN_WIDTH, ATTN_WIDTH + 2 * KV_WIDTH
    tt_in = _pick(length, 1024, 512, 256, 128)
    g_win = jnp.concatenate([_mm_tn("grad_w_in_q", h, dq, BF16, tmo=d, tn=ATTN_WIDTH, tt=tt_in, after=flight[3]),
                             _mm_tn("grad_w_in_kv", h, dkv, BF16, tmo=d, tn=2 * KV_WIDTH, tt=tt_in, more=(hc, dkv_ctx)),
                             _mm_tn("grad_w_in_u", h, du, BF16, tmo=d, tn=POOL_WIDTH, tt=tt_in)], axis=1)
    wo_landed = _join(flight, g_win)
    win_chip = _reduce_to_chip("win", win_bigs, [g_win])
    flight = _split("reduce_win_ici", win_chip + empty(_thirds(win_bigs)), _reduce_ici_remote(win_bigs))
    grad_x, st_mix = _mixer_dx("mixer_dx", [(dq, 0), (dkv, at_kv), (du, at_u)], win_b, xs, dx1, norm_attn_w, sc_a, flight[3])
    _, st_ctx = _mixer_dx("mixer_dx_ctx", [(dkv_ctx, at_kv)], win_b, cx, jnp.zeros((n_ctx, d), F32), norm_attn_w, csc_a, flight[3])
    st_mix, st_ctx = _colsum(st_mix), _colsum(st_ctx)
    win_landed = _join(flight, grad_x)
    g_mixer = (_reduce_finish("win", win_bigs, win_landed[:1], win_landed[1:])
               + _reduce_finish("wo", wo_bigs, wo_landed[:2], wo_landed[2:]))

    zrow = jnp.zeros((d,), F32)
    pad = lambda a: jnp.pad(a, (0, d - a.shape[0]))
    mine_rows = [st_mix[0], st_mix[1], st_mlp[3], st_mlp[0], st_mlp[1], st_loss[2],
                 st_ctx[0], st_ctx[1],
                 st_mix[2] + st_ctx[2], st_mlp[2], st_loss[1],
                 pad(jnp.sum(st_pool, axis=0)), pad(dsink[0, :N_Q_HEADS]), st_loss[0]] + [zrow] * 2
    flight = _allgather8_split("gather_small", jnp.concatenate(mine_rows).reshape(len(mine_rows), d), me)
    res = {"w_mlp_up": tuple(_adamw("adamw_w_mlp_up", w_mlp_up, g_mlp[0].reshape(w_mlp_up.shape), m_w_mlp_up, v_w_mlp_up, flight[3]))}
    small_all = _join(flight, res["w_mlp_up"][1])[1]
    small = small_all[0]
    for i in range(1, 8):
        small = small + small_all[i]
    loss = 0.5 / d * jnp.sum(small[13])
    dm_rows = small_all[:, 0:6, :].reshape(8, 6 * d)
    dm_ctx = jnp.concatenate([small[6], small[7], jnp.zeros((4 * d,), F32)])[None, :]
    dm = jnp.concatenate([dm_rows, jnp.pad(dm_ctx, ((0, 7), (0, 0)))], axis=0)
    g_bada = jnp.sum(dm[:9], axis=0, keepdims=True)
    dm_shard = lax.dynamic_slice_in_dim(dm, chip * n_ada, n_ada, axis=1)
    g_wada, dl_wada, nm_wada, nv_wada, part_cctx = _ada_bwd(cond, dm_shard, w_ada[0], m_w_ada[0], v_w_ada[0])
    flight = _allgather8_split("gather_cctx", part_cctx, me)
    res["w_mlp_down"] = tuple(_adamw("adamw_w_mlp_down", w_mlp_down, g_mlp[1].reshape(w_mlp_down.shape), m_w_mlp_down,
                                     v_w_mlp_down, flight[3]))
    cctx_all = _join(flight, res["w_mlp_down"][1])[1]
    dsilu_in = cctx_all[0, 0] + cctx_all[2, 0] + cctx_all[4, 0] + cctx_all[6, 0]
    sig = 1.0 / (1.0 + jnp.exp(-c_ctx))
    g_cctx = dsilu_in * (sig * (1.0 + c_ctx * (1.0 - sig)))

    for nm, w_, g_, m_, v_ in zip(["w_in", "pool_w", "w_out"], [w_in, pool_w, w_out], g_mixer,
                                  [m_w_in, m_pool_w, m_w_out], [v_w_in, v_pool_w, v_w_out]):
        res[nm] = tuple(_adamw("adamw_" + nm, w_, g_.reshape(w_.shape), m_, v_))
    res["w_ada"] = (g_wada[None], dl_wada[None], nm_wada[None], nv_wada[None])

    def pack(cc, na, nm_, ba, sk, ps, fn):
        flat = [cc.reshape(-1), na.reshape(-1), nm_.reshape(-1), ba.reshape(-1), pad(sk.reshape(-1)), pad(ps.reshape(-1)),
                fn.reshape(-1), jnp.zeros((4 * d,), F32)]
        return jnp.concatenate(flat).reshape(16, d)

    w_s = pack(c_ctx, norm_attn_w, norm_mlp_w, b_ada, attn_sink, pool_scale, final_norm_w)
    m_s = pack(m_c_ctx, m_norm_attn_w, m_norm_mlp_w, m_b_ada, m_attn_sink, m_pool_scale, m_final_norm_w)
    v_s = pack(v_c_ctx, v_norm_attn_w, v_norm_mlp_w, v_b_ada, v_attn_sink, v_pool_scale, v_final_norm_w)
    g_s = pack(g_cctx, small[8], small[9], g_bada, small[12][:N_Q_HEADS], small[11][:POOL_WIDTH], small[10])
    small_out = _adamw("adamw_small", w_s, g_s, m_s, v_s)

    def unpack(p):
        return {"c_ctx": p[0], "norm_attn_w": p[1:2], "norm_mlp_w": p[2:3], "b_ada": p[3:9].reshape(1, 6 * d),
                "attn_sink": p[9:10, :N_Q_HEADS], "pool_scale": p[10:11, :POOL_WIDTH], "final_norm_w": p[11]}

    small_res = [unpack(p) for p in small_out]
    order = ["c_ctx", "norm_attn_w", "norm_mlp_w", "w_ada", "b_ada", "w_in", "attn_sink", "pool_w", "pool_scale",
             "w_out", "w_mlp_up", "w_mlp_down", "final_norm_w"]
    outs = [loss, grad_x.reshape(x.shape)]
    for kind in range(4):
        for nm in order:
            outs.append(res[nm][kind] if nm in res else small_res[kind][nm])
    return tuple(outs)
```

```python
import functools

import jax
import jax.numpy as jnp
from jax import lax
from jax.experimental import pallas as pl
from jax.experimental.pallas import tpu as pltpu

F32 = jnp.float32
BF16 = jnp.bfloat16
EPS = 1e-6
NEG_INF = -1e30
HEAD_DIM = 64
N_Q_HEADS = 16
N_KV_HEADS = 4
GROUP = N_Q_HEADS // N_KV_HEADS
ATTN_WIDTH = N_Q_HEADS * HEAD_DIM
KV_WIDTH = N_KV_HEADS * HEAD_DIM
POOL_WINDOWS = (2, 4, 8, 16)
POOL_GROUP_DIM = 256
POOL_WIDTH = len(POOL_WINDOWS) * POOL_GROUP_DIM
BLOCK = 128
GRID_W = 64
ROPE_BASE = 10000.0
SCALE = HEAD_DIM ** -0.5
HALO = 16
STRIP = 16
ADAM_LR, ADAM_B1, ADAM_B2, ADAM_EPS, ADAM_WD, ADAM_STEP = 0.001, 0.9, 0.999, 1e-08, 0.01, 10
MESH = pl.DeviceIdType.MESH
MIB = 1024 * 1024
ANY = pl.BlockSpec(memory_space=pl.ANY)


def _cp(n_axes, vmem_mib=48):
    return pltpu.CompilerParams(dimension_semantics=("arbitrary",) * n_axes, vmem_limit_bytes=vmem_mib * MIB)


def _fold8(v):
    s = v[0:8]
    for t in range(1, v.shape[0] // 8):
        s = s + v[8 * t:8 * t + 8]
    return s


def _dot(a, b):
    return jnp.dot(a, b, preferred_element_type=F32)


def _dot_nt(a, b):
    return lax.dot_general(a, b, (((1,), (1,)), ((), ())), preferred_element_type=F32)


def _dot_tn(a, b):
    return lax.dot_general(a, b, (((0,), (0,)), ((), ())), preferred_element_type=F32)


def _pick(n, *cands):
    for t in cands:
        if n % t == 0:
            return t
    return n


def _flip(pos, mask):
    return tuple((1 - v) if (mask >> (2 - i)) & 1 else v for i, v in enumerate(pos))


def _exchange(name, ins, out_shapes, remote, local=(), aliases=None):
    n_io = len(ins) + len(out_shapes)

    def body(*refs):
        io = refs[:n_io]
        send_sems, recv_sems, local_sems = refs[n_io:]
        me = (lax.axis_index("x"), lax.axis_index("y"), lax.axis_index("c"))

        def copy(i, sender):
            mask, src_fn, dst_fn = remote[i]
            return pltpu.make_async_remote_copy(
                src_ref=src_fn(io, sender), dst_ref=dst_fn(io, sender), send_sem=send_sems.at[i],
                recv_sem=recv_sems.at[i], device_id=_flip(sender, mask), device_id_type=MESH)

        own = [pltpu.make_async_copy(s(io, me), d(io, me), local_sems.at[i]) for i, (s, d) in enumerate(local)]
        for cp in own:
            cp.start()
        sends = [copy(i, me) for i in range(len(remote))]
        for cp in sends:
            cp.start()
        for i in range(len(remote)):
            copy(i, _flip(me, remote[i][0])).wait_recv()
        for cp in sends:
            cp.wait_send()
        for cp in own:
            cp.wait()

    return pl.pallas_call(
        body, name=name, out_shape=tuple(out_shapes),
        in_specs=[ANY] * len(ins), out_specs=tuple([ANY] * len(out_shapes)),
        scratch_shapes=[pltpu.SemaphoreType.DMA((len(remote),)), pltpu.SemaphoreType.DMA((len(remote),)),
                        pltpu.SemaphoreType.DMA((max(len(local), 1),))],
        input_output_aliases=aliases or {},
    )(*ins)


HBM = pl.BlockSpec(memory_space=pltpu.HBM)
SEM = pl.BlockSpec(memory_space=pltpu.SEMAPHORE)
EFFECT = pltpu.SideEffectType.DATAFLOW_SIDE_EFFECTING


def _split_copy(remote, i, io, send_sems, recv_sems, sender):
    mask, src_fn, dst_fn = remote[i]
    return pltpu.make_async_remote_copy(
        src_ref=src_fn(io, sender), dst_ref=dst_fn(io, sender), send_sem=send_sems.at[i],
        recv_sem=recv_sems.at[i], device_id=_flip(sender, mask), device_id_type=MESH)


def _exchange_start(name, bufs, remote, after=None):
    n, r = len(bufs), len(remote)
    more = [] if after is None else [after]

    def body(*refs):
        io, (send_sems, recv_sems, token) = refs[:n], refs[-3:]
        me = (lax.axis_index("x"), lax.axis_index("y"), lax.axis_index("c"))
        for i in range(r):
            _split_copy(remote, i, io, send_sems, recv_sems, me).start()
        token[...] = jnp.zeros_like(token)

    res = pl.pallas_call(
        body, name=name,
        out_shape=tuple(pltpu.HBM(b.shape, b.dtype) for b in bufs)
        + (pltpu.SemaphoreType.DMA((r,)), pltpu.SemaphoreType.DMA((r,)), jax.ShapeDtypeStruct((8, 128), F32)),
        in_specs=[HBM] * n + [ANY] * len(more), out_specs=tuple([HBM] * n) + (SEM, SEM, pl.BlockSpec(memory_space=pltpu.VMEM)),
        input_output_aliases={i: i for i in range(n)}, compiler_params=pltpu.CompilerParams(has_side_effects=EFFECT),
    )(*[pltpu.with_memory_space_constraint(b, pltpu.HBM) for b in bufs], *more)
    return list(res[:n]), res[n], res[n + 1], res[n + 2]


def _exchange_wait(name, bufs, send_sems, recv_sems, remote, after):
    n, r = len(bufs), len(remote)

    def body(*refs):
        io, ss, rs = refs[:n], refs[n], refs[n + 1]
        me = (lax.axis_index("x"), lax.axis_index("y"), lax.axis_index("c"))
        for i in range(r):
            _split_copy(remote, i, io, ss, rs, _flip(me, remote[i][0])).wait_recv()
        for i in range(r):
            _split_copy(remote, i, io, ss, rs, me).wait_send()

    return list(pl.pallas_call(
        body, name=name, out_shape=tuple(pltpu.HBM(b.shape, b.dtype) for b in bufs),
        in_specs=[HBM] * n + [SEM, SEM, ANY], out_specs=tuple([HBM] * n),
        input_output_aliases={i: i for i in range(n)}, compiler_params=pltpu.CompilerParams(has_side_effects=EFFECT),
    )(*bufs, send_sems, recv_sems, after))


def _my_c():
    return lax.axis_index("c")


def _my_chip():
    return 2 * lax.axis_index("x") + lax.axis_index("y")


def _dev_index(pos):
    return 4 * pos[0] + 2 * pos[1] + pos[2]


def _chip_index(pos):
    return 2 * pos[0] + pos[1]


def _allgather8(name, v):
    out = jax.ShapeDtypeStruct((8,) + v.shape, v.dtype)
    remote = [(mask, lambda io, pos: io[0], lambda io, pos: io[1].at[_dev_index(pos)]) for mask in range(1, 8)]
    local = [(lambda io, pos: io[0], lambda io, pos: io[1].at[_dev_index(pos)])]
    return _exchange(name, [v], [out], remote, local)[0]


class _Big:
    def __init__(self, kind, shard_shape):
        self.kind = kind
        self.shard_shape = tuple(shard_shape)
        if kind == "col":
            r, cs = shard_shape
            self.full_shape = (r, 4 * cs)
            self.piece_shape = (r // 2, cs)
            self.half_shape = (r // 2, 4 * cs)
        elif kind == "row":
            rs, c = shard_shape
            self.full_shape = (4, 2, rs // 2, c)
            self.piece_shape = (1, 1, rs // 2, c)
            self.half_shape = (4, 1, rs // 2, c)
        else:
            self.full_shape = (4, 256, 256)
            self.piece_shape = (2, 64, 256)
            self.half_shape = (2, 256, 256)

    def shard_as_pieces(self, a):
        return a.reshape((1, 2) + self.piece_shape[2:]) if self.kind == "row" else a

    def piece(self, ref, k, h):
        if self.kind == "col":
            r, cs = self.piece_shape
            return ref.at[pl.ds(h * r, r), pl.ds(k * cs, cs)]
        if self.kind == "row":
            return ref.at[pl.ds(k, 1), pl.ds(h, 1)]
        return ref.at[pl.ds(2 * h, 2), pl.ds(64 * k, 64)]

    def half_of_shard(self, ref, h):
        if self.kind == "col":
            return ref.at[pl.ds(h * self.piece_shape[0], self.piece_shape[0])]
        if self.kind == "row":
            return ref.at[:, pl.ds(h, 1)]
        return ref.at[pl.ds(2 * h, 2)]

    def half_of_full(self, ref, h):
        if self.kind == "col":
            return ref.at[pl.ds(h * self.half_shape[0], self.half_shape[0])]
        if self.kind == "row":
            return ref.at[:, pl.ds(h, 1)]
        return ref.at[pl.ds(2 * h, 2)]

    def piece_of_half(self, ref, k):
        if self.kind == "col":
            return ref.at[:, pl.ds(k * self.piece_shape[1], self.piece_shape[1])]
        if self.kind == "row":
            return ref.at[pl.ds(k, 1)]
        return ref.at[:, pl.ds(64 * k, 64)]


CHIP_MASKS = (4, 2, 6)


def _cast_place(name, big, shard, after):
    if big.kind == "col":
        r, cs = big.shard_shape
        tr = _pick(r, 512, 256, 128)
        src, grid, blk = shard, (r // tr,), (tr, cs)
        imap, omap = (lambda i: (i, 0)), (lambda i: (i, _my_chip()))
    elif big.kind == "row":
        rs, c = big.shard_shape
        tr = _pick(rs // 2, 256, 128)
        src, grid, blk = big.shard_as_pieces(shard), (2, rs // 2 // tr), (1, 1, tr, c)
        imap, omap = (lambda h, i: (0, h, i, 0)), (lambda h, i: (_my_chip(), h, i, 0))
    else:
        src, grid, blk = shard, (1,), big.shard_shape
        imap, omap = (lambda i: (0, 0, 0)), (lambda i: (0, _my_chip(), 0))

    def body(s_ref, after_ref, o_ref, token_ref):
        o_ref[...] = s_ref[...].astype(BF16)
        token_ref[...] = jnp.zeros_like(token_ref)

    return pl.pallas_call(
        body, name=name, grid=grid, in_specs=[pl.BlockSpec(blk, imap), ANY],
        out_specs=(pl.BlockSpec(blk, omap), pl.BlockSpec((8, 128), lambda *_: (0, 0))),
        out_shape=(jax.ShapeDtypeStruct(big.full_shape, BF16), jax.ShapeDtypeStruct((8, 128), F32)), compiler_params=_cp(len(grid)),
    )(src, after)


def _gather_ici_remote(bigs, off):
    remote = []
    for a, b in enumerate(bigs):
        for mask in CHIP_MASKS:
            def mine(io, p, a=a, b=b):
                return b.piece(io[off + a], _chip_index(p), p[2])
            remote.append((mask, mine, mine))
    return remote


def _gather_d2d_remote(bigs, off):
    remote = []
    for a, b in enumerate(bigs):
        for mask in CHIP_MASKS:
            def region(io, p, a=a, b=b, mask=mask):
                return b.piece(io[off + a], _chip_index(_flip(p, mask)), p[2])
            remote.append((1, region, region))
    return remote


def _ew(name, fn, ins, out_dtypes, after, rows_per_step=256):
    shape = ins[0].shape
    last = shape[-1]
    rows = 1
    for s in shape[:-1]:
        rows *= s
    ins2 = [a.reshape(rows, last) for a in ins]
    tr = _pick(rows, rows_per_step, 128, 64, 32, 16, 8)
    spec = pl.BlockSpec((tr, last), lambda i: (i, 0))

    def body(*refs):
        outs = fn(*[r[...] for r in refs[:len(ins)]])
        for o_ref, o in zip(refs[len(ins) + 1:], outs):
            o_ref[...] = o.astype(o_ref.dtype)

    outs = pl.pallas_call(
        body, name=name, grid=(rows // tr,), in_specs=[spec] * len(ins) + [ANY], out_specs=tuple([spec] * len(out_dtypes)),
        out_shape=tuple(jax.ShapeDtypeStruct((rows, last), d) for d in out_dtypes), compiler_params=_cp(1),
    )(*ins2, after)
    return [o.reshape(shape) for o in outs]


def _chip_sum(name, big, grad, from_sibling):
    if big.kind == "col":
        rh, w = big.half_shape
        tr = _pick(rh, 256, 128)
        nb = rh // tr
        grid, blk = (nb,), (tr, w)
        gmap, hmap = (lambda i: (_my_c() * nb + i, 0)), (lambda i: (i, 0))
    elif big.kind == "row":
        rh, w = big.half_shape[2:]
        tr = _pick(rh, 256, 128)
        grid, blk = (4, rh // tr), (1, 1, tr, w)
        gmap, hmap = (lambda k, i: (k, _my_c(), i, 0)), (lambda k, i: (k, 0, i, 0))
    else:
        grid, blk = (1,), big.half_shape
        gmap, hmap = (lambda i: (_my_c(), 0, 0)), (lambda i: (0, 0, 0))

    def body(g_ref, s_ref, o_ref):
        o_ref[...] = (g_ref[...].astype(F32) + s_ref[...].astype(F32)).astype(BF16)

    return pl.pallas_call(
        body, name=name, grid=grid, in_specs=[pl.BlockSpec(blk, gmap), pl.BlockSpec(blk, hmap)],
        out_specs=pl.BlockSpec(blk, hmap), out_shape=jax.ShapeDtypeStruct(big.half_shape, BF16), compiler_params=_cp(len(grid)),
    )(grad, from_sibling)


def _piece_sum(name, big, chip_sum, thirds):
    if big.kind == "col":
        rp, cs = big.piece_shape
        tr = _pick(rp, 256, 128)
        nb = rp // tr
        grid, blk, tblk = (nb,), (tr, cs), (1, tr, cs)
        smap, omap = (lambda i: (i, _my_chip())), (lambda i: (_my_c() * nb + i, 0))
        tmap = lambda j: (lambda i: (j, i, 0))
        out_shape = big.shard_shape
    elif big.kind == "row":
        rp, w = big.piece_shape[2:]
        tr = _pick(rp, 256, 128)
        grid, blk, tblk = (rp // tr,), (1, 1, tr, w), (1, 1, 1, tr, w)
        smap, omap = (lambda i: (_my_chip(), 0, i, 0)), (lambda i: (0, _my_c(), i, 0))
        tmap = lambda j: (lambda i: (j, 0, 0, i, 0))
        out_shape = (1, 2, rp, w)
    else:
        grid, blk, tblk = (1,), big.piece_shape, (1,) + big.piece_shape
        smap, omap = (lambda i: (0, _my_chip(), 0)), (lambda i: (_my_c(), 0, 0))
        tmap = lambda j: (lambda i: (j, 0, 0, 0))
        out_shape = big.shard_shape

    def body(s_ref, t0, t1, t2, o_ref):
        o_ref[...] = s_ref[...].astype(F32) + t0[0].astype(F32) + t1[0].astype(F32) + t2[0].astype(F32)

    return pl.pallas_call(
        body, name=name, grid=grid,
        in_specs=[pl.BlockSpec(blk, smap)] + [pl.BlockSpec(tblk, tmap(j)) for j in range(3)],
        out_specs=pl.BlockSpec(blk, omap), out_shape=jax.ShapeDtypeStruct(out_shape, F32), compiler_params=_cp(len(grid)),
    )(chip_sum, thirds, thirds, thirds)


def _split(name, bufs, remote, after=None):
    return _exchange_start(name + "_start", bufs, remote, after) + (remote, name)


def _join(handle, after):
    bufs, send_sems, recv_sems, _, remote, name = handle
    return _exchange_wait(name + "_wait", bufs, send_sems, recv_sems, remote, after)


def _allgather8_split(name, v, me):
    own = lax.dynamic_update_slice(lax.empty((8,) + v.shape, v.dtype), v[None], (me, 0, 0))
    remote = [(mask, lambda io, pos: io[0], lambda io, pos: io[1].at[_dev_index(pos)]) for mask in range(1, 8)]
    return _split(name, [v, own], remote)


def _reduce_d2d_remote(bigs):
    n = len(bigs)
    return [(1, lambda io, p, a=a, b=b: b.half_of_full(io[a], 1 - p[2]), lambda io, p, a=a: io[n + a])
            for a, b in enumerate(bigs)]


def _halves(bigs):
    return [jax.ShapeDtypeStruct(b.half_shape, BF16) for b in bigs]


def _chip_sums(tag, bigs, grads, from_sibling):
    return [_chip_sum(f"reduce_{tag}_chip_sum_{a}", b, g, r) for a, (b, g, r) in enumerate(zip(bigs, grads, from_sibling))]


def _reduce_to_chip(tag, bigs, grads):
    from_sibling = _exchange(f"reduce_{tag}_d2d", grads, _halves(bigs), _reduce_d2d_remote(bigs))
    return _chip_sums(tag, bigs, grads, from_sibling)


def _reduce_ici_remote(bigs):
    n = len(bigs)
    remote = []
    for a, b in enumerate(bigs):
        for j, mask in enumerate(CHIP_MASKS):
            remote.append((mask,
                           lambda io, p, a=a, b=b, mask=mask: b.piece_of_half(io[a], _chip_index(_flip(p, mask))),
                           lambda io, p, a=a, j=j: io[n + a].at[j]))
    return remote


def _thirds(bigs):
    return [jax.ShapeDtypeStruct((3,) + b.piece_shape, BF16) for b in bigs]


def _piece_sums(tag, bigs, chip_sum, from_chips):
    return [_piece_sum(f"reduce_{tag}_sum_{a}", b, s, r) for a, (b, s, r) in enumerate(zip(bigs, chip_sum, from_chips))]


def _share_remote(bigs, off):
    remote = []
    for a, b in enumerate(bigs):
        def mine(io, p, a=a, b=b):
            return b.half_of_shard(io[off + a], p[2])
        remote.append((1, mine, mine))
    return remote


def _reduce_finish(tag, bigs, chip_sum, from_chips):
    n = len(bigs)
    placed = _piece_sums(tag, bigs, chip_sum, from_chips)
    out = _exchange(f"reduce_{tag}_share_d2d", placed, [jax.ShapeDtypeStruct(p.shape, F32) for p in placed],
                    _share_remote(bigs, n), aliases={a: a for a in range(n)})
    return [o.reshape(b.shard_shape) for o, b in zip(out, bigs)]


def _mm(name, a, b, *, nt, tm, tn, tk, epi, extras=(), extra_specs=(), out_shape, out_specs, after=None, vmem_mib=48):
    m, kdim = a.shape
    n = b.shape[0] if nt else b.shape[1]
    gm, gn, gk = m // tm, n // tn, kdim // tk
    a_spec = pl.BlockSpec((tm, tk), lambda j, i, k: (i, k))
    b_spec = pl.BlockSpec((tn, tk), lambda j, i, k: (j, k)) if nt else pl.BlockSpec((tk, tn), lambda j, i, k: (k, j))
    n_ex = len(extras)
    if after is not None:
        extras, extra_specs = tuple(extras) + (after,), list(extra_specs) + [ANY]

    def body(a_ref, b_ref, *rest):
        ex, outs, acc = rest[:n_ex], rest[len(extras):-1], rest[-1]
        dot = _dot_nt if nt else _dot
        if gk == 1:
            acc[...] = dot(a_ref[...], b_ref[...])
            epi(acc, ex, outs)
        else:
            k = pl.program_id(2)

            @pl.when(k == 0)
            def _():
                acc[...] = dot(a_ref[...], b_ref[...])

            @pl.when(k > 0)
            def _():
                acc[...] += dot(a_ref[...], b_ref[...])

            @pl.when(k == gk - 1)
            def _():
                epi(acc, ex, outs)

    return pl.pallas_call(
        body, name=name, grid=(gn, gm, gk), in_specs=[a_spec, b_spec, *extra_specs], out_specs=tuple(out_specs),
        out_shape=tuple(out_shape), scratch_shapes=[pltpu.VMEM((tm, tn), F32)], compiler_params=_cp(3, vmem_mib),
    )(a, b, *extras)


def _mm_deferred(name, a, b, *, nt, tm, epi, tiles, vecs, out_tiles, n_stats, after, vmem_mib=48):
    pieces = a if isinstance(a, (list, tuple)) else [(a, 0)]
    m = pieces[0][0].shape[0]
    n = b.shape[0] if nt else b.shape[1]
    gm = m // tm
    n_a, n_t, n_v, n_o = len(pieces), len(tiles), len(vecs), len(out_tiles)

    def body(*refs):
        a_refs, b_ref, rest = refs[:n_a], refs[n_a], refs[n_a + 1:]
        t_refs, v_refs = rest[:n_t], rest[n_t:n_t + n_v]
        o_refs, st_ref, acc0, acc1 = rest[n_t + n_v + 1:n_t + n_v + 1 + n_o], rest[-3], rest[-2], rest[-1]
        i = pl.program_id(0)

        def dot():
            if n_a == 1 and pieces[0][0].shape[1] == b.shape[1 if nt else 0]:
                return (_dot_nt if nt else _dot)(a_refs[0][...], b_ref[...])
            parts = [_dot_nt(r[...], b_ref[:, off:off + p.shape[1]]) for r, (p, off) in zip(a_refs, pieces)]
            return functools.reduce(lambda u, v: u + v, parts)

        @pl.when(i == 0)
        def _():
            acc1[...] = jnp.zeros_like(acc1)
            st_ref[...] = jnp.zeros_like(st_ref)

        def finish(prev):
            for r0 in range(0, tm, STRIP):
                rs = slice(r0, r0 + STRIP)
                epi(prev[rs, :], rs, t_refs, v_refs, o_refs, st_ref, i > 0)

        @pl.when((i % 2 == 0) & (i < gm))
        def _():
            acc0[...] = dot()
            finish(acc1)

        @pl.when((i % 2 == 1) & (i < gm))
        def _():
            acc1[...] = dot()
            finish(acc0)

        @pl.when(i == gm)
        def _():
            finish(acc1 if gm % 2 == 0 else acc0)

    prev = lambda i: (jnp.maximum(i - 1, 0), 0)
    tile = pl.BlockSpec((tm, n), prev)
    return pl.pallas_call(
        body, name=name, grid=(gm + 1,),
        in_specs=[pl.BlockSpec((tm, p.shape[1]), lambda i: (jnp.minimum(i, gm - 1), 0)) for p, _ in pieces]
        + [pl.BlockSpec(b.shape, lambda i: (0, 0))] + [tile] * n_t + [_row_spec(n)] * n_v + [ANY],
        out_specs=tuple([tile] * n_o) + (_stat_spec(n_stats, n),),
        out_shape=tuple(out_tiles) + (jax.ShapeDtypeStruct((n_stats, 8, n), F32),),
        scratch_shapes=[pltpu.VMEM((tm, n), F32), pltpu.VMEM((tm, n), F32)], compiler_params=_cp(1, vmem_mib),
    )(*[p for p, _ in pieces], b, *tiles, *vecs, after)


def _mm_k_deferred(name, a, b, *, nt, tm, tk, epi, tiles, vecs, out_tiles, n_stats, after, vmem_mib=56):
    m, kdim = a.shape
    n = b.shape[0] if nt else b.shape[1]
    gm, gk = m // tm, kdim // tk
    rows = tm // gk
    n_t, n_v, n_o = len(tiles), len(vecs), len(out_tiles)
    dot = _dot_nt if nt else _dot

    def body(a_ref, b_ref, *rest):
        t_refs, v_refs = rest[:n_t], rest[n_t:n_t + n_v]
        o_refs, st_ref, acc0, acc1 = rest[n_t + n_v + 1:n_t + n_v + 1 + n_o], rest[-3], rest[-2], rest[-1]
        i, k = pl.program_id(0), pl.program_id(1)

        @pl.when((i == 0) & (k == 0))
        def _():
            acc1[...] = jnp.zeros_like(acc1)
            st_ref[...] = jnp.zeros_like(st_ref)

        def finish(prev):
            for r0 in range(0, rows, STRIP):
                acc_rows = prev[pl.ds(pl.multiple_of(k * rows + r0, STRIP), STRIP), :]
                epi(acc_rows, slice(r0, r0 + STRIP), t_refs, v_refs, o_refs, st_ref, i > 0)

        def step(cur, prev):
            cur[...] = jnp.where(k > 0, cur[...], 0.0) + dot(a_ref[...], b_ref[...])
            finish(prev)

        @pl.when((i % 2 == 0) & (i < gm))
        def _():
            step(acc0, acc1)

        @pl.when((i % 2 == 1) & (i < gm))
        def _():
            step(acc1, acc0)

        @pl.when(i == gm)
        def _():
            finish(acc1 if gm % 2 == 0 else acc0)

    prev = lambda i, k: (jnp.where(i == 0, 0, (i - 1) * gk + k), 0)
    part = pl.BlockSpec((rows, n), prev)
    b_spec = pl.BlockSpec((n, tk), lambda i, k: (0, k)) if nt else pl.BlockSpec((tk, n), lambda i, k: (k, 0))
    return pl.pallas_call(
        body, name=name, grid=(gm + 1, gk),
        in_specs=[pl.BlockSpec((tm, tk), lambda i, k: (jnp.minimum(i, gm - 1), k)), b_spec]
        + [part] * n_t + [_row_spec(n)] * n_v + [ANY],
        out_specs=tuple([part] * n_o) + (_stat_spec(n_stats, n),),
        out_shape=tuple(out_tiles) + (jax.ShapeDtypeStruct((n_stats, 8, n), F32),),
        scratch_shapes=[pltpu.VMEM((tm, n), F32), pltpu.VMEM((tm, n), F32)], compiler_params=_cp(2, vmem_mib),
    )(a, b, *tiles, *vecs, after)


def _mm_tn(name, a, b, out_dtype, *, tmo, tn, tt, more=(), after=None, vmem_mib=56):
    t, m = a.shape
    n = b.shape[1]
    gt = t // tt
    wait_for = [] if after is None else [after]

    def body(a_ref, b_ref, *rest):
        o_ref, acc = rest[-2:]
        k = pl.program_id(2)

        @pl.when(k == 0)
        def _():
            first = _dot_tn(a_ref[...], b_ref[...])
            acc[...] = first + _dot_tn(rest[0][...], rest[1][...]) if more else first

        @pl.when(k > 0)
        def _():
            acc[...] += _dot_tn(a_ref[...], b_ref[...])

        @pl.when(k == gt - 1)
        def _():
            o_ref[...] = acc[...].astype(o_ref.dtype)

    more_specs = [pl.BlockSpec((more[0].shape[0], tmo), lambda i, j, k: (0, i)),
                  pl.BlockSpec((more[1].shape[0], tn), lambda i, j, k: (0, j))] if more else []
    return pl.pallas_call(
        body, name=name, grid=(m // tmo, n // tn, gt),
        in_specs=[pl.BlockSpec((tt, tmo), lambda i, j, k: (k, i)), pl.BlockSpec((tt, tn), lambda i, j, k: (k, j))] + more_specs
        + [ANY] * len(wait_for),
        out_specs=pl.BlockSpec((tmo, tn), lambda i, j, k: (i, j)), out_shape=jax.ShapeDtypeStruct((m, n), out_dtype),
        scratch_shapes=[pltpu.VMEM((tmo, tn), F32)], compiler_params=_cp(3, vmem_mib),
    )(a, b, *more, *wait_for)


def _row_spec(d):
    return pl.BlockSpec((1, d), lambda *_: (0, 0))


def _stat_spec(k, d):
    return pl.BlockSpec((k, 8, d), lambda *_: (0, 0, 0))


def _rope(z, cs, sn):
    first = (lax.broadcasted_iota(jnp.int32, (z.shape[0], 128), 1) % 32) < 16
    outs = []
    for j in range(z.shape[1] // 128):
        zc = z[:, 128 * j:128 * (j + 1)]
        partner = jnp.where(first, pltpu.roll(zc, 112, 1), pltpu.roll(zc, 16, 1))
        outs.append(zc * cs + partner * sn)
    return outs[0] if len(outs) == 1 else jnp.concatenate(outs, axis=1)


def _rope_tables(length, rotate, zero=0.0):
    if not rotate:
        return jnp.ones((length, 128), F32), jnp.zeros((length, 128), F32)
    half = HEAD_DIM // 2
    t = jnp.arange(length)
    row = (t // GRID_W).astype(F32) + zero
    col = (t % GRID_W).astype(F32)
    e = jnp.arange(128) % HEAD_DIM
    inv_freq = ROPE_BASE ** (-(2 * ((e % half) % (half // 2))).astype(F32) / half)
    pos = jnp.where(e[None, :] < half, row[:, None], col[:, None])
    ang = pos * inv_freq[None, :]
    first = ((e % half) < half // 2)[None, :]
    return jnp.cos(ang), jnp.where(first, -jnp.sin(ang), jnp.sin(ang))


def _mixer_in(name, x, nw, sh, sc, w_in, cos, sin, after):
    t, d = x.shape
    tm = _pick(t, 256, 128)
    n_in = w_in.shape[1]

    def body(x_ref, nw_ref, sh_ref, sc_ref, w_ref, cos_ref, sin_ref, after_ref, h_ref, q_ref, k_ref, v_ref, u_ref):
        xf = x_ref[...]
        r = lax.rsqrt(jnp.mean(xf * xf, axis=-1, keepdims=True) + EPS)
        hb = (((xf * r) * nw_ref[...]) * (1.0 + sc_ref[...]) + sh_ref[...]).astype(BF16)
        h_ref[...] = hb
        p = _dot(hb, w_ref[...])
        cs, sn = cos_ref[...], sin_ref[...]
        q_ref[...] = (_rope(p[:, :ATTN_WIDTH], cs, sn) * SCALE).astype(BF16)
        k_ref[...] = _rope(p[:, ATTN_WIDTH:ATTN_WIDTH + KV_WIDTH], cs, sn).astype(BF16)
        v_ref[...] = p[:, ATTN_WIDTH + KV_WIDTH:ATTN_WIDTH + 2 * KV_WIDTH].astype(BF16)
        u_ref[...] = p[:, ATTN_WIDTH + 2 * KV_WIDTH:]

    def tile(w):
        return pl.BlockSpec((tm, w), lambda i: (i, 0))

    return pl.pallas_call(
        body, name=name, grid=(t // tm,),
        in_specs=[tile(d), _row_spec(d), _row_spec(d), _row_spec(d), pl.BlockSpec((d, n_in), lambda i: (0, 0)),
                  tile(128), tile(128), ANY],
        out_specs=(tile(d), tile(ATTN_WIDTH), tile(KV_WIDTH), tile(KV_WIDTH), tile(POOL_WIDTH)),
        out_shape=(jax.ShapeDtypeStruct((t, d), BF16), jax.ShapeDtypeStruct((t, ATTN_WIDTH), BF16),
                   jax.ShapeDtypeStruct((t, KV_WIDTH), BF16), jax.ShapeDtypeStruct((t, KV_WIDTH), BF16),
                   jax.ShapeDtypeStruct((t, POOL_WIDTH), F32)),
        compiler_params=_cp(1),
    )(x, nw, sh, sc, w_in, cos, sin, after)


def _attn_specs(nb, n_ctx):
    def blk(w, f):
        return pl.BlockSpec((BLOCK, w), lambda n: (f(n), 0))

    prev = lambda n: jnp.maximum(jnp.minimum(n, nb - 1) - 1, 0)
    cur = lambda n: jnp.minimum(n, nb - 1)
    nxt = lambda n: jnp.minimum(n + 1, nb - 1)
    kv = [blk(KV_WIDTH, prev), blk(KV_WIDTH, cur), blk(KV_WIDTH, nxt)]
    ctx = pl.BlockSpec((n_ctx, KV_WIDTH), lambda n: (0, 0))
    return [pl.BlockSpec(memory_space=pltpu.SMEM), blk(ATTN_WIDTH, cur)] + kv + kv + [ctx, ctx]


def _attn_mask(n, length, n_keys):
    row = lax.broadcasted_iota(jnp.int32, (GROUP * BLOCK, n_keys), 0) % BLOCK
    col = lax.broadcasted_iota(jnp.int32, (GROUP * BLOCK, n_keys), 1)
    kpos = (n - 1) * BLOCK + col
    return ((jnp.abs(col - BLOCK - row) <= BLOCK) & (kpos >= 0) & (kpos < length)) | (col >= 3 * BLOCK)


def _group_rows(block, g):
    return jnp.concatenate([block[:, HEAD_DIM * h:HEAD_DIM * (h + 1)] for h in range(GROUP * g, GROUP * (g + 1))], axis=0)


def _group_sink(sink_ref, g):
    head = lax.broadcasted_iota(jnp.int32, (GROUP * BLOCK, 1), 0) // BLOCK
    out = jnp.full((GROUP * BLOCK, 1), sink_ref[0, GROUP * g], F32)
    for j in range(1, GROUP):
        out = jnp.where(head == j, sink_ref[0, GROUP * g + j], out)
    return out


def _attn_fwd(q, k, v, kc, vc, sink):
    length = q.shape[0]
    nb = length // BLOCK
    n_ctx = kc.shape[0]
    n_keys = 3 * BLOCK + n_ctx

    def body(sink_ref, q_ref, kp, k0, kn, vp, v0, vn, kc_ref, vc_ref, o_ref, p_ref):
        n = pl.program_id(0)
        valid = _attn_mask(n, length, n_keys)
        qb = q_ref[...]
        kall = jnp.concatenate([kp[...], k0[...], kn[...], kc_ref[...]], axis=0)
        vall = jnp.concatenate([vp[...], v0[...], vn[...], vc_ref[...]], axis=0)
        outs = []
        for g in range(N_KV_HEADS):
            lanes = slice(HEAD_DIM * g, HEAD_DIM * (g + 1))
            s = jnp.where(valid, _dot_nt(_group_rows(qb, g), kall[:, lanes]), NEG_INF)
            sk = _group_sink(sink_ref, g)
            m = jnp.maximum(jnp.max(s, axis=-1, keepdims=True), sk)
            e = jnp.exp(s - m)
            e_sink = jnp.exp(sk - m)
            inv = 1.0 / (jnp.sum(e, axis=-1, keepdims=True) + e_sink)
            p_ref[0, g, :, :n_keys] = (e * inv).astype(BF16)
            p_ref[0, g, :, n_keys:] = jnp.broadcast_to(e_sink * inv, (GROUP * BLOCK, 128)).astype(BF16)
            o = _dot(p_ref[0, g, :, :n_keys], vall[:, lanes])
            outs += [o[BLOCK * j:BLOCK * (j + 1)] for j in range(GROUP)]
        o_ref[...] = jnp.concatenate(outs, axis=1).astype(BF16)

    return pl.pallas_call(
        body, name="attn_fwd", grid=(nb,), in_specs=_attn_specs(nb, n_ctx),
        out_specs=(pl.BlockSpec((BLOCK, ATTN_WIDTH), lambda n: (n, 0)),
                   pl.BlockSpec((1, N_KV_HEADS, GROUP * BLOCK, n_keys + 128), lambda n: (n, 0, 0, 0))),
        out_shape=(jax.ShapeDtypeStruct((length, ATTN_WIDTH + POOL_WIDTH), BF16),
                   jax.ShapeDtypeStruct((nb, N_KV_HEADS, GROUP * BLOCK, n_keys + 128), BF16)), compiler_params=_cp(1),
    )(sink, q, k, k, k, v, v, v, kc, vc)


def _attn_bwd(q, k, v, kc, vc, dmix, probs, cos, sin):
    length = q.shape[0]
    nb = length // BLOCK
    n_ctx = kc.shape[0]
    n_keys = 3 * BLOCK + n_ctx

    def body(q_ref, kp, k0, kn, vp, v0, vn, kc_ref, vc_ref, do_ref, p_ref, cos_ref, sin_ref, cos_prev, sin_prev,
             dq_ref, dkv_ref, dkc_ref, dvc_ref, dsink_ref, done, ahead):
        n = pl.program_id(0)

        @pl.when(n == 0)
        def _():
            dkc_ref[...] = jnp.zeros_like(dkc_ref)
            dvc_ref[...] = jnp.zeros_like(dvc_ref)
            dsink_ref[...] = jnp.zeros_like(dsink_ref)
            done[...] = jnp.zeros_like(done)
            ahead[...] = jnp.zeros_like(ahead)

        def write_block(dkv):
            dkv_ref[:, :KV_WIDTH] = _rope(dkv[:, :KV_WIDTH], cos_prev[...], -sin_prev[...]).astype(BF16)
            dkv_ref[:, KV_WIDTH:] = dkv[:, KV_WIDTH:].astype(BF16)

        def query_block():
            qb, dob = q_ref[...], do_ref[...]
            kall = jnp.concatenate([kp[...], k0[...], kn[...], kc_ref[...]], axis=0)
            vall = jnp.concatenate([vp[...], v0[...], vn[...], vc_ref[...]], axis=0)
            srow = lax.broadcasted_iota(jnp.int32, (8, 128), 0)
            slane = lax.broadcasted_iota(jnp.int32, (8, 128), 1)
            dqs, dks, dvs = [], [], []
            dsink = jnp.zeros((8, 128), F32)
            for g in range(N_KV_HEADS):
                lanes = slice(HEAD_DIM * g, HEAD_DIM * (g + 1))
                kg, vg = kall[:, lanes], vall[:, lanes]
                qg, dog = _group_rows(qb, g), _group_rows(dob, g)
                pb = p_ref[0, g, :, :n_keys]
                p = pb.astype(F32)
                dp = _dot_nt(dog, vg)
                delta = jnp.sum(p * dp, axis=-1, keepdims=True)
                ds = (p * (dp - delta)).astype(BF16)
                dq = _dot(ds, kg) * SCALE
                dqs += [dq[BLOCK * j:BLOCK * (j + 1)] for j in range(GROUP)]
                dks.append(_dot_tn(ds, qg))
                dvs.append(_dot_tn(pb, dog))
                d_sink = p_ref[0, g, :, n_keys:].astype(F32)[:, :1] * delta
                for j in range(GROUP):
                    total = -jnp.sum(d_sink[BLOCK * j:BLOCK * (j + 1)], axis=0, keepdims=True)
                    dsink = dsink + jnp.where((srow == 0) & (slane == GROUP * g + j), total, 0.0)
            dq_ref[...] = _rope(jnp.concatenate(dqs, axis=1), cos_ref[...], -sin_ref[...]).astype(BF16)
            dkv = jnp.concatenate(dks + dvs, axis=1)
            write_block(done[...] + dkv[:BLOCK])
            done[...] = ahead[...] + dkv[BLOCK:2 * BLOCK]
            ahead[...] = dkv[2 * BLOCK:3 * BLOCK]
            dkc_ref[...] += dkv[3 * BLOCK:, :KV_WIDTH]
            dvc_ref[...] += dkv[3 * BLOCK:, KV_WIDTH:]
            dsink_ref[...] += dsink

        pl.when(n < nb)(query_block)

        @pl.when(n == nb)
        def _():
            write_block(done[...])

    here = lambda n: (jnp.minimum(n, nb - 1), 0)
    before = lambda n: (jnp.maximum(n - 1, 0), 0)
    ctx = pl.BlockSpec((n_ctx, KV_WIDTH), lambda n: (0, 0))
    return pl.pallas_call(
        body, name="attn_bwd", grid=(nb + 1,),
        in_specs=_attn_specs(nb, n_ctx)[1:] + [pl.BlockSpec((BLOCK, ATTN_WIDTH), here),
                                           pl.BlockSpec((1,) + probs.shape[1:], lambda n: (jnp.minimum(n, nb - 1), 0, 0, 0)),
                                           pl.BlockSpec((BLOCK, 128), here), pl.BlockSpec((BLOCK, 128), here),
                                           pl.BlockSpec((BLOCK, 128), before), pl.BlockSpec((BLOCK, 128), before)],
        out_specs=(pl.BlockSpec((BLOCK, ATTN_WIDTH), here), pl.BlockSpec((BLOCK, 2 * KV_WIDTH), before), ctx, ctx,
                   pl.BlockSpec((8, 128), lambda n: (0, 0))),
        out_shape=(jax.ShapeDtypeStruct((length, ATTN_WIDTH), BF16), jax.ShapeDtypeStruct((length, 2 * KV_WIDTH), BF16),
                   jax.ShapeDtypeStruct((n_ctx, KV_WIDTH), F32), jax.ShapeDtypeStruct((n_ctx, KV_WIDTH), F32),
                   jax.ShapeDtypeStruct((8, 128), F32)),
        scratch_shapes=[pltpu.VMEM((BLOCK, 2 * KV_WIDTH), F32), pltpu.VMEM((BLOCK, 2 * KV_WIDTH), F32)],
        compiler_params=_cp(1),
    )(q, k, k, k, v, v, v, kc, vc, dmix, probs, cos, sin, cos, sin)


def _shift_rows(e, s):
    n = e.shape[0]
    return e if s % n == 0 else pltpu.roll(e, (-s) % n, 0)


def _window_sum(e, w, first):
    s, n = e, 1
    while n < w:
        s = s + _shift_rows(s, n)
        n *= 2
    return _shift_rows(s, first)


def _pool_geometry(i, tm, length):
    pos = i * tm - HALO + lax.broadcasted_iota(jnp.int32, (tm + 2 * HALO, 1), 0)
    inside = (pos >= 0) & (pos < length)
    inv_counts = []
    for w in POOL_WINDOWS:
        lo = jnp.clip(pos - w // 2, 0, length)
        hi = jnp.clip(pos - w // 2 + w, 0, length)
        inv_counts.append(1.0 / jnp.maximum(hi - lo, 1).astype(F32))
    return inside, inv_counts


def _halo_specs(tm, width, length, col=0):
    per = tm // HALO
    last = length // HALO - 1
    return [pl.BlockSpec((HALO, width), lambda i: (jnp.maximum(i * per - 1, 0), col)),
            pl.BlockSpec((tm, width), lambda i: (i, col)),
            pl.BlockSpec((HALO, width), lambda i: (jnp.minimum((i + 1) * per, last), col))]


def _pooled(ext, inv_counts, tm):
    outs = []
    for g, w in enumerate(POOL_WINDOWS):
        e = ext[:, POOL_GROUP_DIM * g:POOL_GROUP_DIM * (g + 1)]
        mean = _window_sum(e, w, -(w // 2)) * inv_counts[g]
        outs.append((mean - e)[HALO:HALO + tm])
    return outs


def _pool_fwd(u, pool_w, pool_scale, mix):
    length = u.shape[0]
    tm = _pick(length, 512, 256, 128)

    def body(up, u0, un, w_ref, sc_ref, mix_ref, o_ref):
        inside, inv_counts = _pool_geometry(pl.program_id(0), tm, length)
        ext = jnp.where(inside, jnp.concatenate([up[...], u0[...], un[...]], axis=0), 0.0)
        pooled = _pooled(ext, inv_counts, tm)
        mixed = [_dot(pooled[g].astype(BF16), w_ref[g]) for g in range(len(POOL_WINDOWS))]
        o_ref[...] = (jnp.concatenate(mixed, axis=1) * sc_ref[...]).astype(BF16)

    return pl.pallas_call(
        body, name="pool_fwd", grid=(length // tm,),
        in_specs=_halo_specs(tm, POOL_WIDTH, length) + [pl.BlockSpec(pool_w.shape, lambda i: (0, 0, 0)), _row_spec(POOL_WIDTH), ANY],
        out_specs=pl.BlockSpec((tm, POOL_WIDTH), lambda i: (i, 1)),
        out_shape=jax.ShapeDtypeStruct(mix.shape, BF16), input_output_aliases={5: 0}, compiler_params=_cp(1),
    )(u, u, u, pool_w, pool_scale, mix)


def _pool_bwd(u, dmix, pool_w, pool_scale, after):
    length = u.shape[0]
    tm = _pick(length, 512, 256, 128)
    n_g = len(POOL_WINDOWS)

    def body(up, u0, un, dp_, d0, dn_, w_ref, sc_ref, after_ref, du_ref, dw_ref, dsc_ref):
        i = pl.program_id(0)

        @pl.when(i == 0)
        def _():
            dw_ref[...] = jnp.zeros_like(dw_ref)
            dsc_ref[...] = jnp.zeros_like(dsc_ref)

        inside, inv_counts = _pool_geometry(i, tm, length)
        ext = jnp.where(inside, jnp.concatenate([up[...], u0[...], un[...]], axis=0), 0.0)
        dext = jnp.where(inside, jnp.concatenate([dp_[...], d0[...], dn_[...]], axis=0).astype(F32), 0.0)
        dmixed = (dext * sc_ref[...]).astype(BF16)
        pooled = _pooled(ext, inv_counts, tm)
        dus, dscs = [], []
        for g, w in enumerate(POOL_WINDOWS):
            lanes = slice(POOL_GROUP_DIM * g, POOL_GROUP_DIM * (g + 1))
            dpooled = _dot_nt(dmixed[:, lanes], w_ref[g])
            spread = _window_sum(dpooled * inv_counts[g], w, -(w // 2 - 1))
            dus.append((spread - dpooled)[HALO:HALO + tm])
            pb = pooled[g].astype(BF16)
            dw_ref[g] += _dot_tn(pb, dmixed[HALO:HALO + tm, lanes])
            prod = dext[HALO:HALO + tm, lanes] * _dot(pb, w_ref[g])
            dscs.append(_fold8(prod))
        du_ref[...] = jnp.concatenate(dus, axis=1).astype(BF16)
        dsc_ref[...] += jnp.concatenate(dscs, axis=1)

    return pl.pallas_call(
        body, name="pool_bwd", grid=(length // tm,),
        in_specs=_halo_specs(tm, POOL_WIDTH, length) + _halo_specs(tm, POOL_WIDTH, length, col=1)
        + [pl.BlockSpec(pool_w.shape, lambda i: (0, 0, 0)), _row_spec(POOL_WIDTH), ANY],
        out_specs=(pl.BlockSpec((tm, POOL_WIDTH), lambda i: (i, 0)), pl.BlockSpec((n_g, POOL_GROUP_DIM, POOL_GROUP_DIM), lambda i: (0, 0, 0)),
                   pl.BlockSpec((8, POOL_WIDTH), lambda i: (0, 0))),
        out_shape=(jax.ShapeDtypeStruct((length, POOL_WIDTH), BF16), jax.ShapeDtypeStruct((n_g, POOL_GROUP_DIM, POOL_GROUP_DIM), F32),
                   jax.ShapeDtypeStruct((8, POOL_WIDTH), F32)),
        compiler_params=_cp(1),
    )(u, u, u, dmix, dmix, dmix, pool_w, pool_scale, after)


def _mixer_out(mix, w_out, x, g_a, nmw, sh_m, sc_m, after):
    t, d = x.shape

    def epi(mo, rs, tiles, vecs, outs, st_ref, live):
        ga, nw, sh, sc = vecs
        x1_ref, mo_ref, hm_ref = outs
        x1 = tiles[0][rs, :] + ga[...] * mo
        x1_ref[rs, :] = x1
        mo_ref[rs, :] = mo.astype(BF16)
        r = lax.rsqrt(jnp.mean(x1 * x1, axis=-1, keepdims=True) + EPS)
        hm_ref[rs, :] = (((x1 * r) * nw[...]) * (1.0 + sc[...]) + sh[...]).astype(BF16)

    return _mm_deferred("mixer_out", mix, w_out, nt=False, tm=_pick(t, 256, 128), epi=epi, tiles=(x,), vecs=(g_a, nmw, sh_m, sc_m),
                        out_tiles=(jax.ShapeDtypeStruct((t, d), F32), jax.ShapeDtypeStruct((t, d), BF16), jax.ShapeDtypeStruct((t, d), BF16)),
                        n_stats=1, after=after)[:3]


def _mlp_up(hm, w_up):
    t, d = hm.shape
    tm = _pick(t, 1024, 512, 256, 128)
    tn = 2048

    def epi(acc, ex, outs):
        outs[0][...] = jnp.square(jnp.maximum(acc[...], 0.0)).astype(BF16)

    return _mm("mlp_up", hm, w_up, nt=False, tm=tm, tn=tn, tk=d, epi=epi,
               out_shape=(jax.ShapeDtypeStruct((t, w_up.shape[1]), BF16),),
               out_specs=(pl.BlockSpec((tm, tn), lambda j, i, k: (i, j)),))[0]


def _mlp_down_loss(act, w_down, x1, target, g_m, fw, after):
    t, d = x1.shape

    def epi(dnv, rs, tiles, vecs, outs, st_ref, live):
        x1_ref, t_ref = tiles
        gm, fw_ref = vecs
        dx2_ref, ddn_ref = outs
        x2 = x1_ref[rs, :] + gm[...] * dnv
        r = lax.rsqrt(jnp.mean(x2 * x2, axis=-1, keepdims=True) + EPS)
        xh = x2 * r
        diff = xh * fw_ref[...] - t_ref[rs, :]
        dy = diff * (1.0 / d)
        dxh = dy * fw_ref[...]
        dx2 = r * (dxh - xh * jnp.mean(dxh * xh, axis=-1, keepdims=True))
        dx2_ref[rs, :] = dx2
        ddn_ref[rs, :] = (dx2 * gm[...]).astype(BF16)
        st_ref[0] += jnp.where(live, _fold8(diff * diff), 0.0)
        st_ref[1] += jnp.where(live, _fold8(dy * xh), 0.0)
        st_ref[2] += jnp.where(live, _fold8(dx2 * dnv), 0.0)

    return _mm_k_deferred("mlp_down_loss", act, w_down, nt=False, tm=_pick(t, 512, 256), tk=_pick(act.shape[1], 2048), epi=epi,
                          tiles=(x1, target), vecs=(g_m, fw), n_stats=3, after=after,
                          out_tiles=(jax.ShapeDtypeStruct((t, d), F32), jax.ShapeDtypeStruct((t, d), BF16)))


def _mlp_dx(dup, w_up, x1, dx2, mo, nmw, sc_m, g_a, after):
    t, d = x1.shape

    def epi(dh, rs, tiles, vecs, outs, st_ref, live):
        x1_ref, dx2_ref, mo_ref = tiles
        nw, sc, ga = vecs
        dx1_ref, dmi_ref = outs
        dx1 = _norm_bwd_rows(dh, x1_ref[rs, :], nw[...], sc[...], st_ref) + dx2_ref[rs, :]
        dx1_ref[rs, :] = dx1
        dmi_ref[rs, :] = (dx1 * ga[...]).astype(BF16)
        st_ref[3] += jnp.where(live, _fold8(dx1 * mo_ref[rs, :].astype(F32)), 0.0)

    return _mm_k_deferred("mlp_dx", dup, w_up, nt=True, tm=_pick(t, 512, 256), tk=_pick(dup.shape[1], 2048), epi=epi,
                          tiles=(x1, dx2, mo), vecs=(nmw, sc_m, g_a), n_stats=4, after=after,
                          out_tiles=(jax.ShapeDtypeStruct((t, d), F32), jax.ShapeDtypeStruct((t, d), BF16)))


def _mlp_dact(ddn, w_down, act):
    t, d = ddn.shape
    tm = _pick(t, 1024, 512, 256, 128)
    tn = 2048

    def epi(acc, ex, outs):
        outs[0][...] = (acc[...] * (2.0 * jnp.sqrt(ex[0][...]).astype(F32))).astype(BF16)

    tile = pl.BlockSpec((tm, tn), lambda j, i, k: (i, j))
    return _mm("mlp_dact", ddn, w_down, nt=True, tm=tm, tn=tn, tk=d, epi=epi, extras=(act,), extra_specs=[tile],
               out_shape=(jax.ShapeDtypeStruct(act.shape, BF16),), out_specs=(tile,), vmem_mib=56)[0]


def _norm_bwd_rows(dh, xv, nw, sc, st_ref):
    r = lax.rsqrt(jnp.mean(xv * xv, axis=-1, keepdims=True) + EPS)
    xh = xv * r
    dy = dh * (1.0 + sc)
    st_ref[0] += _fold8(dh)
    st_ref[1] += _fold8(dh * (xh * nw))
    st_ref[2] += _fold8(dy * xh)
    dxh = dy * nw
    return r * (dxh - xh * jnp.mean(dxh * xh, axis=-1, keepdims=True))


def _mixer_dmix(dmi, w_out, after):
    t, d = dmi.shape
    tm = _pick(t, 1024, 512, 256, 128)

    def epi(acc, ex, outs):
        outs[0][...] = acc[...].astype(BF16)

    n = w_out.shape[0]
    return _mm("mixer_dmix", dmi, w_out, nt=True, tm=tm, tn=n, tk=d, epi=epi, after=after,
               out_shape=(jax.ShapeDtypeStruct((t, n), BF16),), out_specs=(pl.BlockSpec((tm, n), lambda j, i, k: (i, 0)),))[0]


def _mixer_dx(name, dp, w_in, x, dx1, naw, sc_a, after):
    t, d = x.shape

    def epi(dh, rs, tiles, vecs, outs, st_ref, live):
        x_ref, dx1_ref = tiles
        nw, sc = vecs
        outs[0][rs, :] = _norm_bwd_rows(dh, x_ref[rs, :], nw[...], sc[...], st_ref) + dx1_ref[rs, :]

    return _mm_deferred(name, dp, w_in, nt=True, tm=_pick(t, 256, 128), epi=epi, tiles=(x, dx1), vecs=(naw, sc_a),
                        out_tiles=(jax.ShapeDtypeStruct((t, d), F32),), n_stats=3, after=after, vmem_mib=56)


def _silu(v):
    return v / (1.0 + jnp.exp(-v))


def _ada_fwd(cond, w_ada, b_ada):
    d, n = w_ada.shape
    tn = 512

    def body(c_ref, w_ref, b_ref, o_ref):
        o_ref[...] = _dot(_silu(c_ref[...]).astype(BF16), w_ref[...].astype(BF16)) + b_ref[...]

    return pl.pallas_call(
        body, name="ada_fwd", grid=(n // tn,),
        in_specs=[pl.BlockSpec(cond.shape, lambda j: (0, 0)), pl.BlockSpec((d, tn), lambda j: (0, j)), pl.BlockSpec((1, tn), lambda j: (0, j))],
        out_specs=pl.BlockSpec((cond.shape[0], tn), lambda j: (0, j)), out_shape=jax.ShapeDtypeStruct((cond.shape[0], n), F32),
        compiler_params=_cp(1),
    )(cond, w_ada, b_ada)


def _adamw_math(w, g, m, v):
    m = ADAM_B1 * m + (1.0 - ADAM_B1) * g
    v = ADAM_B2 * v + (1.0 - ADAM_B2) * jnp.square(g)
    m_hat = m / (1.0 - ADAM_B1 ** ADAM_STEP)
    v_hat = v / (1.0 - ADAM_B2 ** ADAM_STEP)
    return -ADAM_LR * (m_hat / (jnp.sqrt(v_hat) + ADAM_EPS) + ADAM_WD * w), m, v


def _ada_bwd(cond, dm, w_ada, m_ada, v_ada):
    d, n = w_ada.shape
    tn = 256
    rows = cond.shape[0]

    def body(c_ref, dm_ref, w_ref, m_ref, v_ref, g_ref, dl_ref, nm_ref, nv_ref, pc_ref):
        @pl.when(pl.program_id(0) == 0)
        def _():
            pc_ref[...] = jnp.zeros_like(pc_ref)

        dmb = dm_ref[...].astype(BF16)
        w = w_ref[...]
        g = _dot_tn(_silu(c_ref[...]).astype(BF16), dmb)
        g_ref[...] = g
        dl_ref[...], nm_ref[...], nv_ref[...] = _adamw_math(w, g, m_ref[...], v_ref[...])
        pc_ref[...] += _dot_nt(dm_ref[8:16, :].astype(BF16), w.astype(BF16))

    tile = pl.BlockSpec((d, tn), lambda j: (0, j))
    like = jax.ShapeDtypeStruct((d, n), F32)
    return pl.pallas_call(
        body, name="ada_bwd", grid=(n // tn,),
        in_specs=[pl.BlockSpec((rows, d), lambda j: (0, 0)), pl.BlockSpec((rows, tn), lambda j: (0, j)), tile, tile, tile],
        out_specs=(tile, tile, tile, tile, pl.BlockSpec((8, d), lambda j: (0, 0))),
        out_shape=(like, like, like, like, jax.ShapeDtypeStruct((8, d), F32)), compiler_params=_cp(1),
    )(cond, dm, w_ada, m_ada, v_ada)


def _adamw(name, w, g, m, v, after=None):
    return _ew(name, lambda w_, g_, m_, v_: (g_,) + _adamw_math(w_, g_, m_, v_), [w, g, m, v], [F32, F32, F32, F32],
               g if after is None else after)


def _colsum(st):
    return jnp.sum(st, axis=1)


def kernel(x, c, ctx, c_ctx, norm_attn_w, norm_mlp_w, w_ada, b_ada, w_in, attn_sink, pool_w, pool_scale, w_out, w_mlp_up, w_mlp_down, final_norm_w, loss_target, m_c_ctx, m_norm_attn_w, m_norm_mlp_w, m_w_ada, m_b_ada, m_w_in, m_attn_sink, m_pool_w, m_pool_scale, m_w_out, m_w_mlp_up, m_w_mlp_down, m_final_norm_w, v_c_ctx, v_norm_attn_w, v_norm_mlp_w, v_w_ada, v_b_ada, v_w_in, v_attn_sink, v_pool_w, v_pool_scale, v_w_out, v_w_mlp_up, v_w_mlp_down, v_final_norm_w):
    length, d = x.shape[1], x.shape[2]
    n_ctx = ctx.shape[1]
    pos = (lax.axis_index("x"), lax.axis_index("y"), lax.axis_index("c"))
    me, chip = _dev_index(pos), _chip_index(pos)
    xs, tgt, cx = x.reshape(length, d), loss_target.reshape(length, d), ctx.reshape(n_ctx, d)
    n_ada = w_ada.shape[2]

    c_all = _allgather8("gather_c", jnp.pad(c, ((0, 7), (0, 0))))
    mixer_bigs = [_Big("col", w_in.shape[1:]), _Big("pool", pool_w.shape[1:]), _Big("row", w_out.shape[1:])]
    mlp_bigs = [_Big("col", w_mlp_up.shape[1:]), _Big("row", w_mlp_down.shape[1:])]
    placed = [_cast_place(f"place_{i}", b, s, c_all)[0] for i, (b, s) in enumerate(zip(mixer_bigs, [w_in[0], pool_w[0], w_out[0]]))]
    flight = _split("gather_mixer_ici", placed, _gather_ici_remote(mixer_bigs, 0))
    token, placed_mlp = flight[3], []
    for i, (b, s) in enumerate(zip(mlp_bigs, [w_mlp_up[0], w_mlp_down[0]])):
        p, token = _cast_place(f"place_mlp_{i}", b, s, token)
        placed_mlp.append(p)
    cos, sin = _rope_tables(length, True, token[0, 0])
    cond = jnp.concatenate([c_all[:, 0, :], jnp.pad(c_ctx[None, :], ((0, 7), (0, 0)))], axis=0) + 0.0 * cos[0, 0]
    b_shard = lax.dynamic_slice_in_dim(b_ada, chip * n_ada, n_ada, axis=1)
    mod_all = _allgather8("gather_mod", _ada_fwd(cond, w_ada[0], b_shard))
    mod = jnp.concatenate([mod_all[0], mod_all[2], mod_all[4], mod_all[6]], axis=1)
    mine = lax.dynamic_slice_in_dim(mod, me, 1, axis=0)
    sh_a, sc_a, g_a, sh_m, sc_m, g_m = [mine[:, d * i:d * (i + 1)] for i in range(6)]
    csh_a, csc_a = mod[8:9, :d], mod[8:9, d:2 * d]

    win_b, pw_b, wout_b = _exchange("gather_mixer_d2d", _join(flight, mod), [jax.ShapeDtypeStruct(b.full_shape, BF16) for b in mixer_bigs],
                                    _gather_d2d_remote(mixer_bigs, 3), aliases={0: 0, 1: 1, 2: 2})
    wout_b = wout_b.reshape(-1, d)
    flight = _split("gather_mlp_ici", placed_mlp, _gather_ici_remote(mlp_bigs, 0), after=pw_b)

    one, zero = _rope_tables(n_ctx, False)
    h, q, k, v, u = _mixer_in("mixer_in", xs, norm_attn_w, sh_a, sc_a, win_b, cos, sin, flight[3])
    hc, _, kc, vc, _ = _mixer_in("mixer_in_ctx", cx, norm_attn_w, csh_a, csc_a, win_b, one, zero, flight[3])
    attn, probs = _attn_fwd(q, k, v, kc, vc, attn_sink)
    mix = _pool_fwd(u, pw_b, pool_scale, attn)
    flight = _split("gather_mlp_d2d", _join(flight, mix), _gather_d2d_remote(mlp_bigs, 0))
    x1, mo, hm = _mixer_out(mix, wout_b, xs, g_a, norm_mlp_w, sh_m, sc_m, flight[3])
    wup_b, wdn_b = _join(flight, hm)
    wdn_b = wdn_b.reshape(-1, d)
    act = _mlp_up(hm, wup_b)
    dx2, ddn, st_loss = _mlp_down_loss(act, wdn_b, x1, tgt, g_m, final_norm_w[None, :], c)
    st_loss = _colsum(st_loss)

    tt = _pick(length, 2048, 1024, 512, 256, 128)
    g_wdn = _mm_tn("grad_w_down", act, ddn, BF16, tmo=1024, tn=d, tt=tt)
    dup = _mlp_dact(ddn, wdn_b, act)
    g_wup = _mm_tn("grad_w_up", hm, dup, BF16, tmo=d, tn=1024, tt=tt)
    empty = lambda shapes: [lax.empty(s.shape, s.dtype) for s in shapes]
    grads = [g_wup, g_wdn.reshape(mlp_bigs[1].full_shape)]
    flight = _split("reduce_mlp_d2d", grads + empty(_halves(mlp_bigs)), _reduce_d2d_remote(mlp_bigs))
    dx1, dmi, st_mlp = _mlp_dx(dup, wup_b, x1, dx2, mo, norm_mlp_w, sc_m, g_a, flight[3])
    st_mlp = _colsum(st_mlp)
    landed = _join(flight, dmi)
    mlp_chip = _chip_sums("mlp", mlp_bigs, landed[:2], landed[2:])
    flight = _split("reduce_mlp_ici", mlp_chip + empty(_thirds(mlp_bigs)), _reduce_ici_remote(mlp_bigs))
    g_wout = _mm_tn("grad_w_out", mix, dmi, BF16, tmo=1024, tn=d, tt=tt)
    dmix = _mixer_dmix(dmi, wout_b, flight[3])
    dq, dkv, dkc, dvc, dsink = _attn_bwd(q, k, v, kc, vc, dmix, probs, cos, sin)
    landed = _join(flight, dq)
    flight = _split("reduce_mlp_share", _piece_sums("mlp", mlp_bigs, landed[:2], landed[2:]), _share_remote(mlp_bigs, 0))
    du, g_pw, st_pool = _pool_bwd(u, dmix, pw_b, pool_scale, flight[3])
    g_mlp = _join(flight, du)

    wo_bigs, win_bigs = mixer_bigs[1:], mixer_bigs[:1]
    wo_chip = _reduce_to_chip("wo", wo_bigs, [g_pw.astype(BF16), g_wout.reshape(wo_bigs[1].full_shape)])
    flight = _split("reduce_wo_ici", wo_chip + empty(_thirds(wo_bigs)), _reduce_ici_remote(wo_bigs))
    dkv_ctx = jnp.concatenate([dkc.astype(BF16), dvc.astype(BF16)], axis=1)
    at_kv, at_u = ATTN_WIDTH, ATTN_WIDTH + 2 * KV_WIDTH
    tt_in = _pick(length, 1024, 512, 256, 128)
    g_win = jnp.concatenate([_mm_tn("grad_w_in_q", h, dq, BF16, tmo=d, tn=ATTN_WIDTH, tt=tt_in, after=flight[3]),
                             _mm_tn("grad_w_in_kv", h, dkv, BF16, tmo=d, tn=2 * KV_WIDTH, tt=tt_in, more=(hc, dkv_ctx)),
                             _mm_tn("grad_w_in_u", h, du, BF16, tmo=d, tn=POOL_WIDTH, tt=tt_in)], axis=1)
    wo_landed = _join(flight, g_win)
    win_chip = _reduce_to_chip("win", win_bigs, [g_win])
    flight = _split("reduce_win_ici", win_chip + empty(_thirds(win_bigs)), _reduce_ici_remote(win_bigs))
    grad_x, st_mix = _mixer_dx("mixer_dx", [(dq, 0), (dkv, at_kv), (du, at_u)], win_b, xs, dx1, norm_attn_w, sc_a, flight[3])
    _, st_ctx = _mixer_dx("mixer_dx_ctx", [(dkv_ctx, at_kv)], win_b, cx, jnp.zeros((n_ctx, d), F32), norm_attn_w, csc_a, flight[3])
    st_mix, st_ctx = _colsum(st_mix), _colsum(st_ctx)
    win_landed = _join(flight, grad_x)
    g_mixer = (_reduce_finish("win", win_bigs, win_landed[:1], win_landed[1:])
               + _reduce_finish("wo", wo_bigs, wo_landed[:2], wo_landed[2:]))

    zrow = jnp.zeros((d,), F32)
    pad = lambda a: jnp.pad(a, (0, d - a.shape[0]))
    mine_rows = [st_mix[0], st_mix[1], st_mlp[3], st_mlp[0], st_mlp[1], st_loss[2],
                 st_ctx[0], st_ctx[1],
                 st_mix[2] + st_ctx[2], st_mlp[2], st_loss[1],
                 pad(jnp.sum(st_pool, axis=0)), pad(dsink[0, :N_Q_HEADS]), st_loss[0]] + [zrow] * 2
    flight = _allgather8_split("gather_small", jnp.concatenate(mine_rows).reshape(len(mine_rows), d), me)
    res = {"w_mlp_up": tuple(_adamw("adamw_w_mlp_up", w_mlp_up, g_mlp[0].reshape(w_mlp_up.shape), m_w_mlp_up, v_w_mlp_up, flight[3]))}
    small_all = _join(flight, res["w_mlp_up"][1])[1]
    small = small_all[0]
    for i in range(1, 8):
        small = small + small_all[i]
    loss = 0.5 / d * jnp.sum(small[13])
    dm_rows = small_all[:, 0:6, :].reshape(8, 6 * d)
    dm_ctx = jnp.concatenate([small[6], small[7], jnp.zeros((4 * d,), F32)])[None, :]
    dm = jnp.concatenate([dm_rows, jnp.pad(dm_ctx, ((0, 7), (0, 0)))], axis=0)
    g_bada = jnp.sum(dm[:9], axis=0, keepdims=True)
    dm_shard = lax.dynamic_slice_in_dim(dm, chip * n_ada, n_ada, axis=1)
    g_wada, dl_wada, nm_wada, nv_wada, part_cctx = _ada_bwd(cond, dm_shard, w_ada[0], m_w_ada[0], v_w_ada[0])
    flight = _allgather8_split("gather_cctx", part_cctx, me)
    res["w_mlp_down"] = tuple(_adamw("adamw_w_mlp_down", w_mlp_down, g_mlp[1].reshape(w_mlp_down.shape), m_w_mlp_down,
                                     v_w_mlp_down, flight[3]))
    cctx_all = _join(flight, res["w_mlp_down"][1])[1]
    dsilu_in = cctx_all[0, 0] + cctx_all[2, 0] + cctx_all[4, 0] + cctx_all[6, 0]
    sig = 1.0 / (1.0 + jnp.exp(-c_ctx))
    g_cctx = dsilu_in * (sig * (1.0 + c_ctx * (1.0 - sig)))

    for nm, w_, g_, m_, v_ in zip(["w_in", "pool_w", "w_out"], [w_in, pool_w, w_out], g_mixer,
                                  [m_w_in, m_pool_w, m_w_out], [v_w_in, v_pool_w, v_w_out]):
        res[nm] = tuple(_adamw("adamw_" + nm, w_, g_.reshape(w_.shape), m_, v_))
    res["w_ada"] = (g_wada[None], dl_wada[None], nm_wada[None], nv_wada[None])

    def pack(cc, na, nm_, ba, sk, ps, fn):
        flat = [cc.reshape(-1), na.reshape(-1), nm_.reshape(-1), ba.reshape(-1), pad(sk.reshape(-1)), pad(ps.reshape(-1)),
                fn.reshape(-1), jnp.zeros((4 * d,), F32)]
        return jnp.concatenate(flat).reshape(16, d)

    w_s = pack(c_ctx, norm_attn_w, norm_mlp_w, b_ada, attn_sink, pool_scale, final_norm_w)
    m_s = pack(m_c_ctx, m_norm_attn_w, m_norm_mlp_w, m_b_ada, m_attn_sink, m_pool_scale, m_final_norm_w)
    v_s = pack(v_c_ctx, v_norm_attn_w, v_norm_mlp_w, v_b_ada, v_attn_sink, v_pool_scale, v_final_norm_w)
    g_s = pack(g_cctx, small[8], small[9], g_bada, small[12][:N_Q_HEADS], small[11][:POOL_WIDTH], small[10])
    small_out = _adamw("adamw_small", w_s, g_s, m_s, v_s)

    def unpack(p):
        return {"c_ctx": p[0], "norm_attn_w": p[1:2], "norm_mlp_w": p[2:3], "b_ada": p[3:9].reshape(1, 6 * d),
                "attn_sink": p[9:10, :N_Q_HEADS], "pool_scale": p[10:11, :POOL_WIDTH], "final_norm_w": p[11]}

    small_res = [unpack(p) for p in small_out]
    order = ["c_ctx", "norm_attn_w", "norm_mlp_w", "w_ada", "b_ada", "w_in", "attn_sink", "pool_w", "pool_scale",
             "w_out", "w_mlp_up", "w_mlp_down", "final_norm_w"]
    outs = [loss, grad_x.reshape(x.shape)]
    for kind in range(4):
        for nm in order:
            outs.append(res[nm][kind] if nm in res else small_res[kind][nm])
    return tuple(outs)
```

```python
import functools

import jax
import jax.numpy as jnp
from jax import lax
from jax.experimental import pallas as pl
from jax.experimental.pallas import tpu as pltpu

F32 = jnp.float32
BF16 = jnp.bfloat16
EPS = 1e-6
NEG_INF = -1e30
HEAD_DIM = 64
N_Q_HEADS = 16
N_KV_HEADS = 4
GROUP = N_Q_HEADS // N_KV_HEADS
ATTN_WIDTH = N_Q_HEADS * HEAD_DIM
KV_WIDTH = N_KV_HEADS * HEAD_DIM
POOL_WINDOWS = (2, 4, 8, 16)
POOL_GROUP_DIM = 256
POOL_WIDTH = len(POOL_WINDOWS) * POOL_GROUP_DIM
BLOCK = 128
GRID_W = 64
ROPE_BASE = 10000.0
SCALE = HEAD_DIM ** -0.5
HALO = 16
STRIP = 16
ADAM_LR, ADAM_B1, ADAM_B2, ADAM_EPS, ADAM_WD, ADAM_STEP = 0.001, 0.9, 0.999, 1e-08, 0.01, 10
MESH = pl.DeviceIdType.MESH
MIB = 1024 * 1024
ANY = pl.BlockSpec(memory_space=pl.ANY)


def _cp(n_axes, vmem_mib=48):
    return pltpu.CompilerParams(dimension_semantics=("arbitrary",) * n_axes, vmem_limit_bytes=vmem_mib * MIB)


def _fold8(v):
    s = v[0:8]
    for t in range(1, v.shape[0] // 8):
        s = s + v[8 * t:8 * t + 8]
    return s


def _dot(a, b):
    return jnp.dot(a, b, preferred_element_type=F32)


def _dot_nt(a, b):
    return lax.dot_general(a, b, (((1,), (1,)), ((), ())), preferred_element_type=F32)


def _dot_tn(a, b):
    return lax.dot_general(a, b, (((0,), (0,)), ((), ())), preferred_element_type=F32)


def _pick(n, *cands):
    for t in cands:
        if n % t == 0:
            return t
    return n


def _flip(pos, mask):
    return tuple((1 - v) if (mask >> (2 - i)) & 1 else v for i, v in enumerate(pos))


def _exchange(name, ins, out_shapes, remote, local=(), aliases=None):
    n_io = len(ins) + len(out_shapes)

    def body(*refs):
        io = refs[:n_io]
        send_sems, recv_sems, local_sems = refs[n_io:]
        me = (lax.axis_index("x"), lax.axis_index("y"), lax.axis_index("c"))

        def copy(i, sender):
            mask, src_fn, dst_fn = remote[i]
            return pltpu.make_async_remote_copy(
                src_ref=src_fn(io, sender), dst_ref=dst_fn(io, sender), send_sem=send_sems.at[i],
                recv_sem=recv_sems.at[i], device_id=_flip(sender, mask), device_id_type=MESH)

        own = [pltpu.make_async_copy(s(io, me), d(io, me), local_sems.at[i]) for i, (s, d) in enumerate(local)]
        for cp in own:
            cp.start()
        sends = [copy(i, me) for i in range(len(remote))]
        for cp in sends:
            cp.start()
        for i in range(len(remote)):
            copy(i, _flip(me, remote[i][0])).wait_recv()
        for cp in sends:
            cp.wait_send()
        for cp in own:
            cp.wait()

    return pl.pallas_call(
        body, name=name, out_shape=tuple(out_shapes),
        in_specs=[ANY] * len(ins), out_specs=tuple([ANY] * len(out_shapes)),
        scratch_shapes=[pltpu.SemaphoreType.DMA((len(remote),)), pltpu.SemaphoreType.DMA((len(remote),)),
                        pltpu.SemaphoreType.DMA((max(len(local), 1),))],
        input_output_aliases=aliases or {},
    )(*ins)


HBM = pl.BlockSpec(memory_space=pltpu.HBM)
SEM = pl.BlockSpec(memory_space=pltpu.SEMAPHORE)
EFFECT = pltpu.SideEffectType.DATAFLOW_SIDE_EFFECTING


def _split_copy(remote, i, io, send_sems, recv_sems, sender):
    mask, src_fn, dst_fn = remote[i]
    return pltpu.make_async_remote_copy(
        src_ref=src_fn(io, sender), dst_ref=dst_fn(io, sender), send_sem=send_sems.at[i],
        recv_sem=recv_sems.at[i], device_id=_flip(sender, mask), device_id_type=MESH)


def _exchange_start(name, bufs, remote, after=None):
    n, r = len(bufs), len(remote)
    more = [] if after is None else [after]

    def body(*refs):
        io, (send_sems, recv_sems, token) = refs[:n], refs[-3:]
        me = (lax.axis_index("x"), lax.axis_index("y"), lax.axis_index("c"))
        for i in range(r):
            _split_copy(remote, i, io, send_sems, recv_sems, me).start()
        token[...] = jnp.zeros_like(token)

    res = pl.pallas_call(
        body, name=name,
        out_shape=tuple(pltpu.HBM(b.shape, b.dtype) for b in bufs)
        + (pltpu.SemaphoreType.DMA((r,)), pltpu.SemaphoreType.DMA((r,)), jax.ShapeDtypeStruct((8, 128), F32)),
        in_specs=[HBM] * n + [ANY] * len(more), out_specs=tuple([HBM] * n) + (SEM, SEM, pl.BlockSpec(memory_space=pltpu.VMEM)),
        input_output_aliases={i: i for i in range(n)}, compiler_params=pltpu.CompilerParams(has_side_effects=EFFECT),
    )(*[pltpu.with_memory_space_constraint(b, pltpu.HBM) for b in bufs], *more)
    return list(res[:n]), res[n], res[n + 1], res[n + 2]


def _exchange_wait(name, bufs, send_sems, recv_sems, remote, after):
    n, r = len(bufs), len(remote)

    def body(*refs):
        io, ss, rs = refs[:n], refs[n], refs[n + 1]
        me = (lax.axis_index("x"), lax.axis_index("y"), lax.axis_index("c"))
        for i in range(r):
            _split_copy(remote, i, io, ss, rs, _flip(me, remote[i][0])).wait_recv()
        for i in range(r):
            _split_copy(remote, i, io, ss, rs, me).wait_send()

    return list(pl.pallas_call(
        body, name=name, out_shape=tuple(pltpu.HBM(b.shape, b.dtype) for b in bufs),
        in_specs=[HBM] * n + [SEM, SEM, ANY], out_specs=tuple([HBM] * n),
        input_output_aliases={i: i for i in range(n)}, compiler_params=pltpu.CompilerParams(has_side_effects=EFFECT),
    )(*bufs, send_sems, recv_sems, after))


def _my_c():
    return lax.axis_index("c")


def _my_chip():
    return 2 * lax.axis_index("x") + lax.axis_index("y")


def _dev_index(pos):
    return 4 * pos[0] + 2 * pos[1] + pos[2]


def _chip_index(pos):
    return 2 * pos[0] + pos[1]


def _allgather8(name, v):
    out = jax.ShapeDtypeStruct((8,) + v.shape, v.dtype)
    remote = [(mask, lambda io, pos: io[0], lambda io, pos: io[1].at[_dev_index(pos)]) for mask in range(1, 8)]
    local = [(lambda io, pos: io[0], lambda io, pos: io[1].at[_dev_index(pos)])]
    return _exchange(name, [v], [out], remote, local)[0]


class _Big:
    def __init__(self, kind, shard_shape):
        self.kind = kind
        self.shard_shape = tuple(shard_shape)
        if kind == "col":
            r, cs = shard_shape
            self.full_shape = (r, 4 * cs)
            self.piece_shape = (r // 2, cs)
            self.half_shape = (r // 2, 4 * cs)
        elif kind == "row":
            rs, c = shard_shape
            self.full_shape = (4, 2, rs // 2, c)
            self.piece_shape = (1, 1, rs // 2, c)
            self.half_shape = (4, 1, rs // 2, c)
        else:
            self.full_shape = (4, 256, 256)
            self.piece_shape = (2, 64, 256)
            self.half_shape = (2, 256, 256)

    def shard_as_pieces(self, a):
        return a.reshape((1, 2) + self.piece_shape[2:]) if self.kind == "row" else a

    def piece(self, ref, k, h):
        if self.kind == "col":
            r, cs = self.piece_shape
            return ref.at[pl.ds(h * r, r), pl.ds(k * cs, cs)]
        if self.kind == "row":
            return ref.at[pl.ds(k, 1), pl.ds(h, 1)]
        return ref.at[pl.ds(2 * h, 2), pl.ds(64 * k, 64)]

    def half_of_shard(self, ref, h):
        if self.kind == "col":
            return ref.at[pl.ds(h * self.piece_shape[0], self.piece_shape[0])]
        if self.kind == "row":
            return ref.at[:, pl.ds(h, 1)]
        return ref.at[pl.ds(2 * h, 2)]

    def half_of_full(self, ref, h):
        if self.kind == "col":
            return ref.at[pl.ds(h * self.half_shape[0], self.half_shape[0])]
        if self.kind == "row":
            return ref.at[:, pl.ds(h, 1)]
        return ref.at[pl.ds(2 * h, 2)]

    def piece_of_half(self, ref, k):
        if self.kind == "col":
            return ref.at[:, pl.ds(k * self.piece_shape[1], self.piece_shape[1])]
        if self.kind == "row":
            return ref.at[pl.ds(k, 1)]
        return ref.at[:, pl.ds(64 * k, 64)]


CHIP_MASKS = (4, 2, 6)


def _cast_place(name, big, shard, after):
    if big.kind == "col":
        r, cs = big.shard_shape
        tr = _pick(r, 512, 256, 128)
        src, grid, blk = shard, (r // tr,), (tr, cs)
        imap, omap = (lambda i: (i, 0)), (lambda i: (i, _my_chip()))
    elif big.kind == "row":
        rs, c = big.shard_shape
        tr = _pick(rs // 2, 256, 128)
        src, grid, blk = big.shard_as_pieces(shard), (2, rs // 2 // tr), (1, 1, tr, c)
        imap, omap = (lambda h, i: (0, h, i, 0)), (lambda h, i: (_my_chip(), h, i, 0))
    else:
        src, grid, blk = shard, (1,), big.shard_shape
        imap, omap = (lambda i: (0, 0, 0)), (lambda i: (0, _my_chip(), 0))

    def body(s_ref, after_ref, o_ref, token_ref):
        o_ref[...] = s_ref[...].astype(BF16)
        token_ref[...] = jnp.zeros_like(token_ref)

    return pl.pallas_call(
        body, name=name, grid=grid, in_specs=[pl.BlockSpec(blk, imap), ANY],
        out_specs=(pl.BlockSpec(blk, omap), pl.BlockSpec((8, 128), lambda *_: (0, 0))),
        out_shape=(jax.ShapeDtypeStruct(big.full_shape, BF16), jax.ShapeDtypeStruct((8, 128), F32)), compiler_params=_cp(len(grid)),
    )(src, after)


def _gather_ici_remote(bigs, off):
    remote = []
    for a, b in enumerate(bigs):
        for mask in CHIP_MASKS:
            def mine(io, p, a=a, b=b):
                return b.piece(io[off + a], _chip_index(p), p[2])
            remote.append((mask, mine, mine))
    return remote


def _gather_d2d_remote(bigs, off):
    remote = []
    for a, b in enumerate(bigs):
        for mask in CHIP_MASKS:
            def region(io, p, a=a, b=b, mask=mask):
                return b.piece(io[off + a], _chip_index(_flip(p, mask)), p[2])
            remote.append((1, region, region))
    return remote


def _ew(name, fn, ins, out_dtypes, after, rows_per_step=256):
    shape = ins[0].shape
    last = shape[-1]
    rows = 1
    for s in shape[:-1]:
        rows *= s
    ins2 = [a.reshape(rows, last) for a in ins]
    tr = _pick(rows, rows_per_step, 128, 64, 32, 16, 8)
    spec = pl.BlockSpec((tr, last), lambda i: (i, 0))

    def body(*refs):
        outs = fn(*[r[...] for r in refs[:len(ins)]])
        for o_ref, o in zip(refs[len(ins) + 1:], outs):
            o_ref[...] = o.astype(o_ref.dtype)

    outs = pl.pallas_call(
        body, name=name, grid=(rows // tr,), in_specs=[spec] * len(ins) + [ANY], out_specs=tuple([spec] * len(out_dtypes)),
        out_shape=tuple(jax.ShapeDtypeStruct((rows, last), d) for d in out_dtypes), compiler_params=_cp(1),
    )(*ins2, after)
    return [o.reshape(shape) for o in outs]


def _chip_sum(name, big, grad, from_sibling):
    if big.kind == "col":
        rh, w = big.half_shape
        tr = _pick(rh, 256, 128)
        nb = rh // tr
        grid, blk = (nb,), (tr, w)
        gmap, hmap = (lambda i: (_my_c() * nb + i, 0)), (lambda i: (i, 0))
    elif big.kind == "row":
        rh, w = big.half_shape[2:]
        tr = _pick(rh, 256, 128)
        grid, blk = (4, rh // tr), (1, 1, tr, w)
        gmap, hmap = (lambda k, i: (k, _my_c(), i, 0)), (lambda k, i: (k, 0, i, 0))
    else:
        grid, blk = (1,), big.half_shape
        gmap, hmap = (lambda i: (_my_c(), 0, 0)), (lambda i: (0, 0, 0))

    def body(g_ref, s_ref, o_ref):
        o_ref[...] = (g_ref[...].astype(F32) + s_ref[...].astype(F32)).astype(BF16)

    return pl.pallas_call(
        body, name=name, grid=grid, in_specs=[pl.BlockSpec(blk, gmap), pl.BlockSpec(blk, hmap)],
        out_specs=pl.BlockSpec(blk, hmap), out_shape=jax.ShapeDtypeStruct(big.half_shape, BF16), compiler_params=_cp(len(grid)),
    )(grad, from_sibling)


def _piece_sum(name, big, chip_sum, thirds):
    if big.kind == "col":
        rp, cs = big.piece_shape
        tr = _pick(rp, 256, 128)
        nb = rp // tr
        grid, blk, tblk = (nb,), (tr, cs), (1, tr, cs)
        smap, omap = (lambda i: (i, _my_chip())), (lambda i: (_my_c() * nb + i, 0))
        tmap = lambda j: (lambda i: (j, i, 0))
        out_shape = big.shard_shape
    elif big.kind == "row":
        rp, w = big.piece_shape[2:]
        tr = _pick(rp, 256, 128)
        grid, blk, tblk = (rp // tr,), (1, 1, tr, w), (1, 1, 1, tr, w)
        smap, omap = (lambda i: (_my_chip(), 0, i, 0)), (lambda i: (0, _my_c(), i, 0))
        tmap = lambda j: (lambda i: (j, 0, 0, i, 0))
        out_shape = (1, 2, rp, w)
    else:
        grid, blk, tblk = (1,), big.piece_shape, (1,) + big.piece_shape
        smap, omap = (lambda i: (0, _my_chip(), 0)), (lambda i: (_my_c(), 0, 0))
        tmap = lambda j: (lambda i: (j, 0, 0, 0))
        out_shape = big.shard_shape

    def body(s_ref, t0, t1, t2, o_ref):
        o_ref[...] = s_ref[...].astype(F32) + t0[0].astype(F32) + t1[0].astype(F32) + t2[0].astype(F32)

    return pl.pallas_call(
        body, name=name, grid=grid,
        in_specs=[pl.BlockSpec(blk, smap)] + [pl.BlockSpec(tblk, tmap(j)) for j in range(3)],
        out_specs=pl.BlockSpec(blk, omap), out_shape=jax.ShapeDtypeStruct(out_shape, F32), compiler_params=_cp(len(grid)),
    )(chip_sum, thirds, thirds, thirds)


def _split(name, bufs, remote, after=None):
    return _exchange_start(name + "_start", bufs, remote, after) + (remote, name)


def _join(handle, after):
    bufs, send_sems, recv_sems, _, remote, name = handle
    return _exchange_wait(name + "_wait", bufs, send_sems, recv_sems, remote, after)


def _allgather8_split(name, v, me):
    own = lax.dynamic_update_slice(lax.empty((8,) + v.shape, v.dtype), v[None], (me, 0, 0))
    remote = [(mask, lambda io, pos: io[0], lambda io, pos: io[1].at[_dev_index(pos)]) for mask in range(1, 8)]
    return _split(name, [v, own], remote)


def _reduce_d2d_remote(bigs):
    n = len(bigs)
    return [(1, lambda io, p, a=a, b=b: b.half_of_full(io[a], 1 - p[2]), lambda io, p, a=a: io[n + a])
            for a, b in enumerate(bigs)]


def _halves(bigs):
    return [jax.ShapeDtypeStruct(b.half_shape, BF16) for b in bigs]


def _chip_sums(tag, bigs, grads, from_sibling):
    return [_chip_sum(f"reduce_{tag}_chip_sum_{a}", b, g, r) for a, (b, g, r) in enumerate(zip(bigs, grads, from_sibling))]


def _reduce_to_chip(tag, bigs, grads):
    from_sibling = _exchange(f"reduce_{tag}_d2d", grads, _halves(bigs), _reduce_d2d_remote(bigs))
    return _chip_sums(tag, bigs, grads, from_sibling)


def _reduce_ici_remote(bigs):
    n = len(bigs)
    remote = []
    for a, b in enumerate(bigs):
        for j, mask in enumerate(CHIP_MASKS):
            remote.append((mask,
                           lambda io, p, a=a, b=b, mask=mask: b.piece_of_half(io[a], _chip_index(_flip(p, mask))),
                           lambda io, p, a=a, j=j: io[n + a].at[j]))
    return remote


def _thirds(bigs):
    return [jax.ShapeDtypeStruct((3,) + b.piece_shape, BF16) for b in bigs]


def _piece_sums(tag, bigs, chip_sum, from_chips):
    return [_piece_sum(f"reduce_{tag}_sum_{a}", b, s, r) for a, (b, s, r) in enumerate(zip(bigs, chip_sum, from_chips))]


def _share_remote(bigs, off):
    remote = []
    for a, b in enumerate(bigs):
        def mine(io, p, a=a, b=b):
            return b.half_of_shard(io[off + a], p[2])
        remote.append((1, mine, mine))
    return remote


def _reduce_finish(tag, bigs, chip_sum, from_chips):
    n = len(bigs)
    placed = _piece_sums(tag, bigs, chip_sum, from_chips)
    out = _exchange(f"reduce_{tag}_share_d2d", placed, [jax.ShapeDtypeStruct(p.shape, F32) for p in placed],
                    _share_remote(bigs, n), aliases={a: a for a in range(n)})
    return [o.reshape(b.shard_shape) for o, b in zip(out, bigs)]


def _mm(name, a, b, *, nt, tm, tn, tk, epi, extras=(), extra_specs=(), out_shape, out_specs, after=None, vmem_mib=48):
    m, kdim = a.shape
    n = b.shape[0] if nt else b.shape[1]
    gm, gn, gk = m // tm, n // tn, kdim // tk
    a_spec = pl.BlockSpec((tm, tk), lambda j, i, k: (i, k))
    b_spec = pl.BlockSpec((tn, tk), lambda j, i, k: (j, k)) if nt else pl.BlockSpec((tk, tn), lambda j, i, k: (k, j))
    n_ex = len(extras)
    if after is not None:
        extras, extra_specs = tuple(extras) + (after,), list(extra_specs) + [ANY]

    def body(a_ref, b_ref, *rest):
        ex, outs, acc = rest[:n_ex], rest[len(extras):-1], rest[-1]
        dot = _dot_nt if nt else _dot
        if gk == 1:
            acc[...] = dot(a_ref[...], b_ref[...])
            epi(acc, ex, outs)
        else:
            k = pl.program_id(2)

            @pl.when(k == 0)
            def _():
                acc[...] = dot(a_ref[...], b_ref[...])

            @pl.when(k > 0)
            def _():
                acc[...] += dot(a_ref[...], b_ref[...])

            @pl.when(k == gk - 1)
            def _():
                epi(acc, ex, outs)

    return pl.pallas_call(
        body, name=name, grid=(gn, gm, gk), in_specs=[a_spec, b_spec, *extra_specs], out_specs=tuple(out_specs),
        out_shape=tuple(out_shape), scratch_shapes=[pltpu.VMEM((tm, tn), F32)], compiler_params=_cp(3, vmem_mib),
    )(a, b, *extras)


def _mm_deferred(name, a, b, *, nt, tm, epi, tiles, vecs, out_tiles, n_stats, after, vmem_mib=48):
    pieces = a if isinstance(a, (list, tuple)) else [(a, 0)]
    m = pieces[0][0].shape[0]
    n = b.shape[0] if nt else b.shape[1]
    gm = m // tm
    n_a, n_t, n_v, n_o = len(pieces), len(tiles), len(vecs), len(out_tiles)

    def body(*refs):
        a_refs, b_ref, rest = refs[:n_a], refs[n_a], refs[n_a + 1:]
        t_refs, v_refs = rest[:n_t], rest[n_t:n_t + n_v]
        o_refs, st_ref, acc0, acc1 = rest[n_t + n_v + 1:n_t + n_v + 1 + n_o], rest[-3], rest[-2], rest[-1]
        i = pl.program_id(0)

        def dot():
            if n_a == 1 and pieces[0][0].shape[1] == b.shape[1 if nt else 0]:
                return (_dot_nt if nt else _dot)(a_refs[0][...], b_ref[...])
            parts = [_dot_nt(r[...], b_ref[:, off:off + p.shape[1]]) for r, (p, off) in zip(a_refs, pieces)]
            return functools.reduce(lambda u, v: u + v, parts)

        @pl.when(i == 0)
        def _():
            acc1[...] = jnp.zeros_like(acc1)
            st_ref[...] = jnp.zeros_like(st_ref)

        def finish(prev):
            for r0 in range(0, tm, STRIP):
                rs = slice(r0, r0 + STRIP)
                epi(prev[rs, :], rs, t_refs, v_refs, o_refs, st_ref, i > 0)

        @pl.when((i % 2 == 0) & (i < gm))
        def _():
            acc0[...] = dot()
            finish(acc1)

        @pl.when((i % 2 == 1) & (i < gm))
        def _():
            acc1[...] = dot()
            finish(acc0)

        @pl.when(i == gm)
        def _():
            finish(acc1 if gm % 2 == 0 else acc0)

    prev = lambda i: (jnp.maximum(i - 1, 0), 0)
    tile = pl.BlockSpec((tm, n), prev)
    return pl.pallas_call(
        body, name=name, grid=(gm + 1,),
        in_specs=[pl.BlockSpec((tm, p.shape[1]), lambda i: (jnp.minimum(i, gm - 1), 0)) for p, _ in pieces]
        + [pl.BlockSpec(b.shape, lambda i: (0, 0))] + [tile] * n_t + [_row_spec(n)] * n_v + [ANY],
        out_specs=tuple([tile] * n_o) + (_stat_spec(n_stats, n),),
        out_shape=tuple(out_tiles) + (jax.ShapeDtypeStruct((n_stats, 8, n), F32),),
        scratch_shapes=[pltpu.VMEM((tm, n), F32), pltpu.VMEM((tm, n), F32)], compiler_params=_cp(1, vmem_mib),
    )(*[p for p, _ in pieces], b, *tiles, *vecs, after)


def _mm_k_deferred(name, a, b, *, nt, tm, tk, epi, tiles, vecs, out_tiles, n_stats, after, vmem_mib=56):
    m, kdim = a.shape
    n = b.shape[0] if nt else b.shape[1]
    gm, gk = m // tm, kdim // tk
    rows = tm // gk
    n_t, n_v, n_o = len(tiles), len(vecs), len(out_tiles)
    dot = _dot_nt if nt else _dot

    def body(a_ref, b_ref, *rest):
        t_refs, v_refs = rest[:n_t], rest[n_t:n_t + n_v]
        o_refs, st_ref, acc0, acc1 = rest[n_t + n_v + 1:n_t + n_v + 1 + n_o], rest[-3], rest[-2], rest[-1]
        i, k = pl.program_id(0), pl.program_id(1)

        @pl.when((i == 0) & (k == 0))
        def _():
            acc1[...] = jnp.zeros_like(acc1)
            st_ref[...] = jnp.zeros_like(st_ref)

        def finish(prev):
            for r0 in range(0, rows, STRIP):
                acc_rows = prev[pl.ds(pl.multiple_of(k * rows + r0, STRIP), STRIP), :]
                epi(acc_rows, slice(r0, r0 + STRIP), t_refs, v_refs, o_refs, st_ref, i > 0)

        def step(cur, prev):
            cur[...] = jnp.where(k > 0, cur[...], 0.0) + dot(a_ref[...], b_ref[...])
            finish(prev)

        @pl.when((i % 2 == 0) & (i < gm))
        def _():
            step(acc0, acc1)

        @pl.when((i % 2 == 1) & (i < gm))
        def _():
            step(acc1, acc0)

        @pl.when(i == gm)
        def _():
            finish(acc1 if gm % 2 == 0 else acc0)

    prev = lambda i, k: (jnp.where(i == 0, 0, (i - 1) * gk + k), 0)
    part = pl.BlockSpec((rows, n), prev)
    b_spec = pl.BlockSpec((n, tk), lambda i, k: (0, k)) if nt else pl.BlockSpec((tk, n), lambda i, k: (k, 0))
    return pl.pallas_call(
        body, name=name, grid=(gm + 1, gk),
        in_specs=[pl.BlockSpec((tm, tk), lambda i, k: (jnp.minimum(i, gm - 1), k)), b_spec]
        + [part] * n_t + [_row_spec(n)] * n_v + [ANY],
        out_specs=tuple([part] * n_o) + (_stat_spec(n_stats, n),),
        out_shape=tuple(out_tiles) + (jax.ShapeDtypeStruct((n_stats, 8, n), F32),),
        scratch_shapes=[pltpu.VMEM((tm, n), F32), pltpu.VMEM((tm, n), F32)], compiler_params=_cp(2, vmem_mib),
    )(a, b, *tiles, *vecs, after)


def _mm_tn(name, a, b, out_dtype, *, tmo, tn, tt, more=(), after=None, vmem_mib=56):
    t, m = a.shape
    n = b.shape[1]
    gt = t // tt
    wait_for = [] if after is None else [after]

    def body(a_ref, b_ref, *rest):
        o_ref, acc = rest[-2:]
        k = pl.program_id(2)

        @pl.when(k == 0)
        def _():
            first = _dot_tn(a_ref[...], b_ref[...])
            acc[...] = first + _dot_tn(rest[0][...], rest[1][...]) if more else first

        @pl.when(k > 0)
        def _():
            acc[...] += _dot_tn(a_ref[...], b_ref[...])

        @pl.when(k == gt - 1)
        def _():
            o_ref[...] = acc[...].astype(o_ref.dtype)

    more_specs = [pl.BlockSpec((more[0].shape[0], tmo), lambda i, j, k: (0, i)),
                  pl.BlockSpec((more[1].shape[0], tn), lambda i, j, k: (0, j))] if more else []
    return pl.pallas_call(
        body, name=name, grid=(m // tmo, n // tn, gt),
        in_specs=[pl.BlockSpec((tt, tmo), lambda i, j, k: (k, i)), pl.BlockSpec((tt, tn), lambda i, j, k: (k, j))] + more_specs
        + [ANY] * len(wait_for),
        out_specs=pl.BlockSpec((tmo, tn), lambda i, j, k: (i, j)), out_shape=jax.ShapeDtypeStruct((m, n), out_dtype),
        scratch_shapes=[pltpu.VMEM((tmo, tn), F32)], compiler_params=_cp(3, vmem_mib),
    )(a, b, *more, *wait_for)


def _row_spec(d):
    return pl.BlockSpec((1, d), lambda *_: (0, 0))


def _stat_spec(k, d):
    return pl.BlockSpec((k, 8, d), lambda *_: (0, 0, 0))


def _rope(z, cs, sn):
    first = (lax.broadcasted_iota(jnp.int32, (z.shape[0], 128), 1) % 32) < 16
    outs = []
    for j in range(z.shape[1] // 128):
        zc = z[:, 128 * j:128 * (j + 1)]
        partner = jnp.where(first, pltpu.roll(zc, 112, 1), pltpu.roll(zc, 16, 1))
        outs.append(zc * cs + partner * sn)
    return outs[0] if len(outs) == 1 else jnp.concatenate(outs, axis=1)


def _rope_tables(length, rotate, zero=0.0):
    if not rotate:
        return jnp.ones((length, 128), F32), jnp.zeros((length, 128), F32)
    half = HEAD_DIM // 2
    t = jnp.arange(length)
    row = (t // GRID_W).astype(F32) + zero
    col = (t % GRID_W).astype(F32)
    e = jnp.arange(128) % HEAD_DIM
    inv_freq = ROPE_BASE ** (-(2 * ((e % half) % (half // 2))).astype(F32) / half)
    pos = jnp.where(e[None, :] < half, row[:, None], col[:, None])
    ang = pos * inv_freq[None, :]
    first = ((e % half) < half // 2)[None, :]
    return jnp.cos(ang), jnp.where(first, -jnp.sin(ang), jnp.sin(ang))


def _mixer_in(name, x, nw, sh, sc, w_in, cos, sin, after):
    t, d = x.shape
    tm = _pick(t, 512, 256, 128)
    n_in = w_in.shape[1]

    def body(x_ref, nw_ref, sh_ref, sc_ref, w_ref, cos_ref, sin_ref, after_ref, h_ref, q_ref, k_ref, v_ref, u_ref):
        xf = x_ref[...]
        r = lax.rsqrt(jnp.mean(xf * xf, axis=-1, keepdims=True) + EPS)
        hb = (((xf * r) * nw_ref[...]) * (1.0 + sc_ref[...]) + sh_ref[...]).astype(BF16)
        h_ref[...] = hb
        p = _dot(hb, w_ref[...])
        cs, sn = cos_ref[...], sin_ref[...]
        q_ref[...] = (_rope(p[:, :ATTN_WIDTH], cs, sn) * SCALE).astype(BF16)
        k_ref[...] = _rope(p[:, ATTN_WIDTH:ATTN_WIDTH + KV_WIDTH], cs, sn).astype(BF16)
        v_ref[...] = p[:, ATTN_WIDTH + KV_WIDTH:ATTN_WIDTH + 2 * KV_WIDTH].astype(BF16)
        u_ref[...] = p[:, ATTN_WIDTH + 2 * KV_WIDTH:]

    def tile(w):
        return pl.BlockSpec((tm, w), lambda i: (i, 0))

    return pl.pallas_call(
        body, name=name, grid=(t // tm,),
        in_specs=[tile(d), _row_spec(d), _row_spec(d), _row_spec(d), pl.BlockSpec((d, n_in), lambda i: (0, 0)),
                  tile(128), tile(128), ANY],
        out_specs=(tile(d), tile(ATTN_WIDTH), tile(KV_WIDTH), tile(KV_WIDTH), tile(POOL_WIDTH)),
        out_shape=(jax.ShapeDtypeStruct((t, d), BF16), jax.ShapeDtypeStruct((t, ATTN_WIDTH), BF16),
                   jax.ShapeDtypeStruct((t, KV_WIDTH), BF16), jax.ShapeDtypeStruct((t, KV_WIDTH), BF16),
                   jax.ShapeDtypeStruct((t, POOL_WIDTH), F32)),
        compiler_params=_cp(1, 56),
    )(x, nw, sh, sc, w_in, cos, sin, after)


def _attn_specs(nb, n_ctx):
    def blk(w, f):
        return pl.BlockSpec((BLOCK, w), lambda n: (f(n), 0))

    prev = lambda n: jnp.maximum(jnp.minimum(n, nb - 1) - 1, 0)
    cur = lambda n: jnp.minimum(n, nb - 1)
    nxt = lambda n: jnp.minimum(n + 1, nb - 1)
    kv = [blk(KV_WIDTH, prev), blk(KV_WIDTH, cur), blk(KV_WIDTH, nxt)]
    ctx = pl.BlockSpec((n_ctx, KV_WIDTH), lambda n: (0, 0))
    return [pl.BlockSpec(memory_space=pltpu.SMEM), blk(ATTN_WIDTH, cur)] + kv + kv + [ctx, ctx]


def _attn_mask(n, length, n_keys):
    row = lax.broadcasted_iota(jnp.int32, (GROUP * BLOCK, n_keys), 0) % BLOCK
    col = lax.broadcasted_iota(jnp.int32, (GROUP * BLOCK, n_keys), 1)
    kpos = (n - 1) * BLOCK + col
    return ((jnp.abs(col - BLOCK - row) <= BLOCK) & (kpos >= 0) & (kpos < length)) | (col >= 3 * BLOCK)


def _group_rows(block, g):
    return jnp.concatenate([block[:, HEAD_DIM * h:HEAD_DIM * (h + 1)] for h in range(GROUP * g, GROUP * (g + 1))], axis=0)


def _group_sink(sink_ref, g):
    head = lax.broadcasted_iota(jnp.int32, (GROUP * BLOCK, 1), 0) // BLOCK
    out = jnp.full((GROUP * BLOCK, 1), sink_ref[0, GROUP * g], F32)
    for j in range(1, GROUP):
        out = jnp.where(head == j, sink_ref[0, GROUP * g + j], out)
    return out


def _attn_fwd(q, k, v, kc, vc, sink):
    length = q.shape[0]
    nb = length // BLOCK
    n_ctx = kc.shape[0]
    n_keys = 3 * BLOCK + n_ctx

    def body(sink_ref, q_ref, kp, k0, kn, vp, v0, vn, kc_ref, vc_ref, o_ref, p_ref):
        n = pl.program_id(0)
        valid = _attn_mask(n, length, n_keys)
        qb = q_ref[...]
        kall = jnp.concatenate([kp[...], k0[...], kn[...], kc_ref[...]], axis=0)
        vall = jnp.concatenate([vp[...], v0[...], vn[...], vc_ref[...]], axis=0)
        outs = []
        for g in range(N_KV_HEADS):
            lanes = slice(HEAD_DIM * g, HEAD_DIM * (g + 1))
            s = jnp.where(valid, _dot_nt(_group_rows(qb, g), kall[:, lanes]), NEG_INF)
            sk = _group_sink(sink_ref, g)
            m = jnp.maximum(jnp.max(s, axis=-1, keepdims=True), sk)
            e = jnp.exp(s - m)
            e_sink = jnp.exp(sk - m)
            inv = 1.0 / (jnp.sum(e, axis=-1, keepdims=True) + e_sink)
            p_ref[0, g, :, :n_keys] = (e * inv).astype(BF16)
            p_ref[0, g, :, n_keys:] = jnp.broadcast_to(e_sink * inv, (GROUP * BLOCK, 128)).astype(BF16)
            o = _dot(p_ref[0, g, :, :n_keys], vall[:, lanes])
            outs += [o[BLOCK * j:BLOCK * (j + 1)] for j in range(GROUP)]
        o_ref[...] = jnp.concatenate(outs, axis=1).astype(BF16)

    return pl.pallas_call(
        body, name="attn_fwd", grid=(nb,), in_specs=_attn_specs(nb, n_ctx),
        out_specs=(pl.BlockSpec((BLOCK, ATTN_WIDTH), lambda n: (n, 0)),
                   pl.BlockSpec((1, N_KV_HEADS, GROUP * BLOCK, n_keys + 128), lambda n: (n, 0, 0, 0))),
        out_shape=(jax.ShapeDtypeStruct((length, ATTN_WIDTH + POOL_WIDTH), BF16),
                   jax.ShapeDtypeStruct((nb, N_KV_HEADS, GROUP * BLOCK, n_keys + 128), BF16)), compiler_params=_cp(1),
    )(sink, q, k, k, k, v, v, v, kc, vc)


def _attn_bwd(q, k, v, kc, vc, dmix, probs, cos, sin):
    length = q.shape[0]
    nb = length // BLOCK
    n_ctx = kc.shape[0]
    n_keys = 3 * BLOCK + n_ctx

    def body(q_ref, kp, k0, kn, vp, v0, vn, kc_ref, vc_ref, do_ref, p_ref, cos_ref, sin_ref, cos_prev, sin_prev,
             dq_ref, dkv_ref, dkc_ref, dvc_ref, dsink_ref, done, ahead):
        n = pl.program_id(0)

        @pl.when(n == 0)
        def _():
            dkc_ref[...] = jnp.zeros_like(dkc_ref)
            dvc_ref[...] = jnp.zeros_like(dvc_ref)
            dsink_ref[...] = jnp.zeros_like(dsink_ref)
            done[...] = jnp.zeros_like(done)
            ahead[...] = jnp.zeros_like(ahead)

        def write_block(dkv):
            dkv_ref[:, :KV_WIDTH] = _rope(dkv[:, :KV_WIDTH], cos_prev[...], -sin_prev[...]).astype(BF16)
            dkv_ref[:, KV_WIDTH:] = dkv[:, KV_WIDTH:].astype(BF16)

        def query_block():
            qb, dob = q_ref[...], do_ref[...]
            kall = jnp.concatenate([kp[...], k0[...], kn[...], kc_ref[...]], axis=0)
            vall = jnp.concatenate([vp[...], v0[...], vn[...], vc_ref[...]], axis=0)
            srow = lax.broadcasted_iota(jnp.int32, (8, 128), 0)
            slane = lax.broadcasted_iota(jnp.int32, (8, 128), 1)
            dqs, dks, dvs = [], [], []
            dsink = jnp.zeros((8, 128), F32)
            for g in range(N_KV_HEADS):
                lanes = slice(HEAD_DIM * g, HEAD_DIM * (g + 1))
                kg, vg = kall[:, lanes], vall[:, lanes]
                qg, dog = _group_rows(qb, g), _group_rows(dob, g)
                pb = p_ref[0, g, :, :n_keys]
                p = pb.astype(F32)
                dp = _dot_nt(dog, vg)
                delta = jnp.sum(p * dp, axis=-1, keepdims=True)
                ds = (p * (dp - delta)).astype(BF16)
                dq = _dot(ds, kg) * SCALE
                dqs += [dq[BLOCK * j:BLOCK * (j + 1)] for j in range(GROUP)]
                dks.append(_dot_tn(ds, qg))
                dvs.append(_dot_tn(pb, dog))
                d_sink = p_ref[0, g, :, n_keys:].astype(F32)[:, :1] * delta
                for j in range(GROUP):
                    total = -jnp.sum(d_sink[BLOCK * j:BLOCK * (j + 1)], axis=0, keepdims=True)
                    dsink = dsink + jnp.where((srow == 0) & (slane == GROUP * g + j), total, 0.0)
            dq_ref[...] = _rope(jnp.concatenate(dqs, axis=1), cos_ref[...], -sin_ref[...]).astype(BF16)
            dkv = jnp.concatenate(dks + dvs, axis=1)
            write_block(done[...] + dkv[:BLOCK])
            done[...] = ahead[...] + dkv[BLOCK:2 * BLOCK]
            ahead[...] = dkv[2 * BLOCK:3 * BLOCK]
            dkc_ref[...] += dkv[3 * BLOCK:, :KV_WIDTH]
            dvc_ref[...] += dkv[3 * BLOCK:, KV_WIDTH:]
            dsink_ref[...] += dsink

        pl.when(n < nb)(query_block)

        @pl.when(n == nb)
        def _():
            write_block(done[...])

    here = lambda n: (jnp.minimum(n, nb - 1), 0)
    before = lambda n: (jnp.maximum(n - 1, 0), 0)
    ctx = pl.BlockSpec((n_ctx, KV_WIDTH), lambda n: (0, 0))
    return pl.pallas_call(
        body, name="attn_bwd", grid=(nb + 1,),
        in_specs=_attn_specs(nb, n_ctx)[1:] + [pl.BlockSpec((BLOCK, ATTN_WIDTH), here),
                                           pl.BlockSpec((1,) + probs.shape[1:], lambda n: (jnp.minimum(n, nb - 1), 0, 0, 0)),
                                           pl.BlockSpec((BLOCK, 128), here), pl.BlockSpec((BLOCK, 128), here),
                                           pl.BlockSpec((BLOCK, 128), before), pl.BlockSpec((BLOCK, 128), before)],
        out_specs=(pl.BlockSpec((BLOCK, ATTN_WIDTH), here), pl.BlockSpec((BLOCK, 2 * KV_WIDTH), before), ctx, ctx,
                   pl.BlockSpec((8, 128), lambda n: (0, 0))),
        out_shape=(jax.ShapeDtypeStruct((length, ATTN_WIDTH), BF16), jax.ShapeDtypeStruct((length, 2 * KV_WIDTH), BF16),
                   jax.ShapeDtypeStruct((n_ctx, KV_WIDTH), F32), jax.ShapeDtypeStruct((n_ctx, KV_WIDTH), F32),
                   jax.ShapeDtypeStruct((8, 128), F32)),
        scratch_shapes=[pltpu.VMEM((BLOCK, 2 * KV_WIDTH), F32), pltpu.VMEM((BLOCK, 2 * KV_WIDTH), F32)],
        compiler_params=_cp(1),
    )(q, k, k, k, v, v, v, kc, vc, dmix, probs, cos, sin, cos, sin)


def _shift_rows(e, s):
    n = e.shape[0]
    return e if s % n == 0 else pltpu.roll(e, (-s) % n, 0)


def _window_sum(e, w, first):
    s, n = e, 1
    while n < w:
        s = s + _shift_rows(s, n)
        n *= 2
    return _shift_rows(s, first)


def _pool_geometry(i, tm, length):
    pos = i * tm - HALO + lax.broadcasted_iota(jnp.int32, (tm + 2 * HALO, 1), 0)
    inside = (pos >= 0) & (pos < length)
    inv_counts = []
    for w in POOL_WINDOWS:
        lo = jnp.clip(pos - w // 2, 0, length)
        hi = jnp.clip(pos - w // 2 + w, 0, length)
        inv_counts.append(1.0 / jnp.maximum(hi - lo, 1).astype(F32))
    return inside, inv_counts


def _halo_specs(tm, width, length, col=0):
    per = tm // HALO
    last = length // HALO - 1
    return [pl.BlockSpec((HALO, width), lambda i: (jnp.maximum(i * per - 1, 0), col)),
            pl.BlockSpec((tm, width), lambda i: (i, col)),
            pl.BlockSpec((HALO, width), lambda i: (jnp.minimum((i + 1) * per, last), col))]


def _pooled(ext, inv_counts, tm):
    outs = []
    for g, w in enumerate(POOL_WINDOWS):
        e = ext[:, POOL_GROUP_DIM * g:POOL_GROUP_DIM * (g + 1)]
        mean = _window_sum(e, w, -(w // 2)) * inv_counts[g]
        outs.append((mean - e)[HALO:HALO + tm])
    return outs


def _pool_fwd(u, pool_w, pool_scale, mix):
    length = u.shape[0]
    tm = _pick(length, 512, 256, 128)

    def body(up, u0, un, w_ref, sc_ref, mix_ref, o_ref):
        inside, inv_counts = _pool_geometry(pl.program_id(0), tm, length)
        ext = jnp.where(inside, jnp.concatenate([up[...], u0[...], un[...]], axis=0), 0.0)
        pooled = _pooled(ext, inv_counts, tm)
        mixed = [_dot(pooled[g].astype(BF16), w_ref[g]) for g in range(len(POOL_WINDOWS))]
        o_ref[...] = (jnp.concatenate(mixed, axis=1) * sc_ref[...]).astype(BF16)

    return pl.pallas_call(
        body, name="pool_fwd", grid=(length // tm,),
        in_specs=_halo_specs(tm, POOL_WIDTH, length) + [pl.BlockSpec(pool_w.shape, lambda i: (0, 0, 0)), _row_spec(POOL_WIDTH), ANY],
        out_specs=pl.BlockSpec((tm, POOL_WIDTH), lambda i: (i, 1)),
        out_shape=jax.ShapeDtypeStruct(mix.shape, BF16), input_output_aliases={5: 0}, compiler_params=_cp(1),
    )(u, u, u, pool_w, pool_scale, mix)


def _pool_bwd(u, dmix, pool_w, pool_scale, after):
    length = u.shape[0]
    tm = _pick(length, 512, 256, 128)
    n_g = len(POOL_WINDOWS)

    def body(up, u0, un, dp_, d0, dn_, w_ref, sc_ref, after_ref, du_ref, dw_ref, dsc_ref):
        i = pl.program_id(0)

        @pl.when(i == 0)
        def _():
            dw_ref[...] = jnp.zeros_like(dw_ref)
            dsc_ref[...] = jnp.zeros_like(dsc_ref)

        inside, inv_counts = _pool_geometry(i, tm, length)
        ext = jnp.where(inside, jnp.concatenate([up[...], u0[...], un[...]], axis=0), 0.0)
        dext = jnp.where(inside, jnp.concatenate([dp_[...], d0[...], dn_[...]], axis=0).astype(F32), 0.0)
        dmixed = (dext * sc_ref[...]).astype(BF16)
        pooled = _pooled(ext, inv_counts, tm)
        dus, dscs = [], []
        for g, w in enumerate(POOL_WINDOWS):
            lanes = slice(POOL_GROUP_DIM * g, POOL_GROUP_DIM * (g + 1))
            dpooled = _dot_nt(dmixed[:, lanes], w_ref[g])
            spread = _window_sum(dpooled * inv_counts[g], w, -(w // 2 - 1))
            dus.append((spread - dpooled)[HALO:HALO + tm])
            pb = pooled[g].astype(BF16)
            dw_ref[g] += _dot_tn(pb, dmixed[HALO:HALO + tm, lanes])
            prod = dext[HALO:HALO + tm, lanes] * _dot(pb, w_ref[g])
            dscs.append(_fold8(prod))
        du_ref[...] = jnp.concatenate(dus, axis=1).astype(BF16)
        dsc_ref[...] += jnp.concatenate(dscs, axis=1)

    return pl.pallas_call(
        body, name="pool_bwd", grid=(length // tm,),
        in_specs=_halo_specs(tm, POOL_WIDTH, length) + _halo_specs(tm, POOL_WIDTH, length, col=1)
        + [pl.BlockSpec(pool_w.shape, lambda i: (0, 0, 0)), _row_spec(POOL_WIDTH), ANY],
        out_specs=(pl.BlockSpec((tm, POOL_WIDTH), lambda i: (i, 0)), pl.BlockSpec((n_g, POOL_GROUP_DIM, POOL_GROUP_DIM), lambda i: (0, 0, 0)),
                   pl.BlockSpec((8, POOL_WIDTH), lambda i: (0, 0))),
        out_shape=(jax.ShapeDtypeStruct((length, POOL_WIDTH), BF16), jax.ShapeDtypeStruct((n_g, POOL_GROUP_DIM, POOL_GROUP_DIM), F32),
                   jax.ShapeDtypeStruct((8, POOL_WIDTH), F32)),
        compiler_params=_cp(1),
    )(u, u, u, dmix, dmix, dmix, pool_w, pool_scale, after)


def _mixer_out(mix, w_out, x, g_a, nmw, sh_m, sc_m, after):
    t, d = x.shape

    def epi(mo, rs, tiles, vecs, outs, st_ref, live):
        ga, nw, sh, sc = vecs
        x1_ref, mo_ref, hm_ref = outs
        x1 = tiles[0][rs, :] + ga[...] * mo
        x1_ref[rs, :] = x1
        mo_ref[rs, :] = mo.astype(BF16)
        r = lax.rsqrt(jnp.mean(x1 * x1, axis=-1, keepdims=True) + EPS)
        hm_ref[rs, :] = (((x1 * r) * nw[...]) * (1.0 + sc[...]) + sh[...]).astype(BF16)

    return _mm_deferred("mixer_out", mix, w_out, nt=False, tm=_pick(t, 256, 128), epi=epi, tiles=(x,), vecs=(g_a, nmw, sh_m, sc_m),
                        out_tiles=(jax.ShapeDtypeStruct((t, d), F32), jax.ShapeDtypeStruct((t, d), BF16), jax.ShapeDtypeStruct((t, d), BF16)),
                        n_stats=1, after=after)[:3]


def _mlp_up(hm, w_up):
    t, d = hm.shape
    tm = _pick(t, 1024, 512, 256, 128)
    tn = 2048

    def epi(acc, ex, outs):
        outs[0][...] = jnp.square(jnp.maximum(acc[...], 0.0)).astype(BF16)

    return _mm("mlp_up", hm, w_up, nt=False, tm=tm, tn=tn, tk=d, epi=epi,
               out_shape=(jax.ShapeDtypeStruct((t, w_up.shape[1]), BF16),),
               out_specs=(pl.BlockSpec((tm, tn), lambda j, i, k: (i, j)),))[0]


def _mlp_down_loss(act, w_down, x1, target, g_m, fw, after):
    t, d = x1.shape

    def epi(dnv, rs, tiles, vecs, outs, st_ref, live):
        x1_ref, t_ref = tiles
        gm, fw_ref = vecs
        dx2_ref, ddn_ref = outs
        x2 = x1_ref[rs, :] + gm[...] * dnv
        r = lax.rsqrt(jnp.mean(x2 * x2, axis=-1, keepdims=True) + EPS)
        xh = x2 * r
        diff = xh * fw_ref[...] - t_ref[rs, :]
        dy = diff * (1.0 / d)
        dxh = dy * fw_ref[...]
        dx2 = r * (dxh - xh * jnp.mean(dxh * xh, axis=-1, keepdims=True))
        dx2_ref[rs, :] = dx2
        ddn_ref[rs, :] = (dx2 * gm[...]).astype(BF16)
        st_ref[0] += jnp.where(live, _fold8(diff * diff), 0.0)
        st_ref[1] += jnp.where(live, _fold8(dy * xh), 0.0)
        st_ref[2] += jnp.where(live, _fold8(dx2 * dnv), 0.0)

    return _mm_k_deferred("mlp_down_loss", act, w_down, nt=False, tm=_pick(t, 512, 256), tk=_pick(act.shape[1], 2048), epi=epi,
                          tiles=(x1, target), vecs=(g_m, fw), n_stats=3, after=after,
                          out_tiles=(jax.ShapeDtypeStruct((t, d), F32), jax.ShapeDtypeStruct((t, d), BF16)))


def _mlp_dx(dup, w_up, x1, dx2, mo, nmw, sc_m, g_a, after):
    t, d = x1.shape

    def epi(dh, rs, tiles, vecs, outs, st_ref, live):
        x1_ref, dx2_ref, mo_ref = tiles
        nw, sc, ga = vecs
        dx1_ref, dmi_ref = outs
        dx1 = _norm_bwd_rows(dh, x1_ref[rs, :], nw[...], sc[...], st_ref) + dx2_ref[rs, :]
        dx1_ref[rs, :] = dx1
        dmi_ref[rs, :] = (dx1 * ga[...]).astype(BF16)
        st_ref[3] += jnp.where(live, _fold8(dx1 * mo_ref[rs, :].astype(F32)), 0.0)

    return _mm_k_deferred("mlp_dx", dup, w_up, nt=True, tm=_pick(t, 512, 256), tk=_pick(dup.shape[1], 2048), epi=epi,
                          tiles=(x1, dx2, mo), vecs=(nmw, sc_m, g_a), n_stats=4, after=after,
                          out_tiles=(jax.ShapeDtypeStruct((t, d), F32), jax.ShapeDtypeStruct((t, d), BF16)))


def _mlp_dact(ddn, w_down, act):
    t, d = ddn.shape
    tm = _pick(t, 1024, 512, 256, 128)
    tn = 2048

    def epi(acc, ex, outs):
        outs[0][...] = (acc[...] * (2.0 * jnp.sqrt(ex[0][...]).astype(F32))).astype(BF16)

    tile = pl.BlockSpec((tm, tn), lambda j, i, k: (i, j))
    return _mm("mlp_dact", ddn, w_down, nt=True, tm=tm, tn=tn, tk=d, epi=epi, extras=(act,), extra_specs=[tile],
               out_shape=(jax.ShapeDtypeStruct(act.shape, BF16),), out_specs=(tile,), vmem_mib=56)[0]


def _norm_bwd_rows(dh, xv, nw, sc, st_ref):
    r = lax.rsqrt(jnp.mean(xv * xv, axis=-1, keepdims=True) + EPS)
    xh = xv * r
    dy = dh * (1.0 + sc)
    st_ref[0] += _fold8(dh)
    st_ref[1] += _fold8(dh * (xh * nw))
    st_ref[2] += _fold8(dy * xh)
    dxh = dy * nw
    return r * (dxh - xh * jnp.mean(dxh * xh, axis=-1, keepdims=True))


def _mixer_dmix(dmi, w_out, after):
    t, d = dmi.shape
    tm = _pick(t, 1024, 512, 256, 128)

    def epi(acc, ex, outs):
        outs[0][...] = acc[...].astype(BF16)

    n = w_out.shape[0]
    return _mm("mixer_dmix", dmi, w_out, nt=True, tm=tm, tn=n, tk=d, epi=epi, after=after,
               out_shape=(jax.ShapeDtypeStruct((t, n), BF16),), out_specs=(pl.BlockSpec((tm, n), lambda j, i, k: (i, 0)),))[0]


def _mixer_dx(name, dp, w_in, x, dx1, naw, sc_a, after):
    t, d = x.shape

    def epi(dh, rs, tiles, vecs, outs, st_ref, live):
        x_ref, dx1_ref = tiles
        nw, sc = vecs
        outs[0][rs, :] = _norm_bwd_rows(dh, x_ref[rs, :], nw[...], sc[...], st_ref) + dx1_ref[rs, :]

    return _mm_deferred(name, dp, w_in, nt=True, tm=_pick(t, 256, 128), epi=epi, tiles=(x, dx1), vecs=(naw, sc_a),
                        out_tiles=(jax.ShapeDtypeStruct((t, d), F32),), n_stats=3, after=after, vmem_mib=56)


def _silu(v):
    return v / (1.0 + jnp.exp(-v))


def _ada_fwd(cond, w_ada, b_ada):
    d, n = w_ada.shape
    tn = 512

    def body(c_ref, w_ref, b_ref, o_ref):
        o_ref[...] = _dot(_silu(c_ref[...]).astype(BF16), w_ref[...].astype(BF16)) + b_ref[...]

    return pl.pallas_call(
        body, name="ada_fwd", grid=(n // tn,),
        in_specs=[pl.BlockSpec(cond.shape, lambda j: (0, 0)), pl.BlockSpec((d, tn), lambda j: (0, j)), pl.BlockSpec((1, tn), lambda j: (0, j))],
        out_specs=pl.BlockSpec((cond.shape[0], tn), lambda j: (0, j)), out_shape=jax.ShapeDtypeStruct((cond.shape[0], n), F32),
        compiler_params=_cp(1),
    )(cond, w_ada, b_ada)


def _adamw_math(w, g, m, v):
    m = ADAM_B1 * m + (1.0 - ADAM_B1) * g
    v = ADAM_B2 * v + (1.0 - ADAM_B2) * jnp.square(g)
    m_hat = m / (1.0 - ADAM_B1 ** ADAM_STEP)
    v_hat = v / (1.0 - ADAM_B2 ** ADAM_STEP)
    return -ADAM_LR * (m_hat / (jnp.sqrt(v_hat) + ADAM_EPS) + ADAM_WD * w), m, v


def _ada_bwd(cond, dm, w_ada, m_ada, v_ada):
    d, n = w_ada.shape
    tn = 256
    rows = cond.shape[0]

    def body(c_ref, dm_ref, w_ref, m_ref, v_ref, g_ref, dl_ref, nm_ref, nv_ref, pc_ref):
        @pl.when(pl.program_id(0) == 0)
        def _():
            pc_ref[...] = jnp.zeros_like(pc_ref)

        dmb = dm_ref[...].astype(BF16)
        w = w_ref[...]
        g = _dot_tn(_silu(c_ref[...]).astype(BF16), dmb)
        g_ref[...] = g
        dl_ref[...], nm_ref[...], nv_ref[...] = _adamw_math(w, g, m_ref[...], v_ref[...])
        pc_ref[...] += _dot_nt(dm_ref[8:16, :].astype(BF16), w.astype(BF16))

    tile = pl.BlockSpec((d, tn), lambda j: (0, j))
    like = jax.ShapeDtypeStruct((d, n), F32)
    return pl.pallas_call(
        body, name="ada_bwd", grid=(n // tn,),
        in_specs=[pl.BlockSpec((rows, d), lambda j: (0, 0)), pl.BlockSpec((rows, tn), lambda j: (0, j)), tile, tile, tile],
        out_specs=(tile, tile, tile, tile, pl.BlockSpec((8, d), lambda j: (0, 0))),
        out_shape=(like, like, like, like, jax.ShapeDtypeStruct((8, d), F32)), compiler_params=_cp(1),
    )(cond, dm, w_ada, m_ada, v_ada)


def _adamw(name, w, g, m, v, after=None):
    return _ew(name, lambda w_, g_, m_, v_: (g_,) + _adamw_math(w_, g_, m_, v_), [w, g, m, v], [F32, F32, F32, F32],
               g if after is None else after)


def _colsum(st):
    return jnp.sum(st, axis=1)


def kernel(x, c, ctx, c_ctx, norm_attn_w, norm_mlp_w, w_ada, b_ada, w_in, attn_sink, pool_w, pool_scale, w_out, w_mlp_up, w_mlp_down, final_norm_w, loss_target, m_c_ctx, m_norm_attn_w, m_norm_mlp_w, m_w_ada, m_b_ada, m_w_in, m_attn_sink, m_pool_w, m_pool_scale, m_w_out, m_w_mlp_up, m_w_mlp_down, m_final_norm_w, v_c_ctx, v_norm_attn_w, v_norm_mlp_w, v_w_ada, v_b_ada, v_w_in, v_attn_sink, v_pool_w, v_pool_scale, v_w_out, v_w_mlp_up, v_w_mlp_down, v_final_norm_w):
    length, d = x.shape[1], x.shape[2]
    n_ctx = ctx.shape[1]
    pos = (lax.axis_index("x"), lax.axis_index("y"), lax.axis_index("c"))
    me, chip = _dev_index(pos), _chip_index(pos)
    xs, tgt, cx = x.reshape(length, d), loss_target.reshape(length, d), ctx.reshape(n_ctx, d)
    n_ada = w_ada.shape[2]

    c_all = _allgather8("gather_c", jnp.pad(c, ((0, 7), (0, 0))))
    mixer_bigs = [_Big("col", w_in.shape[1:]), _Big("pool", pool_w.shape[1:]), _Big("row", w_out.shape[1:])]
    mlp_bigs = [_Big("col", w_mlp_up.shape[1:]), _Big("row", w_mlp_down.shape[1:])]
    placed = [_cast_place(f"place_{i}", b, s, c_all)[0] for i, (b, s) in enumerate(zip(mixer_bigs, [w_in[0], pool_w[0], w_out[0]]))]
    flight = _split("gather_mixer_ici", placed, _gather_ici_remote(mixer_bigs, 0))
    token, placed_mlp = flight[3], []
    for i, (b, s) in enumerate(zip(mlp_bigs, [w_mlp_up[0], w_mlp_down[0]])):
        p, token = _cast_place(f"place_mlp_{i}", b, s, token)
        placed_mlp.append(p)
    cos, sin = _rope_tables(length, True, token[0, 0])
    cond = jnp.concatenate([c_all[:, 0, :], jnp.pad(c_ctx[None, :], ((0, 7), (0, 0)))], axis=0) + 0.0 * cos[0, 0]
    b_shard = lax.dynamic_slice_in_dim(b_ada, chip * n_ada, n_ada, axis=1)
    mod_all = _allgather8("gather_mod", _ada_fwd(cond, w_ada[0], b_shard))
    mod = jnp.concatenate([mod_all[0], mod_all[2], mod_all[4], mod_all[6]], axis=1)
    mine = lax.dynamic_slice_in_dim(mod, me, 1, axis=0)
    sh_a, sc_a, g_a, sh_m, sc_m, g_m = [mine[:, d * i:d * (i + 1)] for i in range(6)]
    csh_a, csc_a = mod[8:9, :d], mod[8:9, d:2 * d]

    win_b, pw_b, wout_b = _exchange("gather_mixer_d2d", _join(flight, mod), [jax.ShapeDtypeStruct(b.full_shape, BF16) for b in mixer_bigs],
                                    _gather_d2d_remote(mixer_bigs, 3), aliases={0: 0, 1: 1, 2: 2})
    wout_b = wout_b.reshape(-1, d)
    flight = _split("gather_mlp_ici", placed_mlp, _gather_ici_remote(mlp_bigs, 0), after=pw_b)

    one, zero = _rope_tables(n_ctx, False)
    h, q, k, v, u = _mixer_in("mixer_in", xs, norm_attn_w, sh_a, sc_a, win_b, cos, sin, flight[3])
    hc, _, kc, vc, _ = _mixer_in("mixer_in_ctx", cx, norm_attn_w, csh_a, csc_a, win_b, one, zero, flight[3])
    attn, probs = _attn_fwd(q, k, v, kc, vc, attn_sink)
    mix = _pool_fwd(u, pw_b, pool_scale, attn)
    flight = _split("gather_mlp_d2d", _join(flight, mix), _gather_d2d_remote(mlp_bigs, 0))
    x1, mo, hm = _mixer_out(mix, wout_b, xs, g_a, norm_mlp_w, sh_m, sc_m, flight[3])
    wup_b, wdn_b = _join(flight, hm)
    wdn_b = wdn_b.reshape(-1, d)
    act = _mlp_up(hm, wup_b)
    dx2, ddn, st_loss = _mlp_down_loss(act, wdn_b, x1, tgt, g_m, final_norm_w[None, :], c)
    st_loss = _colsum(st_loss)

    tt = _pick(length, 2048, 1024, 512, 256, 128)
    g_wdn = _mm_tn("grad_w_down", act, ddn, BF16, tmo=1024, tn=d, tt=tt)
    dup = _mlp_dact(ddn, wdn_b, act)
    g_wup = _mm_tn("grad_w_up", hm, dup, BF16, tmo=d, tn=1024, tt=tt)
    empty = lambda shapes: [lax.empty(s.shape, s.dtype) for s in shapes]
    grads = [g_wup, g_wdn.reshape(mlp_bigs[1].full_shape)]
    flight = _split("reduce_mlp_d2d", grads + empty(_halves(mlp_bigs)), _reduce_d2d_remote(mlp_bigs))
    dx1, dmi, st_mlp = _mlp_dx(dup, wup_b, x1, dx2, mo, norm_mlp_w, sc_m, g_a, flight[3])
    st_mlp = _colsum(st_mlp)
    landed = _join(flight, dmi)
    mlp_chip = _chip_sums("mlp", mlp_bigs, landed[:2], landed[2:])
    flight = _split("reduce_mlp_ici", mlp_chip + empty(_thirds(mlp_bigs)), _reduce_ici_remote(mlp_bigs))
    g_wout = _mm_tn("grad_w_out", mix, dmi, BF16, tmo=1024, tn=d, tt=tt)
    dmix = _mixer_dmix(dmi, wout_b, flight[3])
    dq, dkv, dkc, dvc, dsink = _attn_bwd(q, k, v, kc, vc, dmix, probs, cos, sin)
    landed = _join(flight, dq)
    flight = _split("reduce_mlp_share", _piece_sums("mlp", mlp_bigs, landed[:2], landed[2:]), _share_remote(mlp_bigs, 0))
    du, g_pw, st_pool = _pool_bwd(u, dmix, pw_b, pool_scale, flight[3])
    g_mlp = _join(flight, du)

    wo_bigs, win_bigs = mixer_bigs[1:], mixer_bigs[:1]
    wo_chip = _reduce_to_chip("wo", wo_bigs, [g_pw.astype(BF16), g_wout.reshape(wo_bigs[1].full_shape)])
    flight = _split("reduce_wo_ici", wo_chip + empty(_thirds(wo_bigs)), _reduce_ici_remote(wo_bigs))
    dkv_ctx = jnp.concatenate([dkc.astype(BF16), dvc.astype(BF16)], axis=1)
    at_kv, at_u = ATTN_WIDTH, ATTN_WIDTH + 2 * KV_WIDTH
    tt_in = _pick(length, 1024, 512, 256, 128)
    g_win = jnp.concatenate([_mm_tn("grad_w_in_q", h, dq, BF16, tmo=d, tn=ATTN_WIDTH, tt=tt_in, after=flight[3]),
                             _mm_tn("grad_w_in_kv", h, dkv, BF16, tmo=d, tn=2 * KV_WIDTH, tt=tt_in, more=(hc, dkv_ctx)),
                             _mm_tn("grad_w_in_u", h, du, BF16, tmo=d, tn=POOL_WIDTH, tt=tt_in)], axis=1)
    wo_landed = _join(flight, g_win)
    win_chip = _reduce_to_chip("win", win_bigs, [g_win])
    flight = _split("reduce_win_ici", win_chip + empty(_thirds(win_bigs)), _reduce_ici_remote(win_bigs))
    grad_x, st_mix = _mixer_dx("mixer_dx", [(dq, 0), (dkv, at_kv), (du, at_u)], win_b, xs, dx1, norm_attn_w, sc_a, flight[3])
    _, st_ctx = _mixer_dx("mixer_dx_ctx", [(dkv_ctx, at_kv)], win_b, cx, jnp.zeros((n_ctx, d), F32), norm_attn_w, csc_a, flight[3])
    st_mix, st_ctx = _colsum(st_mix), _colsum(st_ctx)
    win_landed = _join(flight, grad_x)
    g_mixer = (_reduce_finish("win", win_bigs, win_landed[:1], win_landed[1:])
               + _reduce_finish("wo", wo_bigs, wo_landed[:2], wo_landed[2:]))

    zrow = jnp.zeros((d,), F32)
    pad = lambda a: jnp.pad(a, (0, d - a.shape[0]))
    mine_rows = [st_mix[0], st_mix[1], st_mlp[3], st_mlp[0], st_mlp[1], st_loss[2],
                 st_ctx[0], st_ctx[1],
                 st_mix[2] + st_ctx[2], st_mlp[2], st_loss[1],
                 pad(jnp.sum(st_pool, axis=0)), pad(dsink[0, :N_Q_HEADS]), st_loss[0]] + [zrow] * 2
    flight = _allgather8_split("gather_small", jnp.concatenate(mine_rows).reshape(len(mine_rows), d), me)
    res = {"w_mlp_up": tuple(_adamw("adamw_w_mlp_up", w_mlp_up, g_mlp[0].reshape(w_mlp_up.shape), m_w_mlp_up, v_w_mlp_up, flight[3]))}
    small_all = _join(flight, res["w_mlp_up"][1])[1]
    small = small_all[0]
    for i in range(1, 8):
        small = small + small_all[i]
    loss = 0.5 / d * jnp.sum(small[13])
    dm_rows = small_all[:, 0:6, :].reshape(8, 6 * d)
    dm_ctx = jnp.concatenate([small[6], small[7], jnp.zeros((4 * d,), F32)])[None, :]
    dm = jnp.concatenate([dm_rows, jnp.pad(dm_ctx, ((0, 7), (0, 0)))], axis=0)
    g_bada = jnp.sum(dm[:9], axis=0, keepdims=True)
    dm_shard = lax.dynamic_slice_in_dim(dm, chip * n_ada, n_ada, axis=1)
    g_wada, dl_wada, nm_wada, nv_wada, part_cctx = _ada_bwd(cond, dm_shard, w_ada[0], m_w_ada[0], v_w_ada[0])
    flight = _allgather8_split("gather_cctx", part_cctx, me)
    res["w_mlp_down"] = tuple(_adamw("adamw_w_mlp_down", w_mlp_down, g_mlp[1].reshape(w_mlp_down.shape), m_w_mlp_down,
                                     v_w_mlp_down, flight[3]))
    cctx_all = _join(flight, res["w_mlp_down"][1])[1]
    dsilu_in = cctx_all[0, 0] + cctx_all[2, 0] + cctx_all[4, 0] + cctx_all[6, 0]
    sig = 1.0 / (1.0 + jnp.exp(-c_ctx))
    g_cctx = dsilu_in * (sig * (1.0 + c_ctx * (1.0 - sig)))

    for nm, w_, g_, m_, v_ in zip(["w_in", "pool_w", "w_out"], [w_in, pool_w, w_out], g_mixer,
                                  [m_w_in, m_pool_w, m_w_out], [v_w_in, v_pool_w, v_w_out]):
        res[nm] = tuple(_adamw("adamw_" + nm, w_, g_.reshape(w_.shape), m_, v_))
    res["w_ada"] = (g_wada[None], dl_wada[None], nm_wada[None], nv_wada[None])

    def pack(cc, na, nm_, ba, sk, ps, fn):
        flat = [cc.reshape(-1), na.reshape(-1), nm_.reshape(-1), ba.reshape(-1), pad(sk.reshape(-1)), pad(ps.reshape(-1)),
                fn.reshape(-1), jnp.zeros((4 * d,), F32)]
        return jnp.concatenate(flat).reshape(16, d)

    w_s = pack(c_ctx, norm_attn_w, norm_mlp_w, b_ada, attn_sink, pool_scale, final_norm_w)
    m_s = pack(m_c_ctx, m_norm_attn_w, m_norm_mlp_w, m_b_ada, m_attn_sink, m_pool_scale, m_final_norm_w)
    v_s = pack(v_c_ctx, v_norm_attn_w, v_norm_mlp_w, v_b_ada, v_attn_sink, v_pool_scale, v_final_norm_w)
    g_s = pack(g_cctx, small[8], small[9], g_bada, small[12][:N_Q_HEADS], small[11][:POOL_WIDTH], small[10])
    small_out = _adamw("adamw_small", w_s, g_s, m_s, v_s)

    def unpack(p):
        return {"c_ctx": p[0], "norm_attn_w": p[1:2], "norm_mlp_w": p[2:3], "b_ada": p[3:9].reshape(1, 6 * d),
                "attn_sink": p[9:10, :N_Q_HEADS], "pool_scale": p[10:11, :POOL_WIDTH], "final_norm_w": p[11]}

    small_res = [unpack(p) for p in small_out]
    order = ["c_ctx", "norm_attn_w", "norm_mlp_w", "w_ada", "b_ada", "w_in", "attn_sink", "pool_w", "pool_scale",
             "w_out", "w_mlp_up", "w_mlp_down", "final_norm_w"]
    outs = [loss, grad_x.reshape(x.shape)]
    for kind in range(4):
        for nm in order:
            outs.append(res[nm][kind] if nm in res else small_res[kind][nm])
    return tuple(outs)
```

```python
import functools

import jax
import jax.numpy as jnp
from jax import lax
from jax.experimental import pallas as pl
from jax.experimental.pallas import tpu as pltpu

F32 = jnp.float32
BF16 = jnp.bfloat16
EPS = 1e-6
NEG_INF = -1e30
HEAD_DIM = 64
N_Q_HEADS = 16
N_KV_HEADS = 4
GROUP = N_Q_HEADS // N_KV_HEADS
ATTN_WIDTH = N_Q_HEADS * HEAD_DIM
KV_WIDTH = N_KV_HEADS * HEAD_DIM
POOL_WINDOWS = (2, 4, 8, 16)
POOL_GROUP_DIM = 256
POOL_WIDTH = len(POOL_WINDOWS) * POOL_GROUP_DIM
BLOCK = 128
GRID_W = 64
ROPE_BASE = 10000.0
SCALE = HEAD_DIM ** -0.5
HALO = 16
STRIP = 16
ADAM_LR, ADAM_B1, ADAM_B2, ADAM_EPS, ADAM_WD, ADAM_STEP = 0.001, 0.9, 0.999, 1e-08, 0.01, 10
MESH = pl.DeviceIdType.MESH
MIB = 1024 * 1024
ANY = pl.BlockSpec(memory_space=pl.ANY)


def _cp(n_axes, vmem_mib=48):
    return pltpu.CompilerParams(dimension_semantics=("arbitrary",) * n_axes, vmem_limit_bytes=vmem_mib * MIB)


def _fold8(v):
    s = v[0:8]
    for t in range(1, v.shape[0] // 8):
        s = s + v[8 * t:8 * t + 8]
    return s


def _dot(a, b):
    return jnp.dot(a, b, preferred_element_type=F32)


def _dot_nt(a, b):
    return lax.dot_general(a, b, (((1,), (1,)), ((), ())), preferred_element_type=F32)


def _dot_tn(a, b):
    return lax.dot_general(a, b, (((0,), (0,)), ((), ())), preferred_element_type=F32)


def _pick(n, *cands):
    for t in cands:
        if n % t == 0:
            return t
    return n


def _flip(pos, mask):
    return tuple((1 - v) if (mask >> (2 - i)) & 1 else v for i, v in enumerate(pos))


def _exchange(name, ins, out_shapes, remote, local=(), aliases=None):
    n_io = len(ins) + len(out_shapes)

    def body(*refs):
        io = refs[:n_io]
        send_sems, recv_sems, local_sems = refs[n_io:]
        me = (lax.axis_index("x"), lax.axis_index("y"), lax.axis_index("c"))

        def copy(i, sender):
            mask, src_fn, dst_fn = remote[i]
            return pltpu.make_async_remote_copy(
                src_ref=src_fn(io, sender), dst_ref=dst_fn(io, sender), send_sem=send_sems.at[i],
                recv_sem=recv_sems.at[i], device_id=_flip(sender, mask), device_id_type=MESH)

        own = [pltpu.make_async_copy(s(io, me), d(io, me), local_sems.at[i]) for i, (s, d) in enumerate(local)]
        for cp in own:
            cp.start()
        sends = [copy(i, me) for i in range(len(remote))]
        for cp in sends:
            cp.start()
        for i in range(len(remote)):
            copy(i, _flip(me, remote[i][0])).wait_recv()
        for cp in sends:
            cp.wait_send()
        for cp in own:
            cp.wait()

    return pl.pallas_call(
        body, name=name, out_shape=tuple(out_shapes),
        in_specs=[ANY] * len(ins), out_specs=tuple([ANY] * len(out_shapes)),
        scratch_shapes=[pltpu.SemaphoreType.DMA((len(remote),)), pltpu.SemaphoreType.DMA((len(remote),)),
                        pltpu.SemaphoreType.DMA((max(len(local), 1),))],
        input_output_aliases=aliases or {},
    )(*ins)


HBM = pl.BlockSpec(memory_space=pltpu.HBM)
SEM = pl.BlockSpec(memory_space=pltpu.SEMAPHORE)
EFFECT = pltpu.SideEffectType.DATAFLOW_SIDE_EFFECTING


def _split_copy(remote, i, io, send_sems, recv_sems, sender):
    mask, src_fn, dst_fn = remote[i]
    return pltpu.make_async_remote_copy(
        src_ref=src_fn(io, sender), dst_ref=dst_fn(io, sender), send_sem=send_sems.at[i],
        recv_sem=recv_sems.at[i], device_id=_flip(sender, mask), device_id_type=MESH)


def _exchange_start(name, bufs, remote, after=None):
    n, r = len(bufs), len(remote)
    more = [] if after is None else [after]

    def body(*refs):
        io, (send_sems, recv_sems, token) = refs[:n], refs[-3:]
        me = (lax.axis_index("x"), lax.axis_index("y"), lax.axis_index("c"))
        for i in range(r):
            _split_copy(remote, i, io, send_sems, recv_sems, me).start()
        token[...] = jnp.zeros_like(token)

    res = pl.pallas_call(
        body, name=name,
        out_shape=tuple(pltpu.HBM(b.shape, b.dtype) for b in bufs)
        + (pltpu.SemaphoreType.DMA((r,)), pltpu.SemaphoreType.DMA((r,)), jax.ShapeDtypeStruct((8, 128), F32)),
        in_specs=[HBM] * n + [ANY] * len(more), out_specs=tuple([HBM] * n) + (SEM, SEM, pl.BlockSpec(memory_space=pltpu.VMEM)),
        input_output_aliases={i: i for i in range(n)}, compiler_params=pltpu.CompilerParams(has_side_effects=EFFECT),
    )(*[pltpu.with_memory_space_constraint(b, pltpu.HBM) for b in bufs], *more)
    return list(res[:n]), res[n], res[n + 1], res[n + 2]


def _exchange_wait(name, bufs, send_sems, recv_sems, remote, after):
    n, r = len(bufs), len(remote)

    def body(*refs):
        io, ss, rs = refs[:n], refs[n], refs[n + 1]
        me = (lax.axis_index("x"), lax.axis_index("y"), lax.axis_index("c"))
        for i in range(r):
            _split_copy(remote, i, io, ss, rs, _flip(me, remote[i][0])).wait_recv()
        for i in range(r):
            _split_copy(remote, i, io, ss, rs, me).wait_send()

    return list(pl.pallas_call(
        body, name=name, out_shape=tuple(pltpu.HBM(b.shape, b.dtype) for b in bufs),
        in_specs=[HBM] * n + [SEM, SEM, ANY], out_specs=tuple([HBM] * n),
        input_output_aliases={i: i for i in range(n)}, compiler_params=pltpu.CompilerParams(has_side_effects=EFFECT),
    )(*bufs, send_sems, recv_sems, after))


def _my_c():
    return lax.axis_index("c")


def _my_chip():
    return 2 * lax.axis_index("x") + lax.axis_index("y")


def _dev_index(pos):
    return 4 * pos[0] + 2 * pos[1] + pos[2]


def _chip_index(pos):
    return 2 * pos[0] + pos[1]


def _allgather8(name, v):
    out = jax.ShapeDtypeStruct((8,) + v.shape, v.dtype)
    remote = [(mask, lambda io, pos: io[0], lambda io, pos: io[1].at[_dev_index(pos)]) for mask in range(1, 8)]
    local = [(lambda io, pos: io[0], lambda io, pos: io[1].at[_dev_index(pos)])]
    return _exchange(name, [v], [out], remote, local)[0]


class _Big:
    def __init__(self, kind, shard_shape):
        self.kind = kind
        self.shard_shape = tuple(shard_shape)
        if kind == "col":
            r, cs = shard_shape
            self.full_shape = (r, 4 * cs)
            self.piece_shape = (r // 2, cs)
            self.half_shape = (r // 2, 4 * cs)
        elif kind == "row":
            rs, c = shard_shape
            self.full_shape = (4, 2, rs // 2, c)
            self.piece_shape = (1, 1, rs // 2, c)
            self.half_shape = (4, 1, rs // 2, c)
        else:
            self.full_shape = (4, 256, 256)
            self.piece_shape = (2, 64, 256)
            self.half_shape = (2, 256, 256)

    def shard_as_pieces(self, a):
        return a.reshape((1, 2) + self.piece_shape[2:]) if self.kind == "row" else a

    def piece(self, ref, k, h):
        if self.kind == "col":
            r, cs = self.piece_shape
            return ref.at[pl.ds(h * r, r), pl.ds(k * cs, cs)]
        if self.kind == "row":
            return ref.at[pl.ds(k, 1), pl.ds(h, 1)]
        return ref.at[pl.ds(2 * h, 2), pl.ds(64 * k, 64)]

    def half_of_shard(self, ref, h):
        if self.kind == "col":
            return ref.at[pl.ds(h * self.piece_shape[0], self.piece_shape[0])]
        if self.kind == "row":
            return ref.at[:, pl.ds(h, 1)]
        return ref.at[pl.ds(2 * h, 2)]

    def half_of_full(self, ref, h):
        if self.kind == "col":
            return ref.at[pl.ds(h * self.half_shape[0], self.half_shape[0])]
        if self.kind == "row":
            return ref.at[:, pl.ds(h, 1)]
        return ref.at[pl.ds(2 * h, 2)]

    def piece_of_half(self, ref, k):
        if self.kind == "col":
            return ref.at[:, pl.ds(k * self.piece_shape[1], self.piece_shape[1])]
        if self.kind == "row":
            return ref.at[pl.ds(k, 1)]
        return ref.at[:, pl.ds(64 * k, 64)]


CHIP_MASKS = (4, 2, 6)


def _cast_place(name, big, shard, after):
    if big.kind == "col":
        r, cs = big.shard_shape
        tr = _pick(r, 512, 256, 128)
        src, grid, blk = shard, (r // tr,), (tr, cs)
        imap, omap = (lambda i: (i, 0)), (lambda i: (i, _my_chip()))
    elif big.kind == "row":
        rs, c = big.shard_shape
        tr = _pick(rs // 2, 256, 128)
        src, grid, blk = big.shard_as_pieces(shard), (2, rs // 2 // tr), (1, 1, tr, c)
        imap, omap = (lambda h, i: (0, h, i, 0)), (lambda h, i: (_my_chip(), h, i, 0))
    else:
        src, grid, blk = shard, (1,), big.shard_shape
        imap, omap = (lambda i: (0, 0, 0)), (lambda i: (0, _my_chip(), 0))

    def body(s_ref, after_ref, o_ref, token_ref):
        o_ref[...] = s_ref[...].astype(BF16)
        token_ref[...] = jnp.zeros_like(token_ref)

    return pl.pallas_call(
        body, name=name, grid=grid, in_specs=[pl.BlockSpec(blk, imap), ANY],
        out_specs=(pl.BlockSpec(blk, omap), pl.BlockSpec((8, 128), lambda *_: (0, 0))),
        out_shape=(jax.ShapeDtypeStruct(big.full_shape, BF16), jax.ShapeDtypeStruct((8, 128), F32)), compiler_params=_cp(len(grid)),
    )(src, after)


def _gather_ici_remote(bigs, off):
    remote = []
    for a, b in enumerate(bigs):
        for mask in CHIP_MASKS:
            def mine(io, p, a=a, b=b):
                return b.piece(io[off + a], _chip_index(p), p[2])
            remote.append((mask, mine, mine))
    return remote


def _gather_d2d_remote(bigs, off):
    remote = []
    for a, b in enumerate(bigs):
        for mask in CHIP_MASKS:
            def region(io, p, a=a, b=b, mask=mask):
                return b.piece(io[off + a], _chip_index(_flip(p, mask)), p[2])
            remote.append((1, region, region))
    return remote


def _ew(name, fn, ins, out_dtypes, after, rows_per_step=256):
    shape = ins[0].shape
    last = shape[-1]
    rows = 1
    for s in shape[:-1]:
        rows *= s
    ins2 = [a.reshape(rows, last) for a in ins]
    tr = _pick(rows, rows_per_step, 128, 64, 32, 16, 8)
    spec = pl.BlockSpec((tr, last), lambda i: (i, 0))

    def body(*refs):
        outs = fn(*[r[...] for r in refs[:len(ins)]])
        for o_ref, o in zip(refs[len(ins) + 1:], outs):
            o_ref[...] = o.astype(o_ref.dtype)

    outs = pl.pallas_call(
        body, name=name, grid=(rows // tr,), in_specs=[spec] * len(ins) + [ANY], out_specs=tuple([spec] * len(out_dtypes)),
        out_shape=tuple(jax.ShapeDtypeStruct((rows, last), d) for d in out_dtypes), compiler_params=_cp(1),
    )(*ins2, after)
    return [o.reshape(shape) for o in outs]


def _chip_sum(name, big, grad, from_sibling):
    if big.kind == "col":
        rh, w = big.half_shape
        tr = _pick(rh, 256, 128)
        nb = rh // tr
        grid, blk = (nb,), (tr, w)
        gmap, hmap = (lambda i: (_my_c() * nb + i, 0)), (lambda i: (i, 0))
    elif big.kind == "row":
        rh, w = big.half_shape[2:]
        tr = _pick(rh, 256, 128)
        grid, blk = (4, rh // tr), (1, 1, tr, w)
        gmap, hmap = (lambda k, i: (k, _my_c(), i, 0)), (lambda k, i: (k, 0, i, 0))
    else:
        grid, blk = (1,), big.half_shape
        gmap, hmap = (lambda i: (_my_c(), 0, 0)), (lambda i: (0, 0, 0))

    def body(g_ref, s_ref, o_ref):
        o_ref[...] = (g_ref[...].astype(F32) + s_ref[...].astype(F32)).astype(BF16)

    return pl.pallas_call(
        body, name=name, grid=grid, in_specs=[pl.BlockSpec(blk, gmap), pl.BlockSpec(blk, hmap)],
        out_specs=pl.BlockSpec(blk, hmap), out_shape=jax.ShapeDtypeStruct(big.half_shape, BF16), compiler_params=_cp(len(grid)),
    )(grad, from_sibling)


def _piece_sum(name, big, chip_sum, thirds):
    if big.kind == "col":
        rp, cs = big.piece_shape
        tr = _pick(rp, 256, 128)
        nb = rp // tr
        grid, blk, tblk = (nb,), (tr, cs), (1, tr, cs)
        smap, omap = (lambda i: (i, _my_chip())), (lambda i: (_my_c() * nb + i, 0))
        tmap = lambda j: (lambda i: (j, i, 0))
        out_shape = big.shard_shape
    elif big.kind == "row":
        rp, w = big.piece_shape[2:]
        tr = _pick(rp, 256, 128)
        grid, blk, tblk = (rp // tr,), (1, 1, tr, w), (1, 1, 1, tr, w)
        smap, omap = (lambda i: (_my_chip(), 0, i, 0)), (lambda i: (0, _my_c(), i, 0))
        tmap = lambda j: (lambda i: (j, 0, 0, i, 0))
        out_shape = (1, 2, rp, w)
    else:
        grid, blk, tblk = (1,), big.piece_shape, (1,) + big.piece_shape
        smap, omap = (lambda i: (0, _my_chip(), 0)), (lambda i: (_my_c(), 0, 0))
        tmap = lambda j: (lambda i: (j, 0, 0, 0))
        out_shape = big.shard_shape

    def body(s_ref, t0, t1, t2, o_ref):
        o_ref[...] = s_ref[...].astype(F32) + t0[0].astype(F32) + t1[0].astype(F32) + t2[0].astype(F32)

    return pl.pallas_call(
        body, name=name, grid=grid,
        in_specs=[pl.BlockSpec(blk, smap)] + [pl.BlockSpec(tblk, tmap(j)) for j in range(3)],
        out_specs=pl.BlockSpec(blk, omap), out_shape=jax.ShapeDtypeStruct(out_shape, F32), compiler_params=_cp(len(grid)),
    )(chip_sum, thirds, thirds, thirds)


def _split(name, bufs, remote, after=None):
    return _exchange_start(name + "_start", bufs, remote, after) + (remote, name)


def _join(handle, after):
    bufs, send_sems, recv_sems, _, remote, name = handle
    return _exchange_wait(name + "_wait", bufs, send_sems, recv_sems, remote, after)


def _allgather8_split(name, v, me):
    own = lax.dynamic_update_slice(lax.empty((8,) + v.shape, v.dtype), v[None], (me, 0, 0))
    remote = [(mask, lambda io, pos: io[0], lambda io, pos: io[1].at[_dev_index(pos)]) for mask in range(1, 8)]
    return _split(name, [v, own], remote)


def _reduce_d2d_remote(bigs):
    n = len(bigs)
    return [(1, lambda io, p, a=a, b=b: b.half_of_full(io[a], 1 - p[2]), lambda io, p, a=a: io[n + a])
            for a, b in enumerate(bigs)]


def _halves(bigs):
    return [jax.ShapeDtypeStruct(b.half_shape, BF16) for b in bigs]


def _chip_sums(tag, bigs, grads, from_sibling):
    return [_chip_sum(f"reduce_{tag}_chip_sum_{a}", b, g, r) for a, (b, g, r) in enumerate(zip(bigs, grads, from_sibling))]


def _reduce_to_chip(tag, bigs, grads):
    from_sibling = _exchange(f"reduce_{tag}_d2d", grads, _halves(bigs), _reduce_d2d_remote(bigs))
    return _chip_sums(tag, bigs, grads, from_sibling)


def _reduce_ici_remote(bigs):
    n = len(bigs)
    remote = []
    for a, b in enumerate(bigs):
        for j, mask in enumerate(CHIP_MASKS):
            remote.append((mask,
                           lambda io, p, a=a, b=b, mask=mask: b.piece_of_half(io[a], _chip_index(_flip(p, mask))),
                           lambda io, p, a=a, j=j: io[n + a].at[j]))
    return remote


def _thirds(bigs):
    return [jax.ShapeDtypeStruct((3,) + b.piece_shape, BF16) for b in bigs]


def _piece_sums(tag, bigs, chip_sum, from_chips):
    return [_piece_sum(f"reduce_{tag}_sum_{a}", b, s, r) for a, (b, s, r) in enumerate(zip(bigs, chip_sum, from_chips))]


def _share_remote(bigs, off):
    remote = []
    for a, b in enumerate(bigs):
        def mine(io, p, a=a, b=b):
            return b.half_of_shard(io[off + a], p[2])
        remote.append((1, mine, mine))
    return remote


def _reduce_finish(tag, bigs, chip_sum, from_chips):
    n = len(bigs)
    placed = _piece_sums(tag, bigs, chip_sum, from_chips)
    out = _exchange(f"reduce_{tag}_share_d2d", placed, [jax.ShapeDtypeStruct(p.shape, F32) for p in placed],
                    _share_remote(bigs, n), aliases={a: a for a in range(n)})
    return [o.reshape(b.shard_shape) for o, b in zip(out, bigs)]


def _mm(name, a, b, *, nt, tm, tn, tk, epi, extras=(), extra_specs=(), out_shape, out_specs, after=None, vmem_mib=48):
    m, kdim = a.shape
    n = b.shape[0] if nt else b.shape[1]
    gm, gn, gk = m // tm, n // tn, kdim // tk
    a_spec = pl.BlockSpec((tm, tk), lambda j, i, k: (i, k))
    b_spec = pl.BlockSpec((tn, tk), lambda j, i, k: (j, k)) if nt else pl.BlockSpec((tk, tn), lambda j, i, k: (k, j))
    n_ex = len(extras)
    if after is not None:
        extras, extra_specs = tuple(extras) + (after,), list(extra_specs) + [ANY]

    def body(a_ref, b_ref, *rest):
        ex, outs, acc = rest[:n_ex], rest[len(extras):-1], rest[-1]
        dot = _dot_nt if nt else _dot
        if gk == 1:
            acc[...] = dot(a_ref[...], b_ref[...])
            epi(acc, ex, outs)
        else:
            k = pl.program_id(2)

            @pl.when(k == 0)
            def _():
                acc[...] = dot(a_ref[...], b_ref[...])

            @pl.when(k > 0)
            def _():
                acc[...] += dot(a_ref[...], b_ref[...])

            @pl.when(k == gk - 1)
            def _():
                epi(acc, ex, outs)

    return pl.pallas_call(
        body, name=name, grid=(gn, gm, gk), in_specs=[a_spec, b_spec, *extra_specs], out_specs=tuple(out_specs),
        out_shape=tuple(out_shape), scratch_shapes=[pltpu.VMEM((tm, tn), F32)], compiler_params=_cp(3, vmem_mib),
    )(a, b, *extras)


def _mm_deferred(name, a, b, *, nt, tm, epi, tiles, vecs, out_tiles, n_stats, after, vmem_mib=48):
    pieces = a if isinstance(a, (list, tuple)) else [(a, 0)]
    m = pieces[0][0].shape[0]
    n = b.shape[0] if nt else b.shape[1]
    gm = m // tm
    n_a, n_t, n_v, n_o = len(pieces), len(tiles), len(vecs), len(out_tiles)

    def body(*refs):
        a_refs, b_ref, rest = refs[:n_a], refs[n_a], refs[n_a + 1:]
        t_refs, v_refs = rest[:n_t], rest[n_t:n_t + n_v]
        o_refs, st_ref, acc0, acc1 = rest[n_t + n_v + 1:n_t + n_v + 1 + n_o], rest[-3], rest[-2], rest[-1]
        i = pl.program_id(0)

        def dot():
            if n_a == 1 and pieces[0][0].shape[1] == b.shape[1 if nt else 0]:
                return (_dot_nt if nt else _dot)(a_refs[0][...], b_ref[...])
            parts = [_dot_nt(r[...], b_ref[:, off:off + p.shape[1]]) for r, (p, off) in zip(a_refs, pieces)]
            return functools.reduce(lambda u, v: u + v, parts)

        @pl.when(i == 0)
        def _():
            acc1[...] = jnp.zeros_like(acc1)
            st_ref[...] = jnp.zeros_like(st_ref)

        def finish(prev):
            for r0 in range(0, tm, STRIP):
                rs = slice(r0, r0 + STRIP)
                epi(prev[rs, :], rs, t_refs, v_refs, o_refs, st_ref, i > 0)

        @pl.when((i % 2 == 0) & (i < gm))
        def _():
            acc0[...] = dot()
            finish(acc1)

        @pl.when((i % 2 == 1) & (i < gm))
        def _():
            acc1[...] = dot()
            finish(acc0)

        @pl.when(i == gm)
        def _():
            finish(acc1 if gm % 2 == 0 else acc0)

    prev = lambda i: (jnp.maximum(i - 1, 0), 0)
    tile = pl.BlockSpec((tm, n), prev)
    return pl.pallas_call(
        body, name=name, grid=(gm + 1,),
        in_specs=[pl.BlockSpec((tm, p.shape[1]), lambda i: (jnp.minimum(i, gm - 1), 0)) for p, _ in pieces]
        + [pl.BlockSpec(b.shape, lambda i: (0, 0))] + [tile] * n_t + [_row_spec(n)] * n_v + [ANY],
        out_specs=tuple([tile] * n_o) + (_stat_spec(n_stats, n),),
        out_shape=tuple(out_tiles) + (jax.ShapeDtypeStruct((n_stats, 8, n), F32),),
        scratch_shapes=[pltpu.VMEM((tm, n), F32), pltpu.VMEM((tm, n), F32)], compiler_params=_cp(1, vmem_mib),
    )(*[p for p, _ in pieces], b, *tiles, *vecs, after)


def _mm_k_deferred(name, a, b, *, nt, tm, tk, epi, tiles, vecs, out_tiles, n_stats, after, vmem_mib=56):
    m, kdim = a.shape
    n = b.shape[0] if nt else b.shape[1]
    gm, gk = m // tm, kdim // tk
    rows = tm // gk
    n_t, n_v, n_o = len(tiles), len(vecs), len(out_tiles)
    dot = _dot_nt if nt else _dot

    def body(a_ref, b_ref, *rest):
        t_refs, v_refs = rest[:n_t], rest[n_t:n_t + n_v]
        o_refs, st_ref, acc0, acc1 = rest[n_t + n_v + 1:n_t + n_v + 1 + n_o], rest[-3], rest[-2], rest[-1]
        i, k = pl.program_id(0), pl.program_id(1)

        @pl.when((i == 0) & (k == 0))
        def _():
            acc1[...] = jnp.zeros_like(acc1)
            st_ref[...] = jnp.zeros_like(st_ref)

        def finish(prev):
            for r0 in range(0, rows, STRIP):
                acc_rows = prev[pl.ds(pl.multiple_of(k * rows + r0, STRIP), STRIP), :]
                epi(acc_rows, slice(r0, r0 + STRIP), t_refs, v_refs, o_refs, st_ref, i > 0)

        def step(cur, prev):
            cur[...] = jnp.where(k > 0, cur[...], 0.0) + dot(a_ref[...], b_ref[...])
            finish(prev)

        @pl.when((i % 2 == 0) & (i < gm))
        def _():
            step(acc0, acc1)

        @pl.when((i % 2 == 1) & (i < gm))
        def _():
            step(acc1, acc0)

        @pl.when(i == gm)
        def _():
            finish(acc1 if gm % 2 == 0 else acc0)

    prev = lambda i, k: (jnp.where(i == 0, 0, (i - 1) * gk + k), 0)
    part = pl.BlockSpec((rows, n), prev)
    b_spec = pl.BlockSpec((n, tk), lambda i, k: (0, k)) if nt else pl.BlockSpec((tk, n), lambda i, k: (k, 0))
    return pl.pallas_call(
        body, name=name, grid=(gm + 1, gk),
        in_specs=[pl.BlockSpec((tm, tk), lambda i, k: (jnp.minimum(i, gm - 1), k)), b_spec]
        + [part] * n_t + [_row_spec(n)] * n_v + [ANY],
        out_specs=tuple([part] * n_o) + (_stat_spec(n_stats, n),),
        out_shape=tuple(out_tiles) + (jax.ShapeDtypeStruct((n_stats, 8, n), F32),),
        scratch_shapes=[pltpu.VMEM((tm, n), F32), pltpu.VMEM((tm, n), F32)], compiler_params=_cp(2, vmem_mib),
    )(a, b, *tiles, *vecs, after)


def _mm_tn(name, a, b, out_dtype, *, tmo, tn, tt, more=(), after=None, vmem_mib=56):
    t, m = a.shape
    n = b.shape[1]
    gt = t // tt
    wait_for = [] if after is None else [after]

    def body(a_ref, b_ref, *rest):
        o_ref, acc = rest[-2:]
        k = pl.program_id(2)

        @pl.when(k == 0)
        def _():
            first = _dot_tn(a_ref[...], b_ref[...])
            acc[...] = first + _dot_tn(rest[0][...], rest[1][...]) if more else first

        @pl.when(k > 0)
        def _():
            acc[...] += _dot_tn(a_ref[...], b_ref[...])

        @pl.when(k == gt - 1)
        def _():
            o_ref[...] = acc[...].astype(o_ref.dtype)

    more_specs = [pl.BlockSpec((more[0].shape[0], tmo), lambda i, j, k: (0, i)),
                  pl.BlockSpec((more[1].shape[0], tn), lambda i, j, k: (0, j))] if more else []
    return pl.pallas_call(
        body, name=name, grid=(m // tmo, n // tn, gt),
        in_specs=[pl.BlockSpec((tt, tmo), lambda i, j, k: (k, i)), pl.BlockSpec((tt, tn), lambda i, j, k: (k, j))] + more_specs
        + [ANY] * len(wait_for),
        out_specs=pl.BlockSpec((tmo, tn), lambda i, j, k: (i, j)), out_shape=jax.ShapeDtypeStruct((m, n), out_dtype),
        scratch_shapes=[pltpu.VMEM((tmo, tn), F32)], compiler_params=_cp(3, vmem_mib),
    )(a, b, *more, *wait_for)


def _row_spec(d):
    return pl.BlockSpec((1, d), lambda *_: (0, 0))


def _stat_spec(k, d):
    return pl.BlockSpec((k, 8, d), lambda *_: (0, 0, 0))


def _rope(z, cs, sn):
    first = (lax.broadcasted_iota(jnp.int32, (z.shape[0], 128), 1) % 32) < 16
    outs = []
    for j in range(z.shape[1] // 128):
        zc = z[:, 128 * j:128 * (j + 1)]
        partner = jnp.where(first, pltpu.roll(zc, 112, 1), pltpu.roll(zc, 16, 1))
        outs.append(zc * cs + partner * sn)
    return outs[0] if len(outs) == 1 else jnp.concatenate(outs, axis=1)


def _rope_tables(length, rotate, zero=0.0):
    if not rotate:
        return jnp.ones((length, 128), F32), jnp.zeros((length, 128), F32)
    half = HEAD_DIM // 2
    t = jnp.arange(length)
    row = (t // GRID_W).astype(F32) + zero
    col = (t % GRID_W).astype(F32)
    e = jnp.arange(128) % HEAD_DIM
    inv_freq = ROPE_BASE ** (-(2 * ((e % half) % (half // 2))).astype(F32) / half)
    pos = jnp.where(e[None, :] < half, row[:, None], col[:, None])
    ang = pos * inv_freq[None, :]
    first = ((e % half) < half // 2)[None, :]
    return jnp.cos(ang), jnp.where(first, -jnp.sin(ang), jnp.sin(ang))


def _mixer_in(name, x, nw, sh, sc, w_in, cos, sin, after):
    t, d = x.shape
    tm = _pick(t, 256, 128)
    n_in = w_in.shape[1]

    def body(x_ref, nw_ref, sh_ref, sc_ref, w_ref, cos_ref, sin_ref, after_ref, h_ref, q_ref, k_ref, v_ref, u_ref):
        xf = x_ref[...]
        r = lax.rsqrt(jnp.mean(xf * xf, axis=-1, keepdims=True) + EPS)
        hb = (((xf * r) * nw_ref[...]) * (1.0 + sc_ref[...]) + sh_ref[...]).astype(BF16)
        h_ref[...] = hb
        p = _dot(hb, w_ref[...])
        cs, sn = cos_ref[...], sin_ref[...]
        q_ref[...] = (_rope(p[:, :ATTN_WIDTH], cs, sn) * SCALE).astype(BF16)
        k_ref[...] = _rope(p[:, ATTN_WIDTH:ATTN_WIDTH + KV_WIDTH], cs, sn).astype(BF16)
        v_ref[...] = p[:, ATTN_WIDTH + KV_WIDTH:ATTN_WIDTH + 2 * KV_WIDTH].astype(BF16)
        u_ref[...] = p[:, ATTN_WIDTH + 2 * KV_WIDTH:]

    def tile(w):
        return pl.BlockSpec((tm, w), lambda i: (i, 0))

    return pl.pallas_call(
        body, name=name, grid=(t // tm,),
        in_specs=[tile(d), _row_spec(d), _row_spec(d), _row_spec(d), pl.BlockSpec((d, n_in), lambda i: (0, 0)),
                  tile(128), tile(128), ANY],
        out_specs=(tile(d), tile(ATTN_WIDTH), tile(KV_WIDTH), tile(KV_WIDTH), tile(POOL_WIDTH)),
        out_shape=(jax.ShapeDtypeStruct((t, d), BF16), jax.ShapeDtypeStruct((t, ATTN_WIDTH), BF16),
                   jax.ShapeDtypeStruct((t, KV_WIDTH), BF16), jax.ShapeDtypeStruct((t, KV_WIDTH), BF16),
                   jax.ShapeDtypeStruct((t, POOL_WIDTH), F32)),
        compiler_params=_cp(1),
    )(x, nw, sh, sc, w_in, cos, sin, after)


def _attn_specs(nb, n_ctx):
    def blk(w, f):
        return pl.BlockSpec((BLOCK, w), lambda n: (f(n), 0))

    prev = lambda n: jnp.maximum(jnp.minimum(n, nb - 1) - 1, 0)
    cur = lambda n: jnp.minimum(n, nb - 1)
    nxt = lambda n: jnp.minimum(n + 1, nb - 1)
    kv = [blk(KV_WIDTH, prev), blk(KV_WIDTH, cur), blk(KV_WIDTH, nxt)]
    ctx = pl.BlockSpec((n_ctx, KV_WIDTH), lambda n: (0, 0))
    return [pl.BlockSpec(memory_space=pltpu.SMEM), blk(ATTN_WIDTH, cur)] + kv + kv + [ctx, ctx]


def _attn_mask(n, length, n_keys):
    row = lax.broadcasted_iota(jnp.int32, (GROUP * BLOCK, n_keys), 0) % BLOCK
    col = lax.broadcasted_iota(jnp.int32, (GROUP * BLOCK, n_keys), 1)
    kpos = (n - 1) * BLOCK + col
    return ((jnp.abs(col - BLOCK - row) <= BLOCK) & (kpos >= 0) & (kpos < length)) | (col >= 3 * BLOCK)


def _group_rows(block, g):
    return jnp.concatenate([block[:, HEAD_DIM * h:HEAD_DIM * (h + 1)] for h in range(GROUP * g, GROUP * (g + 1))], axis=0)


def _group_sink(sink_ref, g):
    head = lax.broadcasted_iota(jnp.int32, (GROUP * BLOCK, 1), 0) // BLOCK
    out = jnp.full((GROUP * BLOCK, 1), sink_ref[0, GROUP * g], F32)
    for j in range(1, GROUP):
        out = jnp.where(head == j, sink_ref[0, GROUP * g + j], out)
    return out


def _attn_fwd(q, k, v, kc, vc, sink):
    length = q.shape[0]
    nb = length // BLOCK
    n_ctx = kc.shape[0]
    n_keys = 3 * BLOCK + n_ctx

    def body(sink_ref, q_ref, kp, k0, kn, vp, v0, vn, kc_ref, vc_ref, o_ref, p_ref):
        n = pl.program_id(0)
        valid = _attn_mask(n, length, n_keys)
        qb = q_ref[...]
        kall = jnp.concatenate([kp[...], k0[...], kn[...], kc_ref[...]], axis=0)
        vall = jnp.concatenate([vp[...], v0[...], vn[...], vc_ref[...]], axis=0)
        outs = []
        for g in range(N_KV_HEADS):
            lanes = slice(HEAD_DIM * g, HEAD_DIM * (g + 1))
            s = jnp.where(valid, _dot_nt(_group_rows(qb, g), kall[:, lanes]), NEG_INF)
            sk = _group_sink(sink_ref, g)
            m = jnp.maximum(jnp.max(s, axis=-1, keepdims=True), sk)
            e = jnp.exp(s - m)
            e_sink = jnp.exp(sk - m)
            inv = 1.0 / (jnp.sum(e, axis=-1, keepdims=True) + e_sink)
            p_ref[0, g, :, :n_keys] = (e * inv).astype(BF16)
            p_ref[0, g, :, n_keys:] = jnp.broadcast_to(e_sink * inv, (GROUP * BLOCK, 128)).astype(BF16)
            o = _dot(p_ref[0, g, :, :n_keys], vall[:, lanes])
            outs += [o[BLOCK * j:BLOCK * (j + 1)] for j in range(GROUP)]
        o_ref[...] = jnp.concatenate(outs, axis=1).astype(BF16)

    return pl.pallas_call(
        body, name="attn_fwd", grid=(nb,), in_specs=_attn_specs(nb, n_ctx),
        out_specs=(pl.BlockSpec((BLOCK, ATTN_WIDTH), lambda n: (n, 0)),
                   pl.BlockSpec((1, N_KV_HEADS, GROUP * BLOCK, n_keys + 128), lambda n: (n, 0, 0, 0))),
        out_shape=(jax.ShapeDtypeStruct((length, ATTN_WIDTH + POOL_WIDTH), BF16),
                   jax.ShapeDtypeStruct((nb, N_KV_HEADS, GROUP * BLOCK, n_keys + 128), BF16)), compiler_params=_cp(1),
    )(sink, q, k, k, k, v, v, v, kc, vc)


def _attn_bwd(q, k, v, kc, vc, dmix, probs, cos, sin):
    length = q.shape[0]
    nb = length // BLOCK
    n_ctx = kc.shape[0]
    n_keys = 3 * BLOCK + n_ctx

    def body(q_ref, kp, k0, kn, vp, v0, vn, kc_ref, vc_ref, do_ref, p_ref, cos_ref, sin_ref, cos_prev, sin_prev,
             dq_ref, dkv_ref, dkc_ref, dvc_ref, dsink_ref, done, ahead):
        n = pl.program_id(0)

        @pl.when(n == 0)
        def _():
            dkc_ref[...] = jnp.zeros_like(dkc_ref)
            dvc_ref[...] = jnp.zeros_like(dvc_ref)
            dsink_ref[...] = jnp.zeros_like(dsink_ref)
            done[...] = jnp.zeros_like(done)
            ahead[...] = jnp.zeros_like(ahead)

        def write_block(dkv):
            dkv_ref[:, :KV_WIDTH] = _rope(dkv[:, :KV_WIDTH], cos_prev[...], -sin_prev[...]).astype(BF16)
            dkv_ref[:, KV_WIDTH:] = dkv[:, KV_WIDTH:].astype(BF16)

        def query_block():
            qb, dob = q_ref[...], do_ref[...]
            kall = jnp.concatenate([kp[...], k0[...], kn[...], kc_ref[...]], axis=0)
            vall = jnp.concatenate([vp[...], v0[...], vn[...], vc_ref[...]], axis=0)
            srow = lax.broadcasted_iota(jnp.int32, (8, 128), 0)
            slane = lax.broadcasted_iota(jnp.int32, (8, 128), 1)
            dqs, dks, dvs = [], [], []
            dsink = jnp.zeros((8, 128), F32)
            for g in range(N_KV_HEADS):
                lanes = slice(HEAD_DIM * g, HEAD_DIM * (g + 1))
                kg, vg = kall[:, lanes], vall[:, lanes]
                qg, dog = _group_rows(qb, g), _group_rows(dob, g)
                pb = p_ref[0, g, :, :n_keys]
                p = pb.astype(F32)
                dp = _dot_nt(dog, vg)
                delta = jnp.sum(p * dp, axis=-1, keepdims=True)
                ds = (p * (dp - delta)).astype(BF16)
                dq = _dot(ds, kg) * SCALE
                dqs += [dq[BLOCK * j:BLOCK * (j + 1)] for j in range(GROUP)]
                dks.append(_dot_tn(ds, qg))
                dvs.append(_dot_tn(pb, dog))
                d_sink = p_ref[0, g, :, n_keys:].astype(F32)[:, :1] * delta
                for j in range(GROUP):
                    total = -jnp.sum(d_sink[BLOCK * j:BLOCK * (j + 1)], axis=0, keepdims=True)
                    dsink = dsink + jnp.where((srow == 0) & (slane == GROUP * g + j), total, 0.0)
            dq_ref[...] = _rope(jnp.concatenate(dqs, axis=1), cos_ref[...], -sin_ref[...]).astype(BF16)
            dkv = jnp.concatenate(dks + dvs, axis=1)
            write_block(done[...] + dkv[:BLOCK])
            done[...] = ahead[...] + dkv[BLOCK:2 * BLOCK]
            ahead[...] = dkv[2 * BLOCK:3 * BLOCK]
            dkc_ref[...] += dkv[3 * BLOCK:, :KV_WIDTH]
            dvc_ref[...] += dkv[3 * BLOCK:, KV_WIDTH:]
            dsink_ref[...] += dsink

        pl.when(n < nb)(query_block)

        @pl.when(n == nb)
        def _():
            write_block(done[...])

    here = lambda n: (jnp.minimum(n, nb - 1), 0)
    before = lambda n: (jnp.maximum(n - 1, 0), 0)
    ctx = pl.BlockSpec((n_ctx, KV_WIDTH), lambda n: (0, 0))
    return pl.pallas_call(
        body, name="attn_bwd", grid=(nb + 1,),
        in_specs=_attn_specs(nb, n_ctx)[1:] + [pl.BlockSpec((BLOCK, ATTN_WIDTH), here),
                                           pl.BlockSpec((1,) + probs.shape[1:], lambda n: (jnp.minimum(n, nb - 1), 0, 0, 0)),
                                           pl.BlockSpec((BLOCK, 128), here), pl.BlockSpec((BLOCK, 128), here),
                                           pl.BlockSpec((BLOCK, 128), before), pl.BlockSpec((BLOCK, 128), before)],
        out_specs=(pl.BlockSpec((BLOCK, ATTN_WIDTH), here), pl.BlockSpec((BLOCK, 2 * KV_WIDTH), before), ctx, ctx,
                   pl.BlockSpec((8, 128), lambda n: (0, 0))),
        out_shape=(jax.ShapeDtypeStruct((length, ATTN_WIDTH), BF16), jax.ShapeDtypeStruct((length, 2 * KV_WIDTH), BF16),
                   jax.ShapeDtypeStruct((n_ctx, KV_WIDTH), F32), jax.ShapeDtypeStruct((n_ctx, KV_WIDTH), F32),
                   jax.ShapeDtypeStruct((8, 128), F32)),
        scratch_shapes=[pltpu.VMEM((BLOCK, 2 * KV_WIDTH), F32), pltpu.VMEM((BLOCK, 2 * KV_WIDTH), F32)],
        compiler_params=_cp(1),
    )(q, k, k, k, v, v, v, kc, vc, dmix, probs, cos, sin, cos, sin)


def _shift_rows(e, s):
    n = e.shape[0]
    return e if s % n == 0 else pltpu.roll(e, (-s) % n, 0)


def _window_sum(e, w, first):
    s, n = e, 1
    while n < w:
        s = s + _shift_rows(s, n)
        n *= 2
    return _shift_rows(s, first)


def _pool_geometry(i, tm, length):
    pos = i * tm - HALO + lax.broadcasted_iota(jnp.int32, (tm + 2 * HALO, 1), 0)
    inside = (pos >= 0) & (pos < length)
    inv_counts = []
    for w in POOL_WINDOWS:
        lo = jnp.clip(pos - w // 2, 0, length)
        hi = jnp.clip(pos - w // 2 + w, 0, length)
        inv_counts.append(1.0 / jnp.maximum(hi - lo, 1).astype(F32))
    return inside, inv_counts


def _halo_specs(tm, width, length, col=0):
    per = tm // HALO
    last = length // HALO - 1
    return [pl.BlockSpec((HALO, width), lambda i: (jnp.maximum(i * per - 1, 0), col)),
            pl.BlockSpec((tm, width), lambda i: (i, col)),
            pl.BlockSpec((HALO, width), lambda i: (jnp.minimum((i + 1) * per, last), col))]


def _pooled(ext, inv_counts, tm):
    outs = []
    for g, w in enumerate(POOL_WINDOWS):
        e = ext[:, POOL_GROUP_DIM * g:POOL_GROUP_DIM * (g + 1)]
        mean = _window_sum(e, w, -(w // 2)) * inv_counts[g]
        outs.append((mean - e)[HALO:HALO + tm])
    return outs


def _pool_fwd(u, pool_w, pool_scale, mix):
    length = u.shape[0]
    tm = _pick(length, 512, 256, 128)

    def body(up, u0, un, w_ref, sc_ref, mix_ref, o_ref):
        inside, inv_counts = _pool_geometry(pl.program_id(0), tm, length)
        ext = jnp.where(inside, jnp.concatenate([up[...], u0[...], un[...]], axis=0), 0.0)
        pooled = _pooled(ext, inv_counts, tm)
        mixed = [_dot(pooled[g].astype(BF16), w_ref[g]) for g in range(len(POOL_WINDOWS))]
        o_ref[...] = (jnp.concatenate(mixed, axis=1) * sc_ref[...]).astype(BF16)

    return pl.pallas_call(
        body, name="pool_fwd", grid=(length // tm,),
        in_specs=_halo_specs(tm, POOL_WIDTH, length) + [pl.BlockSpec(pool_w.shape, lambda i: (0, 0, 0)), _row_spec(POOL_WIDTH), ANY],
        out_specs=pl.BlockSpec((tm, POOL_WIDTH), lambda i: (i, 1)),
        out_shape=jax.ShapeDtypeStruct(mix.shape, BF16), input_output_aliases={5: 0}, compiler_params=_cp(1),
    )(u, u, u, pool_w, pool_scale, mix)


def _pool_bwd(u, dmix, pool_w, pool_scale, after):
    length = u.shape[0]
    tm = _pick(length, 512, 256, 128)
    n_g = len(POOL_WINDOWS)

    def body(up, u0, un, dp_, d0, dn_, w_ref, sc_ref, after_ref, du_ref, dw_ref, dsc_ref):
        i = pl.program_id(0)

        @pl.when(i == 0)
        def _():
            dw_ref[...] = jnp.zeros_like(dw_ref)
            dsc_ref[...] = jnp.zeros_like(dsc_ref)

        inside, inv_counts = _pool_geometry(i, tm, length)
        ext = jnp.where(inside, jnp.concatenate([up[...], u0[...], un[...]], axis=0), 0.0)
        dext = jnp.where(inside, jnp.concatenate([dp_[...], d0[...], dn_[...]], axis=0).astype(F32), 0.0)
        dmixed = (dext * sc_ref[...]).astype(BF16)
        pooled = _pooled(ext, inv_counts, tm)
        dus, dscs = [], []
        for g, w in enumerate(POOL_WINDOWS):
            lanes = slice(POOL_GROUP_DIM * g, POOL_GROUP_DIM * (g + 1))
            dpooled = _dot_nt(dmixed[:, lanes], w_ref[g])
            spread = _window_sum(dpooled * inv_counts[g], w, -(w // 2 - 1))
            dus.append((spread - dpooled)[HALO:HALO + tm])
            pb = pooled[g].astype(BF16)
            dw_ref[g] += _dot_tn(pb, dmixed[HALO:HALO + tm, lanes])
            prod = dext[HALO:HALO + tm, lanes] * _dot(pb, w_ref[g])
            dscs.append(_fold8(prod))
        du_ref[...] = jnp.concatenate(dus, axis=1).astype(BF16)
        dsc_ref[...] += jnp.concatenate(dscs, axis=1)

    return pl.pallas_call(
        body, name="pool_bwd", grid=(length // tm,),
        in_specs=_halo_specs(tm, POOL_WIDTH, length) + _halo_specs(tm, POOL_WIDTH, length, col=1)
        + [pl.BlockSpec(pool_w.shape, lambda i: (0, 0, 0)), _row_spec(POOL_WIDTH), ANY],
        out_specs=(pl.BlockSpec((tm, POOL_WIDTH), lambda i: (i, 0)), pl.BlockSpec((n_g, POOL_GROUP_DIM, POOL_GROUP_DIM), lambda i: (0, 0, 0)),
                   pl.BlockSpec((8, POOL_WIDTH), lambda i: (0, 0))),
        out_shape=(jax.ShapeDtypeStruct((length, POOL_WIDTH), BF16), jax.ShapeDtypeStruct((n_g, POOL_GROUP_DIM, POOL_GROUP_DIM), F32),
                   jax.ShapeDtypeStruct((8, POOL_WIDTH), F32)),
        compiler_params=_cp(1),
    )(u, u, u, dmix, dmix, dmix, pool_w, pool_scale, after)


def _mixer_out(mix, w_out, x, g_a, nmw, sh_m, sc_m, after):
    t, d = x.shape

    def epi(mo, rs, tiles, vecs, outs, st_ref, live):
        ga, nw, sh, sc = vecs
        x1_ref, mo_ref, hm_ref = outs
        x1 = tiles[0][rs, :] + ga[...] * mo
        x1_ref[rs, :] = x1
        mo_ref[rs, :] = mo.astype(BF16)
        r = lax.rsqrt(jnp.mean(x1 * x1, axis=-1, keepdims=True) + EPS)
        hm_ref[rs, :] = (((x1 * r) * nw[...]) * (1.0 + sc[...]) + sh[...]).astype(BF16)

    return _mm_deferred("mixer_out", mix, w_out, nt=False, tm=_pick(t, 256, 128), epi=epi, tiles=(x,), vecs=(g_a, nmw, sh_m, sc_m),
                        out_tiles=(jax.ShapeDtypeStruct((t, d), F32), jax.ShapeDtypeStruct((t, d), BF16), jax.ShapeDtypeStruct((t, d), BF16)),
                        n_stats=1, after=after)[:3]


def _mlp_up(hm, w_up):
    t, d = hm.shape
    tm = _pick(t, 1024, 512, 256, 128)
    tn = 2048

    def epi(acc, ex, outs):
        outs[0][...] = jnp.square(jnp.maximum(acc[...], 0.0)).astype(BF16)

    return _mm("mlp_up", hm, w_up, nt=False, tm=tm, tn=tn, tk=d, epi=epi,
               out_shape=(jax.ShapeDtypeStruct((t, w_up.shape[1]), BF16),),
               out_specs=(pl.BlockSpec((tm, tn), lambda j, i, k: (i, j)),))[0]


def _mlp_down_loss(act, w_down, x1, target, g_m, fw, after):
    t, d = x1.shape

    def epi(dnv, rs, tiles, vecs, outs, st_ref, live):
        x1_ref, t_ref = tiles
        gm, fw_ref = vecs
        dx2_ref, ddn_ref = outs
        x2 = x1_ref[rs, :] + gm[...] * dnv
        r = lax.rsqrt(jnp.mean(x2 * x2, axis=-1, keepdims=True) + EPS)
        xh = x2 * r
        diff = xh * fw_ref[...] - t_ref[rs, :]
        dy = diff * (1.0 / d)
        dxh = dy * fw_ref[...]
        dx2 = r * (dxh - xh * jnp.mean(dxh * xh, axis=-1, keepdims=True))
        dx2_ref[rs, :] = dx2
        ddn_ref[rs, :] = (dx2 * gm[...]).astype(BF16)
        st_ref[0] += jnp.where(live, _fold8(diff * diff), 0.0)
        st_ref[1] += jnp.where(live, _fold8(dy * xh), 0.0)
        st_ref[2] += jnp.where(live, _fold8(dx2 * dnv), 0.0)

    return _mm_k_deferred("mlp_down_loss", act, w_down, nt=False, tm=_pick(t, 512, 256), tk=_pick(act.shape[1], 2048), epi=epi,
                          tiles=(x1, target), vecs=(g_m, fw), n_stats=3, after=after,
                          out_tiles=(jax.ShapeDtypeStruct((t, d), F32), jax.ShapeDtypeStruct((t, d), BF16)))


def _mlp_dx(dup, w_up, x1, dx2, mo, nmw, sc_m, g_a, after):
    t, d = x1.shape

    def epi(dh, rs, tiles, vecs, outs, st_ref, live):
        x1_ref, dx2_ref, mo_ref = tiles
        nw, sc, ga = vecs
        dx1_ref, dmi_ref = outs
        dx1 = _norm_bwd_rows(dh, x1_ref[rs, :], nw[...], sc[...], st_ref) + dx2_ref[rs, :]
        dx1_ref[rs, :] = dx1
        dmi_ref[rs, :] = (dx1 * ga[...]).astype(BF16)
        st_ref[3] += jnp.where(live, _fold8(dx1 * mo_ref[rs, :].astype(F32)), 0.0)

    return _mm_k_deferred("mlp_dx", dup, w_up, nt=True, tm=_pick(t, 512, 256), tk=_pick(dup.shape[1], 2048), epi=epi,
                          tiles=(x1, dx2, mo), vecs=(nmw, sc_m, g_a), n_stats=4, after=after,
                          out_tiles=(jax.ShapeDtypeStruct((t, d), F32), jax.ShapeDtypeStruct((t, d), BF16)))


def _mlp_dact(ddn, w_down, act):
    t, d = ddn.shape
    tm = _pick(t, 1024, 512, 256, 128)
    tn = 2048

    def epi(acc, ex, outs):
        outs[0][...] = (acc[...] * (2.0 * jnp.sqrt(ex[0][...]).astype(F32))).astype(BF16)

    tile = pl.BlockSpec((tm, tn), lambda j, i, k: (i, j))
    return _mm("mlp_dact", ddn, w_down, nt=True, tm=tm, tn=tn, tk=d, epi=epi, extras=(act,), extra_specs=[tile],
               out_shape=(jax.ShapeDtypeStruct(act.shape, BF16),), out_specs=(tile,), vmem_mib=56)[0]


def _norm_bwd_rows(dh, xv, nw, sc, st_ref):
    r = lax.rsqrt(jnp.mean(xv * xv, axis=-1, keepdims=True) + EPS)
    xh = xv * r
    dy = dh * (1.0 + sc)
    st_ref[0] += _fold8(dh)
    st_ref[1] += _fold8(dh * (xh * nw))
    st_ref[2] += _fold8(dy * xh)
    dxh = dy * nw
    return r * (dxh - xh * jnp.mean(dxh * xh, axis=-1, keepdims=True))


def _mixer_dmix(dmi, w_out, after):
    t, d = dmi.shape
    tm = _pick(t, 1024, 512, 256, 128)

    def epi(acc, ex, outs):
        outs[0][...] = acc[...].astype(BF16)

    n = w_out.shape[0]
    return _mm("mixer_dmix", dmi, w_out, nt=True, tm=tm, tn=n, tk=d, epi=epi, after=after,
               out_shape=(jax.ShapeDtypeStruct((t, n), BF16),), out_specs=(pl.BlockSpec((tm, n), lambda j, i, k: (i, 0)),))[0]


def _mixer_dx(name, dp, w_in, x, dx1, naw, sc_a, after):
    t, d = x.shape

    def epi(dh, rs, tiles, vecs, outs, st_ref, live):
        x_ref, dx1_ref = tiles
        nw, sc = vecs
        outs[0][rs, :] = _norm_bwd_rows(dh, x_ref[rs, :], nw[...], sc[...], st_ref) + dx1_ref[rs, :]

    return _mm_deferred(name, dp, w_in, nt=True, tm=_pick(t, 256, 128), epi=epi, tiles=(x, dx1), vecs=(naw, sc_a),
                        out_tiles=(jax.ShapeDtypeStruct((t, d), F32),), n_stats=3, after=after, vmem_mib=56)


def _silu(v):
    return v / (1.0 + jnp.exp(-v))


def _ada_fwd(cond, w_ada, b_ada):
    d, n = w_ada.shape
    tn = 512

    def body(c_ref, w_ref, b_ref, o_ref):
        o_ref[...] = _dot(_silu(c_ref[...]).astype(BF16), w_ref[...].astype(BF16)) + b_ref[...]

    return pl.pallas_call(
        body, name="ada_fwd", grid=(n // tn,),
        in_specs=[pl.BlockSpec(cond.shape, lambda j: (0, 0)), pl.BlockSpec((d, tn), lambda j: (0, j)), pl.BlockSpec((1, tn), lambda j: (0, j))],
        out_specs=pl.BlockSpec((cond.shape[0], tn), lambda j: (0, j)), out_shape=jax.ShapeDtypeStruct((cond.shape[0], n), F32),
        compiler_params=_cp(1),
    )(cond, w_ada, b_ada)


def _adamw_math(w, g, m, v):
    m = ADAM_B1 * m + (1.0 - ADAM_B1) * g
    v = ADAM_B2 * v + (1.0 - ADAM_B2) * jnp.square(g)
    m_hat = m / (1.0 - ADAM_B1 ** ADAM_STEP)
    v_hat = v / (1.0 - ADAM_B2 ** ADAM_STEP)
    return -ADAM_LR * (m_hat / (jnp.sqrt(v_hat) + ADAM_EPS) + ADAM_WD * w), m, v


def _ada_bwd(cond, dm, w_ada, m_ada, v_ada):
    d, n = w_ada.shape
    tn = 256
    rows = cond.shape[0]

    def body(c_ref, dm_ref, w_ref, m_ref, v_ref, g_ref, dl_ref, nm_ref, nv_ref, pc_ref):
        @pl.when(pl.program_id(0) == 0)
        def _():
            pc_ref[...] = jnp.zeros_like(pc_ref)

        dmb = dm_ref[...].astype(BF16)
        w = w_ref[...]
        g = _dot_tn(_silu(c_ref[...]).astype(BF16), dmb)
        g_ref[...] = g
        dl_ref[...], nm_ref[...], nv_ref[...] = _adamw_math(w, g, m_ref[...], v_ref[...])
        pc_ref[...] += _dot_nt(dm_ref[8:16, :].astype(BF16), w.astype(BF16))

    tile = pl.BlockSpec((d, tn), lambda j: (0, j))
    like = jax.ShapeDtypeStruct((d, n), F32)
    return pl.pallas_call(
        body, name="ada_bwd", grid=(n // tn,),
        in_specs=[pl.BlockSpec((rows, d), lambda j: (0, 0)), pl.BlockSpec((rows, tn), lambda j: (0, j)), tile, tile, tile],
        out_specs=(tile, tile, tile, tile, pl.BlockSpec((8, d), lambda j: (0, 0))),
        out_shape=(like, like, like, like, jax.ShapeDtypeStruct((8, d), F32)), compiler_params=_cp(1),
    )(cond, dm, w_ada, m_ada, v_ada)


def _adamw(name, w, g, m, v, after=None):
    return _ew(name, lambda w_, g_, m_, v_: (g_,) + _adamw_math(w_, g_, m_, v_), [w, g, m, v], [F32, F32, F32, F32],
               g if after is None else after)


def _colsum(st):
    return jnp.sum(st, axis=1)


def kernel(x, c, ctx, c_ctx, norm_attn_w, norm_mlp_w, w_ada, b_ada, w_in, attn_sink, pool_w, pool_scale, w_out, w_mlp_up, w_mlp_down, final_norm_w, loss_target, m_c_ctx, m_norm_attn_w, m_norm_mlp_w, m_w_ada, m_b_ada, m_w_in, m_attn_sink, m_pool_w, m_pool_scale, m_w_out, m_w_mlp_up, m_w_mlp_down, m_final_norm_w, v_c_ctx, v_norm_attn_w, v_norm_mlp_w, v_w_ada, v_b_ada, v_w_in, v_attn_sink, v_pool_w, v_pool_scale, v_w_out, v_w_mlp_up, v_w_mlp_down, v_final_norm_w):
    length, d = x.shape[1], x.shape[2]
    n_ctx = ctx.shape[1]
    pos = (lax.axis_index("x"), lax.axis_index("y"), lax.axis_index("c"))
    me, chip = _dev_index(pos), _chip_index(pos)
    xs, tgt, cx = x.reshape(length, d), loss_target.reshape(length, d), ctx.reshape(n_ctx, d)
    n_ada = w_ada.shape[2]

    c_all = _allgather8("gather_c", jnp.pad(c, ((0, 7), (0, 0))))
    mixer_bigs = [_Big("col", w_in.shape[1:]), _Big("pool", pool_w.shape[1:]), _Big("row", w_out.shape[1:])]
    mlp_bigs = [_Big("col", w_mlp_up.shape[1:]), _Big("row", w_mlp_down.shape[1:])]
    placed = [_cast_place(f"place_{i}", b, s, c_all)[0] for i, (b, s) in enumerate(zip(mixer_bigs, [w_in[0], pool_w[0], w_out[0]]))]
    wo_bigs, win_bigs = mixer_bigs[1:], mixer_bigs[:1]
    win_flight = _split("gather_win_ici", placed[:1], _gather_ici_remote(win_bigs, 0))
    token, placed_mlp = win_flight[3], []
    for i, (b, s) in enumerate(zip(mlp_bigs, [w_mlp_up[0], w_mlp_down[0]])):
        p, token = _cast_place(f"place_mlp_{i}", b, s, token)
        placed_mlp.append(p)
    cos, sin = _rope_tables(length, True, token[0, 0])
    cond = jnp.concatenate([c_all[:, 0, :], jnp.pad(c_ctx[None, :], ((0, 7), (0, 0)))], axis=0) + 0.0 * cos[0, 0]
    b_shard = lax.dynamic_slice_in_dim(b_ada, chip * n_ada, n_ada, axis=1)
    mod_all = _allgather8("gather_mod", _ada_fwd(cond, w_ada[0], b_shard))
    mod = jnp.concatenate([mod_all[0], mod_all[2], mod_all[4], mod_all[6]], axis=1)
    mine = lax.dynamic_slice_in_dim(mod, me, 1, axis=0)
    sh_a, sc_a, g_a, sh_m, sc_m, g_m = [mine[:, d * i:d * (i + 1)] for i in range(6)]
    csh_a, csc_a = mod[8:9, :d], mod[8:9, d:2 * d]

    full = lambda bigs: [jax.ShapeDtypeStruct(b.full_shape, BF16) for b in bigs]
    wo_flight = _split("gather_wo_ici", placed[1:], _gather_ici_remote(wo_bigs, 0), after=mod)
    (win_b,) = _exchange("gather_win_d2d", _join(win_flight, mod), full(win_bigs), _gather_d2d_remote(win_bigs, 1), aliases={0: 0})
    flight = _split("gather_mlp_ici", placed_mlp, _gather_ici_remote(mlp_bigs, 0), after=win_b)

    one, zero = _rope_tables(n_ctx, False)
    h, q, k, v, u = _mixer_in("mixer_in", xs, norm_attn_w, sh_a, sc_a, win_b, cos, sin, flight[3])
    hc, _, kc, vc, _ = _mixer_in("mixer_in_ctx", cx, norm_attn_w, csh_a, csc_a, win_b, one, zero, flight[3])
    pw_b, wout_b = _exchange("gather_wo_d2d", _join(wo_flight, hc), full(wo_bigs), _gather_d2d_remote(wo_bigs, 2), aliases={0: 0, 1: 1})
    wout_b = wout_b.reshape(-1, d)
    attn, probs = _attn_fwd(q, k, v, kc, vc, attn_sink)
    mix = _pool_fwd(u, pw_b, pool_scale, attn)
    flight = _split("gather_mlp_d2d", _join(flight, mix), _gather_d2d_remote(mlp_bigs, 0))
    x1, mo, hm = _mixer_out(mix, wout_b, xs, g_a, norm_mlp_w, sh_m, sc_m, flight[3])
    wup_b, wdn_b = _join(flight, hm)
    wdn_b = wdn_b.reshape(-1, d)
    act = _mlp_up(hm, wup_b)
    dx2, ddn, st_loss = _mlp_down_loss(act, wdn_b, x1, tgt, g_m, final_norm_w[None, :], c)
    st_loss = _colsum(st_loss)

    tt = _pick(length, 2048, 1024, 512, 256, 128)
    g_wdn = _mm_tn("grad_w_down", act, ddn, BF16, tmo=1024, tn=d, tt=tt)
    dup = _mlp_dact(ddn, wdn_b, act)
    g_wup = _mm_tn("grad_w_up", hm, dup, BF16, tmo=d, tn=1024, tt=tt)
    empty = lambda shapes: [lax.empty(s.shape, s.dtype) for s in shapes]
    grads = [g_wup, g_wdn.reshape(mlp_bigs[1].full_shape)]
    flight = _split("reduce_mlp_d2d", grads + empty(_halves(mlp_bigs)), _reduce_d2d_remote(mlp_bigs))
    dx1, dmi, st_mlp = _mlp_dx(dup, wup_b, x1, dx2, mo, norm_mlp_w, sc_m, g_a, flight[3])
    st_mlp = _colsum(st_mlp)
    landed = _join(flight, dmi)
    mlp_chip = _chip_sums("mlp", mlp_bigs, landed[:2], landed[2:])
    flight = _split("reduce_mlp_ici", mlp_chip + empty(_thirds(mlp_bigs)), _reduce_ici_remote(mlp_bigs))
    g_wout = _mm_tn("grad_w_out", mix, dmi, BF16, tmo=1024, tn=d, tt=tt)
    dmix = _mixer_dmix(dmi, wout_b, flight[3])
    dq, dkv, dkc, dvc, dsink = _attn_bwd(q, k, v, kc, vc, dmix, probs, cos, sin)
    landed = _join(flight, dq)
    flight = _split("reduce_mlp_share", _piece_sums("mlp", mlp_bigs, landed[:2], landed[2:]), _share_remote(mlp_bigs, 0))
    du, g_pw, st_pool = _pool_bwd(u, dmix, pw_b, pool_scale, flight[3])
    g_mlp = _join(flight, du)

    wo_bigs, win_bigs = mixer_bigs[1:], mixer_bigs[:1]
    wo_chip = _reduce_to_chip("wo", wo_bigs, [g_pw.astype(BF16), g_wout.reshape(wo_bigs[1].full_shape)])
    flight = _split("reduce_wo_ici", wo_chip + empty(_thirds(wo_bigs)), _reduce_ici_remote(wo_bigs))
    dkv_ctx = jnp.concatenate([dkc.astype(BF16), dvc.astype(BF16)], axis=1)
    at_kv, at_u = ATTN_WIDTH, ATTN_WIDTH + 2 * KV_WIDTH
    tt_in = _pick(length, 1024, 512, 256, 128)
    g_win = jnp.concatenate([_mm_tn("grad_w_in_q", h, dq, BF16, tmo=d, tn=ATTN_WIDTH, tt=tt_in, after=flight[3]),
                             _mm_tn("grad_w_in_kv", h, dkv, BF16, tmo=d, tn=2 * KV_WIDTH, tt=tt_in, more=(hc, dkv_ctx)),
                             _mm_tn("grad_w_in_u", h, du, BF16, tmo=d, tn=POOL_WIDTH, tt=tt_in)], axis=1)
    wo_landed = _join(flight, g_win)
    win_chip = _reduce_to_chip("win", win_bigs, [g_win])
    flight = _split("reduce_win_ici", win_chip + empty(_thirds(win_bigs)), _reduce_ici_remote(win_bigs))
    grad_x, st_mix = _mixer_dx("mixer_dx", [(dq, 0), (dkv, at_kv), (du, at_u)], win_b, xs, dx1, norm_attn_w, sc_a, flight[3])
    _, st_ctx = _mixer_dx("mixer_dx_ctx", [(dkv_ctx, at_kv)], win_b, cx, jnp.zeros((n_ctx, d), F32), norm_attn_w, csc_a, flight[3])
    st_mix, st_ctx = _colsum(st_mix), _colsum(st_ctx)
    win_landed = _join(flight, grad_x)
    g_mixer = (_reduce_finish("win", win_bigs, win_landed[:1], win_landed[1:])
               + _reduce_finish("wo", wo_bigs, wo_landed[:2], wo_landed[2:]))

    zrow = jnp.zeros((d,), F32)
    pad = lambda a: jnp.pad(a, (0, d - a.shape[0]))
    mine_rows = [st_mix[0], st_mix[1], st_mlp[3], st_mlp[0], st_mlp[1], st_loss[2],
                 st_ctx[0], st_ctx[1],
                 st_mix[2] + st_ctx[2], st_mlp[2], st_loss[1],
                 pad(jnp.sum(st_pool, axis=0)), pad(dsink[0, :N_Q_HEADS]), st_loss[0]] + [zrow] * 2
    flight = _allgather8_split("gather_small", jnp.concatenate(mine_rows).reshape(len(mine_rows), d), me)
    res = {"w_mlp_up": tuple(_adamw("adamw_w_mlp_up", w_mlp_up, g_mlp[0].reshape(w_mlp_up.shape), m_w_mlp_up, v_w_mlp_up, flight[3]))}
    small_all = _join(flight, res["w_mlp_up"][1])[1]
    small = small_all[0]
    for i in range(1, 8):
        small = small + small_all[i]
    loss = 0.5 / d * jnp.sum(small[13])
    dm_rows = small_all[:, 0:6, :].reshape(8, 6 * d)
    dm_ctx = jnp.concatenate([small[6], small[7], jnp.zeros((4 * d,), F32)])[None, :]
    dm = jnp.concatenate([dm_rows, jnp.pad(dm_ctx, ((0, 7), (0, 0)))], axis=0)
    g_bada = jnp.sum(dm[:9], axis=0, keepdims=True)
    dm_shard = lax.dynamic_slice_in_dim(dm, chip * n_ada, n_ada, axis=1)
    g_wada, dl_wada, nm_wada, nv_wada, part_cctx = _ada_bwd(cond, dm_shard, w_ada[0], m_w_ada[0], v_w_ada[0])
    flight = _allgather8_split("gather_cctx", part_cctx, me)
    res["w_mlp_down"] = tuple(_adamw("adamw_w_mlp_down", w_mlp_down, g_mlp[1].reshape(w_mlp_down.shape), m_w_mlp_down,
                                     v_w_mlp_down, flight[3]))
    cctx_all = _join(flight, res["w_mlp_down"][1])[1]
    dsilu_in = cctx_all[0, 0] + cctx_all[2, 0] + cctx_all[4, 0] + cctx_all[6, 0]
    sig = 1.0 / (1.0 + jnp.exp(-c_ctx))
    g_cctx = dsilu_in * (sig * (1.0 + c_ctx * (1.0 - sig)))

    for nm, w_, g_, m_, v_ in zip(["w_in", "pool_w", "w_out"], [w_in, pool_w, w_out], g_mixer,
                                  [m_w_in, m_pool_w, m_w_out], [v_w_in, v_pool_w, v_w_out]):
        res[nm] = tuple(_adamw("adamw_" + nm, w_, g_.reshape(w_.shape), m_, v_))
    res["w_ada"] = (g_wada[None], dl_wada[None], nm_wada[None], nv_wada[None])

    def pack(cc, na, nm_, ba, sk, ps, fn):
        flat = [cc.reshape(-1), na.reshape(-1), nm_.reshape(-1), ba.reshape(-1), pad(sk.reshape(-1)), pad(ps.reshape(-1)),
                fn.reshape(-1), jnp.zeros((4 * d,), F32)]
        return jnp.concatenate(flat).reshape(16, d)

    w_s = pack(c_ctx, norm_attn_w, norm_mlp_w, b_ada, attn_sink, pool_scale, final_norm_w)
    m_s = pack(m_c_ctx, m_norm_attn_w, m_norm_mlp_w, m_b_ada, m_attn_sink, m_pool_scale, m_final_norm_w)
    v_s = pack(v_c_ctx, v_norm_attn_w, v_norm_mlp_w, v_b_ada, v_attn_sink, v_pool_scale, v_final_norm_w)
    g_s = pack(g_cctx, small[8], small[9], g_bada, small[12][:N_Q_HEADS], small[11][:POOL_WIDTH], small[10])
    small_out = _adamw("adamw_small", w_s, g_s, m_s, v_s)

    def unpack(p):
        return {"c_ctx": p[0], "norm_attn_w": p[1:2], "norm_mlp_w": p[2:3], "b_ada": p[3:9].reshape(1, 6 * d),
                "attn_sink": p[9:10, :N_Q_HEADS], "pool_scale": p[10:11, :POOL_WIDTH], "final_norm_w": p[11]}

    small_res = [unpack(p) for p in small_out]
    order = ["c_ctx", "norm_attn_w", "norm_mlp_w", "w_ada", "b_ada", "w_in", "attn_sink", "pool_w", "pool_scale",
             "w_out", "w_mlp_up", "w_mlp_down", "final_norm_w"]
    outs = [loss, grad_x.reshape(x.shape)]
    for kind in range(4):
        for nm in order:
            outs.append(res[nm][kind] if nm in res else small_res[kind][nm])
    return tuple(outs)
```

```python
import functools

import jax
import jax.numpy as jnp
from jax import lax
from jax.experimental import pallas as pl
from jax.experimental.pallas import tpu as pltpu

F32 = jnp.float32
BF16 = jnp.bfloat16
EPS = 1e-6
NEG_INF = -1e30
HEAD_DIM = 64
N_Q_HEADS = 16
N_KV_HEADS = 4
GROUP = N_Q_HEADS // N_KV_HEADS
ATTN_WIDTH = N_Q_HEADS * HEAD_DIM
KV_WIDTH = N_KV_HEADS * HEAD_DIM
POOL_WINDOWS = (2, 4, 8, 16)
POOL_GROUP_DIM = 256
POOL_WIDTH = len(POOL_WINDOWS) * POOL_GROUP_DIM
BLOCK = 128
GRID_W = 64
ROPE_BASE = 10000.0
SCALE = HEAD_DIM ** -0.5
HALO = 16
STRIP = 16
ADAM_LR, ADAM_B1, ADAM_B2, ADAM_EPS, ADAM_WD, ADAM_STEP = 0.001, 0.9, 0.999, 1e-08, 0.01, 10
MESH = pl.DeviceIdType.MESH
MIB = 1024 * 1024
ANY = pl.BlockSpec(memory_space=pl.ANY)


def _cp(n_axes, vmem_mib=48):
    return pltpu.CompilerParams(dimension_semantics=("arbitrary",) * n_axes, vmem_limit_bytes=vmem_mib * MIB)


def _fold8(v):
    s = v[0:8]
    for t in range(1, v.shape[0] // 8):
        s = s + v[8 * t:8 * t + 8]
    return s


def _dot(a, b):
    return jnp.dot(a, b, preferred_element_type=F32)


def _dot_nt(a, b):
    return lax.dot_general(a, b, (((1,), (1,)), ((), ())), preferred_element_type=F32)


def _dot_tn(a, b):
    return lax.dot_general(a, b, (((0,), (0,)), ((), ())), preferred_element_type=F32)


def _pick(n, *cands):
    for t in cands:
        if n % t == 0:
            return t
    return n


def _flip(pos, mask):
    return tuple((1 - v) if (mask >> (2 - i)) & 1 else v for i, v in enumerate(pos))


def _exchange(name, ins, out_shapes, remote, local=(), aliases=None):
    n_io = len(ins) + len(out_shapes)

    def body(*refs):
        io = refs[:n_io]
        send_sems, recv_sems, local_sems = refs[n_io:]
        me = (lax.axis_index("x"), lax.axis_index("y"), lax.axis_index("c"))

        def copy(i, sender):
            mask, src_fn, dst_fn = remote[i]
            return pltpu.make_async_remote_copy(
                src_ref=src_fn(io, sender), dst_ref=dst_fn(io, sender), send_sem=send_sems.at[i],
                recv_sem=recv_sems.at[i], device_id=_flip(sender, mask), device_id_type=MESH)

        own = [pltpu.make_async_copy(s(io, me), d(io, me), local_sems.at[i]) for i, (s, d) in enumerate(local)]
        for cp in own:
            cp.start()
        sends = [copy(i, me) for i in range(len(remote))]
        for cp in sends:
            cp.start()
        for i in range(len(remote)):
            copy(i, _flip(me, remote[i][0])).wait_recv()
        for cp in sends:
            cp.wait_send()
        for cp in own:
            cp.wait()

    return pl.pallas_call(
        body, name=name, out_shape=tuple(out_shapes),
        in_specs=[ANY] * len(ins), out_specs=tuple([ANY] * len(out_shapes)),
        scratch_shapes=[pltpu.SemaphoreType.DMA((len(remote),)), pltpu.SemaphoreType.DMA((len(remote),)),
                        pltpu.SemaphoreType.DMA((max(len(local), 1),))],
        input_output_aliases=aliases or {},
    )(*ins)


HBM = pl.BlockSpec(memory_space=pltpu.HBM)
SEM = pl.BlockSpec(memory_space=pltpu.SEMAPHORE)
EFFECT = pltpu.SideEffectType.DATAFLOW_SIDE_EFFECTING


def _split_copy(remote, i, io, send_sems, recv_sems, sender):
    mask, src_fn, dst_fn = remote[i]
    return pltpu.make_async_remote_copy(
        src_ref=src_fn(io, sender), dst_ref=dst_fn(io, sender), send_sem=send_sems.at[i],
        recv_sem=recv_sems.at[i], device_id=_flip(sender, mask), device_id_type=MESH)


def _exchange_start(name, bufs, remote, after=None):
    n, r = len(bufs), len(remote)
    more = [] if after is None else [after]

    def body(*refs):
        io, (send_sems, recv_sems, token) = refs[:n], refs[-3:]
        me = (lax.axis_index("x"), lax.axis_index("y"), lax.axis_index("c"))
        for i in range(r):
            _split_copy(remote, i, io, send_sems, recv_sems, me).start()
        token[...] = jnp.zeros_like(token)

    res = pl.pallas_call(
        body, name=name,
        out_shape=tuple(pltpu.HBM(b.shape, b.dtype) for b in bufs)
        + (pltpu.SemaphoreType.DMA((r,)), pltpu.SemaphoreType.DMA((r,)), jax.ShapeDtypeStruct((8, 128), F32)),
        in_specs=[HBM] * n + [ANY] * len(more), out_specs=tuple([HBM] * n) + (SEM, SEM, pl.BlockSpec(memory_space=pltpu.VMEM)),
        input_output_aliases={i: i for i in range(n)}, compiler_params=pltpu.CompilerParams(has_side_effects=EFFECT),
    )(*[pltpu.with_memory_space_constraint(b, pltpu.HBM) for b in bufs], *more)
    return list(res[:n]), res[n], res[n + 1], res[n + 2]


def _exchange_wait(name, bufs, send_sems, recv_sems, remote, after):
    n, r = len(bufs), len(remote)

    def body(*refs):
        io, ss, rs = refs[:n], refs[n], refs[n + 1]
        me = (lax.axis_index("x"), lax.axis_index("y"), lax.axis_index("c"))
        for i in range(r):
            _split_copy(remote, i, io, ss, rs, _flip(me, remote[i][0])).wait_recv()
        for i in range(r):
            _split_copy(remote, i, io, ss, rs, me).wait_send()

    return list(pl.pallas_call(
        body, name=name, out_shape=tuple(pltpu.HBM(b.shape, b.dtype) for b in bufs),
        in_specs=[HBM] * n + [SEM, SEM, ANY], out_specs=tuple([HBM] * n),
        input_output_aliases={i: i for i in range(n)}, compiler_params=pltpu.CompilerParams(has_side_effects=EFFECT),
    )(*bufs, send_sems, recv_sems, after))


def _my_c():
    return lax.axis_index("c")


def _my_chip():
    return 2 * lax.axis_index("x") + lax.axis_index("y")


def _dev_index(pos):
    return 4 * pos[0] + 2 * pos[1] + pos[2]


def _chip_index(pos):
    return 2 * pos[0] + pos[1]


def _allgather8(name, v):
    out = jax.ShapeDtypeStruct((8,) + v.shape, v.dtype)
    remote = [(mask, lambda io, pos: io[0], lambda io, pos: io[1].at[_dev_index(pos)]) for mask in range(1, 8)]
    local = [(lambda io, pos: io[0], lambda io, pos: io[1].at[_dev_index(pos)])]
    return _exchange(name, [v], [out], remote, local)[0]


class _Big:
    def __init__(self, kind, shard_shape):
        self.kind = kind
        self.shard_shape = tuple(shard_shape)
        if kind == "col":
            r, cs = shard_shape
            self.full_shape = (r, 4 * cs)
            self.piece_shape = (r // 2, cs)
            self.half_shape = (r // 2, 4 * cs)
        elif kind == "row":
            rs, c = shard_shape
            self.full_shape = (4, 2, rs // 2, c)
            self.piece_shape = (1, 1, rs // 2, c)
            self.half_shape = (4, 1, rs // 2, c)
        else:
            self.full_shape = (4, 256, 256)
            self.piece_shape = (2, 64, 256)
            self.half_shape = (2, 256, 256)

    def shard_as_pieces(self, a):
        return a.reshape((1, 2) + self.piece_shape[2:]) if self.kind == "row" else a

    def piece(self, ref, k, h):
        if self.kind == "col":
            r, cs = self.piece_shape
            return ref.at[pl.ds(h * r, r), pl.ds(k * cs, cs)]
        if self.kind == "row":
            return ref.at[pl.ds(k, 1), pl.ds(h, 1)]
        return ref.at[pl.ds(2 * h, 2), pl.ds(64 * k, 64)]

    def half_of_shard(self, ref, h):
        if self.kind == "col":
            return ref.at[pl.ds(h * self.piece_shape[0], self.piece_shape[0])]
        if self.kind == "row":
            return ref.at[:, pl.ds(h, 1)]
        return ref.at[pl.ds(2 * h, 2)]

    def half_of_full(self, ref, h):
        if self.kind == "col":
            return ref.at[pl.ds(h * self.half_shape[0], self.half_shape[0])]
        if self.kind == "row":
            return ref.at[:, pl.ds(h, 1)]
        return ref.at[pl.ds(2 * h, 2)]

    def piece_of_half(self, ref, k):
        if self.kind == "col":
            return ref.at[:, pl.ds(k * self.piece_shape[1], self.piece_shape[1])]
        if self.kind == "row":
            return ref.at[pl.ds(k, 1)]
        return ref.at[:, pl.ds(64 * k, 64)]


CHIP_MASKS = (4, 2, 6)


def _cast_place(name, big, shard, after):
    if big.kind == "col":
        r, cs = big.shard_shape
        tr = _pick(r, 512, 256, 128)
        src, grid, blk = shard, (r // tr,), (tr, cs)
        imap, omap = (lambda i: (i, 0)), (lambda i: (i, _my_chip()))
    elif big.kind == "row":
        rs, c = big.shard_shape
        tr = _pick(rs // 2, 256, 128)
        src, grid, blk = big.shard_as_pieces(shard), (2, rs // 2 // tr), (1, 1, tr, c)
        imap, omap = (lambda h, i: (0, h, i, 0)), (lambda h, i: (_my_chip(), h, i, 0))
    else:
        src, grid, blk = shard, (1,), big.shard_shape
        imap, omap = (lambda i: (0, 0, 0)), (lambda i: (0, _my_chip(), 0))

    def body(s_ref, after_ref, o_ref, token_ref):
        o_ref[...] = s_ref[...].astype(BF16)
        token_ref[...] = jnp.zeros_like(token_ref)

    return pl.pallas_call(
        body, name=name, grid=grid, in_specs=[pl.BlockSpec(blk, imap), ANY],
        out_specs=(pl.BlockSpec(blk, omap), pl.BlockSpec((8, 128), lambda *_: (0, 0))),
        out_shape=(jax.ShapeDtypeStruct(big.full_shape, BF16), jax.ShapeDtypeStruct((8, 128), F32)), compiler_params=_cp(len(grid)),
    )(src, after)


def _gather_ici_remote(bigs, off):
    remote = []
    for a, b in enumerate(bigs):
        for mask in CHIP_MASKS:
            def mine(io, p, a=a, b=b):
                return b.piece(io[off + a], _chip_index(p), p[2])
            remote.append((mask, mine, mine))
    return remote


def _gather_d2d_remote(bigs, off):
    remote = []
    for a, b in enumerate(bigs):
        for mask in CHIP_MASKS:
            def region(io, p, a=a, b=b, mask=mask):
                return b.piece(io[off + a], _chip_index(_flip(p, mask)), p[2])
            remote.append((1, region, region))
    return remote


def _ew(name, fn, ins, out_dtypes, after, rows_per_step=256):
    shape = ins[0].shape
    last = shape[-1]
    rows = 1
    for s in shape[:-1]:
        rows *= s
    ins2 = [a.reshape(rows, last) for a in ins]
    tr = _pick(rows, rows_per_step, 128, 64, 32, 16, 8)
    spec = pl.BlockSpec((tr, last), lambda i: (i, 0))

    def body(*refs):
        outs = fn(*[r[...] for r in refs[:len(ins)]])
        for o_ref, o in zip(refs[len(ins) + 1:], outs):
            o_ref[...] = o.astype(o_ref.dtype)

    outs = pl.pallas_call(
        body, name=name, grid=(rows // tr,), in_specs=[spec] * len(ins) + [ANY], out_specs=tuple([spec] * len(out_dtypes)),
        out_shape=tuple(jax.ShapeDtypeStruct((rows, last), d) for d in out_dtypes), compiler_params=_cp(1),
    )(*ins2, after)
    return [o.reshape(shape) for o in outs]


def _chip_sum(name, big, grad, from_sibling):
    if big.kind == "col":
        rh, w = big.half_shape
        tr = _pick(rh, 256, 128)
        nb = rh // tr
        grid, blk = (nb,), (tr, w)
        gmap, hmap = (lambda i: (_my_c() * nb + i, 0)), (lambda i: (i, 0))
    elif big.kind == "row":
        rh, w = big.half_shape[2:]
        tr = _pick(rh, 256, 128)
        grid, blk = (4, rh // tr), (1, 1, tr, w)
        gmap, hmap = (lambda k, i: (k, _my_c(), i, 0)), (lambda k, i: (k, 0, i, 0))
    else:
        grid, blk = (1,), big.half_shape
        gmap, hmap = (lambda i: (_my_c(), 0, 0)), (lambda i: (0, 0, 0))

    def body(g_ref, s_ref, o_ref):
        o_ref[...] = (g_ref[...].astype(F32) + s_ref[...].astype(F32)).astype(BF16)

    return pl.pallas_call(
        body, name=name, grid=grid, in_specs=[pl.BlockSpec(blk, gmap), pl.BlockSpec(blk, hmap)],
        out_specs=pl.BlockSpec(blk, hmap), out_shape=jax.ShapeDtypeStruct(big.half_shape, BF16), compiler_params=_cp(len(grid)),
    )(grad, from_sibling)


def _piece_sum(name, big, chip_sum, thirds):
    if big.kind == "col":
        rp, cs = big.piece_shape
        tr = _pick(rp, 256, 128)
        nb = rp // tr
        grid, blk, tblk = (nb,), (tr, cs), (1, tr, cs)
        smap, omap = (lambda i: (i, _my_chip())), (lambda i: (_my_c() * nb + i, 0))
        tmap = lambda j: (lambda i: (j, i, 0))
        out_shape = big.shard_shape
    elif big.kind == "row":
        rp, w = big.piece_shape[2:]
        tr = _pick(rp, 256, 128)
        grid, blk, tblk = (rp // tr,), (1, 1, tr, w), (1, 1, 1, tr, w)
        smap, omap = (lambda i: (_my_chip(), 0, i, 0)), (lambda i: (0, _my_c(), i, 0))
        tmap = lambda j: (lambda i: (j, 0, 0, i, 0))
        out_shape = (1, 2, rp, w)
    else:
        grid, blk, tblk = (1,), big.piece_shape, (1,) + big.piece_shape
        smap, omap = (lambda i: (0, _my_chip(), 0)), (lambda i: (_my_c(), 0, 0))
        tmap = lambda j: (lambda i: (j, 0, 0, 0))
        out_shape = big.shard_shape

    def body(s_ref, t0, t1, t2, o_ref):
        o_ref[...] = s_ref[...].astype(F32) + t0[0].astype(F32) + t1[0].astype(F32) + t2[0].astype(F32)

    return pl.pallas_call(
        body, name=name, grid=grid,
        in_specs=[pl.BlockSpec(blk, smap)] + [pl.BlockSpec(tblk, tmap(j)) for j in range(3)],
        out_specs=pl.BlockSpec(blk, omap), out_shape=jax.ShapeDtypeStruct(out_shape, F32), compiler_params=_cp(len(grid)),
    )(chip_sum, thirds, thirds, thirds)


def _split(name, bufs, remote, after=None):
    return _exchange_start(name + "_start", bufs, remote, after) + (remote, name)


def _join(handle, after):
    bufs, send_sems, recv_sems, _, remote, name = handle
    return _exchange_wait(name + "_wait", bufs, send_sems, recv_sems, remote, after)


def _allgather8_split(name, v, me):
    own = lax.dynamic_update_slice(lax.empty((8,) + v.shape, v.dtype), v[None], (me, 0, 0))
    remote = [(mask, lambda io, pos: io[0], lambda io, pos: io[1].at[_dev_index(pos)]) for mask in range(1, 8)]
    return _split(name, [v, own], remote)


def _reduce_d2d_remote(bigs):
    n = len(bigs)
    return [(1, lambda io, p, a=a, b=b: b.half_of_full(io[a], 1 - p[2]), lambda io, p, a=a: io[n + a])
            for a, b in enumerate(bigs)]


def _halves(bigs):
    return [jax.ShapeDtypeStruct(b.half_shape, BF16) for b in bigs]


def _chip_sums(tag, bigs, grads, from_sibling):
    return [_chip_sum(f"reduce_{tag}_chip_sum_{a}", b, g, r) for a, (b, g, r) in enumerate(zip(bigs, grads, from_sibling))]


def _reduce_to_chip(tag, bigs, grads):
    from_sibling = _exchange(f"reduce_{tag}_d2d", grads, _halves(bigs), _reduce_d2d_remote(bigs))
    return _chip_sums(tag, bigs, grads, from_sibling)


def _reduce_ici_remote(bigs):
    n = len(bigs)
    remote = []
    for a, b in enumerate(bigs):
        for j, mask in enumerate(CHIP_MASKS):
            remote.append((mask,
                           lambda io, p, a=a, b=b, mask=mask: b.piece_of_half(io[a], _chip_index(_flip(p, mask))),
                           lambda io, p, a=a, j=j: io[n + a].at[j]))
    return remote


def _thirds(bigs):
    return [jax.ShapeDtypeStruct((3,) + b.piece_shape, BF16) for b in bigs]


def _piece_sums(tag, bigs, chip_sum, from_chips):
    return [_piece_sum(f"reduce_{tag}_sum_{a}", b, s, r) for a, (b, s, r) in enumerate(zip(bigs, chip_sum, from_chips))]


def _share_remote(bigs, off):
    remote = []
    for a, b in enumerate(bigs):
        def mine(io, p, a=a, b=b):
            return b.half_of_shard(io[off + a], p[2])
        remote.append((1, mine, mine))
    return remote


def _reduce_finish(tag, bigs, chip_sum, from_chips):
    n = len(bigs)
    placed = _piece_sums(tag, bigs, chip_sum, from_chips)
    out = _exchange(f"reduce_{tag}_share_d2d", placed, [jax.ShapeDtypeStruct(p.shape, F32) for p in placed],
                    _share_remote(bigs, n), aliases={a: a for a in range(n)})
    return [o.reshape(b.shard_shape) for o, b in zip(out, bigs)]


def _mm(name, a, b, *, nt, tm, tn, tk, epi, extras=(), extra_specs=(), out_shape, out_specs, after=None, vmem_mib=48):
    m, kdim = a.shape
    n = b.shape[0] if nt else b.shape[1]
    gm, gn, gk = m // tm, n // tn, kdim // tk
    a_spec = pl.BlockSpec((tm, tk), lambda j, i, k: (i, k))
    b_spec = pl.BlockSpec((tn, tk), lambda j, i, k: (j, k)) if nt else pl.BlockSpec((tk, tn), lambda j, i, k: (k, j))
    n_ex = len(extras)
    if after is not None:
        extras, extra_specs = tuple(extras) + (after,), list(extra_specs) + [ANY]

    def body(a_ref, b_ref, *rest):
        ex, outs, acc = rest[:n_ex], rest[len(extras):-1], rest[-1]
        dot = _dot_nt if nt else _dot
        if gk == 1:
            acc[...] = dot(a_ref[...], b_ref[...])
            epi(acc, ex, outs)
        else:
            k = pl.program_id(2)

            @pl.when(k == 0)
            def _():
                acc[...] = dot(a_ref[...], b_ref[...])

            @pl.when(k > 0)
            def _():
                acc[...] += dot(a_ref[...], b_ref[...])

            @pl.when(k == gk - 1)
            def _():
                epi(acc, ex, outs)

    return pl.pallas_call(
        body, name=name, grid=(gn, gm, gk), in_specs=[a_spec, b_spec, *extra_specs], out_specs=tuple(out_specs),
        out_shape=tuple(out_shape), scratch_shapes=[pltpu.VMEM((tm, tn), F32)], compiler_params=_cp(3, vmem_mib),
    )(a, b, *extras)


def _mm_deferred(name, a, b, *, nt, tm, epi, tiles, vecs, out_tiles, n_stats, after, vmem_mib=48):
    pieces = a if isinstance(a, (list, tuple)) else [(a, 0)]
    m = pieces[0][0].shape[0]
    n = b.shape[0] if nt else b.shape[1]
    gm = m // tm
    n_a, n_t, n_v, n_o = len(pieces), len(tiles), len(vecs), len(out_tiles)

    def body(*refs):
        a_refs, b_ref, rest = refs[:n_a], refs[n_a], refs[n_a + 1:]
        t_refs, v_refs = rest[:n_t], rest[n_t:n_t + n_v]
        o_refs, st_ref, acc0, acc1 = rest[n_t + n_v + 1:n_t + n_v + 1 + n_o], rest[-3], rest[-2], rest[-1]
        i = pl.program_id(0)

        def dot():
            if n_a == 1 and pieces[0][0].shape[1] == b.shape[1 if nt else 0]:
                return (_dot_nt if nt else _dot)(a_refs[0][...], b_ref[...])
            parts = [_dot_nt(r[...], b_ref[:, off:off + p.shape[1]]) for r, (p, off) in zip(a_refs, pieces)]
            return functools.reduce(lambda u, v: u + v, parts)

        @pl.when(i == 0)
        def _():
            acc1[...] = jnp.zeros_like(acc1)
            st_ref[...] = jnp.zeros_like(st_ref)

        def finish(prev):
            for r0 in range(0, tm, STRIP):
                rs = slice(r0, r0 + STRIP)
                epi(prev[rs, :], rs, t_refs, v_refs, o_refs, st_ref, i > 0)

        @pl.when((i % 2 == 0) & (i < gm))
        def _():
            acc0[...] = dot()
            finish(acc1)

        @pl.when((i % 2 == 1) & (i < gm))
        def _():
            acc1[...] = dot()
            finish(acc0)

        @pl.when(i == gm)
        def _():
            finish(acc1 if gm % 2 == 0 else acc0)

    prev = lambda i: (jnp.maximum(i - 1, 0), 0)
    tile = pl.BlockSpec((tm, n), prev)
    return pl.pallas_call(
        body, name=name, grid=(gm + 1,),
        in_specs=[pl.BlockSpec((tm, p.shape[1]), lambda i: (jnp.minimum(i, gm - 1), 0)) for p, _ in pieces]
        + [pl.BlockSpec(b.shape, lambda i: (0, 0))] + [tile] * n_t + [_row_spec(n)] * n_v + [ANY],
        out_specs=tuple([tile] * n_o) + (_stat_spec(n_stats, n),),
        out_shape=tuple(out_tiles) + (jax.ShapeDtypeStruct((n_stats, 8, n), F32),),
        scratch_shapes=[pltpu.VMEM((tm, n), F32), pltpu.VMEM((tm, n), F32)], compiler_params=_cp(1, vmem_mib),
    )(*[p for p, _ in pieces], b, *tiles, *vecs, after)


def _mm_k_deferred(name, a, b, *, nt, tm, tk, epi, tiles, vecs, out_tiles, n_stats, after, vmem_mib=56):
    m, kdim = a.shape
    n = b.shape[0] if nt else b.shape[1]
    gm, gk = m // tm, kdim // tk
    rows = tm // gk
    n_t, n_v, n_o = len(tiles), len(vecs), len(out_tiles)
    dot = _dot_nt if nt else _dot

    def body(a_ref, b_ref, *rest):
        t_refs, v_refs = rest[:n_t], rest[n_t:n_t + n_v]
        o_refs, st_ref, acc0, acc1 = rest[n_t + n_v + 1:n_t + n_v + 1 + n_o], rest[-3], rest[-2], rest[-1]
        i, k = pl.program_id(0), pl.program_id(1)

        @pl.when((i == 0) & (k == 0))
        def _():
            acc1[...] = jnp.zeros_like(acc1)
            st_ref[...] = jnp.zeros_like(st_ref)

        def finish(prev):
            for r0 in range(0, rows, STRIP):
                acc_rows = prev[pl.ds(pl.multiple_of(k * rows + r0, STRIP), STRIP), :]
                epi(acc_rows, slice(r0, r0 + STRIP), t_refs, v_refs, o_refs, st_ref, i > 0)

        def step(cur, prev):
            cur[...] = jnp.where(k > 0, cur[...], 0.0) + dot(a_ref[...], b_ref[...])
            finish(prev)

        @pl.when((i % 2 == 0) & (i < gm))
        def _():
            step(acc0, acc1)

        @pl.when((i % 2 == 1) & (i < gm))
        def _():
            step(acc1, acc0)

        @pl.when(i == gm)
        def _():
            finish(acc1 if gm % 2 == 0 else acc0)

    prev = lambda i, k: (jnp.where(i == 0, 0, (i - 1) * gk + k), 0)
    part = pl.BlockSpec((rows, n), prev)
    b_spec = pl.BlockSpec((n, tk), lambda i, k: (0, k)) if nt else pl.BlockSpec((tk, n), lambda i, k: (k, 0))
    return pl.pallas_call(
        body, name=name, grid=(gm + 1, gk),
        in_specs=[pl.BlockSpec((tm, tk), lambda i, k: (jnp.minimum(i, gm - 1), k)), b_spec]
        + [part] * n_t + [_row_spec(n)] * n_v + [ANY],
        out_specs=tuple([part] * n_o) + (_stat_spec(n_stats, n),),
        out_shape=tuple(out_tiles) + (jax.ShapeDtypeStruct((n_stats, 8, n), F32),),
        scratch_shapes=[pltpu.VMEM((tm, n), F32), pltpu.VMEM((tm, n), F32)], compiler_params=_cp(2, vmem_mib),
    )(a, b, *tiles, *vecs, after)


def _mm_tn(name, a, b, out_dtype, *, tmo, tn, tt, more=(), after=None, vmem_mib=56):
    t, m = a.shape
    n = b.shape[1]
    gt = t // tt
    wait_for = [] if after is None else [after]

    def body(a_ref, b_ref, *rest):
        o_ref, acc = rest[-2:]
        k = pl.program_id(2)

        @pl.when(k == 0)
        def _():
            first = _dot_tn(a_ref[...], b_ref[...])
            acc[...] = first + _dot_tn(rest[0][...], rest[1][...]) if more else first

        @pl.when(k > 0)
        def _():
            acc[...] += _dot_tn(a_ref[...], b_ref[...])

        @pl.when(k == gt - 1)
        def _():
            o_ref[...] = acc[...].astype(o_ref.dtype)

    more_specs = [pl.BlockSpec((more[0].shape[0], tmo), lambda i, j, k: (0, i)),
                  pl.BlockSpec((more[1].shape[0], tn), lambda i, j, k: (0, j))] if more else []
    return pl.pallas_call(
        body, name=name, grid=(m // tmo, n // tn, gt),
        in_specs=[pl.BlockSpec((tt, tmo), lambda i, j, k: (k, i)), pl.BlockSpec((tt, tn), lambda i, j, k: (k, j))] + more_specs
        + [ANY] * len(wait_for),
        out_specs=pl.BlockSpec((tmo, tn), lambda i, j, k: (i, j)), out_shape=jax.ShapeDtypeStruct((m, n), out_dtype),
        scratch_shapes=[pltpu.VMEM((tmo, tn), F32)], compiler_params=_cp(3, vmem_mib),
    )(a, b, *more, *wait_for)


def _row_spec(d):
    return pl.BlockSpec((1, d), lambda *_: (0, 0))


def _stat_spec(k, d):
    return pl.BlockSpec((k, 8, d), lambda *_: (0, 0, 0))


def _rope(z, cs, sn):
    first = (lax.broadcasted_iota(jnp.int32, (z.shape[0], 128), 1) % 32) < 16
    outs = []
    for j in range(z.shape[1] // 128):
        zc = z[:, 128 * j:128 * (j + 1)]
        partner = jnp.where(first, pltpu.roll(zc, 112, 1), pltpu.roll(zc, 16, 1))
        outs.append(zc * cs + partner * sn)
    return outs[0] if len(outs) == 1 else jnp.concatenate(outs, axis=1)


def _rope_tables(length, rotate, zero=0.0):
    if not rotate:
        return jnp.ones((length, 128), F32), jnp.zeros((length, 128), F32)
    half = HEAD_DIM // 2
    t = jnp.arange(length)
    row = (t // GRID_W).astype(F32) + zero
    col = (t % GRID_W).astype(F32)
    e = jnp.arange(128) % HEAD_DIM
    inv_freq = ROPE_BASE ** (-(2 * ((e % half) % (half // 2))).astype(F32) / half)
    pos = jnp.where(e[None, :] < half, row[:, None], col[:, None])
    ang = pos * inv_freq[None, :]
    first = ((e % half) < half // 2)[None, :]
    return jnp.cos(ang), jnp.where(first, -jnp.sin(ang), jnp.sin(ang))


def _mixer_in(name, x, nw, sh, sc, w_in, cos, sin, after):
    t, d = x.shape
    tm = _pick(t, 256, 128)
    n_in = w_in.shape[1]

    def body(x_ref, nw_ref, sh_ref, sc_ref, w_ref, cos_ref, sin_ref, after_ref, h_ref, q_ref, k_ref, v_ref, u_ref):
        xf = x_ref[...]
        r = lax.rsqrt(jnp.mean(xf * xf, axis=-1, keepdims=True) + EPS)
        hb = (((xf * r) * nw_ref[...]) * (1.0 + sc_ref[...]) + sh_ref[...]).astype(BF16)
        h_ref[...] = hb
        p = _dot(hb, w_ref[...])
        cs, sn = cos_ref[...], sin_ref[...]
        q_ref[...] = (_rope(p[:, :ATTN_WIDTH], cs, sn) * SCALE).astype(BF16)
        k_ref[...] = _rope(p[:, ATTN_WIDTH:ATTN_WIDTH + KV_WIDTH], cs, sn).astype(BF16)
        v_ref[...] = p[:, ATTN_WIDTH + KV_WIDTH:ATTN_WIDTH + 2 * KV_WIDTH].astype(BF16)
        u_ref[...] = p[:, ATTN_WIDTH + 2 * KV_WIDTH:]

    def tile(w):
        return pl.BlockSpec((tm, w), lambda i: (i, 0))

    return pl.pallas_call(
        body, name=name, grid=(t // tm,),
        in_specs=[tile(d), _row_spec(d), _row_spec(d), _row_spec(d), pl.BlockSpec((d, n_in), lambda i: (0, 0)),
                  tile(128), tile(128), ANY],
        out_specs=(tile(d), tile(ATTN_WIDTH), tile(KV_WIDTH), tile(KV_WIDTH), tile(POOL_WIDTH)),
        out_shape=(jax.ShapeDtypeStruct((t, d), BF16), jax.ShapeDtypeStruct((t, ATTN_WIDTH), BF16),
                   jax.ShapeDtypeStruct((t, KV_WIDTH), BF16), jax.ShapeDtypeStruct((t, KV_WIDTH), BF16),
                   jax.ShapeDtypeStruct((t, POOL_WIDTH), F32)),
        compiler_params=_cp(1),
    )(x, nw, sh, sc, w_in, cos, sin, after)


def _attn_specs(nb, n_ctx):
    def blk(w, f):
        return pl.BlockSpec((BLOCK, w), lambda n: (f(n), 0))

    prev = lambda n: jnp.maximum(jnp.minimum(n, nb - 1) - 1, 0)
    cur = lambda n: jnp.minimum(n, nb - 1)
    nxt = lambda n: jnp.minimum(n + 1, nb - 1)
    kv = [blk(KV_WIDTH, prev), blk(KV_WIDTH, cur), blk(KV_WIDTH, nxt)]
    ctx = pl.BlockSpec((n_ctx, KV_WIDTH), lambda n: (0, 0))
    return [pl.BlockSpec(memory_space=pltpu.SMEM), blk(ATTN_WIDTH, cur)] + kv + kv + [ctx, ctx]


def _attn_mask(n, length, n_keys):
    row = lax.broadcasted_iota(jnp.int32, (GROUP * BLOCK, n_keys), 0) % BLOCK
    col = lax.broadcasted_iota(jnp.int32, (GROUP * BLOCK, n_keys), 1)
    kpos = (n - 1) * BLOCK + col
    return ((jnp.abs(col - BLOCK - row) <= BLOCK) & (kpos >= 0) & (kpos < length)) | (col >= 3 * BLOCK)


def _group_rows(block, g):
    return jnp.concatenate([block[:, HEAD_DIM * h:HEAD_DIM * (h + 1)] for h in range(GROUP * g, GROUP * (g + 1))], axis=0)


def _group_sink(sink_ref, g):
    head = lax.broadcasted_iota(jnp.int32, (GROUP * BLOCK, 1), 0) // BLOCK
    out = jnp.full((GROUP * BLOCK, 1), sink_ref[0, GROUP * g], F32)
    for j in range(1, GROUP):
        out = jnp.where(head == j, sink_ref[0, GROUP * g + j], out)
    return out


def _attn_fwd(q, k, v, kc, vc, sink):
    length = q.shape[0]
    nb = length // BLOCK
    n_ctx = kc.shape[0]
    n_keys = 3 * BLOCK + n_ctx

    def body(sink_ref, q_ref, kp, k0, kn, vp, v0, vn, kc_ref, vc_ref, o_ref, p_ref):
        n = pl.program_id(0)
        valid = _attn_mask(n, length, n_keys)
        qb = q_ref[...]
        kall = jnp.concatenate([kp[...], k0[...], kn[...], kc_ref[...]], axis=0)
        vall = jnp.concatenate([vp[...], v0[...], vn[...], vc_ref[...]], axis=0)
        outs = []
        for g in range(N_KV_HEADS):
            lanes = slice(HEAD_DIM * g, HEAD_DIM * (g + 1))
            s = jnp.where(valid, _dot_nt(_group_rows(qb, g), kall[:, lanes]), NEG_INF)
            sk = _group_sink(sink_ref, g)
            m = jnp.maximum(jnp.max(s, axis=-1, keepdims=True), sk)
            e = jnp.exp(s - m)
            e_sink = jnp.exp(sk - m)
            inv = 1.0 / (jnp.sum(e, axis=-1, keepdims=True) + e_sink)
            p_ref[0, g, :, :n_keys] = (e * inv).astype(BF16)
            p_ref[0, g, :, n_keys:] = jnp.broadcast_to(e_sink * inv, (GROUP * BLOCK, 128)).astype(BF16)
            o = _dot(p_ref[0, g, :, :n_keys], vall[:, lanes])
            outs += [o[BLOCK * j:BLOCK * (j + 1)] for j in range(GROUP)]
        o_ref[...] = jnp.concatenate(outs, axis=1).astype(BF16)

    return pl.pallas_call(
        body, name="attn_fwd", grid=(nb,), in_specs=_attn_specs(nb, n_ctx),
        out_specs=(pl.BlockSpec((BLOCK, ATTN_WIDTH), lambda n: (n, 0)),
                   pl.BlockSpec((1, N_KV_HEADS, GROUP * BLOCK, n_keys + 128), lambda n: (n, 0, 0, 0))),
        out_shape=(jax.ShapeDtypeStruct((length, ATTN_WIDTH + POOL_WIDTH), BF16),
                   jax.ShapeDtypeStruct((nb, N_KV_HEADS, GROUP * BLOCK, n_keys + 128), BF16)), compiler_params=_cp(1),
    )(sink, q, k, k, k, v, v, v, kc, vc)


def _attn_bwd(q, k, v, kc, vc, dmix, probs, cos, sin):
    length = q.shape[0]
    nb = length // BLOCK
    n_ctx = kc.shape[0]
    n_keys = 3 * BLOCK + n_ctx

    def body(q_ref, kp, k0, kn, vp, v0, vn, kc_ref, vc_ref, do_ref, p_ref, cos_ref, sin_ref, cos_prev, sin_prev,
             dq_ref, dkv_ref, dkc_ref, dvc_ref, dsink_ref, done, ahead):
        n = pl.program_id(0)

        @pl.when(n == 0)
        def _():
            dkc_ref[...] = jnp.zeros_like(dkc_ref)
            dvc_ref[...] = jnp.zeros_like(dvc_ref)
            dsink_ref[...] = jnp.zeros_like(dsink_ref)
            done[...] = jnp.zeros_like(done)
            ahead[...] = jnp.zeros_like(ahead)

        def write_block(dkv):
            dkv_ref[:, :KV_WIDTH] = _rope(dkv[:, :KV_WIDTH], cos_prev[...], -sin_prev[...]).astype(BF16)
            dkv_ref[:, KV_WIDTH:] = dkv[:, KV_WIDTH:].astype(BF16)

        def query_block():
            qb, dob = q_ref[...], do_ref[...]
            kall = jnp.concatenate([kp[...], k0[...], kn[...], kc_ref[...]], axis=0)
            vall = jnp.concatenate([vp[...], v0[...], vn[...], vc_ref[...]], axis=0)
            srow = lax.broadcasted_iota(jnp.int32, (8, 128), 0)
            slane = lax.broadcasted_iota(jnp.int32, (8, 128), 1)
            dqs, dks, dvs = [], [], []
            dsink = jnp.zeros((8, 128), F32)
            for g in range(N_KV_HEADS):
                lanes = slice(HEAD_DIM * g, HEAD_DIM * (g + 1))
                kg, vg = kall[:, lanes], vall[:, lanes]
                qg, dog = _group_rows(qb, g), _group_rows(dob, g)
                pb = p_ref[0, g, :, :n_keys]
                p = pb.astype(F32)
                dp = _dot_nt(dog, vg)
                delta = jnp.sum(p * dp, axis=-1, keepdims=True)
                ds = (p * (dp - delta)).astype(BF16)
                dq = _dot(ds, kg) * SCALE
                dqs += [dq[BLOCK * j:BLOCK * (j + 1)] for j in range(GROUP)]
                dks.append(_dot_tn(ds, qg))
                dvs.append(_dot_tn(pb, dog))
                d_sink = p_ref[0, g, :, n_keys:].astype(F32)[:, :1] * delta
                for j in range(GROUP):
                    total = -jnp.sum(d_sink[BLOCK * j:BLOCK * (j + 1)], axis=0, keepdims=True)
                    dsink = dsink + jnp.where((srow == 0) & (slane == GROUP * g + j), total, 0.0)
            dq_ref[...] = _rope(jnp.concatenate(dqs, axis=1), cos_ref[...], -sin_ref[...]).astype(BF16)
            dkv = jnp.concatenate(dks + dvs, axis=1)
            write_block(done[...] + dkv[:BLOCK])
            done[...] = ahead[...] + dkv[BLOCK:2 * BLOCK]
            ahead[...] = dkv[2 * BLOCK:3 * BLOCK]
            dkc_ref[...] += dkv[3 * BLOCK:, :KV_WIDTH]
            dvc_ref[...] += dkv[3 * BLOCK:, KV_WIDTH:]
            dsink_ref[...] += dsink

        pl.when(n < nb)(query_block)

        @pl.when(n == nb)
        def _():
            write_block(done[...])

    here = lambda n: (jnp.minimum(n, nb - 1), 0)
    before = lambda n: (jnp.maximum(n - 1, 0), 0)
    ctx = pl.BlockSpec((n_ctx, KV_WIDTH), lambda n: (0, 0))
    return pl.pallas_call(
        body, name="attn_bwd", grid=(nb + 1,),
        in_specs=_attn_specs(nb, n_ctx)[1:] + [pl.BlockSpec((BLOCK, ATTN_WIDTH), here),
                                           pl.BlockSpec((1,) + probs.shape[1:], lambda n: (jnp.minimum(n, nb - 1), 0, 0, 0)),
                                           pl.BlockSpec((BLOCK, 128), here), pl.BlockSpec((BLOCK, 128), here),
                                           pl.BlockSpec((BLOCK, 128), before), pl.BlockSpec((BLOCK, 128), before)],
        out_specs=(pl.BlockSpec((BLOCK, ATTN_WIDTH), here), pl.BlockSpec((BLOCK, 2 * KV_WIDTH), before), ctx, ctx,
                   pl.BlockSpec((8, 128), lambda n: (0, 0))),
        out_shape=(jax.ShapeDtypeStruct((length, ATTN_WIDTH), BF16), jax.ShapeDtypeStruct((length, 2 * KV_WIDTH), BF16),
                   jax.ShapeDtypeStruct((n_ctx, KV_WIDTH), F32), jax.ShapeDtypeStruct((n_ctx, KV_WIDTH), F32),
                   jax.ShapeDtypeStruct((8, 128), F32)),
        scratch_shapes=[pltpu.VMEM((BLOCK, 2 * KV_WIDTH), F32), pltpu.VMEM((BLOCK, 2 * KV_WIDTH), F32)],
        compiler_params=_cp(1),
    )(q, k, k, k, v, v, v, kc, vc, dmix, probs, cos, sin, cos, sin)


def _shift_rows(e, s):
    n = e.shape[0]
    return e if s % n == 0 else pltpu.roll(e, (-s) % n, 0)


def _window_sum(e, w, first):
    s, n = e, 1
    while n < w:
        s = s + _shift_rows(s, n)
        n *= 2
    return _shift_rows(s, first)


def _pool_geometry(i, tm, length):
    pos = i * tm - HALO + lax.broadcasted_iota(jnp.int32, (tm + 2 * HALO, 1), 0)
    inside = (pos >= 0) & (pos < length)
    inv_counts = []
    for w in POOL_WINDOWS:
        lo = jnp.clip(pos - w // 2, 0, length)
        hi = jnp.clip(pos - w // 2 + w, 0, length)
        inv_counts.append(1.0 / jnp.maximum(hi - lo, 1).astype(F32))
    return inside, inv_counts


def _halo_specs(tm, width, length, col=0):
    per = tm // HALO
    last = length // HALO - 1
    return [pl.BlockSpec((HALO, width), lambda i: (jnp.maximum(i * per - 1, 0), col)),
            pl.BlockSpec((tm, width), lambda i: (i, col)),
            pl.BlockSpec((HALO, width), lambda i: (jnp.minimum((i + 1) * per, last), col))]


def _pooled(ext, inv_counts, tm):
    outs = []
    for g, w in enumerate(POOL_WINDOWS):
        e = ext[:, POOL_GROUP_DIM * g:POOL_GROUP_DIM * (g + 1)]
        mean = _window_sum(e, w, -(w // 2)) * inv_counts[g]
        outs.append((mean - e)[HALO:HALO + tm])
    return outs


def _pool_fwd(u, pool_w, pool_scale, mix):
    length = u.shape[0]
    tm = _pick(length, 512, 256, 128)

    def body(up, u0, un, w_ref, sc_ref, mix_ref, o_ref):
        inside, inv_counts = _pool_geometry(pl.program_id(0), tm, length)
        ext = jnp.where(inside, jnp.concatenate([up[...], u0[...], un[...]], axis=0), 0.0)
        pooled = _pooled(ext, inv_counts, tm)
        mixed = [_dot(pooled[g].astype(BF16), w_ref[g]) for g in range(len(POOL_WINDOWS))]
        o_ref[...] = (jnp.concatenate(mixed, axis=1) * sc_ref[...]).astype(BF16)

    return pl.pallas_call(
        body, name="pool_fwd", grid=(length // tm,),
        in_specs=_halo_specs(tm, POOL_WIDTH, length) + [pl.BlockSpec(pool_w.shape, lambda i: (0, 0, 0)), _row_spec(POOL_WIDTH), ANY],
        out_specs=pl.BlockSpec((tm, POOL_WIDTH), lambda i: (i, 1)),
        out_shape=jax.ShapeDtypeStruct(mix.shape, BF16), input_output_aliases={5: 0}, compiler_params=_cp(1),
    )(u, u, u, pool_w, pool_scale, mix)


def _pool_bwd(u, dmix, pool_w, pool_scale, after):
    length = u.shape[0]
    tm = _pick(length, 512, 256, 128)
    n_g = len(POOL_WINDOWS)

    def body(up, u0, un, dp_, d0, dn_, w_ref, sc_ref, after_ref, du_ref, dw_ref, dsc_ref):
        i = pl.program_id(0)

        @pl.when(i == 0)
        def _():
            dw_ref[...] = jnp.zeros_like(dw_ref)
            dsc_ref[...] = jnp.zeros_like(dsc_ref)

        inside, inv_counts = _pool_geometry(i, tm, length)
        ext = jnp.where(inside, jnp.concatenate([up[...], u0[...], un[...]], axis=0), 0.0)
        dext = jnp.where(inside, jnp.concatenate([dp_[...], d0[...], dn_[...]], axis=0).astype(F32), 0.0)
        dmixed = (dext * sc_ref[...]).astype(BF16)
        pooled = _pooled(ext, inv_counts, tm)
        dus, dscs = [], []
        for g, w in enumerate(POOL_WINDOWS):
            lanes = slice(POOL_GROUP_DIM * g, POOL_GROUP_DIM * (g + 1))
            dpooled = _dot_nt(dmixed[:, lanes], w_ref[g])
            spread = _window_sum(dpooled * inv_counts[g], w, -(w // 2 - 1))
            dus.append((spread - dpooled)[HALO:HALO + tm])
            pb = pooled[g].astype(BF16)
            dw_ref[g] += _dot_tn(pb, dmixed[HALO:HALO + tm, lanes])
            prod = dext[HALO:HALO + tm, lanes] * _dot(pb, w_ref[g])
            dscs.append(_fold8(prod))
        du_ref[...] = jnp.concatenate(dus, axis=1).astype(BF16)
        dsc_ref[...] += jnp.concatenate(dscs, axis=1)

    return pl.pallas_call(
        body, name="pool_bwd", grid=(length // tm,),
        in_specs=_halo_specs(tm, POOL_WIDTH, length) + _halo_specs(tm, POOL_WIDTH, length, col=1)
        + [pl.BlockSpec(pool_w.shape, lambda i: (0, 0, 0)), _row_spec(POOL_WIDTH), ANY],
        out_specs=(pl.BlockSpec((tm, POOL_WIDTH), lambda i: (i, 0)), pl.BlockSpec((n_g, POOL_GROUP_DIM, POOL_GROUP_DIM), lambda i: (0, 0, 0)),
                   pl.BlockSpec((8, POOL_WIDTH), lambda i: (0, 0))),
        out_shape=(jax.ShapeDtypeStruct((length, POOL_WIDTH), BF16), jax.ShapeDtypeStruct((n_g, POOL_GROUP_DIM, POOL_GROUP_DIM), F32),
                   jax.ShapeDtypeStruct((8, POOL_WIDTH), F32)),
        compiler_params=_cp(1),
    )(u, u, u, dmix, dmix, dmix, pool_w, pool_scale, after)


def _mixer_out(mix, w_out, x, g_a, nmw, sh_m, sc_m, after):
    t, d = x.shape

    def epi(mo, rs, tiles, vecs, outs, st_ref, live):
        ga, nw, sh, sc = vecs
        x1_ref, mo_ref, hm_ref = outs
        x1 = tiles[0][rs, :] + ga[...] * mo
        x1_ref[rs, :] = x1
        mo_ref[rs, :] = mo.astype(BF16)
        r = lax.rsqrt(jnp.mean(x1 * x1, axis=-1, keepdims=True) + EPS)
        hm_ref[rs, :] = (((x1 * r) * nw[...]) * (1.0 + sc[...]) + sh[...]).astype(BF16)

    return _mm_deferred("mixer_out", mix, w_out, nt=False, tm=_pick(t, 256, 128), epi=epi, tiles=(x,), vecs=(g_a, nmw, sh_m, sc_m),
                        out_tiles=(jax.ShapeDtypeStruct((t, d), F32), jax.ShapeDtypeStruct((t, d), BF16), jax.ShapeDtypeStruct((t, d), BF16)),
                        n_stats=1, after=after)[:3]


def _mlp_up(hm, w_up):
    t, d = hm.shape
    tm = _pick(t, 1024, 512, 256, 128)
    tn = 2048

    def epi(acc, ex, outs):
        outs[0][...] = jnp.square(jnp.maximum(acc[...], 0.0)).astype(BF16)

    return _mm("mlp_up", hm, w_up, nt=False, tm=tm, tn=tn, tk=d, epi=epi,
               out_shape=(jax.ShapeDtypeStruct((t, w_up.shape[1]), BF16),),
               out_specs=(pl.BlockSpec((tm, tn), lambda j, i, k: (i, j)),))[0]


def _mlp_down_loss(act, w_down, x1, target, g_m, fw, after):
    t, d = x1.shape

    def epi(dnv, rs, tiles, vecs, outs, st_ref, live):
        x1_ref, t_ref = tiles
        gm, fw_ref = vecs
        dx2_ref, ddn_ref = outs
        x2 = x1_ref[rs, :] + gm[...] * dnv
        r = lax.rsqrt(jnp.mean(x2 * x2, axis=-1, keepdims=True) + EPS)
        xh = x2 * r
        diff = xh * fw_ref[...] - t_ref[rs, :]
        dy = diff * (1.0 / d)
        dxh = dy * fw_ref[...]
        dx2 = r * (dxh - xh * jnp.mean(dxh * xh, axis=-1, keepdims=True))
        dx2_ref[rs, :] = dx2
        ddn_ref[rs, :] = (dx2 * gm[...]).astype(BF16)
        st_ref[0] += jnp.where(live, _fold8(diff * diff), 0.0)
        st_ref[1] += jnp.where(live, _fold8(dy * xh), 0.0)
        st_ref[2] += jnp.where(live, _fold8(dx2 * dnv), 0.0)

    return _mm_k_deferred("mlp_down_loss", act, w_down, nt=False, tm=_pick(t, 512, 256), tk=_pick(act.shape[1], 2048), epi=epi,
                          tiles=(x1, target), vecs=(g_m, fw), n_stats=3, after=after,
                          out_tiles=(jax.ShapeDtypeStruct((t, d), F32), jax.ShapeDtypeStruct((t, d), BF16)))


def _mlp_dx(dup, w_up, x1, dx2, mo, nmw, sc_m, g_a, after):
    t, d = x1.shape

    def epi(dh, rs, tiles, vecs, outs, st_ref, live):
        x1_ref, dx2_ref, mo_ref = tiles
        nw, sc, ga = vecs
        dx1_ref, dmi_ref = outs
        dx1 = _norm_bwd_rows(dh, x1_ref[rs, :], nw[...], sc[...], st_ref) + dx2_ref[rs, :]
        dx1_ref[rs, :] = dx1
        dmi_ref[rs, :] = (dx1 * ga[...]).astype(BF16)
        st_ref[3] += jnp.where(live, _fold8(dx1 * mo_ref[rs, :].astype(F32)), 0.0)

    return _mm_k_deferred("mlp_dx", dup, w_up, nt=True, tm=_pick(t, 512, 256), tk=_pick(dup.shape[1], 2048), epi=epi,
                          tiles=(x1, dx2, mo), vecs=(nmw, sc_m, g_a), n_stats=4, after=after,
                          out_tiles=(jax.ShapeDtypeStruct((t, d), F32), jax.ShapeDtypeStruct((t, d), BF16)))


def _mlp_dact(ddn, w_down, act):
    t, d = ddn.shape
    tm = _pick(t, 1024, 512, 256, 128)
    tn = 2048

    def epi(acc, ex, outs):
        outs[0][...] = (acc[...] * (2.0 * jnp.sqrt(ex[0][...]).astype(F32))).astype(BF16)

    tile = pl.BlockSpec((tm, tn), lambda j, i, k: (i, j))
    return _mm("mlp_dact", ddn, w_down, nt=True, tm=tm, tn=tn, tk=d, epi=epi, extras=(act,), extra_specs=[tile],
               out_shape=(jax.ShapeDtypeStruct(act.shape, BF16),), out_specs=(tile,), vmem_mib=56)[0]


def _norm_bwd_rows(dh, xv, nw, sc, st_ref):
    r = lax.rsqrt(jnp.mean(xv * xv, axis=-1, keepdims=True) + EPS)
    xh = xv * r
    dy = dh * (1.0 + sc)
    st_ref[0] += _fold8(dh)
    st_ref[1] += _fold8(dh * (xh * nw))
    st_ref[2] += _fold8(dy * xh)
    dxh = dy * nw
    return r * (dxh - xh * jnp.mean(dxh * xh, axis=-1, keepdims=True))


def _mixer_dmix(dmi, w_out, after):
    t, d = dmi.shape
    tm = _pick(t, 1024, 512, 256, 128)

    def epi(acc, ex, outs):
        outs[0][...] = acc[...].astype(BF16)

    n = w_out.shape[0]
    return _mm("mixer_dmix", dmi, w_out, nt=True, tm=tm, tn=n, tk=d, epi=epi, after=after,
               out_shape=(jax.ShapeDtypeStruct((t, n), BF16),), out_specs=(pl.BlockSpec((tm, n), lambda j, i, k: (i, 0)),))[0]


def _mixer_dx(name, dp, w_in, x, dx1, naw, sc_a, after):
    t, d = x.shape

    def epi(dh, rs, tiles, vecs, outs, st_ref, live):
        x_ref, dx1_ref = tiles
        nw, sc = vecs
        outs[0][rs, :] = _norm_bwd_rows(dh, x_ref[rs, :], nw[...], sc[...], st_ref) + dx1_ref[rs, :]

    return _mm_deferred(name, dp, w_in, nt=True, tm=_pick(t, 256, 128), epi=epi, tiles=(x, dx1), vecs=(naw, sc_a),
                        out_tiles=(jax.ShapeDtypeStruct((t, d), F32),), n_stats=3, after=after, vmem_mib=56)


def _silu(v):
    return v / (1.0 + jnp.exp(-v))


def _ada_fwd(cond, w_ada, b_ada):
    d, n = w_ada.shape
    tn = 512

    def body(c_ref, w_ref, b_ref, o_ref):
        o_ref[...] = _dot(_silu(c_ref[...]).astype(BF16), w_ref[...].astype(BF16)) + b_ref[...]

    return pl.pallas_call(
        body, name="ada_fwd", grid=(n // tn,),
        in_specs=[pl.BlockSpec(cond.shape, lambda j: (0, 0)), pl.BlockSpec((d, tn), lambda j: (0, j)), pl.BlockSpec((1, tn), lambda j: (0, j))],
        out_specs=pl.BlockSpec((cond.shape[0], tn), lambda j: (0, j)), out_shape=jax.ShapeDtypeStruct((cond.shape[0], n), F32),
        compiler_params=_cp(1),
    )(cond, w_ada, b_ada)


def _adamw_math(w, g, m, v):
    m = ADAM_B1 * m + (1.0 - ADAM_B1) * g
    v = ADAM_B2 * v + (1.0 - ADAM_B2) * jnp.square(g)
    m_hat = m / (1.0 - ADAM_B1 ** ADAM_STEP)
    v_hat = v / (1.0 - ADAM_B2 ** ADAM_STEP)
    return -ADAM_LR * (m_hat / (jnp.sqrt(v_hat) + ADAM_EPS) + ADAM_WD * w), m, v


def _ada_bwd(cond, dm, w_ada, m_ada, v_ada):
    d, n = w_ada.shape
    tn = 256
    rows = cond.shape[0]

    def body(c_ref, dm_ref, w_ref, m_ref, v_ref, g_ref, dl_ref, nm_ref, nv_ref, pc_ref):
        @pl.when(pl.program_id(0) == 0)
        def _():
            pc_ref[...] = jnp.zeros_like(pc_ref)

        dmb = dm_ref[...].astype(BF16)
        w = w_ref[...]
        g = _dot_tn(_silu(c_ref[...]).astype(BF16), dmb)
        g_ref[...] = g
        dl_ref[...], nm_ref[...], nv_ref[...] = _adamw_math(w, g, m_ref[...], v_ref[...])
        pc_ref[...] += _dot_nt(dm_ref[8:16, :].astype(BF16), w.astype(BF16))

    tile = pl.BlockSpec((d, tn), lambda j: (0, j))
    like = jax.ShapeDtypeStruct((d, n), F32)
    return pl.pallas_call(
        body, name="ada_bwd", grid=(n // tn,),
        in_specs=[pl.BlockSpec((rows, d), lambda j: (0, 0)), pl.BlockSpec((rows, tn), lambda j: (0, j)), tile, tile, tile],
        out_specs=(tile, tile, tile, tile, pl.BlockSpec((8, d), lambda j: (0, 0))),
        out_shape=(like, like, like, like, jax.ShapeDtypeStruct((8, d), F32)), compiler_params=_cp(1),
    )(cond, dm, w_ada, m_ada, v_ada)


def _adamw(name, w, g, m, v, after=None):
    return _ew(name, lambda w_, g_, m_, v_: (g_,) + _adamw_math(w_, g_, m_, v_), [w, g, m, v], [F32, F32, F32, F32],
               g if after is None else after)


def _colsum(st):
    return jnp.sum(st, axis=1)


def kernel(x, c, ctx, c_ctx, norm_attn_w, norm_mlp_w, w_ada, b_ada, w_in, attn_sink, pool_w, pool_scale, w_out, w_mlp_up, w_mlp_down, final_norm_w, loss_target, m_c_ctx, m_norm_attn_w, m_norm_mlp_w, m_w_ada, m_b_ada, m_w_in, m_attn_sink, m_pool_w, m_pool_scale, m_w_out, m_w_mlp_up, m_w_mlp_down, m_final_norm_w, v_c_ctx, v_norm_attn_w, v_norm_mlp_w, v_w_ada, v_b_ada, v_w_in, v_attn_sink, v_pool_w, v_pool_scale, v_w_out, v_w_mlp_up, v_w_mlp_down, v_final_norm_w):
    length, d = x.shape[1], x.shape[2]
    n_ctx = ctx.shape[1]
    pos = (lax.axis_index("x"), lax.axis_index("y"), lax.axis_index("c"))
    me, chip = _dev_index(pos), _chip_index(pos)
    xs, tgt, cx = x.reshape(length, d), loss_target.reshape(length, d), ctx.reshape(n_ctx, d)
    n_ada = w_ada.shape[2]

    c_all = _allgather8("gather_c", jnp.pad(c, ((0, 7), (0, 0))))
    mixer_bigs = [_Big("col", w_in.shape[1:]), _Big("pool", pool_w.shape[1:]), _Big("row", w_out.shape[1:])]
    mlp_bigs = [_Big("col", w_mlp_up.shape[1:]), _Big("row", w_mlp_down.shape[1:])]
    placed = [_cast_place(f"place_{i}", b, s, c_all)[0] for i, (b, s) in enumerate(zip(mixer_bigs, [w_in[0], pool_w[0], w_out[0]]))]
    wo_bigs, win_bigs = mixer_bigs[1:], mixer_bigs[:1]
    win_flight = _split("gather_win_ici", placed[:1], _gather_ici_remote(win_bigs, 0))
    token, placed_mlp = win_flight[3], []
    for i, (b, s) in enumerate(zip(mlp_bigs, [w_mlp_up[0], w_mlp_down[0]])):
        p, token = _cast_place(f"place_mlp_{i}", b, s, token)
        placed_mlp.append(p)
    cos, sin = _rope_tables(length, True, token[0, 0])
    cond = jnp.concatenate([c_all[:, 0, :], jnp.pad(c_ctx[None, :], ((0, 7), (0, 0)))], axis=0) + 0.0 * cos[0, 0]
    b_shard = lax.dynamic_slice_in_dim(b_ada, chip * n_ada, n_ada, axis=1)
    mod_all = _allgather8("gather_mod", _ada_fwd(cond, w_ada[0], b_shard))
    mod = jnp.concatenate([mod_all[0], mod_all[2], mod_all[4], mod_all[6]], axis=1)
    mine = lax.dynamic_slice_in_dim(mod, me, 1, axis=0)
    sh_a, sc_a, g_a, sh_m, sc_m, g_m = [mine[:, d * i:d * (i + 1)] for i in range(6)]
    csh_a, csc_a = mod[8:9, :d], mod[8:9, d:2 * d]

    full = lambda bigs: [jax.ShapeDtypeStruct(b.full_shape, BF16) for b in bigs]
    wo_flight = _split("gather_wo_ici", placed[1:], _gather_ici_remote(wo_bigs, 0), after=mod)
    (win_b,) = _exchange("gather_win_d2d", _join(win_flight, mod), full(win_bigs), _gather_d2d_remote(win_bigs, 1), aliases={0: 0})
    flight = _split("gather_mlp_ici", placed_mlp, _gather_ici_remote(mlp_bigs, 0), after=wo_flight[3] + win_b[0:8, 0:128].astype(F32))

    one, zero = _rope_tables(n_ctx, False)
    h, q, k, v, u = _mixer_in("mixer_in", xs, norm_attn_w, sh_a, sc_a, win_b, cos, sin, flight[3])
    hc, _, kc, vc, _ = _mixer_in("mixer_in_ctx", cx, norm_attn_w, csh_a, csc_a, win_b, one, zero, flight[3])
    pw_b, wout_b = _exchange("gather_wo_d2d", _join(wo_flight, hc), full(wo_bigs), _gather_d2d_remote(wo_bigs, 2), aliases={0: 0, 1: 1})
    wout_b = wout_b.reshape(-1, d)
    attn, probs = _attn_fwd(q, k, v, kc, vc, attn_sink)
    mix = _pool_fwd(u, pw_b, pool_scale, attn)
    flight = _split("gather_mlp_d2d", _join(flight, mix), _gather_d2d_remote(mlp_bigs, 0))
    x1, mo, hm = _mixer_out(mix, wout_b, xs, g_a, norm_mlp_w, sh_m, sc_m, flight[3])
    wup_b, wdn_b = _join(flight, hm)
    wdn_b = wdn_b.reshape(-1, d)
    act = _mlp_up(hm, wup_b)
    dx2, ddn, st_loss = _mlp_down_loss(act, wdn_b, x1, tgt, g_m, final_norm_w[None, :], c)
    st_loss = _colsum(st_loss)

    tt = _pick(length, 2048, 1024, 512, 256, 128)
    g_wdn = _mm_tn("grad_w_down", act, ddn, BF16, tmo=1024, tn=d, tt=tt)
    dup = _mlp_dact(ddn, wdn_b, act)
    g_wup = _mm_tn("grad_w_up", hm, dup, BF16, tmo=d, tn=1024, tt=tt)
    empty = lambda shapes: [lax.empty(s.shape, s.dtype) for s in shapes]
    grads = [g_wup, g_wdn.reshape(mlp_bigs[1].full_shape)]
    flight = _split("reduce_mlp_d2d", grads + empty(_halves(mlp_bigs)), _reduce_d2d_remote(mlp_bigs))
    dx1, dmi, st_mlp = _mlp_dx(dup, wup_b, x1, dx2, mo, norm_mlp_w, sc_m, g_a, flight[3])
    st_mlp = _colsum(st_mlp)
    landed = _join(flight, dmi)
    mlp_chip = _chip_sums("mlp", mlp_bigs, landed[:2], landed[2:])
    flight = _split("reduce_mlp_ici", mlp_chip + empty(_thirds(mlp_bigs)), _reduce_ici_remote(mlp_bigs))
    g_wout = _mm_tn("grad_w_out", mix, dmi, BF16, tmo=1024, tn=d, tt=tt)
    dmix = _mixer_dmix(dmi, wout_b, flight[3])
    dq, dkv, dkc, dvc, dsink = _attn_bwd(q, k, v, kc, vc, dmix, probs, cos, sin)
    landed = _join(flight, dq)
    flight = _split("reduce_mlp_share", _piece_sums("mlp", mlp_bigs, landed[:2], landed[2:]), _share_remote(mlp_bigs, 0))
    du, g_pw, st_pool = _pool_bwd(u, dmix, pw_b, pool_scale, flight[3])
    g_mlp = _join(flight, du)

    wo_bigs, win_bigs = mixer_bigs[1:], mixer_bigs[:1]
    wo_chip = _reduce_to_chip("wo", wo_bigs, [g_pw.astype(BF16), g_wout.reshape(wo_bigs[1].full_shape)])
    flight = _split("reduce_wo_ici", wo_chip + empty(_thirds(wo_bigs)), _reduce_ici_remote(wo_bigs))
    dkv_ctx = jnp.concatenate([dkc.astype(BF16), dvc.astype(BF16)], axis=1)
    at_kv, at_u = ATTN_WIDTH, ATTN_WIDTH + 2 * KV_WIDTH
    tt_in = _pick(length, 1024, 512, 256, 128)
    g_win = jnp.concatenate([_mm_tn("grad_w_in_q", h, dq, BF16, tmo=d, tn=ATTN_WIDTH, tt=tt_in, after=flight[3]),
                             _mm_tn("grad_w_in_kv", h, dkv, BF16, tmo=d, tn=2 * KV_WIDTH, tt=tt_in, more=(hc, dkv_ctx)),
                             _mm_tn("grad_w_in_u", h, du, BF16, tmo=d, tn=POOL_WIDTH, tt=tt_in)], axis=1)
    wo_landed = _join(flight, g_win)
    win_chip = _reduce_to_chip("win", win_bigs, [g_win])
    flight = _split("reduce_win_ici", win_chip + empty(_thirds(win_bigs)), _reduce_ici_remote(win_bigs))
    grad_x, st_mix = _mixer_dx("mixer_dx", [(dq, 0), (dkv, at_kv), (du, at_u)], win_b, xs, dx1, norm_attn_w, sc_a, flight[3])
    _, st_ctx = _mixer_dx("mixer_dx_ctx", [(dkv_ctx, at_kv)], win_b, cx, jnp.zeros((n_ctx, d), F32), norm_attn_w, csc_a, flight[3])
    st_mix, st_ctx = _colsum(st_mix), _colsum(st_ctx)
    win_landed = _join(flight, grad_x)
    g_mixer = (_reduce_finish("win", win_bigs, win_landed[:1], win_landed[1:])
               + _reduce_finish("wo", wo_bigs, wo_landed[:2], wo_landed[2:]))

    zrow = jnp.zeros((d,), F32)
    pad = lambda a: jnp.pad(a, (0, d - a.shape[0]))
    mine_rows = [st_mix[0], st_mix[1], st_mlp[3], st_mlp[0], st_mlp[1], st_loss[2],
                 st_ctx[0], st_ctx[1],
                 st_mix[2] + st_ctx[2], st_mlp[2], st_loss[1],
                 pad(jnp.sum(st_pool, axis=0)), pad(dsink[0, :N_Q_HEADS]), st_loss[0]] + [zrow] * 2
    flight = _allgather8_split("gather_small", jnp.concatenate(mine_rows).reshape(len(mine_rows), d), me)
    res = {"w_mlp_up": tuple(_adamw("adamw_w_mlp_up", w_mlp_up, g_mlp[0].reshape(w_mlp_up.shape), m_w_mlp_up, v_w_mlp_up, flight[3]))}
    small_all = _join(flight, res["w_mlp_up"][1])[1]
    small = small_all[0]
    for i in range(1, 8):
        small = small + small_all[i]
    loss = 0.5 / d * jnp.sum(small[13])
    dm_rows = small_all[:, 0:6, :].reshape(8, 6 * d)
    dm_ctx = jnp.concatenate([small[6], small[7], jnp.zeros((4 * d,), F32)])[None, :]
    dm = jnp.concatenate([dm_rows, jnp.pad(dm_ctx, ((0, 7), (0, 0)))], axis=0)
    g_bada = jnp.sum(dm[:9], axis=0, keepdims=True)
    dm_shard = lax.dynamic_slice_in_dim(dm, chip * n_ada, n_ada, axis=1)
    g_wada, dl_wada, nm_wada, nv_wada, part_cctx = _ada_bwd(cond, dm_shard, w_ada[0], m_w_ada[0], v_w_ada[0])
    flight = _allgather8_split("gather_cctx", part_cctx, me)
    res["w_mlp_down"] = tuple(_adamw("adamw_w_mlp_down", w_mlp_down, g_mlp[1].reshape(w_mlp_down.shape), m_w_mlp_down,
                                     v_w_mlp_down, flight[3]))
    cctx_all = _join(flight, res["w_mlp_down"][1])[1]
    dsilu_in = cctx_all[0, 0] + cctx_all[2, 0] + cctx_all[4, 0] + cctx_all[6, 0]
    sig = 1.0 / (1.0 + jnp.exp(-c_ctx))
    g_cctx = dsilu_in * (sig * (1.0 + c_ctx * (1.0 - sig)))

    for nm, w_, g_, m_, v_ in zip(["w_in", "pool_w", "w_out"], [w_in, pool_w, w_out], g_mixer,
                                  [m_w_in, m_pool_w, m_w_out], [v_w_in, v_pool_w, v_w_out]):
        res[nm] = tuple(_adamw("adamw_" + nm, w_, g_.reshape(w_.shape), m_, v_))
    res["w_ada"] = (g_wada[None], dl_wada[None], nm_wada[None], nv_wada[None])

    def pack(cc, na, nm_, ba, sk, ps, fn):
        flat = [cc.reshape(-1), na.reshape(-1), nm_.reshape(-1), ba.reshape(-1), pad(sk.reshape(-1)), pad(ps.reshape(-1)),
                fn.reshape(-1), jnp.zeros((4 * d,), F32)]
        return jnp.concatenate(flat).reshape(16, d)

    w_s = pack(c_ctx, norm_attn_w, norm_mlp_w, b_ada, attn_sink, pool_scale, final_norm_w)
    m_s = pack(m_c_ctx, m_norm_attn_w, m_norm_mlp_w, m_b_ada, m_attn_sink, m_pool_scale, m_final_norm_w)
    v_s = pack(v_c_ctx, v_norm_attn_w, v_norm_mlp_w, v_b_ada, v_attn_sink, v_pool_scale, v_final_norm_w)
    g_s = pack(g_cctx, small[8], small[9], g_bada, small[12][:N_Q_HEADS], small[11][:POOL_WIDTH], small[10])
    small_out = _adamw("adamw_small", w_s, g_s, m_s, v_s)

    def unpack(p):
        return {"c_ctx": p[0], "norm_attn_w": p[1:2], "norm_mlp_w": p[2:3], "b_ada": p[3:9].reshape(1, 6 * d),
                "attn_sink": p[9:10, :N_Q_HEADS], "pool_scale": p[10:11, :POOL_WIDTH], "final_norm_w": p[11]}

    small_res = [unpack(p) for p in small_out]
    order = ["c_ctx", "norm_attn_w", "norm_mlp_w", "w_ada", "b_ada", "w_in", "attn_sink", "pool_w", "pool_scale",
             "w_out", "w_mlp_up", "w_mlp_down", "final_norm_w"]
    outs = [loss, grad_x.reshape(x.shape)]
    for kind in range(4):
        for nm in order:
            outs.append(res[nm][kind] if nm in res else small_res[kind][nm])
    return tuple(outs)
```
